```python
import math
import jax, jax.numpy as jnp
from jax import lax
import numpy as np

D_MODEL = 4096
BATCH = 8
SEQ = 4096
DEPTH = 1

D_MIX = D_MODEL
D_ATTN = D_MIX // 2
D_SSM = D_MIX - D_ATTN
HEAD_DIM = 64
N_Q_HEADS = D_ATTN // HEAD_DIM
N_KV_HEADS = max(1, N_Q_HEADS // 8)
Q_PER_KV = N_Q_HEADS // N_KV_HEADS
D_KV = N_KV_HEADS * HEAD_DIM
WINDOW = 128
BLOCK = WINDOW
ROPE_THETA = 10000.0
SSM_GROUP = 16
N_SSM_GROUPS = D_SSM // SSM_GROUP
STATE = 64
DT_MIN = 1e-3
DT_MAX = 1e-1
D_FF = 4 * D_MODEL
N_MOD = 6
EPS = 1e-6
D_IN = D_ATTN + 2 * D_KV + D_SSM

kernel_name = "hymba_s5_swa_sink_adaln_block"


def rmsnorm(x, g):
    xf = x.astype(jnp.float32)
    y = xf * lax.rsqrt(jnp.mean(xf * xf, axis=-1, keepdims=True) + EPS)
    return (y * g.astype(jnp.float32)).astype(x.dtype)


def rope(x):
    s = x.shape[1]
    half = x.shape[-1] // 2
    inv_freq = ROPE_THETA ** (-jnp.arange(half, dtype=jnp.float32) / half)
    ang = jnp.arange(s, dtype=jnp.float32)[:, None] * inv_freq[None, :]
    cos = jnp.cos(ang)[None, :, None, :]
    sin = jnp.sin(ang)[None, :, None, :]
    xf = x.astype(jnp.float32)
    x1, x2 = xf[..., :half], xf[..., half:]
    out = jnp.concatenate([x1 * cos - x2 * sin, x2 * cos + x1 * sin], axis=-1)
    return out.astype(x.dtype)


def sliding_window_attention(q, k, v, sinks):
    b, s = q.shape[0], q.shape[1]
    nb = s // BLOCK
    qb = q.reshape(b, nb, BLOCK, N_KV_HEADS, Q_PER_KV, HEAD_DIM).astype(jnp.float32)
    kb = k.reshape(b, nb, BLOCK, N_KV_HEADS, HEAD_DIM).astype(jnp.float32)
    vb = v.reshape(b, nb, BLOCK, N_KV_HEADS, HEAD_DIM).astype(jnp.float32)
    kk = jnp.concatenate([jnp.concatenate([jnp.zeros_like(kb[:, :1]), kb[:, :-1]], axis=1), kb], axis=2)
    vv = jnp.concatenate([jnp.concatenate([jnp.zeros_like(vb[:, :1]), vb[:, :-1]], axis=1), vb], axis=2)
    scores = jnp.einsum('bnqhgd,bnkhd->bnhgqk', qb, kk) * (HEAD_DIM ** -0.5)
    qi = jnp.arange(BLOCK)[:, None] + BLOCK
    kj = jnp.arange(2 * BLOCK)[None, :]
    rel = qi - kj
    band = (rel >= 0) & (rel < WINDOW)
    key_ok = (jnp.arange(nb)[:, None] > 0) | (jnp.arange(2 * BLOCK)[None, :] >= BLOCK)
    mask = band[None, :, :] & key_ok[:, None, :]
    scores = jnp.where(mask[None, :, None, None], scores, jnp.float32(-1e30))
    sink = sinks.astype(jnp.float32).reshape(N_KV_HEADS, Q_PER_KV)[None, None, :, :, None, None]
    m = jnp.maximum(jnp.max(scores, axis=-1, keepdims=True), sink)
    p = jnp.exp(scores - m)
    probs = p / (jnp.sum(p, axis=-1, keepdims=True) + jnp.exp(sink - m))
    out = jnp.einsum('bnhgqk,bnkhd->bnqhgd', probs, vv)
    return out.reshape(b, s, N_Q_HEADS * HEAD_DIM).astype(q.dtype)


def _scan_op(e1, e2):
    a1, b1 = e1
    a2, b2 = e2
    return a2 * a1, a2 * b1 + b2


def s5_mixer(u, lam_re, lam_im, log_step, b_re, b_im, c_re, c_im, d_skip, w_glu, b_glu):
    bsz, s = u.shape[0], u.shape[1]
    f32 = jnp.float32
    uf = u.astype(f32).reshape(bsz, s, N_SSM_GROUPS, SSM_GROUP)
    step = jnp.exp(log_step.astype(f32))[:, None]
    lam = lax.complex(lam_re.astype(f32), lam_im.astype(f32))
    lam_bar = jnp.exp(lam * step)
    coef = (lam_bar - 1.0) / lam
    b_bar = coef[..., None] * lax.complex(b_re.astype(f32), b_im.astype(f32))
    bu = jnp.einsum('bsgh,gph->sbgp', uf.astype(jnp.complex64), b_bar)
    a = jnp.broadcast_to(lam_bar[None, None], (s, 1, N_SSM_GROUPS, STATE))
    _, states = lax.associative_scan(_scan_op, (a, bu), axis=0)
    c_mat = lax.complex(c_re.astype(f32), c_im.astype(f32))
    y = jnp.real(jnp.einsum('sbgp,ghp->bsgh', states, c_mat))
    y = y + d_skip.astype(f32).reshape(N_SSM_GROUPS, SSM_GROUP) * uf
    y = jax.nn.gelu(y.reshape(bsz, s, D_SSM), approximate=False)
    out = y * jax.nn.sigmoid(y @ w_glu.astype(f32) + b_glu.astype(f32))
    return out.astype(u.dtype)


def _fwd_setup_inputs(seed: int = 0) -> dict:
    key = jax.random.key(seed)
    ks = jax.random.split(key, 24)
    f32 = jnp.float32
    nrm = lambda k, shape, sc: jax.random.normal(k, shape, f32) * sc
    inputs = {
        "x": nrm(ks[0], (BATCH, SEQ, D_MODEL), 1.0),
        "c": nrm(ks[1], (BATCH, D_MODEL), 1.0),
        "w_ada": nrm(ks[2], (DEPTH, D_MODEL, N_MOD * D_MODEL), 0.5 * D_MODEL ** -0.5),
        "b_ada": nrm(ks[3], (DEPTH, N_MOD * D_MODEL), 0.01),
        "norm1_g": 1.0 + nrm(ks[4], (DEPTH, D_MODEL), 0.02),
        "w_in": nrm(ks[5], (DEPTH, D_MODEL, D_IN), D_MODEL ** -0.5),
        "sinks": nrm(ks[6], (DEPTH, N_Q_HEADS), 0.5),
        "ssm_lam_re": -0.5 + nrm(ks[7], (DEPTH, N_SSM_GROUPS, STATE), 0.01),
        "ssm_lam_im": jnp.pi * jnp.arange(STATE, dtype=f32)[None, None, :] + nrm(ks[8], (DEPTH, N_SSM_GROUPS, STATE), 0.01),
        "ssm_log_step": jax.random.uniform(ks[9], (DEPTH, N_SSM_GROUPS), f32, math.log(DT_MIN), math.log(DT_MAX)),
        "ssm_b_re": nrm(ks[10], (DEPTH, N_SSM_GROUPS, STATE, SSM_GROUP), (2 * SSM_GROUP) ** -0.5),
        "ssm_b_im": nrm(ks[11], (DEPTH, N_SSM_GROUPS, STATE, SSM_GROUP), (2 * SSM_GROUP) ** -0.5),
        "ssm_c_re": nrm(ks[12], (DEPTH, N_SSM_GROUPS, SSM_GROUP, STATE), (2 * STATE) ** -0.5),
        "ssm_c_im": nrm(ks[13], (DEPTH, N_SSM_GROUPS, SSM_GROUP, STATE), (2 * STATE) ** -0.5),
        "ssm_d": nrm(ks[14], (DEPTH, D_SSM), 1.0),
        "w_glu": nrm(ks[15], (DEPTH, D_SSM, D_SSM), D_SSM ** -0.5),
        "b_glu": nrm(ks[16], (DEPTH, D_SSM), 0.01),
        "attn_out_g": 1.0 + nrm(ks[17], (DEPTH, D_ATTN), 0.02),
        "ssm_out_g": 1.0 + nrm(ks[18], (DEPTH, D_SSM), 0.02),
        "w_out": nrm(ks[19], (DEPTH, D_MIX, D_MODEL), D_MIX ** -0.5),
        "norm2_g": 1.0 + nrm(ks[20], (DEPTH, D_MODEL), 0.02),
        "w_ff1": nrm(ks[21], (DEPTH, D_MODEL, D_FF), D_MODEL ** -0.5),
        "w_ff2": nrm(ks[22], (DEPTH, D_FF, D_MODEL), D_FF ** -0.5),
        "final_g": 1.0 + nrm(ks[23], (D_MODEL,), 0.02),
    }
    return inputs


def _fwd_reference(x, c, w_ada, b_ada, norm1_g, w_in, sinks, ssm_lam_re, ssm_lam_im, ssm_log_step,
              ssm_b_re, ssm_b_im, ssm_c_re, ssm_c_im, ssm_d, w_glu, b_glu, attn_out_g, ssm_out_g,
              w_out, norm2_g, w_ff1, w_ff2, final_g):
    bsz, s, _ = x.shape
    c_act = jax.nn.silu(c.astype(jnp.float32))
    for l in range(DEPTH):
        mod = (c_act @ w_ada[l].astype(jnp.float32) + b_ada[l].astype(jnp.float32)).astype(x.dtype)
        shift1, scale1, gate1, shift2, scale2, gate2 = [m[:, None, :] for m in jnp.split(mod, N_MOD, axis=-1)]

        h = rmsnorm(x, norm1_g[l]) * (1.0 + scale1) + shift1
        proj = h @ w_in[l]
        q = proj[..., :D_ATTN].reshape(bsz, s, N_Q_HEADS, HEAD_DIM)
        k = proj[..., D_ATTN:D_ATTN + D_KV].reshape(bsz, s, N_KV_HEADS, HEAD_DIM)
        v = proj[..., D_ATTN + D_KV:D_ATTN + 2 * D_KV].reshape(bsz, s, N_KV_HEADS, HEAD_DIM)
        u = proj[..., D_ATTN + 2 * D_KV:]
        attn = sliding_window_attention(rope(q), rope(k), v, sinks[l])
        ssm = s5_mixer(u, ssm_lam_re[l], ssm_lam_im[l], ssm_log_step[l], ssm_b_re[l], ssm_b_im[l],
                       ssm_c_re[l], ssm_c_im[l], ssm_d[l], w_glu[l], b_glu[l])
        mixed = jnp.concatenate([rmsnorm(attn, attn_out_g[l]), rmsnorm(ssm, ssm_out_g[l])], axis=-1)
        x = x + gate1 * (mixed @ w_out[l])

        h2 = rmsnorm(x, norm2_g[l]) * (1.0 + scale2) + shift2
        ff = jnp.square(jax.nn.relu(h2 @ w_ff1[l])) @ w_ff2[l]
        x = x + gate2 * ff
    return rmsnorm(x, final_g)


import jax as _jax
import jax.numpy as _jnp

TWIN_FORMAT = 'train_step'
FWD_PARAMS = ['x', 'c', 'w_ada', 'b_ada', 'norm1_g', 'w_in', 'sinks', 'ssm_lam_re', 'ssm_lam_im', 'ssm_log_step', 'ssm_b_re', 'ssm_b_im', 'ssm_c_re', 'ssm_c_im', 'ssm_d', 'w_glu', 'b_glu', 'attn_out_g', 'ssm_out_g', 'w_out', 'norm2_g', 'w_ff1', 'w_ff2', 'final_g']
TWIN_WEIGHTS = ['w_ada', 'b_ada', 'norm1_g', 'w_in', 'sinks', 'ssm_lam_re', 'ssm_lam_im', 'ssm_log_step', 'ssm_b_re', 'ssm_b_im', 'ssm_c_re', 'ssm_c_im', 'ssm_d', 'w_glu', 'b_glu', 'attn_out_g', 'ssm_out_g', 'w_out', 'norm2_g', 'w_ff1', 'w_ff2', 'final_g']
TWIN_DIFF_INPUT = 'x'
TWIN_INPUTS = ['x', 'c', 'w_ada', 'b_ada', 'norm1_g', 'w_in', 'sinks', 'ssm_lam_re', 'ssm_lam_im', 'ssm_log_step', 'ssm_b_re', 'ssm_b_im', 'ssm_c_re', 'ssm_c_im', 'ssm_d', 'w_glu', 'b_glu', 'attn_out_g', 'ssm_out_g', 'w_out', 'norm2_g', 'w_ff1', 'w_ff2', 'final_g', 'loss_target', 'm_w_ada', 'm_b_ada', 'm_norm1_g', 'm_w_in', 'm_sinks', 'm_ssm_lam_re', 'm_ssm_lam_im', 'm_ssm_log_step', 'm_ssm_b_re', 'm_ssm_b_im', 'm_ssm_c_re', 'm_ssm_c_im', 'm_ssm_d', 'm_w_glu', 'm_b_glu', 'm_attn_out_g', 'm_ssm_out_g', 'm_w_out', 'm_norm2_g', 'm_w_ff1', 'm_w_ff2', 'm_final_g', 'v_w_ada', 'v_b_ada', 'v_norm1_g', 'v_w_in', 'v_sinks', 'v_ssm_lam_re', 'v_ssm_lam_im', 'v_ssm_log_step', 'v_ssm_b_re', 'v_ssm_b_im', 'v_ssm_c_re', 'v_ssm_c_im', 'v_ssm_d', 'v_w_glu', 'v_b_glu', 'v_attn_out_g', 'v_ssm_out_g', 'v_w_out', 'v_norm2_g', 'v_w_ff1', 'v_w_ff2', 'v_final_g']
TWIN_OUTPUTS = ['loss', 'grad_x', 'grad_w_ada', 'grad_b_ada', 'grad_norm1_g', 'grad_w_in', 'grad_sinks', 'grad_ssm_lam_re', 'grad_ssm_lam_im', 'grad_ssm_log_step', 'grad_ssm_b_re', 'grad_ssm_b_im', 'grad_ssm_c_re', 'grad_ssm_c_im', 'grad_ssm_d', 'grad_w_glu', 'grad_b_glu', 'grad_attn_out_g', 'grad_ssm_out_g', 'grad_w_out', 'grad_norm2_g', 'grad_w_ff1', 'grad_w_ff2', 'grad_final_g', 'delta_w_ada', 'delta_b_ada', 'delta_norm1_g', 'delta_w_in', 'delta_sinks', 'delta_ssm_lam_re', 'delta_ssm_lam_im', 'delta_ssm_log_step', 'delta_ssm_b_re', 'delta_ssm_b_im', 'delta_ssm_c_re', 'delta_ssm_c_im', 'delta_ssm_d', 'delta_w_glu', 'delta_b_glu', 'delta_attn_out_g', 'delta_ssm_out_g', 'delta_w_out', 'delta_norm2_g', 'delta_w_ff1', 'delta_w_ff2', 'delta_final_g', 'new_m_w_ada', 'new_m_b_ada', 'new_m_norm1_g', 'new_m_w_in', 'new_m_sinks', 'new_m_ssm_lam_re', 'new_m_ssm_lam_im', 'new_m_ssm_log_step', 'new_m_ssm_b_re', 'new_m_ssm_b_im', 'new_m_ssm_c_re', 'new_m_ssm_c_im', 'new_m_ssm_d', 'new_m_w_glu', 'new_m_b_glu', 'new_m_attn_out_g', 'new_m_ssm_out_g', 'new_m_w_out', 'new_m_norm2_g', 'new_m_w_ff1', 'new_m_w_ff2', 'new_m_final_g', 'new_v_w_ada', 'new_v_b_ada', 'new_v_norm1_g', 'new_v_w_in', 'new_v_sinks', 'new_v_ssm_lam_re', 'new_v_ssm_lam_im', 'new_v_ssm_log_step', 'new_v_ssm_b_re', 'new_v_ssm_b_im', 'new_v_ssm_c_re', 'new_v_ssm_c_im', 'new_v_ssm_d', 'new_v_w_glu', 'new_v_b_glu', 'new_v_attn_out_g', 'new_v_ssm_out_g', 'new_v_w_out', 'new_v_norm2_g', 'new_v_w_ff1', 'new_v_w_ff2', 'new_v_final_g']
TWIN_LEAF_KINDS = {'loss': 'loss', 'grad_x': 'grad_x', 'grad_w_ada': 'grad_w', 'grad_b_ada': 'grad_w', 'grad_norm1_g': 'grad_w', 'grad_w_in': 'grad_w', 'grad_sinks': 'grad_w', 'grad_ssm_lam_re': 'grad_w', 'grad_ssm_lam_im': 'grad_w', 'grad_ssm_log_step': 'grad_w', 'grad_ssm_b_re': 'grad_w', 'grad_ssm_b_im': 'grad_w', 'grad_ssm_c_re': 'grad_w', 'grad_ssm_c_im': 'grad_w', 'grad_ssm_d': 'grad_w', 'grad_w_glu': 'grad_w', 'grad_b_glu': 'grad_w', 'grad_attn_out_g': 'grad_w', 'grad_ssm_out_g': 'grad_w', 'grad_w_out': 'grad_w', 'grad_norm2_g': 'grad_w', 'grad_w_ff1': 'grad_w', 'grad_w_ff2': 'grad_w', 'grad_final_g': 'grad_w', 'delta_w_ada': 'delta_w', 'delta_b_ada': 'delta_w', 'delta_norm1_g': 'delta_w', 'delta_w_in': 'delta_w', 'delta_sinks': 'delta_w', 'delta_ssm_lam_re': 'delta_w', 'delta_ssm_lam_im': 'delta_w', 'delta_ssm_log_step': 'delta_w', 'delta_ssm_b_re': 'delta_w', 'delta_ssm_b_im': 'delta_w', 'delta_ssm_c_re': 'delta_w', 'delta_ssm_c_im': 'delta_w', 'delta_ssm_d': 'delta_w', 'delta_w_glu': 'delta_w', 'delta_b_glu': 'delta_w', 'delta_attn_out_g': 'delta_w', 'delta_ssm_out_g': 'delta_w', 'delta_w_out': 'delta_w', 'delta_norm2_g': 'delta_w', 'delta_w_ff1': 'delta_w', 'delta_w_ff2': 'delta_w', 'delta_final_g': 'delta_w', 'new_m_w_ada': 'new_m', 'new_m_b_ada': 'new_m', 'new_m_norm1_g': 'new_m', 'new_m_w_in': 'new_m', 'new_m_sinks': 'new_m', 'new_m_ssm_lam_re': 'new_m', 'new_m_ssm_lam_im': 'new_m', 'new_m_ssm_log_step': 'new_m', 'new_m_ssm_b_re': 'new_m', 'new_m_ssm_b_im': 'new_m', 'new_m_ssm_c_re': 'new_m', 'new_m_ssm_c_im': 'new_m', 'new_m_ssm_d': 'new_m', 'new_m_w_glu': 'new_m', 'new_m_b_glu': 'new_m', 'new_m_attn_out_g': 'new_m', 'new_m_ssm_out_g': 'new_m', 'new_m_w_out': 'new_m', 'new_m_norm2_g': 'new_m', 'new_m_w_ff1': 'new_m', 'new_m_w_ff2': 'new_m', 'new_m_final_g': 'new_m', 'new_v_w_ada': 'new_v', 'new_v_b_ada': 'new_v', 'new_v_norm1_g': 'new_v', 'new_v_w_in': 'new_v', 'new_v_sinks': 'new_v', 'new_v_ssm_lam_re': 'new_v', 'new_v_ssm_lam_im': 'new_v', 'new_v_ssm_log_step': 'new_v', 'new_v_ssm_b_re': 'new_v', 'new_v_ssm_b_im': 'new_v', 'new_v_ssm_c_re': 'new_v', 'new_v_ssm_c_im': 'new_v', 'new_v_ssm_d': 'new_v', 'new_v_w_glu': 'new_v', 'new_v_b_glu': 'new_v', 'new_v_attn_out_g': 'new_v', 'new_v_ssm_out_g': 'new_v', 'new_v_w_out': 'new_v', 'new_v_norm2_g': 'new_v', 'new_v_w_ff1': 'new_v', 'new_v_w_ff2': 'new_v', 'new_v_final_g': 'new_v'}


def _forward(args):
    return _fwd_reference(*[args[k] for k in FWD_PARAMS])


def _output_shape():
    out = _jax.eval_shape(lambda: _forward(_fwd_setup_inputs(0)))
    return out.shape, out.dtype

N_MICROBATCH = 1
ADAM_LR = 0.001
ADAM_B1 = 0.9
ADAM_B2 = 0.999
ADAM_EPS = 1e-08
ADAM_WD = 0.01
ADAM_STEP = 10
PER_EXAMPLE_BATCH_AXIS = {'x': 0, 'c': 0, 'loss_target': 0}
SHARED_INPUTS = []
_WEIGHT_DTYPES = {'w_ada': _jnp.float32, 'b_ada': _jnp.float32, 'norm1_g': _jnp.float32, 'w_in': _jnp.float32, 'sinks': _jnp.float32, 'ssm_lam_re': _jnp.float32, 'ssm_lam_im': _jnp.float32, 'ssm_log_step': _jnp.float32, 'ssm_b_re': _jnp.float32, 'ssm_b_im': _jnp.float32, 'ssm_c_re': _jnp.float32, 'ssm_c_im': _jnp.float32, 'ssm_d': _jnp.float32, 'w_glu': _jnp.float32, 'b_glu': _jnp.float32, 'attn_out_g': _jnp.float32, 'ssm_out_g': _jnp.float32, 'w_out': _jnp.float32, 'norm2_g': _jnp.float32, 'w_ff1': _jnp.float32, 'w_ff2': _jnp.float32, 'final_g': _jnp.float32}
MOMENT_SCALE = {'w_ada': 2.483135e-02, 'b_ada': 4.244512e-02, 'norm1_g': 1.350140e-02, 'w_in': 1.518265e-02, 'sinks': 3.859104e-03, 'ssm_lam_re': 1.038501e-03, 'ssm_lam_im': 1.266624e-03, 'ssm_log_step': 4.960469e-01, 'ssm_b_re': 5.695928e-04, 'ssm_b_im': 5.950032e-04, 'ssm_c_re': 1.126473e-03, 'ssm_c_im': 1.180704e-03, 'ssm_d': 1.605474e-02, 'w_glu': 4.444541e-03, 'b_glu': 7.399001e-03, 'attn_out_g': 1.471503e-02, 'ssm_out_g': 1.665947e-02, 'w_out': 1.464511e-02, 'norm2_g': 1.885217e-02, 'w_ff1': 9.905668e-03, 'w_ff2': 1.842123e-02, 'final_g': 8.060115e+00}


def _to_microbatches(a, axis):
    t = _jnp.moveaxis(a, axis, 0)
    t = t.reshape((N_MICROBATCH, t.shape[0] // N_MICROBATCH) + t.shape[1:])
    return _jnp.moveaxis(t, 1, axis + 1)


def setup_inputs(seed: int = 0) -> dict:
    inp = _fwd_setup_inputs(seed)
    key = _jax.random.fold_in(_jax.random.key(seed), 7919)
    shape, _ = _output_shape()
    out = dict(inp)
    out["loss_target"] = _jax.random.normal(_jax.random.fold_in(key, 0), shape, _jnp.float32)
    for i, name in enumerate(TWIN_WEIGHTS):
        w = inp[name].astype(_jnp.float32)
        if MOMENT_SCALE is None:
            s = _jnp.sqrt(_jnp.mean(_jnp.square(w)) + 1e-30)
        else:
            s = MOMENT_SCALE[name]
        km, kv = _jax.random.split(_jax.random.fold_in(key, i + 1))
        out[name] = w
        out["m_" + name] = s * _jax.random.normal(km, w.shape, _jnp.float32)
        out["v_" + name] = (s * s) * _jax.random.uniform(kv, w.shape, _jnp.float32, 0.5, 1.5)
    if N_MICROBATCH > 1:
        for name, axis in PER_EXAMPLE_BATCH_AXIS.items():
            out[name] = _to_microbatches(out[name], axis)
    return {'x': out['x'], 'c': out['c'], 'w_ada': out['w_ada'], 'b_ada': out['b_ada'], 'norm1_g': out['norm1_g'], 'w_in': out['w_in'], 'sinks': out['sinks'], 'ssm_lam_re': out['ssm_lam_re'], 'ssm_lam_im': out['ssm_lam_im'], 'ssm_log_step': out['ssm_log_step'], 'ssm_b_re': out['ssm_b_re'], 'ssm_b_im': out['ssm_b_im'], 'ssm_c_re': out['ssm_c_re'], 'ssm_c_im': out['ssm_c_im'], 'ssm_d': out['ssm_d'], 'w_glu': out['w_glu'], 'b_glu': out['b_glu'], 'attn_out_g': out['attn_out_g'], 'ssm_out_g': out['ssm_out_g'], 'w_out': out['w_out'], 'norm2_g': out['norm2_g'], 'w_ff1': out['w_ff1'], 'w_ff2': out['w_ff2'], 'final_g': out['final_g'], 'loss_target': out['loss_target'], 'm_w_ada': out['m_w_ada'], 'm_b_ada': out['m_b_ada'], 'm_norm1_g': out['m_norm1_g'], 'm_w_in': out['m_w_in'], 'm_sinks': out['m_sinks'], 'm_ssm_lam_re': out['m_ssm_lam_re'], 'm_ssm_lam_im': out['m_ssm_lam_im'], 'm_ssm_log_step': out['m_ssm_log_step'], 'm_ssm_b_re': out['m_ssm_b_re'], 'm_ssm_b_im': out['m_ssm_b_im'], 'm_ssm_c_re': out['m_ssm_c_re'], 'm_ssm_c_im': out['m_ssm_c_im'], 'm_ssm_d': out['m_ssm_d'], 'm_w_glu': out['m_w_glu'], 'm_b_glu': out['m_b_glu'], 'm_attn_out_g': out['m_attn_out_g'], 'm_ssm_out_g': out['m_ssm_out_g'], 'm_w_out': out['m_w_out'], 'm_norm2_g': out['m_norm2_g'], 'm_w_ff1': out['m_w_ff1'], 'm_w_ff2': out['m_w_ff2'], 'm_final_g': out['m_final_g'], 'v_w_ada': out['v_w_ada'], 'v_b_ada': out['v_b_ada'], 'v_norm1_g': out['v_norm1_g'], 'v_w_in': out['v_w_in'], 'v_sinks': out['v_sinks'], 'v_ssm_lam_re': out['v_ssm_lam_re'], 'v_ssm_lam_im': out['v_ssm_lam_im'], 'v_ssm_log_step': out['v_ssm_log_step'], 'v_ssm_b_re': out['v_ssm_b_re'], 'v_ssm_b_im': out['v_ssm_b_im'], 'v_ssm_c_re': out['v_ssm_c_re'], 'v_ssm_c_im': out['v_ssm_c_im'], 'v_ssm_d': out['v_ssm_d'], 'v_w_glu': out['v_w_glu'], 'v_b_glu': out['v_b_glu'], 'v_attn_out_g': out['v_attn_out_g'], 'v_ssm_out_g': out['v_ssm_out_g'], 'v_w_out': out['v_w_out'], 'v_norm2_g': out['v_norm2_g'], 'v_w_ff1': out['v_w_ff1'], 'v_w_ff2': out['v_w_ff2'], 'v_final_g': out['v_final_g']}


def _loss(weights, diff, rest, loss_target):
    with _jax.named_scope("forward"):
        args = {**rest, TWIN_DIFF_INPUT: diff, **{k: w.astype(_WEIGHT_DTYPES[k]) for k, w in weights.items()}}
        y = _forward(args)
    with _jax.named_scope("loss_head"):
        err = _jnp.square(y.astype(_jnp.float32) - loss_target)
        return 0.5 * _jnp.sum(_jnp.mean(err, axis=-1)) if err.ndim else 0.5 * err


def _adamw(w, g, m, v):
    m = ADAM_B1 * m + (1.0 - ADAM_B1) * g
    v = ADAM_B2 * v + (1.0 - ADAM_B2) * _jnp.square(g)
    m_hat = m / (1.0 - ADAM_B1 ** ADAM_STEP)
    v_hat = v / (1.0 - ADAM_B2 ** ADAM_STEP)
    delta = -ADAM_LR * (m_hat / (_jnp.sqrt(v_hat) + ADAM_EPS) + ADAM_WD * w)
    return delta, m, v


def reference(x, c, w_ada, b_ada, norm1_g, w_in, sinks, ssm_lam_re, ssm_lam_im, ssm_log_step, ssm_b_re, ssm_b_im, ssm_c_re, ssm_c_im, ssm_d, w_glu, b_glu, attn_out_g, ssm_out_g, w_out, norm2_g, w_ff1, w_ff2, final_g, loss_target, m_w_ada, m_b_ada, m_norm1_g, m_w_in, m_sinks, m_ssm_lam_re, m_ssm_lam_im, m_ssm_log_step, m_ssm_b_re, m_ssm_b_im, m_ssm_c_re, m_ssm_c_im, m_ssm_d, m_w_glu, m_b_glu, m_attn_out_g, m_ssm_out_g, m_w_out, m_norm2_g, m_w_ff1, m_w_ff2, m_final_g, v_w_ada, v_b_ada, v_norm1_g, v_w_in, v_sinks, v_ssm_lam_re, v_ssm_lam_im, v_ssm_log_step, v_ssm_b_re, v_ssm_b_im, v_ssm_c_re, v_ssm_c_im, v_ssm_d, v_w_glu, v_b_glu, v_attn_out_g, v_ssm_out_g, v_w_out, v_norm2_g, v_w_ff1, v_w_ff2, v_final_g):
    given = dict(x=x, c=c, w_ada=w_ada, b_ada=b_ada, norm1_g=norm1_g, w_in=w_in, sinks=sinks, ssm_lam_re=ssm_lam_re, ssm_lam_im=ssm_lam_im, ssm_log_step=ssm_log_step, ssm_b_re=ssm_b_re, ssm_b_im=ssm_b_im, ssm_c_re=ssm_c_re, ssm_c_im=ssm_c_im, ssm_d=ssm_d, w_glu=w_glu, b_glu=b_glu, attn_out_g=attn_out_g, ssm_out_g=ssm_out_g, w_out=w_out, norm2_g=norm2_g, w_ff1=w_ff1, w_ff2=w_ff2, final_g=final_g, loss_target=loss_target, m_w_ada=m_w_ada, m_b_ada=m_b_ada, m_norm1_g=m_norm1_g, m_w_in=m_w_in, m_sinks=m_sinks, m_ssm_lam_re=m_ssm_lam_re, m_ssm_lam_im=m_ssm_lam_im, m_ssm_log_step=m_ssm_log_step, m_ssm_b_re=m_ssm_b_re, m_ssm_b_im=m_ssm_b_im, m_ssm_c_re=m_ssm_c_re, m_ssm_c_im=m_ssm_c_im, m_ssm_d=m_ssm_d, m_w_glu=m_w_glu, m_b_glu=m_b_glu, m_attn_out_g=m_attn_out_g, m_ssm_out_g=m_ssm_out_g, m_w_out=m_w_out, m_norm2_g=m_norm2_g, m_w_ff1=m_w_ff1, m_w_ff2=m_w_ff2, m_final_g=m_final_g, v_w_ada=v_w_ada, v_b_ada=v_b_ada, v_norm1_g=v_norm1_g, v_w_in=v_w_in, v_sinks=v_sinks, v_ssm_lam_re=v_ssm_lam_re, v_ssm_lam_im=v_ssm_lam_im, v_ssm_log_step=v_ssm_log_step, v_ssm_b_re=v_ssm_b_re, v_ssm_b_im=v_ssm_b_im, v_ssm_c_re=v_ssm_c_re, v_ssm_c_im=v_ssm_c_im, v_ssm_d=v_ssm_d, v_w_glu=v_w_glu, v_b_glu=v_b_glu, v_attn_out_g=v_attn_out_g, v_ssm_out_g=v_ssm_out_g, v_w_out=v_w_out, v_norm2_g=v_norm2_g, v_w_ff1=v_w_ff1, v_w_ff2=v_w_ff2, v_final_g=v_final_g)
    weights = {n: given[n] for n in TWIN_WEIGHTS}
    shared = {n: given[n] for n in SHARED_INPUTS}
    per_example = {n: given[n] for n in ['x', 'c']}
    grad_fn = _jax.value_and_grad(_loss, argnums=(0, 1))

    def one_microbatch(ex, loss_target):
        ex = dict(ex)
        diff = ex.pop(TWIN_DIFF_INPUT)
        return grad_fn(weights, diff, {**shared, **ex}, loss_target)

    if N_MICROBATCH == 1:
        loss, (grad_w, grad_x) = one_microbatch(per_example, given["loss_target"])
    else:
        def body(carry, xs):
            loss_sum, grad_sum = carry
            l_k, (gw_k, gx_k) = one_microbatch(xs[0], xs[1])
            with _jax.named_scope("update"):
                return (loss_sum + l_k, _jax.tree.map(_jnp.add, grad_sum, gw_k)), gx_k

        init = (_jnp.zeros((), _jnp.float32), _jax.tree.map(_jnp.zeros_like, weights))
        (loss, grad_w), grad_x = _jax.lax.scan(body, init, (per_example, given["loss_target"]))
    with _jax.named_scope("update"):
        delta_w, new_m, new_v = {}, {}, {}
        for n in TWIN_WEIGHTS:
            delta_w[n], new_m[n], new_v[n] = _adamw(weights[n], grad_w[n], given["m_" + n], given["v_" + n])
    return (loss, grad_x, *[grad_w[n] for n in TWIN_WEIGHTS], *[delta_w[n] for n in TWIN_WEIGHTS],
            *[new_m[n] for n in TWIN_WEIGHTS], *[new_v[n] for n in TWIN_WEIGHTS])
```

```python
import functools
import math

import jax
import jax.numpy as jnp
from jax import lax
from jax.experimental import pallas as pl
from jax.experimental.pallas import tpu as pltpu

F32, BF16 = jnp.float32, jnp.bfloat16
SDS = jax.ShapeDtypeStruct
MESH_AXES = ("x", "y", "c")
N_DEV = 8
VMEM_LIMIT_BYTES = 56 * 1024 * 1024
SUBLANES, LANES = 8, 128

HEAD_DIM = 64
Q_PER_KV = 8
WINDOW = 128
ROPE_THETA = 10000.0
EPS = 1e-6
N_MOD = 6
SSM_CHUNK = 16
SSM_GROUPS_PER_STEP = 8

ADAM_LR, ADAM_B1, ADAM_B2, ADAM_EPS, ADAM_WD, ADAM_STEP = 0.001, 0.9, 0.999, 1e-08, 0.01, 10
HIGHEST = lax.Precision.HIGHEST


def _cparams(sem):
    return pltpu.CompilerParams(dimension_semantics=sem, vmem_limit_bytes=VMEM_LIMIT_BYTES)


def _block_index(p):
    return 4 * p[0] + 2 * p[1] + p[2]


def _all_gather(name, arrs):
    n = len(arrs)

    def body(*refs):
        ins, outs = refs[:n], refs[n:2 * n]
        send_sems, recv_sems, local_sems = refs[2 * n:]
        x, y, c = lax.axis_index("x"), lax.axis_index("y"), lax.axis_index("c")
        me, sibling = (x, y, c), (x, y, 1 - c)
        chips = [(1 - x, y), (x, 1 - y), (1 - x, 1 - y)]

        def copy(a, k, block, to, from_input=False):
            dst = outs[a].at[_block_index(block)]
            return pltpu.make_async_remote_copy(
                src_ref=ins[a] if from_input else dst, dst_ref=dst,
                send_sem=send_sems.at[a * 7 + k], recv_sem=recv_sems.at[a * 7 + k],
                device_id=to, device_id_type=pl.DeviceIdType.MESH)

        started = []
        for a in range(n):
            mine = pltpu.make_async_copy(ins[a], outs[a].at[_block_index(me)], local_sems.at[a])
            mine.start()
            started.append(mine)
        sends = []
        for a in range(n):
            first = [copy(a, 0, me, sibling, True)]
            first += [copy(a, 1 + j, me, (*chip, c), True) for j, chip in enumerate(chips)]
            for cp in first:
                cp.start()
            sends += first
        for a in range(n):
            for j, chip in enumerate(chips):
                copy(a, 1 + j, (*chip, c), me).wait_recv()
                fwd = copy(a, 4 + j, (*chip, c), sibling)
                fwd.start()
                sends.append(fwd)
        for a in range(n):
            copy(a, 0, sibling, me).wait_recv()
            for j, chip in enumerate(chips):
                copy(a, 4 + j, (*chip, 1 - c), me).wait_recv()
        for cp in sends:
            cp.wait_send()
        for mine in started:
            mine.wait()

    any_spec = pl.BlockSpec(memory_space=pl.ANY)
    return pl.pallas_call(
        body, name=name,
        out_shape=[SDS((N_DEV,) + a.shape, a.dtype) for a in arrs],
        in_specs=[any_spec] * n, out_specs=[any_spec] * n,
        scratch_shapes=[pltpu.SemaphoreType.DMA((7 * n,)), pltpu.SemaphoreType.DMA((7 * n,)),
                        pltpu.SemaphoreType.DMA((n,))],
    )(*arrs)


def _all_to_all(name, arrs):
    n = len(arrs)

    def body(*refs):
        ins, outs = refs[:n], refs[n:2 * n]
        send_sems, recv_sems, local_sems = refs[2 * n:]
        x, y, c = lax.axis_index("x"), lax.axis_index("y"), lax.axis_index("c")
        me = _block_index((x, y, c))
        copies = []
        for a in range(n):
            mine = pltpu.make_async_copy(ins[a].at[me], outs[a].at[me], local_sems.at[a])
            mine.start()
            copies.append(mine)
        for a in range(n):
            for k in range(1, N_DEV):
                peer = ((1 - x) if (k & 4) else x, (1 - y) if (k & 2) else y, (1 - c) if (k & 1) else c)
                cp = pltpu.make_async_remote_copy(
                    src_ref=ins[a].at[_block_index(peer)], dst_ref=outs[a].at[me],
                    send_sem=send_sems.at[a * 7 + k - 1], recv_sem=recv_sems.at[a * 7 + k - 1],
                    device_id=peer, device_id_type=pl.DeviceIdType.MESH)
                cp.start()
                copies.append(cp)
        for cp in copies:
            cp.wait()

    any_spec = pl.BlockSpec(memory_space=pl.ANY)
    return pl.pallas_call(
        body, name=name,
        out_shape=[SDS(a.shape, a.dtype) for a in arrs],
        in_specs=[any_spec] * n, out_specs=[any_spec] * n,
        scratch_shapes=[pltpu.SemaphoreType.DMA((7 * n,)), pltpu.SemaphoreType.DMA((7 * n,)),
                        pltpu.SemaphoreType.DMA((n,))],
    )(*arrs)


def _rowwise(name, fn, rows, vecs, row_outs, acc_outs=(), tm=128):
    t = rows[0].shape[0]
    tm = min(tm, t)
    assert t % tm == 0 and tm % SUBLANES == 0
    n_r, n_v, n_o = len(rows), len(vecs), len(row_outs)

    def body(*refs):
        r_in, v_in = refs[:n_r], refs[n_r:n_r + n_v]
        r_out, a_out = refs[n_r + n_v:n_r + n_v + n_o], refs[n_r + n_v + n_o:]
        outs, accs = fn([r[...] for r in r_in], [v[...] for v in v_in])
        for o_ref, o in zip(r_out, outs, strict=True):
            o_ref[...] = o.astype(o_ref.dtype)
        if a_out:
            @pl.when(pl.program_id(0) == 0)
            def _():
                for a_ref in a_out:
                    a_ref[...] = jnp.zeros_like(a_ref)
            for a_ref, a in zip(a_out, accs, strict=True):
                a_ref[...] += a.reshape(tm // SUBLANES, SUBLANES, a.shape[-1]).sum(axis=0)

    in_specs = [pl.BlockSpec((tm, r.shape[1]), lambda i: (i, 0)) for r in rows]
    in_specs += [pl.BlockSpec(v.shape, lambda i: (0, 0)) for v in vecs]
    out_specs = [pl.BlockSpec((tm, w), lambda i: (i, 0)) for w, _ in row_outs]
    out_specs += [pl.BlockSpec((SUBLANES, w), lambda i: (0, 0)) for w in acc_outs]
    out_shape = [SDS((t, w), dt) for w, dt in row_outs] + [SDS((SUBLANES, w), F32) for w in acc_outs]
    return pl.pallas_call(body, name=name, grid=(t // tm,), in_specs=in_specs, out_specs=out_specs,
                          out_shape=out_shape, compiler_params=_cparams(("arbitrary",)))(*rows, *vecs)


def _tile(n, want):
    if n <= want:
        return n
    for t in range(want // LANES * LANES, 0, -LANES):
        if n % t == 0:
            return t
    raise ValueError(f"no tile for {n}")


_DOT_DIMS = {"nn": (((1,), (0,)), ((), ())), "nt": (((1,), (1,)), ((), ())), "tn": (((0,), (0,)), ((), ()))}


def _matmul(name, a, b, mode, out_dtypes, epilogue=None, extras=(), vecs=(), a_pro=None,
            tm=1024, tn=1024, tk=512, exact=False):
    if mode == "tn":
        (k, m), (k2, n) = a.shape, b.shape
    elif mode == "nt":
        (m, k), (n, k2) = a.shape, b.shape
    else:
        (m, k), (k2, n) = a.shape, b.shape
    assert k == k2
    tm, tn, tk = _tile(m, tm), _tile(n, tn), _tile(k, tk)
    nk = k // tk
    n_e, n_v, n_o = len(extras), len(vecs), len(out_dtypes)
    precision = HIGHEST if exact else None

    def body(*refs):
        a_ref, b_ref = refs[:2]
        e_refs, v_refs = refs[2:2 + n_e], refs[2 + n_e:2 + n_e + n_v]
        o_refs, acc_ref = refs[2 + n_e + n_v:2 + n_e + n_v + n_o], refs[-1]
        kk = pl.program_id(2)

        @pl.when(kk == 0)
        def _():
            acc_ref[...] = jnp.zeros_like(acc_ref)

        av = a_ref[...]
        if a_pro is not None:
            av = a_pro(av)
        acc_ref[...] += lax.dot_general(av, b_ref[...], _DOT_DIMS[mode], precision=precision,
                                        preferred_element_type=F32)

        @pl.when(kk == nk - 1)
        def _():
            acc = acc_ref[...]
            res = (acc,) if epilogue is None else epilogue(acc, [e[...] for e in e_refs], [v[...] for v in v_refs])
            for o_ref, r in zip(o_refs, res, strict=True):
                o_ref[...] = r.astype(o_ref.dtype)

    if mode == "tn":
        a_spec = pl.BlockSpec((tk, tm), lambda i, j, kk: (kk, i))
    else:
        a_spec = pl.BlockSpec((tm, tk), lambda i, j, kk: (i, kk))
    if mode == "nt":
        b_spec = pl.BlockSpec((tn, tk), lambda i, j, kk: (j, kk))
    else:
        b_spec = pl.BlockSpec((tk, tn), lambda i, j, kk: (kk, j))
    tile = pl.BlockSpec((tm, tn), lambda i, j, kk: (i, j))
    in_specs = [a_spec, b_spec] + [tile] * n_e + [pl.BlockSpec((1, tn), lambda i, j, kk: (0, j))] * n_v
    res = pl.pallas_call(
        body, name=name, grid=(m // tm, n // tn, nk), in_specs=in_specs, out_specs=[tile] * n_o,
        out_shape=[SDS((m, n), dt) for dt in out_dtypes],
        scratch_shapes=[pltpu.VMEM((tm, tn), F32)],
        compiler_params=_cparams(("parallel", "parallel", "arbitrary")),
    )(a, b, *extras, *vecs)
    return res[0] if n_o == 1 else res


def _rms_fwd(x):
    r = lax.rsqrt(jnp.mean(x * x, axis=-1, keepdims=True) + EPS)
    return x * r, r


def _rms_bwd(dxn, xn, r):
    return r * (dxn - xn * jnp.mean(dxn * xn, axis=-1, keepdims=True))


_INV_SQRT2 = 1.0 / math.sqrt(2.0)
_INV_SQRT2PI = 1.0 / math.sqrt(2.0 * math.pi)


def _gelu(y):
    return 0.5 * y * (1.0 + lax.erf(y * _INV_SQRT2))


def _gelu_grad(y):
    return 0.5 * (1.0 + lax.erf(y * _INV_SQRT2)) + y * (_INV_SQRT2PI * jnp.exp(-0.5 * y * y))


def _sigmoid(z):
    return 1.0 / (1.0 + jnp.exp(-z))


def _adam_math(w, g, m, v):
    m = ADAM_B1 * m + (1.0 - ADAM_B1) * g
    v = ADAM_B2 * v + (1.0 - ADAM_B2) * (g * g)
    m_hat = m / (1.0 - ADAM_B1 ** ADAM_STEP)
    v_hat = v / (1.0 - ADAM_B2 ** ADAM_STEP)
    delta = -ADAM_LR * (m_hat / (jnp.sqrt(v_hat) + ADAM_EPS) + ADAM_WD * w)
    return delta, m, v


def _norm_mod_fwd(name, x, g, scale, shift):
    def fn(rows, vecs):
        (xv,), (gv, sc, sh) = rows, vecs
        xn, _ = _rms_fwd(xv)
        return [(xn * gv) * (1.0 + sc) + sh], []
    return _rowwise(name, fn, [x], [g, scale, shift], [(x.shape[1], BF16)])[0]


def _norm_mod_bwd(name, x, dh, dres, g, scale):
    d = x.shape[1]

    def fn(rows, vecs):
        (xv, dhv, drv), (gv, sc) = rows, vecs
        xn, r = _rms_fwd(xv)
        t = xn * gv
        dt = dhv * (1.0 + sc)
        dx = drv + _rms_bwd(dt * gv, xn, r)
        return [dx], [dhv * t, dhv, dt * xn]
    dx, dscale, dshift, dg = _rowwise(name, fn, [x, dh, dres], [g, scale], [(d, F32)], [d, d, d])
    return dx, dscale.sum(0), dshift.sum(0), dg.sum(0)


def _gate_bwd(name, dx, val, gate):
    d = dx.shape[1]

    def fn(rows, vecs):
        (dxv, vv), (gv,) = rows, vecs
        return [dxv * gv], [dxv * vv.astype(F32)]
    dval, dgate = _rowwise(name, fn, [dx, val], [gate], [(d, BF16)], [d])
    return dval, dgate.sum(0)


def _final_loss(name, x, tgt, g):
    d = x.shape[1]

    def fn(rows, vecs):
        (xv, tv), (gv,) = rows, vecs
        xn, r = _rms_fwd(xv)
        e = xn * gv - tv
        dy = e * (1.0 / d)
        dx = _rms_bwd(dy * gv, xn, r)
        return [dx], [e * e, dy * xn]
    dx, sq, dg = _rowwise(name, fn, [x, tgt], [g], [(d, F32)], [d, d])
    return dx, 0.5 * jnp.sum(sq) / d, dg.sum(0)


def _group_norm_fwd(name, attn, ssm, g_a, g_s):
    def fn(rows, vecs):
        (av, sv), (ga, gs) = rows, vecs
        return [jnp.concatenate([_rms_fwd(av)[0] * ga, _rms_fwd(sv)[0] * gs], axis=1)], []
    return _rowwise(name, fn, [attn, ssm], [g_a, g_s], [(attn.shape[1] + ssm.shape[1], BF16)])[0]


def _group_norm_bwd(name, attn, ssm, dmixed, g_a, g_s):
    da_w, ds_w = attn.shape[1], ssm.shape[1]

    def fn(rows, vecs):
        (av, sv, dm), (ga, gs) = rows, vecs
        an, ra = _rms_fwd(av)
        sn, rs = _rms_fwd(sv)
        dma, dms = dm[:, :da_w], dm[:, da_w:]
        return [_rms_bwd(dma * ga, an, ra), _rms_bwd(dms * gs, sn, rs)], [dma * an, dms * sn]
    dattn, dssm, dga, dgs = _rowwise(name, fn, [attn, ssm, dmixed], [g_a, g_s],
                                     [(da_w, F32), (ds_w, F32)], [da_w, ds_w])
    return dattn, dssm, dga.sum(0), dgs.sum(0)


def _gelu_fwd(name, y):
    def fn(rows, vecs):
        return [_gelu(rows[0])], []
    return _rowwise(name, fn, [y], [], [(y.shape[1], BF16)])[0]


def _glu_bwd(name, dout, y, z):
    d = y.shape[1]

    def fn(rows, vecs):
        dov, yv, zv = rows
        sg = _sigmoid(zv)
        dz = dov * _gelu(yv) * sg * (1.0 - sg)
        return [dz, dov * sg], [dz]
    dz, dyg, db = _rowwise(name, fn, [dout, y, z], [], [(d, BF16), (d, F32)], [d])
    return dz, dyg, db.sum(0)


def _adam_shard(name, parts, w, m, v):
    r, c = w.shape
    row_bytes = 2 * c * (N_DEV * parts.dtype.itemsize + 7 * 4)
    tr = min(128, r)
    while tr > SUBLANES and tr * row_bytes > VMEM_LIMIT_BYTES // 2:
        tr //= 2
    assert r % tr == 0

    def body(p_ref, w_ref, m_ref, v_ref, g_out, d_out, m_out, v_out):
        g = p_ref[0].astype(F32)
        for j in range(1, N_DEV):
            g = g + p_ref[j].astype(F32)
        delta, m_new, v_new = _adam_math(w_ref[...], g, m_ref[...], v_ref[...])
        g_out[...], d_out[...], m_out[...], v_out[...] = g, delta, m_new, v_new

    tile = pl.BlockSpec((tr, c), lambda i: (i, 0))
    return pl.pallas_call(
        body, name=name, grid=(r // tr,),
        in_specs=[pl.BlockSpec((N_DEV, tr, c), lambda i: (0, i, 0)), tile, tile, tile],
        out_specs=[tile] * 4, out_shape=[SDS((r, c), F32)] * 4,
        compiler_params=_cparams(("parallel",)))(parts, w, m, v)


def _ada_update(name, c_act_t, dmod, w, m, v, tr=128):
    r, c = w.shape
    tr = min(tr, r)
    assert r % tr == 0

    def body(c_ref, d_ref, w_ref, m_ref, v_ref, g_out, d_out, m_out, v_out):
        g = jnp.dot(c_ref[...], d_ref[...], precision=HIGHEST, preferred_element_type=F32)
        delta, m_new, v_new = _adam_math(w_ref[...], g, m_ref[...], v_ref[...])
        g_out[...], d_out[...], m_out[...], v_out[...] = g, delta, m_new, v_new

    tile = pl.BlockSpec((tr, c), lambda i: (i, 0))
    return pl.pallas_call(
        body, name=name, grid=(r // tr,),
        in_specs=[pl.BlockSpec((tr, N_DEV), lambda i: (i, 0)), pl.BlockSpec((N_DEV, c), lambda i: (0, 0)),
                  tile, tile, tile],
        out_specs=[tile] * 4, out_shape=[SDS((r, c), F32)] * 4,
        compiler_params=_cparams(("parallel",)))(c_act_t, dmod, w, m, v)


def _rotate_half(x):
    w = x.shape[1]
    half = HEAD_DIM // 2
    lane = lax.broadcasted_iota(jnp.int32, x.shape, 1)
    return jnp.where((lane % HEAD_DIM) < half, -pltpu.roll(x, w - half, 1), pltpu.roll(x, half, 1))


def _lane_tile(tab, w):
    return tab[:, :w] if w <= LANES else jnp.tile(tab, (1, w // LANES))


def _rope(x, cos, sin):
    return x * cos + _rotate_half(x) * sin


def _rope_t(dy, cos, sin):
    return dy * cos - _rotate_half(dy) * sin


def _band_mask(n):
    i = lax.broadcasted_iota(jnp.int32, (WINDOW, 2 * WINDOW), 0)
    j = lax.broadcasted_iota(jnp.int32, (WINDOW, 2 * WINDOW), 1)
    return (j > i) & (j <= i + WINDOW) & ((n > 0) | (j >= WINDOW))


def _attn_specs(da, dkv, nb):
    cur = lambda n: (jnp.minimum(n, nb - 1), 0)
    prev = lambda n: (jnp.maximum(jnp.minimum(n, nb - 1) - 1, 0), 0)
    return dict(
        q=pl.BlockSpec((WINDOW, da), cur), kv_cur=pl.BlockSpec((WINDOW, dkv), cur),
        kv_prev=pl.BlockSpec((WINDOW, dkv), prev), tab_cur=pl.BlockSpec((WINDOW, LANES), cur),
        tab_prev=pl.BlockSpec((WINDOW, LANES), prev))


def _attn_fwd(name, q, k, v, cos, sin, sinks):
    s, da = q.shape
    dkv = k.shape[1]
    nq, nb = da // HEAD_DIM, s // WINDOW
    scale = HEAD_DIM ** -0.5

    def body(q_ref, kp_ref, kc_ref, vp_ref, vc_ref, cc_ref, sc_ref, cp_ref, sp_ref, sink_ref, o_ref, lse_ref):
        n = pl.program_id(0)
        cc, sc, cp, sp = cc_ref[...], sc_ref[...], cp_ref[...], sp_ref[...]
        qr = _rope(q_ref[...], _lane_tile(cc, da), _lane_tile(sc, da)).astype(BF16)
        kk = jnp.concatenate([_rope(kp_ref[...], _lane_tile(cp, dkv), _lane_tile(sp, dkv)),
                              _rope(kc_ref[...], _lane_tile(cc, dkv), _lane_tile(sc, dkv))], axis=0).astype(BF16)
        vv = jnp.concatenate([vp_ref[...], vc_ref[...]], axis=0).astype(BF16)
        valid = _band_mask(n)
        for hq in range(nq):
            hk = hq // Q_PER_KV
            qs, ks = slice(hq * HEAD_DIM, (hq + 1) * HEAD_DIM), slice(hk * HEAD_DIM, (hk + 1) * HEAD_DIM)
            sco = lax.dot_general(qr[:, qs], kk[:, ks], _DOT_DIMS["nt"], preferred_element_type=F32) * scale
            sco = jnp.where(valid, sco, -1e30)
            sink = sink_ref[0:1, hq:hq + 1]
            mx = jnp.maximum(jnp.max(sco, axis=1, keepdims=True), sink)
            p = jnp.exp(sco - mx)
            den = jnp.sum(p, axis=1, keepdims=True) + jnp.exp(sink - mx)
            o_ref[:, qs] = jnp.dot((p / den).astype(BF16), vv[:, ks], preferred_element_type=F32)
            lse_ref[:, hq:hq + 1] = mx + jnp.log(den)

    sp_ = _attn_specs(da, dkv, nb)
    return pl.pallas_call(
        body, name=name, grid=(nb,),
        in_specs=[sp_["q"], sp_["kv_prev"], sp_["kv_cur"], sp_["kv_prev"], sp_["kv_cur"],
                  sp_["tab_cur"], sp_["tab_cur"], sp_["tab_prev"], sp_["tab_prev"],
                  pl.BlockSpec((1, nq), lambda n: (0, 0))],
        out_specs=[sp_["q"], pl.BlockSpec((WINDOW, nq), lambda n: (n, 0))],
        out_shape=[SDS((s, da), F32), SDS((s, nq), F32)],
        compiler_params=_cparams(("arbitrary",)))(q, k, k, v, v, cos, sin, cos, sin, sinks)


def _attn_bwd(name, q, k, v, cos, sin, sinks, out, lse, dout):
    s, da = q.shape
    dkv = k.shape[1]
    nq, nb = da // HEAD_DIM, s // WINDOW
    scale = HEAD_DIM ** -0.5

    def body(q_ref, kp_ref, kc_ref, vp_ref, vc_ref, cc_ref, sc_ref, cp_ref, sp_ref, sink_ref, o_ref, lse_ref,
             do_ref, dq_ref, dk_ref, dv_ref, dsink_ref, dk_carry, dv_carry):
        n = pl.program_id(0)
        cp, sp = _lane_tile(cp_ref[...], dkv), _lane_tile(sp_ref[...], dkv)

        @pl.when(n == 0)
        def _():
            dk_carry[...] = jnp.zeros_like(dk_carry)
            dv_carry[...] = jnp.zeros_like(dv_carry)
            dsink_ref[...] = jnp.zeros_like(dsink_ref)

        @pl.when(n < nb)
        def _():
            cc, sc = cc_ref[...], sc_ref[...]
            qr = _rope(q_ref[...], _lane_tile(cc, da), _lane_tile(sc, da)).astype(BF16)
            kk = jnp.concatenate([_rope(kp_ref[...], cp, sp),
                                  _rope(kc_ref[...], _lane_tile(cc, dkv), _lane_tile(sc, dkv))], axis=0).astype(BF16)
            vv = jnp.concatenate([vp_ref[...], vc_ref[...]], axis=0).astype(BF16)
            valid = _band_mask(n)
            dkk = [None] * (nq // Q_PER_KV)
            dvv = [None] * (nq // Q_PER_KV)
            dsink_cols = []
            for hq in range(nq):
                hk = hq // Q_PER_KV
                qs, ks = slice(hq * HEAD_DIM, (hq + 1) * HEAD_DIM), slice(hk * HEAD_DIM, (hk + 1) * HEAD_DIM)
                lse_h = lse_ref[:, hq:hq + 1]
                sco = lax.dot_general(qr[:, qs], kk[:, ks], _DOT_DIMS["nt"], preferred_element_type=F32) * scale
                probs = jnp.where(valid, jnp.exp(sco - lse_h), 0.0)
                do_h = do_ref[:, qs]
                delta = jnp.sum(do_h * o_ref[:, qs], axis=1, keepdims=True)
                dp = lax.dot_general(do_h.astype(BF16), vv[:, ks], _DOT_DIMS["nt"], preferred_element_type=F32)
                ds = (probs * (dp - delta) * scale).astype(BF16)
                dq_ref[:, qs] = jnp.dot(ds, kk[:, ks], preferred_element_type=F32)
                dk_h = lax.dot_general(ds, qr[:, qs], _DOT_DIMS["tn"], preferred_element_type=F32)
                dv_h = lax.dot_general(probs.astype(BF16), do_h.astype(BF16), _DOT_DIMS["tn"],
                                       preferred_element_type=F32)
                dkk[hk] = dk_h if dkk[hk] is None else dkk[hk] + dk_h
                dvv[hk] = dv_h if dvv[hk] is None else dvv[hk] + dv_h
                dsink_cols.append(-jnp.exp(sink_ref[0:1, hq:hq + 1] - lse_h) * delta)
            dq_ref[...] = _rope_t(dq_ref[...], _lane_tile(cc, da), _lane_tile(sc, da))
            for hq in range(nq):
                col = dsink_cols[hq]
                dsink_ref[:, hq:hq + 1] += col.reshape(WINDOW // SUBLANES, SUBLANES, 1).sum(axis=0)
            for hk in range(nq // Q_PER_KV):
                ks = slice(hk * HEAD_DIM, (hk + 1) * HEAD_DIM)
                dk_ref[:, ks] = dk_carry[:, ks] + dkk[hk][:WINDOW]
                dv_ref[:, ks] = dv_carry[:, ks] + dvv[hk][:WINDOW]
                dk_carry[:, ks] = dkk[hk][WINDOW:]
                dv_carry[:, ks] = dvv[hk][WINDOW:]
            dk_ref[...] = _rope_t(dk_ref[...], cp, sp)

        @pl.when(n == nb)
        def _():
            dk_ref[...] = _rope_t(dk_carry[...], cp, sp)
            dv_ref[...] = dv_carry[...]

    sp_ = _attn_specs(da, dkv, nb)
    last_prev = lambda n: (jnp.maximum(n - 1, 0), 0)
    tab_prev = pl.BlockSpec((WINDOW, LANES), last_prev)
    kv_out = pl.BlockSpec((WINDOW, dkv), last_prev)
    lse_spec = pl.BlockSpec((WINDOW, nq), lambda n: (jnp.minimum(n, nb - 1), 0))
    dq, dk, dv, dsink = pl.pallas_call(
        body, name=name, grid=(nb + 1,),
        in_specs=[sp_["q"], sp_["kv_prev"], sp_["kv_cur"], sp_["kv_prev"], sp_["kv_cur"],
                  sp_["tab_cur"], sp_["tab_cur"], tab_prev, tab_prev,
                  pl.BlockSpec((1, nq), lambda n: (0, 0)), sp_["q"], lse_spec, sp_["q"]],
        out_specs=[sp_["q"], kv_out, kv_out, pl.BlockSpec((SUBLANES, nq), lambda n: (0, 0))],
        out_shape=[SDS((s, da), F32), SDS((s, dkv), F32), SDS((s, dkv), F32), SDS((SUBLANES, nq), F32)],
        scratch_shapes=[pltpu.VMEM((WINDOW, dkv), F32), pltpu.VMEM((WINDOW, dkv), F32)],
        compiler_params=_cparams(("arbitrary",)))(q, k, k, v, v, cos, sin, cos, sin, sinks, out, lse, dout)
    return dq, dk, dv, dsink.sum(0)


def _ssm_operators(lam_re, lam_im, log_step, b_re, b_im, c_re, c_im, d_skip):
    g, p = lam_re.shape
    h = b_re.shape[-1]
    l = SSM_CHUNK
    step = jnp.exp(log_step)[:, None]
    mag = jnp.exp(lam_re * step)
    ar, ai = mag * jnp.cos(lam_im * step), mag * jnp.sin(lam_im * step)
    den = lam_re * lam_re + lam_im * lam_im
    cr = ((ar - 1.0) * lam_re + ai * lam_im) / den
    ci = (ai * lam_re - (ar - 1.0) * lam_im) / den
    bbr = cr[..., None] * b_re - ci[..., None] * b_im
    bbi = cr[..., None] * b_im + ci[..., None] * b_re
    pr, pi = [jnp.ones_like(ar)], [jnp.zeros_like(ar)]
    for _ in range(l):
        pr, pi = pr + [pr[-1] * ar - pi[-1] * ai], pi + [pr[-1] * ai + pi[-1] * ar]
    pwr, pwi = jnp.stack(pr, axis=1), jnp.stack(pi, axis=1)
    cpr = c_re[:, None] * pwr[:, :, None, :] - c_im[:, None] * pwi[:, :, None, :]
    cpi = c_re[:, None] * pwi[:, :, None, :] + c_im[:, None] * pwr[:, :, None, :]
    kern = (jnp.einsum("gtop,gpi->gtoi", cpr[:, :l], bbr, precision=HIGHEST)
            - jnp.einsum("gtop,gpi->gtoi", cpi[:, :l], bbi, precision=HIGHEST))
    kern = kern.at[:, 0].add(d_skip.reshape(g, h)[:, :, None] * jnp.eye(h, dtype=F32))
    tm = jnp.stack([jnp.pad(kern[:, :l - j], ((0, 0), (j, 0), (0, 0), (0, 0))) for j in range(l)], axis=1)
    tm = tm.transpose(0, 1, 4, 2, 3).reshape(g, l * h, l * h)
    rev_r, rev_i = pwr[:, l - 1::-1][:, :l], pwi[:, l - 1::-1][:, :l]
    er = rev_r[:, :, None, :] * bbr.transpose(0, 2, 1)[:, None] - rev_i[:, :, None, :] * bbi.transpose(0, 2, 1)[:, None]
    ei = rev_r[:, :, None, :] * bbi.transpose(0, 2, 1)[:, None] + rev_i[:, :, None, :] * bbr.transpose(0, 2, 1)[:, None]
    em = jnp.concatenate([er, ei], axis=-1).reshape(g, l * h, 2 * p)
    fr = cpr[:, 1:].transpose(0, 3, 1, 2).reshape(g, p, l * h)
    fi = -cpi[:, 1:].transpose(0, 3, 1, 2).reshape(g, p, l * h)
    fm = jnp.concatenate([fr, fi], axis=1)
    return tm, em, fm, pwr[:, l], pwi[:, l]


def _decay_lanes(alr, ali):
    return jnp.concatenate([alr, alr], axis=1), jnp.concatenate([-ali, ali], axis=1)


def _ssm_fwd(name, u, tm, em, fm, acat, bcat):
    g, nc, lh = u.shape
    p2 = em.shape[-1]
    gb = SSM_GROUPS_PER_STEP
    half = p2 // 2

    def body(u_ref, tm_ref, em_ref, fm_ref, a_ref, b_ref, y_ref, xp_ref, st_ref):
        for i in range(gb):
            st_ref[pl.ds(i, nc, stride=gb), :] = jnp.dot(u_ref[i], em_ref[i], precision=HIGHEST,
                                                         preferred_element_type=F32)
        av, bv = a_ref[...], b_ref[...]

        def step(c, x):
            rows = pl.ds(pl.multiple_of(c * gb, gb), gb)
            loc = st_ref[rows, :]
            st_ref[rows, :] = x
            return av * x + bv * pltpu.roll(x, half, 1) + loc
        lax.fori_loop(0, nc, step, jnp.zeros((gb, p2), F32))
        for i in range(gb):
            xp = st_ref[pl.ds(i, nc, stride=gb), :]
            xp_ref[i] = xp
            y_ref[i] = (jnp.dot(u_ref[i], tm_ref[i], precision=HIGHEST, preferred_element_type=F32)
                        + jnp.dot(xp, fm_ref[i], precision=HIGHEST, preferred_element_type=F32))

    blk = lambda r, c: pl.BlockSpec((gb, r, c), lambda i: (i, 0, 0))
    vec = pl.BlockSpec((gb, p2), lambda i: (i, 0))
    return pl.pallas_call(
        body, name=name, grid=(g // gb,),
        in_specs=[blk(nc, lh), blk(lh, lh), blk(lh, p2), blk(p2, lh), vec, vec],
        out_specs=[blk(nc, lh), blk(nc, p2)],
        out_shape=[SDS((g, nc, lh), F32), SDS((g, nc, p2), F32)],
        scratch_shapes=[pltpu.VMEM((nc * gb, p2), F32)],
        compiler_params=_cparams(("parallel",)))(u, tm, em, fm, acat, bcat)


def _ssm_bwd(name, u, dy, xprev, tm, em, fm, acat, bcat):
    g, nc, lh = u.shape
    p2 = em.shape[-1]
    gb = SSM_GROUPS_PER_STEP
    half = p2 // 2

    def body(u_ref, dy_ref, xp_ref, tm_ref, em_ref, fm_ref, a_ref, b_ref,
             du_ref, dtm_ref, dem_ref, dfm_ref, r1_ref, r2_ref, gs_ref, xs_ref):
        for i in range(gb):
            gs_ref[pl.ds(i, nc, stride=gb), :] = lax.dot_general(
                dy_ref[i], fm_ref[i], _DOT_DIMS["nt"], precision=HIGHEST, preferred_element_type=F32)
            xs_ref[pl.ds(i, nc, stride=gb), :] = xp_ref[i]
        av, bv = a_ref[...], b_ref[...]

        def step(t, carry):
            grad, r1, r2 = carry
            c = nc - 1 - t
            rows = pl.ds(pl.multiple_of(c * gb, gb), gb)
            dxp, xp = gs_ref[rows, :], xs_ref[rows, :]
            gs_ref[rows, :] = grad
            r1 = r1 + grad * xp
            r2 = r2 + grad * pltpu.roll(xp, half, 1)
            return dxp + av * grad - bv * pltpu.roll(grad, half, 1), r1, r2
        zero = jnp.zeros((gb, p2), F32)
        _, r1, r2 = lax.fori_loop(0, nc, step, (zero, zero, zero))
        r1_ref[...], r2_ref[...] = r1, r2
        for i in range(gb):
            dxl = gs_ref[pl.ds(i, nc, stride=gb), :]
            du_ref[i] = (lax.dot_general(dy_ref[i], tm_ref[i], _DOT_DIMS["nt"], precision=HIGHEST,
                                         preferred_element_type=F32)
                         + lax.dot_general(dxl, em_ref[i], _DOT_DIMS["nt"], precision=HIGHEST,
                                           preferred_element_type=F32))
            dtm_ref[i] = lax.dot_general(u_ref[i], dy_ref[i], _DOT_DIMS["tn"], precision=HIGHEST,
                                         preferred_element_type=F32)
            dfm_ref[i] = lax.dot_general(xp_ref[i], dy_ref[i], _DOT_DIMS["tn"], precision=HIGHEST,
                                         preferred_element_type=F32)
            dem_ref[i] = lax.dot_general(u_ref[i], dxl, _DOT_DIMS["tn"], precision=HIGHEST,
                                         preferred_element_type=F32)

    blk = lambda r, c: pl.BlockSpec((gb, r, c), lambda i: (i, 0, 0))
    vec = pl.BlockSpec((gb, p2), lambda i: (i, 0))
    return pl.pallas_call(
        body, name=name, grid=(g // gb,),
        in_specs=[blk(nc, lh), blk(nc, lh), blk(nc, p2), blk(lh, lh), blk(lh, p2), blk(p2, lh), vec, vec],
        out_specs=[blk(nc, lh), blk(lh, lh), blk(lh, p2), blk(p2, lh), vec, vec],
        out_shape=[SDS((g, nc, lh), F32), SDS((g, lh, lh), F32), SDS((g, lh, p2), F32), SDS((g, p2, lh), F32),
                   SDS((g, p2), F32), SDS((g, p2), F32)],
        scratch_shapes=[pltpu.VMEM((nc * gb, p2), F32), pltpu.VMEM((nc * gb, p2), F32)],
        compiler_params=_cparams(("parallel",)))(u, dy, xprev, tm, em, fm, acat, bcat)


def _to_chunks(u, g, h):
    s = u.shape[0]
    return u.reshape(s // SSM_CHUNK, SSM_CHUNK, g, h).transpose(2, 0, 1, 3).reshape(g, s // SSM_CHUNK, SSM_CHUNK * h)


def _from_chunks(y, g, h):
    nc = y.shape[1]
    return y.reshape(g, nc, SSM_CHUNK, h).transpose(1, 2, 0, 3).reshape(nc * SSM_CHUNK, g * h)


_SMALL = ("b_ada", "norm1_g", "sinks", "ssm_lam_re", "ssm_lam_im", "ssm_log_step", "ssm_b_re", "ssm_b_im",
          "ssm_c_re", "ssm_c_im", "ssm_d", "b_glu", "attn_out_g", "ssm_out_g", "norm2_g", "final_g")
_WEIGHTS = ("w_ada", "b_ada", "norm1_g", "w_in", "sinks", "ssm_lam_re", "ssm_lam_im", "ssm_log_step", "ssm_b_re",
            "ssm_b_im", "ssm_c_re", "ssm_c_im", "ssm_d", "w_glu", "b_glu", "attn_out_g", "ssm_out_g", "w_out",
            "norm2_g", "w_ff1", "w_ff2", "final_g")
_PACK_ALIGN = 128 * LANES


def _pack(parts):
    flat = jnp.concatenate([p.reshape(-1).astype(F32) for p in parts])
    pad = (-flat.shape[0]) % _PACK_ALIGN
    return jnp.pad(flat, (0, pad)).reshape(-1, LANES)


def _cols_to_blocks(w, n_blocks):
    k, n = w.shape
    return w.reshape(k, n_blocks, n // n_blocks).transpose(1, 0, 2)


def _blocks_to_cols(w):
    nb, k, n = w.shape
    return w.transpose(1, 0, 2).reshape(k, nb * n)


def kernel(x, c, w_ada, b_ada, norm1_g, w_in, sinks, ssm_lam_re, ssm_lam_im, ssm_log_step, ssm_b_re, ssm_b_im, ssm_c_re, ssm_c_im, ssm_d, w_glu, b_glu, attn_out_g, ssm_out_g, w_out, norm2_g, w_ff1, w_ff2, final_g, loss_target, m_w_ada, m_b_ada, m_norm1_g, m_w_in, m_sinks, m_ssm_lam_re, m_ssm_lam_im, m_ssm_log_step, m_ssm_b_re, m_ssm_b_im, m_ssm_c_re, m_ssm_c_im, m_ssm_d, m_w_glu, m_b_glu, m_attn_out_g, m_ssm_out_g, m_w_out, m_norm2_g, m_w_ff1, m_w_ff2, m_final_g, v_w_ada, v_b_ada, v_norm1_g, v_w_in, v_sinks, v_ssm_lam_re, v_ssm_lam_im, v_ssm_log_step, v_ssm_b_re, v_ssm_b_im, v_ssm_c_re, v_ssm_c_im, v_ssm_d, v_w_glu, v_b_glu, v_attn_out_g, v_ssm_out_g, v_w_out, v_norm2_g, v_w_ff1, v_w_ff2, v_final_g):
    args = dict(locals())
    weights = {n: args[n] for n in _WEIGHTS}
    mom = {n: args["m_" + n] for n in _WEIGHTS}
    var = {n: args["v_" + n] for n in _WEIGHTS}
    me = 4 * lax.axis_index("x") + 2 * lax.axis_index("y") + lax.axis_index("c")

    _, s, d = x.shape
    xs, tgt = x[0], loss_target[0]
    d_ssm = ssm_d.shape[-1]
    d_attn = d - d_ssm
    nq = d_attn // HEAD_DIM
    d_kv = (nq // Q_PER_KV) * HEAD_DIM
    g_ssm, p_state, h_ssm = ssm_b_re.shape[1:]

    shards = [w_in[0].astype(BF16), w_glu[0].astype(BF16), w_out[0].astype(BF16), w_ff1[0].astype(BF16),
              w_ff2[0].astype(BF16)]
    g_in, g_glu, g_out, g_ff1, g_ff2 = _all_gather("gather_weights", shards)
    w_in_f = _blocks_to_cols(g_in)
    wq, wkv, wu = w_in_f[:, :d_attn], w_in_f[:, d_attn:d_attn + 2 * d_kv], w_in_f[:, d_attn + 2 * d_kv:]
    w_glu_f = g_glu.reshape(d_ssm, d_ssm)
    w_out_f = g_out.reshape(d, d)
    w_ff1_f = _blocks_to_cols(g_ff1)
    w_ff2_f = g_ff2.reshape(-1, d)

    c_all = _all_gather("gather_c", [c])[0].reshape(N_DEV, d)
    n_loc = w_ada.shape[-1]
    b_loc = lax.dynamic_slice_in_dim(b_ada, me * n_loc, n_loc, axis=1)
    silu = lambda t: t * _sigmoid(t)
    mod_part = _matmul("ada_mod", c_all, w_ada[0], "nn", [F32], a_pro=silu, vecs=[b_loc], exact=True,
                       epilogue=lambda acc, e, v: (acc + v[0],), tn=512, tk=d)
    mod_all = _all_gather("gather_mod", [mod_part])[0]
    mod = lax.dynamic_index_in_dim(mod_all, me, axis=1, keepdims=False).reshape(N_MOD, 1, d)
    shift1, scale1, gate1, shift2, scale2, gate2 = [mod[i] for i in range(N_MOD)]

    h1 = _norm_mod_fwd("norm1", xs, norm1_g, scale1, shift1)
    q = _matmul("proj_q", h1, wq, "nn", [F32])
    kv = _matmul("proj_kv", h1, wkv, "nn", [F32])
    u = _matmul("proj_u", h1, wu, "nn", [F32])
    k, v = kv[:, :d_kv], kv[:, d_kv:]

    half = HEAD_DIM // 2
    inv_freq = ROPE_THETA ** (-jnp.arange(half, dtype=F32) / half)
    ang = jnp.arange(s, dtype=F32)[:, None] * inv_freq[None, :]
    cos_t, sin_t = jnp.tile(jnp.cos(ang), (1, 4)), jnp.tile(jnp.sin(ang), (1, 4))
    attn, lse = _attn_fwd("attn_fwd", q, k, v, cos_t, sin_t, sinks)

    ssm_params = (ssm_lam_re[0], ssm_lam_im[0], ssm_log_step[0], ssm_b_re[0], ssm_b_im[0], ssm_c_re[0],
                  ssm_c_im[0], ssm_d[0])
    (tm_op, em_op, fm_op, alr, ali), ssm_vjp = jax.vjp(_ssm_operators, *ssm_params)
    acat, bcat = _decay_lanes(alr, ali)
    u_ch = _to_chunks(u, g_ssm, h_ssm)
    y_ch, x_prev = _ssm_fwd("ssm_fwd", u_ch, tm_op, em_op, fm_op, acat, bcat)
    y_ssm = _from_chunks(y_ch, g_ssm, h_ssm)
    yg = _gelu_fwd("gelu", y_ssm)
    ssm_out, z_glu = _matmul(
        "glu", yg, w_glu_f, "nn", [F32, F32], extras=[y_ssm], vecs=[b_glu],
        epilogue=lambda acc, e, v: (_gelu(e[0]) * _sigmoid(acc + v[0]), acc + v[0]))
    mixed = _group_norm_fwd("group_norm", attn, ssm_out, attn_out_g, ssm_out_g)
    x2, mo = _matmul("out_proj", mixed, w_out_f, "nn", [F32, BF16], extras=[xs], vecs=[gate1],
                     epilogue=lambda acc, e, v: (e[0] + v[0] * acc, acc))

    h2 = _norm_mod_fwd("norm2", x2, norm2_g, scale2, shift2)
    a_ff, f_ff = _matmul("ff1", h2, w_ff1_f, "nn", [BF16, BF16],
                         epilogue=lambda acc, e, v: (acc, jnp.square(jnp.maximum(acc, 0.0))))
    x3, ff = _matmul("ff2", f_ff, w_ff2_f, "nn", [F32, BF16], extras=[x2], vecs=[gate2],
                     epilogue=lambda acc, e, v: (e[0] + v[0] * acc, acc))

    dx3, loss_local, d_final_g = _final_loss("final_loss", x3, tgt, final_g.reshape(1, d))
    loss = lax.psum(loss_local, MESH_AXES)

    dff, d_gate2 = _gate_bwd("gate2_bwd", dx3, ff, gate2)
    da_ff = _matmul("ff2_dx", dff, w_ff2_f, "nt", [BF16], extras=[a_ff],
                    epilogue=lambda acc, e, v: (acc * (2.0 * jnp.maximum(e[0].astype(F32), 0.0)),))
    dw_ff2 = _matmul("ff2_dw", f_ff, dff, "tn", [BF16])
    dh2 = _matmul("ff1_dx", da_ff, w_ff1_f, "nt", [F32])
    dw_ff1 = _matmul("ff1_dw", h2, da_ff, "tn", [BF16])
    dx2, d_scale2, d_shift2, d_norm2_g = _norm_mod_bwd("norm2_bwd", x2, dh2, dx3, norm2_g, scale2)

    dmo, d_gate1 = _gate_bwd("gate1_bwd", dx2, mo, gate1)
    dmixed = _matmul("out_dx", dmo, w_out_f, "nt", [F32])
    dw_out = _matmul("out_dw", mixed, dmo, "tn", [BF16])
    dattn, dssm_out, d_attn_g, d_ssm_g = _group_norm_bwd("group_norm_bwd", attn, ssm_out, dmixed, attn_out_g, ssm_out_g)

    dz, dyg_direct, d_b_glu = _glu_bwd("glu_bwd", dssm_out, y_ssm, z_glu)
    dy_ssm = _matmul("glu_dx", dz, w_glu_f, "nt", [F32], extras=[dyg_direct, y_ssm],
                     epilogue=lambda acc, e, v: ((acc + e[0]) * _gelu_grad(e[1]),))
    dw_glu = _matmul("glu_dw", yg, dz, "tn", [BF16])
    du_ch, d_tm, d_em, d_fm, r1, r2 = _ssm_bwd("ssm_bwd", u_ch, _to_chunks(dy_ssm, g_ssm, h_ssm), x_prev,
                                               tm_op, em_op, fm_op, acat, bcat)
    d_alr = r1[:, :p_state] + r1[:, p_state:]
    d_ali = r2[:, p_state:] - r2[:, :p_state]
    d_ssm_params = ssm_vjp((d_tm, d_em, d_fm, d_alr, d_ali))
    du = _from_chunks(du_ch, g_ssm, h_ssm)

    dq, dk, dv, d_sinks = _attn_bwd("attn_bwd", q, k, v, cos_t, sin_t, sinks, attn, lse, dattn)
    dproj = jnp.concatenate([dq, dk, dv, du], axis=1).astype(BF16)
    dh1 = _matmul("in_dx", dproj, w_in_f, "nt", [F32])
    dw_in = _matmul("in_dw", h1, dproj, "tn", [BF16])
    grad_x, d_scale1, d_shift1, d_norm1_g = _norm_mod_bwd("norm1_bwd", xs, dh1, dx2, norm1_g, scale1)

    d_mod = jnp.concatenate([d_shift1, d_scale1, d_gate1, d_shift2, d_scale2, d_gate2])
    small_g = dict(zip(("ssm_lam_re", "ssm_lam_im", "ssm_log_step", "ssm_b_re", "ssm_b_im", "ssm_c_re", "ssm_c_im",
                        "ssm_d"), d_ssm_params, strict=True))
    small_g.update(b_ada=d_mod, norm1_g=d_norm1_g, sinks=d_sinks, b_glu=d_b_glu, attn_out_g=d_attn_g,
                   ssm_out_g=d_ssm_g, norm2_g=d_norm2_g, final_g=d_final_g)
    small_parts = _all_gather("gather_small_grads", [_pack([small_g[n] for n in _SMALL])])[0]
    small = _adam_shard("adam_small", small_parts, _pack([weights[n] for n in _SMALL]),
                        _pack([mom[n] for n in _SMALL]), _pack([var[n] for n in _SMALL]))
    out = {}
    off = 0
    for n in _SMALL:
        size = weights[n].size
        out[n] = [t.reshape(-1)[off:off + size].reshape(weights[n].shape) for t in small]
        off += size

    dmod_all = small_parts.reshape(N_DEV, -1)[:, :N_MOD * d]
    dmod_loc = lax.dynamic_slice_in_dim(dmod_all, me * n_loc, n_loc, axis=1)
    c_act_t = silu(c_all).T
    out["w_ada"] = [t[None] for t in _ada_update("adam_w_ada", c_act_t, dmod_loc, w_ada[0], m_w_ada[0], v_w_ada[0])]

    blocks = [_cols_to_blocks(dw_in, N_DEV), dw_glu.reshape(N_DEV, -1, d_ssm), dw_out.reshape(N_DEV, -1, d),
              _cols_to_blocks(dw_ff1, N_DEV), dw_ff2.reshape(N_DEV, -1, d)]
    parts = _all_to_all("exchange_grads", blocks)
    for n, p in zip(("w_in", "w_glu", "w_out", "w_ff1", "w_ff2"), parts, strict=True):
        out[n] = [t[None] for t in _adam_shard("adam_" + n, p, weights[n][0], mom[n][0], var[n][0])]

    return (loss, grad_x[None], *[out[n][0] for n in _WEIGHTS], *[out[n][1] for n in _WEIGHTS],
            *[out[n][2] for n in _WEIGHTS], *[out[n][3] for n in _WEIGHTS])
```

```python
import functools
import math
import operator

import jax
import jax.numpy as jnp
from jax import lax
from jax.experimental import pallas as pl
from jax.experimental.pallas import tpu as pltpu

F32, BF16 = jnp.float32, jnp.bfloat16
SDS = jax.ShapeDtypeStruct
MESH_AXES = ("x", "y", "c")
N_DEV = 8
VMEM_LIMIT_BYTES = 56 * 1024 * 1024
SUBLANES, LANES = 8, 128

HEAD_DIM = 64
Q_PER_KV = 8
WINDOW = 128
ROPE_THETA = 10000.0
EPS = 1e-6
N_MOD = 6
SSM_CHUNK = 16
SSM_GROUPS_PER_STEP = 8

ADAM_LR, ADAM_B1, ADAM_B2, ADAM_EPS, ADAM_WD, ADAM_STEP = 0.001, 0.9, 0.999, 1e-08, 0.01, 10
HIGHEST = lax.Precision.HIGHEST

RELATIONS_ALL = (1, 4, 2, 6, 5, 3, 7)
RELATIONS_SAME_CORE = (1, 4, 2, 6)
RELATIONS_OTHER_CORE = (5, 3, 7)


def _cparams(sem):
    return pltpu.CompilerParams(dimension_semantics=sem, vmem_limit_bytes=VMEM_LIMIT_BYTES)


def _block_index(p):
    return 4 * p[0] + 2 * p[1] + p[2]


def _me():
    return lax.axis_index("x"), lax.axis_index("y"), lax.axis_index("c")


class _Gather:
    def __init__(self, arrs):
        self.ins = list(arrs)
        self.out_shapes = [SDS((N_DEV,) + a.shape, a.dtype) for a in arrs]
        self.n_rdma, self.n_local = 7 * len(arrs), len(arrs)
        self.rdma_base = self.local_base = 0

    def _copy(self, ins, outs, send, recv, a, k, block, to, from_input=False):
        dst = outs[a].at[_block_index(block)]
        sem = self.rdma_base + a * 7 + k
        return pltpu.make_async_remote_copy(
            src_ref=ins[a] if from_input else dst, dst_ref=dst, send_sem=send.at[sem], recv_sem=recv.at[sem],
            device_id=to, device_id_type=pl.DeviceIdType.MESH)

    def _first(self, ins, outs, send, recv, a):
        x, y, c = _me()
        chips = [(1 - x, y), (x, 1 - y), (1 - x, 1 - y)]
        cps = [self._copy(ins, outs, send, recv, a, 0, (x, y, c), (x, y, 1 - c), True)]
        return cps + [self._copy(ins, outs, send, recv, a, 1 + j, (x, y, c), (*chip, c), True)
                      for j, chip in enumerate(chips)]

    def _mine(self, ins, outs, local, a):
        return pltpu.make_async_copy(ins[a], outs[a].at[_block_index(_me())], local.at[self.local_base + a])

    def start(self, ins, outs, send, recv, local):
        for a in range(len(ins)):
            self._mine(ins, outs, local, a).start()
            for cp in self._first(ins, outs, send, recv, a):
                cp.start()

    def finish(self, ins, outs, send, recv, local):
        x, y, c = _me()
        me, sibling = (x, y, c), (x, y, 1 - c)
        chips = [(1 - x, y), (x, 1 - y), (1 - x, 1 - y)]
        forwards = []
        for a in range(len(ins)):
            for j, chip in enumerate(chips):
                self._copy(ins, outs, send, recv, a, 1 + j, (*chip, c), me).wait_recv()
                fwd = self._copy(ins, outs, send, recv, a, 4 + j, (*chip, c), sibling)
                fwd.start()
                forwards.append(fwd)
        for a in range(len(ins)):
            self._copy(ins, outs, send, recv, a, 0, sibling, me).wait_recv()
            for j, chip in enumerate(chips):
                self._copy(ins, outs, send, recv, a, 4 + j, (*chip, 1 - c), me).wait_recv()
            for cp in self._first(ins, outs, send, recv, a):
                cp.wait_send()
            self._mine(ins, outs, local, a).wait()
        for fwd in forwards:
            fwd.wait_send()


class _Exchange:
    def __init__(self, arrs, relations):
        self.ins, self.relations = list(arrs), tuple(relations)
        self.out_shapes = [SDS((len(relations),) + a.shape[1:], a.dtype) for a in arrs]
        self.n_rdma, self.n_local = len(relations) * len(arrs), 0
        self.rdma_base = self.local_base = 0

    def _copies(self, ins, outs, send, recv):
        x, y, c = _me()
        cps = []
        for a in range(len(ins)):
            for s, k in enumerate(self.relations):
                peer = ((1 - x) if (k & 4) else x, (1 - y) if (k & 2) else y, (1 - c) if (k & 1) else c)
                sem = self.rdma_base + a * len(self.relations) + s
                cps.append(pltpu.make_async_remote_copy(
                    src_ref=ins[a].at[_block_index(peer)], dst_ref=outs[a].at[s], send_sem=send.at[sem],
                    recv_sem=recv.at[sem], device_id=peer, device_id_type=pl.DeviceIdType.MESH))
        return cps

    def start(self, ins, outs, send, recv, local):
        for cp in self._copies(ins, outs, send, recv):
            cp.start()

    def finish(self, ins, outs, send, recv, local):
        for cp in self._copies(ins, outs, send, recv):
            cp.wait()


class _Plans:
    def __init__(self, plans):
        self.plans = list(plans)
        self.ins = [a for p in plans for a in p.ins]
        self.out_shapes = [s for p in plans for s in p.out_shapes]
        self.n_rdma = self.n_local = 0
        for p in plans:
            p.rdma_base, p.local_base = self.n_rdma, self.n_local
            self.n_rdma, self.n_local = self.n_rdma + p.n_rdma, self.n_local + p.n_local

    def _each(self, ins, outs):
        i = o = 0
        for p in self.plans:
            yield p, ins[i:i + len(p.ins)], outs[o:o + len(p.out_shapes)]
            i, o = i + len(p.ins), o + len(p.out_shapes)

    def start(self, ins, outs, send, recv, local):
        for p, pi, po in self._each(ins, outs):
            p.start(pi, po, send, recv, local)

    def finish(self, ins, outs, send, recv, local):
        for p, pi, po in self._each(ins, outs):
            p.finish(pi, po, send, recv, local)


def _plan_scratch(plan):
    return [pltpu.SemaphoreType.DMA((plan.n_rdma,)), pltpu.SemaphoreType.DMA((plan.n_rdma,)),
            pltpu.SemaphoreType.DMA((max(plan.n_local, 1),))]


def _run_plan(name, plan):
    n = len(plan.ins)

    def body(*refs):
        ins, outs, sems = refs[:n], refs[n:len(refs) - 3], refs[len(refs) - 3:]
        plan.start(ins, outs, *sems)
        plan.finish(ins, outs, *sems)

    any_spec = pl.BlockSpec(memory_space=pl.ANY)
    return pl.pallas_call(body, name=name, out_shape=list(plan.out_shapes), in_specs=[any_spec] * n,
                          out_specs=[any_spec] * len(plan.out_shapes), scratch_shapes=_plan_scratch(plan))(*plan.ins)


def _pcall(name, body, grid, in_specs, ins, out_specs, out_shape, scratch=(), semantics=None, plan=None):
    if plan is None:
        return pl.pallas_call(body, name=name, grid=grid, in_specs=list(in_specs), out_specs=list(out_specs),
                              out_shape=list(out_shape), scratch_shapes=list(scratch),
                              compiler_params=_cparams(semantics))(*ins)
    n_in, n_out, n_scr = len(ins), len(out_shape), len(scratch)
    p_in, p_out = len(plan.ins), len(plan.out_shapes)

    def with_plan(*refs):
        k_in, c_in = refs[:n_in], refs[n_in:n_in + p_in]
        refs = refs[n_in + p_in:]
        k_out, c_out = refs[:n_out], refs[n_out:n_out + p_out]
        refs = refs[n_out + p_out:]
        k_scr, sems = refs[:n_scr], refs[n_scr:]
        ids = [pl.program_id(d) for d in range(len(grid))]
        first = functools.reduce(operator.and_, [i == 0 for i in ids])
        last = functools.reduce(operator.and_, [i == g - 1 for i, g in zip(ids, grid)])

        @pl.when(first)
        def _():
            plan.start(c_in, c_out, *sems)

        body(*k_in, *k_out, *k_scr)

        @pl.when(last)
        def _():
            plan.finish(c_in, c_out, *sems)

    any_spec = pl.BlockSpec(memory_space=pl.ANY)
    res = pl.pallas_call(
        with_plan, name=name, grid=grid, in_specs=list(in_specs) + [any_spec] * p_in,
        out_specs=list(out_specs) + [any_spec] * p_out, out_shape=list(out_shape) + list(plan.out_shapes),
        scratch_shapes=list(scratch) + _plan_scratch(plan),
        compiler_params=_cparams(("arbitrary",) * len(grid)))(*ins, *plan.ins)
    return res[:n_out], res[n_out:]


def _rowwise(name, fn, rows, vecs, row_outs, acc_outs=(), tm=128):
    t = rows[0].shape[0]
    tm = min(tm, t)
    assert t % tm == 0 and tm % SUBLANES == 0
    n_r, n_v, n_o = len(rows), len(vecs), len(row_outs)

    def body(*refs):
        r_in, v_in = refs[:n_r], refs[n_r:n_r + n_v]
        r_out, a_out = refs[n_r + n_v:n_r + n_v + n_o], refs[n_r + n_v + n_o:]
        outs, accs = fn([r[...] for r in r_in], [v[...] for v in v_in])
        for o_ref, o in zip(r_out, outs, strict=True):
            o_ref[...] = o.astype(o_ref.dtype)
        if a_out:
            @pl.when(pl.program_id(0) == 0)
            def _():
                for a_ref in a_out:
                    a_ref[...] = jnp.zeros_like(a_ref)
            for a_ref, a in zip(a_out, accs, strict=True):
                a_ref[...] += a.reshape(tm // SUBLANES, SUBLANES, a.shape[-1]).sum(axis=0)

    in_specs = [pl.BlockSpec((tm, r.shape[1]), lambda i: (i, 0)) for r in rows]
    in_specs += [pl.BlockSpec(v.shape, lambda i: (0, 0)) for v in vecs]
    out_specs = [pl.BlockSpec((tm, w), lambda i: (i, 0)) for w, _ in row_outs]
    out_specs += [pl.BlockSpec((SUBLANES, w), lambda i: (0, 0)) for w in acc_outs]
    out_shape = [SDS((t, w), dt) for w, dt in row_outs] + [SDS((SUBLANES, w), F32) for w in acc_outs]
    return _pcall(name, body, (t // tm,), in_specs, [*rows, *vecs], out_specs, out_shape, semantics=("arbitrary",))


def _tile(n, want):
    if n <= want:
        return n
    for t in range(want // LANES * LANES, 0, -LANES):
        if n % t == 0:
            return t
    raise ValueError(f"no tile for {n}")


_DOT_DIMS = {"nn": (((1,), (0,)), ((), ())), "nt": (((1,), (1,)), ((), ())), "tn": (((0,), (0,)), ((), ()))}


def _matmul(name, a, b, mode, out_dtypes, epilogue=None, extras=(), vecs=(), a_pro=None,
            tm=1024, tn=512, tk=4096, exact=False, b_blocked=False, out_blocked=0, plan=None):
    cs = b.shape[-1] if b_blocked else None
    b2 = (b.shape[1], b.shape[0] * b.shape[2]) if b_blocked else b.shape
    if mode == "tn":
        (k, m), (k2, n) = a.shape, b2
    elif mode == "nt":
        (m, k), (n, k2) = a.shape, b2
    else:
        (m, k), (k2, n) = a.shape, b2
    assert k == k2 and not (b_blocked and mode == "tn")
    tm, tn, tk = _tile(m, tm), _tile(n, tn), _tile(k, tk)
    if b_blocked and mode == "nn":
        tn = _tile(cs, tn)
    if b_blocked and mode == "nt":
        tk = _tile(cs, tk)
    if out_blocked:
        tn = _tile(n // out_blocked, tn)
    nk = k // tk
    n_e, n_v, n_o = len(extras), len(vecs), len(out_dtypes)
    precision = HIGHEST if exact else None

    def body(*refs):
        a_ref, b_ref = refs[:2]
        e_refs, v_refs = refs[2:2 + n_e], refs[2 + n_e:2 + n_e + n_v]
        o_refs = refs[2 + n_e + n_v:2 + n_e + n_v + n_o]

        def product():
            av = a_ref[...]
            if a_pro is not None:
                av = a_pro(av)
            return lax.dot_general(av, b_ref[...], _DOT_DIMS[mode], precision=precision, preferred_element_type=F32)

        def finish(acc):
            res = (acc,) if epilogue is None else epilogue(acc, [e[...] for e in e_refs], [v[...] for v in v_refs])
            for o_ref, r in zip(o_refs, res, strict=True):
                o_ref[...] = r.astype(o_ref.dtype)

        if nk == 1:
            finish(product())
            return
        acc_ref = refs[-1]
        kk = pl.program_id(2)

        @pl.when(kk == 0)
        def _():
            acc_ref[...] = product()

        @pl.when(kk > 0)
        def _():
            acc_ref[...] += product()

        @pl.when(kk == nk - 1)
        def _():
            finish(acc_ref[...])

    if mode == "tn":
        a_spec = pl.BlockSpec((tk, tm), lambda i, j, kk: (kk, i))
    else:
        a_spec = pl.BlockSpec((tm, tk), lambda i, j, kk: (i, kk))
    if b_blocked and mode == "nn":
        per = cs // tn
        b_spec = pl.BlockSpec((None, tk, tn), lambda i, j, kk: (j // per, kk, j % per))
    elif b_blocked:
        per = cs // tk
        b_spec = pl.BlockSpec((None, tn, tk), lambda i, j, kk: (kk // per, j, kk % per))
    elif mode == "nt":
        b_spec = pl.BlockSpec((tn, tk), lambda i, j, kk: (j, kk))
    else:
        b_spec = pl.BlockSpec((tk, tn), lambda i, j, kk: (kk, j))
    tile = pl.BlockSpec((tm, tn), lambda i, j, kk: (i, j))
    if out_blocked:
        per_o = n // out_blocked // tn
        out_spec = pl.BlockSpec((None, tm, tn), lambda i, j, kk: (j // per_o, i, j % per_o))
        out_shape = [SDS((out_blocked, m, n // out_blocked), dt) for dt in out_dtypes]
    else:
        out_spec, out_shape = tile, [SDS((m, n), dt) for dt in out_dtypes]
    in_specs = [a_spec, b_spec] + [tile] * n_e + [pl.BlockSpec((1, tn), lambda i, j, kk: (0, j))] * n_v
    res = _pcall(name, body, (m // tm, n // tn, nk), in_specs, [a, b, *extras, *vecs], [out_spec] * n_o, out_shape,
                 scratch=[pltpu.VMEM((tm, tn), F32)] if nk > 1 else [],
                 semantics=("parallel", "parallel", "arbitrary"), plan=plan)
    if plan is None:
        return res[0] if n_o == 1 else res
    return (res[0][0] if n_o == 1 else res[0]), res[1]


def _rms_fwd(x):
    r = lax.rsqrt(jnp.mean(x * x, axis=-1, keepdims=True) + EPS)
    return x * r, r


def _rms_bwd(dxn, xn, r):
    return r * (dxn - xn * jnp.mean(dxn * xn, axis=-1, keepdims=True))


_INV_SQRT2 = 1.0 / math.sqrt(2.0)
_INV_SQRT2PI = 1.0 / math.sqrt(2.0 * math.pi)


def _gelu(y):
    return 0.5 * y * (1.0 + lax.erf(y * _INV_SQRT2))


def _gelu_grad(y):
    return 0.5 * (1.0 + lax.erf(y * _INV_SQRT2)) + y * (_INV_SQRT2PI * jnp.exp(-0.5 * y * y))


def _sigmoid(z):
    return 1.0 / (1.0 + jnp.exp(-z))


def _adam_math(w, g, m, v):
    m = ADAM_B1 * m + (1.0 - ADAM_B1) * g
    v = ADAM_B2 * v + (1.0 - ADAM_B2) * (g * g)
    m_hat = m / (1.0 - ADAM_B1 ** ADAM_STEP)
    v_hat = v / (1.0 - ADAM_B2 ** ADAM_STEP)
    delta = -ADAM_LR * (m_hat / (jnp.sqrt(v_hat) + ADAM_EPS) + ADAM_WD * w)
    return delta, m, v


def _norm_mod_fwd(name, x, g, scale, shift):
    def fn(rows, vecs):
        (xv,), (gv, sc, sh) = rows, vecs
        xn, _ = _rms_fwd(xv)
        return [(xn * gv) * (1.0 + sc) + sh], []
    return _rowwise(name, fn, [x], [g, scale, shift], [(x.shape[1], BF16)])[0]


def _norm_mod_bwd(name, x, dh, dres, g, scale):
    d = x.shape[1]

    def fn(rows, vecs):
        (xv, dhv, drv), (gv, sc) = rows, vecs
        xn, r = _rms_fwd(xv)
        t = xn * gv
        dt = dhv * (1.0 + sc)
        dx = drv + _rms_bwd(dt * gv, xn, r)
        return [dx], [dhv * t, dhv, dt * xn]
    dx, dscale, dshift, dg = _rowwise(name, fn, [x, dh, dres], [g, scale], [(d, F32)], [d, d, d])
    return dx, dscale.sum(0), dshift.sum(0), dg.sum(0)


def _gate_bwd(name, dx, val, gate):
    d = dx.shape[1]

    def fn(rows, vecs):
        (dxv, vv), (gv,) = rows, vecs
        return [dxv * gv], [dxv * vv.astype(F32)]
    dval, dgate = _rowwise(name, fn, [dx, val], [gate], [(d, BF16)], [d])
    return dval, dgate.sum(0)


def _final_loss(name, x, tgt, g):
    d = x.shape[1]

    def fn(rows, vecs):
        (xv, tv), (gv,) = rows, vecs
        xn, r = _rms_fwd(xv)
        e = xn * gv - tv
        dy = e * (1.0 / d)
        dx = _rms_bwd(dy * gv, xn, r)
        return [dx], [e * e, dy * xn]
    dx, sq, dg = _rowwise(name, fn, [x, tgt], [g], [(d, F32)], [d, d])
    return dx, 0.5 * jnp.sum(sq) / d, dg.sum(0)


def _group_norm_fwd(name, attn, ssm, g_a, g_s):
    def fn(rows, vecs):
        (av, sv), (ga, gs) = rows, vecs
        return [jnp.concatenate([_rms_fwd(av)[0] * ga, _rms_fwd(sv)[0] * gs], axis=1)], []
    return _rowwise(name, fn, [attn, ssm], [g_a, g_s], [(attn.shape[1] + ssm.shape[1], BF16)])[0]


def _group_norm_bwd(name, attn, ssm, dmixed, g_a, g_s):
    da_w, ds_w = attn.shape[1], ssm.shape[1]

    def fn(rows, vecs):
        (av, sv, dm), (ga, gs) = rows, vecs
        an, ra = _rms_fwd(av)
        sn, rs = _rms_fwd(sv)
        dma, dms = dm[:, :da_w], dm[:, da_w:]
        return [_rms_bwd(dma * ga, an, ra), _rms_bwd(dms * gs, sn, rs)], [dma * an, dms * sn]
    dattn, dssm, dga, dgs = _rowwise(name, fn, [attn, ssm, dmixed], [g_a, g_s],
                                     [(da_w, F32), (ds_w, F32)], [da_w, ds_w])
    return dattn, dssm, dga.sum(0), dgs.sum(0)


def _gelu_fwd(name, y):
    def fn(rows, vecs):
        return [_gelu(rows[0])], []
    return _rowwise(name, fn, [y], [], [(y.shape[1], BF16)])[0]


def _glu_bwd(name, dout, y, z):
    d = y.shape[1]

    def fn(rows, vecs):
        dov, yv, zv = rows
        sg = _sigmoid(zv)
        dz = dov * _gelu(yv) * sg * (1.0 - sg)
        return [dz, dov * sg], [dz]
    dz, dyg, db = _rowwise(name, fn, [dout, y, z], [], [(d, BF16), (d, F32)], [d])
    return dz, dyg, db.sum(0)


def _adam_shard(name, parts, w, m, v):
    r, c = w.shape
    n_parts = sum(1 if p.ndim == 2 else p.shape[0] for p in parts)
    row_bytes = 2 * c * (n_parts * parts[0].dtype.itemsize + 7 * 4)
    tr = min(128, r)
    while tr > SUBLANES and tr * row_bytes > VMEM_LIMIT_BYTES // 2:
        tr //= 2
    assert r % tr == 0
    n_p = len(parts)

    def body(*refs):
        p_refs, (w_ref, m_ref, v_ref, g_out, d_out, m_out, v_out) = refs[:n_p], refs[n_p:]
        g = None
        for p_ref in p_refs:
            terms = [p_ref[...]] if len(p_ref.shape) == 2 else [p_ref[j] for j in range(p_ref.shape[0])]
            for t in terms:
                g = t.astype(F32) if g is None else g + t.astype(F32)
        delta, m_new, v_new = _adam_math(w_ref[...], g, m_ref[...], v_ref[...])
        g_out[...], d_out[...], m_out[...], v_out[...] = g, delta, m_new, v_new

    tile = pl.BlockSpec((tr, c), lambda i: (i, 0))
    p_specs = [tile if p.ndim == 2 else pl.BlockSpec((p.shape[0], tr, c), lambda i: (0, i, 0)) for p in parts]
    return _pcall(name, body, (r // tr,), p_specs + [tile] * 3, [*parts, w, m, v], [tile] * 4, [SDS((r, c), F32)] * 4,
                  semantics=("parallel",))


def _ada_update(name, c_act_t, dmod, w, m, v, tr=128):
    r, c = w.shape
    tr = min(tr, r)
    assert r % tr == 0

    def body(c_ref, d_ref, w_ref, m_ref, v_ref, g_out, d_out, m_out, v_out):
        g = jnp.dot(c_ref[...], d_ref[...], precision=HIGHEST, preferred_element_type=F32)
        delta, m_new, v_new = _adam_math(w_ref[...], g, m_ref[...], v_ref[...])
        g_out[...], d_out[...], m_out[...], v_out[...] = g, delta, m_new, v_new

    tile = pl.BlockSpec((tr, c), lambda i: (i, 0))
    in_specs = [pl.BlockSpec((tr, N_DEV), lambda i: (i, 0)), pl.BlockSpec((N_DEV, c), lambda i: (0, 0)), tile, tile, tile]
    return _pcall(name, body, (r // tr,), in_specs, [c_act_t, dmod, w, m, v], [tile] * 4, [SDS((r, c), F32)] * 4,
                  semantics=("parallel",))


def _rotate_half(x):
    w = x.shape[1]
    half = HEAD_DIM // 2
    lane = lax.broadcasted_iota(jnp.int32, x.shape, 1)
    return jnp.where((lane % HEAD_DIM) < half, -pltpu.roll(x, w - half, 1), pltpu.roll(x, half, 1))


def _lane_tile(tab, w):
    return tab[:, :w] if w <= LANES else jnp.tile(tab, (1, w // LANES))


def _rope(x, cos, sin):
    return x * cos + _rotate_half(x) * sin


def _rope_t(dy, cos, sin):
    return dy * cos - _rotate_half(dy) * sin


def _band_mask(n):
    i = lax.broadcasted_iota(jnp.int32, (WINDOW, 2 * WINDOW), 0)
    j = lax.broadcasted_iota(jnp.int32, (WINDOW, 2 * WINDOW), 1)
    return (j > i) & (j <= i + WINDOW) & ((n > 0) | (j >= WINDOW))


def _attn_specs(da, dkv, nb):
    cur = lambda n: (jnp.minimum(n, nb - 1), 0)
    prev = lambda n: (jnp.maximum(jnp.minimum(n, nb - 1) - 1, 0), 0)
    return dict(
        q=pl.BlockSpec((WINDOW, da), cur), kv_cur=pl.BlockSpec((WINDOW, dkv), cur),
        kv_prev=pl.BlockSpec((WINDOW, dkv), prev), tab_cur=pl.BlockSpec((WINDOW, LANES), cur),
        tab_prev=pl.BlockSpec((WINDOW, LANES), prev))


def _attn_fwd(name, q, k, v, cos, sin, sinks, plan=None):
    s, da = q.shape
    dkv = k.shape[1]
    nq, nb = da // HEAD_DIM, s // WINDOW
    scale = HEAD_DIM ** -0.5

    def body(q_ref, kp_ref, kc_ref, vp_ref, vc_ref, cc_ref, sc_ref, cp_ref, sp_ref, sink_ref, o_ref, lse_ref):
        n = pl.program_id(0)
        cc, sc, cp, sp = cc_ref[...], sc_ref[...], cp_ref[...], sp_ref[...]
        qr = _rope(q_ref[...], _lane_tile(cc, da), _lane_tile(sc, da)).astype(BF16)
        kk = jnp.concatenate([_rope(kp_ref[...], _lane_tile(cp, dkv), _lane_tile(sp, dkv)),
                              _rope(kc_ref[...], _lane_tile(cc, dkv), _lane_tile(sc, dkv))], axis=0).astype(BF16)
        vv = jnp.concatenate([vp_ref[...], vc_ref[...]], axis=0).astype(BF16)
        valid = _band_mask(n)
        for hq in range(nq):
            hk = hq // Q_PER_KV
            qs, ks = slice(hq * HEAD_DIM, (hq + 1) * HEAD_DIM), slice(hk * HEAD_DIM, (hk + 1) * HEAD_DIM)
            sco = lax.dot_general(qr[:, qs], kk[:, ks], _DOT_DIMS["nt"], preferred_element_type=F32) * scale
            sco = jnp.where(valid, sco, -1e30)
            sink = sink_ref[0:1, hq:hq + 1]
            mx = jnp.maximum(jnp.max(sco, axis=1, keepdims=True), sink)
            p = jnp.exp(sco - mx)
            den = jnp.sum(p, axis=1, keepdims=True) + jnp.exp(sink - mx)
            o_ref[:, qs] = jnp.dot((p / den).astype(BF16), vv[:, ks], preferred_element_type=F32)
            lse_ref[:, hq:hq + 1] = mx + jnp.log(den)

    sp_ = _attn_specs(da, dkv, nb)
    in_specs = [sp_["q"], sp_["kv_prev"], sp_["kv_cur"], sp_["kv_prev"], sp_["kv_cur"],
                sp_["tab_cur"], sp_["tab_cur"], sp_["tab_prev"], sp_["tab_prev"], pl.BlockSpec((1, nq), lambda n: (0, 0))]
    return _pcall(name, body, (nb,), in_specs, [q, k, k, v, v, cos, sin, cos, sin, sinks],
                  [sp_["q"], pl.BlockSpec((WINDOW, nq), lambda n: (n, 0))], [SDS((s, da), F32), SDS((s, nq), F32)],
                  semantics=("arbitrary",), plan=plan)


def _attn_bwd(name, q, k, v, cos, sin, sinks, out, lse, dout, plan=None):
    s, da = q.shape
    dkv = k.shape[1]
    nq, nb = da // HEAD_DIM, s // WINDOW
    scale = HEAD_DIM ** -0.5

    def body(q_ref, kp_ref, kc_ref, vp_ref, vc_ref, cc_ref, sc_ref, cp_ref, sp_ref, sink_ref, o_ref, lse_ref,
             do_ref, dq_ref, dk_ref, dv_ref, dsink_ref, dk_carry, dv_carry):
        n = pl.program_id(0)
        cp, sp = _lane_tile(cp_ref[...], dkv), _lane_tile(sp_ref[...], dkv)

        @pl.when(n == 0)
        def _():
            dk_carry[...] = jnp.zeros_like(dk_carry)
            dv_carry[...] = jnp.zeros_like(dv_carry)
            dsink_ref[...] = jnp.zeros_like(dsink_ref)

        @pl.when(n < nb)
        def _():
            cc, sc = cc_ref[...], sc_ref[...]
            qr = _rope(q_ref[...], _lane_tile(cc, da), _lane_tile(sc, da)).astype(BF16)
            kk = jnp.concatenate([_rope(kp_ref[...], cp, sp),
                                  _rope(kc_ref[...], _lane_tile(cc, dkv), _lane_tile(sc, dkv))], axis=0).astype(BF16)
            vv = jnp.concatenate([vp_ref[...], vc_ref[...]], axis=0).astype(BF16)
            valid = _band_mask(n)
            dkk = [None] * (nq // Q_PER_KV)
            dvv = [None] * (nq // Q_PER_KV)
            dsink_cols = []
            for hq in range(nq):
                hk = hq // Q_PER_KV
                qs, ks = slice(hq * HEAD_DIM, (hq + 1) * HEAD_DIM), slice(hk * HEAD_DIM, (hk + 1) * HEAD_DIM)
                lse_h = lse_ref[:, hq:hq + 1]
                sco = lax.dot_general(qr[:, qs], kk[:, ks], _DOT_DIMS["nt"], preferred_element_type=F32) * scale
                probs = jnp.where(valid, jnp.exp(sco - lse_h), 0.0)
                do_h = do_ref[:, qs]
                delta = jnp.sum(do_h * o_ref[:, qs], axis=1, keepdims=True)
                dp = lax.dot_general(do_h.astype(BF16), vv[:, ks], _DOT_DIMS["nt"], preferred_element_type=F32)
                ds = (probs * (dp - delta) * scale).astype(BF16)
                dq_ref[:, qs] = jnp.dot(ds, kk[:, ks], preferred_element_type=F32)
                dk_h = lax.dot_general(ds, qr[:, qs], _DOT_DIMS["tn"], preferred_element_type=F32)
                dv_h = lax.dot_general(probs.astype(BF16), do_h.astype(BF16), _DOT_DIMS["tn"],
                                       preferred_element_type=F32)
                dkk[hk] = dk_h if dkk[hk] is None else dkk[hk] + dk_h
                dvv[hk] = dv_h if dvv[hk] is None else dvv[hk] + dv_h
                dsink_cols.append(-jnp.exp(sink_ref[0:1, hq:hq + 1] - lse_h) * delta)
            dq_ref[...] = _rope_t(dq_ref[...], _lane_tile(cc, da), _lane_tile(sc, da))
            for hq in range(nq):
                col = dsink_cols[hq]
                dsink_ref[:, hq:hq + 1] += col.reshape(WINDOW // SUBLANES, SUBLANES, 1).sum(axis=0)
            for hk in range(nq // Q_PER_KV):
                ks = slice(hk * HEAD_DIM, (hk + 1) * HEAD_DIM)
                dk_ref[:, ks] = dk_carry[:, ks] + dkk[hk][:WINDOW]
                dv_ref[:, ks] = dv_carry[:, ks] + dvv[hk][:WINDOW]
                dk_carry[:, ks] = dkk[hk][WINDOW:]
                dv_carry[:, ks] = dvv[hk][WINDOW:]
            dk_ref[...] = _rope_t(dk_ref[...], cp, sp)

        @pl.when(n == nb)
        def _():
            dk_ref[...] = _rope_t(dk_carry[...], cp, sp)
            dv_ref[...] = dv_carry[...]

    sp_ = _attn_specs(da, dkv, nb)
    last_prev = lambda n: (jnp.maximum(n - 1, 0), 0)
    tab_prev = pl.BlockSpec((WINDOW, LANES), last_prev)
    kv_out = pl.BlockSpec((WINDOW, dkv), last_prev)
    lse_spec = pl.BlockSpec((WINDOW, nq), lambda n: (jnp.minimum(n, nb - 1), 0))
    in_specs = [sp_["q"], sp_["kv_prev"], sp_["kv_cur"], sp_["kv_prev"], sp_["kv_cur"],
                sp_["tab_cur"], sp_["tab_cur"], tab_prev, tab_prev,
                pl.BlockSpec((1, nq), lambda n: (0, 0)), sp_["q"], lse_spec, sp_["q"]]
    res = _pcall(name, body, (nb + 1,), in_specs, [q, k, k, v, v, cos, sin, cos, sin, sinks, out, lse, dout],
                 [sp_["q"], kv_out, kv_out, pl.BlockSpec((SUBLANES, nq), lambda n: (0, 0))],
                 [SDS((s, da), F32), SDS((s, dkv), F32), SDS((s, dkv), F32), SDS((SUBLANES, nq), F32)],
                 scratch=[pltpu.VMEM((WINDOW, dkv), F32), pltpu.VMEM((WINDOW, dkv), F32)],
                 semantics=("arbitrary",), plan=plan)
    (dq, dk, dv, dsink), rest = res if plan is not None else (res, None)
    return (dq, dk, dv, dsink.sum(0)), rest


def _ssm_operators(lam_re, lam_im, log_step, b_re, b_im, c_re, c_im, d_skip):
    g, p = lam_re.shape
    h = b_re.shape[-1]
    l = SSM_CHUNK
    step = jnp.exp(log_step)[:, None]
    mag = jnp.exp(lam_re * step)
    ar, ai = mag * jnp.cos(lam_im * step), mag * jnp.sin(lam_im * step)
    den = lam_re * lam_re + lam_im * lam_im
    cr = ((ar - 1.0) * lam_re + ai * lam_im) / den
    ci = (ai * lam_re - (ar - 1.0) * lam_im) / den
    bbr = cr[..., None] * b_re - ci[..., None] * b_im
    bbi = cr[..., None] * b_im + ci[..., None] * b_re
    pr, pi = [jnp.ones_like(ar)], [jnp.zeros_like(ar)]
    for _ in range(l):
        pr, pi = pr + [pr[-1] * ar - pi[-1] * ai], pi + [pr[-1] * ai + pi[-1] * ar]
    pwr, pwi = jnp.stack(pr, axis=1), jnp.stack(pi, axis=1)
    cpr = c_re[:, None] * pwr[:, :, None, :] - c_im[:, None] * pwi[:, :, None, :]
    cpi = c_re[:, None] * pwi[:, :, None, :] + c_im[:, None] * pwr[:, :, None, :]
    kern = (jnp.einsum("gtop,gpi->gtoi", cpr[:, :l], bbr, precision=HIGHEST)
            - jnp.einsum("gtop,gpi->gtoi", cpi[:, :l], bbi, precision=HIGHEST))
    kern = kern.at[:, 0].add(d_skip.reshape(g, h)[:, :, None] * jnp.eye(h, dtype=F32))
    tm = jnp.stack([jnp.pad(kern[:, :l - j], ((0, 0), (j, 0), (0, 0), (0, 0))) for j in range(l)], axis=1)
    tm = tm.transpose(0, 1, 4, 2, 3).reshape(g, l * h, l * h)
    rev_r, rev_i = pwr[:, l - 1::-1][:, :l], pwi[:, l - 1::-1][:, :l]
    er = rev_r[:, :, None, :] * bbr.transpose(0, 2, 1)[:, None] - rev_i[:, :, None, :] * bbi.transpose(0, 2, 1)[:, None]
    ei = rev_r[:, :, None, :] * bbi.transpose(0, 2, 1)[:, None] + rev_i[:, :, None, :] * bbr.transpose(0, 2, 1)[:, None]
    em = jnp.concatenate([er, ei], axis=-1).reshape(g, l * h, 2 * p)
    fr = cpr[:, 1:].transpose(0, 3, 1, 2).reshape(g, p, l * h)
    fi = -cpi[:, 1:].transpose(0, 3, 1, 2).reshape(g, p, l * h)
    fm = jnp.concatenate([fr, fi], axis=1)
    return tm, em, fm, pwr[:, l], pwi[:, l]


def _decay_lanes(alr, ali):
    return jnp.concatenate([alr, alr], axis=1), jnp.concatenate([-ali, ali], axis=1)


def _ssm_fwd(name, u, tm, em, fm, acat, bcat):
    g, nc, lh = u.shape
    p2 = em.shape[-1]
    gb = SSM_GROUPS_PER_STEP
    half = p2 // 2

    def body(u_ref, tm_ref, em_ref, fm_ref, a_ref, b_ref, y_ref, xp_ref, st_ref):
        for i in range(gb):
            st_ref[pl.ds(i, nc, stride=gb), :] = jnp.dot(u_ref[i], em_ref[i], precision=HIGHEST,
                                                         preferred_element_type=F32)
        av, bv = a_ref[...], b_ref[...]

        def step(c, x):
            rows = pl.ds(pl.multiple_of(c * gb, gb), gb)
            loc = st_ref[rows, :]
            st_ref[rows, :] = x
            return av * x + bv * pltpu.roll(x, half, 1) + loc
        lax.fori_loop(0, nc, step, jnp.zeros((gb, p2), F32))
        for i in range(gb):
            xp = st_ref[pl.ds(i, nc, stride=gb), :]
            xp_ref[i] = xp
            y_ref[i] = (jnp.dot(u_ref[i], tm_ref[i], precision=HIGHEST, preferred_element_type=F32)
                        + jnp.dot(xp, fm_ref[i], precision=HIGHEST, preferred_element_type=F32))

    blk = lambda r, c: pl.BlockSpec((gb, r, c), lambda i: (i, 0, 0))
    vec = pl.BlockSpec((gb, p2), lambda i: (i, 0))
    return _pcall(name, body, (g // gb,), [blk(nc, lh), blk(lh, lh), blk(lh, p2), blk(p2, lh), vec, vec],
                  [u, tm, em, fm, acat, bcat], [blk(nc, lh), blk(nc, p2)],
                  [SDS((g, nc, lh), F32), SDS((g, nc, p2), F32)],
                  scratch=[pltpu.VMEM((nc * gb, p2), F32)], semantics=("parallel",))


def _ssm_bwd(name, u, dy, xprev, tm, em, fm, acat, bcat, plan=None):
    g, nc, lh = u.shape
    p2 = em.shape[-1]
    gb = SSM_GROUPS_PER_STEP
    half = p2 // 2

    def body(u_ref, dy_ref, xp_ref, tm_ref, em_ref, fm_ref, a_ref, b_ref,
             du_ref, dtm_ref, dem_ref, dfm_ref, r1_ref, r2_ref, gs_ref, xs_ref):
        for i in range(gb):
            gs_ref[pl.ds(i, nc, stride=gb), :] = lax.dot_general(
                dy_ref[i], fm_ref[i], _DOT_DIMS["nt"], precision=HIGHEST, preferred_element_type=F32)
            xs_ref[pl.ds(i, nc, stride=gb), :] = xp_ref[i]
        av, bv = a_ref[...], b_ref[...]

        def step(t, carry):
            grad, r1, r2 = carry
            c = nc - 1 - t
            rows = pl.ds(pl.multiple_of(c * gb, gb), gb)
            dxp, xp = gs_ref[rows, :], xs_ref[rows, :]
            gs_ref[rows, :] = grad
            r1 = r1 + grad * xp
            r2 = r2 + grad * pltpu.roll(xp, half, 1)
            return dxp + av * grad - bv * pltpu.roll(grad, half, 1), r1, r2
        zero = jnp.zeros((gb, p2), F32)
        _, r1, r2 = lax.fori_loop(0, nc, step, (zero, zero, zero))
        r1_ref[...], r2_ref[...] = r1, r2
        for i in range(gb):
            dxl = gs_ref[pl.ds(i, nc, stride=gb), :]
            du_ref[i] = (lax.dot_general(dy_ref[i], tm_ref[i], _DOT_DIMS["nt"], precision=HIGHEST,
                                         preferred_element_type=F32)
                         + lax.dot_general(dxl, em_ref[i], _DOT_DIMS["nt"], precision=HIGHEST,
                                           preferred_element_type=F32))
            dtm_ref[i] = lax.dot_general(u_ref[i], dy_ref[i], _DOT_DIMS["tn"], precision=HIGHEST,
                                         preferred_element_type=F32)
            dfm_ref[i] = lax.dot_general(xp_ref[i], dy_ref[i], _DOT_DIMS["tn"], precision=HIGHEST,
                                         preferred_element_type=F32)
            dem_ref[i] = lax.dot_general(u_ref[i], dxl, _DOT_DIMS["tn"], precision=HIGHEST,
                                         preferred_element_type=F32)

    blk = lambda r, c: pl.BlockSpec((gb, r, c), lambda i: (i, 0, 0))
    vec = pl.BlockSpec((gb, p2), lambda i: (i, 0))
    res = _pcall(name, body, (g // gb,),
                 [blk(nc, lh), blk(nc, lh), blk(nc, p2), blk(lh, lh), blk(lh, p2), blk(p2, lh), vec, vec],
                 [u, dy, xprev, tm, em, fm, acat, bcat],
                 [blk(nc, lh), blk(lh, lh), blk(lh, p2), blk(p2, lh), vec, vec],
                 [SDS((g, nc, lh), F32), SDS((g, lh, lh), F32), SDS((g, lh, p2), F32), SDS((g, p2, lh), F32),
                  SDS((g, p2), F32), SDS((g, p2), F32)],
                 scratch=[pltpu.VMEM((nc * gb, p2), F32), pltpu.VMEM((nc * gb, p2), F32)],
                 semantics=("parallel",), plan=plan)
    return res if plan is not None else (res, None)


def _to_chunks(u, g, h):
    s = u.shape[0]
    return u.reshape(s // SSM_CHUNK, SSM_CHUNK, g, h).transpose(2, 0, 1, 3).reshape(g, s // SSM_CHUNK, SSM_CHUNK * h)


def _from_chunks(y, g, h):
    nc = y.shape[1]
    return y.reshape(g, nc, SSM_CHUNK, h).transpose(1, 2, 0, 3).reshape(nc * SSM_CHUNK, g * h)


_SMALL = ("b_ada", "norm1_g", "sinks", "ssm_lam_re", "ssm_lam_im", "ssm_log_step", "ssm_b_re", "ssm_b_im",
          "ssm_c_re", "ssm_c_im", "ssm_d", "b_glu", "attn_out_g", "ssm_out_g", "norm2_g", "final_g")
_WEIGHTS = ("w_ada", "b_ada", "norm1_g", "w_in", "sinks", "ssm_lam_re", "ssm_lam_im", "ssm_log_step", "ssm_b_re",
            "ssm_b_im", "ssm_c_re", "ssm_c_im", "ssm_d", "w_glu", "b_glu", "attn_out_g", "ssm_out_g", "w_out",
            "norm2_g", "w_ff1", "w_ff2", "final_g")
_PACK_ALIGN = 128 * LANES


def _pack(parts):
    flat = jnp.concatenate([p.reshape(-1).astype(F32) for p in parts])
    pad = (-flat.shape[0]) % _PACK_ALIGN
    return jnp.pad(flat, (0, pad)).reshape(-1, LANES)


def _cols_to_blocks(w, n_blocks):
    k, n = w.shape
    return w.reshape(k, n_blocks, n // n_blocks).transpose(1, 0, 2)


def _blocks_to_cols(w):
    nb, k, n = w.shape
    return w.transpose(1, 0, 2).reshape(k, nb * n)


def kernel(x, c, w_ada, b_ada, norm1_g, w_in, sinks, ssm_lam_re, ssm_lam_im, ssm_log_step, ssm_b_re, ssm_b_im, ssm_c_re, ssm_c_im, ssm_d, w_glu, b_glu, attn_out_g, ssm_out_g, w_out, norm2_g, w_ff1, w_ff2, final_g, loss_target, m_w_ada, m_b_ada, m_norm1_g, m_w_in, m_sinks, m_ssm_lam_re, m_ssm_lam_im, m_ssm_log_step, m_ssm_b_re, m_ssm_b_im, m_ssm_c_re, m_ssm_c_im, m_ssm_d, m_w_glu, m_b_glu, m_attn_out_g, m_ssm_out_g, m_w_out, m_norm2_g, m_w_ff1, m_w_ff2, m_final_g, v_w_ada, v_b_ada, v_norm1_g, v_w_in, v_sinks, v_ssm_lam_re, v_ssm_lam_im, v_ssm_log_step, v_ssm_b_re, v_ssm_b_im, v_ssm_c_re, v_ssm_c_im, v_ssm_d, v_w_glu, v_b_glu, v_attn_out_g, v_ssm_out_g, v_w_out, v_norm2_g, v_w_ff1, v_w_ff2, v_final_g):
    args = dict(locals())
    weights = {n: args[n] for n in _WEIGHTS}
    mom = {n: args["m_" + n] for n in _WEIGHTS}
    var = {n: args["v_" + n] for n in _WEIGHTS}
    me = 4 * lax.axis_index("x") + 2 * lax.axis_index("y") + lax.axis_index("c")

    _, s, d = x.shape
    xs, tgt = x[0], loss_target[0]
    d_ssm = ssm_d.shape[-1]
    d_attn = d - d_ssm
    nq = d_attn // HEAD_DIM
    d_kv = (nq // Q_PER_KV) * HEAD_DIM
    g_ssm, p_state, h_ssm = ssm_b_re.shape[1:]

    c_all, g_in = _run_plan("gather_c_w_in", _Gather([c, w_in[0].astype(BF16)]))
    c_all = c_all.reshape(N_DEV, d)
    w_in_f = _blocks_to_cols(g_in)
    wq, wkv, wu = w_in_f[:, :d_attn], w_in_f[:, d_attn:d_attn + 2 * d_kv], w_in_f[:, d_attn + 2 * d_kv:]

    n_loc = w_ada.shape[-1]
    b_loc = lax.dynamic_slice_in_dim(b_ada, me * n_loc, n_loc, axis=1)
    silu = lambda t: t * _sigmoid(t)
    mod_part = _matmul("ada_mod", c_all, w_ada[0], "nn", [F32], a_pro=silu, vecs=[b_loc], exact=True,
                       epilogue=lambda acc, e, v: (acc + v[0],), tn=512, tk=d)
    mod_all = _run_plan("gather_mod", _Gather([mod_part]))[0]
    mod = lax.dynamic_index_in_dim(mod_all, me, axis=1, keepdims=False).reshape(N_MOD, 1, d)
    shift1, scale1, gate1, shift2, scale2, gate2 = [mod[i] for i in range(N_MOD)]

    h1 = _norm_mod_fwd("norm1", xs, norm1_g, scale1, shift1)
    q, (g_glu, g_out) = _matmul("proj_q", h1, wq, "nn", [F32],
                                plan=_Gather([w_glu[0].astype(BF16), w_out[0].astype(BF16)]))
    kv = _matmul("proj_kv", h1, wkv, "nn", [F32])
    u = _matmul("proj_u", h1, wu, "nn", [F32])
    k, v = kv[:, :d_kv], kv[:, d_kv:]
    w_glu_f = g_glu.reshape(d_ssm, d_ssm)
    w_out_f = g_out.reshape(d, d)

    half = HEAD_DIM // 2
    inv_freq = ROPE_THETA ** (-jnp.arange(half, dtype=F32) / half)
    ang = jnp.arange(s, dtype=F32)[:, None] * inv_freq[None, :]
    cos_t, sin_t = jnp.tile(jnp.cos(ang), (1, 4)), jnp.tile(jnp.sin(ang), (1, 4))
    (attn, lse), (g_ff1,) = _attn_fwd("attn_fwd", q, k, v, cos_t, sin_t, sinks, plan=_Gather([w_ff1[0].astype(BF16)]))

    ssm_params = (ssm_lam_re[0], ssm_lam_im[0], ssm_log_step[0], ssm_b_re[0], ssm_b_im[0], ssm_c_re[0],
                  ssm_c_im[0], ssm_d[0])
    (tm_op, em_op, fm_op, alr, ali), ssm_vjp = jax.vjp(_ssm_operators, *ssm_params)
    acat, bcat = _decay_lanes(alr, ali)
    u_ch = _to_chunks(u, g_ssm, h_ssm)
    y_ch, x_prev = _ssm_fwd("ssm_fwd", u_ch, tm_op, em_op, fm_op, acat, bcat)
    y_ssm = _from_chunks(y_ch, g_ssm, h_ssm)
    yg = _gelu_fwd("gelu", y_ssm)
    ssm_out, z_glu = _matmul(
        "glu", yg, w_glu_f, "nn", [F32, F32], extras=[y_ssm], vecs=[b_glu],
        epilogue=lambda acc, e, v: (_gelu(e[0]) * _sigmoid(acc + v[0]), acc + v[0]))
    mixed = _group_norm_fwd("group_norm", attn, ssm_out, attn_out_g, ssm_out_g)
    x2, mo = _matmul("out_proj", mixed, w_out_f, "nn", [F32, BF16], extras=[xs], vecs=[gate1],
                     epilogue=lambda acc, e, v: (e[0] + v[0] * acc, acc))

    h2 = _norm_mod_fwd("norm2", x2, norm2_g, scale2, shift2)
    (a_ff, f_ff), (g_ff2,) = _matmul("ff1", h2, g_ff1, "nn", [BF16, BF16], b_blocked=True,
                                     epilogue=lambda acc, e, v: (acc, jnp.square(jnp.maximum(acc, 0.0))),
                                     plan=_Gather([w_ff2[0].astype(BF16)]))
    w_ff2_f = g_ff2.reshape(-1, d)
    x3, ff = _matmul("ff2", f_ff, w_ff2_f, "nn", [F32, BF16], extras=[x2], vecs=[gate2],
                     epilogue=lambda acc, e, v: (e[0] + v[0] * acc, acc))

    dx3, loss_local, d_final_g = _final_loss("final_loss", x3, tgt, final_g.reshape(1, d))
    loss = lax.psum(loss_local, MESH_AXES)

    dff, d_gate2 = _gate_bwd("gate2_bwd", dx3, ff, gate2)
    dw_ff2 = _matmul("ff2_dw", f_ff, dff, "tn", [BF16]).reshape(N_DEV, -1, d)
    da_ff, (r_ff2_a,) = _matmul("ff2_dx", dff, w_ff2_f, "nt", [BF16], extras=[a_ff],
                                epilogue=lambda acc, e, v: (acc * (2.0 * jnp.maximum(e[0].astype(F32), 0.0)),),
                                plan=_Exchange([dw_ff2], RELATIONS_SAME_CORE))
    dw_ff1, (r_ff2_b,) = _matmul("ff1_dw", h2, da_ff, "tn", [BF16], out_blocked=N_DEV,
                                 plan=_Exchange([dw_ff2], RELATIONS_OTHER_CORE))
    dh2, (r_ff1_a,) = _matmul("ff1_dx", da_ff, g_ff1, "nt", [F32], b_blocked=True,
                              plan=_Exchange([dw_ff1], RELATIONS_SAME_CORE))
    dx2, d_scale2, d_shift2, d_norm2_g = _norm_mod_bwd("norm2_bwd", x2, dh2, dx3, norm2_g, scale2)

    dmo, d_gate1 = _gate_bwd("gate1_bwd", dx2, mo, gate1)
    dw_out = _matmul("out_dw", mixed, dmo, "tn", [BF16]).reshape(N_DEV, -1, d)
    dmixed, (r_ff1_b,) = _matmul("out_dx", dmo, w_out_f, "nt", [F32], plan=_Exchange([dw_ff1], (5, 3)))
    dattn, dssm_out, d_attn_g, d_ssm_g = _group_norm_bwd("group_norm_bwd", attn, ssm_out, dmixed, attn_out_g, ssm_out_g)

    dz, dyg_direct, d_b_glu = _glu_bwd("glu_bwd", dssm_out, y_ssm, z_glu)
    dw_glu = _matmul("glu_dw", yg, dz, "tn", [BF16]).reshape(N_DEV, -1, d_ssm)
    dy_ssm = _matmul("glu_dx", dz, w_glu_f, "nt", [F32], extras=[dyg_direct, y_ssm],
                     epilogue=lambda acc, e, v: ((acc + e[0]) * _gelu_grad(e[1]),))
    (du_ch, d_tm, d_em, d_fm, r1, r2), (r_ff1_c, r_out) = _ssm_bwd(
        "ssm_bwd", u_ch, _to_chunks(dy_ssm, g_ssm, h_ssm), x_prev, tm_op, em_op, fm_op, acat, bcat,
        plan=_Plans([_Exchange([dw_ff1], (7,)), _Exchange([dw_out], RELATIONS_ALL)]))
    d_alr = r1[:, :p_state] + r1[:, p_state:]
    d_ali = r2[:, p_state:] - r2[:, :p_state]
    d_ssm_params = ssm_vjp((d_tm, d_em, d_fm, d_alr, d_ali))
    du = _from_chunks(du_ch, g_ssm, h_ssm)

    (dq, dk, dv, d_sinks), (r_glu,) = _attn_bwd("attn_bwd", q, k, v, cos_t, sin_t, sinks, attn, lse, dattn,
                                                plan=_Exchange([dw_glu], RELATIONS_ALL))
    dproj = jnp.concatenate([dq, dk, dv, du], axis=1).astype(BF16)
    dw_in = _cols_to_blocks(_matmul("in_dw", h1, dproj, "tn", [BF16]), N_DEV)
    dh1, (r_in,) = _matmul("in_dx", dproj, w_in_f, "nt", [F32], plan=_Exchange([dw_in], RELATIONS_ALL))
    grad_x, d_scale1, d_shift1, d_norm1_g = _norm_mod_bwd("norm1_bwd", xs, dh1, dx2, norm1_g, scale1)

    d_mod = jnp.concatenate([d_shift1, d_scale1, d_gate1, d_shift2, d_scale2, d_gate2])
    small_g = dict(zip(("ssm_lam_re", "ssm_lam_im", "ssm_log_step", "ssm_b_re", "ssm_b_im", "ssm_c_re", "ssm_c_im",
                        "ssm_d"), d_ssm_params, strict=True))
    small_g.update(b_ada=d_mod, norm1_g=d_norm1_g, sinks=d_sinks, b_glu=d_b_glu, attn_out_g=d_attn_g,
                   ssm_out_g=d_ssm_g, norm2_g=d_norm2_g, final_g=d_final_g)
    small_parts = _run_plan("gather_small_grads", _Gather([_pack([small_g[n] for n in _SMALL])]))[0]
    small = _adam_shard("adam_small", [small_parts], _pack([weights[n] for n in _SMALL]),
                        _pack([mom[n] for n in _SMALL]), _pack([var[n] for n in _SMALL]))
    out = {}
    off = 0
    for n in _SMALL:
        size = weights[n].size
        out[n] = [t.reshape(-1)[off:off + size].reshape(weights[n].shape) for t in small]
        off += size

    dmod_all = small_parts.reshape(N_DEV, -1)[:, :N_MOD * d]
    dmod_loc = lax.dynamic_slice_in_dim(dmod_all, me * n_loc, n_loc, axis=1)
    c_act_t = silu(c_all).T
    out["w_ada"] = [t[None] for t in _ada_update("adam_w_ada", c_act_t, dmod_loc, w_ada[0], m_w_ada[0], v_w_ada[0])]

    mine = lambda blocks: lax.dynamic_index_in_dim(blocks, me, axis=0, keepdims=False)
    received = dict(w_in=[mine(dw_in), r_in], w_glu=[mine(dw_glu), r_glu], w_out=[mine(dw_out), r_out],
                    w_ff1=[mine(dw_ff1), r_ff1_a, r_ff1_b, r_ff1_c], w_ff2=[mine(dw_ff2), r_ff2_a, r_ff2_b])
    for n, parts in received.items():
        out[n] = [t[None] for t in _adam_shard("adam_" + n, parts, weights[n][0], mom[n][0], var[n][0])]

    return (loss, grad_x[None], *[out[n][0] for n in _WEIGHTS], *[out[n][1] for n in _WEIGHTS],
            *[out[n][2] for n in _WEIGHTS], *[out[n][3] for n in _WEIGHTS])
```

```python
import functools
import math
import operator

import jax
import jax.numpy as jnp
from jax import lax
from jax.experimental import pallas as pl
from jax.experimental.pallas import tpu as pltpu

F32, BF16 = jnp.float32, jnp.bfloat16
SDS = jax.ShapeDtypeStruct
MESH_AXES = ("x", "y", "c")
N_DEV = 8
VMEM_LIMIT_BYTES = 56 * 1024 * 1024
SUBLANES, LANES = 8, 128

HEAD_DIM = 64
Q_PER_KV = 8
WINDOW = 128
ROPE_THETA = 10000.0
EPS = 1e-6
N_MOD = 6
SSM_CHUNK = 16
SSM_GROUPS_PER_STEP = 8

ADAM_LR, ADAM_B1, ADAM_B2, ADAM_EPS, ADAM_WD, ADAM_STEP = 0.001, 0.9, 0.999, 1e-08, 0.01, 10
HIGHEST = lax.Precision.HIGHEST

RELATIONS_ALL = (1, 4, 2, 6, 5, 3, 7)
RELATIONS_SAME_CORE = (1, 4, 2, 6)
RELATIONS_OTHER_CORE = (5, 3, 7)


def _cparams(sem):
    return pltpu.CompilerParams(dimension_semantics=sem, vmem_limit_bytes=VMEM_LIMIT_BYTES)


def _block_index(p):
    return 4 * p[0] + 2 * p[1] + p[2]


def _me():
    return lax.axis_index("x"), lax.axis_index("y"), lax.axis_index("c")


class _Gather:
    def __init__(self, arrs):
        self.ins = list(arrs)
        self.out_shapes = [SDS((N_DEV,) + a.shape, a.dtype) for a in arrs]
        self.n_rdma, self.n_local = 7 * len(arrs), len(arrs)
        self.rdma_base = self.local_base = 0

    def _copy(self, ins, outs, send, recv, a, k, block, to, from_input=False):
        dst = outs[a].at[_block_index(block)]
        sem = self.rdma_base + a * 7 + k
        return pltpu.make_async_remote_copy(
            src_ref=ins[a] if from_input else dst, dst_ref=dst, send_sem=send.at[sem], recv_sem=recv.at[sem],
            device_id=to, device_id_type=pl.DeviceIdType.MESH)

    def _first(self, ins, outs, send, recv, a):
        x, y, c = _me()
        chips = [(1 - x, y), (x, 1 - y), (1 - x, 1 - y)]
        cps = [self._copy(ins, outs, send, recv, a, 0, (x, y, c), (x, y, 1 - c), True)]
        return cps + [self._copy(ins, outs, send, recv, a, 1 + j, (x, y, c), (*chip, c), True)
                      for j, chip in enumerate(chips)]

    def _mine(self, ins, outs, local, a):
        return pltpu.make_async_copy(ins[a], outs[a].at[_block_index(_me())], local.at[self.local_base + a])

    def start(self, ins, outs, send, recv, local):
        for a in range(len(ins)):
            self._mine(ins, outs, local, a).start()
            for cp in self._first(ins, outs, send, recv, a):
                cp.start()

    def finish(self, ins, outs, send, recv, local):
        x, y, c = _me()
        me, sibling = (x, y, c), (x, y, 1 - c)
        chips = [(1 - x, y), (x, 1 - y), (1 - x, 1 - y)]
        forwards = []
        for a in range(len(ins)):
            for j, chip in enumerate(chips):
                self._copy(ins, outs, send, recv, a, 1 + j, (*chip, c), me).wait_recv()
                fwd = self._copy(ins, outs, send, recv, a, 4 + j, (*chip, c), sibling)
                fwd.start()
                forwards.append(fwd)
        for a in range(len(ins)):
            self._copy(ins, outs, send, recv, a, 0, sibling, me).wait_recv()
            for j, chip in enumerate(chips):
                self._copy(ins, outs, send, recv, a, 4 + j, (*chip, 1 - c), me).wait_recv()
            for cp in self._first(ins, outs, send, recv, a):
                cp.wait_send()
            self._mine(ins, outs, local, a).wait()
        for fwd in forwards:
            fwd.wait_send()


class _Exchange:
    def __init__(self, arrs, relations):
        self.ins, self.relations = list(arrs), tuple(relations)
        self.out_shapes = [SDS((len(relations),) + a.shape[1:], a.dtype) for a in arrs]
        self.n_rdma, self.n_local = len(relations) * len(arrs), 0
        self.rdma_base = self.local_base = 0

    def _copies(self, ins, outs, send, recv):
        x, y, c = _me()
        cps = []
        for a in range(len(ins)):
            for s, k in enumerate(self.relations):
                peer = ((1 - x) if (k & 4) else x, (1 - y) if (k & 2) else y, (1 - c) if (k & 1) else c)
                sem = self.rdma_base + a * len(self.relations) + s
                cps.append(pltpu.make_async_remote_copy(
                    src_ref=ins[a].at[_block_index(peer)], dst_ref=outs[a].at[s], send_sem=send.at[sem],
                    recv_sem=recv.at[sem], device_id=peer, device_id_type=pl.DeviceIdType.MESH))
        return cps

    def start(self, ins, outs, send, recv, local):
        for cp in self._copies(ins, outs, send, recv):
            cp.start()

    def finish(self, ins, outs, send, recv, local):
        for cp in self._copies(ins, outs, send, recv):
            cp.wait()


class _Plans:
    def __init__(self, plans):
        self.plans = list(plans)
        self.ins = [a for p in plans for a in p.ins]
        self.out_shapes = [s for p in plans for s in p.out_shapes]
        self.n_rdma = self.n_local = 0
        for p in plans:
            p.rdma_base, p.local_base = self.n_rdma, self.n_local
            self.n_rdma, self.n_local = self.n_rdma + p.n_rdma, self.n_local + p.n_local

    def _each(self, ins, outs):
        i = o = 0
        for p in self.plans:
            yield p, ins[i:i + len(p.ins)], outs[o:o + len(p.out_shapes)]
            i, o = i + len(p.ins), o + len(p.out_shapes)

    def start(self, ins, outs, send, recv, local):
        for p, pi, po in self._each(ins, outs):
            p.start(pi, po, send, recv, local)

    def finish(self, ins, outs, send, recv, local):
        for p, pi, po in self._each(ins, outs):
            p.finish(pi, po, send, recv, local)


def _plan_scratch(plan):
    return [pltpu.SemaphoreType.DMA((plan.n_rdma,)), pltpu.SemaphoreType.DMA((plan.n_rdma,)),
            pltpu.SemaphoreType.DMA((max(plan.n_local, 1),))]


def _run_plan(name, plan):
    n = len(plan.ins)

    def body(*refs):
        ins, outs, sems = refs[:n], refs[n:len(refs) - 3], refs[len(refs) - 3:]
        plan.start(ins, outs, *sems)
        plan.finish(ins, outs, *sems)

    any_spec = pl.BlockSpec(memory_space=pl.ANY)
    return pl.pallas_call(body, name=name, out_shape=list(plan.out_shapes), in_specs=[any_spec] * n,
                          out_specs=[any_spec] * len(plan.out_shapes), scratch_shapes=_plan_scratch(plan))(*plan.ins)


def _pcall(name, body, grid, in_specs, ins, out_specs, out_shape, scratch=(), semantics=None, plan=None):
    if plan is None:
        return pl.pallas_call(body, name=name, grid=grid, in_specs=list(in_specs), out_specs=list(out_specs),
                              out_shape=list(out_shape), scratch_shapes=list(scratch),
                              compiler_params=_cparams(semantics))(*ins)
    n_in, n_out, n_scr = len(ins), len(out_shape), len(scratch)
    p_in, p_out = len(plan.ins), len(plan.out_shapes)

    def with_plan(*refs):
        k_in, c_in = refs[:n_in], refs[n_in:n_in + p_in]
        refs = refs[n_in + p_in:]
        k_out, c_out = refs[:n_out], refs[n_out:n_out + p_out]
        refs = refs[n_out + p_out:]
        k_scr, sems = refs[:n_scr], refs[n_scr:]
        ids = [pl.program_id(d) for d in range(len(grid))]
        first = functools.reduce(operator.and_, [i == 0 for i in ids])
        last = functools.reduce(operator.and_, [i == g - 1 for i, g in zip(ids, grid)])

        @pl.when(first)
        def _():
            plan.start(c_in, c_out, *sems)

        body(*k_in, *k_out, *k_scr)

        @pl.when(last)
        def _():
            plan.finish(c_in, c_out, *sems)

    any_spec = pl.BlockSpec(memory_space=pl.ANY)
    res = pl.pallas_call(
        with_plan, name=name, grid=grid, in_specs=list(in_specs) + [any_spec] * p_in,
        out_specs=list(out_specs) + [any_spec] * p_out, out_shape=list(out_shape) + list(plan.out_shapes),
        scratch_shapes=list(scratch) + _plan_scratch(plan),
        compiler_params=_cparams(("arbitrary",) * len(grid)))(*ins, *plan.ins)
    return res[:n_out], res[n_out:]


def _rowwise(name, fn, rows, vecs, row_outs, acc_outs=(), tm=128, plan=None):
    t = rows[0].shape[0]
    tm = min(tm, t)
    assert t % tm == 0 and tm % SUBLANES == 0
    n_r, n_v, n_o = len(rows), len(vecs), len(row_outs)

    def body(*refs):
        r_in, v_in = refs[:n_r], refs[n_r:n_r + n_v]
        r_out, a_out = refs[n_r + n_v:n_r + n_v + n_o], refs[n_r + n_v + n_o:]
        outs, accs = fn([r[...] for r in r_in], [v[...] for v in v_in])
        for o_ref, o in zip(r_out, outs, strict=True):
            o_ref[...] = o.astype(o_ref.dtype)
        if a_out:
            @pl.when(pl.program_id(0) == 0)
            def _():
                for a_ref in a_out:
                    a_ref[...] = jnp.zeros_like(a_ref)
            for a_ref, a in zip(a_out, accs, strict=True):
                a_ref[...] += a.reshape(tm // SUBLANES, SUBLANES, a.shape[-1]).sum(axis=0)

    in_specs = [pl.BlockSpec((tm, r.shape[1]), lambda i: (i, 0)) for r in rows]
    in_specs += [pl.BlockSpec(v.shape, lambda i: (0, 0)) for v in vecs]
    out_specs = [pl.BlockSpec((tm, w), lambda i: (i, 0)) for w, _ in row_outs]
    out_specs += [pl.BlockSpec((SUBLANES, w), lambda i: (0, 0)) for w in acc_outs]
    out_shape = [SDS((t, w), dt) for w, dt in row_outs] + [SDS((SUBLANES, w), F32) for w in acc_outs]
    return _pcall(name, body, (t // tm,), in_specs, [*rows, *vecs], out_specs, out_shape, semantics=("arbitrary",),
                  plan=plan)


def _tile(n, want):
    if n <= want:
        return n
    for t in range(want // LANES * LANES, 0, -LANES):
        if n % t == 0:
            return t
    raise ValueError(f"no tile for {n}")


_DOT_DIMS = {"nn": (((1,), (0,)), ((), ())), "nt": (((1,), (1,)), ((), ())), "tn": (((0,), (0,)), ((), ()))}


def _matmul(name, a, b, mode, out_dtypes, epilogue=None, extras=(), vecs=(), a_pro=None,
            tm=1024, tn=512, tk=4096, exact=False, b_blocked=False, out_blocked=0, b_rows=None, plan=None):
    cs = b.shape[-1] if b_blocked else None
    b2 = (b.shape[1], b.shape[0] * b.shape[2]) if b_blocked else b.shape
    if mode == "tn":
        (k, m), (k2, n) = a.shape, b2
    elif mode == "nt":
        (m, k), (n, k2) = a.shape, b2
    else:
        (m, k), (k2, n) = a.shape, b2
    assert k == k2 and not (b_blocked and mode == "tn")
    row0 = 0
    if b_rows is not None:
        assert mode == "nt" and not b_blocked
        row0, n = b_rows
        tn = _tile(math.gcd(n, row0) if row0 else n, tn)
    tm, tn, tk = _tile(m, tm), _tile(n, tn), _tile(k, tk)
    if b_blocked and mode == "nn":
        tn = _tile(cs, tn)
    if b_blocked and mode == "nt":
        tk = _tile(cs, tk)
    if out_blocked:
        tn = _tile(n // out_blocked, tn)
    nk = k // tk
    n_e, n_v, n_o = len(extras), len(vecs), len(out_dtypes)
    precision = HIGHEST if exact else None

    def body(*refs):
        a_ref, b_ref = refs[:2]
        e_refs, v_refs = refs[2:2 + n_e], refs[2 + n_e:2 + n_e + n_v]
        o_refs = refs[2 + n_e + n_v:2 + n_e + n_v + n_o]

        def product():
            av = a_ref[...]
            if a_pro is not None:
                av = a_pro(av)
            return lax.dot_general(av, b_ref[...], _DOT_DIMS[mode], precision=precision, preferred_element_type=F32)

        def finish(acc):
            res = (acc,) if epilogue is None else epilogue(acc, [e[...] for e in e_refs], [v[...] for v in v_refs])
            for o_ref, r in zip(o_refs, res, strict=True):
                o_ref[...] = r.astype(o_ref.dtype)

        if nk == 1:
            finish(product())
            return
        acc_ref = refs[-1]
        kk = pl.program_id(2)

        @pl.when(kk == 0)
        def _():
            acc_ref[...] = product()

        @pl.when(kk > 0)
        def _():
            acc_ref[...] += product()

        @pl.when(kk == nk - 1)
        def _():
            finish(acc_ref[...])

    if mode == "tn":
        a_spec = pl.BlockSpec((tk, tm), lambda i, j, kk: (kk, i))
    else:
        a_spec = pl.BlockSpec((tm, tk), lambda i, j, kk: (i, kk))
    if b_blocked and mode == "nn":
        per = cs // tn
        b_spec = pl.BlockSpec((None, tk, tn), lambda i, j, kk: (j // per, kk, j % per))
    elif b_blocked:
        per = cs // tk
        b_spec = pl.BlockSpec((None, tn, tk), lambda i, j, kk: (kk // per, j, kk % per))
    elif mode == "nt":
        assert row0 % tn == 0
        b_spec = pl.BlockSpec((tn, tk), lambda i, j, kk: (j + row0 // tn, kk))
    else:
        b_spec = pl.BlockSpec((tk, tn), lambda i, j, kk: (kk, j))
    tile = pl.BlockSpec((tm, tn), lambda i, j, kk: (i, j))
    if out_blocked:
        per_o = n // out_blocked // tn
        out_spec = pl.BlockSpec((None, tm, tn), lambda i, j, kk: (j // per_o, i, j % per_o))
        out_shape = [SDS((out_blocked, m, n // out_blocked), dt) for dt in out_dtypes]
    else:
        out_spec, out_shape = tile, [SDS((m, n), dt) for dt in out_dtypes]
    in_specs = [a_spec, b_spec] + [tile] * n_e + [pl.BlockSpec((1, tn), lambda i, j, kk: (0, j))] * n_v
    res = _pcall(name, body, (m // tm, n // tn, nk), in_specs, [a, b, *extras, *vecs], [out_spec] * n_o, out_shape,
                 scratch=[pltpu.VMEM((tm, tn), F32)] if nk > 1 else [],
                 semantics=("parallel", "parallel", "arbitrary"), plan=plan)
    if plan is None:
        return res[0] if n_o == 1 else res
    return (res[0][0] if n_o == 1 else res[0]), res[1]


def _rms_fwd(x):
    r = lax.rsqrt(jnp.mean(x * x, axis=-1, keepdims=True) + EPS)
    return x * r, r


def _rms_bwd(dxn, xn, r):
    return r * (dxn - xn * jnp.mean(dxn * xn, axis=-1, keepdims=True))


_INV_SQRT2 = 1.0 / math.sqrt(2.0)
_INV_SQRT2PI = 1.0 / math.sqrt(2.0 * math.pi)


def _gelu(y):
    return 0.5 * y * (1.0 + lax.erf(y * _INV_SQRT2))


def _gelu_grad(y):
    return 0.5 * (1.0 + lax.erf(y * _INV_SQRT2)) + y * (_INV_SQRT2PI * jnp.exp(-0.5 * y * y))


def _sigmoid(z):
    return 1.0 / (1.0 + jnp.exp(-z))


def _adam_math(w, g, m, v):
    m = ADAM_B1 * m + (1.0 - ADAM_B1) * g
    v = ADAM_B2 * v + (1.0 - ADAM_B2) * (g * g)
    m_hat = m / (1.0 - ADAM_B1 ** ADAM_STEP)
    v_hat = v / (1.0 - ADAM_B2 ** ADAM_STEP)
    delta = -ADAM_LR * (m_hat / (jnp.sqrt(v_hat) + ADAM_EPS) + ADAM_WD * w)
    return delta, m, v


def _norm_mod_fwd(name, x, g, scale, shift):
    def fn(rows, vecs):
        (xv,), (gv, sc, sh) = rows, vecs
        xn, _ = _rms_fwd(xv)
        return [(xn * gv) * (1.0 + sc) + sh], []
    return _rowwise(name, fn, [x], [g, scale, shift], [(x.shape[1], BF16)])[0]


def _norm_mod_bwd(name, x, dh, dres, g, scale, plan=None):
    d = x.shape[1]

    def fn(rows, vecs):
        (xv, dhv, drv), (gv, sc) = rows, vecs
        xn, r = _rms_fwd(xv)
        t = xn * gv
        dt = dhv * (1.0 + sc)
        dx = drv + _rms_bwd(dt * gv, xn, r)
        return [dx], [dhv * t, dhv, dt * xn]
    res = _rowwise(name, fn, [x, dh, dres], [g, scale], [(d, F32)], [d, d, d], plan=plan)
    (dx, dscale, dshift, dg), rest = res if plan is not None else (res, None)
    return (dx, dscale.sum(0), dshift.sum(0), dg.sum(0)), rest


def _gate_bwd(name, dx, val, gate):
    d = dx.shape[1]

    def fn(rows, vecs):
        (dxv, vv), (gv,) = rows, vecs
        return [dxv * gv], [dxv * vv.astype(F32)]
    dval, dgate = _rowwise(name, fn, [dx, val], [gate], [(d, BF16)], [d])
    return dval, dgate.sum(0)


def _final_loss(name, x, tgt, g):
    d = x.shape[1]

    def fn(rows, vecs):
        (xv, tv), (gv,) = rows, vecs
        xn, r = _rms_fwd(xv)
        e = xn * gv - tv
        dy = e * (1.0 / d)
        dx = _rms_bwd(dy * gv, xn, r)
        return [dx], [e * e, dy * xn]
    dx, sq, dg = _rowwise(name, fn, [x, tgt], [g], [(d, F32)], [d, d])
    return dx, 0.5 * jnp.sum(sq) / d, dg.sum(0)


def _group_norm_fwd(name, attn, ssm, g_a, g_s):
    def fn(rows, vecs):
        (av, sv), (ga, gs) = rows, vecs
        return [jnp.concatenate([_rms_fwd(av)[0] * ga, _rms_fwd(sv)[0] * gs], axis=1)], []
    return _rowwise(name, fn, [attn, ssm], [g_a, g_s], [(attn.shape[1] + ssm.shape[1], BF16)])[0]


def _group_norm_bwd(name, attn, ssm, dmixed, g_a, g_s):
    da_w, ds_w = attn.shape[1], ssm.shape[1]

    def fn(rows, vecs):
        (av, sv, dm), (ga, gs) = rows, vecs
        an, ra = _rms_fwd(av)
        sn, rs = _rms_fwd(sv)
        dma, dms = dm[:, :da_w], dm[:, da_w:]
        return [_rms_bwd(dma * ga, an, ra), _rms_bwd(dms * gs, sn, rs)], [dma * an, dms * sn]
    dattn, dssm, dga, dgs = _rowwise(name, fn, [attn, ssm, dmixed], [g_a, g_s],
                                     [(da_w, F32), (ds_w, F32)], [da_w, ds_w])
    return dattn, dssm, dga.sum(0), dgs.sum(0)


def _gelu_fwd(name, y):
    def fn(rows, vecs):
        return [_gelu(rows[0])], []
    return _rowwise(name, fn, [y], [], [(y.shape[1], BF16)])[0]


def _glu_bwd(name, dout, y, z):
    d = y.shape[1]

    def fn(rows, vecs):
        dov, yv, zv = rows
        sg = _sigmoid(zv)
        dz = dov * _gelu(yv) * sg * (1.0 - sg)
        return [dz, dov * sg], [dz]
    dz, dyg, db = _rowwise(name, fn, [dout, y, z], [], [(d, BF16), (d, F32)], [d])
    return dz, dyg, db.sum(0)


def _adam_shard(name, parts, w, m, v):
    r, c = w.shape
    n_parts = sum(1 if p.ndim == 2 else p.shape[0] for p in parts)
    row_bytes = 2 * c * (n_parts * parts[0].dtype.itemsize + 7 * 4)
    tr = min(128, r)
    while tr > SUBLANES and tr * row_bytes > VMEM_LIMIT_BYTES // 2:
        tr //= 2
    assert r % tr == 0
    n_p = len(parts)

    def body(*refs):
        p_refs, (w_ref, m_ref, v_ref, g_out, d_out, m_out, v_out) = refs[:n_p], refs[n_p:]
        g = None
        for p_ref in p_refs:
            terms = [p_ref[...]] if len(p_ref.shape) == 2 else [p_ref[j] for j in range(p_ref.shape[0])]
            for t in terms:
                g = t.astype(F32) if g is None else g + t.astype(F32)
        delta, m_new, v_new = _adam_math(w_ref[...], g, m_ref[...], v_ref[...])
        g_out[...], d_out[...], m_out[...], v_out[...] = g, delta, m_new, v_new

    tile = pl.BlockSpec((tr, c), lambda i: (i, 0))
    p_specs = [tile if p.ndim == 2 else pl.BlockSpec((p.shape[0], tr, c), lambda i: (0, i, 0)) for p in parts]
    return _pcall(name, body, (r // tr,), p_specs + [tile] * 3, [*parts, w, m, v], [tile] * 4, [SDS((r, c), F32)] * 4,
                  semantics=("parallel",))


def _ada_update(name, c_act_t, dmod, w, m, v, tr=128, plan=None):
    r, c = w.shape
    tr = min(tr, r)
    assert r % tr == 0

    def body(c_ref, d_ref, w_ref, m_ref, v_ref, g_out, d_out, m_out, v_out):
        g = jnp.dot(c_ref[...], d_ref[...], precision=HIGHEST, preferred_element_type=F32)
        delta, m_new, v_new = _adam_math(w_ref[...], g, m_ref[...], v_ref[...])
        g_out[...], d_out[...], m_out[...], v_out[...] = g, delta, m_new, v_new

    tile = pl.BlockSpec((tr, c), lambda i: (i, 0))
    in_specs = [pl.BlockSpec((tr, N_DEV), lambda i: (i, 0)), pl.BlockSpec((N_DEV, c), lambda i: (0, 0)), tile, tile, tile]
    return _pcall(name, body, (r // tr,), in_specs, [c_act_t, dmod, w, m, v], [tile] * 4, [SDS((r, c), F32)] * 4,
                  semantics=("parallel",), plan=plan)


def _rotate_half(x):
    w = x.shape[1]
    half = HEAD_DIM // 2
    lane = lax.broadcasted_iota(jnp.int32, x.shape, 1)
    return jnp.where((lane % HEAD_DIM) < half, -pltpu.roll(x, w - half, 1), pltpu.roll(x, half, 1))


def _lane_tile(tab, w):
    return tab[:, :w] if w <= LANES else jnp.tile(tab, (1, w // LANES))


def _rope(x, cos, sin):
    return x * cos + _rotate_half(x) * sin


def _rope_t(dy, cos, sin):
    return dy * cos - _rotate_half(dy) * sin


def _band_mask(n):
    shape = (Q_PER_KV * WINDOW, 2 * WINDOW)
    i = lax.broadcasted_iota(jnp.int32, shape, 0) & (WINDOW - 1)
    j = lax.broadcasted_iota(jnp.int32, shape, 1)
    return (j > i) & (j <= i + WINDOW) & ((n > 0) | (j >= WINDOW))


def _stack_heads(x, hk):
    first = hk * Q_PER_KV
    return jnp.concatenate([x[:, (first + g) * HEAD_DIM:(first + g + 1) * HEAD_DIM] for g in range(Q_PER_KV)], axis=0)


def _stack_cols(ref, hk):
    first = hk * Q_PER_KV
    return jnp.concatenate([ref[:, first + g:first + g + 1] for g in range(Q_PER_KV)], axis=0)


def _stack_sinks(sink_ref, hk):
    first = hk * Q_PER_KV
    return jnp.concatenate([jnp.broadcast_to(sink_ref[0:1, first + g:first + g + 1], (WINDOW, 1))
                            for g in range(Q_PER_KV)], axis=0)


def _attn_specs(da, dkv, nb):
    cur = lambda n: (jnp.minimum(n, nb - 1), 0)
    prev = lambda n: (jnp.maximum(jnp.minimum(n, nb - 1) - 1, 0), 0)
    return dict(
        q=pl.BlockSpec((WINDOW, da), cur), kv_cur=pl.BlockSpec((WINDOW, dkv), cur),
        kv_prev=pl.BlockSpec((WINDOW, dkv), prev), tab_cur=pl.BlockSpec((WINDOW, LANES), cur),
        tab_prev=pl.BlockSpec((WINDOW, LANES), prev))


def _attn_fwd(name, q, k, v, cos, sin, sinks, plan=None):
    s, da = q.shape
    dkv = k.shape[1]
    nq, nb = da // HEAD_DIM, s // WINDOW
    scale = HEAD_DIM ** -0.5

    def body(q_ref, kp_ref, kc_ref, vp_ref, vc_ref, cc_ref, sc_ref, cp_ref, sp_ref, sink_ref, o_ref, lse_ref):
        n = pl.program_id(0)
        cc, sc, cp, sp = cc_ref[...], sc_ref[...], cp_ref[...], sp_ref[...]
        qr = _rope(q_ref[...], _lane_tile(cc, da), _lane_tile(sc, da)).astype(BF16)
        kk = jnp.concatenate([_rope(kp_ref[...], _lane_tile(cp, dkv), _lane_tile(sp, dkv)),
                              _rope(kc_ref[...], _lane_tile(cc, dkv), _lane_tile(sc, dkv))], axis=0).astype(BF16)
        vv = jnp.concatenate([vp_ref[...], vc_ref[...]], axis=0).astype(BF16)
        valid = _band_mask(n)
        for hk in range(nq // Q_PER_KV):
            ks = slice(hk * HEAD_DIM, (hk + 1) * HEAD_DIM)
            sco = lax.dot_general(_stack_heads(qr, hk), kk[:, ks], _DOT_DIMS["nt"], preferred_element_type=F32) * scale
            sco = jnp.where(valid, sco, -1e30)
            sink = _stack_sinks(sink_ref, hk)
            mx = jnp.maximum(jnp.max(sco, axis=1, keepdims=True), sink)
            p = jnp.exp(sco - mx)
            den = jnp.sum(p, axis=1, keepdims=True) + jnp.exp(sink - mx)
            o8 = jnp.dot((p / den).astype(BF16), vv[:, ks], preferred_element_type=F32)
            lse8 = mx + jnp.log(den)
            for g in range(Q_PER_KV):
                hq, rows = hk * Q_PER_KV + g, slice(g * WINDOW, (g + 1) * WINDOW)
                o_ref[:, hq * HEAD_DIM:(hq + 1) * HEAD_DIM] = o8[rows]
                lse_ref[:, hq:hq + 1] = lse8[rows]

    sp_ = _attn_specs(da, dkv, nb)
    in_specs = [sp_["q"], sp_["kv_prev"], sp_["kv_cur"], sp_["kv_prev"], sp_["kv_cur"],
                sp_["tab_cur"], sp_["tab_cur"], sp_["tab_prev"], sp_["tab_prev"], pl.BlockSpec((1, nq), lambda n: (0, 0))]
    return _pcall(name, body, (nb,), in_specs, [q, k, k, v, v, cos, sin, cos, sin, sinks],
                  [sp_["q"], pl.BlockSpec((WINDOW, nq), lambda n: (n, 0))], [SDS((s, da), F32), SDS((s, nq), F32)],
                  semantics=("arbitrary",), plan=plan)


def _attn_bwd(name, q, k, v, cos, sin, sinks, out, lse, dout, plan=None):
    s, da = q.shape
    dkv = k.shape[1]
    nq, nb = da // HEAD_DIM, s // WINDOW
    scale = HEAD_DIM ** -0.5

    def body(q_ref, kp_ref, kc_ref, vp_ref, vc_ref, cc_ref, sc_ref, cp_ref, sp_ref, sink_ref, o_ref, lse_ref,
             do_ref, dq_ref, dk_ref, dv_ref, dsink_ref, dk_carry, dv_carry):
        n = pl.program_id(0)
        cp, sp = _lane_tile(cp_ref[...], dkv), _lane_tile(sp_ref[...], dkv)

        @pl.when(n == 0)
        def _():
            dk_carry[...] = jnp.zeros_like(dk_carry)
            dv_carry[...] = jnp.zeros_like(dv_carry)
            dsink_ref[...] = jnp.zeros_like(dsink_ref)

        @pl.when(n < nb)
        def _():
            cc, sc = cc_ref[...], sc_ref[...]
            qr = _rope(q_ref[...], _lane_tile(cc, da), _lane_tile(sc, da)).astype(BF16)
            kk = jnp.concatenate([_rope(kp_ref[...], cp, sp),
                                  _rope(kc_ref[...], _lane_tile(cc, dkv), _lane_tile(sc, dkv))], axis=0).astype(BF16)
            vv = jnp.concatenate([vp_ref[...], vc_ref[...]], axis=0).astype(BF16)
            valid = _band_mask(n)
            do_all, o_all = do_ref[...], o_ref[...]
            for hk in range(nq // Q_PER_KV):
                ks = slice(hk * HEAD_DIM, (hk + 1) * HEAD_DIM)
                q8, lse8 = _stack_heads(qr, hk), _stack_cols(lse_ref, hk)
                sco = lax.dot_general(q8, kk[:, ks], _DOT_DIMS["nt"], preferred_element_type=F32) * scale
                probs = jnp.where(valid, jnp.exp(sco - lse8), 0.0)
                do8 = _stack_heads(do_all, hk)
                delta = jnp.sum(do8 * _stack_heads(o_all, hk), axis=1, keepdims=True)
                do8 = do8.astype(BF16)
                dp = lax.dot_general(do8, vv[:, ks], _DOT_DIMS["nt"], preferred_element_type=F32)
                ds = (probs * (dp - delta) * scale).astype(BF16)
                dq8 = jnp.dot(ds, kk[:, ks], preferred_element_type=F32)
                dk_h = lax.dot_general(ds, q8, _DOT_DIMS["tn"], preferred_element_type=F32)
                dv_h = lax.dot_general(probs.astype(BF16), do8, _DOT_DIMS["tn"], preferred_element_type=F32)
                dsink8 = -jnp.exp(_stack_sinks(sink_ref, hk) - lse8) * delta
                for g in range(Q_PER_KV):
                    hq, rows = hk * Q_PER_KV + g, slice(g * WINDOW, (g + 1) * WINDOW)
                    dq_ref[:, hq * HEAD_DIM:(hq + 1) * HEAD_DIM] = dq8[rows]
                    dsink_ref[:, hq:hq + 1] += dsink8[rows].reshape(WINDOW // SUBLANES, SUBLANES, 1).sum(axis=0)
                dk_ref[:, ks] = dk_carry[:, ks] + dk_h[:WINDOW]
                dv_ref[:, ks] = dv_carry[:, ks] + dv_h[:WINDOW]
                dk_carry[:, ks] = dk_h[WINDOW:]
                dv_carry[:, ks] = dv_h[WINDOW:]
            dq_ref[...] = _rope_t(dq_ref[...], _lane_tile(cc, da), _lane_tile(sc, da))
            dk_ref[...] = _rope_t(dk_ref[...], cp, sp)

        @pl.when(n == nb)
        def _():
            dk_ref[...] = _rope_t(dk_carry[...], cp, sp)
            dv_ref[...] = dv_carry[...]

    sp_ = _attn_specs(da, dkv, nb)
    last_prev = lambda n: (jnp.maximum(n - 1, 0), 0)
    tab_prev = pl.BlockSpec((WINDOW, LANES), last_prev)
    kv_out = pl.BlockSpec((WINDOW, dkv), last_prev)
    lse_spec = pl.BlockSpec((WINDOW, nq), lambda n: (jnp.minimum(n, nb - 1), 0))
    in_specs = [sp_["q"], sp_["kv_prev"], sp_["kv_cur"], sp_["kv_prev"], sp_["kv_cur"],
                sp_["tab_cur"], sp_["tab_cur"], tab_prev, tab_prev,
                pl.BlockSpec((1, nq), lambda n: (0, 0)), sp_["q"], lse_spec, sp_["q"]]
    res = _pcall(name, body, (nb + 1,), in_specs, [q, k, k, v, v, cos, sin, cos, sin, sinks, out, lse, dout],
                 [sp_["q"], kv_out, kv_out, pl.BlockSpec((SUBLANES, nq), lambda n: (0, 0))],
                 [SDS((s, da), F32), SDS((s, dkv), F32), SDS((s, dkv), F32), SDS((SUBLANES, nq), F32)],
                 scratch=[pltpu.VMEM((WINDOW, dkv), F32), pltpu.VMEM((WINDOW, dkv), F32)],
                 semantics=("arbitrary",), plan=plan)
    (dq, dk, dv, dsink), rest = res if plan is not None else (res, None)
    return (dq, dk, dv, dsink.sum(0)), rest


def _ssm_operators(lam_re, lam_im, log_step, b_re, b_im, c_re, c_im, d_skip):
    g, p = lam_re.shape
    h = b_re.shape[-1]
    l = SSM_CHUNK
    step = jnp.exp(log_step)[:, None]
    mag = jnp.exp(lam_re * step)
    ar, ai = mag * jnp.cos(lam_im * step), mag * jnp.sin(lam_im * step)
    den = lam_re * lam_re + lam_im * lam_im
    cr = ((ar - 1.0) * lam_re + ai * lam_im) / den
    ci = (ai * lam_re - (ar - 1.0) * lam_im) / den
    bbr = cr[..., None] * b_re - ci[..., None] * b_im
    bbi = cr[..., None] * b_im + ci[..., None] * b_re
    pr, pi = [jnp.ones_like(ar)], [jnp.zeros_like(ar)]
    for _ in range(l):
        pr, pi = pr + [pr[-1] * ar - pi[-1] * ai], pi + [pr[-1] * ai + pi[-1] * ar]
    pwr, pwi = jnp.stack(pr, axis=1), jnp.stack(pi, axis=1)
    cpr = c_re[:, None] * pwr[:, :, None, :] - c_im[:, None] * pwi[:, :, None, :]
    cpi = c_re[:, None] * pwi[:, :, None, :] + c_im[:, None] * pwr[:, :, None, :]
    kern = (jnp.einsum("gtop,gpi->gtoi", cpr[:, :l], bbr, precision=HIGHEST)
            - jnp.einsum("gtop,gpi->gtoi", cpi[:, :l], bbi, precision=HIGHEST))
    kern = kern.at[:, 0].add(d_skip.reshape(g, h)[:, :, None] * jnp.eye(h, dtype=F32))
    tm = jnp.stack([jnp.pad(kern[:, :l - j], ((0, 0), (j, 0), (0, 0), (0, 0))) for j in range(l)], axis=1)
    tm = tm.transpose(0, 1, 4, 2, 3).reshape(g, l * h, l * h)
    rev_r, rev_i = pwr[:, l - 1::-1][:, :l], pwi[:, l - 1::-1][:, :l]
    er = rev_r[:, :, None, :] * bbr.transpose(0, 2, 1)[:, None] - rev_i[:, :, None, :] * bbi.transpose(0, 2, 1)[:, None]
    ei = rev_r[:, :, None, :] * bbi.transpose(0, 2, 1)[:, None] + rev_i[:, :, None, :] * bbr.transpose(0, 2, 1)[:, None]
    em = jnp.concatenate([er, ei], axis=-1).reshape(g, l * h, 2 * p)
    fr = cpr[:, 1:].transpose(0, 3, 1, 2).reshape(g, p, l * h)
    fi = -cpi[:, 1:].transpose(0, 3, 1, 2).reshape(g, p, l * h)
    fm = jnp.concatenate([fr, fi], axis=1)
    return tm, em, fm, pwr[:, l], pwi[:, l]


def _decay_lanes(alr, ali):
    return jnp.concatenate([alr, alr], axis=1), jnp.concatenate([-ali, ali], axis=1)


def _ssm_fwd(name, u, tm, em, fm, acat, bcat, plan=None):
    s, ds = u.shape
    g, lh, p2 = em.shape
    gb, h = SSM_GROUPS_PER_STEP, lh // SSM_CHUNK
    assert gb * h == LANES and g * h == ds and s % SSM_CHUNK == 0
    nc, half = s // SSM_CHUNK, p2 // 2

    def body(u_ref, tm_ref, em_ref, fm_ref, a_ref, b_ref, y_ref, xp_ref, uc_ref, yc_ref, st_ref):
        _to_chunks(u_ref, uc_ref, nc, h)
        for i in range(gb):
            st_ref[pl.ds(i, nc, stride=gb), :] = jnp.dot(uc_ref[i], em_ref[i], precision=HIGHEST,
                                                         preferred_element_type=F32)
        av, bv = a_ref[...], b_ref[...]

        def step(c, carry):
            x, xs = carry
            rows = pl.ds(pl.multiple_of(c * gb, gb), gb)
            loc = st_ref[rows, :]
            st_ref[rows, :] = x
            return av * x + bv * xs + loc, av * xs - bv * x + pltpu.roll(loc, half, 1)
        zero = jnp.zeros((gb, p2), F32)
        lax.fori_loop(0, nc, step, (zero, zero), unroll=4)
        for i in range(gb):
            xp = st_ref[pl.ds(i, nc, stride=gb), :]
            xp_ref[i] = xp
            yc_ref[i] = (jnp.dot(uc_ref[i], tm_ref[i], precision=HIGHEST, preferred_element_type=F32)
                         + jnp.dot(xp, fm_ref[i], precision=HIGHEST, preferred_element_type=F32))
        _from_chunks(yc_ref, y_ref, nc, h)

    blk = lambda r, c: pl.BlockSpec((gb, r, c), lambda i: (i, 0, 0))
    vec = pl.BlockSpec((gb, p2), lambda i: (i, 0))
    col = pl.BlockSpec((s, LANES), lambda i: (0, i))
    return _pcall(name, body, (g // gb,), [col, blk(lh, lh), blk(lh, p2), blk(p2, lh), vec, vec],
                  [u, tm, em, fm, acat, bcat], [col, blk(nc, p2), blk(nc, lh)],
                  [SDS((s, ds), F32), SDS((g, nc, p2), F32), SDS((g, nc, lh), F32)],
                  scratch=[pltpu.VMEM((gb, nc, lh), F32), pltpu.VMEM((nc * gb, p2), F32)],
                  semantics=("parallel",), plan=plan)


def _ssm_bwd(name, u_chunks, dy, xprev, tm, em, fm, acat, bcat, plan=None):
    s, ds = dy.shape
    g, lh, p2 = em.shape
    gb, h = SSM_GROUPS_PER_STEP, lh // SSM_CHUNK
    nc, half = s // SSM_CHUNK, p2 // 2

    def body(uc_ref, dy_ref, xp_ref, tm_ref, em_ref, fm_ref, a_ref, b_ref,
             du_ref, dtm_ref, dem_ref, dfm_ref, r1_ref, r2_ref, dyc_ref, duc_ref, gs_ref, xs_ref):
        _to_chunks(dy_ref, dyc_ref, nc, h)
        for i in range(gb):
            gs_ref[pl.ds(i, nc, stride=gb), :] = lax.dot_general(
                dyc_ref[i], fm_ref[i], _DOT_DIMS["nt"], precision=HIGHEST, preferred_element_type=F32)
            xs_ref[pl.ds(i, nc, stride=gb), :] = xp_ref[i]
        av, bv = a_ref[...], b_ref[...]

        def step(t, carry):
            grad, gsw, r1, r2 = carry
            c = nc - 1 - t
            rows = pl.ds(pl.multiple_of(c * gb, gb), gb)
            dxp, xp = gs_ref[rows, :], xs_ref[rows, :]
            gs_ref[rows, :] = grad
            r1 = r1 + grad * xp
            r2 = r2 + grad * pltpu.roll(xp, half, 1)
            return dxp + av * grad - bv * gsw, pltpu.roll(dxp, half, 1) + av * gsw + bv * grad, r1, r2
        zero = jnp.zeros((gb, p2), F32)
        _, _, r1, r2 = lax.fori_loop(0, nc, step, (zero, zero, zero, zero), unroll=4)
        r1_ref[...], r2_ref[...] = r1, r2
        for i in range(gb):
            dxl = gs_ref[pl.ds(i, nc, stride=gb), :]
            duc_ref[i] = (lax.dot_general(dyc_ref[i], tm_ref[i], _DOT_DIMS["nt"], precision=HIGHEST,
                                          preferred_element_type=F32)
                          + lax.dot_general(dxl, em_ref[i], _DOT_DIMS["nt"], precision=HIGHEST,
                                            preferred_element_type=F32))
            dtm_ref[i] = lax.dot_general(uc_ref[i], dyc_ref[i], _DOT_DIMS["tn"], precision=HIGHEST,
                                         preferred_element_type=F32)
            dfm_ref[i] = lax.dot_general(xp_ref[i], dyc_ref[i], _DOT_DIMS["tn"], precision=HIGHEST,
                                         preferred_element_type=F32)
            dem_ref[i] = lax.dot_general(uc_ref[i], dxl, _DOT_DIMS["tn"], precision=HIGHEST,
                                         preferred_element_type=F32)
        _from_chunks(duc_ref, du_ref, nc, h)

    blk = lambda r, c: pl.BlockSpec((gb, r, c), lambda i: (i, 0, 0))
    vec = pl.BlockSpec((gb, p2), lambda i: (i, 0))
    col = pl.BlockSpec((s, LANES), lambda i: (0, i))
    chunked = pltpu.VMEM((gb, nc, lh), F32)
    res = _pcall(name, body, (g // gb,),
                 [blk(nc, lh), col, blk(nc, p2), blk(lh, lh), blk(lh, p2), blk(p2, lh), vec, vec],
                 [u_chunks, dy, xprev, tm, em, fm, acat, bcat],
                 [col, blk(lh, lh), blk(lh, p2), blk(p2, lh), vec, vec],
                 [SDS((s, ds), F32), SDS((g, lh, lh), F32), SDS((g, lh, p2), F32), SDS((g, p2, lh), F32),
                  SDS((g, p2), F32), SDS((g, p2), F32)],
                 scratch=[chunked, chunked, pltpu.VMEM((nc * gb, p2), F32), pltpu.VMEM((nc * gb, p2), F32)],
                 semantics=("parallel",), plan=plan)
    return res if plan is not None else (res, None)


def _to_chunks(src_ref, dst_ref, nc, h):
    per = LANES // h
    grp = lax.broadcasted_iota(jnp.int32, (nc, LANES), 1) // h
    for g in range(per):
        for part in range(SSM_CHUNK * h // LANES):
            acc = None
            for i in range(part * per, (part + 1) * per):
                piece = src_ref[pl.ds(i, nc, stride=SSM_CHUNK), :]
                lo = (i * h) % LANES
                if (lo - g * h) % LANES:
                    piece = pltpu.roll(piece, (lo - g * h) % LANES, 1)
                acc = piece if acc is None else jnp.where(grp == lo // h, piece, acc)
            dst_ref[g, :, part * LANES:(part + 1) * LANES] = acc


def _from_chunks(src_ref, dst_ref, nc, h):
    per = LANES // h
    grp = lax.broadcasted_iota(jnp.int32, (nc, LANES), 1) // h
    for i in range(SSM_CHUNK):
        part, lo = divmod(i * h, LANES)
        row = None
        for g in range(per):
            piece = src_ref[g, :, part * LANES:(part + 1) * LANES]
            if (g * h - lo) % LANES:
                piece = pltpu.roll(piece, (g * h - lo) % LANES, 1)
            row = piece if row is None else jnp.where(grp == g, piece, row)
        dst_ref[pl.ds(i, nc, stride=SSM_CHUNK), :] = row


_SMALL = ("b_ada", "norm1_g", "sinks", "ssm_lam_re", "ssm_lam_im", "ssm_log_step", "ssm_b_re", "ssm_b_im",
          "ssm_c_re", "ssm_c_im", "ssm_d", "b_glu", "attn_out_g", "ssm_out_g", "norm2_g", "final_g")
_WEIGHTS = ("w_ada", "b_ada", "norm1_g", "w_in", "sinks", "ssm_lam_re", "ssm_lam_im", "ssm_log_step", "ssm_b_re",
            "ssm_b_im", "ssm_c_re", "ssm_c_im", "ssm_d", "w_glu", "b_glu", "attn_out_g", "ssm_out_g", "w_out",
            "norm2_g", "w_ff1", "w_ff2", "final_g")
_PACK_ALIGN = 128 * LANES


def _pack(parts):
    flat = jnp.concatenate([p.reshape(-1).astype(F32) for p in parts])
    pad = (-flat.shape[0]) % _PACK_ALIGN
    return jnp.pad(flat, (0, pad)).reshape(-1, LANES)


def kernel(x, c, w_ada, b_ada, norm1_g, w_in, sinks, ssm_lam_re, ssm_lam_im, ssm_log_step, ssm_b_re, ssm_b_im, ssm_c_re, ssm_c_im, ssm_d, w_glu, b_glu, attn_out_g, ssm_out_g, w_out, norm2_g, w_ff1, w_ff2, final_g, loss_target, m_w_ada, m_b_ada, m_norm1_g, m_w_in, m_sinks, m_ssm_lam_re, m_ssm_lam_im, m_ssm_log_step, m_ssm_b_re, m_ssm_b_im, m_ssm_c_re, m_ssm_c_im, m_ssm_d, m_w_glu, m_b_glu, m_attn_out_g, m_ssm_out_g, m_w_out, m_norm2_g, m_w_ff1, m_w_ff2, m_final_g, v_w_ada, v_b_ada, v_norm1_g, v_w_in, v_sinks, v_ssm_lam_re, v_ssm_lam_im, v_ssm_log_step, v_ssm_b_re, v_ssm_b_im, v_ssm_c_re, v_ssm_c_im, v_ssm_d, v_w_glu, v_b_glu, v_attn_out_g, v_ssm_out_g, v_w_out, v_norm2_g, v_w_ff1, v_w_ff2, v_final_g):
    args = dict(locals())
    weights = {n: args[n] for n in _WEIGHTS}
    mom = {n: args["m_" + n] for n in _WEIGHTS}
    var = {n: args["v_" + n] for n in _WEIGHTS}
    me = 4 * lax.axis_index("x") + 2 * lax.axis_index("y") + lax.axis_index("c")

    _, s, d = x.shape
    xs, tgt = x[0], loss_target[0]
    d_ssm = ssm_d.shape[-1]
    d_attn = d - d_ssm
    nq = d_attn // HEAD_DIM
    d_kv = (nq // Q_PER_KV) * HEAD_DIM
    p_state = ssm_b_re.shape[2]

    c_all, g_in = _run_plan("gather_c_w_in", _Gather([c, w_in[0].T.astype(BF16)]))
    c_all = c_all.reshape(N_DEV, d)
    w_in_t = g_in.reshape(-1, d)

    n_loc = w_ada.shape[-1]
    b_loc = lax.dynamic_slice_in_dim(b_ada, me * n_loc, n_loc, axis=1)
    silu = lambda t: t * _sigmoid(t)
    mod_part = _matmul("ada_mod", c_all, w_ada[0], "nn", [F32], a_pro=silu, vecs=[b_loc], exact=True,
                       epilogue=lambda acc, e, v: (acc + v[0],), tn=512, tk=d)
    mod_all = _run_plan("gather_mod", _Gather([mod_part]))[0]
    mod = lax.dynamic_index_in_dim(mod_all, me, axis=1, keepdims=False).reshape(N_MOD, 1, d)
    shift1, scale1, gate1, shift2, scale2, gate2 = [mod[i] for i in range(N_MOD)]

    h1 = _norm_mod_fwd("norm1", xs, norm1_g, scale1, shift1)
    q = _matmul("proj_q", h1, w_in_t, "nt", [F32], b_rows=(0, d_attn))
    kv = _matmul("proj_kv", h1, w_in_t, "nt", [F32], b_rows=(d_attn, 2 * d_kv))
    u = _matmul("proj_u", h1, w_in_t, "nt", [F32], b_rows=(d_attn + 2 * d_kv, d_ssm))
    k, v = kv[:, :d_kv], kv[:, d_kv:]

    ssm_params = (ssm_lam_re[0], ssm_lam_im[0], ssm_log_step[0], ssm_b_re[0], ssm_b_im[0], ssm_c_re[0],
                  ssm_c_im[0], ssm_d[0])
    (tm_op, em_op, fm_op, alr, ali), ssm_vjp = jax.vjp(_ssm_operators, *ssm_params)
    acat, bcat = _decay_lanes(alr, ali)
    (y_ssm, x_prev, u_chunks), (g_glu, g_out) = _ssm_fwd(
        "ssm_fwd", u, tm_op, em_op, fm_op, acat, bcat, plan=_Gather([w_glu[0].astype(BF16), w_out[0].astype(BF16)]))
    w_glu_f = g_glu.reshape(d_ssm, d_ssm)
    w_out_f = g_out.reshape(d, d)

    half = HEAD_DIM // 2
    inv_freq = ROPE_THETA ** (-jnp.arange(half, dtype=F32) / half)
    ang = jnp.arange(s, dtype=F32)[:, None] * inv_freq[None, :]
    cos_t, sin_t = jnp.tile(jnp.cos(ang), (1, 4)), jnp.tile(jnp.sin(ang), (1, 4))
    (attn, lse), (g_ff1,) = _attn_fwd("attn_fwd", q, k, v, cos_t, sin_t, sinks, plan=_Gather([w_ff1[0].astype(BF16)]))
    yg = _gelu_fwd("gelu", y_ssm)
    ssm_out, z_glu = _matmul(
        "glu", yg, w_glu_f, "nn", [F32, F32], extras=[y_ssm], vecs=[b_glu],
        epilogue=lambda acc, e, v: (_gelu(e[0]) * _sigmoid(acc + v[0]), acc + v[0]))
    mixed = _group_norm_fwd("group_norm", attn, ssm_out, attn_out_g, ssm_out_g)
    x2, mo = _matmul("out_proj", mixed, w_out_f, "nn", [F32, BF16], extras=[xs], vecs=[gate1],
                     epilogue=lambda acc, e, v: (e[0] + v[0] * acc, acc))

    h2 = _norm_mod_fwd("norm2", x2, norm2_g, scale2, shift2)
    (a_ff, f_ff), (g_ff2,) = _matmul("ff1", h2, g_ff1, "nn", [BF16, BF16], b_blocked=True,
                                     epilogue=lambda acc, e, v: (acc, jnp.square(jnp.maximum(acc, 0.0))),
                                     plan=_Gather([w_ff2[0].astype(BF16)]))
    w_ff2_f = g_ff2.reshape(-1, d)
    x3, ff = _matmul("ff2", f_ff, w_ff2_f, "nn", [F32, BF16], extras=[x2], vecs=[gate2],
                     epilogue=lambda acc, e, v: (e[0] + v[0] * acc, acc))

    dx3, loss_local, d_final_g = _final_loss("final_loss", x3, tgt, final_g.reshape(1, d))
    loss = lax.psum(loss_local, MESH_AXES)

    dff, d_gate2 = _gate_bwd("gate2_bwd", dx3, ff, gate2)
    dw_ff2 = _matmul("ff2_dw", f_ff, dff, "tn", [BF16]).reshape(N_DEV, -1, d)
    da_ff, (r_ff2_a,) = _matmul("ff2_dx", dff, w_ff2_f, "nt", [BF16], extras=[a_ff],
                                epilogue=lambda acc, e, v: (acc * (2.0 * jnp.maximum(e[0].astype(F32), 0.0)),),
                                plan=_Exchange([dw_ff2], RELATIONS_SAME_CORE))
    dw_ff1, (r_ff2_b,) = _matmul("ff1_dw", h2, da_ff, "tn", [BF16], out_blocked=N_DEV,
                                 plan=_Exchange([dw_ff2], RELATIONS_OTHER_CORE))
    dh2, (r_ff1_a,) = _matmul("ff1_dx", da_ff, g_ff1, "nt", [F32], b_blocked=True,
                              plan=_Exchange([dw_ff1], RELATIONS_SAME_CORE))
    (dx2, d_scale2, d_shift2, d_norm2_g), _ = _norm_mod_bwd("norm2_bwd", x2, dh2, dx3, norm2_g, scale2)

    dmo, d_gate1 = _gate_bwd("gate1_bwd", dx2, mo, gate1)
    dw_out = _matmul("out_dw", mixed, dmo, "tn", [BF16]).reshape(N_DEV, -1, d)
    dmixed, (r_ff1_b,) = _matmul("out_dx", dmo, w_out_f, "nt", [F32], plan=_Exchange([dw_ff1], (5, 3)))
    dattn, dssm_out, d_attn_g, d_ssm_g = _group_norm_bwd("group_norm_bwd", attn, ssm_out, dmixed, attn_out_g, ssm_out_g)

    dz, dyg_direct, d_b_glu = _glu_bwd("glu_bwd", dssm_out, y_ssm, z_glu)
    dw_glu = _matmul("glu_dw", yg, dz, "tn", [BF16]).reshape(N_DEV, -1, d_ssm)
    dy_ssm = _matmul("glu_dx", dz, w_glu_f, "nt", [F32], extras=[dyg_direct, y_ssm],
                     epilogue=lambda acc, e, v: ((acc + e[0]) * _gelu_grad(e[1]),))
    (du, d_tm, d_em, d_fm, r1, r2), (r_ff1_c,) = _ssm_bwd(
        "ssm_bwd", u_chunks, dy_ssm, x_prev, tm_op, em_op, fm_op, acat, bcat, plan=_Exchange([dw_ff1], (7,)))
    d_alr = r1[:, :p_state] + r1[:, p_state:]
    d_ali = r2[:, p_state:] - r2[:, :p_state]
    d_ssm_params = ssm_vjp((d_tm, d_em, d_fm, d_alr, d_ali))

    (dq, dk, dv, d_sinks), (r_out, r_glu) = _attn_bwd(
        "attn_bwd", q, k, v, cos_t, sin_t, sinks, attn, lse, dattn,
        plan=_Plans([_Exchange([dw_out], RELATIONS_ALL), _Exchange([dw_glu], RELATIONS_ALL)]))
    dproj = jnp.concatenate([dq, dk, dv, du], axis=1).astype(BF16)
    dw_in_t = _matmul("in_dw", dproj, h1, "tn", [BF16]).reshape(N_DEV, -1, d)
    dh1, (r_in_a,) = _matmul("in_dx", dproj, w_in_t, "nn", [F32], plan=_Exchange([dw_in_t], RELATIONS_SAME_CORE))
    (grad_x, d_scale1, d_shift1, d_norm1_g), (r_in_b,) = _norm_mod_bwd(
        "norm1_bwd", xs, dh1, dx2, norm1_g, scale1, plan=_Exchange([dw_in_t], (5, 3)))

    d_mod = jnp.concatenate([d_shift1, d_scale1, d_gate1, d_shift2, d_scale2, d_gate2])
    small_g = dict(zip(("ssm_lam_re", "ssm_lam_im", "ssm_log_step", "ssm_b_re", "ssm_b_im", "ssm_c_re", "ssm_c_im",
                        "ssm_d"), d_ssm_params, strict=True))
    small_g.update(b_ada=d_mod, norm1_g=d_norm1_g, sinks=d_sinks, b_glu=d_b_glu, attn_out_g=d_attn_g,
                   ssm_out_g=d_ssm_g, norm2_g=d_norm2_g, final_g=d_final_g)
    small_parts = _run_plan("gather_small_grads", _Gather([_pack([small_g[n] for n in _SMALL])]))[0]
    small = _adam_shard("adam_small", [small_parts], _pack([weights[n] for n in _SMALL]),
                        _pack([mom[n] for n in _SMALL]), _pack([var[n] for n in _SMALL]))
    out = {}
    off = 0
    for n in _SMALL:
        size = weights[n].size
        out[n] = [t.reshape(-1)[off:off + size].reshape(weights[n].shape) for t in small]
        off += size

    dmod_all = small_parts.reshape(N_DEV, -1)[:, :N_MOD * d]
    dmod_loc = lax.dynamic_slice_in_dim(dmod_all, me * n_loc, n_loc, axis=1)
    c_act_t = silu(c_all).T
    ada_out, (r_in_c,) = _ada_update("adam_w_ada", c_act_t, dmod_loc, w_ada[0], m_w_ada[0], v_w_ada[0],
                                     plan=_Exchange([dw_in_t], (7,)))
    out["w_ada"] = [t[None] for t in ada_out]

    mine = lambda blocks: lax.dynamic_index_in_dim(blocks, me, axis=0, keepdims=False)
    in_parts = [mine(dw_in_t).T] + [r.transpose(0, 2, 1) for r in (r_in_a, r_in_b, r_in_c)]
    received = dict(w_in=in_parts, w_glu=[mine(dw_glu), r_glu], w_out=[mine(dw_out), r_out],
                    w_ff1=[mine(dw_ff1), r_ff1_a, r_ff1_b, r_ff1_c], w_ff2=[mine(dw_ff2), r_ff2_a, r_ff2_b])
    for n, parts in received.items():
        out[n] = [t[None] for t in _adam_shard("adam_" + n, parts, weights[n][0], mom[n][0], var[n][0])]

    return (loss, grad_x[None], *[out[n][0] for n in _WEIGHTS], *[out[n][1] for n in _WEIGHTS],
            *[out[n][2] for n in _WEIGHTS], *[out[n][3] for n in _WEIGHTS])
```

```python
import functools
import math
import operator

import jax
import jax.numpy as jnp
from jax import lax
from jax.experimental import pallas as pl
from jax.experimental.pallas import tpu as pltpu

F32, BF16 = jnp.float32, jnp.bfloat16
SDS = jax.ShapeDtypeStruct
MESH_AXES = ("x", "y", "c")
N_DEV = 8
VMEM_LIMIT_BYTES = 56 * 1024 * 1024
SUBLANES, LANES = 8, 128

HEAD_DIM = 64
Q_PER_KV = 8
WINDOW = 128
ROPE_THETA = 10000.0
EPS = 1e-6
N_MOD = 6
SSM_CHUNK = 16
SSM_GROUPS_PER_STEP = 8

ADAM_LR, ADAM_B1, ADAM_B2, ADAM_EPS, ADAM_WD, ADAM_STEP = 0.001, 0.9, 0.999, 1e-08, 0.01, 10
HIGHEST = lax.Precision.HIGHEST

RELATIONS_ALL = (1, 4, 2, 6, 5, 3, 7)
RELATIONS_SAME_CORE = (1, 4, 2, 6)
RELATIONS_OTHER_CORE = (5, 3, 7)


def _cparams(sem):
    return pltpu.CompilerParams(dimension_semantics=sem, vmem_limit_bytes=VMEM_LIMIT_BYTES)


def _block_index(p):
    return 4 * p[0] + 2 * p[1] + p[2]


def _me():
    return lax.axis_index("x"), lax.axis_index("y"), lax.axis_index("c")


class _Gather:
    def __init__(self, arrs):
        self.ins = list(arrs)
        self.out_shapes = [SDS((N_DEV,) + a.shape, a.dtype) for a in arrs]
        self.n_rdma, self.n_local = 7 * len(arrs), len(arrs)
        self.rdma_base = self.local_base = 0

    def _copy(self, ins, outs, send, recv, a, k, block, to, from_input=False):
        dst = outs[a].at[_block_index(block)]
        sem = self.rdma_base + a * 7 + k
        return pltpu.make_async_remote_copy(
            src_ref=ins[a] if from_input else dst, dst_ref=dst, send_sem=send.at[sem], recv_sem=recv.at[sem],
            device_id=to, device_id_type=pl.DeviceIdType.MESH)

    def _first(self, ins, outs, send, recv, a):
        x, y, c = _me()
        chips = [(1 - x, y), (x, 1 - y), (1 - x, 1 - y)]
        cps = [self._copy(ins, outs, send, recv, a, 0, (x, y, c), (x, y, 1 - c), True)]
        return cps + [self._copy(ins, outs, send, recv, a, 1 + j, (x, y, c), (*chip, c), True)
                      for j, chip in enumerate(chips)]

    def _mine(self, ins, outs, local, a):
        return pltpu.make_async_copy(ins[a], outs[a].at[_block_index(_me())], local.at[self.local_base + a])

    def start(self, ins, outs, send, recv, local):
        for a in range(len(ins)):
            self._mine(ins, outs, local, a).start()
            for cp in self._first(ins, outs, send, recv, a):
                cp.start()

    def finish(self, ins, outs, send, recv, local):
        x, y, c = _me()
        me, sibling = (x, y, c), (x, y, 1 - c)
        chips = [(1 - x, y), (x, 1 - y), (1 - x, 1 - y)]
        forwards = []
        for a in range(len(ins)):
            for j, chip in enumerate(chips):
                self._copy(ins, outs, send, recv, a, 1 + j, (*chip, c), me).wait_recv()
                fwd = self._copy(ins, outs, send, recv, a, 4 + j, (*chip, c), sibling)
                fwd.start()
                forwards.append(fwd)
        for a in range(len(ins)):
            self._copy(ins, outs, send, recv, a, 0, sibling, me).wait_recv()
            for j, chip in enumerate(chips):
                self._copy(ins, outs, send, recv, a, 4 + j, (*chip, 1 - c), me).wait_recv()
            for cp in self._first(ins, outs, send, recv, a):
                cp.wait_send()
            self._mine(ins, outs, local, a).wait()
        for fwd in forwards:
            fwd.wait_send()


class _Exchange:
    def __init__(self, arrs, relations):
        self.ins, self.relations = list(arrs), tuple(relations)
        self.out_shapes = [SDS((len(relations),) + a.shape[1:], a.dtype) for a in arrs]
        self.n_rdma, self.n_local = len(relations) * len(arrs), 0
        self.rdma_base = self.local_base = 0

    def _copies(self, ins, outs, send, recv):
        x, y, c = _me()
        cps = []
        for a in range(len(ins)):
            for s, k in enumerate(self.relations):
                peer = ((1 - x) if (k & 4) else x, (1 - y) if (k & 2) else y, (1 - c) if (k & 1) else c)
                sem = self.rdma_base + a * len(self.relations) + s
                cps.append(pltpu.make_async_remote_copy(
                    src_ref=ins[a].at[_block_index(peer)], dst_ref=outs[a].at[s], send_sem=send.at[sem],
                    recv_sem=recv.at[sem], device_id=peer, device_id_type=pl.DeviceIdType.MESH))
        return cps

    def start(self, ins, outs, send, recv, local):
        for cp in self._copies(ins, outs, send, recv):
            cp.start()

    def finish(self, ins, outs, send, recv, local):
        for cp in self._copies(ins, outs, send, recv):
            cp.wait()


class _Plans:
    def __init__(self, plans):
        self.plans = list(plans)
        self.ins = [a for p in plans for a in p.ins]
        self.out_shapes = [s for p in plans for s in p.out_shapes]
        self.n_rdma = self.n_local = 0
        for p in plans:
            p.rdma_base, p.local_base = self.n_rdma, self.n_local
            self.n_rdma, self.n_local = self.n_rdma + p.n_rdma, self.n_local + p.n_local

    def _each(self, ins, outs):
        i = o = 0
        for p in self.plans:
            yield p, ins[i:i + len(p.ins)], outs[o:o + len(p.out_shapes)]
            i, o = i + len(p.ins), o + len(p.out_shapes)

    def start(self, ins, outs, send, recv, local):
        for p, pi, po in self._each(ins, outs):
            p.start(pi, po, send, recv, local)

    def finish(self, ins, outs, send, recv, local):
        for p, pi, po in self._each(ins, outs):
            p.finish(pi, po, send, recv, local)


def _plan_scratch(plan):
    return [pltpu.SemaphoreType.DMA((plan.n_rdma,)), pltpu.SemaphoreType.DMA((plan.n_rdma,)),
            pltpu.SemaphoreType.DMA((max(plan.n_local, 1),))]


def _run_plan(name, plan):
    n = len(plan.ins)

    def body(*refs):
        ins, outs, sems = refs[:n], refs[n:len(refs) - 3], refs[len(refs) - 3:]
        plan.start(ins, outs, *sems)
        plan.finish(ins, outs, *sems)

    any_spec = pl.BlockSpec(memory_space=pl.ANY)
    return pl.pallas_call(body, name=name, out_shape=list(plan.out_shapes), in_specs=[any_spec] * n,
                          out_specs=[any_spec] * len(plan.out_shapes), scratch_shapes=_plan_scratch(plan))(*plan.ins)


def _pcall(name, body, grid, in_specs, ins, out_specs, out_shape, scratch=(), semantics=None, plan=None):
    if plan is None:
        return pl.pallas_call(body, name=name, grid=grid, in_specs=list(in_specs), out_specs=list(out_specs),
                              out_shape=list(out_shape), scratch_shapes=list(scratch),
                              compiler_params=_cparams(semantics))(*ins)
    n_in, n_out, n_scr = len(ins), len(out_shape), len(scratch)
    p_in, p_out = len(plan.ins), len(plan.out_shapes)

    def with_plan(*refs):
        k_in, c_in = refs[:n_in], refs[n_in:n_in + p_in]
        refs = refs[n_in + p_in:]
        k_out, c_out = refs[:n_out], refs[n_out:n_out + p_out]
        refs = refs[n_out + p_out:]
        k_scr, sems = refs[:n_scr], refs[n_scr:]
        ids = [pl.program_id(d) for d in range(len(grid))]
        first = functools.reduce(operator.and_, [i == 0 for i in ids])
        last = functools.reduce(operator.and_, [i == g - 1 for i, g in zip(ids, grid)])

        @pl.when(first)
        def _():
            plan.start(c_in, c_out, *sems)

        body(*k_in, *k_out, *k_scr)

        @pl.when(last)
        def _():
            plan.finish(c_in, c_out, *sems)

    any_spec = pl.BlockSpec(memory_space=pl.ANY)
    res = pl.pallas_call(
        with_plan, name=name, grid=grid, in_specs=list(in_specs) + [any_spec] * p_in,
        out_specs=list(out_specs) + [any_spec] * p_out, out_shape=list(out_shape) + list(plan.out_shapes),
        scratch_shapes=list(scratch) + _plan_scratch(plan),
        compiler_params=_cparams(("arbitrary",) * len(grid)))(*ins, *plan.ins)
    return res[:n_out], res[n_out:]


def _rowwise(name, fn, rows, vecs, row_outs, acc_outs=(), tm=128, plan=None):
    t = rows[0].shape[0]
    tm = min(tm, t)
    assert t % tm == 0 and tm % SUBLANES == 0
    n_r, n_v, n_o = len(rows), len(vecs), len(row_outs)

    def body(*refs):
        r_in, v_in = refs[:n_r], refs[n_r:n_r + n_v]
        r_out, a_out = refs[n_r + n_v:n_r + n_v + n_o], refs[n_r + n_v + n_o:]
        outs, accs = fn([r[...] for r in r_in], [v[...] for v in v_in])
        for o_ref, o in zip(r_out, outs, strict=True):
            o_ref[...] = o.astype(o_ref.dtype)
        if a_out:
            @pl.when(pl.program_id(0) == 0)
            def _():
                for a_ref in a_out:
                    a_ref[...] = jnp.zeros_like(a_ref)
            for a_ref, a in zip(a_out, accs, strict=True):
                a_ref[...] += a.reshape(tm // SUBLANES, SUBLANES, a.shape[-1]).sum(axis=0)

    in_specs = [pl.BlockSpec((tm, r.shape[1]), lambda i: (i, 0)) for r in rows]
    in_specs += [pl.BlockSpec(v.shape, lambda i: (0, 0)) for v in vecs]
    out_specs = [pl.BlockSpec((tm, w), lambda i: (i, 0)) for w, _ in row_outs]
    out_specs += [pl.BlockSpec((SUBLANES, w), lambda i: (0, 0)) for w in acc_outs]
    out_shape = [SDS((t, w), dt) for w, dt in row_outs] + [SDS((SUBLANES, w), F32) for w in acc_outs]
    return _pcall(name, body, (t // tm,), in_specs, [*rows, *vecs], out_specs, out_shape, semantics=("arbitrary",),
                  plan=plan)


def _tile(n, want):
    if n <= want:
        return n
    for t in range(want // LANES * LANES, 0, -LANES):
        if n % t == 0:
            return t
    raise ValueError(f"no tile for {n}")


_DOT_DIMS = {"nn": (((1,), (0,)), ((), ())), "nt": (((1,), (1,)), ((), ())), "tn": (((0,), (0,)), ((), ()))}


def _matmul(name, a, b, mode, out_dtypes, epilogue=None, extras=(), vecs=(), a_pro=None,
            tm=1024, tn=512, tk=4096, exact=False, b_blocked=False, out_blocked=0, b_rows=None, plan=None):
    cs = b.shape[-1] if b_blocked else None
    b2 = (b.shape[1], b.shape[0] * b.shape[2]) if b_blocked else b.shape
    if mode == "tn":
        (k, m), (k2, n) = a.shape, b2
    elif mode == "nt":
        (m, k), (n, k2) = a.shape, b2
    else:
        (m, k), (k2, n) = a.shape, b2
    assert k == k2 and not (b_blocked and mode == "tn")
    row0 = 0
    if b_rows is not None:
        assert mode == "nt" and not b_blocked
        row0, n = b_rows
        tn = _tile(math.gcd(n, row0) if row0 else n, tn)
    tm, tn, tk = _tile(m, tm), _tile(n, tn), _tile(k, tk)
    if b_blocked and mode == "nn":
        tn = _tile(cs, tn)
    if b_blocked and mode == "nt":
        tk = _tile(cs, tk)
    if out_blocked:
        tn = _tile(n // out_blocked, tn)
    nk = k // tk
    n_e, n_v, n_o = len(extras), len(vecs), len(out_dtypes)
    precision = HIGHEST if exact else None

    def body(*refs):
        a_ref, b_ref = refs[:2]
        e_refs, v_refs = refs[2:2 + n_e], refs[2 + n_e:2 + n_e + n_v]
        o_refs = refs[2 + n_e + n_v:2 + n_e + n_v + n_o]

        def product():
            av = a_ref[...]
            if a_pro is not None:
                av = a_pro(av)
            return lax.dot_general(av, b_ref[...], _DOT_DIMS[mode], precision=precision, preferred_element_type=F32)

        def finish(acc):
            res = (acc,) if epilogue is None else epilogue(acc, [e[...] for e in e_refs], [v[...] for v in v_refs])
            for o_ref, r in zip(o_refs, res, strict=True):
                o_ref[...] = r.astype(o_ref.dtype)

        if nk == 1:
            finish(product())
            return
        acc_ref = refs[-1]
        kk = pl.program_id(2)

        @pl.when(kk == 0)
        def _():
            acc_ref[...] = product()

        @pl.when(kk > 0)
        def _():
            acc_ref[...] += product()

        @pl.when(kk == nk - 1)
        def _():
            finish(acc_ref[...])

    if mode == "tn":
        a_spec = pl.BlockSpec((tk, tm), lambda i, j, kk: (kk, i))
    else:
        a_spec = pl.BlockSpec((tm, tk), lambda i, j, kk: (i, kk))
    if b_blocked and mode == "nn":
        per = cs // tn
        b_spec = pl.BlockSpec((None, tk, tn), lambda i, j, kk: (j // per, kk, j % per))
    elif b_blocked:
        per = cs // tk
        b_spec = pl.BlockSpec((None, tn, tk), lambda i, j, kk: (kk // per, j, kk % per))
    elif mode == "nt":
        assert row0 % tn == 0
        b_spec = pl.BlockSpec((tn, tk), lambda i, j, kk: (j + row0 // tn, kk))
    else:
        b_spec = pl.BlockSpec((tk, tn), lambda i, j, kk: (kk, j))
    tile = pl.BlockSpec((tm, tn), lambda i, j, kk: (i, j))
    if out_blocked:
        per_o = n // out_blocked // tn
        out_spec = pl.BlockSpec((None, tm, tn), lambda i, j, kk: (j // per_o, i, j % per_o))
        out_shape = [SDS((out_blocked, m, n // out_blocked), dt) for dt in out_dtypes]
    else:
        out_spec, out_shape = tile, [SDS((m, n), dt) for dt in out_dtypes]
    in_specs = [a_spec, b_spec] + [tile] * n_e + [pl.BlockSpec((1, tn), lambda i, j, kk: (0, j))] * n_v
    res = _pcall(name, body, (m // tm, n // tn, nk), in_specs, [a, b, *extras, *vecs], [out_spec] * n_o, out_shape,
                 scratch=[pltpu.VMEM((tm, tn), F32)] if nk > 1 else [],
                 semantics=("parallel", "parallel", "arbitrary"), plan=plan)
    if plan is None:
        return res[0] if n_o == 1 else res
    return (res[0][0] if n_o == 1 else res[0]), res[1]


def _rms_fwd(x):
    r = lax.rsqrt(jnp.mean(x * x, axis=-1, keepdims=True) + EPS)
    return x * r, r


def _rms_bwd(dxn, xn, r):
    return r * (dxn - xn * jnp.mean(dxn * xn, axis=-1, keepdims=True))


_INV_SQRT2 = 1.0 / math.sqrt(2.0)
_INV_SQRT2PI = 1.0 / math.sqrt(2.0 * math.pi)


def _gelu(y):
    return 0.5 * y * (1.0 + lax.erf(y * _INV_SQRT2))


def _gelu_grad(y):
    return 0.5 * (1.0 + lax.erf(y * _INV_SQRT2)) + y * (_INV_SQRT2PI * jnp.exp(-0.5 * y * y))


def _sigmoid(z):
    return 1.0 / (1.0 + jnp.exp(-z))


def _adam_math(w, g, m, v):
    m = ADAM_B1 * m + (1.0 - ADAM_B1) * g
    v = ADAM_B2 * v + (1.0 - ADAM_B2) * (g * g)
    m_hat = m / (1.0 - ADAM_B1 ** ADAM_STEP)
    v_hat = v / (1.0 - ADAM_B2 ** ADAM_STEP)
    delta = -ADAM_LR * (m_hat / (jnp.sqrt(v_hat) + ADAM_EPS) + ADAM_WD * w)
    return delta, m, v


def _norm_mod_fwd(name, x, g, scale, shift):
    def fn(rows, vecs):
        (xv,), (gv, sc, sh) = rows, vecs
        xn, _ = _rms_fwd(xv)
        return [(xn * gv) * (1.0 + sc) + sh], []
    return _rowwise(name, fn, [x], [g, scale, shift], [(x.shape[1], BF16)])[0]


def _norm_mod_bwd(name, x, dh, dres, g, scale, plan=None):
    d = x.shape[1]

    def fn(rows, vecs):
        (xv, dhv, drv), (gv, sc) = rows, vecs
        xn, r = _rms_fwd(xv)
        t = xn * gv
        dt = dhv * (1.0 + sc)
        dx = drv + _rms_bwd(dt * gv, xn, r)
        return [dx], [dhv * t, dhv, dt * xn]
    res = _rowwise(name, fn, [x, dh, dres], [g, scale], [(d, F32)], [d, d, d], plan=plan)
    (dx, dscale, dshift, dg), rest = res if plan is not None else (res, None)
    return (dx, dscale.sum(0), dshift.sum(0), dg.sum(0)), rest


def _gate_bwd(name, dx, val, gate):
    d = dx.shape[1]

    def fn(rows, vecs):
        (dxv, vv), (gv,) = rows, vecs
        return [dxv * gv], [dxv * vv.astype(F32)]
    dval, dgate = _rowwise(name, fn, [dx, val], [gate], [(d, BF16)], [d])
    return dval, dgate.sum(0)


def _final_loss(name, x, tgt, g):
    d = x.shape[1]

    def fn(rows, vecs):
        (xv, tv), (gv,) = rows, vecs
        xn, r = _rms_fwd(xv)
        e = xn * gv - tv
        dy = e * (1.0 / d)
        dx = _rms_bwd(dy * gv, xn, r)
        return [dx], [e * e, dy * xn]
    dx, sq, dg = _rowwise(name, fn, [x, tgt], [g], [(d, F32)], [d, d])
    return dx, 0.5 * jnp.sum(sq) / d, dg.sum(0)


def _group_norm_fwd(name, attn, ssm, g_a, g_s):
    def fn(rows, vecs):
        (av, sv), (ga, gs) = rows, vecs
        return [jnp.concatenate([_rms_fwd(av)[0] * ga, _rms_fwd(sv)[0] * gs], axis=1)], []
    return _rowwise(name, fn, [attn, ssm], [g_a, g_s], [(attn.shape[1] + ssm.shape[1], BF16)])[0]


def _group_norm_bwd(name, attn, ssm, dmixed, g_a, g_s):
    da_w, ds_w = attn.shape[1], ssm.shape[1]

    def fn(rows, vecs):
        (av, sv, dm), (ga, gs) = rows, vecs
        an, ra = _rms_fwd(av)
        sn, rs = _rms_fwd(sv)
        dma, dms = dm[:, :da_w], dm[:, da_w:]
        return [_rms_bwd(dma * ga, an, ra), _rms_bwd(dms * gs, sn, rs)], [dma * an, dms * sn]
    dattn, dssm, dga, dgs = _rowwise(name, fn, [attn, ssm, dmixed], [g_a, g_s],
                                     [(da_w, F32), (ds_w, F32)], [da_w, ds_w])
    return dattn, dssm, dga.sum(0), dgs.sum(0)


def _gelu_fwd(name, y):
    def fn(rows, vecs):
        return [_gelu(rows[0])], []
    return _rowwise(name, fn, [y], [], [(y.shape[1], BF16)])[0]


def _glu_bwd(name, dout, y, z):
    d = y.shape[1]

    def fn(rows, vecs):
        dov, yv, zv = rows
        sg = _sigmoid(zv)
        dz = dov * _gelu(yv) * sg * (1.0 - sg)
        return [dz, dov * sg], [dz]
    dz, dyg, db = _rowwise(name, fn, [dout, y, z], [], [(d, BF16), (d, F32)], [d])
    return dz, dyg, db.sum(0)


def _adam_shard(name, parts, w, m, v):
    r, c = w.shape
    n_parts = sum(1 if p.ndim == 2 else p.shape[0] for p in parts)
    row_bytes = 2 * c * (n_parts * parts[0].dtype.itemsize + 7 * 4)
    tr = min(128, r)
    while tr > SUBLANES and tr * row_bytes > VMEM_LIMIT_BYTES // 2:
        tr //= 2
    assert r % tr == 0
    n_p = len(parts)

    def body(*refs):
        p_refs, (w_ref, m_ref, v_ref, g_out, d_out, m_out, v_out) = refs[:n_p], refs[n_p:]
        g = None
        for p_ref in p_refs:
            terms = [p_ref[...]] if len(p_ref.shape) == 2 else [p_ref[j] for j in range(p_ref.shape[0])]
            for t in terms:
                g = t.astype(F32) if g is None else g + t.astype(F32)
        delta, m_new, v_new = _adam_math(w_ref[...], g, m_ref[...], v_ref[...])
        g_out[...], d_out[...], m_out[...], v_out[...] = g, delta, m_new, v_new

    tile = pl.BlockSpec((tr, c), lambda i: (i, 0))
    p_specs = [tile if p.ndim == 2 else pl.BlockSpec((p.shape[0], tr, c), lambda i: (0, i, 0)) for p in parts]
    return _pcall(name, body, (r // tr,), p_specs + [tile] * 3, [*parts, w, m, v], [tile] * 4, [SDS((r, c), F32)] * 4,
                  semantics=("parallel",))


def _ada_update(name, c_act_t, dmod, w, m, v, tr=128, plan=None):
    r, c = w.shape
    tr = min(tr, r)
    assert r % tr == 0

    def body(c_ref, d_ref, w_ref, m_ref, v_ref, g_out, d_out, m_out, v_out):
        g = jnp.dot(c_ref[...], d_ref[...], precision=HIGHEST, preferred_element_type=F32)
        delta, m_new, v_new = _adam_math(w_ref[...], g, m_ref[...], v_ref[...])
        g_out[...], d_out[...], m_out[...], v_out[...] = g, delta, m_new, v_new

    tile = pl.BlockSpec((tr, c), lambda i: (i, 0))
    in_specs = [pl.BlockSpec((tr, N_DEV), lambda i: (i, 0)), pl.BlockSpec((N_DEV, c), lambda i: (0, 0)), tile, tile, tile]
    return _pcall(name, body, (r // tr,), in_specs, [c_act_t, dmod, w, m, v], [tile] * 4, [SDS((r, c), F32)] * 4,
                  semantics=("parallel",), plan=plan)


def _rotate_half(x):
    w = x.shape[1]
    half = HEAD_DIM // 2
    lane = lax.broadcasted_iota(jnp.int32, x.shape, 1)
    return jnp.where((lane % HEAD_DIM) < half, -pltpu.roll(x, w - half, 1), pltpu.roll(x, half, 1))


def _lane_tile(tab, w):
    return tab[:, :w] if w <= LANES else jnp.tile(tab, (1, w // LANES))


def _rope(x, cos, sin):
    return x * cos + _rotate_half(x) * sin


def _rope_t(dy, cos, sin):
    return dy * cos - _rotate_half(dy) * sin


def _band_mask(n):
    shape = (Q_PER_KV * WINDOW, 2 * WINDOW)
    i = lax.broadcasted_iota(jnp.int32, shape, 0) & (WINDOW - 1)
    j = lax.broadcasted_iota(jnp.int32, shape, 1)
    return (j > i) & (j <= i + WINDOW) & ((n > 0) | (j >= WINDOW))


def _stack_heads(x, hk):
    first = hk * Q_PER_KV
    return jnp.concatenate([x[:, (first + g) * HEAD_DIM:(first + g + 1) * HEAD_DIM] for g in range(Q_PER_KV)], axis=0)


def _stack_cols(ref, hk):
    first = hk * Q_PER_KV
    return jnp.concatenate([ref[:, first + g:first + g + 1] for g in range(Q_PER_KV)], axis=0)


def _stack_sinks(sink_ref, hk):
    first = hk * Q_PER_KV
    return jnp.concatenate([jnp.broadcast_to(sink_ref[0:1, first + g:first + g + 1], (WINDOW, 1))
                            for g in range(Q_PER_KV)], axis=0)


def _attn_specs(da, dkv, nb):
    cur = lambda n: (jnp.minimum(n, nb - 1), 0)
    prev = lambda n: (jnp.maximum(jnp.minimum(n, nb - 1) - 1, 0), 0)
    return dict(
        q=pl.BlockSpec((WINDOW, da), cur), kv_cur=pl.BlockSpec((WINDOW, dkv), cur),
        kv_prev=pl.BlockSpec((WINDOW, dkv), prev), tab_cur=pl.BlockSpec((WINDOW, LANES), cur),
        tab_prev=pl.BlockSpec((WINDOW, LANES), prev))


def _attn_fwd(name, q, k, v, cos, sin, sinks, plan=None):
    s, da = q.shape
    dkv = k.shape[1]
    nq, nb = da // HEAD_DIM, s // WINDOW
    scale = HEAD_DIM ** -0.5

    def body(q_ref, kp_ref, kc_ref, vp_ref, vc_ref, cc_ref, sc_ref, cp_ref, sp_ref, sink_ref, o_ref, lse_ref):
        n = pl.program_id(0)
        cc, sc, cp, sp = cc_ref[...], sc_ref[...], cp_ref[...], sp_ref[...]
        qr = _rope(q_ref[...], _lane_tile(cc, da), _lane_tile(sc, da)).astype(BF16)
        kk = jnp.concatenate([_rope(kp_ref[...], _lane_tile(cp, dkv), _lane_tile(sp, dkv)),
                              _rope(kc_ref[...], _lane_tile(cc, dkv), _lane_tile(sc, dkv))], axis=0).astype(BF16)
        vv = jnp.concatenate([vp_ref[...], vc_ref[...]], axis=0).astype(BF16)
        valid = _band_mask(n)
        for hk in range(nq // Q_PER_KV):
            ks = slice(hk * HEAD_DIM, (hk + 1) * HEAD_DIM)
            sco = lax.dot_general(_stack_heads(qr, hk), kk[:, ks], _DOT_DIMS["nt"], preferred_element_type=F32) * scale
            sco = jnp.where(valid, sco, -1e30)
            sink = _stack_sinks(sink_ref, hk)
            mx = jnp.maximum(jnp.max(sco, axis=1, keepdims=True), sink)
            p = jnp.exp(sco - mx)
            den = jnp.sum(p, axis=1, keepdims=True) + jnp.exp(sink - mx)
            o8 = jnp.dot((p / den).astype(BF16), vv[:, ks], preferred_element_type=F32)
            lse8 = mx + jnp.log(den)
            for g in range(Q_PER_KV):
                hq, rows = hk * Q_PER_KV + g, slice(g * WINDOW, (g + 1) * WINDOW)
                o_ref[:, hq * HEAD_DIM:(hq + 1) * HEAD_DIM] = o8[rows]
                lse_ref[:, hq:hq + 1] = lse8[rows]

    sp_ = _attn_specs(da, dkv, nb)
    in_specs = [sp_["q"], sp_["kv_prev"], sp_["kv_cur"], sp_["kv_prev"], sp_["kv_cur"],
                sp_["tab_cur"], sp_["tab_cur"], sp_["tab_prev"], sp_["tab_prev"], pl.BlockSpec((1, nq), lambda n: (0, 0))]
    return _pcall(name, body, (nb,), in_specs, [q, k, k, v, v, cos, sin, cos, sin, sinks],
                  [sp_["q"], pl.BlockSpec((WINDOW, nq), lambda n: (n, 0))], [SDS((s, da), F32), SDS((s, nq), F32)],
                  semantics=("arbitrary",), plan=plan)


def _attn_bwd(name, q, k, v, cos, sin, sinks, out, lse, dout, plan=None):
    s, da = q.shape
    dkv = k.shape[1]
    nq, nb = da // HEAD_DIM, s // WINDOW
    scale = HEAD_DIM ** -0.5

    def body(q_ref, kp_ref, kc_ref, vp_ref, vc_ref, cc_ref, sc_ref, cp_ref, sp_ref, sink_ref, o_ref, lse_ref,
             do_ref, dq_ref, dk_ref, dv_ref, dsink_ref, dk_carry, dv_carry):
        n = pl.program_id(0)
        cp, sp = _lane_tile(cp_ref[...], dkv), _lane_tile(sp_ref[...], dkv)

        @pl.when(n == 0)
        def _():
            dk_carry[...] = jnp.zeros_like(dk_carry)
            dv_carry[...] = jnp.zeros_like(dv_carry)
            dsink_ref[...] = jnp.zeros_like(dsink_ref)

        @pl.when(n < nb)
        def _():
            cc, sc = cc_ref[...], sc_ref[...]
            qr = _rope(q_ref[...], _lane_tile(cc, da), _lane_tile(sc, da)).astype(BF16)
            kk = jnp.concatenate([_rope(kp_ref[...], cp, sp),
                                  _rope(kc_ref[...], _lane_tile(cc, dkv), _lane_tile(sc, dkv))], axis=0).astype(BF16)
            vv = jnp.concatenate([vp_ref[...], vc_ref[...]], axis=0).astype(BF16)
            valid = _band_mask(n)
            do_all, o_all = do_ref[...], o_ref[...]
            for hk in range(nq // Q_PER_KV):
                ks = slice(hk * HEAD_DIM, (hk + 1) * HEAD_DIM)
                q8, lse8 = _stack_heads(qr, hk), _stack_cols(lse_ref, hk)
                sco = lax.dot_general(q8, kk[:, ks], _DOT_DIMS["nt"], preferred_element_type=F32) * scale
                probs = jnp.where(valid, jnp.exp(sco - lse8), 0.0)
                do8 = _stack_heads(do_all, hk)
                delta = jnp.sum(do8 * _stack_heads(o_all, hk), axis=1, keepdims=True)
                do8 = do8.astype(BF16)
                dp = lax.dot_general(do8, vv[:, ks], _DOT_DIMS["nt"], preferred_element_type=F32)
                ds = (probs * (dp - delta) * scale).astype(BF16)
                dq8 = jnp.dot(ds, kk[:, ks], preferred_element_type=F32)
                dk_h = lax.dot_general(ds, q8, _DOT_DIMS["tn"], preferred_element_type=F32)
                dv_h = lax.dot_general(probs.astype(BF16), do8, _DOT_DIMS["tn"], preferred_element_type=F32)
                dsink8 = -jnp.exp(_stack_sinks(sink_ref, hk) - lse8) * delta
                for g in range(Q_PER_KV):
                    hq, rows = hk * Q_PER_KV + g, slice(g * WINDOW, (g + 1) * WINDOW)
                    dq_ref[:, hq * HEAD_DIM:(hq + 1) * HEAD_DIM] = dq8[rows]
                    dsink_ref[:, hq:hq + 1] += dsink8[rows].reshape(WINDOW // SUBLANES, SUBLANES, 1).sum(axis=0)
                dk_ref[:, ks] = dk_carry[:, ks] + dk_h[:WINDOW]
                dv_ref[:, ks] = dv_carry[:, ks] + dv_h[:WINDOW]
                dk_carry[:, ks] = dk_h[WINDOW:]
                dv_carry[:, ks] = dv_h[WINDOW:]
            dq_ref[...] = _rope_t(dq_ref[...], _lane_tile(cc, da), _lane_tile(sc, da))
            dk_ref[...] = _rope_t(dk_ref[...], cp, sp)

        @pl.when(n == nb)
        def _():
            dk_ref[...] = _rope_t(dk_carry[...], cp, sp)
            dv_ref[...] = dv_carry[...]

    sp_ = _attn_specs(da, dkv, nb)
    last_prev = lambda n: (jnp.maximum(n - 1, 0), 0)
    tab_prev = pl.BlockSpec((WINDOW, LANES), last_prev)
    kv_out = pl.BlockSpec((WINDOW, dkv), last_prev)
    lse_spec = pl.BlockSpec((WINDOW, nq), lambda n: (jnp.minimum(n, nb - 1), 0))
    in_specs = [sp_["q"], sp_["kv_prev"], sp_["kv_cur"], sp_["kv_prev"], sp_["kv_cur"],
                sp_["tab_cur"], sp_["tab_cur"], tab_prev, tab_prev,
                pl.BlockSpec((1, nq), lambda n: (0, 0)), sp_["q"], lse_spec, sp_["q"]]
    res = _pcall(name, body, (nb + 1,), in_specs, [q, k, k, v, v, cos, sin, cos, sin, sinks, out, lse, dout],
                 [sp_["q"], kv_out, kv_out, pl.BlockSpec((SUBLANES, nq), lambda n: (0, 0))],
                 [SDS((s, da), F32), SDS((s, dkv), F32), SDS((s, dkv), F32), SDS((SUBLANES, nq), F32)],
                 scratch=[pltpu.VMEM((WINDOW, dkv), F32), pltpu.VMEM((WINDOW, dkv), F32)],
                 semantics=("arbitrary",), plan=plan)
    (dq, dk, dv, dsink), rest = res if plan is not None else (res, None)
    return (dq, dk, dv, dsink.sum(0)), rest


def _ssm_operators(lam_re, lam_im, log_step, b_re, b_im, c_re, c_im, d_skip):
    g, p = lam_re.shape
    h = b_re.shape[-1]
    l = SSM_CHUNK
    step = jnp.exp(log_step)[:, None]
    mag = jnp.exp(lam_re * step)
    ar, ai = mag * jnp.cos(lam_im * step), mag * jnp.sin(lam_im * step)
    den = lam_re * lam_re + lam_im * lam_im
    cr = ((ar - 1.0) * lam_re + ai * lam_im) / den
    ci = (ai * lam_re - (ar - 1.0) * lam_im) / den
    bbr = cr[..., None] * b_re - ci[..., None] * b_im
    bbi = cr[..., None] * b_im + ci[..., None] * b_re
    pr, pi = [jnp.ones_like(ar)], [jnp.zeros_like(ar)]
    for _ in range(l):
        pr, pi = pr + [pr[-1] * ar - pi[-1] * ai], pi + [pr[-1] * ai + pi[-1] * ar]
    pwr, pwi = jnp.stack(pr, axis=1), jnp.stack(pi, axis=1)
    cpr = c_re[:, None] * pwr[:, :, None, :] - c_im[:, None] * pwi[:, :, None, :]
    cpi = c_re[:, None] * pwi[:, :, None, :] + c_im[:, None] * pwr[:, :, None, :]
    kern = (jnp.einsum("gtop,gpi->gtoi", cpr[:, :l], bbr, precision=HIGHEST)
            - jnp.einsum("gtop,gpi->gtoi", cpi[:, :l], bbi, precision=HIGHEST))
    kern = kern.at[:, 0].add(d_skip.reshape(g, h)[:, :, None] * jnp.eye(h, dtype=F32))
    tm = jnp.stack([jnp.pad(kern[:, :l - j], ((0, 0), (j, 0), (0, 0), (0, 0))) for j in range(l)], axis=1)
    tm = tm.transpose(0, 1, 4, 2, 3).reshape(g, l * h, l * h)
    rev_r, rev_i = pwr[:, l - 1::-1][:, :l], pwi[:, l - 1::-1][:, :l]
    er = rev_r[:, :, None, :] * bbr.transpose(0, 2, 1)[:, None] - rev_i[:, :, None, :] * bbi.transpose(0, 2, 1)[:, None]
    ei = rev_r[:, :, None, :] * bbi.transpose(0, 2, 1)[:, None] + rev_i[:, :, None, :] * bbr.transpose(0, 2, 1)[:, None]
    em = jnp.concatenate([er, ei], axis=-1).reshape(g, l * h, 2 * p)
    fr = cpr[:, 1:].transpose(0, 3, 1, 2).reshape(g, p, l * h)
    fi = -cpi[:, 1:].transpose(0, 3, 1, 2).reshape(g, p, l * h)
    fm = jnp.concatenate([fr, fi], axis=1)
    return tm, em, fm, pwr[:, l], pwi[:, l]


def _decay_lanes(alr, ali):
    return jnp.concatenate([alr, alr], axis=1), jnp.concatenate([-ali, ali], axis=1)


def _ssm_fwd(name, u, tm, em, fm, acat, bcat, plan=None):
    s, ds = u.shape
    g, lh, p2 = em.shape
    gb, h = SSM_GROUPS_PER_STEP, lh // SSM_CHUNK
    assert gb * h == LANES and g * h == ds and s % SSM_CHUNK == 0
    nc, half = s // SSM_CHUNK, p2 // 2

    def body(u_ref, tm_ref, em_ref, fm_ref, a_ref, b_ref, y_ref, xp_ref, uc_ref, yc_ref, st_ref):
        _to_chunks(u_ref, uc_ref, nc, h)
        for i in range(gb):
            st_ref[pl.ds(i, nc, stride=gb), :] = jnp.dot(uc_ref[i], em_ref[i], precision=HIGHEST,
                                                         preferred_element_type=F32)
        av, bv = a_ref[...], b_ref[...]

        def step(c, carry):
            x, xs = carry
            rows = pl.ds(pl.multiple_of(c * gb, gb), gb)
            loc = st_ref[rows, :]
            st_ref[rows, :] = x
            return av * x + bv * xs + loc, av * xs - bv * x + pltpu.roll(loc, half, 1)
        zero = jnp.zeros((gb, p2), F32)
        lax.fori_loop(0, nc, step, (zero, zero), unroll=4)
        for i in range(gb):
            xp = st_ref[pl.ds(i, nc, stride=gb), :]
            xp_ref[i] = xp
            yc_ref[i] = (jnp.dot(uc_ref[i], tm_ref[i], precision=HIGHEST, preferred_element_type=F32)
                         + jnp.dot(xp, fm_ref[i], precision=HIGHEST, preferred_element_type=F32))
        _from_chunks(yc_ref, y_ref, nc, h)

    blk = lambda r, c: pl.BlockSpec((gb, r, c), lambda i: (i, 0, 0))
    vec = pl.BlockSpec((gb, p2), lambda i: (i, 0))
    col = pl.BlockSpec((s, LANES), lambda i: (0, i))
    return _pcall(name, body, (g // gb,), [col, blk(lh, lh), blk(lh, p2), blk(p2, lh), vec, vec],
                  [u, tm, em, fm, acat, bcat], [col, blk(nc, p2), blk(nc, lh)],
                  [SDS((s, ds), F32), SDS((g, nc, p2), F32), SDS((g, nc, lh), F32)],
                  scratch=[pltpu.VMEM((gb, nc, lh), F32), pltpu.VMEM((nc * gb, p2), F32)],
                  semantics=("parallel",), plan=plan)


def _ssm_bwd(name, u_chunks, dy, xprev, tm, em, fm, acat, bcat, plan=None):
    s, ds = dy.shape
    g, lh, p2 = em.shape
    gb, h = SSM_GROUPS_PER_STEP, lh // SSM_CHUNK
    nc, half = s // SSM_CHUNK, p2 // 2

    def body(uc_ref, dy_ref, xp_ref, tm_ref, em_ref, fm_ref, a_ref, b_ref,
             du_ref, dtm_ref, dem_ref, dfm_ref, r1_ref, r2_ref, dyc_ref, duc_ref, gs_ref, xs_ref):
        _to_chunks(dy_ref, dyc_ref, nc, h)
        for i in range(gb):
            gs_ref[pl.ds(i, nc, stride=gb), :] = lax.dot_general(
                dyc_ref[i], fm_ref[i], _DOT_DIMS["nt"], precision=HIGHEST, preferred_element_type=F32)
            xs_ref[pl.ds(i, nc, stride=gb), :] = xp_ref[i]
        av, bv = a_ref[...], b_ref[...]

        def step(t, carry):
            grad, gsw, r1, r2 = carry
            c = nc - 1 - t
            rows = pl.ds(pl.multiple_of(c * gb, gb), gb)
            dxp, xp = gs_ref[rows, :], xs_ref[rows, :]
            gs_ref[rows, :] = grad
            r1 = r1 + grad * xp
            r2 = r2 + grad * pltpu.roll(xp, half, 1)
            return dxp + av * grad - bv * gsw, pltpu.roll(dxp, half, 1) + av * gsw + bv * grad, r1, r2
        zero = jnp.zeros((gb, p2), F32)
        _, _, r1, r2 = lax.fori_loop(0, nc, step, (zero, zero, zero, zero), unroll=4)
        r1_ref[...], r2_ref[...] = r1, r2
        for i in range(gb):
            dxl = gs_ref[pl.ds(i, nc, stride=gb), :]
            duc_ref[i] = (lax.dot_general(dyc_ref[i], tm_ref[i], _DOT_DIMS["nt"], precision=HIGHEST,
                                          preferred_element_type=F32)
                          + lax.dot_general(dxl, em_ref[i], _DOT_DIMS["nt"], precision=HIGHEST,
                                            preferred_element_type=F32))
            dtm_ref[i] = lax.dot_general(uc_ref[i], dyc_ref[i], _DOT_DIMS["tn"], precision=HIGHEST,
                                         preferred_element_type=F32)
            dfm_ref[i] = lax.dot_general(xp_ref[i], dyc_ref[i], _DOT_DIMS["tn"], precision=HIGHEST,
                                         preferred_element_type=F32)
            dem_ref[i] = lax.dot_general(uc_ref[i], dxl, _DOT_DIMS["tn"], precision=HIGHEST,
                                         preferred_element_type=F32)
        _from_chunks(duc_ref, du_ref, nc, h)

    blk = lambda r, c: pl.BlockSpec((gb, r, c), lambda i: (i, 0, 0))
    vec = pl.BlockSpec((gb, p2), lambda i: (i, 0))
    col = pl.BlockSpec((s, LANES), lambda i: (0, i))
    chunked = pltpu.VMEM((gb, nc, lh), F32)
    res = _pcall(name, body, (g // gb,),
                 [blk(nc, lh), col, blk(nc, p2), blk(lh, lh), blk(lh, p2), blk(p2, lh), vec, vec],
                 [u_chunks, dy, xprev, tm, em, fm, acat, bcat],
                 [col, blk(lh, lh), blk(lh, p2), blk(p2, lh), vec, vec],
                 [SDS((s, ds), F32), SDS((g, lh, lh), F32), SDS((g, lh, p2), F32), SDS((g, p2, lh), F32),
                  SDS((g, p2), F32), SDS((g, p2), F32)],
                 scratch=[chunked, chunked, pltpu.VMEM((nc * gb, p2), F32), pltpu.VMEM((nc * gb, p2), F32)],
                 semantics=("parallel",), plan=plan)
    return res if plan is not None else (res, None)


def _to_chunks(src_ref, dst_ref, nc, h):
    per = LANES // h
    grp = lax.broadcasted_iota(jnp.int32, (nc, LANES), 1) // h
    for g in range(per):
        for part in range(SSM_CHUNK * h // LANES):
            acc = None
            for i in range(part * per, (part + 1) * per):
                piece = src_ref[pl.ds(i, nc, stride=SSM_CHUNK), :]
                lo = (i * h) % LANES
                if (lo - g * h) % LANES:
                    piece = pltpu.roll(piece, (lo - g * h) % LANES, 1)
                acc = piece if acc is None else jnp.where(grp == lo // h, piece, acc)
            dst_ref[g, :, part * LANES:(part + 1) * LANES] = acc


def _from_chunks(src_ref, dst_ref, nc, h):
    per = LANES // h
    grp = lax.broadcasted_iota(jnp.int32, (nc, LANES), 1) // h
    for i in range(SSM_CHUNK):
        part, lo = divmod(i * h, LANES)
        row = None
        for g in range(per):
            piece = src_ref[g, :, part * LANES:(part + 1) * LANES]
            if (g * h - lo) % LANES:
                piece = pltpu.roll(piece, (g * h - lo) % LANES, 1)
            row = piece if row is None else jnp.where(grp == g, piece, row)
        dst_ref[pl.ds(i, nc, stride=SSM_CHUNK), :] = row


_SMALL = ("b_ada", "norm1_g", "sinks", "ssm_lam_re", "ssm_lam_im", "ssm_log_step", "ssm_b_re", "ssm_b_im",
          "ssm_c_re", "ssm_c_im", "ssm_d", "b_glu", "attn_out_g", "ssm_out_g", "norm2_g", "final_g")
_WEIGHTS = ("w_ada", "b_ada", "norm1_g", "w_in", "sinks", "ssm_lam_re", "ssm_lam_im", "ssm_log_step", "ssm_b_re",
            "ssm_b_im", "ssm_c_re", "ssm_c_im", "ssm_d", "w_glu", "b_glu", "attn_out_g", "ssm_out_g", "w_out",
            "norm2_g", "w_ff1", "w_ff2", "final_g")
_PACK_ALIGN = 128 * LANES


def _pack(parts):
    flat = jnp.concatenate([p.reshape(-1).astype(F32) for p in parts])
    pad = (-flat.shape[0]) % _PACK_ALIGN
    return jnp.pad(flat, (0, pad)).reshape(-1, LANES)


def kernel(x, c, w_ada, b_ada, norm1_g, w_in, sinks, ssm_lam_re, ssm_lam_im, ssm_log_step, ssm_b_re, ssm_b_im, ssm_c_re, ssm_c_im, ssm_d, w_glu, b_glu, attn_out_g, ssm_out_g, w_out, norm2_g, w_ff1, w_ff2, final_g, loss_target, m_w_ada, m_b_ada, m_norm1_g, m_w_in, m_sinks, m_ssm_lam_re, m_ssm_lam_im, m_ssm_log_step, m_ssm_b_re, m_ssm_b_im, m_ssm_c_re, m_ssm_c_im, m_ssm_d, m_w_glu, m_b_glu, m_attn_out_g, m_ssm_out_g, m_w_out, m_norm2_g, m_w_ff1, m_w_ff2, m_final_g, v_w_ada, v_b_ada, v_norm1_g, v_w_in, v_sinks, v_ssm_lam_re, v_ssm_lam_im, v_ssm_log_step, v_ssm_b_re, v_ssm_b_im, v_ssm_c_re, v_ssm_c_im, v_ssm_d, v_w_glu, v_b_glu, v_attn_out_g, v_ssm_out_g, v_w_out, v_norm2_g, v_w_ff1, v_w_ff2, v_final_g):
    args = dict(locals())
    weights = {n: args[n] for n in _WEIGHTS}
    mom = {n: args["m_" + n] for n in _WEIGHTS}
    var = {n: args["v_" + n] for n in _WEIGHTS}
    me = 4 * lax.axis_index("x") + 2 * lax.axis_index("y") + lax.axis_index("c")

    _, s, d = x.shape
    xs, tgt = x[0], loss_target[0]
    d_ssm = ssm_d.shape[-1]
    d_attn = d - d_ssm
    nq = d_attn // HEAD_DIM
    d_kv = (nq // Q_PER_KV) * HEAD_DIM
    p_state = ssm_b_re.shape[2]

    c_all, g_in = _run_plan("gather_c_w_in", _Gather([c, w_in[0].T.astype(BF16)]))
    c_all = c_all.reshape(N_DEV, d)
    w_in_t = g_in.reshape(-1, d)

    n_loc = w_ada.shape[-1]
    b_loc = lax.dynamic_slice_in_dim(b_ada, me * n_loc, n_loc, axis=1)
    silu = lambda t: t * _sigmoid(t)
    mod_part = _matmul("ada_mod", c_all, w_ada[0], "nn", [F32], a_pro=silu, vecs=[b_loc], exact=True,
                       epilogue=lambda acc, e, v: (acc + v[0],), tn=512, tk=d)
    mod_all = _run_plan("gather_mod", _Gather([mod_part]))[0]
    mod = lax.dynamic_index_in_dim(mod_all, me, axis=1, keepdims=False).reshape(N_MOD, 1, d)
    shift1, scale1, gate1, shift2, scale2, gate2 = [mod[i] for i in range(N_MOD)]

    h1 = _norm_mod_fwd("norm1", xs, norm1_g, scale1, shift1)
    q = _matmul("proj_q", h1, w_in_t, "nt", [F32], b_rows=(0, d_attn))
    kv = _matmul("proj_kv", h1, w_in_t, "nt", [F32], b_rows=(d_attn, 2 * d_kv))
    u = _matmul("proj_u", h1, w_in_t, "nt", [F32], b_rows=(d_attn + 2 * d_kv, d_ssm))
    k, v = kv[:, :d_kv], kv[:, d_kv:]

    half = HEAD_DIM // 2
    inv_freq = ROPE_THETA ** (-jnp.arange(half, dtype=F32) / half)
    ang = jnp.arange(s, dtype=F32)[:, None] * inv_freq[None, :]
    cos_t, sin_t = jnp.tile(jnp.cos(ang), (1, 4)), jnp.tile(jnp.sin(ang), (1, 4))
    (attn, lse), (g_glu, g_out) = _attn_fwd("attn_fwd", q, k, v, cos_t, sin_t, sinks,
                                            plan=_Gather([w_glu[0].astype(BF16), w_out[0].astype(BF16)]))
    w_glu_f = g_glu.reshape(d_ssm, d_ssm)
    w_out_f = g_out.reshape(d, d)

    ssm_params = (ssm_lam_re[0], ssm_lam_im[0], ssm_log_step[0], ssm_b_re[0], ssm_b_im[0], ssm_c_re[0],
                  ssm_c_im[0], ssm_d[0])
    (tm_op, em_op, fm_op, alr, ali), ssm_vjp = jax.vjp(_ssm_operators, *ssm_params)
    acat, bcat = _decay_lanes(alr, ali)
    (y_ssm, x_prev, u_chunks), (g_ff1,) = _ssm_fwd("ssm_fwd", u, tm_op, em_op, fm_op, acat, bcat,
                                                   plan=_Gather([w_ff1[0].astype(BF16)]))
    yg = _gelu_fwd("gelu", y_ssm)
    ssm_out, z_glu = _matmul(
        "glu", yg, w_glu_f, "nn", [F32, F32], extras=[y_ssm], vecs=[b_glu],
        epilogue=lambda acc, e, v: (_gelu(e[0]) * _sigmoid(acc + v[0]), acc + v[0]))
    mixed = _group_norm_fwd("group_norm", attn, ssm_out, attn_out_g, ssm_out_g)
    x2, mo = _matmul("out_proj", mixed, w_out_f, "nn", [F32, BF16], extras=[xs], vecs=[gate1],
                     epilogue=lambda acc, e, v: (e[0] + v[0] * acc, acc))

    h2 = _norm_mod_fwd("norm2", x2, norm2_g, scale2, shift2)
    (a_ff, f_ff), (g_ff2,) = _matmul("ff1", h2, g_ff1, "nn", [BF16, BF16], b_blocked=True,
                                     epilogue=lambda acc, e, v: (acc, jnp.square(jnp.maximum(acc, 0.0))),
                                     plan=_Gather([w_ff2[0].astype(BF16)]))
    w_ff2_f = g_ff2.reshape(-1, d)
    x3, ff = _matmul("ff2", f_ff, w_ff2_f, "nn", [F32, BF16], extras=[x2], vecs=[gate2],
                     epilogue=lambda acc, e, v: (e[0] + v[0] * acc, acc))

    dx3, loss_local, d_final_g = _final_loss("final_loss", x3, tgt, final_g.reshape(1, d))
    loss = lax.psum(loss_local, MESH_AXES)

    dff, d_gate2 = _gate_bwd("gate2_bwd", dx3, ff, gate2)
    dw_ff2 = _matmul("ff2_dw", f_ff, dff, "tn", [BF16]).reshape(N_DEV, -1, d)
    da_ff, (r_ff2_a,) = _matmul("ff2_dx", dff, w_ff2_f, "nt", [BF16], extras=[a_ff],
                                epilogue=lambda acc, e, v: (acc * (2.0 * jnp.maximum(e[0].astype(F32), 0.0)),),
                                plan=_Exchange([dw_ff2], RELATIONS_SAME_CORE))
    dw_ff1, (r_ff2_b,) = _matmul("ff1_dw", h2, da_ff, "tn", [BF16], out_blocked=N_DEV,
                                 plan=_Exchange([dw_ff2], RELATIONS_OTHER_CORE))
    dh2, (r_ff1_a,) = _matmul("ff1_dx", da_ff, g_ff1, "nt", [F32], b_blocked=True,
                              plan=_Exchange([dw_ff1], RELATIONS_SAME_CORE))
    (dx2, d_scale2, d_shift2, d_norm2_g), _ = _norm_mod_bwd("norm2_bwd", x2, dh2, dx3, norm2_g, scale2)

    dmo, d_gate1 = _gate_bwd("gate1_bwd", dx2, mo, gate1)
    dw_out = _matmul("out_dw", mixed, dmo, "tn", [BF16]).reshape(N_DEV, -1, d)
    dmixed, (r_ff1_b,) = _matmul("out_dx", dmo, w_out_f, "nt", [F32], plan=_Exchange([dw_ff1], (5,)))
    dattn, dssm_out, d_attn_g, d_ssm_g = _group_norm_bwd("group_norm_bwd", attn, ssm_out, dmixed, attn_out_g, ssm_out_g)

    dz, dyg_direct, d_b_glu = _glu_bwd("glu_bwd", dssm_out, y_ssm, z_glu)
    dw_glu = _matmul("glu_dw", yg, dz, "tn", [BF16]).reshape(N_DEV, -1, d_ssm)
    dy_ssm = _matmul("glu_dx", dz, w_glu_f, "nt", [F32], extras=[dyg_direct, y_ssm],
                     epilogue=lambda acc, e, v: ((acc + e[0]) * _gelu_grad(e[1]),))
    (du, d_tm, d_em, d_fm, r1, r2), (r_ff1_c, r_out_a) = _ssm_bwd(
        "ssm_bwd", u_chunks, dy_ssm, x_prev, tm_op, em_op, fm_op, acat, bcat,
        plan=_Plans([_Exchange([dw_ff1], (3, 7)), _Exchange([dw_out], (1, 4))]))
    d_alr = r1[:, :p_state] + r1[:, p_state:]
    d_ali = r2[:, p_state:] - r2[:, :p_state]
    d_ssm_params = ssm_vjp((d_tm, d_em, d_fm, d_alr, d_ali))

    (dq, dk, dv, d_sinks), (r_out_b,) = _attn_bwd("attn_bwd", q, k, v, cos_t, sin_t, sinks, attn, lse, dattn,
                                                  plan=_Exchange([dw_out], (2, 6, 5, 3, 7)))
    dproj = jnp.concatenate([dq, dk, dv, du], axis=1).astype(BF16)
    dw_in_t, (r_glu,) = _matmul("in_dw", dproj, h1, "tn", [BF16], plan=_Exchange([dw_glu], RELATIONS_ALL))
    dw_in_t = dw_in_t.reshape(N_DEV, -1, d)
    dh1, (r_in_a,) = _matmul("in_dx", dproj, w_in_t, "nn", [F32], plan=_Exchange([dw_in_t], RELATIONS_SAME_CORE))
    (grad_x, d_scale1, d_shift1, d_norm1_g), _ = _norm_mod_bwd("norm1_bwd", xs, dh1, dx2, norm1_g, scale1)

    d_mod = jnp.concatenate([d_shift1, d_scale1, d_gate1, d_shift2, d_scale2, d_gate2])
    small_g = dict(zip(("ssm_lam_re", "ssm_lam_im", "ssm_log_step", "ssm_b_re", "ssm_b_im", "ssm_c_re", "ssm_c_im",
                        "ssm_d"), d_ssm_params, strict=True))
    small_g.update(b_ada=d_mod, norm1_g=d_norm1_g, sinks=d_sinks, b_glu=d_b_glu, attn_out_g=d_attn_g,
                   ssm_out_g=d_ssm_g, norm2_g=d_norm2_g, final_g=d_final_g)
    small_parts, r_in_b = _run_plan("gather_small_grads", _Plans([_Gather([_pack([small_g[n] for n in _SMALL])]),
                                                                  _Exchange([dw_in_t], RELATIONS_OTHER_CORE)]))
    small = _adam_shard("adam_small", [small_parts], _pack([weights[n] for n in _SMALL]),
                        _pack([mom[n] for n in _SMALL]), _pack([var[n] for n in _SMALL]))
    out = {}
    off = 0
    for n in _SMALL:
        size = weights[n].size
        out[n] = [t.reshape(-1)[off:off + size].reshape(weights[n].shape) for t in small]
        off += size

    dmod_all = small_parts.reshape(N_DEV, -1)[:, :N_MOD * d]
    dmod_loc = lax.dynamic_slice_in_dim(dmod_all, me * n_loc, n_loc, axis=1)
    c_act_t = silu(c_all).T
    out["w_ada"] = [t[None] for t in _ada_update("adam_w_ada", c_act_t, dmod_loc, w_ada[0], m_w_ada[0], v_w_ada[0])]

    mine = lambda blocks: lax.dynamic_index_in_dim(blocks, me, axis=0, keepdims=False)
    in_parts = [mine(dw_in_t).T] + [r.transpose(0, 2, 1) for r in (r_in_a, r_in_b)]
    received = dict(w_in=in_parts, w_glu=[mine(dw_glu), r_glu], w_out=[mine(dw_out), r_out_a, r_out_b],
                    w_ff1=[mine(dw_ff1), r_ff1_a, r_ff1_b, r_ff1_c], w_ff2=[mine(dw_ff2), r_ff2_a, r_ff2_b])
    for n, parts in received.items():
        out[n] = [t[None] for t in _adam_shard("adam_" + n, parts, weights[n][0], mom[n][0], var[n][0])]

    return (loss, grad_x[None], *[out[n][0] for n in _WEIGHTS], *[out[n][1] for n in _WEIGHTS],
            *[out[n][2] for n in _WEIGHTS], *[out[n][3] for n in _WEIGHTS])
```

```python
import functools
import math
import operator

import jax
import jax.numpy as jnp
from jax import lax
from jax.experimental import pallas as pl
from jax.experimental.pallas import tpu as pltpu

F32, BF16 = jnp.float32, jnp.bfloat16
SDS = jax.ShapeDtypeStruct
MESH_AXES = ("x", "y", "c")
N_DEV = 8
VMEM_LIMIT_BYTES = 56 * 1024 * 1024
SUBLANES, LANES = 8, 128

HEAD_DIM = 64
Q_PER_KV = 8
WINDOW = 128
ROPE_THETA = 10000.0
EPS = 1e-6
N_MOD = 6
SSM_CHUNK = 16
SSM_GROUPS_PER_STEP = 8

ADAM_LR, ADAM_B1, ADAM_B2, ADAM_EPS, ADAM_WD, ADAM_STEP = 0.001, 0.9, 0.999, 1e-08, 0.01, 10
HIGHEST = lax.Precision.HIGHEST

RELATIONS_ALL = (1, 4, 2, 6, 5, 3, 7)
RELATIONS_SAME_CORE = (1, 4, 2, 6)
RELATIONS_OTHER_CORE = (5, 3, 7)


def _cparams(sem):
    return pltpu.CompilerParams(dimension_semantics=sem, vmem_limit_bytes=VMEM_LIMIT_BYTES)


def _block_index(p):
    return 4 * p[0] + 2 * p[1] + p[2]


def _me():
    return lax.axis_index("x"), lax.axis_index("y"), lax.axis_index("c")


class _Gather:
    def __init__(self, arrs):
        self.ins = list(arrs)
        self.out_shapes = [SDS((N_DEV,) + a.shape, a.dtype) for a in arrs]
        self.n_rdma, self.n_local = 7 * len(arrs), len(arrs)
        self.rdma_base = self.local_base = 0

    def _copy(self, ins, outs, send, recv, a, k, block, to, from_input=False):
        dst = outs[a].at[_block_index(block)]
        sem = self.rdma_base + a * 7 + k
        return pltpu.make_async_remote_copy(
            src_ref=ins[a] if from_input else dst, dst_ref=dst, send_sem=send.at[sem], recv_sem=recv.at[sem],
            device_id=to, device_id_type=pl.DeviceIdType.MESH)

    def _first(self, ins, outs, send, recv, a):
        x, y, c = _me()
        chips = [(1 - x, y), (x, 1 - y), (1 - x, 1 - y)]
        cps = [self._copy(ins, outs, send, recv, a, 0, (x, y, c), (x, y, 1 - c), True)]
        return cps + [self._copy(ins, outs, send, recv, a, 1 + j, (x, y, c), (*chip, c), True)
                      for j, chip in enumerate(chips)]

    def _mine(self, ins, outs, local, a):
        return pltpu.make_async_copy(ins[a], outs[a].at[_block_index(_me())], local.at[self.local_base + a])

    def start(self, ins, outs, send, recv, local):
        for a in range(len(ins)):
            self._mine(ins, outs, local, a).start()
            for cp in self._first(ins, outs, send, recv, a):
                cp.start()

    def finish(self, ins, outs, send, recv, local):
        x, y, c = _me()
        me, sibling = (x, y, c), (x, y, 1 - c)
        chips = [(1 - x, y), (x, 1 - y), (1 - x, 1 - y)]
        forwards = []
        for a in range(len(ins)):
            for j, chip in enumerate(chips):
                self._copy(ins, outs, send, recv, a, 1 + j, (*chip, c), me).wait_recv()
                fwd = self._copy(ins, outs, send, recv, a, 4 + j, (*chip, c), sibling)
                fwd.start()
                forwards.append(fwd)
        for a in range(len(ins)):
            self._copy(ins, outs, send, recv, a, 0, sibling, me).wait_recv()
            for j, chip in enumerate(chips):
                self._copy(ins, outs, send, recv, a, 4 + j, (*chip, 1 - c), me).wait_recv()
            for cp in self._first(ins, outs, send, recv, a):
                cp.wait_send()
            self._mine(ins, outs, local, a).wait()
        for fwd in forwards:
            fwd.wait_send()


class _Exchange:
    def __init__(self, arrs, relations):
        self.ins, self.relations = list(arrs), tuple(relations)
        self.out_shapes = [SDS((len(relations),) + a.shape[1:], a.dtype) for a in arrs]
        self.n_rdma, self.n_local = len(relations) * len(arrs), 0
        self.rdma_base = self.local_base = 0

    def _copies(self, ins, outs, send, recv):
        x, y, c = _me()
        cps = []
        for a in range(len(ins)):
            for s, k in enumerate(self.relations):
                peer = ((1 - x) if (k & 4) else x, (1 - y) if (k & 2) else y, (1 - c) if (k & 1) else c)
                sem = self.rdma_base + a * len(self.relations) + s
                cps.append(pltpu.make_async_remote_copy(
                    src_ref=ins[a].at[_block_index(peer)], dst_ref=outs[a].at[s], send_sem=send.at[sem],
                    recv_sem=recv.at[sem], device_id=peer, device_id_type=pl.DeviceIdType.MESH))
        return cps

    def start(self, ins, outs, send, recv, local):
        for cp in self._copies(ins, outs, send, recv):
            cp.start()

    def finish(self, ins, outs, send, recv, local):
        for cp in self._copies(ins, outs, send, recv):
            cp.wait()


class _PairSwap:
    def __init__(self, arrs):
        self.ins = list(arrs)
        self.out_shapes = [SDS((4,) + a.shape[1:], a.dtype) for a in arrs]
        self.n_rdma, self.n_local = 4 * len(arrs), 0
        self.rdma_base = self.local_base = 0

    def _copies(self, ins, outs, send, recv):
        x, y, c = _me()
        cps = []
        for a in range(len(ins)):
            for s in range(4):
                sem = self.rdma_base + a * 4 + s
                cps.append(pltpu.make_async_remote_copy(
                    src_ref=ins[a].at[2 * s + (1 - c)], dst_ref=outs[a].at[s], send_sem=send.at[sem],
                    recv_sem=recv.at[sem], device_id=(x, y, 1 - c), device_id_type=pl.DeviceIdType.MESH))
        return cps

    def start(self, ins, outs, send, recv, local):
        for cp in self._copies(ins, outs, send, recv):
            cp.start()

    def finish(self, ins, outs, send, recv, local):
        for cp in self._copies(ins, outs, send, recv):
            cp.wait()


CHIP_X, CHIP_Y, CHIP_DIAGONAL = (1, 0), (0, 1), (1, 1)


class _ChipExchange:
    def __init__(self, arrs, hops):
        self.ins, self.hops = list(arrs), tuple(hops)
        self.out_shapes = [SDS((len(hops),) + a.shape[1:], a.dtype) for a in arrs]
        self.n_rdma, self.n_local = len(hops) * len(arrs), 0
        self.rdma_base = self.local_base = 0

    def _copies(self, ins, outs, send, recv):
        x, y, c = _me()
        cps = []
        for a in range(len(ins)):
            for s, (fx, fy) in enumerate(self.hops):
                px, py = (1 - x) if fx else x, (1 - y) if fy else y
                sem = self.rdma_base + a * len(self.hops) + s
                cps.append(pltpu.make_async_remote_copy(
                    src_ref=ins[a].at[2 * px + py], dst_ref=outs[a].at[s], send_sem=send.at[sem],
                    recv_sem=recv.at[sem], device_id=(px, py, c), device_id_type=pl.DeviceIdType.MESH))
        return cps

    def start(self, ins, outs, send, recv, local):
        for cp in self._copies(ins, outs, send, recv):
            cp.start()

    def finish(self, ins, outs, send, recv, local):
        for cp in self._copies(ins, outs, send, recv):
            cp.wait()


class _Plans:
    def __init__(self, plans):
        self.plans = list(plans)
        self.ins = [a for p in plans for a in p.ins]
        self.out_shapes = [s for p in plans for s in p.out_shapes]
        self.n_rdma = self.n_local = 0
        for p in plans:
            p.rdma_base, p.local_base = self.n_rdma, self.n_local
            self.n_rdma, self.n_local = self.n_rdma + p.n_rdma, self.n_local + p.n_local

    def _each(self, ins, outs):
        i = o = 0
        for p in self.plans:
            yield p, ins[i:i + len(p.ins)], outs[o:o + len(p.out_shapes)]
            i, o = i + len(p.ins), o + len(p.out_shapes)

    def start(self, ins, outs, send, recv, local):
        for p, pi, po in self._each(ins, outs):
            p.start(pi, po, send, recv, local)

    def finish(self, ins, outs, send, recv, local):
        for p, pi, po in self._each(ins, outs):
            p.finish(pi, po, send, recv, local)


def _plan_scratch(plan):
    return [pltpu.SemaphoreType.DMA((plan.n_rdma,)), pltpu.SemaphoreType.DMA((plan.n_rdma,)),
            pltpu.SemaphoreType.DMA((max(plan.n_local, 1),))]


def _run_plan(name, plan):
    n = len(plan.ins)

    def body(*refs):
        ins, outs, sems = refs[:n], refs[n:len(refs) - 3], refs[len(refs) - 3:]
        plan.start(ins, outs, *sems)
        plan.finish(ins, outs, *sems)

    any_spec = pl.BlockSpec(memory_space=pl.ANY)
    return pl.pallas_call(body, name=name, out_shape=list(plan.out_shapes), in_specs=[any_spec] * n,
                          out_specs=[any_spec] * len(plan.out_shapes), scratch_shapes=_plan_scratch(plan))(*plan.ins)


def _pcall(name, body, grid, in_specs, ins, out_specs, out_shape, scratch=(), semantics=None, plan=None):
    if plan is None:
        return pl.pallas_call(body, name=name, grid=grid, in_specs=list(in_specs), out_specs=list(out_specs),
                              out_shape=list(out_shape), scratch_shapes=list(scratch),
                              compiler_params=_cparams(semantics))(*ins)
    n_in, n_out, n_scr = len(ins), len(out_shape), len(scratch)
    p_in, p_out = len(plan.ins), len(plan.out_shapes)

    def with_plan(*refs):
        k_in, c_in = refs[:n_in], refs[n_in:n_in + p_in]
        refs = refs[n_in + p_in:]
        k_out, c_out = refs[:n_out], refs[n_out:n_out + p_out]
        refs = refs[n_out + p_out:]
        k_scr, sems = refs[:n_scr], refs[n_scr:]
        ids = [pl.program_id(d) for d in range(len(grid))]
        first = functools.reduce(operator.and_, [i == 0 for i in ids])
        last = functools.reduce(operator.and_, [i == g - 1 for i, g in zip(ids, grid)])

        @pl.when(first)
        def _():
            plan.start(c_in, c_out, *sems)

        body(*k_in, *k_out, *k_scr)

        @pl.when(last)
        def _():
            plan.finish(c_in, c_out, *sems)

    any_spec = pl.BlockSpec(memory_space=pl.ANY)
    res = pl.pallas_call(
        with_plan, name=name, grid=grid, in_specs=list(in_specs) + [any_spec] * p_in,
        out_specs=list(out_specs) + [any_spec] * p_out, out_shape=list(out_shape) + list(plan.out_shapes),
        scratch_shapes=list(scratch) + _plan_scratch(plan),
        compiler_params=_cparams(("arbitrary",) * len(grid)))(*ins, *plan.ins)
    return res[:n_out], res[n_out:]


def _rowwise(name, fn, rows, vecs, row_outs, acc_outs=(), tm=128, plan=None):
    t = rows[0].shape[0]
    tm = min(tm, t)
    assert t % tm == 0 and tm % SUBLANES == 0
    n_r, n_v, n_o = len(rows), len(vecs), len(row_outs)

    def body(*refs):
        r_in, v_in = refs[:n_r], refs[n_r:n_r + n_v]
        r_out, a_out = refs[n_r + n_v:n_r + n_v + n_o], refs[n_r + n_v + n_o:]
        outs, accs = fn([r[...] for r in r_in], [v[...] for v in v_in])
        for o_ref, o in zip(r_out, outs, strict=True):
            o_ref[...] = o.astype(o_ref.dtype)
        if a_out:
            @pl.when(pl.program_id(0) == 0)
            def _():
                for a_ref in a_out:
                    a_ref[...] = jnp.zeros_like(a_ref)
            for a_ref, a in zip(a_out, accs, strict=True):
                a_ref[...] += a.reshape(tm // SUBLANES, SUBLANES, a.shape[-1]).sum(axis=0)

    in_specs = [pl.BlockSpec((tm, r.shape[1]), lambda i: (i, 0)) for r in rows]
    in_specs += [pl.BlockSpec(v.shape, lambda i: (0, 0)) for v in vecs]
    out_specs = [pl.BlockSpec((tm, w), lambda i: (i, 0)) for w, _ in row_outs]
    out_specs += [pl.BlockSpec((SUBLANES, w), lambda i: (0, 0)) for w in acc_outs]
    out_shape = [SDS((t, w), dt) for w, dt in row_outs] + [SDS((SUBLANES, w), F32) for w in acc_outs]
    return _pcall(name, body, (t // tm,), in_specs, [*rows, *vecs], out_specs, out_shape, semantics=("arbitrary",),
                  plan=plan)


def _tile(n, want):
    if n <= want:
        return n
    for t in range(want // LANES * LANES, 0, -LANES):
        if n % t == 0:
            return t
    raise ValueError(f"no tile for {n}")


_DOT_DIMS = {"nn": (((1,), (0,)), ((), ())), "nt": (((1,), (1,)), ((), ())), "tn": (((0,), (0,)), ((), ()))}


def _matmul(name, a, b, mode, out_dtypes, epilogue=None, extras=(), vecs=(), a_pro=None,
            tm=1024, tn=512, tk=4096, exact=False, b_blocked=False, out_blocked=0, b_rows=None, plan=None):
    cs = b.shape[-1] if b_blocked else None
    b2 = (b.shape[1], b.shape[0] * b.shape[2]) if b_blocked else b.shape
    if mode == "tn":
        (k, m), (k2, n) = a.shape, b2
    elif mode == "nt":
        (m, k), (n, k2) = a.shape, b2
    else:
        (m, k), (k2, n) = a.shape, b2
    assert k == k2 and not (b_blocked and mode == "tn")
    row0 = 0
    if b_rows is not None:
        assert mode == "nt" and not b_blocked
        row0, n = b_rows
        tn = _tile(math.gcd(n, row0) if row0 else n, tn)
    tm, tn, tk = _tile(m, tm), _tile(n, tn), _tile(k, tk)
    if b_blocked and mode == "nn":
        tn = _tile(cs, tn)
    if b_blocked and mode == "nt":
        tk = _tile(cs, tk)
    if out_blocked:
        tn = _tile(n // out_blocked, tn)
    nk = k // tk
    n_e, n_v, n_o = len(extras), len(vecs), len(out_dtypes)
    precision = HIGHEST if exact else None

    def body(*refs):
        a_ref, b_ref = refs[:2]
        e_refs, v_refs = refs[2:2 + n_e], refs[2 + n_e:2 + n_e + n_v]
        o_refs = refs[2 + n_e + n_v:2 + n_e + n_v + n_o]

        def product():
            av = a_ref[...]
            if a_pro is not None:
                av = a_pro(av)
            return lax.dot_general(av, b_ref[...], _DOT_DIMS[mode], precision=precision, preferred_element_type=F32)

        def finish(acc):
            res = (acc,) if epilogue is None else epilogue(acc, [e[...] for e in e_refs], [v[...] for v in v_refs])
            for o_ref, r in zip(o_refs, res, strict=True):
                o_ref[...] = r.astype(o_ref.dtype)

        if nk == 1:
            finish(product())
            return
        acc_ref = refs[-1]
        kk = pl.program_id(2)

        @pl.when(kk == 0)
        def _():
            acc_ref[...] = product()

        @pl.when(kk > 0)
        def _():
            acc_ref[...] += product()

        @pl.when(kk == nk - 1)
        def _():
            finish(acc_ref[...])

    if mode == "tn":
        a_spec = pl.BlockSpec((tk, tm), lambda i, j, kk: (kk, i))
    else:
        a_spec = pl.BlockSpec((tm, tk), lambda i, j, kk: (i, kk))
    if b_blocked and mode == "nn":
        per = cs // tn
        b_spec = pl.BlockSpec((None, tk, tn), lambda i, j, kk: (j // per, kk, j % per))
    elif b_blocked:
        per = cs // tk
        b_spec = pl.BlockSpec((None, tn, tk), lambda i, j, kk: (kk // per, j, kk % per))
    elif mode == "nt":
        assert row0 % tn == 0
        b_spec = pl.BlockSpec((tn, tk), lambda i, j, kk: (j + row0 // tn, kk))
    else:
        b_spec = pl.BlockSpec((tk, tn), lambda i, j, kk: (kk, j))
    tile = pl.BlockSpec((tm, tn), lambda i, j, kk: (i, j))
    if out_blocked:
        per_o = n // out_blocked // tn
        out_spec = pl.BlockSpec((None, tm, tn), lambda i, j, kk: (j // per_o, i, j % per_o))
        out_shape = [SDS((out_blocked, m, n // out_blocked), dt) for dt in out_dtypes]
    else:
        out_spec, out_shape = tile, [SDS((m, n), dt) for dt in out_dtypes]
    in_specs = [a_spec, b_spec] + [tile] * n_e + [pl.BlockSpec((1, tn), lambda i, j, kk: (0, j))] * n_v
    res = _pcall(name, body, (m // tm, n // tn, nk), in_specs, [a, b, *extras, *vecs], [out_spec] * n_o, out_shape,
                 scratch=[pltpu.VMEM((tm, tn), F32)] if nk > 1 else [],
                 semantics=("parallel", "parallel", "arbitrary"), plan=plan)
    if plan is None:
        return res[0] if n_o == 1 else res
    return (res[0][0] if n_o == 1 else res[0]), res[1]


def _rms_fwd(x):
    r = lax.rsqrt(jnp.mean(x * x, axis=-1, keepdims=True) + EPS)
    return x * r, r


def _rms_bwd(dxn, xn, r):
    return r * (dxn - xn * jnp.mean(dxn * xn, axis=-1, keepdims=True))


_INV_SQRT2 = 1.0 / math.sqrt(2.0)
_INV_SQRT2PI = 1.0 / math.sqrt(2.0 * math.pi)


def _gelu(y):
    return 0.5 * y * (1.0 + lax.erf(y * _INV_SQRT2))


def _gelu_grad(y):
    return 0.5 * (1.0 + lax.erf(y * _INV_SQRT2)) + y * (_INV_SQRT2PI * jnp.exp(-0.5 * y * y))


def _sigmoid(z):
    return 1.0 / (1.0 + jnp.exp(-z))


def _adam_math(w, g, m, v):
    m = ADAM_B1 * m + (1.0 - ADAM_B1) * g
    v = ADAM_B2 * v + (1.0 - ADAM_B2) * (g * g)
    m_hat = m / (1.0 - ADAM_B1 ** ADAM_STEP)
    v_hat = v / (1.0 - ADAM_B2 ** ADAM_STEP)
    delta = -ADAM_LR * (m_hat / (jnp.sqrt(v_hat) + ADAM_EPS) + ADAM_WD * w)
    return delta, m, v


def _norm_mod_fwd(name, x, g, scale, shift):
    def fn(rows, vecs):
        (xv,), (gv, sc, sh) = rows, vecs
        xn, _ = _rms_fwd(xv)
        return [(xn * gv) * (1.0 + sc) + sh], []
    return _rowwise(name, fn, [x], [g, scale, shift], [(x.shape[1], BF16)])[0]


def _norm_mod_bwd(name, x, dh, dres, g, scale, plan=None):
    d = x.shape[1]

    def fn(rows, vecs):
        (xv, dhv, drv), (gv, sc) = rows, vecs
        xn, r = _rms_fwd(xv)
        t = xn * gv
        dt = dhv * (1.0 + sc)
        dx = drv + _rms_bwd(dt * gv, xn, r)
        return [dx], [dhv * t, dhv, dt * xn]
    res = _rowwise(name, fn, [x, dh, dres], [g, scale], [(d, F32)], [d, d, d], plan=plan)
    (dx, dscale, dshift, dg), rest = res if plan is not None else (res, None)
    return (dx, dscale.sum(0), dshift.sum(0), dg.sum(0)), rest


def _gate_bwd(name, dx, val, gate):
    d = dx.shape[1]

    def fn(rows, vecs):
        (dxv, vv), (gv,) = rows, vecs
        return [dxv * gv], [dxv * vv.astype(F32)]
    dval, dgate = _rowwise(name, fn, [dx, val], [gate], [(d, BF16)], [d])
    return dval, dgate.sum(0)


def _final_loss(name, x, tgt, g):
    d = x.shape[1]

    def fn(rows, vecs):
        (xv, tv), (gv,) = rows, vecs
        xn, r = _rms_fwd(xv)
        e = xn * gv - tv
        dy = e * (1.0 / d)
        dx = _rms_bwd(dy * gv, xn, r)
        return [dx], [e * e, dy * xn]
    dx, sq, dg = _rowwise(name, fn, [x, tgt], [g], [(d, F32)], [d, d])
    return dx, 0.5 * jnp.sum(sq) / d, dg.sum(0)


def _group_norm_fwd(name, attn, ssm, g_a, g_s):
    def fn(rows, vecs):
        (av, sv), (ga, gs) = rows, vecs
        return [jnp.concatenate([_rms_fwd(av)[0] * ga, _rms_fwd(sv)[0] * gs], axis=1)], []
    return _rowwise(name, fn, [attn, ssm], [g_a, g_s], [(attn.shape[1] + ssm.shape[1], BF16)])[0]


def _group_norm_bwd(name, attn, ssm, dmixed, g_a, g_s):
    da_w, ds_w = attn.shape[1], ssm.shape[1]

    def fn(rows, vecs):
        (av, sv, dm), (ga, gs) = rows, vecs
        an, ra = _rms_fwd(av)
        sn, rs = _rms_fwd(sv)
        dma, dms = dm[:, :da_w], dm[:, da_w:]
        return [_rms_bwd(dma * ga, an, ra), _rms_bwd(dms * gs, sn, rs)], [dma * an, dms * sn]
    dattn, dssm, dga, dgs = _rowwise(name, fn, [attn, ssm, dmixed], [g_a, g_s],
                                     [(da_w, F32), (ds_w, F32)], [da_w, ds_w])
    return dattn, dssm, dga.sum(0), dgs.sum(0)


def _gelu_fwd(name, y):
    def fn(rows, vecs):
        return [_gelu(rows[0])], []
    return _rowwise(name, fn, [y], [], [(y.shape[1], BF16)])[0]


def _glu_bwd(name, dout, y, z):
    d = y.shape[1]

    def fn(rows, vecs):
        dov, yv, zv = rows
        sg = _sigmoid(zv)
        dz = dov * _gelu(yv) * sg * (1.0 - sg)
        return [dz, dov * sg], [dz]
    dz, dyg, db = _rowwise(name, fn, [dout, y, z], [], [(d, BF16), (d, F32)], [d])
    return dz, dyg, db.sum(0)


def _adam_shard(name, parts, w, m, v):
    r, c = w.shape
    n_parts = sum(1 if p.ndim == 2 else p.shape[0] for p in parts)
    row_bytes = 2 * c * (n_parts * parts[0].dtype.itemsize + 7 * 4)
    tr = min(128, r)
    while tr > SUBLANES and tr * row_bytes > VMEM_LIMIT_BYTES // 2:
        tr //= 2
    assert r % tr == 0
    n_p = len(parts)

    def body(*refs):
        p_refs, (w_ref, m_ref, v_ref, g_out, d_out, m_out, v_out) = refs[:n_p], refs[n_p:]
        g = None
        for p_ref in p_refs:
            terms = [p_ref[...]] if len(p_ref.shape) == 2 else [p_ref[j] for j in range(p_ref.shape[0])]
            for t in terms:
                g = t.astype(F32) if g is None else g + t.astype(F32)
        delta, m_new, v_new = _adam_math(w_ref[...], g, m_ref[...], v_ref[...])
        g_out[...], d_out[...], m_out[...], v_out[...] = g, delta, m_new, v_new

    tile = pl.BlockSpec((tr, c), lambda i: (i, 0))
    p_specs = [tile if p.ndim == 2 else pl.BlockSpec((p.shape[0], tr, c), lambda i: (0, i, 0)) for p in parts]
    return _pcall(name, body, (r // tr,), p_specs + [tile] * 3, [*parts, w, m, v], [tile] * 4, [SDS((r, c), F32)] * 4,
                  semantics=("parallel",))


def _pair_add(name, blocks, from_sibling):
    _, r, c = blocks.shape
    tr = min(256, r)
    assert r % tr == 0

    def body(b0_ref, b1_ref, s_ref, o_ref):
        mine = jnp.where(lax.axis_index("c") == 0, b0_ref[...].astype(F32), b1_ref[...].astype(F32))
        o_ref[...] = (mine + s_ref[...].astype(F32)).astype(o_ref.dtype)

    core_block = lambda k: pl.BlockSpec((None, tr, c), lambda s, i: (2 * s + k, i, 0))
    slot = pl.BlockSpec((None, tr, c), lambda s, i: (s, i, 0))
    return _pcall(name, body, (4, r // tr), [core_block(0), core_block(1), slot], [blocks, blocks, from_sibling],
                  [slot], [SDS((4, r, c), blocks.dtype)], semantics=("parallel", "parallel"))[0]


def _ada_update(name, c_act_t, dmod, w, m, v, tr=128, plan=None):
    r, c = w.shape
    tr = min(tr, r)
    assert r % tr == 0

    def body(c_ref, d_ref, w_ref, m_ref, v_ref, g_out, d_out, m_out, v_out):
        g = jnp.dot(c_ref[...], d_ref[...], precision=HIGHEST, preferred_element_type=F32)
        delta, m_new, v_new = _adam_math(w_ref[...], g, m_ref[...], v_ref[...])
        g_out[...], d_out[...], m_out[...], v_out[...] = g, delta, m_new, v_new

    tile = pl.BlockSpec((tr, c), lambda i: (i, 0))
    in_specs = [pl.BlockSpec((tr, N_DEV), lambda i: (i, 0)), pl.BlockSpec((N_DEV, c), lambda i: (0, 0)), tile, tile, tile]
    return _pcall(name, body, (r // tr,), in_specs, [c_act_t, dmod, w, m, v], [tile] * 4, [SDS((r, c), F32)] * 4,
                  semantics=("parallel",), plan=plan)


def _rotate_half(x):
    w = x.shape[1]
    half = HEAD_DIM // 2
    lane = lax.broadcasted_iota(jnp.int32, x.shape, 1)
    return jnp.where((lane % HEAD_DIM) < half, -pltpu.roll(x, w - half, 1), pltpu.roll(x, half, 1))


def _lane_tile(tab, w):
    return tab[:, :w] if w <= LANES else jnp.tile(tab, (1, w // LANES))


def _rope(x, cos, sin):
    return x * cos + _rotate_half(x) * sin


def _rope_t(dy, cos, sin):
    return dy * cos - _rotate_half(dy) * sin


def _band_mask(n):
    shape = (Q_PER_KV * WINDOW, 2 * WINDOW)
    i = lax.broadcasted_iota(jnp.int32, shape, 0) & (WINDOW - 1)
    j = lax.broadcasted_iota(jnp.int32, shape, 1)
    return (j > i) & (j <= i + WINDOW) & ((n > 0) | (j >= WINDOW))


def _stack_heads(x, hk):
    first = hk * Q_PER_KV
    return jnp.concatenate([x[:, (first + g) * HEAD_DIM:(first + g + 1) * HEAD_DIM] for g in range(Q_PER_KV)], axis=0)


def _stack_cols(ref, hk):
    first = hk * Q_PER_KV
    return jnp.concatenate([ref[:, first + g:first + g + 1] for g in range(Q_PER_KV)], axis=0)


def _stack_sinks(sink_ref, hk):
    first = hk * Q_PER_KV
    return jnp.concatenate([jnp.broadcast_to(sink_ref[0:1, first + g:first + g + 1], (WINDOW, 1))
                            for g in range(Q_PER_KV)], axis=0)


def _attn_specs(da, dkv, nb):
    cur = lambda n: (jnp.minimum(n, nb - 1), 0)
    prev = lambda n: (jnp.maximum(jnp.minimum(n, nb - 1) - 1, 0), 0)
    return dict(
        q=pl.BlockSpec((WINDOW, da), cur), kv_cur=pl.BlockSpec((WINDOW, dkv), cur),
        kv_prev=pl.BlockSpec((WINDOW, dkv), prev), tab_cur=pl.BlockSpec((WINDOW, LANES), cur),
        tab_prev=pl.BlockSpec((WINDOW, LANES), prev))


def _attn_fwd(name, q, k, v, cos, sin, sinks, plan=None):
    s, da = q.shape
    dkv = k.shape[1]
    nq, nb = da // HEAD_DIM, s // WINDOW
    scale = HEAD_DIM ** -0.5

    def body(q_ref, kp_ref, kc_ref, vp_ref, vc_ref, cc_ref, sc_ref, cp_ref, sp_ref, sink_ref, o_ref, lse_ref):
        n = pl.program_id(0)
        cc, sc, cp, sp = cc_ref[...], sc_ref[...], cp_ref[...], sp_ref[...]
        qr = _rope(q_ref[...], _lane_tile(cc, da), _lane_tile(sc, da)).astype(BF16)
        kk = jnp.concatenate([_rope(kp_ref[...], _lane_tile(cp, dkv), _lane_tile(sp, dkv)),
                              _rope(kc_ref[...], _lane_tile(cc, dkv), _lane_tile(sc, dkv))], axis=0).astype(BF16)
        vv = jnp.concatenate([vp_ref[...], vc_ref[...]], axis=0).astype(BF16)
        valid = _band_mask(n)
        for hk in range(nq // Q_PER_KV):
            ks = slice(hk * HEAD_DIM, (hk + 1) * HEAD_DIM)
            sco = lax.dot_general(_stack_heads(qr, hk), kk[:, ks], _DOT_DIMS["nt"], preferred_element_type=F32) * scale
            sco = jnp.where(valid, sco, -1e30)
            sink = _stack_sinks(sink_ref, hk)
            mx = jnp.maximum(jnp.max(sco, axis=1, keepdims=True), sink)
            p = jnp.exp(sco - mx)
            den = jnp.sum(p, axis=1, keepdims=True) + jnp.exp(sink - mx)
            o8 = jnp.dot((p / den).astype(BF16), vv[:, ks], preferred_element_type=F32)
            lse8 = mx + jnp.log(den)
            for g in range(Q_PER_KV):
                hq, rows = hk * Q_PER_KV + g, slice(g * WINDOW, (g + 1) * WINDOW)
                o_ref[:, hq * HEAD_DIM:(hq + 1) * HEAD_DIM] = o8[rows]
                lse_ref[:, hq:hq + 1] = lse8[rows]

    sp_ = _attn_specs(da, dkv, nb)
    in_specs = [sp_["q"], sp_["kv_prev"], sp_["kv_cur"], sp_["kv_prev"], sp_["kv_cur"],
                sp_["tab_cur"], sp_["tab_cur"], sp_["tab_prev"], sp_["tab_prev"], pl.BlockSpec((1, nq), lambda n: (0, 0))]
    return _pcall(name, body, (nb,), in_specs, [q, k, k, v, v, cos, sin, cos, sin, sinks],
                  [sp_["q"], pl.BlockSpec((WINDOW, nq), lambda n: (n, 0))], [SDS((s, da), F32), SDS((s, nq), F32)],
                  semantics=("arbitrary",), plan=plan)


def _attn_bwd(name, q, k, v, cos, sin, sinks, out, lse, dout, plan=None):
    s, da = q.shape
    dkv = k.shape[1]
    nq, nb = da // HEAD_DIM, s // WINDOW
    scale = HEAD_DIM ** -0.5

    def body(q_ref, kp_ref, kc_ref, vp_ref, vc_ref, cc_ref, sc_ref, cp_ref, sp_ref, sink_ref, o_ref, lse_ref,
             do_ref, dq_ref, dk_ref, dv_ref, dsink_ref, dk_carry, dv_carry):
        n = pl.program_id(0)
        cp, sp = _lane_tile(cp_ref[...], dkv), _lane_tile(sp_ref[...], dkv)

        @pl.when(n == 0)
        def _():
            dk_carry[...] = jnp.zeros_like(dk_carry)
            dv_carry[...] = jnp.zeros_like(dv_carry)
            dsink_ref[...] = jnp.zeros_like(dsink_ref)

        @pl.when(n < nb)
        def _():
            cc, sc = cc_ref[...], sc_ref[...]
            qr = _rope(q_ref[...], _lane_tile(cc, da), _lane_tile(sc, da)).astype(BF16)
            kk = jnp.concatenate([_rope(kp_ref[...], cp, sp),
                                  _rope(kc_ref[...], _lane_tile(cc, dkv), _lane_tile(sc, dkv))], axis=0).astype(BF16)
            vv = jnp.concatenate([vp_ref[...], vc_ref[...]], axis=0).astype(BF16)
            valid = _band_mask(n)
            do_all, o_all = do_ref[...], o_ref[...]
            for hk in range(nq // Q_PER_KV):
                ks = slice(hk * HEAD_DIM, (hk + 1) * HEAD_DIM)
                q8, lse8 = _stack_heads(qr, hk), _stack_cols(lse_ref, hk)
                sco = lax.dot_general(q8, kk[:, ks], _DOT_DIMS["nt"], preferred_element_type=F32) * scale
                probs = jnp.where(valid, jnp.exp(sco - lse8), 0.0)
                do8 = _stack_heads(do_all, hk)
                delta = jnp.sum(do8 * _stack_heads(o_all, hk), axis=1, keepdims=True)
                do8 = do8.astype(BF16)
                dp = lax.dot_general(do8, vv[:, ks], _DOT_DIMS["nt"], preferred_element_type=F32)
                ds = (probs * (dp - delta) * scale).astype(BF16)
                dq8 = jnp.dot(ds, kk[:, ks], preferred_element_type=F32)
                dk_h = lax.dot_general(ds, q8, _DOT_DIMS["tn"], preferred_element_type=F32)
                dv_h = lax.dot_general(probs.astype(BF16), do8, _DOT_DIMS["tn"], preferred_element_type=F32)
                dsink8 = -jnp.exp(_stack_sinks(sink_ref, hk) - lse8) * delta
                for g in range(Q_PER_KV):
                    hq, rows = hk * Q_PER_KV + g, slice(g * WINDOW, (g + 1) * WINDOW)
                    dq_ref[:, hq * HEAD_DIM:(hq + 1) * HEAD_DIM] = dq8[rows]
                    dsink_ref[:, hq:hq + 1] += dsink8[rows].reshape(WINDOW // SUBLANES, SUBLANES, 1).sum(axis=0)
                dk_ref[:, ks] = dk_carry[:, ks] + dk_h[:WINDOW]
                dv_ref[:, ks] = dv_carry[:, ks] + dv_h[:WINDOW]
                dk_carry[:, ks] = dk_h[WINDOW:]
                dv_carry[:, ks] = dv_h[WINDOW:]
            dq_ref[...] = _rope_t(dq_ref[...], _lane_tile(cc, da), _lane_tile(sc, da))
            dk_ref[...] = _rope_t(dk_ref[...], cp, sp)

        @pl.when(n == nb)
        def _():
            dk_ref[...] = _rope_t(dk_carry[...], cp, sp)
            dv_ref[...] = dv_carry[...]

    sp_ = _attn_specs(da, dkv, nb)
    last_prev = lambda n: (jnp.maximum(n - 1, 0), 0)
    tab_prev = pl.BlockSpec((WINDOW, LANES), last_prev)
    kv_out = pl.BlockSpec((WINDOW, dkv), last_prev)
    lse_spec = pl.BlockSpec((WINDOW, nq), lambda n: (jnp.minimum(n, nb - 1), 0))
    in_specs = [sp_["q"], sp_["kv_prev"], sp_["kv_cur"], sp_["kv_prev"], sp_["kv_cur"],
                sp_["tab_cur"], sp_["tab_cur"], tab_prev, tab_prev,
                pl.BlockSpec((1, nq), lambda n: (0, 0)), sp_["q"], lse_spec, sp_["q"]]
    res = _pcall(name, body, (nb + 1,), in_specs, [q, k, k, v, v, cos, sin, cos, sin, sinks, out, lse, dout],
                 [sp_["q"], kv_out, kv_out, pl.BlockSpec((SUBLANES, nq), lambda n: (0, 0))],
                 [SDS((s, da), F32), SDS((s, dkv), F32), SDS((s, dkv), F32), SDS((SUBLANES, nq), F32)],
                 scratch=[pltpu.VMEM((WINDOW, dkv), F32), pltpu.VMEM((WINDOW, dkv), F32)],
                 semantics=("arbitrary",), plan=plan)
    (dq, dk, dv, dsink), rest = res if plan is not None else (res, None)
    return (dq, dk, dv, dsink.sum(0)), rest


def _ssm_operators(lam_re, lam_im, log_step, b_re, b_im, c_re, c_im, d_skip):
    g, p = lam_re.shape
    h = b_re.shape[-1]
    l = SSM_CHUNK
    step = jnp.exp(log_step)[:, None]
    mag = jnp.exp(lam_re * step)
    ar, ai = mag * jnp.cos(lam_im * step), mag * jnp.sin(lam_im * step)
    den = lam_re * lam_re + lam_im * lam_im
    cr = ((ar - 1.0) * lam_re + ai * lam_im) / den
    ci = (ai * lam_re - (ar - 1.0) * lam_im) / den
    bbr = cr[..., None] * b_re - ci[..., None] * b_im
    bbi = cr[..., None] * b_im + ci[..., None] * b_re
    pr, pi = [jnp.ones_like(ar)], [jnp.zeros_like(ar)]
    for _ in range(l):
        pr, pi = pr + [pr[-1] * ar - pi[-1] * ai], pi + [pr[-1] * ai + pi[-1] * ar]
    pwr, pwi = jnp.stack(pr, axis=1), jnp.stack(pi, axis=1)
    cpr = c_re[:, None] * pwr[:, :, None, :] - c_im[:, None] * pwi[:, :, None, :]
    cpi = c_re[:, None] * pwi[:, :, None, :] + c_im[:, None] * pwr[:, :, None, :]
    kern = (jnp.einsum("gtop,gpi->gtoi", cpr[:, :l], bbr, precision=HIGHEST)
            - jnp.einsum("gtop,gpi->gtoi", cpi[:, :l], bbi, precision=HIGHEST))
    kern = kern.at[:, 0].add(d_skip.reshape(g, h)[:, :, None] * jnp.eye(h, dtype=F32))
    tm = jnp.stack([jnp.pad(kern[:, :l - j], ((0, 0), (j, 0), (0, 0), (0, 0))) for j in range(l)], axis=1)
    tm = tm.transpose(0, 1, 4, 2, 3).reshape(g, l * h, l * h)
    rev_r, rev_i = pwr[:, l - 1::-1][:, :l], pwi[:, l - 1::-1][:, :l]
    er = rev_r[:, :, None, :] * bbr.transpose(0, 2, 1)[:, None] - rev_i[:, :, None, :] * bbi.transpose(0, 2, 1)[:, None]
    ei = rev_r[:, :, None, :] * bbi.transpose(0, 2, 1)[:, None] + rev_i[:, :, None, :] * bbr.transpose(0, 2, 1)[:, None]
    em = jnp.concatenate([er, ei], axis=-1).reshape(g, l * h, 2 * p)
    fr = cpr[:, 1:].transpose(0, 3, 1, 2).reshape(g, p, l * h)
    fi = -cpi[:, 1:].transpose(0, 3, 1, 2).reshape(g, p, l * h)
    fm = jnp.concatenate([fr, fi], axis=1)
    return tm, em, fm, pwr[:, l], pwi[:, l]


def _decay_lanes(alr, ali):
    return jnp.concatenate([alr, alr], axis=1), jnp.concatenate([-ali, ali], axis=1)


def _ssm_fwd(name, u, tm, em, fm, acat, bcat, plan=None):
    s, ds = u.shape
    g, lh, p2 = em.shape
    gb, h = SSM_GROUPS_PER_STEP, lh // SSM_CHUNK
    assert gb * h == LANES and g * h == ds and s % SSM_CHUNK == 0
    nc, half = s // SSM_CHUNK, p2 // 2

    def body(u_ref, tm_ref, em_ref, fm_ref, a_ref, b_ref, y_ref, xp_ref, uc_ref, yc_ref, st_ref):
        _to_chunks(u_ref, uc_ref, nc, h)
        for i in range(gb):
            st_ref[pl.ds(i, nc, stride=gb), :] = jnp.dot(uc_ref[i], em_ref[i], precision=HIGHEST,
                                                         preferred_element_type=F32)
        av, bv = a_ref[...], b_ref[...]

        def step(c, carry):
            x, xs = carry
            rows = pl.ds(pl.multiple_of(c * gb, gb), gb)
            loc = st_ref[rows, :]
            st_ref[rows, :] = x
            return av * x + bv * xs + loc, av * xs - bv * x + pltpu.roll(loc, half, 1)
        zero = jnp.zeros((gb, p2), F32)
        lax.fori_loop(0, nc, step, (zero, zero), unroll=4)
        for i in range(gb):
            xp = st_ref[pl.ds(i, nc, stride=gb), :]
            xp_ref[i] = xp
            yc_ref[i] = (jnp.dot(uc_ref[i], tm_ref[i], precision=HIGHEST, preferred_element_type=F32)
                         + jnp.dot(xp, fm_ref[i], precision=HIGHEST, preferred_element_type=F32))
        _from_chunks(yc_ref, y_ref, nc, h)

    blk = lambda r, c: pl.BlockSpec((gb, r, c), lambda i: (i, 0, 0))
    vec = pl.BlockSpec((gb, p2), lambda i: (i, 0))
    col = pl.BlockSpec((s, LANES), lambda i: (0, i))
    return _pcall(name, body, (g // gb,), [col, blk(lh, lh), blk(lh, p2), blk(p2, lh), vec, vec],
                  [u, tm, em, fm, acat, bcat], [col, blk(nc, p2), blk(nc, lh)],
                  [SDS((s, ds), F32), SDS((g, nc, p2), F32), SDS((g, nc, lh), F32)],
                  scratch=[pltpu.VMEM((gb, nc, lh), F32), pltpu.VMEM((nc * gb, p2), F32)],
                  semantics=("parallel",), plan=plan)


def _ssm_bwd(name, u_chunks, dy, xprev, tm, em, fm, acat, bcat, plan=None):
    s, ds = dy.shape
    g, lh, p2 = em.shape
    gb, h = SSM_GROUPS_PER_STEP, lh // SSM_CHUNK
    nc, half = s // SSM_CHUNK, p2 // 2

    def body(uc_ref, dy_ref, xp_ref, tm_ref, em_ref, fm_ref, a_ref, b_ref,
             du_ref, dtm_ref, dem_ref, dfm_ref, r1_ref, r2_ref, dyc_ref, duc_ref, gs_ref, xs_ref):
        _to_chunks(dy_ref, dyc_ref, nc, h)
        for i in range(gb):
            gs_ref[pl.ds(i, nc, stride=gb), :] = lax.dot_general(
                dyc_ref[i], fm_ref[i], _DOT_DIMS["nt"], precision=HIGHEST, preferred_element_type=F32)
            xs_ref[pl.ds(i, nc, stride=gb), :] = xp_ref[i]
        av, bv = a_ref[...], b_ref[...]

        def step(t, carry):
            grad, gsw, r1, r2 = carry
            c = nc - 1 - t
            rows = pl.ds(pl.multiple_of(c * gb, gb), gb)
            dxp, xp = gs_ref[rows, :], xs_ref[rows, :]
            gs_ref[rows, :] = grad
            r1 = r1 + grad * xp
            r2 = r2 + grad * pltpu.roll(xp, half, 1)
            return dxp + av * grad - bv * gsw, pltpu.roll(dxp, half, 1) + av * gsw + bv * grad, r1, r2
        zero = jnp.zeros((gb, p2), F32)
        _, _, r1, r2 = lax.fori_loop(0, nc, step, (zero, zero, zero, zero), unroll=4)
        r1_ref[...], r2_ref[...] = r1, r2
        for i in range(gb):
            dxl = gs_ref[pl.ds(i, nc, stride=gb), :]
            duc_ref[i] = (lax.dot_general(dyc_ref[i], tm_ref[i], _DOT_DIMS["nt"], precision=HIGHEST,
                                          preferred_element_type=F32)
                          + lax.dot_general(dxl, em_ref[i], _DOT_DIMS["nt"], precision=HIGHEST,
                                            preferred_element_type=F32))
            dtm_ref[i] = lax.dot_general(uc_ref[i], dyc_ref[i], _DOT_DIMS["tn"], precision=HIGHEST,
                                         preferred_element_type=F32)
            dfm_ref[i] = lax.dot_general(xp_ref[i], dyc_ref[i], _DOT_DIMS["tn"], precision=HIGHEST,
                                         preferred_element_type=F32)
            dem_ref[i] = lax.dot_general(uc_ref[i], dxl, _DOT_DIMS["tn"], precision=HIGHEST,
                                         preferred_element_type=F32)
        _from_chunks(duc_ref, du_ref, nc, h)

    blk = lambda r, c: pl.BlockSpec((gb, r, c), lambda i: (i, 0, 0))
    vec = pl.BlockSpec((gb, p2), lambda i: (i, 0))
    col = pl.BlockSpec((s, LANES), lambda i: (0, i))
    chunked = pltpu.VMEM((gb, nc, lh), F32)
    res = _pcall(name, body, (g // gb,),
                 [blk(nc, lh), col, blk(nc, p2), blk(lh, lh), blk(lh, p2), blk(p2, lh), vec, vec],
                 [u_chunks, dy, xprev, tm, em, fm, acat, bcat],
                 [col, blk(lh, lh), blk(lh, p2), blk(p2, lh), vec, vec],
                 [SDS((s, ds), F32), SDS((g, lh, lh), F32), SDS((g, lh, p2), F32), SDS((g, p2, lh), F32),
                  SDS((g, p2), F32), SDS((g, p2), F32)],
                 scratch=[chunked, chunked, pltpu.VMEM((nc * gb, p2), F32), pltpu.VMEM((nc * gb, p2), F32)],
                 semantics=("parallel",), plan=plan)
    return res if plan is not None else (res, None)


def _to_chunks(src_ref, dst_ref, nc, h):
    per = LANES // h
    grp = lax.broadcasted_iota(jnp.int32, (nc, LANES), 1) // h
    for g in range(per):
        for part in range(SSM_CHUNK * h // LANES):
            acc = None
            for i in range(part * per, (part + 1) * per):
                piece = src_ref[pl.ds(i, nc, stride=SSM_CHUNK), :]
                lo = (i * h) % LANES
                if (lo - g * h) % LANES:
                    piece = pltpu.roll(piece, (lo - g * h) % LANES, 1)
                acc = piece if acc is None else jnp.where(grp == lo // h, piece, acc)
            dst_ref[g, :, part * LANES:(part + 1) * LANES] = acc


def _from_chunks(src_ref, dst_ref, nc, h):
    per = LANES // h
    grp = lax.broadcasted_iota(jnp.int32, (nc, LANES), 1) // h
    for i in range(SSM_CHUNK):
        part, lo = divmod(i * h, LANES)
        row = None
        for g in range(per):
            piece = src_ref[g, :, part * LANES:(part + 1) * LANES]
            if (g * h - lo) % LANES:
                piece = pltpu.roll(piece, (g * h - lo) % LANES, 1)
            row = piece if row is None else jnp.where(grp == g, piece, row)
        dst_ref[pl.ds(i, nc, stride=SSM_CHUNK), :] = row


_SMALL = ("b_ada", "norm1_g", "sinks", "ssm_lam_re", "ssm_lam_im", "ssm_log_step", "ssm_b_re", "ssm_b_im",
          "ssm_c_re", "ssm_c_im", "ssm_d", "b_glu", "attn_out_g", "ssm_out_g", "norm2_g", "final_g")
_WEIGHTS = ("w_ada", "b_ada", "norm1_g", "w_in", "sinks", "ssm_lam_re", "ssm_lam_im", "ssm_log_step", "ssm_b_re",
            "ssm_b_im", "ssm_c_re", "ssm_c_im", "ssm_d", "w_glu", "b_glu", "attn_out_g", "ssm_out_g", "w_out",
            "norm2_g", "w_ff1", "w_ff2", "final_g")
_PACK_ALIGN = 128 * LANES


def _pack(parts):
    flat = jnp.concatenate([p.reshape(-1).astype(F32) for p in parts])
    pad = (-flat.shape[0]) % _PACK_ALIGN
    return jnp.pad(flat, (0, pad)).reshape(-1, LANES)


def kernel(x, c, w_ada, b_ada, norm1_g, w_in, sinks, ssm_lam_re, ssm_lam_im, ssm_log_step, ssm_b_re, ssm_b_im, ssm_c_re, ssm_c_im, ssm_d, w_glu, b_glu, attn_out_g, ssm_out_g, w_out, norm2_g, w_ff1, w_ff2, final_g, loss_target, m_w_ada, m_b_ada, m_norm1_g, m_w_in, m_sinks, m_ssm_lam_re, m_ssm_lam_im, m_ssm_log_step, m_ssm_b_re, m_ssm_b_im, m_ssm_c_re, m_ssm_c_im, m_ssm_d, m_w_glu, m_b_glu, m_attn_out_g, m_ssm_out_g, m_w_out, m_norm2_g, m_w_ff1, m_w_ff2, m_final_g, v_w_ada, v_b_ada, v_norm1_g, v_w_in, v_sinks, v_ssm_lam_re, v_ssm_lam_im, v_ssm_log_step, v_ssm_b_re, v_ssm_b_im, v_ssm_c_re, v_ssm_c_im, v_ssm_d, v_w_glu, v_b_glu, v_attn_out_g, v_ssm_out_g, v_w_out, v_norm2_g, v_w_ff1, v_w_ff2, v_final_g):
    args = dict(locals())
    weights = {n: args[n] for n in _WEIGHTS}
    mom = {n: args["m_" + n] for n in _WEIGHTS}
    var = {n: args["v_" + n] for n in _WEIGHTS}
    me = 4 * lax.axis_index("x") + 2 * lax.axis_index("y") + lax.axis_index("c")

    _, s, d = x.shape
    xs, tgt = x[0], loss_target[0]
    d_ssm = ssm_d.shape[-1]
    d_attn = d - d_ssm
    nq = d_attn // HEAD_DIM
    d_kv = (nq // Q_PER_KV) * HEAD_DIM
    p_state = ssm_b_re.shape[2]

    c_all, g_in = _run_plan("gather_c_w_in", _Gather([c, w_in[0].T.astype(BF16)]))
    c_all = c_all.reshape(N_DEV, d)
    w_in_t = g_in.reshape(-1, d)

    n_loc = w_ada.shape[-1]
    b_loc = lax.dynamic_slice_in_dim(b_ada, me * n_loc, n_loc, axis=1)
    silu = lambda t: t * _sigmoid(t)
    mod_part = _matmul("ada_mod", c_all, w_ada[0], "nn", [F32], a_pro=silu, vecs=[b_loc], exact=True,
                       epilogue=lambda acc, e, v: (acc + v[0],), tn=512, tk=d)
    mod_all = _run_plan("gather_mod", _Gather([mod_part]))[0]
    mod = lax.dynamic_index_in_dim(mod_all, me, axis=1, keepdims=False).reshape(N_MOD, 1, d)
    shift1, scale1, gate1, shift2, scale2, gate2 = [mod[i] for i in range(N_MOD)]

    h1 = _norm_mod_fwd("norm1", xs, norm1_g, scale1, shift1)
    q = _matmul("proj_q", h1, w_in_t, "nt", [F32], b_rows=(0, d_attn))
    kv = _matmul("proj_kv", h1, w_in_t, "nt", [F32], b_rows=(d_attn, 2 * d_kv))
    u = _matmul("proj_u", h1, w_in_t, "nt", [F32], b_rows=(d_attn + 2 * d_kv, d_ssm))
    k, v = kv[:, :d_kv], kv[:, d_kv:]

    half = HEAD_DIM // 2
    inv_freq = ROPE_THETA ** (-jnp.arange(half, dtype=F32) / half)
    ang = jnp.arange(s, dtype=F32)[:, None] * inv_freq[None, :]
    cos_t, sin_t = jnp.tile(jnp.cos(ang), (1, 4)), jnp.tile(jnp.sin(ang), (1, 4))
    (attn, lse), (g_glu, g_out) = _attn_fwd("attn_fwd", q, k, v, cos_t, sin_t, sinks,
                                            plan=_Gather([w_glu[0].astype(BF16), w_out[0].astype(BF16)]))
    w_glu_f = g_glu.reshape(d_ssm, d_ssm)
    w_out_f = g_out.reshape(d, d)

    ssm_params = (ssm_lam_re[0], ssm_lam_im[0], ssm_log_step[0], ssm_b_re[0], ssm_b_im[0], ssm_c_re[0],
                  ssm_c_im[0], ssm_d[0])
    (tm_op, em_op, fm_op, alr, ali), ssm_vjp = jax.vjp(_ssm_operators, *ssm_params)
    acat, bcat = _decay_lanes(alr, ali)
    (y_ssm, x_prev, u_chunks), (g_ff1,) = _ssm_fwd("ssm_fwd", u, tm_op, em_op, fm_op, acat, bcat,
                                                   plan=_Gather([w_ff1[0].astype(BF16)]))
    yg = _gelu_fwd("gelu", y_ssm)
    ssm_out, z_glu = _matmul(
        "glu", yg, w_glu_f, "nn", [F32, F32], extras=[y_ssm], vecs=[b_glu],
        epilogue=lambda acc, e, v: (_gelu(e[0]) * _sigmoid(acc + v[0]), acc + v[0]))
    mixed = _group_norm_fwd("group_norm", attn, ssm_out, attn_out_g, ssm_out_g)
    x2, mo = _matmul("out_proj", mixed, w_out_f, "nn", [F32, BF16], extras=[xs], vecs=[gate1],
                     epilogue=lambda acc, e, v: (e[0] + v[0] * acc, acc))

    h2 = _norm_mod_fwd("norm2", x2, norm2_g, scale2, shift2)
    (a_ff, f_ff), (g_ff2,) = _matmul("ff1", h2, g_ff1, "nn", [BF16, BF16], b_blocked=True,
                                     epilogue=lambda acc, e, v: (acc, jnp.square(jnp.maximum(acc, 0.0))),
                                     plan=_Gather([w_ff2[0].astype(BF16)]))
    w_ff2_f = g_ff2.reshape(-1, d)
    x3, ff = _matmul("ff2", f_ff, w_ff2_f, "nn", [F32, BF16], extras=[x2], vecs=[gate2],
                     epilogue=lambda acc, e, v: (e[0] + v[0] * acc, acc))

    dx3, loss_local, d_final_g = _final_loss("final_loss", x3, tgt, final_g.reshape(1, d))
    loss = lax.psum(loss_local, MESH_AXES)

    dff, d_gate2 = _gate_bwd("gate2_bwd", dx3, ff, gate2)
    dw_ff2 = _matmul("ff2_dw", f_ff, dff, "tn", [BF16]).reshape(N_DEV, -1, d)
    da_ff, (p_ff2,) = _matmul("ff2_dx", dff, w_ff2_f, "nt", [BF16], extras=[a_ff],
                              epilogue=lambda acc, e, v: (acc * (2.0 * jnp.maximum(e[0].astype(F32), 0.0)),),
                              plan=_PairSwap([dw_ff2]))
    s_ff2 = _pair_add("pair_add_ff2", dw_ff2, p_ff2)
    dw_ff1, (r_ff2_a,) = _matmul("ff1_dw", h2, da_ff, "tn", [BF16], out_blocked=N_DEV,
                                 plan=_ChipExchange([s_ff2], (CHIP_X, CHIP_Y)))
    dh2, (r_ff2_b, p_ff1) = _matmul("ff1_dx", da_ff, g_ff1, "nt", [F32], b_blocked=True,
                                    plan=_Plans([_ChipExchange([s_ff2], (CHIP_DIAGONAL,)), _PairSwap([dw_ff1])]))
    s_ff1 = _pair_add("pair_add_ff1", dw_ff1, p_ff1)
    (dx2, d_scale2, d_shift2, d_norm2_g), _ = _norm_mod_bwd("norm2_bwd", x2, dh2, dx3, norm2_g, scale2)

    dmo, d_gate1 = _gate_bwd("gate1_bwd", dx2, mo, gate1)
    dw_out = _matmul("out_dw", mixed, dmo, "tn", [BF16]).reshape(N_DEV, -1, d)
    dmixed, (r_out_a,) = _matmul("out_dx", dmo, w_out_f, "nt", [F32], plan=_Exchange([dw_out], (1, 4, 2)))
    dattn, dssm_out, d_attn_g, d_ssm_g = _group_norm_bwd("group_norm_bwd", attn, ssm_out, dmixed, attn_out_g, ssm_out_g)

    dz, dyg_direct, d_b_glu = _glu_bwd("glu_bwd", dssm_out, y_ssm, z_glu)
    dw_glu = _matmul("glu_dw", yg, dz, "tn", [BF16]).reshape(N_DEV, -1, d_ssm)
    dy_ssm = _matmul("glu_dx", dz, w_glu_f, "nt", [F32], extras=[dyg_direct, y_ssm],
                     epilogue=lambda acc, e, v: ((acc + e[0]) * _gelu_grad(e[1]),))
    (du, d_tm, d_em, d_fm, r1, r2), (r_ff1_a, r_out_b) = _ssm_bwd(
        "ssm_bwd", u_chunks, dy_ssm, x_prev, tm_op, em_op, fm_op, acat, bcat,
        plan=_Plans([_ChipExchange([s_ff1], (CHIP_X, CHIP_Y)), _Exchange([dw_out], (5, 3, 6))]))
    d_alr = r1[:, :p_state] + r1[:, p_state:]
    d_ali = r2[:, p_state:] - r2[:, :p_state]
    d_ssm_params = ssm_vjp((d_tm, d_em, d_fm, d_alr, d_ali))

    (dq, dk, dv, d_sinks), (r_ff1_b,) = _attn_bwd("attn_bwd", q, k, v, cos_t, sin_t, sinks, attn, lse, dattn,
                                                  plan=_ChipExchange([s_ff1], (CHIP_DIAGONAL,)))
    dproj = jnp.concatenate([dq, dk, dv, du], axis=1).astype(BF16)
    dw_in_t, (r_out_c, r_glu) = _matmul(
        "in_dw", dproj, h1, "tn", [BF16],
        plan=_Plans([_Exchange([dw_out], (7,)), _Exchange([dw_glu], RELATIONS_ALL)]))
    dw_in_t = dw_in_t.reshape(N_DEV, -1, d)
    dh1, (r_in_a,) = _matmul("in_dx", dproj, w_in_t, "nn", [F32], plan=_Exchange([dw_in_t], RELATIONS_SAME_CORE))
    (grad_x, d_scale1, d_shift1, d_norm1_g), _ = _norm_mod_bwd("norm1_bwd", xs, dh1, dx2, norm1_g, scale1)

    d_mod = jnp.concatenate([d_shift1, d_scale1, d_gate1, d_shift2, d_scale2, d_gate2])
    small_g = dict(zip(("ssm_lam_re", "ssm_lam_im", "ssm_log_step", "ssm_b_re", "ssm_b_im", "ssm_c_re", "ssm_c_im",
                        "ssm_d"), d_ssm_params, strict=True))
    small_g.update(b_ada=d_mod, norm1_g=d_norm1_g, sinks=d_sinks, b_glu=d_b_glu, attn_out_g=d_attn_g,
                   ssm_out_g=d_ssm_g, norm2_g=d_norm2_g, final_g=d_final_g)
    small_parts, r_in_b = _run_plan("gather_small_grads", _Plans([_Gather([_pack([small_g[n] for n in _SMALL])]),
                                                                  _Exchange([dw_in_t], RELATIONS_OTHER_CORE)]))
    small = _adam_shard("adam_small", [small_parts], _pack([weights[n] for n in _SMALL]),
                        _pack([mom[n] for n in _SMALL]), _pack([var[n] for n in _SMALL]))
    out = {}
    off = 0
    for n in _SMALL:
        size = weights[n].size
        out[n] = [t.reshape(-1)[off:off + size].reshape(weights[n].shape) for t in small]
        off += size

    dmod_all = small_parts.reshape(N_DEV, -1)[:, :N_MOD * d]
    dmod_loc = lax.dynamic_slice_in_dim(dmod_all, me * n_loc, n_loc, axis=1)
    c_act_t = silu(c_all).T
    out["w_ada"] = [t[None] for t in _ada_update("adam_w_ada", c_act_t, dmod_loc, w_ada[0], m_w_ada[0], v_w_ada[0])]

    mine = lambda blocks: lax.dynamic_index_in_dim(blocks, me, axis=0, keepdims=False)
    in_parts = [mine(dw_in_t).T] + [r.transpose(0, 2, 1) for r in (r_in_a, r_in_b)]
    my_chip = 2 * lax.axis_index("x") + lax.axis_index("y")
    chip_sum = lambda sums: lax.dynamic_index_in_dim(sums, my_chip, axis=0, keepdims=False)
    received = dict(w_in=in_parts, w_glu=[mine(dw_glu), r_glu], w_out=[mine(dw_out), r_out_a, r_out_b, r_out_c],
                    w_ff1=[chip_sum(s_ff1), r_ff1_a, r_ff1_b], w_ff2=[chip_sum(s_ff2), r_ff2_a, r_ff2_b])
    for n, parts in received.items():
        out[n] = [t[None] for t in _adam_shard("adam_" + n, parts, weights[n][0], mom[n][0], var[n][0])]

    return (loss, grad_x[None], *[out[n][0] for n in _WEIGHTS], *[out[n][1] for n in _WEIGHTS],
            *[out[n][2] for n in _WEIGHTS], *[out[n][3] for n in _WEIGHTS])
```

```python
import math

import jax
import jax.numpy as jnp
from jax import lax
from jax.experimental import pallas as pl
from jax.experimental.pallas import tpu as pltpu

F32, BF16 = jnp.float32, jnp.bfloat16
SDS = jax.ShapeDtypeStruct
MESH_AXES = ("x", "y", "c")
N_DEV = 8
VMEM_LIMIT_BYTES = 56 * 1024 * 1024
SUBLANES, LANES = 8, 128

HEAD_DIM = 64
Q_PER_KV = 8
WINDOW = 128
ROPE_THETA = 10000.0
EPS = 1e-6
N_MOD = 6
SSM_CHUNK = 16
SSM_GROUPS_PER_STEP = 8

ADAM_LR, ADAM_B1, ADAM_B2, ADAM_EPS, ADAM_WD, ADAM_STEP = 0.001, 0.9, 0.999, 1e-08, 0.01, 10
HIGHEST = lax.Precision.HIGHEST

RELATIONS_ALL = (1, 4, 2, 6, 5, 3, 7)
RELATIONS_SAME_CORE = (1, 4, 2, 6)
RELATIONS_OTHER_CORE = (5, 3, 7)
PLAN_MIDDLE = 0.65


def _cparams(sem):
    return pltpu.CompilerParams(dimension_semantics=sem, vmem_limit_bytes=VMEM_LIMIT_BYTES)


def _block_index(p):
    return 4 * p[0] + 2 * p[1] + p[2]


def _me():
    return lax.axis_index("x"), lax.axis_index("y"), lax.axis_index("c")


class _Plan:
    def middle(self, ins, outs, send, recv, local):
        pass


class _Gather(_Plan):
    TO_SIBLING, TO_X, TO_Y, RELAY, PASS_X, PASS_Y, PASS_DIAGONAL = range(7)

    def __init__(self, arrs):
        self.ins = list(arrs)
        self.out_shapes = [SDS((N_DEV,) + a.shape, a.dtype) for a in arrs]
        self.n_rdma, self.n_local = 7 * len(arrs), len(arrs)
        self.rdma_base = self.local_base = 0

    def _copy(self, ins, outs, send, recv, a, k, block, to, from_input=False):
        dst = outs[a].at[_block_index(block)]
        sem = self.rdma_base + a * 7 + k
        return pltpu.make_async_remote_copy(
            src_ref=ins[a] if from_input else dst, dst_ref=dst, send_sem=send.at[sem], recv_sem=recv.at[sem],
            device_id=to, device_id_type=pl.DeviceIdType.MESH)

    @staticmethod
    def _places():
        x, y, c = _me()
        return (x, y, c), (x, y, 1 - c), (1 - x, y, c), (x, 1 - y, c), (1 - x, 1 - y, c)

    def _first(self, ins, outs, send, recv, a):
        me, sibling, x_nbr, y_nbr, _ = self._places()
        return [self._copy(ins, outs, send, recv, a, k, me, to, True)
                for k, to in ((self.TO_SIBLING, sibling), (self.TO_X, x_nbr), (self.TO_Y, y_nbr))]

    def _mine(self, ins, outs, local, a):
        return pltpu.make_async_copy(ins[a], outs[a].at[_block_index(_me())], local.at[self.local_base + a])

    def start(self, ins, outs, send, recv, local):
        for a in range(len(ins)):
            self._mine(ins, outs, local, a).start()
            for cp in self._first(ins, outs, send, recv, a):
                cp.start()

    def middle(self, ins, outs, send, recv, local):
        me, sibling, x_nbr, y_nbr, _ = self._places()
        core = me[2]
        for a in range(len(ins)):
            self._copy(ins, outs, send, recv, a, self.TO_X, x_nbr, me).wait_recv()
            self._copy(ins, outs, send, recv, a, self.TO_Y, y_nbr, me).wait_recv()

            @pl.when(core == 0)
            def _():
                self._copy(ins, outs, send, recv, a, self.RELAY, x_nbr, y_nbr).start()

            @pl.when(core == 1)
            def _():
                self._copy(ins, outs, send, recv, a, self.RELAY, y_nbr, x_nbr).start()

            self._copy(ins, outs, send, recv, a, self.PASS_X, x_nbr, sibling).start()
            self._copy(ins, outs, send, recv, a, self.PASS_Y, y_nbr, sibling).start()

    def finish(self, ins, outs, send, recv, local):
        me, sibling, x_nbr, y_nbr, diagonal = self._places()
        other = lambda p: (p[0], p[1], 1 - p[2])
        for a in range(len(ins)):
            self._copy(ins, outs, send, recv, a, self.RELAY, diagonal, me).wait_recv()
            self._copy(ins, outs, send, recv, a, self.PASS_DIAGONAL, diagonal, sibling).start()
        for a in range(len(ins)):
            self._copy(ins, outs, send, recv, a, self.TO_SIBLING, sibling, me).wait_recv()
            for k, src in ((self.PASS_X, x_nbr), (self.PASS_Y, y_nbr), (self.PASS_DIAGONAL, diagonal)):
                self._copy(ins, outs, send, recv, a, k, other(src), me).wait_recv()
                self._copy(ins, outs, send, recv, a, k, src, sibling).wait_send()
            for cp in self._first(ins, outs, send, recv, a):
                cp.wait_send()
            self._copy(ins, outs, send, recv, a, self.RELAY, me, me).wait_send()
            self._mine(ins, outs, local, a).wait()


class _Exchange(_Plan):
    def __init__(self, arrs, relations):
        self.ins, self.relations = list(arrs), tuple(relations)
        self.out_shapes = [SDS((len(relations),) + a.shape[1:], a.dtype) for a in arrs]
        self.n_rdma, self.n_local = len(relations) * len(arrs), 0
        self.rdma_base = self.local_base = 0

    def _copies(self, ins, outs, send, recv):
        x, y, c = _me()
        cps = []
        for a in range(len(ins)):
            for s, k in enumerate(self.relations):
                peer = ((1 - x) if (k & 4) else x, (1 - y) if (k & 2) else y, (1 - c) if (k & 1) else c)
                sem = self.rdma_base + a * len(self.relations) + s
                cps.append(pltpu.make_async_remote_copy(
                    src_ref=ins[a].at[_block_index(peer)], dst_ref=outs[a].at[s], send_sem=send.at[sem],
                    recv_sem=recv.at[sem], device_id=peer, device_id_type=pl.DeviceIdType.MESH))
        return cps

    def start(self, ins, outs, send, recv, local):
        for cp in self._copies(ins, outs, send, recv):
            cp.start()

    def finish(self, ins, outs, send, recv, local):
        for cp in self._copies(ins, outs, send, recv):
            cp.wait()


class _PairSwap(_Plan):
    def __init__(self, arrs):
        self.ins = list(arrs)
        self.out_shapes = [SDS((4,) + a.shape[1:], a.dtype) for a in arrs]
        self.n_rdma, self.n_local = 4 * len(arrs), 0
        self.rdma_base = self.local_base = 0

    def _copies(self, ins, outs, send, recv):
        x, y, c = _me()
        cps = []
        for a in range(len(ins)):
            for s in range(4):
                sem = self.rdma_base + a * 4 + s
                cps.append(pltpu.make_async_remote_copy(
                    src_ref=ins[a].at[2 * s + (1 - c)], dst_ref=outs[a].at[s], send_sem=send.at[sem],
                    recv_sem=recv.at[sem], device_id=(x, y, 1 - c), device_id_type=pl.DeviceIdType.MESH))
        return cps

    def start(self, ins, outs, send, recv, local):
        for cp in self._copies(ins, outs, send, recv):
            cp.start()

    def finish(self, ins, outs, send, recv, local):
        for cp in self._copies(ins, outs, send, recv):
            cp.wait()


CHIP_X, CHIP_Y, CHIP_DIAGONAL = (1, 0), (0, 1), (1, 1)


class _ChipExchange(_Plan):
    def __init__(self, arrs, hops):
        self.ins, self.hops = list(arrs), tuple(hops)
        self.out_shapes = [SDS((len(hops),) + a.shape[1:], a.dtype) for a in arrs]
        self.n_rdma, self.n_local = len(hops) * len(arrs), 0
        self.rdma_base = self.local_base = 0

    def _copies(self, ins, outs, send, recv):
        x, y, c = _me()
        cps = []
        for a in range(len(ins)):
            for s, (fx, fy) in enumerate(self.hops):
                px, py = (1 - x) if fx else x, (1 - y) if fy else y
                sem = self.rdma_base + a * len(self.hops) + s
                cps.append(pltpu.make_async_remote_copy(
                    src_ref=ins[a].at[2 * px + py], dst_ref=outs[a].at[s], send_sem=send.at[sem],
                    recv_sem=recv.at[sem], device_id=(px, py, c), device_id_type=pl.DeviceIdType.MESH))
        return cps

    def start(self, ins, outs, send, recv, local):
        for cp in self._copies(ins, outs, send, recv):
            cp.start()

    def finish(self, ins, outs, send, recv, local):
        for cp in self._copies(ins, outs, send, recv):
            cp.wait()


class _Plans:
    def __init__(self, plans):
        self.plans = list(plans)
        self.ins = [a for p in plans for a in p.ins]
        self.out_shapes = [s for p in plans for s in p.out_shapes]
        self.n_rdma = self.n_local = 0
        for p in plans:
            p.rdma_base, p.local_base = self.n_rdma, self.n_local
            self.n_rdma, self.n_local = self.n_rdma + p.n_rdma, self.n_local + p.n_local

    def _each(self, ins, outs):
        i = o = 0
        for p in self.plans:
            yield p, ins[i:i + len(p.ins)], outs[o:o + len(p.out_shapes)]
            i, o = i + len(p.ins), o + len(p.out_shapes)

    def start(self, ins, outs, send, recv, local):
        for p, pi, po in self._each(ins, outs):
            p.start(pi, po, send, recv, local)

    def middle(self, ins, outs, send, recv, local):
        for p, pi, po in self._each(ins, outs):
            p.middle(pi, po, send, recv, local)

    def finish(self, ins, outs, send, recv, local):
        for p, pi, po in self._each(ins, outs):
            p.finish(pi, po, send, recv, local)


def _plan_scratch(plan):
    return [pltpu.SemaphoreType.DMA((plan.n_rdma,)), pltpu.SemaphoreType.DMA((plan.n_rdma,)),
            pltpu.SemaphoreType.DMA((max(plan.n_local, 1),))]


def _run_plan(name, plan):
    n = len(plan.ins)

    def body(*refs):
        ins, outs, sems = refs[:n], refs[n:len(refs) - 3], refs[len(refs) - 3:]
        plan.start(ins, outs, *sems)
        plan.middle(ins, outs, *sems)
        plan.finish(ins, outs, *sems)

    any_spec = pl.BlockSpec(memory_space=pl.ANY)
    return pl.pallas_call(body, name=name, out_shape=list(plan.out_shapes), in_specs=[any_spec] * n,
                          out_specs=[any_spec] * len(plan.out_shapes), scratch_shapes=_plan_scratch(plan))(*plan.ins)


def _pcall(name, body, grid, in_specs, ins, out_specs, out_shape, scratch=(), semantics=None, plan=None):
    if plan is None:
        return pl.pallas_call(body, name=name, grid=grid, in_specs=list(in_specs), out_specs=list(out_specs),
                              out_shape=list(out_shape), scratch_shapes=list(scratch),
                              compiler_params=_cparams(semantics))(*ins)
    n_in, n_out, n_scr = len(ins), len(out_shape), len(scratch)
    p_in, p_out = len(plan.ins), len(plan.out_shapes)

    def with_plan(*refs):
        k_in, c_in = refs[:n_in], refs[n_in:n_in + p_in]
        refs = refs[n_in + p_in:]
        k_out, c_out = refs[:n_out], refs[n_out:n_out + p_out]
        refs = refs[n_out + p_out:]
        k_scr, sems = refs[:n_scr], refs[n_scr:]
        step = 0
        for d, g in enumerate(grid):
            step = step * g + pl.program_id(d)
        n_steps = math.prod(grid)

        @pl.when(step == 0)
        def _():
            plan.start(c_in, c_out, *sems)

        @pl.when(step == min(n_steps - 1, int(n_steps * PLAN_MIDDLE)))
        def _():
            plan.middle(c_in, c_out, *sems)

        body(*k_in, *k_out, *k_scr)

        @pl.when(step == n_steps - 1)
        def _():
            plan.finish(c_in, c_out, *sems)

    any_spec = pl.BlockSpec(memory_space=pl.ANY)
    res = pl.pallas_call(
        with_plan, name=name, grid=grid, in_specs=list(in_specs) + [any_spec] * p_in,
        out_specs=list(out_specs) + [any_spec] * p_out, out_shape=list(out_shape) + list(plan.out_shapes),
        scratch_shapes=list(scratch) + _plan_scratch(plan),
        compiler_params=_cparams(("arbitrary",) * len(grid)))(*ins, *plan.ins)
    return res[:n_out], res[n_out:]


def _rowwise(name, fn, rows, vecs, row_outs, acc_outs=(), tm=128, plan=None):
    t = rows[0].shape[0]
    tm = min(tm, t)
    assert t % tm == 0 and tm % SUBLANES == 0
    n_r, n_v, n_o = len(rows), len(vecs), len(row_outs)

    def body(*refs):
        r_in, v_in = refs[:n_r], refs[n_r:n_r + n_v]
        r_out, a_out = refs[n_r + n_v:n_r + n_v + n_o], refs[n_r + n_v + n_o:]
        outs, accs = fn([r[...] for r in r_in], [v[...] for v in v_in])
        for o_ref, o in zip(r_out, outs, strict=True):
            o_ref[...] = o.astype(o_ref.dtype)
        if a_out:
            @pl.when(pl.program_id(0) == 0)
            def _():
                for a_ref in a_out:
                    a_ref[...] = jnp.zeros_like(a_ref)
            for a_ref, a in zip(a_out, accs, strict=True):
                a_ref[...] += a.reshape(tm // SUBLANES, SUBLANES, a.shape[-1]).sum(axis=0)

    in_specs = [pl.BlockSpec((tm, r.shape[1]), lambda i: (i, 0)) for r in rows]
    in_specs += [pl.BlockSpec(v.shape, lambda i: (0, 0)) for v in vecs]
    out_specs = [pl.BlockSpec((tm, w), lambda i: (i, 0)) for w, _ in row_outs]
    out_specs += [pl.BlockSpec((SUBLANES, w), lambda i: (0, 0)) for w in acc_outs]
    out_shape = [SDS((t, w), dt) for w, dt in row_outs] + [SDS((SUBLANES, w), F32) for w in acc_outs]
    return _pcall(name, body, (t // tm,), in_specs, [*rows, *vecs], out_specs, out_shape, semantics=("arbitrary",),
                  plan=plan)


def _tile(n, want):
    if n <= want:
        return n
    for t in range(want // LANES * LANES, 0, -LANES):
        if n % t == 0:
            return t
    raise ValueError(f"no tile for {n}")


_DOT_DIMS = {"nn": (((1,), (0,)), ((), ())), "nt": (((1,), (1,)), ((), ())), "tn": (((0,), (0,)), ((), ()))}


def _matmul(name, a, b, mode, out_dtypes, epilogue=None, extras=(), vecs=(), a_pro=None,
            tm=1024, tn=512, tk=4096, exact=False, b_blocked=False, out_blocked=0, b_rows=None, plan=None):
    cs = b.shape[-1] if b_blocked else None
    b2 = (b.shape[1], b.shape[0] * b.shape[2]) if b_blocked else b.shape
    if mode == "tn":
        (k, m), (k2, n) = a.shape, b2
    elif mode == "nt":
        (m, k), (n, k2) = a.shape, b2
    else:
        (m, k), (k2, n) = a.shape, b2
    assert k == k2 and not (b_blocked and mode == "tn")
    row0 = 0
    if b_rows is not None:
        assert mode == "nt" and not b_blocked
        row0, n = b_rows
        tn = _tile(math.gcd(n, row0) if row0 else n, tn)
    tm, tn, tk = _tile(m, tm), _tile(n, tn), _tile(k, tk)
    if b_blocked and mode == "nn":
        tn = _tile(cs, tn)
    if b_blocked and mode == "nt":
        tk = _tile(cs, tk)
    if out_blocked:
        tn = _tile(n // out_blocked, tn)
    nk = k // tk
    n_e, n_v, n_o = len(extras), len(vecs), len(out_dtypes)
    precision = HIGHEST if exact else None

    def body(*refs):
        a_ref, b_ref = refs[:2]
        e_refs, v_refs = refs[2:2 + n_e], refs[2 + n_e:2 + n_e + n_v]
        o_refs = refs[2 + n_e + n_v:2 + n_e + n_v + n_o]

        def product():
            av = a_ref[...]
            if a_pro is not None:
                av = a_pro(av)
            return lax.dot_general(av, b_ref[...], _DOT_DIMS[mode], precision=precision, preferred_element_type=F32)

        def finish(acc):
            res = (acc,) if epilogue is None else epilogue(acc, [e[...] for e in e_refs], [v[...] for v in v_refs])
            for o_ref, r in zip(o_refs, res, strict=True):
                o_ref[...] = r.astype(o_ref.dtype)

        if nk == 1:
            finish(product())
            return
        acc_ref = refs[-1]
        kk = pl.program_id(2)

        @pl.when(kk == 0)
        def _():
            acc_ref[...] = product()

        @pl.when(kk > 0)
        def _():
            acc_ref[...] += product()

        @pl.when(kk == nk - 1)
        def _():
            finish(acc_ref[...])

    if mode == "tn":
        a_spec = pl.BlockSpec((tk, tm), lambda i, j, kk: (kk, i))
    else:
        a_spec = pl.BlockSpec((tm, tk), lambda i, j, kk: (i, kk))
    if b_blocked and mode == "nn":
        per = cs // tn
        b_spec = pl.BlockSpec((None, tk, tn), lambda i, j, kk: (j // per, kk, j % per))
    elif b_blocked:
        per = cs // tk
        b_spec = pl.BlockSpec((None, tn, tk), lambda i, j, kk: (kk // per, j, kk % per))
    elif mode == "nt":
        assert row0 % tn == 0
        b_spec = pl.BlockSpec((tn, tk), lambda i, j, kk: (j + row0 // tn, kk))
    else:
        b_spec = pl.BlockSpec((tk, tn), lambda i, j, kk: (kk, j))
    tile = pl.BlockSpec((tm, tn), lambda i, j, kk: (i, j))
    if out_blocked:
        per_o = n // out_blocked // tn
        out_spec = pl.BlockSpec((None, tm, tn), lambda i, j, kk: (j // per_o, i, j % per_o))
        out_shape = [SDS((out_blocked, m, n // out_blocked), dt) for dt in out_dtypes]
    else:
        out_spec, out_shape = tile, [SDS((m, n), dt) for dt in out_dtypes]
    in_specs = [a_spec, b_spec] + [tile] * n_e + [pl.BlockSpec((1, tn), lambda i, j, kk: (0, j))] * n_v
    res = _pcall(name, body, (m // tm, n // tn, nk), in_specs, [a, b, *extras, *vecs], [out_spec] * n_o, out_shape,
                 scratch=[pltpu.VMEM((tm, tn), F32)] if nk > 1 else [],
                 semantics=("parallel", "parallel", "arbitrary"), plan=plan)
    if plan is None:
        return res[0] if n_o == 1 else res
    return (res[0][0] if n_o == 1 else res[0]), res[1]


def _rms_fwd(x):
    r = lax.rsqrt(jnp.mean(x * x, axis=-1, keepdims=True) + EPS)
    return x * r, r


def _rms_bwd(dxn, xn, r):
    return r * (dxn - xn * jnp.mean(dxn * xn, axis=-1, keepdims=True))


_INV_SQRT2 = 1.0 / math.sqrt(2.0)
_INV_SQRT2PI = 1.0 / math.sqrt(2.0 * math.pi)


def _gelu(y):
    return 0.5 * y * (1.0 + lax.erf(y * _INV_SQRT2))


def _gelu_grad(y):
    return 0.5 * (1.0 + lax.erf(y * _INV_SQRT2)) + y * (_INV_SQRT2PI * jnp.exp(-0.5 * y * y))


def _sigmoid(z):
    return 1.0 / (1.0 + jnp.exp(-z))


def _adam_math(w, g, m, v):
    m = ADAM_B1 * m + (1.0 - ADAM_B1) * g
    v = ADAM_B2 * v + (1.0 - ADAM_B2) * (g * g)
    m_hat = m / (1.0 - ADAM_B1 ** ADAM_STEP)
    v_hat = v / (1.0 - ADAM_B2 ** ADAM_STEP)
    delta = -ADAM_LR * (m_hat / (jnp.sqrt(v_hat) + ADAM_EPS) + ADAM_WD * w)
    return delta, m, v


def _norm_mod_fwd(name, x, g, scale, shift):
    def fn(rows, vecs):
        (xv,), (gv, sc, sh) = rows, vecs
        xn, _ = _rms_fwd(xv)
        return [(xn * gv) * (1.0 + sc) + sh], []
    return _rowwise(name, fn, [x], [g, scale, shift], [(x.shape[1], BF16)])[0]


def _norm_mod_bwd(name, x, dh, dres, g, scale, plan=None):
    d = x.shape[1]

    def fn(rows, vecs):
        (xv, dhv, drv), (gv, sc) = rows, vecs
        xn, r = _rms_fwd(xv)
        t = xn * gv
        dt = dhv * (1.0 + sc)
        dx = drv + _rms_bwd(dt * gv, xn, r)
        return [dx], [dhv * t, dhv, dt * xn]
    res = _rowwise(name, fn, [x, dh, dres], [g, scale], [(d, F32)], [d, d, d], plan=plan)
    (dx, dscale, dshift, dg), rest = res if plan is not None else (res, None)
    return (dx, dscale.sum(0), dshift.sum(0), dg.sum(0)), rest


def _gate_bwd(name, dx, val, gate):
    d = dx.shape[1]

    def fn(rows, vecs):
        (dxv, vv), (gv,) = rows, vecs
        return [dxv * gv], [dxv * vv.astype(F32)]
    dval, dgate = _rowwise(name, fn, [dx, val], [gate], [(d, BF16)], [d])
    return dval, dgate.sum(0)


def _final_loss(name, x, tgt, g):
    d = x.shape[1]

    def fn(rows, vecs):
        (xv, tv), (gv,) = rows, vecs
        xn, r = _rms_fwd(xv)
        e = xn * gv - tv
        dy = e * (1.0 / d)
        dx = _rms_bwd(dy * gv, xn, r)
        return [dx], [e * e, dy * xn]
    dx, sq, dg = _rowwise(name, fn, [x, tgt], [g], [(d, F32)], [d, d])
    return dx, 0.5 * jnp.sum(sq) / d, dg.sum(0)


def _group_norm_fwd(name, attn, ssm, g_a, g_s):
    def fn(rows, vecs):
        (av, sv), (ga, gs) = rows, vecs
        return [jnp.concatenate([_rms_fwd(av)[0] * ga, _rms_fwd(sv)[0] * gs], axis=1)], []
    return _rowwise(name, fn, [attn, ssm], [g_a, g_s], [(attn.shape[1] + ssm.shape[1], BF16)])[0]


def _group_norm_bwd(name, attn, ssm, dmixed, g_a, g_s):
    da_w, ds_w = attn.shape[1], ssm.shape[1]

    def fn(rows, vecs):
        (av, sv, dm), (ga, gs) = rows, vecs
        an, ra = _rms_fwd(av)
        sn, rs = _rms_fwd(sv)
        dma, dms = dm[:, :da_w], dm[:, da_w:]
        return [_rms_bwd(dma * ga, an, ra), _rms_bwd(dms * gs, sn, rs)], [dma * an, dms * sn]
    dattn, dssm, dga, dgs = _rowwise(name, fn, [attn, ssm, dmixed], [g_a, g_s],
                                     [(da_w, F32), (ds_w, F32)], [da_w, ds_w])
    return dattn, dssm, dga.sum(0), dgs.sum(0)


def _gelu_fwd(name, y):
    def fn(rows, vecs):
        return [_gelu(rows[0])], []
    return _rowwise(name, fn, [y], [], [(y.shape[1], BF16)])[0]


def _glu_bwd(name, dout, y, z):
    d = y.shape[1]

    def fn(rows, vecs):
        dov, yv, zv = rows
        sg = _sigmoid(zv)
        dz = dov * _gelu(yv) * sg * (1.0 - sg)
        return [dz, dov * sg], [dz]
    dz, dyg, db = _rowwise(name, fn, [dout, y, z], [], [(d, BF16), (d, F32)], [d])
    return dz, dyg, db.sum(0)


def _adam_shard(name, parts, w, m, v):
    r, c = w.shape
    n_parts = sum(1 if p.ndim == 2 else p.shape[0] for p in parts)
    row_bytes = 2 * c * (n_parts * parts[0].dtype.itemsize + 7 * 4)
    tr = min(128, r)
    while tr > SUBLANES and tr * row_bytes > VMEM_LIMIT_BYTES // 2:
        tr //= 2
    assert r % tr == 0
    n_p = len(parts)

    def body(*refs):
        p_refs, (w_ref, m_ref, v_ref, g_out, d_out, m_out, v_out) = refs[:n_p], refs[n_p:]
        g = None
        for p_ref in p_refs:
            terms = [p_ref[...]] if len(p_ref.shape) == 2 else [p_ref[j] for j in range(p_ref.shape[0])]
            for t in terms:
                g = t.astype(F32) if g is None else g + t.astype(F32)
        delta, m_new, v_new = _adam_math(w_ref[...], g, m_ref[...], v_ref[...])
        g_out[...], d_out[...], m_out[...], v_out[...] = g, delta, m_new, v_new

    tile = pl.BlockSpec((tr, c), lambda i: (i, 0))
    p_specs = [tile if p.ndim == 2 else pl.BlockSpec((p.shape[0], tr, c), lambda i: (0, i, 0)) for p in parts]
    return _pcall(name, body, (r // tr,), p_specs + [tile] * 3, [*parts, w, m, v], [tile] * 4, [SDS((r, c), F32)] * 4,
                  semantics=("parallel",))


def _pair_add(name, blocks, from_sibling):
    _, r, c = blocks.shape
    tr = min(256, r)
    assert r % tr == 0

    def body(b0_ref, b1_ref, s_ref, o_ref):
        mine = jnp.where(lax.axis_index("c") == 0, b0_ref[...].astype(F32), b1_ref[...].astype(F32))
        o_ref[...] = (mine + s_ref[...].astype(F32)).astype(o_ref.dtype)

    core_block = lambda k: pl.BlockSpec((None, tr, c), lambda s, i: (2 * s + k, i, 0))
    slot = pl.BlockSpec((None, tr, c), lambda s, i: (s, i, 0))
    return _pcall(name, body, (4, r // tr), [core_block(0), core_block(1), slot], [blocks, blocks, from_sibling],
                  [slot], [SDS((4, r, c), blocks.dtype)], semantics=("parallel", "parallel"))[0]


def _ada_update(name, c_act_t, dmod, w, m, v, tr=128, plan=None):
    r, c = w.shape
    tr = min(tr, r)
    assert r % tr == 0

    def body(c_ref, d_ref, w_ref, m_ref, v_ref, g_out, d_out, m_out, v_out):
        g = jnp.dot(c_ref[...], d_ref[...], precision=HIGHEST, preferred_element_type=F32)
        delta, m_new, v_new = _adam_math(w_ref[...], g, m_ref[...], v_ref[...])
        g_out[...], d_out[...], m_out[...], v_out[...] = g, delta, m_new, v_new

    tile = pl.BlockSpec((tr, c), lambda i: (i, 0))
    in_specs = [pl.BlockSpec((tr, N_DEV), lambda i: (i, 0)), pl.BlockSpec((N_DEV, c), lambda i: (0, 0)), tile, tile, tile]
    return _pcall(name, body, (r // tr,), in_specs, [c_act_t, dmod, w, m, v], [tile] * 4, [SDS((r, c), F32)] * 4,
                  semantics=("parallel",), plan=plan)


def _rotate_half(x):
    w = x.shape[1]
    half = HEAD_DIM // 2
    lane = lax.broadcasted_iota(jnp.int32, x.shape, 1)
    return jnp.where((lane % HEAD_DIM) < half, -pltpu.roll(x, w - half, 1), pltpu.roll(x, half, 1))


def _lane_tile(tab, w):
    return tab[:, :w] if w <= LANES else jnp.tile(tab, (1, w // LANES))


def _rope(x, cos, sin):
    return x * cos + _rotate_half(x) * sin


def _rope_t(dy, cos, sin):
    return dy * cos - _rotate_half(dy) * sin


def _band_mask(n):
    shape = (Q_PER_KV * WINDOW, 2 * WINDOW)
    i = lax.broadcasted_iota(jnp.int32, shape, 0) & (WINDOW - 1)
    j = lax.broadcasted_iota(jnp.int32, shape, 1)
    return (j > i) & (j <= i + WINDOW) & ((n > 0) | (j >= WINDOW))


def _stack_heads(x, hk):
    first = hk * Q_PER_KV
    return jnp.concatenate([x[:, (first + g) * HEAD_DIM:(first + g + 1) * HEAD_DIM] for g in range(Q_PER_KV)], axis=0)


def _stack_cols(ref, hk):
    first = hk * Q_PER_KV
    return jnp.concatenate([ref[:, first + g:first + g + 1] for g in range(Q_PER_KV)], axis=0)


def _stack_sinks(sink_ref, hk):
    first = hk * Q_PER_KV
    return jnp.concatenate([jnp.broadcast_to(sink_ref[0:1, first + g:first + g + 1], (WINDOW, 1))
                            for g in range(Q_PER_KV)], axis=0)


def _attn_specs(da, dkv, nb):
    cur = lambda n: (jnp.minimum(n, nb - 1), 0)
    prev = lambda n: (jnp.maximum(jnp.minimum(n, nb - 1) - 1, 0), 0)
    return dict(
        q=pl.BlockSpec((WINDOW, da), cur), kv_cur=pl.BlockSpec((WINDOW, dkv), cur),
        kv_prev=pl.BlockSpec((WINDOW, dkv), prev), tab_cur=pl.BlockSpec((WINDOW, LANES), cur),
        tab_prev=pl.BlockSpec((WINDOW, LANES), prev))


def _attn_fwd(name, q, k, v, cos, sin, sinks, plan=None):
    s, da = q.shape
    dkv = k.shape[1]
    nq, nb = da // HEAD_DIM, s // WINDOW
    scale = HEAD_DIM ** -0.5

    def body(q_ref, kp_ref, kc_ref, vp_ref, vc_ref, cc_ref, sc_ref, cp_ref, sp_ref, sink_ref, o_ref, lse_ref):
        n = pl.program_id(0)
        cc, sc, cp, sp = cc_ref[...], sc_ref[...], cp_ref[...], sp_ref[...]
        qr = _rope(q_ref[...], _lane_tile(cc, da), _lane_tile(sc, da)).astype(BF16)
        kk = jnp.concatenate([_rope(kp_ref[...], _lane_tile(cp, dkv), _lane_tile(sp, dkv)),
                              _rope(kc_ref[...], _lane_tile(cc, dkv), _lane_tile(sc, dkv))], axis=0).astype(BF16)
        vv = jnp.concatenate([vp_ref[...], vc_ref[...]], axis=0).astype(BF16)
        valid = _band_mask(n)
        for hk in range(nq // Q_PER_KV):
            ks = slice(hk * HEAD_DIM, (hk + 1) * HEAD_DIM)
            sco = lax.dot_general(_stack_heads(qr, hk), kk[:, ks], _DOT_DIMS["nt"], preferred_element_type=F32) * scale
            sco = jnp.where(valid, sco, -1e30)
            sink = _stack_sinks(sink_ref, hk)
            mx = jnp.maximum(jnp.max(sco, axis=1, keepdims=True), sink)
            p = jnp.exp(sco - mx)
            den = jnp.sum(p, axis=1, keepdims=True) + jnp.exp(sink - mx)
            o8 = jnp.dot((p / den).astype(BF16), vv[:, ks], preferred_element_type=F32)
            lse8 = mx + jnp.log(den)
            for g in range(Q_PER_KV):
                hq, rows = hk * Q_PER_KV + g, slice(g * WINDOW, (g + 1) * WINDOW)
                o_ref[:, hq * HEAD_DIM:(hq + 1) * HEAD_DIM] = o8[rows]
                lse_ref[:, hq:hq + 1] = lse8[rows]

    sp_ = _attn_specs(da, dkv, nb)
    in_specs = [sp_["q"], sp_["kv_prev"], sp_["kv_cur"], sp_["kv_prev"], sp_["kv_cur"],
                sp_["tab_cur"], sp_["tab_cur"], sp_["tab_prev"], sp_["tab_prev"], pl.BlockSpec((1, nq), lambda n: (0, 0))]
    return _pcall(name, body, (nb,), in_specs, [q, k, k, v, v, cos, sin, cos, sin, sinks],
                  [sp_["q"], pl.BlockSpec((WINDOW, nq), lambda n: (n, 0))], [SDS((s, da), F32), SDS((s, nq), F32)],
                  semantics=("arbitrary",), plan=plan)


def _attn_bwd(name, q, k, v, cos, sin, sinks, out, lse, dout, plan=None):
    s, da = q.shape
    dkv = k.shape[1]
    nq, nb = da // HEAD_DIM, s // WINDOW
    scale = HEAD_DIM ** -0.5

    def body(q_ref, kp_ref, kc_ref, vp_ref, vc_ref, cc_ref, sc_ref, cp_ref, sp_ref, sink_ref, o_ref, lse_ref,
             do_ref, dq_ref, dk_ref, dv_ref, dsink_ref, dk_carry, dv_carry):
        n = pl.program_id(0)
        cp, sp = _lane_tile(cp_ref[...], dkv), _lane_tile(sp_ref[...], dkv)

        @pl.when(n == 0)
        def _():
            dk_carry[...] = jnp.zeros_like(dk_carry)
            dv_carry[...] = jnp.zeros_like(dv_carry)
            dsink_ref[...] = jnp.zeros_like(dsink_ref)

        @pl.when(n < nb)
        def _():
            cc, sc = cc_ref[...], sc_ref[...]
            qr = _rope(q_ref[...], _lane_tile(cc, da), _lane_tile(sc, da)).astype(BF16)
            kk = jnp.concatenate([_rope(kp_ref[...], cp, sp),
                                  _rope(kc_ref[...], _lane_tile(cc, dkv), _lane_tile(sc, dkv))], axis=0).astype(BF16)
            vv = jnp.concatenate([vp_ref[...], vc_ref[...]], axis=0).astype(BF16)
            valid = _band_mask(n)
            do_all, o_all = do_ref[...], o_ref[...]
            for hk in range(nq // Q_PER_KV):
                ks = slice(hk * HEAD_DIM, (hk + 1) * HEAD_DIM)
                q8, lse8 = _stack_heads(qr, hk), _stack_cols(lse_ref, hk)
                sco = lax.dot_general(q8, kk[:, ks], _DOT_DIMS["nt"], preferred_element_type=F32) * scale
                probs = jnp.where(valid, jnp.exp(sco - lse8), 0.0)
                do8 = _stack_heads(do_all, hk)
                delta = jnp.sum(do8 * _stack_heads(o_all, hk), axis=1, keepdims=True)
                do8 = do8.astype(BF16)
                dp = lax.dot_general(do8, vv[:, ks], _DOT_DIMS["nt"], preferred_element_type=F32)
                ds = (probs * (dp - delta) * scale).astype(BF16)
                dq8 = jnp.dot(ds, kk[:, ks], preferred_element_type=F32)
                dk_h = lax.dot_general(ds, q8, _DOT_DIMS["tn"], preferred_element_type=F32)
                dv_h = lax.dot_general(probs.astype(BF16), do8, _DOT_DIMS["tn"], preferred_element_type=F32)
                dsink8 = -jnp.exp(_stack_sinks(sink_ref, hk) - lse8) * delta
                for g in range(Q_PER_KV):
                    hq, rows = hk * Q_PER_KV + g, slice(g * WINDOW, (g + 1) * WINDOW)
                    dq_ref[:, hq * HEAD_DIM:(hq + 1) * HEAD_DIM] = dq8[rows]
                    dsink_ref[:, hq:hq + 1] += dsink8[rows].reshape(WINDOW // SUBLANES, SUBLANES, 1).sum(axis=0)
                dk_ref[:, ks] = dk_carry[:, ks] + dk_h[:WINDOW]
                dv_ref[:, ks] = dv_carry[:, ks] + dv_h[:WINDOW]
                dk_carry[:, ks] = dk_h[WINDOW:]
                dv_carry[:, ks] = dv_h[WINDOW:]
            dq_ref[...] = _rope_t(dq_ref[...], _lane_tile(cc, da), _lane_tile(sc, da))
            dk_ref[...] = _rope_t(dk_ref[...], cp, sp)

        @pl.when(n == nb)
        def _():
            dk_ref[...] = _rope_t(dk_carry[...], cp, sp)
            dv_ref[...] = dv_carry[...]

    sp_ = _attn_specs(da, dkv, nb)
    last_prev = lambda n: (jnp.maximum(n - 1, 0), 0)
    tab_prev = pl.BlockSpec((WINDOW, LANES), last_prev)
    kv_out = pl.BlockSpec((WINDOW, dkv), last_prev)
    lse_spec = pl.BlockSpec((WINDOW, nq), lambda n: (jnp.minimum(n, nb - 1), 0))
    in_specs = [sp_["q"], sp_["kv_prev"], sp_["kv_cur"], sp_["kv_prev"], sp_["kv_cur"],
                sp_["tab_cur"], sp_["tab_cur"], tab_prev, tab_prev,
                pl.BlockSpec((1, nq), lambda n: (0, 0)), sp_["q"], lse_spec, sp_["q"]]
    res = _pcall(name, body, (nb + 1,), in_specs, [q, k, k, v, v, cos, sin, cos, sin, sinks, out, lse, dout],
                 [sp_["q"], kv_out, kv_out, pl.BlockSpec((SUBLANES, nq), lambda n: (0, 0))],
                 [SDS((s, da), F32), SDS((s, dkv), F32), SDS((s, dkv), F32), SDS((SUBLANES, nq), F32)],
                 scratch=[pltpu.VMEM((WINDOW, dkv), F32), pltpu.VMEM((WINDOW, dkv), F32)],
                 semantics=("arbitrary",), plan=plan)
    (dq, dk, dv, dsink), rest = res if plan is not None else (res, None)
    return (dq, dk, dv, dsink.sum(0)), rest


def _ssm_operators(lam_re, lam_im, log_step, b_re, b_im, c_re, c_im, d_skip):
    g, p = lam_re.shape
    h = b_re.shape[-1]
    l = SSM_CHUNK
    step = jnp.exp(log_step)[:, None]
    mag = jnp.exp(lam_re * step)
    ar, ai = mag * jnp.cos(lam_im * step), mag * jnp.sin(lam_im * step)
    den = lam_re * lam_re + lam_im * lam_im
    cr = ((ar - 1.0) * lam_re + ai * lam_im) / den
    ci = (ai * lam_re - (ar - 1.0) * lam_im) / den
    bbr = cr[..., None] * b_re - ci[..., None] * b_im
    bbi = cr[..., None] * b_im + ci[..., None] * b_re
    pr, pi = [jnp.ones_like(ar)], [jnp.zeros_like(ar)]
    for _ in range(l):
        pr, pi = pr + [pr[-1] * ar - pi[-1] * ai], pi + [pr[-1] * ai + pi[-1] * ar]
    pwr, pwi = jnp.stack(pr, axis=1), jnp.stack(pi, axis=1)
    cpr = c_re[:, None] * pwr[:, :, None, :] - c_im[:, None] * pwi[:, :, None, :]
    cpi = c_re[:, None] * pwi[:, :, None, :] + c_im[:, None] * pwr[:, :, None, :]
    kern = (jnp.einsum("gtop,gpi->gtoi", cpr[:, :l], bbr, precision=HIGHEST)
            - jnp.einsum("gtop,gpi->gtoi", cpi[:, :l], bbi, precision=HIGHEST))
    kern = kern.at[:, 0].add(d_skip.reshape(g, h)[:, :, None] * jnp.eye(h, dtype=F32))
    tm = jnp.stack([jnp.pad(kern[:, :l - j], ((0, 0), (j, 0), (0, 0), (0, 0))) for j in range(l)], axis=1)
    tm = tm.transpose(0, 1, 4, 2, 3).reshape(g, l * h, l * h)
    rev_r, rev_i = pwr[:, l - 1::-1][:, :l], pwi[:, l - 1::-1][:, :l]
    er = rev_r[:, :, None, :] * bbr.transpose(0, 2, 1)[:, None] - rev_i[:, :, None, :] * bbi.transpose(0, 2, 1)[:, None]
    ei = rev_r[:, :, None, :] * bbi.transpose(0, 2, 1)[:, None] + rev_i[:, :, None, :] * bbr.transpose(0, 2, 1)[:, None]
    em = jnp.concatenate([er, ei], axis=-1).reshape(g, l * h, 2 * p)
    fr = cpr[:, 1:].transpose(0, 3, 1, 2).reshape(g, p, l * h)
    fi = -cpi[:, 1:].transpose(0, 3, 1, 2).reshape(g, p, l * h)
    fm = jnp.concatenate([fr, fi], axis=1)
    return tm, em, fm, pwr[:, l], pwi[:, l]


def _decay_lanes(alr, ali):
    return jnp.concatenate([alr, alr], axis=1), jnp.concatenate([-ali, ali], axis=1)


def _ssm_fwd(name, u, tm, em, fm, acat, bcat, plan=None):
    s, ds = u.shape
    g, lh, p2 = em.shape
    gb, h = SSM_GROUPS_PER_STEP, lh // SSM_CHUNK
    assert gb * h == LANES and g * h == ds and s % SSM_CHUNK == 0
    nc, half = s // SSM_CHUNK, p2 // 2

    def body(u_ref, tm_ref, em_ref, fm_ref, a_ref, b_ref, y_ref, xp_ref, uc_ref, yc_ref, st_ref):
        _to_chunks(u_ref, uc_ref, nc, h)
        for i in range(gb):
            st_ref[pl.ds(i, nc, stride=gb), :] = jnp.dot(uc_ref[i], em_ref[i], precision=HIGHEST,
                                                         preferred_element_type=F32)
        av, bv = a_ref[...], b_ref[...]

        def step(c, carry):
            x, xs = carry
            rows = pl.ds(pl.multiple_of(c * gb, gb), gb)
            loc = st_ref[rows, :]
            st_ref[rows, :] = x
            return av * x + bv * xs + loc, av * xs - bv * x + pltpu.roll(loc, half, 1)
        zero = jnp.zeros((gb, p2), F32)
        lax.fori_loop(0, nc, step, (zero, zero), unroll=4)
        for i in range(gb):
            xp = st_ref[pl.ds(i, nc, stride=gb), :]
            xp_ref[i] = xp
            yc_ref[i] = (jnp.dot(uc_ref[i], tm_ref[i], precision=HIGHEST, preferred_element_type=F32)
                         + jnp.dot(xp, fm_ref[i], precision=HIGHEST, preferred_element_type=F32))
        _from_chunks(yc_ref, y_ref, nc, h)

    blk = lambda r, c: pl.BlockSpec((gb, r, c), lambda i: (i, 0, 0))
    vec = pl.BlockSpec((gb, p2), lambda i: (i, 0))
    col = pl.BlockSpec((s, LANES), lambda i: (0, i))
    return _pcall(name, body, (g // gb,), [col, blk(lh, lh), blk(lh, p2), blk(p2, lh), vec, vec],
                  [u, tm, em, fm, acat, bcat], [col, blk(nc, p2), blk(nc, lh)],
                  [SDS((s, ds), F32), SDS((g, nc, p2), F32), SDS((g, nc, lh), F32)],
                  scratch=[pltpu.VMEM((gb, nc, lh), F32), pltpu.VMEM((nc * gb, p2), F32)],
                  semantics=("parallel",), plan=plan)


def _ssm_bwd(name, u_chunks, dy, xprev, tm, em, fm, acat, bcat, plan=None):
    s, ds = dy.shape
    g, lh, p2 = em.shape
    gb, h = SSM_GROUPS_PER_STEP, lh // SSM_CHUNK
    nc, half = s // SSM_CHUNK, p2 // 2

    def body(uc_ref, dy_ref, xp_ref, tm_ref, em_ref, fm_ref, a_ref, b_ref,
             du_ref, dtm_ref, dem_ref, dfm_ref, r1_ref, r2_ref, dyc_ref, duc_ref, gs_ref, xs_ref):
        _to_chunks(dy_ref, dyc_ref, nc, h)
        for i in range(gb):
            gs_ref[pl.ds(i, nc, stride=gb), :] = lax.dot_general(
                dyc_ref[i], fm_ref[i], _DOT_DIMS["nt"], precision=HIGHEST, preferred_element_type=F32)
            xs_ref[pl.ds(i, nc, stride=gb), :] = xp_ref[i]
        av, bv = a_ref[...], b_ref[...]

        def step(t, carry):
            grad, gsw, r1, r2 = carry
            c = nc - 1 - t
            rows = pl.ds(pl.multiple_of(c * gb, gb), gb)
            dxp, xp = gs_ref[rows, :], xs_ref[rows, :]
            gs_ref[rows, :] = grad
            r1 = r1 + grad * xp
            r2 = r2 + grad * pltpu.roll(xp, half, 1)
            return dxp + av * grad - bv * gsw, pltpu.roll(dxp, half, 1) + av * gsw + bv * grad, r1, r2
        zero = jnp.zeros((gb, p2), F32)
        _, _, r1, r2 = lax.fori_loop(0, nc, step, (zero, zero, zero, zero), unroll=4)
        r1_ref[...], r2_ref[...] = r1, r2
        for i in range(gb):
            dxl = gs_ref[pl.ds(i, nc, stride=gb), :]
            duc_ref[i] = (lax.dot_general(dyc_ref[i], tm_ref[i], _DOT_DIMS["nt"], precision=HIGHEST,
                                          preferred_element_type=F32)
                          + lax.dot_general(dxl, em_ref[i], _DOT_DIMS["nt"], precision=HIGHEST,
                                            preferred_element_type=F32))
            dtm_ref[i] = lax.dot_general(uc_ref[i], dyc_ref[i], _DOT_DIMS["tn"], precision=HIGHEST,
                                         preferred_element_type=F32)
            dfm_ref[i] = lax.dot_general(xp_ref[i], dyc_ref[i], _DOT_DIMS["tn"], precision=HIGHEST,
                                         preferred_element_type=F32)
            dem_ref[i] = lax.dot_general(uc_ref[i], dxl, _DOT_DIMS["tn"], precision=HIGHEST,
                                         preferred_element_type=F32)
        _from_chunks(duc_ref, du_ref, nc, h)

    blk = lambda r, c: pl.BlockSpec((gb, r, c), lambda i: (i, 0, 0))
    vec = pl.BlockSpec((gb, p2), lambda i: (i, 0))
    col = pl.BlockSpec((s, LANES), lambda i: (0, i))
    chunked = pltpu.VMEM((gb, nc, lh), F32)
    res = _pcall(name, body, (g // gb,),
                 [blk(nc, lh), col, blk(nc, p2), blk(lh, lh), blk(lh, p2), blk(p2, lh), vec, vec],
                 [u_chunks, dy, xprev, tm, em, fm, acat, bcat],
                 [col, blk(lh, lh), blk(lh, p2), blk(p2, lh), vec, vec],
                 [SDS((s, ds), F32), SDS((g, lh, lh), F32), SDS((g, lh, p2), F32), SDS((g, p2, lh), F32),
                  SDS((g, p2), F32), SDS((g, p2), F32)],
                 scratch=[chunked, chunked, pltpu.VMEM((nc * gb, p2), F32), pltpu.VMEM((nc * gb, p2), F32)],
                 semantics=("parallel",), plan=plan)
    return res if plan is not None else (res, None)


def _to_chunks(src_ref, dst_ref, nc, h):
    per = LANES // h
    grp = lax.broadcasted_iota(jnp.int32, (nc, LANES), 1) // h
    for g in range(per):
        for part in range(SSM_CHUNK * h // LANES):
            acc = None
            for i in range(part * per, (part + 1) * per):
                piece = src_ref[pl.ds(i, nc, stride=SSM_CHUNK), :]
                lo = (i * h) % LANES
                if (lo - g * h) % LANES:
                    piece = pltpu.roll(piece, (lo - g * h) % LANES, 1)
                acc = piece if acc is None else jnp.where(grp == lo // h, piece, acc)
            dst_ref[g, :, part * LANES:(part + 1) * LANES] = acc


def _from_chunks(src_ref, dst_ref, nc, h):
    per = LANES // h
    grp = lax.broadcasted_iota(jnp.int32, (nc, LANES), 1) // h
    for i in range(SSM_CHUNK):
        part, lo = divmod(i * h, LANES)
        row = None
        for g in range(per):
            piece = src_ref[g, :, part * LANES:(part + 1) * LANES]
            if (g * h - lo) % LANES:
                piece = pltpu.roll(piece, (g * h - lo) % LANES, 1)
            row = piece if row is None else jnp.where(grp == g, piece, row)
        dst_ref[pl.ds(i, nc, stride=SSM_CHUNK), :] = row


_SMALL = ("b_ada", "norm1_g", "sinks", "ssm_lam_re", "ssm_lam_im", "ssm_log_step", "ssm_b_re", "ssm_b_im",
          "ssm_c_re", "ssm_c_im", "ssm_d", "b_glu", "attn_out_g", "ssm_out_g", "norm2_g", "final_g")
_WEIGHTS = ("w_ada", "b_ada", "norm1_g", "w_in", "sinks", "ssm_lam_re", "ssm_lam_im", "ssm_log_step", "ssm_b_re",
            "ssm_b_im", "ssm_c_re", "ssm_c_im", "ssm_d", "w_glu", "b_glu", "attn_out_g", "ssm_out_g", "w_out",
            "norm2_g", "w_ff1", "w_ff2", "final_g")
_PACK_ALIGN = 128 * LANES


def _pack(parts):
    flat = jnp.concatenate([p.reshape(-1).astype(F32) for p in parts])
    pad = (-flat.shape[0]) % _PACK_ALIGN
    return jnp.pad(flat, (0, pad)).reshape(-1, LANES)


def kernel(x, c, w_ada, b_ada, norm1_g, w_in, sinks, ssm_lam_re, ssm_lam_im, ssm_log_step, ssm_b_re, ssm_b_im, ssm_c_re, ssm_c_im, ssm_d, w_glu, b_glu, attn_out_g, ssm_out_g, w_out, norm2_g, w_ff1, w_ff2, final_g, loss_target, m_w_ada, m_b_ada, m_norm1_g, m_w_in, m_sinks, m_ssm_lam_re, m_ssm_lam_im, m_ssm_log_step, m_ssm_b_re, m_ssm_b_im, m_ssm_c_re, m_ssm_c_im, m_ssm_d, m_w_glu, m_b_glu, m_attn_out_g, m_ssm_out_g, m_w_out, m_norm2_g, m_w_ff1, m_w_ff2, m_final_g, v_w_ada, v_b_ada, v_norm1_g, v_w_in, v_sinks, v_ssm_lam_re, v_ssm_lam_im, v_ssm_log_step, v_ssm_b_re, v_ssm_b_im, v_ssm_c_re, v_ssm_c_im, v_ssm_d, v_w_glu, v_b_glu, v_attn_out_g, v_ssm_out_g, v_w_out, v_norm2_g, v_w_ff1, v_w_ff2, v_final_g):
    args = dict(locals())
    weights = {n: args[n] for n in _WEIGHTS}
    mom = {n: args["m_" + n] for n in _WEIGHTS}
    var = {n: args["v_" + n] for n in _WEIGHTS}
    me = 4 * lax.axis_index("x") + 2 * lax.axis_index("y") + lax.axis_index("c")

    _, s, d = x.shape
    xs, tgt = x[0], loss_target[0]
    d_ssm = ssm_d.shape[-1]
    d_attn = d - d_ssm
    nq = d_attn // HEAD_DIM
    d_kv = (nq // Q_PER_KV) * HEAD_DIM
    p_state = ssm_b_re.shape[2]

    c_all, g_in = _run_plan("gather_c_w_in", _Gather([c, w_in[0].T.astype(BF16)]))
    c_all = c_all.reshape(N_DEV, d)
    w_in_t = g_in.reshape(-1, d)

    n_loc = w_ada.shape[-1]
    b_loc = lax.dynamic_slice_in_dim(b_ada, me * n_loc, n_loc, axis=1)
    silu = lambda t: t * _sigmoid(t)
    mod_part = _matmul("ada_mod", c_all, w_ada[0], "nn", [F32], a_pro=silu, vecs=[b_loc], exact=True,
                       epilogue=lambda acc, e, v: (acc + v[0],), tn=512, tk=d)
    mod_all = _run_plan("gather_mod", _Gather([mod_part]))[0]
    mod = lax.dynamic_index_in_dim(mod_all, me, axis=1, keepdims=False).reshape(N_MOD, 1, d)
    shift1, scale1, gate1, shift2, scale2, gate2 = [mod[i] for i in range(N_MOD)]

    h1 = _norm_mod_fwd("norm1", xs, norm1_g, scale1, shift1)
    q = _matmul("proj_q", h1, w_in_t, "nt", [F32], b_rows=(0, d_attn))
    kv = _matmul("proj_kv", h1, w_in_t, "nt", [F32], b_rows=(d_attn, 2 * d_kv))
    u = _matmul("proj_u", h1, w_in_t, "nt", [F32], b_rows=(d_attn + 2 * d_kv, d_ssm))
    k, v = kv[:, :d_kv], kv[:, d_kv:]

    half = HEAD_DIM // 2
    inv_freq = ROPE_THETA ** (-jnp.arange(half, dtype=F32) / half)
    ang = jnp.arange(s, dtype=F32)[:, None] * inv_freq[None, :]
    cos_t, sin_t = jnp.tile(jnp.cos(ang), (1, 4)), jnp.tile(jnp.sin(ang), (1, 4))
    (attn, lse), (g_glu, g_out) = _attn_fwd("attn_fwd", q, k, v, cos_t, sin_t, sinks,
                                            plan=_Gather([w_glu[0].astype(BF16), w_out[0].astype(BF16)]))
    w_glu_f = g_glu.reshape(d_ssm, d_ssm)
    w_out_f = g_out.reshape(d, d)

    ssm_params = (ssm_lam_re[0], ssm_lam_im[0], ssm_log_step[0], ssm_b_re[0], ssm_b_im[0], ssm_c_re[0],
                  ssm_c_im[0], ssm_d[0])
    (tm_op, em_op, fm_op, alr, ali), ssm_vjp = jax.vjp(_ssm_operators, *ssm_params)
    acat, bcat = _decay_lanes(alr, ali)
    (y_ssm, x_prev, u_chunks), (g_ff1,) = _ssm_fwd("ssm_fwd", u, tm_op, em_op, fm_op, acat, bcat,
                                                   plan=_Gather([w_ff1[0].astype(BF16)]))
    yg = _gelu_fwd("gelu", y_ssm)
    ssm_out, z_glu = _matmul(
        "glu", yg, w_glu_f, "nn", [F32, F32], extras=[y_ssm], vecs=[b_glu],
        epilogue=lambda acc, e, v: (_gelu(e[0]) * _sigmoid(acc + v[0]), acc + v[0]))
    mixed = _group_norm_fwd("group_norm", attn, ssm_out, attn_out_g, ssm_out_g)
    x2, mo = _matmul("out_proj", mixed, w_out_f, "nn", [F32, BF16], extras=[xs], vecs=[gate1],
                     epilogue=lambda acc, e, v: (e[0] + v[0] * acc, acc))

    h2 = _norm_mod_fwd("norm2", x2, norm2_g, scale2, shift2)
    (a_ff, f_ff), (g_ff2,) = _matmul("ff1", h2, g_ff1, "nn", [BF16, BF16], b_blocked=True,
                                     epilogue=lambda acc, e, v: (acc, jnp.square(jnp.maximum(acc, 0.0))),
                                     plan=_Gather([w_ff2[0].astype(BF16)]))
    w_ff2_f = g_ff2.reshape(-1, d)
    x3, ff = _matmul("ff2", f_ff, w_ff2_f, "nn", [F32, BF16], extras=[x2], vecs=[gate2],
                     epilogue=lambda acc, e, v: (e[0] + v[0] * acc, acc))

    dx3, loss_local, d_final_g = _final_loss("final_loss", x3, tgt, final_g.reshape(1, d))
    loss = lax.psum(loss_local, MESH_AXES)

    dff, d_gate2 = _gate_bwd("gate2_bwd", dx3, ff, gate2)
    dw_ff2 = _matmul("ff2_dw", f_ff, dff, "tn", [BF16]).reshape(N_DEV, -1, d)
    da_ff, (p_ff2,) = _matmul("ff2_dx", dff, w_ff2_f, "nt", [BF16], extras=[a_ff],
                              epilogue=lambda acc, e, v: (acc * (2.0 * jnp.maximum(e[0].astype(F32), 0.0)),),
                              plan=_PairSwap([dw_ff2]))
    s_ff2 = _pair_add("pair_add_ff2", dw_ff2, p_ff2)
    dw_ff1, (r_ff2_a,) = _matmul("ff1_dw", h2, da_ff, "tn", [BF16], out_blocked=N_DEV,
                                 plan=_ChipExchange([s_ff2], (CHIP_X, CHIP_Y)))
    dh2, (r_ff2_b,) = _matmul("ff1_dx", da_ff, g_ff1, "nt", [F32], b_blocked=True,
                              plan=_ChipExchange([s_ff2], (CHIP_DIAGONAL,)))
    (dx2, d_scale2, d_shift2, d_norm2_g), (p_ff1,) = _norm_mod_bwd("norm2_bwd", x2, dh2, dx3, norm2_g, scale2,
                                                                   plan=_PairSwap([dw_ff1]))
    s_ff1 = _pair_add("pair_add_ff1", dw_ff1, p_ff1)

    dmo, d_gate1 = _gate_bwd("gate1_bwd", dx2, mo, gate1)
    dw_out = _matmul("out_dw", mixed, dmo, "tn", [BF16]).reshape(N_DEV, -1, d)
    dmixed, (r_out_a,) = _matmul("out_dx", dmo, w_out_f, "nt", [F32], plan=_Exchange([dw_out], (1, 4, 2)))
    dattn, dssm_out, d_attn_g, d_ssm_g = _group_norm_bwd("group_norm_bwd", attn, ssm_out, dmixed, attn_out_g, ssm_out_g)

    dz, dyg_direct, d_b_glu = _glu_bwd("glu_bwd", dssm_out, y_ssm, z_glu)
    dw_glu = _matmul("glu_dw", yg, dz, "tn", [BF16]).reshape(N_DEV, -1, d_ssm)
    dy_ssm = _matmul("glu_dx", dz, w_glu_f, "nt", [F32], extras=[dyg_direct, y_ssm],
                     epilogue=lambda acc, e, v: ((acc + e[0]) * _gelu_grad(e[1]),))
    (du, d_tm, d_em, d_fm, r1, r2), (r_ff1_a, r_out_b) = _ssm_bwd(
        "ssm_bwd", u_chunks, dy_ssm, x_prev, tm_op, em_op, fm_op, acat, bcat,
        plan=_Plans([_ChipExchange([s_ff1], (CHIP_X, CHIP_Y)), _Exchange([dw_out], (5, 3, 6))]))
    d_alr = r1[:, :p_state] + r1[:, p_state:]
    d_ali = r2[:, p_state:] - r2[:, :p_state]
    d_ssm_params = ssm_vjp((d_tm, d_em, d_fm, d_alr, d_ali))

    (dq, dk, dv, d_sinks), (r_ff1_b,) = _attn_bwd("attn_bwd", q, k, v, cos_t, sin_t, sinks, attn, lse, dattn,
                                                  plan=_ChipExchange([s_ff1], (CHIP_DIAGONAL,)))
    dproj = jnp.concatenate([dq, dk, dv, du], axis=1).astype(BF16)
    dw_in_t, (r_out_c, r_glu) = _matmul(
        "in_dw", dproj, h1, "tn", [BF16],
        plan=_Plans([_Exchange([dw_out], (7,)), _Exchange([dw_glu], RELATIONS_ALL)]))
    dw_in_t = dw_in_t.reshape(N_DEV, -1, d)
    dh1, (r_in_a,) = _matmul("in_dx", dproj, w_in_t, "nn", [F32], plan=_Exchange([dw_in_t], RELATIONS_SAME_CORE))
    (grad_x, d_scale1, d_shift1, d_norm1_g), _ = _norm_mod_bwd("norm1_bwd", xs, dh1, dx2, norm1_g, scale1)

    d_mod = jnp.concatenate([d_shift1, d_scale1, d_gate1, d_shift2, d_scale2, d_gate2])
    small_g = dict(zip(("ssm_lam_re", "ssm_lam_im", "ssm_log_step", "ssm_b_re", "ssm_b_im", "ssm_c_re", "ssm_c_im",
                        "ssm_d"), d_ssm_params, strict=True))
    small_g.update(b_ada=d_mod, norm1_g=d_norm1_g, sinks=d_sinks, b_glu=d_b_glu, attn_out_g=d_attn_g,
                   ssm_out_g=d_ssm_g, norm2_g=d_norm2_g, final_g=d_final_g)
    small_parts, r_in_b = _run_plan("gather_small_grads", _Plans([_Gather([_pack([small_g[n] for n in _SMALL])]),
                                                                  _Exchange([dw_in_t], RELATIONS_OTHER_CORE)]))
    small = _adam_shard("adam_small", [small_parts], _pack([weights[n] for n in _SMALL]),
                        _pack([mom[n] for n in _SMALL]), _pack([var[n] for n in _SMALL]))
    out = {}
    off = 0
    for n in _SMALL:
        size = weights[n].size
        out[n] = [t.reshape(-1)[off:off + size].reshape(weights[n].shape) for t in small]
        off += size

    dmod_all = small_parts.reshape(N_DEV, -1)[:, :N_MOD * d]
    dmod_loc = lax.dynamic_slice_in_dim(dmod_all, me * n_loc, n_loc, axis=1)
    c_act_t = silu(c_all).T
    out["w_ada"] = [t[None] for t in _ada_update("adam_w_ada", c_act_t, dmod_loc, w_ada[0], m_w_ada[0], v_w_ada[0])]

    mine = lambda blocks: lax.dynamic_index_in_dim(blocks, me, axis=0, keepdims=False)
    in_parts = [mine(dw_in_t).T] + [r.transpose(0, 2, 1) for r in (r_in_a, r_in_b)]
    my_chip = 2 * lax.axis_index("x") + lax.axis_index("y")
    chip_sum = lambda sums: lax.dynamic_index_in_dim(sums, my_chip, axis=0, keepdims=False)
    received = dict(w_in=in_parts, w_glu=[mine(dw_glu), r_glu], w_out=[mine(dw_out), r_out_a, r_out_b, r_out_c],
                    w_ff1=[chip_sum(s_ff1), r_ff1_a, r_ff1_b], w_ff2=[chip_sum(s_ff2), r_ff2_a, r_ff2_b])
    for n, parts in received.items():
        out[n] = [t[None] for t in _adam_shard("adam_" + n, parts, weights[n][0], mom[n][0], var[n][0])]

    return (loss, grad_x[None], *[out[n][0] for n in _WEIGHTS], *[out[n][1] for n in _WEIGHTS],
            *[out[n][2] for n in _WEIGHTS], *[out[n][3] for n in _WEIGHTS])
```

```python
import math

import jax
import jax.numpy as jnp
from jax import lax
from jax.experimental import pallas as pl
from jax.experimental.pallas import tpu as pltpu

F32, BF16 = jnp.float32, jnp.bfloat16
SDS = jax.ShapeDtypeStruct
MESH_AXES = ("x", "y", "c")
N_DEV = 8
VMEM_LIMIT_BYTES = 56 * 1024 * 1024
MATMUL_VMEM_BUDGET = 44 * 1024 * 1024
SUBLANES, LANES = 8, 128

HEAD_DIM = 64
Q_PER_KV = 8
WINDOW = 128
ROPE_THETA = 10000.0
EPS = 1e-6
N_MOD = 6
SSM_CHUNK = 16
SSM_GROUPS_PER_STEP = 8

ADAM_LR, ADAM_B1, ADAM_B2, ADAM_EPS, ADAM_WD, ADAM_STEP = 0.001, 0.9, 0.999, 1e-08, 0.01, 10
HIGHEST = lax.Precision.HIGHEST
SSM_PRECISION = lax.Precision.HIGH

RELATIONS_ALL = (1, 4, 2, 6, 5, 3, 7)
RELATIONS_SAME_CORE = (1, 4, 2, 6)
RELATIONS_OTHER_CORE = (5, 3, 7)
PLAN_MIDDLE = 0.65


def _cparams(sem):
    return pltpu.CompilerParams(dimension_semantics=sem, vmem_limit_bytes=VMEM_LIMIT_BYTES)


def _block_index(p):
    return 4 * p[0] + 2 * p[1] + p[2]


def _me():
    return lax.axis_index("x"), lax.axis_index("y"), lax.axis_index("c")


class _Plan:
    def middle(self, ins, outs, send, recv, local):
        pass


class _Gather(_Plan):
    TO_SIBLING, TO_X, TO_Y, RELAY, PASS_X, PASS_Y, PASS_DIAGONAL = range(7)

    def __init__(self, arrs):
        self.ins = list(arrs)
        self.out_shapes = [SDS((N_DEV,) + a.shape, a.dtype) for a in arrs]
        self.n_rdma, self.n_local = 7 * len(arrs), len(arrs)
        self.rdma_base = self.local_base = 0

    def _copy(self, ins, outs, send, recv, a, k, block, to, from_input=False):
        dst = outs[a].at[_block_index(block)]
        sem = self.rdma_base + a * 7 + k
        return pltpu.make_async_remote_copy(
            src_ref=ins[a] if from_input else dst, dst_ref=dst, send_sem=send.at[sem], recv_sem=recv.at[sem],
            device_id=to, device_id_type=pl.DeviceIdType.MESH)

    @staticmethod
    def _places():
        x, y, c = _me()
        return (x, y, c), (x, y, 1 - c), (1 - x, y, c), (x, 1 - y, c), (1 - x, 1 - y, c)

    def _first(self, ins, outs, send, recv, a):
        me, sibling, x_nbr, y_nbr, _ = self._places()
        return [self._copy(ins, outs, send, recv, a, k, me, to, True)
                for k, to in ((self.TO_SIBLING, sibling), (self.TO_X, x_nbr), (self.TO_Y, y_nbr))]

    def _mine(self, ins, outs, local, a):
        return pltpu.make_async_copy(ins[a], outs[a].at[_block_index(_me())], local.at[self.local_base + a])

    def start(self, ins, outs, send, recv, local):
        for a in range(len(ins)):
            self._mine(ins, outs, local, a).start()
            for cp in self._first(ins, outs, send, recv, a):
                cp.start()

    def middle(self, ins, outs, send, recv, local):
        me, sibling, x_nbr, y_nbr, _ = self._places()
        core = me[2]
        for a in range(len(ins)):
            self._copy(ins, outs, send, recv, a, self.TO_X, x_nbr, me).wait_recv()
            self._copy(ins, outs, send, recv, a, self.TO_Y, y_nbr, me).wait_recv()

            @pl.when(core == 0)
            def _():
                self._copy(ins, outs, send, recv, a, self.RELAY, x_nbr, y_nbr).start()

            @pl.when(core == 1)
            def _():
                self._copy(ins, outs, send, recv, a, self.RELAY, y_nbr, x_nbr).start()

            self._copy(ins, outs, send, recv, a, self.PASS_X, x_nbr, sibling).start()
            self._copy(ins, outs, send, recv, a, self.PASS_Y, y_nbr, sibling).start()

    def finish(self, ins, outs, send, recv, local):
        me, sibling, x_nbr, y_nbr, diagonal = self._places()
        other = lambda p: (p[0], p[1], 1 - p[2])
        for a in range(len(ins)):
            self._copy(ins, outs, send, recv, a, self.RELAY, diagonal, me).wait_recv()
            self._copy(ins, outs, send, recv, a, self.PASS_DIAGONAL, diagonal, sibling).start()
        for a in range(len(ins)):
            self._copy(ins, outs, send, recv, a, self.TO_SIBLING, sibling, me).wait_recv()
            for k, src in ((self.PASS_X, x_nbr), (self.PASS_Y, y_nbr), (self.PASS_DIAGONAL, diagonal)):
                self._copy(ins, outs, send, recv, a, k, other(src), me).wait_recv()
                self._copy(ins, outs, send, recv, a, k, src, sibling).wait_send()
            for cp in self._first(ins, outs, send, recv, a):
                cp.wait_send()
            self._copy(ins, outs, send, recv, a, self.RELAY, me, me).wait_send()
            self._mine(ins, outs, local, a).wait()


class _Exchange(_Plan):
    def __init__(self, arrs, relations):
        self.ins, self.relations = list(arrs), tuple(relations)
        self.out_shapes = [SDS((len(relations),) + a.shape[1:], a.dtype) for a in arrs]
        self.n_rdma, self.n_local = len(relations) * len(arrs), 0
        self.rdma_base = self.local_base = 0

    def _copies(self, ins, outs, send, recv):
        x, y, c = _me()
        cps = []
        for a in range(len(ins)):
            for s, k in enumerate(self.relations):
                peer = ((1 - x) if (k & 4) else x, (1 - y) if (k & 2) else y, (1 - c) if (k & 1) else c)
                sem = self.rdma_base + a * len(self.relations) + s
                cps.append(pltpu.make_async_remote_copy(
                    src_ref=ins[a].at[_block_index(peer)], dst_ref=outs[a].at[s], send_sem=send.at[sem],
                    recv_sem=recv.at[sem], device_id=peer, device_id_type=pl.DeviceIdType.MESH))
        return cps

    def start(self, ins, outs, send, recv, local):
        for cp in self._copies(ins, outs, send, recv):
            cp.start()

    def finish(self, ins, outs, send, recv, local):
        for cp in self._copies(ins, outs, send, recv):
            cp.wait()


class _PairSwap(_Plan):
    def __init__(self, arrs):
        self.ins = list(arrs)
        self.out_shapes = [SDS((4,) + a.shape[1:], a.dtype) for a in arrs]
        self.n_rdma, self.n_local = 4 * len(arrs), 0
        self.rdma_base = self.local_base = 0

    def _copies(self, ins, outs, send, recv):
        x, y, c = _me()
        cps = []
        for a in range(len(ins)):
            for s in range(4):
                sem = self.rdma_base + a * 4 + s
                cps.append(pltpu.make_async_remote_copy(
                    src_ref=ins[a].at[2 * s + (1 - c)], dst_ref=outs[a].at[s], send_sem=send.at[sem],
                    recv_sem=recv.at[sem], device_id=(x, y, 1 - c), device_id_type=pl.DeviceIdType.MESH))
        return cps

    def start(self, ins, outs, send, recv, local):
        for cp in self._copies(ins, outs, send, recv):
            cp.start()

    def finish(self, ins, outs, send, recv, local):
        for cp in self._copies(ins, outs, send, recv):
            cp.wait()


CHIP_X, CHIP_Y, CHIP_DIAGONAL = (1, 0), (0, 1), (1, 1)


class _ChipExchange(_Plan):
    def __init__(self, arrs, hops):
        self.ins, self.hops = list(arrs), tuple(hops)
        self.out_shapes = [SDS((len(hops),) + a.shape[1:], a.dtype) for a in arrs]
        self.n_rdma, self.n_local = len(hops) * len(arrs), 0
        self.rdma_base = self.local_base = 0

    def _copies(self, ins, outs, send, recv):
        x, y, c = _me()
        cps = []
        for a in range(len(ins)):
            for s, (fx, fy) in enumerate(self.hops):
                px, py = (1 - x) if fx else x, (1 - y) if fy else y
                sem = self.rdma_base + a * len(self.hops) + s
                cps.append(pltpu.make_async_remote_copy(
                    src_ref=ins[a].at[2 * px + py], dst_ref=outs[a].at[s], send_sem=send.at[sem],
                    recv_sem=recv.at[sem], device_id=(px, py, c), device_id_type=pl.DeviceIdType.MESH))
        return cps

    def start(self, ins, outs, send, recv, local):
        for cp in self._copies(ins, outs, send, recv):
            cp.start()

    def finish(self, ins, outs, send, recv, local):
        for cp in self._copies(ins, outs, send, recv):
            cp.wait()


class _Plans:
    def __init__(self, plans):
        self.plans = list(plans)
        self.ins = [a for p in plans for a in p.ins]
        self.out_shapes = [s for p in plans for s in p.out_shapes]
        self.n_rdma = self.n_local = 0
        for p in plans:
            p.rdma_base, p.local_base = self.n_rdma, self.n_local
            self.n_rdma, self.n_local = self.n_rdma + p.n_rdma, self.n_local + p.n_local

    def _each(self, ins, outs):
        i = o = 0
        for p in self.plans:
            yield p, ins[i:i + len(p.ins)], outs[o:o + len(p.out_shapes)]
            i, o = i + len(p.ins), o + len(p.out_shapes)

    def start(self, ins, outs, send, recv, local):
        for p, pi, po in self._each(ins, outs):
            p.start(pi, po, send, recv, local)

    def middle(self, ins, outs, send, recv, local):
        for p, pi, po in self._each(ins, outs):
            p.middle(pi, po, send, recv, local)

    def finish(self, ins, outs, send, recv, local):
        for p, pi, po in self._each(ins, outs):
            p.finish(pi, po, send, recv, local)


def _plan_scratch(plan):
    return [pltpu.SemaphoreType.DMA((plan.n_rdma,)), pltpu.SemaphoreType.DMA((plan.n_rdma,)),
            pltpu.SemaphoreType.DMA((max(plan.n_local, 1),))]


def _run_plan(name, plan):
    n = len(plan.ins)

    def body(*refs):
        ins, outs, sems = refs[:n], refs[n:len(refs) - 3], refs[len(refs) - 3:]
        plan.start(ins, outs, *sems)
        plan.middle(ins, outs, *sems)
        plan.finish(ins, outs, *sems)

    any_spec = pl.BlockSpec(memory_space=pl.ANY)
    return pl.pallas_call(body, name=name, out_shape=list(plan.out_shapes), in_specs=[any_spec] * n,
                          out_specs=[any_spec] * len(plan.out_shapes), scratch_shapes=_plan_scratch(plan))(*plan.ins)


def _pcall(name, body, grid, in_specs, ins, out_specs, out_shape, scratch=(), semantics=None, plan=None):
    if plan is None:
        return pl.pallas_call(body, name=name, grid=grid, in_specs=list(in_specs), out_specs=list(out_specs),
                              out_shape=list(out_shape), scratch_shapes=list(scratch),
                              compiler_params=_cparams(semantics))(*ins)
    n_in, n_out, n_scr = len(ins), len(out_shape), len(scratch)
    p_in, p_out = len(plan.ins), len(plan.out_shapes)

    def with_plan(*refs):
        k_in, c_in = refs[:n_in], refs[n_in:n_in + p_in]
        refs = refs[n_in + p_in:]
        k_out, c_out = refs[:n_out], refs[n_out:n_out + p_out]
        refs = refs[n_out + p_out:]
        k_scr, sems = refs[:n_scr], refs[n_scr:]
        step = 0
        for d, g in enumerate(grid):
            step = step * g + pl.program_id(d)
        n_steps = math.prod(grid)

        @pl.when(step == 0)
        def _():
            plan.start(c_in, c_out, *sems)

        @pl.when(step == min(n_steps - 1, int(n_steps * PLAN_MIDDLE)))
        def _():
            plan.middle(c_in, c_out, *sems)

        body(*k_in, *k_out, *k_scr)

        @pl.when(step == n_steps - 1)
        def _():
            plan.finish(c_in, c_out, *sems)

    any_spec = pl.BlockSpec(memory_space=pl.ANY)
    res = pl.pallas_call(
        with_plan, name=name, grid=grid, in_specs=list(in_specs) + [any_spec] * p_in,
        out_specs=list(out_specs) + [any_spec] * p_out, out_shape=list(out_shape) + list(plan.out_shapes),
        scratch_shapes=list(scratch) + _plan_scratch(plan),
        compiler_params=_cparams(("arbitrary",) * len(grid)))(*ins, *plan.ins)
    return res[:n_out], res[n_out:]


def _rowwise(name, fn, rows, vecs, row_outs, acc_outs=(), tm=128, plan=None):
    t = rows[0].shape[0]
    tm = min(tm, t)
    assert t % tm == 0 and tm % SUBLANES == 0
    n_r, n_v, n_o = len(rows), len(vecs), len(row_outs)

    def body(*refs):
        r_in, v_in = refs[:n_r], refs[n_r:n_r + n_v]
        r_out, a_out = refs[n_r + n_v:n_r + n_v + n_o], refs[n_r + n_v + n_o:]
        outs, accs = fn([r[...] for r in r_in], [v[...] for v in v_in])
        for o_ref, o in zip(r_out, outs, strict=True):
            o_ref[...] = o.astype(o_ref.dtype)
        if a_out:
            @pl.when(pl.program_id(0) == 0)
            def _():
                for a_ref in a_out:
                    a_ref[...] = jnp.zeros_like(a_ref)
            for a_ref, a in zip(a_out, accs, strict=True):
                a_ref[...] += a.reshape(tm // SUBLANES, SUBLANES, a.shape[-1]).sum(axis=0)

    in_specs = [pl.BlockSpec((tm, r.shape[1]), lambda i: (i, 0)) for r in rows]
    in_specs += [pl.BlockSpec(v.shape, lambda i: (0, 0)) for v in vecs]
    out_specs = [pl.BlockSpec((tm, w), lambda i: (i, 0)) for w, _ in row_outs]
    out_specs += [pl.BlockSpec((SUBLANES, w), lambda i: (0, 0)) for w in acc_outs]
    out_shape = [SDS((t, w), dt) for w, dt in row_outs] + [SDS((SUBLANES, w), F32) for w in acc_outs]
    return _pcall(name, body, (t // tm,), in_specs, [*rows, *vecs], out_specs, out_shape, semantics=("arbitrary",),
                  plan=plan)


def _tile(n, want):
    if n <= want:
        return n
    for t in range(want // LANES * LANES, 0, -LANES):
        if n % t == 0:
            return t
    raise ValueError(f"no tile for {n}")


_DOT_DIMS = {"nn": (((1,), (0,)), ((), ())), "nt": (((1,), (1,)), ((), ())), "tn": (((0,), (0,)), ((), ()))}


def _matmul(name, a, b, mode, out_dtypes, epilogue=None, extras=(), vecs=(), a_pro=None,
            tm=1024, tn=512, tk=4096, exact=False, b_blocked=False, out_blocked=0, b_rows=None, plan=None):
    cs = b.shape[-1] if b_blocked else None
    b2 = (b.shape[1], b.shape[0] * b.shape[2]) if b_blocked else b.shape
    if mode == "tn":
        (k, m), (k2, n) = a.shape, b2
    elif mode == "nt":
        (m, k), (n, k2) = a.shape, b2
    else:
        (m, k), (k2, n) = a.shape, b2
    assert k == k2 and not (b_blocked and mode == "tn")
    row0 = 0
    if b_rows is not None:
        assert mode == "nt" and not b_blocked
        row0, n = b_rows
        tn = _tile(math.gcd(n, row0) if row0 else n, tn)
    tm, tn, tk = _tile(m, tm), _tile(n, tn), _tile(k, tk)
    if b_blocked and mode == "nn":
        tn = _tile(cs, tn)
    if b_blocked and mode == "nt":
        tk = _tile(cs, tk)
    if out_blocked:
        tn = _tile(n // out_blocked, tn)
    nk = k // tk

    def vmem_bytes(width):
        operands = 2 * (tm * tk * a.dtype.itemsize + tk * width * b.dtype.itemsize)
        tiles = 2 * tm * width * (sum(jnp.dtype(dt).itemsize for dt in out_dtypes) + sum(e.dtype.itemsize for e in extras))
        return operands + tiles + tm * width * 4 * (2 if nk > 1 else 1)

    extent = cs if (b_blocked and mode == "nn") else n // out_blocked if out_blocked else n
    if extent % (2 * tn) == 0 and row0 % (2 * tn) == 0 and vmem_bytes(2 * tn) <= MATMUL_VMEM_BUDGET:
        tn *= 2
    n_e, n_v, n_o = len(extras), len(vecs), len(out_dtypes)
    precision = HIGHEST if exact else None

    def body(*refs):
        a_ref, b_ref = refs[:2]
        e_refs, v_refs = refs[2:2 + n_e], refs[2 + n_e:2 + n_e + n_v]
        o_refs = refs[2 + n_e + n_v:2 + n_e + n_v + n_o]

        def product():
            av = a_ref[...]
            if a_pro is not None:
                av = a_pro(av)
            return lax.dot_general(av, b_ref[...], _DOT_DIMS[mode], precision=precision, preferred_element_type=F32)

        def finish(acc):
            res = (acc,) if epilogue is None else epilogue(acc, [e[...] for e in e_refs], [v[...] for v in v_refs])
            for o_ref, r in zip(o_refs, res, strict=True):
                o_ref[...] = r.astype(o_ref.dtype)

        if nk == 1:
            finish(product())
            return
        acc_ref = refs[-1]
        kk = pl.program_id(2)

        @pl.when(kk == 0)
        def _():
            acc_ref[...] = product()

        @pl.when(kk > 0)
        def _():
            acc_ref[...] += product()

        @pl.when(kk == nk - 1)
        def _():
            finish(acc_ref[...])

    if mode == "tn":
        a_spec = pl.BlockSpec((tk, tm), lambda i, j, kk: (kk, i))
    else:
        a_spec = pl.BlockSpec((tm, tk), lambda i, j, kk: (i, kk))
    if b_blocked and mode == "nn":
        per = cs // tn
        b_spec = pl.BlockSpec((None, tk, tn), lambda i, j, kk: (j // per, kk, j % per))
    elif b_blocked:
        per = cs // tk
        b_spec = pl.BlockSpec((None, tn, tk), lambda i, j, kk: (kk // per, j, kk % per))
    elif mode == "nt":
        assert row0 % tn == 0
        b_spec = pl.BlockSpec((tn, tk), lambda i, j, kk: (j + row0 // tn, kk))
    else:
        b_spec = pl.BlockSpec((tk, tn), lambda i, j, kk: (kk, j))
    tile = pl.BlockSpec((tm, tn), lambda i, j, kk: (i, j))
    if out_blocked:
        per_o = n // out_blocked // tn
        out_spec = pl.BlockSpec((None, tm, tn), lambda i, j, kk: (j // per_o, i, j % per_o))
        out_shape = [SDS((out_blocked, m, n // out_blocked), dt) for dt in out_dtypes]
    else:
        out_spec, out_shape = tile, [SDS((m, n), dt) for dt in out_dtypes]
    in_specs = [a_spec, b_spec] + [tile] * n_e + [pl.BlockSpec((1, tn), lambda i, j, kk: (0, j))] * n_v
    res = _pcall(name, body, (m // tm, n // tn, nk), in_specs, [a, b, *extras, *vecs], [out_spec] * n_o, out_shape,
                 scratch=[pltpu.VMEM((tm, tn), F32)] if nk > 1 else [],
                 semantics=("parallel", "parallel", "arbitrary"), plan=plan)
    if plan is None:
        return res[0] if n_o == 1 else res
    return (res[0][0] if n_o == 1 else res[0]), res[1]


def _rms_fwd(x):
    r = lax.rsqrt(jnp.mean(x * x, axis=-1, keepdims=True) + EPS)
    return x * r, r


def _rms_bwd(dxn, xn, r):
    return r * (dxn - xn * jnp.mean(dxn * xn, axis=-1, keepdims=True))


_INV_SQRT2 = 1.0 / math.sqrt(2.0)
_INV_SQRT2PI = 1.0 / math.sqrt(2.0 * math.pi)


def _gelu(y):
    return 0.5 * y * (1.0 + lax.erf(y * _INV_SQRT2))


def _gelu_grad(y):
    return 0.5 * (1.0 + lax.erf(y * _INV_SQRT2)) + y * (_INV_SQRT2PI * jnp.exp(-0.5 * y * y))


def _sigmoid(z):
    return 1.0 / (1.0 + jnp.exp(-z))


def _adam_math(w, g, m, v):
    m = ADAM_B1 * m + (1.0 - ADAM_B1) * g
    v = ADAM_B2 * v + (1.0 - ADAM_B2) * (g * g)
    m_hat = m / (1.0 - ADAM_B1 ** ADAM_STEP)
    v_hat = v / (1.0 - ADAM_B2 ** ADAM_STEP)
    delta = -ADAM_LR * (m_hat / (jnp.sqrt(v_hat) + ADAM_EPS) + ADAM_WD * w)
    return delta, m, v


def _norm_mod_fwd(name, x, g, scale, shift):
    def fn(rows, vecs):
        (xv,), (gv, sc, sh) = rows, vecs
        xn, _ = _rms_fwd(xv)
        return [(xn * gv) * (1.0 + sc) + sh], []
    return _rowwise(name, fn, [x], [g, scale, shift], [(x.shape[1], BF16)])[0]


def _norm_mod_bwd(name, x, dh, dres, g, scale, plan=None):
    d = x.shape[1]

    def fn(rows, vecs):
        (xv, dhv, drv), (gv, sc) = rows, vecs
        xn, r = _rms_fwd(xv)
        t = xn * gv
        dt = dhv * (1.0 + sc)
        dx = drv + _rms_bwd(dt * gv, xn, r)
        return [dx], [dhv * t, dhv, dt * xn]
    res = _rowwise(name, fn, [x, dh, dres], [g, scale], [(d, F32)], [d, d, d], plan=plan)
    (dx, dscale, dshift, dg), rest = res if plan is not None else (res, None)
    return (dx, dscale.sum(0), dshift.sum(0), dg.sum(0)), rest


def _gate_bwd(name, dx, val, gate):
    d = dx.shape[1]

    def fn(rows, vecs):
        (dxv, vv), (gv,) = rows, vecs
        return [dxv * gv], [dxv * vv.astype(F32)]
    dval, dgate = _rowwise(name, fn, [dx, val], [gate], [(d, BF16)], [d])
    return dval, dgate.sum(0)


def _final_loss(name, x, tgt, g):
    d = x.shape[1]

    def fn(rows, vecs):
        (xv, tv), (gv,) = rows, vecs
        xn, r = _rms_fwd(xv)
        e = xn * gv - tv
        dy = e * (1.0 / d)
        dx = _rms_bwd(dy * gv, xn, r)
        return [dx], [e * e, dy * xn]
    dx, sq, dg = _rowwise(name, fn, [x, tgt], [g], [(d, F32)], [d, d])
    return dx, 0.5 * jnp.sum(sq) / d, dg.sum(0)


def _group_norm_fwd(name, attn, ssm, g_a, g_s):
    def fn(rows, vecs):
        (av, sv), (ga, gs) = rows, vecs
        return [jnp.concatenate([_rms_fwd(av)[0] * ga, _rms_fwd(sv)[0] * gs], axis=1)], []
    return _rowwise(name, fn, [attn, ssm], [g_a, g_s], [(attn.shape[1] + ssm.shape[1], BF16)])[0]


def _group_norm_bwd(name, attn, ssm, dmixed, g_a, g_s):
    da_w, ds_w = attn.shape[1], ssm.shape[1]

    def fn(rows, vecs):
        (av, sv, dm), (ga, gs) = rows, vecs
        an, ra = _rms_fwd(av)
        sn, rs = _rms_fwd(sv)
        dma, dms = dm[:, :da_w], dm[:, da_w:]
        return [_rms_bwd(dma * ga, an, ra), _rms_bwd(dms * gs, sn, rs)], [dma * an, dms * sn]
    dattn, dssm, dga, dgs = _rowwise(name, fn, [attn, ssm, dmixed], [g_a, g_s],
                                     [(da_w, F32), (ds_w, F32)], [da_w, ds_w])
    return dattn, dssm, dga.sum(0), dgs.sum(0)


def _gelu_fwd(name, y):
    def fn(rows, vecs):
        return [_gelu(rows[0])], []
    return _rowwise(name, fn, [y], [], [(y.shape[1], BF16)])[0]


def _glu_bwd(name, dout, y, z):
    d = y.shape[1]

    def fn(rows, vecs):
        dov, yv, zv = rows
        sg = _sigmoid(zv)
        dz = dov * _gelu(yv) * sg * (1.0 - sg)
        return [dz, dov * sg], [dz]
    dz, dyg, db = _rowwise(name, fn, [dout, y, z], [], [(d, BF16), (d, F32)], [d])
    return dz, dyg, db.sum(0)


def _adam_shard(name, parts, w, m, v):
    r, c = w.shape
    n_parts = sum(1 if p.ndim == 2 else p.shape[0] for p in parts)
    row_bytes = 2 * c * (n_parts * parts[0].dtype.itemsize + 7 * 4)
    tr = min(128, r)
    while tr > SUBLANES and tr * row_bytes > VMEM_LIMIT_BYTES // 2:
        tr //= 2
    assert r % tr == 0
    n_p = len(parts)

    def body(*refs):
        p_refs, (w_ref, m_ref, v_ref, g_out, d_out, m_out, v_out) = refs[:n_p], refs[n_p:]
        g = None
        for p_ref in p_refs:
            terms = [p_ref[...]] if len(p_ref.shape) == 2 else [p_ref[j] for j in range(p_ref.shape[0])]
            for t in terms:
                g = t.astype(F32) if g is None else g + t.astype(F32)
        delta, m_new, v_new = _adam_math(w_ref[...], g, m_ref[...], v_ref[...])
        g_out[...], d_out[...], m_out[...], v_out[...] = g, delta, m_new, v_new

    tile = pl.BlockSpec((tr, c), lambda i: (i, 0))
    p_specs = [tile if p.ndim == 2 else pl.BlockSpec((p.shape[0], tr, c), lambda i: (0, i, 0)) for p in parts]
    return _pcall(name, body, (r // tr,), p_specs + [tile] * 3, [*parts, w, m, v], [tile] * 4, [SDS((r, c), F32)] * 4,
                  semantics=("parallel",))


def _pair_add(name, blocks, from_sibling):
    _, r, c = blocks.shape
    tr = min(256, r)
    assert r % tr == 0

    def body(b0_ref, b1_ref, s_ref, o_ref):
        mine = jnp.where(lax.axis_index("c") == 0, b0_ref[...].astype(F32), b1_ref[...].astype(F32))
        o_ref[...] = (mine + s_ref[...].astype(F32)).astype(o_ref.dtype)

    core_block = lambda k: pl.BlockSpec((None, tr, c), lambda s, i: (2 * s + k, i, 0))
    slot = pl.BlockSpec((None, tr, c), lambda s, i: (s, i, 0))
    return _pcall(name, body, (4, r // tr), [core_block(0), core_block(1), slot], [blocks, blocks, from_sibling],
                  [slot], [SDS((4, r, c), blocks.dtype)], semantics=("parallel", "parallel"))[0]


def _ada_update(name, c_act_t, dmod, w, m, v, tr=128, plan=None):
    r, c = w.shape
    tr = min(tr, r)
    assert r % tr == 0

    def body(c_ref, d_ref, w_ref, m_ref, v_ref, g_out, d_out, m_out, v_out):
        g = jnp.dot(c_ref[...], d_ref[...], precision=lax.Precision.HIGHEST, preferred_element_type=F32)
        delta, m_new, v_new = _adam_math(w_ref[...], g, m_ref[...], v_ref[...])
        g_out[...], d_out[...], m_out[...], v_out[...] = g, delta, m_new, v_new

    tile = pl.BlockSpec((tr, c), lambda i: (i, 0))
    in_specs = [pl.BlockSpec((tr, N_DEV), lambda i: (i, 0)), pl.BlockSpec((N_DEV, c), lambda i: (0, 0)), tile, tile, tile]
    return _pcall(name, body, (r // tr,), in_specs, [c_act_t, dmod, w, m, v], [tile] * 4, [SDS((r, c), F32)] * 4,
                  semantics=("parallel",), plan=plan)


def _rotate_half(x):
    w = x.shape[1]
    half = HEAD_DIM // 2
    lane = lax.broadcasted_iota(jnp.int32, x.shape, 1)
    return jnp.where((lane % HEAD_DIM) < half, -pltpu.roll(x, w - half, 1), pltpu.roll(x, half, 1))


def _lane_tile(tab, w):
    return tab[:, :w] if w <= LANES else jnp.tile(tab, (1, w // LANES))


def _rope(x, cos, sin):
    return x * cos + _rotate_half(x) * sin


def _rope_t(dy, cos, sin):
    return dy * cos - _rotate_half(dy) * sin


def _band_mask(n):
    shape = (Q_PER_KV * WINDOW, 2 * WINDOW)
    i = lax.broadcasted_iota(jnp.int32, shape, 0) & (WINDOW - 1)
    j = lax.broadcasted_iota(jnp.int32, shape, 1)
    return (j > i) & (j <= i + WINDOW) & ((n > 0) | (j >= WINDOW))


def _stack_heads(x, hk):
    first = hk * Q_PER_KV
    return jnp.concatenate([x[:, (first + g) * HEAD_DIM:(first + g + 1) * HEAD_DIM] for g in range(Q_PER_KV)], axis=0)


def _stack_cols(ref, hk):
    first = hk * Q_PER_KV
    return jnp.concatenate([ref[:, first + g:first + g + 1] for g in range(Q_PER_KV)], axis=0)


def _stack_sinks(sink_ref, hk):
    first = hk * Q_PER_KV
    return jnp.concatenate([jnp.broadcast_to(sink_ref[0:1, first + g:first + g + 1], (WINDOW, 1))
                            for g in range(Q_PER_KV)], axis=0)


def _attn_specs(da, dkv, nb):
    cur = lambda n: (jnp.minimum(n, nb - 1), 0)
    prev = lambda n: (jnp.maximum(jnp.minimum(n, nb - 1) - 1, 0), 0)
    return dict(
        q=pl.BlockSpec((WINDOW, da), cur), kv_cur=pl.BlockSpec((WINDOW, dkv), cur),
        kv_prev=pl.BlockSpec((WINDOW, dkv), prev), tab_cur=pl.BlockSpec((WINDOW, LANES), cur),
        tab_prev=pl.BlockSpec((WINDOW, LANES), prev))


def _attn_fwd(name, q, k, v, cos, sin, sinks, plan=None):
    s, da = q.shape
    dkv = k.shape[1]
    nq, nb = da // HEAD_DIM, s // WINDOW
    scale = HEAD_DIM ** -0.5

    def body(q_ref, kp_ref, kc_ref, vp_ref, vc_ref, cc_ref, sc_ref, cp_ref, sp_ref, sink_ref, o_ref, lse_ref):
        n = pl.program_id(0)
        cc, sc, cp, sp = cc_ref[...], sc_ref[...], cp_ref[...], sp_ref[...]
        qr = _rope(q_ref[...], _lane_tile(cc, da), _lane_tile(sc, da)).astype(BF16)
        kk = jnp.concatenate([_rope(kp_ref[...], _lane_tile(cp, dkv), _lane_tile(sp, dkv)),
                              _rope(kc_ref[...], _lane_tile(cc, dkv), _lane_tile(sc, dkv))], axis=0).astype(BF16)
        vv = jnp.concatenate([vp_ref[...], vc_ref[...]], axis=0).astype(BF16)
        valid = _band_mask(n)
        for hk in range(nq // Q_PER_KV):
            ks = slice(hk * HEAD_DIM, (hk + 1) * HEAD_DIM)
            sco = lax.dot_general(_stack_heads(qr, hk), kk[:, ks], _DOT_DIMS["nt"], preferred_element_type=F32) * scale
            sco = jnp.where(valid, sco, -1e30)
            sink = _stack_sinks(sink_ref, hk)
            mx = jnp.maximum(jnp.max(sco, axis=1, keepdims=True), sink)
            p = jnp.exp(sco - mx)
            den = jnp.sum(p, axis=1, keepdims=True) + jnp.exp(sink - mx)
            o8 = jnp.dot((p / den).astype(BF16), vv[:, ks], preferred_element_type=F32)
            lse8 = mx + jnp.log(den)
            for g in range(Q_PER_KV):
                hq, rows = hk * Q_PER_KV + g, slice(g * WINDOW, (g + 1) * WINDOW)
                o_ref[:, hq * HEAD_DIM:(hq + 1) * HEAD_DIM] = o8[rows]
                lse_ref[:, hq:hq + 1] = lse8[rows]

    sp_ = _attn_specs(da, dkv, nb)
    in_specs = [sp_["q"], sp_["kv_prev"], sp_["kv_cur"], sp_["kv_prev"], sp_["kv_cur"],
                sp_["tab_cur"], sp_["tab_cur"], sp_["tab_prev"], sp_["tab_prev"], pl.BlockSpec((1, nq), lambda n: (0, 0))]
    return _pcall(name, body, (nb,), in_specs, [q, k, k, v, v, cos, sin, cos, sin, sinks],
                  [sp_["q"], pl.BlockSpec((WINDOW, nq), lambda n: (n, 0))], [SDS((s, da), F32), SDS((s, nq), F32)],
                  semantics=("arbitrary",), plan=plan)


def _attn_bwd(name, q, k, v, cos, sin, sinks, out, lse, dout, plan=None):
    s, da = q.shape
    dkv = k.shape[1]
    nq, nb = da // HEAD_DIM, s // WINDOW
    scale = HEAD_DIM ** -0.5

    def body(q_ref, kp_ref, kc_ref, vp_ref, vc_ref, cc_ref, sc_ref, cp_ref, sp_ref, sink_ref, o_ref, lse_ref,
             do_ref, dq_ref, dk_ref, dv_ref, dsink_ref, dk_carry, dv_carry):
        n = pl.program_id(0)
        cp, sp = _lane_tile(cp_ref[...], dkv), _lane_tile(sp_ref[...], dkv)

        @pl.when(n == 0)
        def _():
            dk_carry[...] = jnp.zeros_like(dk_carry)
            dv_carry[...] = jnp.zeros_like(dv_carry)
            dsink_ref[...] = jnp.zeros_like(dsink_ref)

        @pl.when(n < nb)
        def _():
            cc, sc = cc_ref[...], sc_ref[...]
            qr = _rope(q_ref[...], _lane_tile(cc, da), _lane_tile(sc, da)).astype(BF16)
            kk = jnp.concatenate([_rope(kp_ref[...], cp, sp),
                                  _rope(kc_ref[...], _lane_tile(cc, dkv), _lane_tile(sc, dkv))], axis=0).astype(BF16)
            vv = jnp.concatenate([vp_ref[...], vc_ref[...]], axis=0).astype(BF16)
            valid = _band_mask(n)
            do_all, o_all = do_ref[...], o_ref[...]
            for hk in range(nq // Q_PER_KV):
                ks = slice(hk * HEAD_DIM, (hk + 1) * HEAD_DIM)
                q8, lse8 = _stack_heads(qr, hk), _stack_cols(lse_ref, hk)
                sco = lax.dot_general(q8, kk[:, ks], _DOT_DIMS["nt"], preferred_element_type=F32) * scale
                probs = jnp.where(valid, jnp.exp(sco - lse8), 0.0)
                do8 = _stack_heads(do_all, hk)
                delta = jnp.sum(do8 * _stack_heads(o_all, hk), axis=1, keepdims=True)
                do8 = do8.astype(BF16)
                dp = lax.dot_general(do8, vv[:, ks], _DOT_DIMS["nt"], preferred_element_type=F32)
                ds = (probs * (dp - delta) * scale).astype(BF16)
                dq8 = jnp.dot(ds, kk[:, ks], preferred_element_type=F32)
                dk_h = lax.dot_general(ds, q8, _DOT_DIMS["tn"], preferred_element_type=F32)
                dv_h = lax.dot_general(probs.astype(BF16), do8, _DOT_DIMS["tn"], preferred_element_type=F32)
                dsink8 = -jnp.exp(_stack_sinks(sink_ref, hk) - lse8) * delta
                for g in range(Q_PER_KV):
                    hq, rows = hk * Q_PER_KV + g, slice(g * WINDOW, (g + 1) * WINDOW)
                    dq_ref[:, hq * HEAD_DIM:(hq + 1) * HEAD_DIM] = dq8[rows]
                    dsink_ref[:, hq:hq + 1] += dsink8[rows].reshape(WINDOW // SUBLANES, SUBLANES, 1).sum(axis=0)
                dk_ref[:, ks] = dk_carry[:, ks] + dk_h[:WINDOW]
                dv_ref[:, ks] = dv_carry[:, ks] + dv_h[:WINDOW]
                dk_carry[:, ks] = dk_h[WINDOW:]
                dv_carry[:, ks] = dv_h[WINDOW:]
            dq_ref[...] = _rope_t(dq_ref[...], _lane_tile(cc, da), _lane_tile(sc, da))
            dk_ref[...] = _rope_t(dk_ref[...], cp, sp)

        @pl.when(n == nb)
        def _():
            dk_ref[...] = _rope_t(dk_carry[...], cp, sp)
            dv_ref[...] = dv_carry[...]

    sp_ = _attn_specs(da, dkv, nb)
    last_prev = lambda n: (jnp.maximum(n - 1, 0), 0)
    tab_prev = pl.BlockSpec((WINDOW, LANES), last_prev)
    kv_out = pl.BlockSpec((WINDOW, dkv), last_prev)
    lse_spec = pl.BlockSpec((WINDOW, nq), lambda n: (jnp.minimum(n, nb - 1), 0))
    in_specs = [sp_["q"], sp_["kv_prev"], sp_["kv_cur"], sp_["kv_prev"], sp_["kv_cur"],
                sp_["tab_cur"], sp_["tab_cur"], tab_prev, tab_prev,
                pl.BlockSpec((1, nq), lambda n: (0, 0)), sp_["q"], lse_spec, sp_["q"]]
    res = _pcall(name, body, (nb + 1,), in_specs, [q, k, k, v, v, cos, sin, cos, sin, sinks, out, lse, dout],
                 [sp_["q"], kv_out, kv_out, pl.BlockSpec((SUBLANES, nq), lambda n: (0, 0))],
                 [SDS((s, da), F32), SDS((s, dkv), F32), SDS((s, dkv), F32), SDS((SUBLANES, nq), F32)],
                 scratch=[pltpu.VMEM((WINDOW, dkv), F32), pltpu.VMEM((WINDOW, dkv), F32)],
                 semantics=("arbitrary",), plan=plan)
    (dq, dk, dv, dsink), rest = res if plan is not None else (res, None)
    return (dq, dk, dv, dsink.sum(0)), rest


def _ssm_operators(lam_re, lam_im, log_step, b_re, b_im, c_re, c_im, d_skip):
    g, p = lam_re.shape
    h = b_re.shape[-1]
    l = SSM_CHUNK
    step = jnp.exp(log_step)[:, None]
    mag = jnp.exp(lam_re * step)
    ar, ai = mag * jnp.cos(lam_im * step), mag * jnp.sin(lam_im * step)
    den = lam_re * lam_re + lam_im * lam_im
    cr = ((ar - 1.0) * lam_re + ai * lam_im) / den
    ci = (ai * lam_re - (ar - 1.0) * lam_im) / den
    bbr = cr[..., None] * b_re - ci[..., None] * b_im
    bbi = cr[..., None] * b_im + ci[..., None] * b_re
    pr, pi = [jnp.ones_like(ar)], [jnp.zeros_like(ar)]
    for _ in range(l):
        pr, pi = pr + [pr[-1] * ar - pi[-1] * ai], pi + [pr[-1] * ai + pi[-1] * ar]
    pwr, pwi = jnp.stack(pr, axis=1), jnp.stack(pi, axis=1)
    cpr = c_re[:, None] * pwr[:, :, None, :] - c_im[:, None] * pwi[:, :, None, :]
    cpi = c_re[:, None] * pwi[:, :, None, :] + c_im[:, None] * pwr[:, :, None, :]
    kern = (jnp.einsum("gtop,gpi->gtoi", cpr[:, :l], bbr, precision=lax.Precision.HIGHEST)
            - jnp.einsum("gtop,gpi->gtoi", cpi[:, :l], bbi, precision=lax.Precision.HIGHEST))
    kern = kern.at[:, 0].add(d_skip.reshape(g, h)[:, :, None] * jnp.eye(h, dtype=F32))
    tm = jnp.stack([jnp.pad(kern[:, :l - j], ((0, 0), (j, 0), (0, 0), (0, 0))) for j in range(l)], axis=1)
    tm = tm.transpose(0, 1, 4, 2, 3).reshape(g, l * h, l * h)
    rev_r, rev_i = pwr[:, l - 1::-1][:, :l], pwi[:, l - 1::-1][:, :l]
    er = rev_r[:, :, None, :] * bbr.transpose(0, 2, 1)[:, None] - rev_i[:, :, None, :] * bbi.transpose(0, 2, 1)[:, None]
    ei = rev_r[:, :, None, :] * bbi.transpose(0, 2, 1)[:, None] + rev_i[:, :, None, :] * bbr.transpose(0, 2, 1)[:, None]
    em = jnp.concatenate([er, ei], axis=-1).reshape(g, l * h, 2 * p)
    fr = cpr[:, 1:].transpose(0, 3, 1, 2).reshape(g, p, l * h)
    fi = -cpi[:, 1:].transpose(0, 3, 1, 2).reshape(g, p, l * h)
    fm = jnp.concatenate([fr, fi], axis=1)
    return tm, em, fm, pwr[:, l], pwi[:, l]


def _decay_lanes(alr, ali):
    return jnp.concatenate([alr, alr], axis=1), jnp.concatenate([-ali, ali], axis=1)


def _ssm_fwd(name, u, tm, em, fm, acat, bcat, plan=None):
    s, ds = u.shape
    g, lh, p2 = em.shape
    gb, h = SSM_GROUPS_PER_STEP, lh // SSM_CHUNK
    assert gb * h == LANES and g * h == ds and s % SSM_CHUNK == 0
    nc, half = s // SSM_CHUNK, p2 // 2

    def body(u_ref, tm_ref, em_ref, fm_ref, a_ref, b_ref, y_ref, xp_ref, uc_ref, yc_ref, st_ref):
        _to_chunks(u_ref, uc_ref, nc, h)
        for i in range(gb):
            st_ref[pl.ds(i, nc, stride=gb), :] = jnp.dot(uc_ref[i], em_ref[i], precision=SSM_PRECISION,
                                                         preferred_element_type=F32)
        av, bv = a_ref[...], b_ref[...]

        def step(c, carry):
            x, xs = carry
            rows = pl.ds(pl.multiple_of(c * gb, gb), gb)
            loc = st_ref[rows, :]
            st_ref[rows, :] = x
            return av * x + bv * xs + loc, av * xs - bv * x + pltpu.roll(loc, half, 1)
        zero = jnp.zeros((gb, p2), F32)
        lax.fori_loop(0, nc, step, (zero, zero), unroll=4)
        for i in range(gb):
            xp = st_ref[pl.ds(i, nc, stride=gb), :]
            xp_ref[i] = xp
            yc_ref[i] = (jnp.dot(uc_ref[i], tm_ref[i], precision=SSM_PRECISION, preferred_element_type=F32)
                         + jnp.dot(xp, fm_ref[i], precision=SSM_PRECISION, preferred_element_type=F32))
        _from_chunks(yc_ref, y_ref, nc, h)

    blk = lambda r, c: pl.BlockSpec((gb, r, c), lambda i: (i, 0, 0))
    vec = pl.BlockSpec((gb, p2), lambda i: (i, 0))
    col = pl.BlockSpec((s, LANES), lambda i: (0, i))
    return _pcall(name, body, (g // gb,), [col, blk(lh, lh), blk(lh, p2), blk(p2, lh), vec, vec],
                  [u, tm, em, fm, acat, bcat], [col, blk(nc, p2), blk(nc, lh)],
                  [SDS((s, ds), F32), SDS((g, nc, p2), F32), SDS((g, nc, lh), F32)],
                  scratch=[pltpu.VMEM((gb, nc, lh), F32), pltpu.VMEM((nc * gb, p2), F32)],
                  semantics=("parallel",), plan=plan)


def _ssm_bwd(name, u_chunks, dy, xprev, tm, em, fm, acat, bcat, plan=None):
    s, ds = dy.shape
    g, lh, p2 = em.shape
    gb, h = SSM_GROUPS_PER_STEP, lh // SSM_CHUNK
    nc, half = s // SSM_CHUNK, p2 // 2

    def body(uc_ref, dy_ref, xp_ref, tm_ref, em_ref, fm_ref, a_ref, b_ref,
             du_ref, dtm_ref, dem_ref, dfm_ref, r1_ref, r2_ref, dyc_ref, duc_ref, gs_ref, xs_ref):
        _to_chunks(dy_ref, dyc_ref, nc, h)
        for i in range(gb):
            gs_ref[pl.ds(i, nc, stride=gb), :] = lax.dot_general(
                dyc_ref[i], fm_ref[i], _DOT_DIMS["nt"], precision=SSM_PRECISION, preferred_element_type=F32)
            xs_ref[pl.ds(i, nc, stride=gb), :] = xp_ref[i]
        av, bv = a_ref[...], b_ref[...]

        def step(t, carry):
            grad, gsw, r1, r2 = carry
            c = nc - 1 - t
            rows = pl.ds(pl.multiple_of(c * gb, gb), gb)
            dxp, xp = gs_ref[rows, :], xs_ref[rows, :]
            gs_ref[rows, :] = grad
            r1 = r1 + grad * xp
            r2 = r2 + grad * pltpu.roll(xp, half, 1)
            return dxp + av * grad - bv * gsw, pltpu.roll(dxp, half, 1) + av * gsw + bv * grad, r1, r2
        zero = jnp.zeros((gb, p2), F32)
        _, _, r1, r2 = lax.fori_loop(0, nc, step, (zero, zero, zero, zero), unroll=4)
        r1_ref[...], r2_ref[...] = r1, r2
        for i in range(gb):
            dxl = gs_ref[pl.ds(i, nc, stride=gb), :]
            duc_ref[i] = (lax.dot_general(dyc_ref[i], tm_ref[i], _DOT_DIMS["nt"], precision=SSM_PRECISION,
                                          preferred_element_type=F32)
                          + lax.dot_general(dxl, em_ref[i], _DOT_DIMS["nt"], precision=SSM_PRECISION,
                                            preferred_element_type=F32))
            dtm_ref[i] = lax.dot_general(uc_ref[i], dyc_ref[i], _DOT_DIMS["tn"], precision=SSM_PRECISION,
                                         preferred_element_type=F32)
            dfm_ref[i] = lax.dot_general(xp_ref[i], dyc_ref[i], _DOT_DIMS["tn"], precision=SSM_PRECISION,
                                         preferred_element_type=F32)
            dem_ref[i] = lax.dot_general(uc_ref[i], dxl, _DOT_DIMS["tn"], precision=SSM_PRECISION,
                                         preferred_element_type=F32)
        _from_chunks(duc_ref, du_ref, nc, h)

    blk = lambda r, c: pl.BlockSpec((gb, r, c), lambda i: (i, 0, 0))
    vec = pl.BlockSpec((gb, p2), lambda i: (i, 0))
    col = pl.BlockSpec((s, LANES), lambda i: (0, i))
    chunked = pltpu.VMEM((gb, nc, lh), F32)
    res = _pcall(name, body, (g // gb,),
                 [blk(nc, lh), col, blk(nc, p2), blk(lh, lh), blk(lh, p2), blk(p2, lh), vec, vec],
                 [u_chunks, dy, xprev, tm, em, fm, acat, bcat],
                 [col, blk(lh, lh), blk(lh, p2), blk(p2, lh), vec, vec],
                 [SDS((s, ds), F32), SDS((g, lh, lh), F32), SDS((g, lh, p2), F32), SDS((g, p2, lh), F32),
                  SDS((g, p2), F32), SDS((g, p2), F32)],
                 scratch=[chunked, chunked, pltpu.VMEM((nc * gb, p2), F32), pltpu.VMEM((nc * gb, p2), F32)],
                 semantics=("parallel",), plan=plan)
    return res if plan is not None else (res, None)


def _to_chunks(src_ref, dst_ref, nc, h):
    per = LANES // h
    grp = lax.broadcasted_iota(jnp.int32, (nc, LANES), 1) // h
    for g in range(per):
        for part in range(SSM_CHUNK * h // LANES):
            acc = None
            for i in range(part * per, (part + 1) * per):
                piece = src_ref[pl.ds(i, nc, stride=SSM_CHUNK), :]
                lo = (i * h) % LANES
                if (lo - g * h) % LANES:
                    piece = pltpu.roll(piece, (lo - g * h) % LANES, 1)
                acc = piece if acc is None else jnp.where(grp == lo // h, piece, acc)
            dst_ref[g, :, part * LANES:(part + 1) * LANES] = acc


def _from_chunks(src_ref, dst_ref, nc, h):
    per = LANES // h
    grp = lax.broadcasted_iota(jnp.int32, (nc, LANES), 1) // h
    for i in range(SSM_CHUNK):
        part, lo = divmod(i * h, LANES)
        row = None
        for g in range(per):
            piece = src_ref[g, :, part * LANES:(part + 1) * LANES]
            if (g * h - lo) % LANES:
                piece = pltpu.roll(piece, (g * h - lo) % LANES, 1)
            row = piece if row is None else jnp.where(grp == g, piece, row)
        dst_ref[pl.ds(i, nc, stride=SSM_CHUNK), :] = row


_SMALL = ("b_ada", "norm1_g", "sinks", "ssm_lam_re", "ssm_lam_im", "ssm_log_step", "ssm_b_re", "ssm_b_im",
          "ssm_c_re", "ssm_c_im", "ssm_d", "b_glu", "attn_out_g", "ssm_out_g", "norm2_g", "final_g")
_WEIGHTS = ("w_ada", "b_ada", "norm1_g", "w_in", "sinks", "ssm_lam_re", "ssm_lam_im", "ssm_log_step", "ssm_b_re",
            "ssm_b_im", "ssm_c_re", "ssm_c_im", "ssm_d", "w_glu", "b_glu", "attn_out_g", "ssm_out_g", "w_out",
            "norm2_g", "w_ff1", "w_ff2", "final_g")
_PACK_ALIGN = 128 * LANES


def _pack(parts):
    flat = jnp.concatenate([p.reshape(-1).astype(F32) for p in parts])
    pad = (-flat.shape[0]) % _PACK_ALIGN
    return jnp.pad(flat, (0, pad)).reshape(-1, LANES)


def kernel(x, c, w_ada, b_ada, norm1_g, w_in, sinks, ssm_lam_re, ssm_lam_im, ssm_log_step, ssm_b_re, ssm_b_im, ssm_c_re, ssm_c_im, ssm_d, w_glu, b_glu, attn_out_g, ssm_out_g, w_out, norm2_g, w_ff1, w_ff2, final_g, loss_target, m_w_ada, m_b_ada, m_norm1_g, m_w_in, m_sinks, m_ssm_lam_re, m_ssm_lam_im, m_ssm_log_step, m_ssm_b_re, m_ssm_b_im, m_ssm_c_re, m_ssm_c_im, m_ssm_d, m_w_glu, m_b_glu, m_attn_out_g, m_ssm_out_g, m_w_out, m_norm2_g, m_w_ff1, m_w_ff2, m_final_g, v_w_ada, v_b_ada, v_norm1_g, v_w_in, v_sinks, v_ssm_lam_re, v_ssm_lam_im, v_ssm_log_step, v_ssm_b_re, v_ssm_b_im, v_ssm_c_re, v_ssm_c_im, v_ssm_d, v_w_glu, v_b_glu, v_attn_out_g, v_ssm_out_g, v_w_out, v_norm2_g, v_w_ff1, v_w_ff2, v_final_g):
    args = dict(locals())
    weights = {n: args[n] for n in _WEIGHTS}
    mom = {n: args["m_" + n] for n in _WEIGHTS}
    var = {n: args["v_" + n] for n in _WEIGHTS}
    me = 4 * lax.axis_index("x") + 2 * lax.axis_index("y") + lax.axis_index("c")

    _, s, d = x.shape
    xs, tgt = x[0], loss_target[0]
    d_ssm = ssm_d.shape[-1]
    d_attn = d - d_ssm
    nq = d_attn // HEAD_DIM
    d_kv = (nq // Q_PER_KV) * HEAD_DIM
    p_state = ssm_b_re.shape[2]

    c_all, g_in = _run_plan("gather_c_w_in", _Gather([c, w_in[0].T.astype(BF16)]))
    c_all = c_all.reshape(N_DEV, d)
    w_in_t = g_in.reshape(-1, d)

    n_loc = w_ada.shape[-1]
    b_loc = lax.dynamic_slice_in_dim(b_ada, me * n_loc, n_loc, axis=1)
    silu = lambda t: t * _sigmoid(t)
    mod_part = _matmul("ada_mod", c_all, w_ada[0], "nn", [F32], a_pro=silu, vecs=[b_loc], exact=True,
                       epilogue=lambda acc, e, v: (acc + v[0],), tn=512, tk=d)
    mod_all = _run_plan("gather_mod", _Gather([mod_part]))[0]
    mod = lax.dynamic_index_in_dim(mod_all, me, axis=1, keepdims=False).reshape(N_MOD, 1, d)
    shift1, scale1, gate1, shift2, scale2, gate2 = [mod[i] for i in range(N_MOD)]

    h1 = _norm_mod_fwd("norm1", xs, norm1_g, scale1, shift1)
    q = _matmul("proj_q", h1, w_in_t, "nt", [F32], b_rows=(0, d_attn))
    kv = _matmul("proj_kv", h1, w_in_t, "nt", [F32], b_rows=(d_attn, 2 * d_kv))
    u = _matmul("proj_u", h1, w_in_t, "nt", [F32], b_rows=(d_attn + 2 * d_kv, d_ssm))
    k, v = kv[:, :d_kv], kv[:, d_kv:]

    half = HEAD_DIM // 2
    inv_freq = ROPE_THETA ** (-jnp.arange(half, dtype=F32) / half)
    ang = jnp.arange(s, dtype=F32)[:, None] * inv_freq[None, :]
    cos_t, sin_t = jnp.tile(jnp.cos(ang), (1, 4)), jnp.tile(jnp.sin(ang), (1, 4))
    (attn, lse), (g_glu, g_out) = _attn_fwd("attn_fwd", q, k, v, cos_t, sin_t, sinks,
                                            plan=_Gather([w_glu[0].astype(BF16), w_out[0].astype(BF16)]))
    w_glu_f = g_glu.reshape(d_ssm, d_ssm)
    w_out_f = g_out.reshape(d, d)

    ssm_params = (ssm_lam_re[0], ssm_lam_im[0], ssm_log_step[0], ssm_b_re[0], ssm_b_im[0], ssm_c_re[0],
                  ssm_c_im[0], ssm_d[0])
    (tm_op, em_op, fm_op, alr, ali), ssm_vjp = jax.vjp(_ssm_operators, *ssm_params)
    acat, bcat = _decay_lanes(alr, ali)
    (y_ssm, x_prev, u_chunks), (g_ff1,) = _ssm_fwd("ssm_fwd", u, tm_op, em_op, fm_op, acat, bcat,
                                                   plan=_Gather([w_ff1[0].astype(BF16)]))
    yg = _gelu_fwd("gelu", y_ssm)
    ssm_out, z_glu = _matmul(
        "glu", yg, w_glu_f, "nn", [F32, F32], extras=[y_ssm], vecs=[b_glu],
        epilogue=lambda acc, e, v: (_gelu(e[0]) * _sigmoid(acc + v[0]), acc + v[0]))
    mixed = _group_norm_fwd("group_norm", attn, ssm_out, attn_out_g, ssm_out_g)
    x2, mo = _matmul("out_proj", mixed, w_out_f, "nn", [F32, BF16], extras=[xs], vecs=[gate1],
                     epilogue=lambda acc, e, v: (e[0] + v[0] * acc, acc))

    h2 = _norm_mod_fwd("norm2", x2, norm2_g, scale2, shift2)
    (a_ff, f_ff), (g_ff2,) = _matmul("ff1", h2, g_ff1, "nn", [BF16, BF16], b_blocked=True,
                                     epilogue=lambda acc, e, v: (acc, jnp.square(jnp.maximum(acc, 0.0))),
                                     plan=_Gather([w_ff2[0].astype(BF16)]))
    w_ff2_f = g_ff2.reshape(-1, d)
    x3, ff = _matmul("ff2", f_ff, w_ff2_f, "nn", [F32, BF16], extras=[x2], vecs=[gate2],
                     epilogue=lambda acc, e, v: (e[0] + v[0] * acc, acc))

    dx3, loss_local, d_final_g = _final_loss("final_loss", x3, tgt, final_g.reshape(1, d))
    loss = lax.psum(loss_local, MESH_AXES)

    dff, d_gate2 = _gate_bwd("gate2_bwd", dx3, ff, gate2)
    dw_ff2 = _matmul("ff2_dw", f_ff, dff, "tn", [BF16]).reshape(N_DEV, -1, d)
    da_ff, (p_ff2,) = _matmul("ff2_dx", dff, w_ff2_f, "nt", [BF16], extras=[a_ff],
                              epilogue=lambda acc, e, v: (acc * (2.0 * jnp.maximum(e[0].astype(F32), 0.0)),),
                              plan=_PairSwap([dw_ff2]))
    s_ff2 = _pair_add("pair_add_ff2", dw_ff2, p_ff2)
    dw_ff1, (r_ff2_a,) = _matmul("ff1_dw", h2, da_ff, "tn", [BF16], out_blocked=N_DEV,
                                 plan=_ChipExchange([s_ff2], (CHIP_X, CHIP_Y)))
    dh2, (r_ff2_b,) = _matmul("ff1_dx", da_ff, g_ff1, "nt", [F32], b_blocked=True,
                              plan=_ChipExchange([s_ff2], (CHIP_DIAGONAL,)))
    (dx2, d_scale2, d_shift2, d_norm2_g), (p_ff1,) = _norm_mod_bwd("norm2_bwd", x2, dh2, dx3, norm2_g, scale2,
                                                                   plan=_PairSwap([dw_ff1]))
    s_ff1 = _pair_add("pair_add_ff1", dw_ff1, p_ff1)

    dmo, d_gate1 = _gate_bwd("gate1_bwd", dx2, mo, gate1)
    dw_out = _matmul("out_dw", mixed, dmo, "tn", [BF16]).reshape(N_DEV, -1, d)
    dmixed, (r_out_a,) = _matmul("out_dx", dmo, w_out_f, "nt", [F32], plan=_Exchange([dw_out], (1, 4, 2)))
    dattn, dssm_out, d_attn_g, d_ssm_g = _group_norm_bwd("group_norm_bwd", attn, ssm_out, dmixed, attn_out_g, ssm_out_g)

    dz, dyg_direct, d_b_glu = _glu_bwd("glu_bwd", dssm_out, y_ssm, z_glu)
    dw_glu = _matmul("glu_dw", yg, dz, "tn", [BF16]).reshape(N_DEV, -1, d_ssm)
    dy_ssm = _matmul("glu_dx", dz, w_glu_f, "nt", [F32], extras=[dyg_direct, y_ssm],
                     epilogue=lambda acc, e, v: ((acc + e[0]) * _gelu_grad(e[1]),))
    (du, d_tm, d_em, d_fm, r1, r2), (r_ff1_a, r_out_b) = _ssm_bwd(
        "ssm_bwd", u_chunks, dy_ssm, x_prev, tm_op, em_op, fm_op, acat, bcat,
        plan=_Plans([_ChipExchange([s_ff1], (CHIP_X, CHIP_Y)), _Exchange([dw_out], (5, 3, 6))]))
    d_alr = r1[:, :p_state] + r1[:, p_state:]
    d_ali = r2[:, p_state:] - r2[:, :p_state]
    d_ssm_params = ssm_vjp((d_tm, d_em, d_fm, d_alr, d_ali))

    (dq, dk, dv, d_sinks), (r_ff1_b,) = _attn_bwd("attn_bwd", q, k, v, cos_t, sin_t, sinks, attn, lse, dattn,
                                                  plan=_ChipExchange([s_ff1], (CHIP_DIAGONAL,)))
    dproj = jnp.concatenate([dq, dk, dv, du], axis=1).astype(BF16)
    dw_in_t, (r_out_c, r_glu) = _matmul(
        "in_dw", dproj, h1, "tn", [BF16],
        plan=_Plans([_Exchange([dw_out], (7,)), _Exchange([dw_glu], RELATIONS_ALL)]))
    dw_in_t = dw_in_t.reshape(N_DEV, -1, d)
    dh1, (r_in_a,) = _matmul("in_dx", dproj, w_in_t, "nn", [F32], plan=_Exchange([dw_in_t], RELATIONS_SAME_CORE))
    (grad_x, d_scale1, d_shift1, d_norm1_g), _ = _norm_mod_bwd("norm1_bwd", xs, dh1, dx2, norm1_g, scale1)

    d_mod = jnp.concatenate([d_shift1, d_scale1, d_gate1, d_shift2, d_scale2, d_gate2])
    small_g = dict(zip(("ssm_lam_re", "ssm_lam_im", "ssm_log_step", "ssm_b_re", "ssm_b_im", "ssm_c_re", "ssm_c_im",
                        "ssm_d"), d_ssm_params, strict=True))
    small_g.update(b_ada=d_mod, norm1_g=d_norm1_g, sinks=d_sinks, b_glu=d_b_glu, attn_out_g=d_attn_g,
                   ssm_out_g=d_ssm_g, norm2_g=d_norm2_g, final_g=d_final_g)
    small_parts, r_in_b = _run_plan("gather_small_grads", _Plans([_Gather([_pack([small_g[n] for n in _SMALL])]),
                                                                  _Exchange([dw_in_t], RELATIONS_OTHER_CORE)]))
    small = _adam_shard("adam_small", [small_parts], _pack([weights[n] for n in _SMALL]),
                        _pack([mom[n] for n in _SMALL]), _pack([var[n] for n in _SMALL]))
    out = {}
    off = 0
    for n in _SMALL:
        size = weights[n].size
        out[n] = [t.reshape(-1)[off:off + size].reshape(weights[n].shape) for t in small]
        off += size

    dmod_all = small_parts.reshape(N_DEV, -1)[:, :N_MOD * d]
    dmod_loc = lax.dynamic_slice_in_dim(dmod_all, me * n_loc, n_loc, axis=1)
    c_act_t = silu(c_all).T
    out["w_ada"] = [t[None] for t in _ada_update("adam_w_ada", c_act_t, dmod_loc, w_ada[0], m_w_ada[0], v_w_ada[0])]

    mine = lambda blocks: lax.dynamic_index_in_dim(blocks, me, axis=0, keepdims=False)
    in_parts = [mine(dw_in_t).T] + [r.transpose(0, 2, 1) for r in (r_in_a, r_in_b)]
    my_chip = 2 * lax.axis_index("x") + lax.axis_index("y")
    chip_sum = lambda sums: lax.dynamic_index_in_dim(sums, my_chip, axis=0, keepdims=False)
    received = dict(w_in=in_parts, w_glu=[mine(dw_glu), r_glu], w_out=[mine(dw_out), r_out_a, r_out_b, r_out_c],
                    w_ff1=[chip_sum(s_ff1), r_ff1_a, r_ff1_b], w_ff2=[chip_sum(s_ff2), r_ff2_a, r_ff2_b])
    for n, parts in received.items():
        out[n] = [t[None] for t in _adam_shard("adam_" + n, parts, weights[n][0], mom[n][0], var[n][0])]

    return (loss, grad_x[None], *[out[n][0] for n in _WEIGHTS], *[out[n][1] for n in _WEIGHTS],
            *[out[n][2] for n in _WEIGHTS], *[out[n][3] for n in _WEIGHTS])
```

```python
import math

import jax
import jax.numpy as jnp
from jax import lax
from jax.experimental import pallas as pl
from jax.experimental.pallas import tpu as pltpu

F32, BF16 = jnp.float32, jnp.bfloat16
SDS = jax.ShapeDtypeStruct
MESH_AXES = ("x", "y", "c")
N_DEV = 8
VMEM_LIMIT_BYTES = 56 * 1024 * 1024
MATMUL_VMEM_BUDGET = 44 * 1024 * 1024
SUBLANES, LANES = 8, 128

HEAD_DIM = 64
Q_PER_KV = 8
WINDOW = 128
ROPE_THETA = 10000.0
EPS = 1e-6
N_MOD = 6
SSM_CHUNK = 16
SSM_GROUPS_PER_STEP = 8

ADAM_LR, ADAM_B1, ADAM_B2, ADAM_EPS, ADAM_WD, ADAM_STEP = 0.001, 0.9, 0.999, 1e-08, 0.01, 10
HIGHEST = lax.Precision.HIGHEST
SSM_PRECISION = lax.Precision.HIGH

RELATIONS_ALL = (1, 4, 2, 6, 5, 3, 7)
RELATIONS_SAME_CORE = (1, 4, 2, 6)
RELATIONS_OTHER_CORE = (5, 3, 7)
PLAN_MIDDLE = 0.65


def _cparams(sem):
    return pltpu.CompilerParams(dimension_semantics=sem, vmem_limit_bytes=VMEM_LIMIT_BYTES)


def _block_index(p):
    return 4 * p[0] + 2 * p[1] + p[2]


def _me():
    return lax.axis_index("x"), lax.axis_index("y"), lax.axis_index("c")


class _Plan:
    def middle(self, ins, outs, send, recv, local):
        pass


class _Gather(_Plan):
    TO_SIBLING, TO_X, TO_Y, RELAY, PASS_X, PASS_Y, PASS_DIAGONAL = range(7)

    def __init__(self, arrs):
        self.ins = list(arrs)
        self.out_shapes = [SDS((N_DEV,) + a.shape, a.dtype) for a in arrs]
        self.n_rdma, self.n_local = 7 * len(arrs), len(arrs)
        self.rdma_base = self.local_base = 0

    def _copy(self, ins, outs, send, recv, a, k, block, to, from_input=False):
        dst = outs[a].at[_block_index(block)]
        sem = self.rdma_base + a * 7 + k
        return pltpu.make_async_remote_copy(
            src_ref=ins[a] if from_input else dst, dst_ref=dst, send_sem=send.at[sem], recv_sem=recv.at[sem],
            device_id=to, device_id_type=pl.DeviceIdType.MESH)

    @staticmethod
    def _places():
        x, y, c = _me()
        return (x, y, c), (x, y, 1 - c), (1 - x, y, c), (x, 1 - y, c), (1 - x, 1 - y, c)

    def _first(self, ins, outs, send, recv, a):
        me, sibling, x_nbr, y_nbr, _ = self._places()
        return [self._copy(ins, outs, send, recv, a, k, me, to, True)
                for k, to in ((self.TO_SIBLING, sibling), (self.TO_X, x_nbr), (self.TO_Y, y_nbr))]

    def _mine(self, ins, outs, local, a):
        return pltpu.make_async_copy(ins[a], outs[a].at[_block_index(_me())], local.at[self.local_base + a])

    def start(self, ins, outs, send, recv, local):
        for a in range(len(ins)):
            self._mine(ins, outs, local, a).start()
            for cp in self._first(ins, outs, send, recv, a):
                cp.start()

    def middle(self, ins, outs, send, recv, local):
        me, sibling, x_nbr, y_nbr, _ = self._places()
        core = me[2]
        for a in range(len(ins)):
            self._copy(ins, outs, send, recv, a, self.TO_X, x_nbr, me).wait_recv()
            self._copy(ins, outs, send, recv, a, self.TO_Y, y_nbr, me).wait_recv()

            @pl.when(core == 0)
            def _():
                self._copy(ins, outs, send, recv, a, self.RELAY, x_nbr, y_nbr).start()

            @pl.when(core == 1)
            def _():
                self._copy(ins, outs, send, recv, a, self.RELAY, y_nbr, x_nbr).start()

            self._copy(ins, outs, send, recv, a, self.PASS_X, x_nbr, sibling).start()
            self._copy(ins, outs, send, recv, a, self.PASS_Y, y_nbr, sibling).start()

    def finish(self, ins, outs, send, recv, local):
        me, sibling, x_nbr, y_nbr, diagonal = self._places()
        other = lambda p: (p[0], p[1], 1 - p[2])
        for a in range(len(ins)):
            self._copy(ins, outs, send, recv, a, self.RELAY, diagonal, me).wait_recv()
            self._copy(ins, outs, send, recv, a, self.PASS_DIAGONAL, diagonal, sibling).start()
        for a in range(len(ins)):
            self._copy(ins, outs, send, recv, a, self.TO_SIBLING, sibling, me).wait_recv()
            for k, src in ((self.PASS_X, x_nbr), (self.PASS_Y, y_nbr), (self.PASS_DIAGONAL, diagonal)):
                self._copy(ins, outs, send, recv, a, k, other(src), me).wait_recv()
                self._copy(ins, outs, send, recv, a, k, src, sibling).wait_send()
            for cp in self._first(ins, outs, send, recv, a):
                cp.wait_send()
            self._copy(ins, outs, send, recv, a, self.RELAY, me, me).wait_send()
            self._mine(ins, outs, local, a).wait()


class _Exchange(_Plan):
    def __init__(self, arrs, relations):
        self.ins, self.relations = list(arrs), tuple(relations)
        self.out_shapes = [SDS((len(relations),) + a.shape[1:], a.dtype) for a in arrs]
        self.n_rdma, self.n_local = len(relations) * len(arrs), 0
        self.rdma_base = self.local_base = 0

    def _copies(self, ins, outs, send, recv):
        x, y, c = _me()
        cps = []
        for a in range(len(ins)):
            for s, k in enumerate(self.relations):
                peer = ((1 - x) if (k & 4) else x, (1 - y) if (k & 2) else y, (1 - c) if (k & 1) else c)
                sem = self.rdma_base + a * len(self.relations) + s
                cps.append(pltpu.make_async_remote_copy(
                    src_ref=ins[a].at[_block_index(peer)], dst_ref=outs[a].at[s], send_sem=send.at[sem],
                    recv_sem=recv.at[sem], device_id=peer, device_id_type=pl.DeviceIdType.MESH))
        return cps

    def start(self, ins, outs, send, recv, local):
        for cp in self._copies(ins, outs, send, recv):
            cp.start()

    def finish(self, ins, outs, send, recv, local):
        for cp in self._copies(ins, outs, send, recv):
            cp.wait()


class _PairSwap(_Plan):
    def __init__(self, arrs):
        self.ins = list(arrs)
        self.out_shapes = [SDS((4,) + a.shape[1:], a.dtype) for a in arrs]
        self.n_rdma, self.n_local = 4 * len(arrs), 0
        self.rdma_base = self.local_base = 0

    def _copies(self, ins, outs, send, recv):
        x, y, c = _me()
        cps = []
        for a in range(len(ins)):
            for s in range(4):
                sem = self.rdma_base + a * 4 + s
                cps.append(pltpu.make_async_remote_copy(
                    src_ref=ins[a].at[2 * s + (1 - c)], dst_ref=outs[a].at[s], send_sem=send.at[sem],
                    recv_sem=recv.at[sem], device_id=(x, y, 1 - c), device_id_type=pl.DeviceIdType.MESH))
        return cps

    def start(self, ins, outs, send, recv, local):
        for cp in self._copies(ins, outs, send, recv):
            cp.start()

    def finish(self, ins, outs, send, recv, local):
        for cp in self._copies(ins, outs, send, recv):
            cp.wait()


CHIP_X, CHIP_Y, CHIP_DIAGONAL = (1, 0), (0, 1), (1, 1)


class _ChipExchange(_Plan):
    def __init__(self, arrs, hops):
        self.ins, self.hops = list(arrs), tuple(hops)
        self.out_shapes = [SDS((len(hops),) + a.shape[1:], a.dtype) for a in arrs]
        self.n_rdma, self.n_local = len(hops) * len(arrs), 0
        self.rdma_base = self.local_base = 0

    def _copies(self, ins, outs, send, recv):
        x, y, c = _me()
        cps = []
        for a in range(len(ins)):
            for s, (fx, fy) in enumerate(self.hops):
                px, py = (1 - x) if fx else x, (1 - y) if fy else y
                sem = self.rdma_base + a * len(self.hops) + s
                cps.append(pltpu.make_async_remote_copy(
                    src_ref=ins[a].at[2 * px + py], dst_ref=outs[a].at[s], send_sem=send.at[sem],
                    recv_sem=recv.at[sem], device_id=(px, py, c), device_id_type=pl.DeviceIdType.MESH))
        return cps

    def start(self, ins, outs, send, recv, local):
        for cp in self._copies(ins, outs, send, recv):
            cp.start()

    def finish(self, ins, outs, send, recv, local):
        for cp in self._copies(ins, outs, send, recv):
            cp.wait()


class _Plans:
    def __init__(self, plans):
        self.plans = list(plans)
        self.ins = [a for p in plans for a in p.ins]
        self.out_shapes = [s for p in plans for s in p.out_shapes]
        self.n_rdma = self.n_local = 0
        for p in plans:
            p.rdma_base, p.local_base = self.n_rdma, self.n_local
            self.n_rdma, self.n_local = self.n_rdma + p.n_rdma, self.n_local + p.n_local

    def _each(self, ins, outs):
        i = o = 0
        for p in self.plans:
            yield p, ins[i:i + len(p.ins)], outs[o:o + len(p.out_shapes)]
            i, o = i + len(p.ins), o + len(p.out_shapes)

    def start(self, ins, outs, send, recv, local):
        for p, pi, po in self._each(ins, outs):
            p.start(pi, po, send, recv, local)

    def middle(self, ins, outs, send, recv, local):
        for p, pi, po in self._each(ins, outs):
            p.middle(pi, po, send, recv, local)

    def finish(self, ins, outs, send, recv, local):
        for p, pi, po in self._each(ins, outs):
            p.finish(pi, po, send, recv, local)


def _plan_scratch(plan):
    return [pltpu.SemaphoreType.DMA((plan.n_rdma,)), pltpu.SemaphoreType.DMA((plan.n_rdma,)),
            pltpu.SemaphoreType.DMA((max(plan.n_local, 1),))]


def _run_plan(name, plan):
    n = len(plan.ins)

    def body(*refs):
        ins, outs, sems = refs[:n], refs[n:len(refs) - 3], refs[len(refs) - 3:]
        plan.start(ins, outs, *sems)
        plan.middle(ins, outs, *sems)
        plan.finish(ins, outs, *sems)

    any_spec = pl.BlockSpec(memory_space=pl.ANY)
    return pl.pallas_call(body, name=name, out_shape=list(plan.out_shapes), in_specs=[any_spec] * n,
                          out_specs=[any_spec] * len(plan.out_shapes), scratch_shapes=_plan_scratch(plan))(*plan.ins)


def _pcall(name, body, grid, in_specs, ins, out_specs, out_shape, scratch=(), semantics=None, plan=None):
    if plan is None:
        return pl.pallas_call(body, name=name, grid=grid, in_specs=list(in_specs), out_specs=list(out_specs),
                              out_shape=list(out_shape), scratch_shapes=list(scratch),
                              compiler_params=_cparams(semantics))(*ins)
    n_in, n_out, n_scr = len(ins), len(out_shape), len(scratch)
    p_in, p_out = len(plan.ins), len(plan.out_shapes)

    def with_plan(*refs):
        k_in, c_in = refs[:n_in], refs[n_in:n_in + p_in]
        refs = refs[n_in + p_in:]
        k_out, c_out = refs[:n_out], refs[n_out:n_out + p_out]
        refs = refs[n_out + p_out:]
        k_scr, sems = refs[:n_scr], refs[n_scr:]
        step = 0
        for d, g in enumerate(grid):
            step = step * g + pl.program_id(d)
        n_steps = math.prod(grid)

        @pl.when(step == 0)
        def _():
            plan.start(c_in, c_out, *sems)

        @pl.when(step == min(n_steps - 1, int(n_steps * PLAN_MIDDLE)))
        def _():
            plan.middle(c_in, c_out, *sems)

        body(*k_in, *k_out, *k_scr)

        @pl.when(step == n_steps - 1)
        def _():
            plan.finish(c_in, c_out, *sems)

    any_spec = pl.BlockSpec(memory_space=pl.ANY)
    res = pl.pallas_call(
        with_plan, name=name, grid=grid, in_specs=list(in_specs) + [any_spec] * p_in,
        out_specs=list(out_specs) + [any_spec] * p_out, out_shape=list(out_shape) + list(plan.out_shapes),
        scratch_shapes=list(scratch) + _plan_scratch(plan),
        compiler_params=_cparams(("arbitrary",) * len(grid)))(*ins, *plan.ins)
    return res[:n_out], res[n_out:]


def _rowwise(name, fn, rows, vecs, row_outs, acc_outs=(), tm=128, plan=None):
    t = rows[0].shape[0]
    tm = min(tm, t)
    assert t % tm == 0 and tm % SUBLANES == 0
    n_r, n_v, n_o = len(rows), len(vecs), len(row_outs)

    def body(*refs):
        r_in, v_in = refs[:n_r], refs[n_r:n_r + n_v]
        r_out, a_out = refs[n_r + n_v:n_r + n_v + n_o], refs[n_r + n_v + n_o:]
        outs, accs = fn([r[...] for r in r_in], [v[...] for v in v_in])
        for o_ref, o in zip(r_out, outs, strict=True):
            o_ref[...] = o.astype(o_ref.dtype)
        if a_out:
            @pl.when(pl.program_id(0) == 0)
            def _():
                for a_ref in a_out:
                    a_ref[...] = jnp.zeros_like(a_ref)
            for a_ref, a in zip(a_out, accs, strict=True):
                a_ref[...] += a.reshape(tm // SUBLANES, SUBLANES, a.shape[-1]).sum(axis=0)

    in_specs = [pl.BlockSpec((tm, r.shape[1]), lambda i: (i, 0)) for r in rows]
    in_specs += [pl.BlockSpec(v.shape, lambda i: (0, 0)) for v in vecs]
    out_specs = [pl.BlockSpec((tm, w), lambda i: (i, 0)) for w, _ in row_outs]
    out_specs += [pl.BlockSpec((SUBLANES, w), lambda i: (0, 0)) for w in acc_outs]
    out_shape = [SDS((t, w), dt) for w, dt in row_outs] + [SDS((SUBLANES, w), F32) for w in acc_outs]
    return _pcall(name, body, (t // tm,), in_specs, [*rows, *vecs], out_specs, out_shape, semantics=("arbitrary",),
                  plan=plan)


def _tile(n, want):
    if n <= want:
        return n
    for t in range(want // LANES * LANES, 0, -LANES):
        if n % t == 0:
            return t
    raise ValueError(f"no tile for {n}")


_DOT_DIMS = {"nn": (((1,), (0,)), ((), ())), "nt": (((1,), (1,)), ((), ())), "tn": (((0,), (0,)), ((), ()))}


def _matmul(name, a, b, mode, out_dtypes, epilogue=None, extras=(), vecs=(), a_pro=None,
            tm=1024, tn=512, tk=4096, exact=False, b_blocked=False, out_blocked=0, b_rows=None, plan=None):
    cs = b.shape[-1] if b_blocked else None
    b2 = (b.shape[1], b.shape[0] * b.shape[2]) if b_blocked else b.shape
    if mode == "tn":
        (k, m), (k2, n) = a.shape, b2
    elif mode == "nt":
        (m, k), (n, k2) = a.shape, b2
    else:
        (m, k), (k2, n) = a.shape, b2
    assert k == k2 and not (b_blocked and mode == "tn")
    row0 = 0
    if b_rows is not None:
        assert mode == "nt" and not b_blocked
        row0, n = b_rows
        tn = _tile(math.gcd(n, row0) if row0 else n, tn)
    tm, tn, tk = _tile(m, tm), _tile(n, tn), _tile(k, tk)
    if b_blocked and mode == "nn":
        tn = _tile(cs, tn)
    if b_blocked and mode == "nt":
        tk = _tile(cs, tk)
    if out_blocked:
        tn = _tile(n // out_blocked, tn)
    nk = k // tk

    def vmem_bytes(width):
        operands = 2 * (tm * tk * a.dtype.itemsize + tk * width * b.dtype.itemsize)
        tiles = 2 * tm * width * (sum(jnp.dtype(dt).itemsize for dt in out_dtypes) + sum(e.dtype.itemsize for e in extras))
        return operands + tiles + tm * width * 4 * (2 if nk > 1 else 1)

    extent = cs if (b_blocked and mode == "nn") else n // out_blocked if out_blocked else n
    if extent % (2 * tn) == 0 and row0 % (2 * tn) == 0 and vmem_bytes(2 * tn) <= MATMUL_VMEM_BUDGET:
        tn *= 2
    n_e, n_v, n_o = len(extras), len(vecs), len(out_dtypes)
    precision = HIGHEST if exact else None

    def body(*refs):
        a_ref, b_ref = refs[:2]
        e_refs, v_refs = refs[2:2 + n_e], refs[2 + n_e:2 + n_e + n_v]
        o_refs = refs[2 + n_e + n_v:2 + n_e + n_v + n_o]

        def product():
            av = a_ref[...]
            if a_pro is not None:
                av = a_pro(av)
            return lax.dot_general(av, b_ref[...], _DOT_DIMS[mode], precision=precision, preferred_element_type=F32)

        def finish(acc):
            res = (acc,) if epilogue is None else epilogue(acc, [e[...] for e in e_refs], [v[...] for v in v_refs])
            for o_ref, r in zip(o_refs, res, strict=True):
                o_ref[...] = r.astype(o_ref.dtype)

        if nk == 1:
            finish(product())
            return
        acc_ref = refs[-1]
        kk = pl.program_id(2)

        @pl.when(kk == 0)
        def _():
            acc_ref[...] = product()

        @pl.when(kk > 0)
        def _():
            acc_ref[...] += product()

        @pl.when(kk == nk - 1)
        def _():
            finish(acc_ref[...])

    if mode == "tn":
        a_spec = pl.BlockSpec((tk, tm), lambda i, j, kk: (kk, i))
    else:
        a_spec = pl.BlockSpec((tm, tk), lambda i, j, kk: (i, kk))
    if b_blocked and mode == "nn":
        per = cs // tn
        b_spec = pl.BlockSpec((None, tk, tn), lambda i, j, kk: (j // per, kk, j % per))
    elif b_blocked:
        per = cs // tk
        b_spec = pl.BlockSpec((None, tn, tk), lambda i, j, kk: (kk // per, j, kk % per))
    elif mode == "nt":
        assert row0 % tn == 0
        b_spec = pl.BlockSpec((tn, tk), lambda i, j, kk: (j + row0 // tn, kk))
    else:
        b_spec = pl.BlockSpec((tk, tn), lambda i, j, kk: (kk, j))
    tile = pl.BlockSpec((tm, tn), lambda i, j, kk: (i, j))
    if out_blocked:
        per_o = n // out_blocked // tn
        out_spec = pl.BlockSpec((None, tm, tn), lambda i, j, kk: (j // per_o, i, j % per_o))
        out_shape = [SDS((out_blocked, m, n // out_blocked), dt) for dt in out_dtypes]
    else:
        out_spec, out_shape = tile, [SDS((m, n), dt) for dt in out_dtypes]
    in_specs = [a_spec, b_spec] + [tile] * n_e + [pl.BlockSpec((1, tn), lambda i, j, kk: (0, j))] * n_v
    res = _pcall(name, body, (m // tm, n // tn, nk), in_specs, [a, b, *extras, *vecs], [out_spec] * n_o, out_shape,
                 scratch=[pltpu.VMEM((tm, tn), F32)] if nk > 1 else [],
                 semantics=("parallel", "parallel", "arbitrary"), plan=plan)
    if plan is None:
        return res[0] if n_o == 1 else res
    return (res[0][0] if n_o == 1 else res[0]), res[1]


def _rms_fwd(x):
    r = lax.rsqrt(jnp.mean(x * x, axis=-1, keepdims=True) + EPS)
    return x * r, r


def _rms_bwd(dxn, xn, r):
    return r * (dxn - xn * jnp.mean(dxn * xn, axis=-1, keepdims=True))


_INV_SQRT2 = 1.0 / math.sqrt(2.0)
_INV_SQRT2PI = 1.0 / math.sqrt(2.0 * math.pi)


def _gelu(y):
    return 0.5 * y * (1.0 + lax.erf(y * _INV_SQRT2))


def _gelu_grad(y):
    return 0.5 * (1.0 + lax.erf(y * _INV_SQRT2)) + y * (_INV_SQRT2PI * jnp.exp(-0.5 * y * y))


def _sigmoid(z):
    return 1.0 / (1.0 + jnp.exp(-z))


def _adam_math(w, g, m, v):
    m = ADAM_B1 * m + (1.0 - ADAM_B1) * g
    v = ADAM_B2 * v + (1.0 - ADAM_B2) * (g * g)
    m_hat = m / (1.0 - ADAM_B1 ** ADAM_STEP)
    v_hat = v / (1.0 - ADAM_B2 ** ADAM_STEP)
    delta = -ADAM_LR * (m_hat / (jnp.sqrt(v_hat) + ADAM_EPS) + ADAM_WD * w)
    return delta, m, v


def _norm_mod_fwd(name, x, g, scale, shift):
    def fn(rows, vecs):
        (xv,), (gv, sc, sh) = rows, vecs
        xn, _ = _rms_fwd(xv)
        return [(xn * gv) * (1.0 + sc) + sh], []
    return _rowwise(name, fn, [x], [g, scale, shift], [(x.shape[1], BF16)])[0]


def _norm_mod_bwd(name, x, dh, dres, g, scale, gated=None, plan=None):
    d = x.shape[1]

    def fn(rows, vecs):
        xv, dhv, drv = rows[:3]
        gv, sc = vecs[:2]
        xn, r = _rms_fwd(xv)
        t = xn * gv
        dt = dhv * (1.0 + sc)
        dx = drv + _rms_bwd(dt * gv, xn, r)
        if gated is None:
            return [dx], [dhv * t, dhv, dt * xn]
        return [dx, dx * vecs[2]], [dhv * t, dhv, dt * xn, dx * rows[3].astype(F32)]
    extra_rows, extra_vecs = ([gated[0]], [gated[1]]) if gated is not None else ([], [])
    res = _rowwise(name, fn, [x, dh, dres] + extra_rows, [g, scale] + extra_vecs,
                   [(d, F32)] + [(d, BF16)] * len(extra_rows), [d] * (3 + len(extra_rows)), plan=plan)
    outs, rest = res if plan is not None else (res, None)
    n_rows = 1 + len(extra_rows)
    return (*outs[:n_rows], *[a.sum(0) for a in outs[n_rows:]]), rest


def _final_loss(name, x, tgt, g, val, gate):
    d = x.shape[1]

    def fn(rows, vecs):
        (xv, tv, vv), (gv, gate_v) = rows, vecs
        xn, r = _rms_fwd(xv)
        e = xn * gv - tv
        dy = e * (1.0 / d)
        dx = _rms_bwd(dy * gv, xn, r)
        return [dx, dx * gate_v], [e * e, dy * xn, dx * vv.astype(F32)]
    dx, dval, sq, dg, dgate = _rowwise(name, fn, [x, tgt, val], [g, gate], [(d, F32), (d, BF16)], [d, d, d])
    return dx, dval, 0.5 * jnp.sum(sq) / d, dg.sum(0), dgate.sum(0)


def _group_norm_fwd(name, attn, ssm, g_a, g_s):
    def fn(rows, vecs):
        (av, sv), (ga, gs) = rows, vecs
        return [jnp.concatenate([_rms_fwd(av)[0] * ga, _rms_fwd(sv)[0] * gs], axis=1)], []
    return _rowwise(name, fn, [attn, ssm], [g_a, g_s], [(attn.shape[1] + ssm.shape[1], BF16)])[0]


def _group_norm_bwd(name, attn, ssm, dmixed, g_a, g_s, plan=None):
    da_w, ds_w = attn.shape[1], ssm.shape[1]

    def fn(rows, vecs):
        (av, sv, dm), (ga, gs) = rows, vecs
        an, ra = _rms_fwd(av)
        sn, rs = _rms_fwd(sv)
        dma, dms = dm[:, :da_w], dm[:, da_w:]
        return [_rms_bwd(dma * ga, an, ra), _rms_bwd(dms * gs, sn, rs)], [dma * an, dms * sn]
    res = _rowwise(name, fn, [attn, ssm, dmixed], [g_a, g_s], [(da_w, F32), (ds_w, F32)], [da_w, ds_w], plan=plan)
    (dattn, dssm, dga, dgs), rest = res if plan is not None else (res, None)
    return (dattn, dssm, dga.sum(0), dgs.sum(0)), rest


def _gelu_fwd(name, y):
    def fn(rows, vecs):
        return [_gelu(rows[0])], []
    return _rowwise(name, fn, [y], [], [(y.shape[1], BF16)])[0]


def _glu_bwd(name, dout, y, z):
    d = y.shape[1]

    def fn(rows, vecs):
        dov, yv, zv = rows
        sg = _sigmoid(zv)
        dz = dov * _gelu(yv) * sg * (1.0 - sg)
        return [dz, dov * sg], [dz]
    dz, dyg, db = _rowwise(name, fn, [dout, y, z], [], [(d, BF16), (d, F32)], [d])
    return dz, dyg, db.sum(0)


def _adam_shard(name, parts, w, m, v):
    r, c = w.shape
    n_parts = sum(1 if p.ndim == 2 else p.shape[0] for p in parts)
    row_bytes = 2 * c * (n_parts * parts[0].dtype.itemsize + 7 * 4)
    tr = min(128, r)
    while tr > SUBLANES and tr * row_bytes > VMEM_LIMIT_BYTES // 2:
        tr //= 2
    assert r % tr == 0
    n_p = len(parts)

    def body(*refs):
        p_refs, (w_ref, m_ref, v_ref, g_out, d_out, m_out, v_out) = refs[:n_p], refs[n_p:]
        g = None
        for p_ref in p_refs:
            terms = [p_ref[...]] if len(p_ref.shape) == 2 else [p_ref[j] for j in range(p_ref.shape[0])]
            for t in terms:
                g = t.astype(F32) if g is None else g + t.astype(F32)
        delta, m_new, v_new = _adam_math(w_ref[...], g, m_ref[...], v_ref[...])
        g_out[...], d_out[...], m_out[...], v_out[...] = g, delta, m_new, v_new

    tile = pl.BlockSpec((tr, c), lambda i: (i, 0))
    p_specs = [tile if p.ndim == 2 else pl.BlockSpec((p.shape[0], tr, c), lambda i: (0, i, 0)) for p in parts]
    return _pcall(name, body, (r // tr,), p_specs + [tile] * 3, [*parts, w, m, v], [tile] * 4, [SDS((r, c), F32)] * 4,
                  semantics=("parallel",))


def _pair_add(name, blocks, from_sibling):
    _, r, c = blocks.shape
    tr = min(256, r)
    assert r % tr == 0

    def body(b0_ref, b1_ref, s_ref, o_ref):
        mine = jnp.where(lax.axis_index("c") == 0, b0_ref[...].astype(F32), b1_ref[...].astype(F32))
        o_ref[...] = (mine + s_ref[...].astype(F32)).astype(o_ref.dtype)

    core_block = lambda k: pl.BlockSpec((None, tr, c), lambda s, i: (2 * s + k, i, 0))
    slot = pl.BlockSpec((None, tr, c), lambda s, i: (s, i, 0))
    return _pcall(name, body, (4, r // tr), [core_block(0), core_block(1), slot], [blocks, blocks, from_sibling],
                  [slot], [SDS((4, r, c), blocks.dtype)], semantics=("parallel", "parallel"))[0]


def _ada_update(name, c_act_t, dmod, w, m, v, tr=128, plan=None):
    r, c = w.shape
    tr = min(tr, r)
    assert r % tr == 0

    def body(c_ref, d_ref, w_ref, m_ref, v_ref, g_out, d_out, m_out, v_out):
        g = jnp.dot(c_ref[...], d_ref[...], precision=lax.Precision.HIGHEST, preferred_element_type=F32)
        delta, m_new, v_new = _adam_math(w_ref[...], g, m_ref[...], v_ref[...])
        g_out[...], d_out[...], m_out[...], v_out[...] = g, delta, m_new, v_new

    tile = pl.BlockSpec((tr, c), lambda i: (i, 0))
    in_specs = [pl.BlockSpec((tr, N_DEV), lambda i: (i, 0)), pl.BlockSpec((N_DEV, c), lambda i: (0, 0)), tile, tile, tile]
    return _pcall(name, body, (r // tr,), in_specs, [c_act_t, dmod, w, m, v], [tile] * 4, [SDS((r, c), F32)] * 4,
                  semantics=("parallel",), plan=plan)


def _rotate_half(x):
    w = x.shape[1]
    half = HEAD_DIM // 2
    lane = lax.broadcasted_iota(jnp.int32, x.shape, 1)
    return jnp.where((lane % HEAD_DIM) < half, -pltpu.roll(x, w - half, 1), pltpu.roll(x, half, 1))


def _lane_tile(tab, w):
    return tab[:, :w] if w <= LANES else jnp.tile(tab, (1, w // LANES))


def _rope(x, cos, sin):
    return x * cos + _rotate_half(x) * sin


def _rope_t(dy, cos, sin):
    return dy * cos - _rotate_half(dy) * sin


def _band_mask(n):
    shape = (Q_PER_KV * WINDOW, 2 * WINDOW)
    i = lax.broadcasted_iota(jnp.int32, shape, 0) & (WINDOW - 1)
    j = lax.broadcasted_iota(jnp.int32, shape, 1)
    return (j > i) & (j <= i + WINDOW) & ((n > 0) | (j >= WINDOW))


def _stack_heads(x, hk):
    first = hk * Q_PER_KV
    return jnp.concatenate([x[:, (first + g) * HEAD_DIM:(first + g + 1) * HEAD_DIM] for g in range(Q_PER_KV)], axis=0)


def _stack_cols(ref, hk):
    first = hk * Q_PER_KV
    return jnp.concatenate([ref[:, first + g:first + g + 1] for g in range(Q_PER_KV)], axis=0)


def _stack_sinks(sink_ref, hk):
    first = hk * Q_PER_KV
    return jnp.concatenate([jnp.broadcast_to(sink_ref[0:1, first + g:first + g + 1], (WINDOW, 1))
                            for g in range(Q_PER_KV)], axis=0)


def _attn_specs(da, dkv, nb):
    cur = lambda n: (jnp.minimum(n, nb - 1), 0)
    prev = lambda n: (jnp.maximum(jnp.minimum(n, nb - 1) - 1, 0), 0)
    return dict(
        q=pl.BlockSpec((WINDOW, da), cur), kv_cur=pl.BlockSpec((WINDOW, dkv), cur),
        kv_prev=pl.BlockSpec((WINDOW, dkv), prev), tab_cur=pl.BlockSpec((WINDOW, LANES), cur),
        tab_prev=pl.BlockSpec((WINDOW, LANES), prev))


def _attn_fwd(name, q, k, v, cos, sin, sinks, plan=None):
    s, da = q.shape
    dkv = k.shape[1]
    nq, nb = da // HEAD_DIM, s // WINDOW
    scale = HEAD_DIM ** -0.5

    def body(q_ref, kp_ref, kc_ref, vp_ref, vc_ref, cc_ref, sc_ref, cp_ref, sp_ref, sink_ref, o_ref, lse_ref):
        n = pl.program_id(0)
        cc, sc, cp, sp = cc_ref[...], sc_ref[...], cp_ref[...], sp_ref[...]
        qr = _rope(q_ref[...], _lane_tile(cc, da), _lane_tile(sc, da)).astype(BF16)
        kk = jnp.concatenate([_rope(kp_ref[...], _lane_tile(cp, dkv), _lane_tile(sp, dkv)),
                              _rope(kc_ref[...], _lane_tile(cc, dkv), _lane_tile(sc, dkv))], axis=0).astype(BF16)
        vv = jnp.concatenate([vp_ref[...], vc_ref[...]], axis=0).astype(BF16)
        valid = _band_mask(n)
        for hk in range(nq // Q_PER_KV):
            ks = slice(hk * HEAD_DIM, (hk + 1) * HEAD_DIM)
            sco = lax.dot_general(_stack_heads(qr, hk), kk[:, ks], _DOT_DIMS["nt"], preferred_element_type=F32) * scale
            sco = jnp.where(valid, sco, -1e30)
            sink = _stack_sinks(sink_ref, hk)
            mx = jnp.maximum(jnp.max(sco, axis=1, keepdims=True), sink)
            p = jnp.exp(sco - mx)
            den = jnp.sum(p, axis=1, keepdims=True) + jnp.exp(sink - mx)
            o8 = jnp.dot((p / den).astype(BF16), vv[:, ks], preferred_element_type=F32)
            lse8 = mx + jnp.log(den)
            for g in range(Q_PER_KV):
                hq, rows = hk * Q_PER_KV + g, slice(g * WINDOW, (g + 1) * WINDOW)
                o_ref[:, hq * HEAD_DIM:(hq + 1) * HEAD_DIM] = o8[rows]
                lse_ref[:, hq:hq + 1] = lse8[rows]

    sp_ = _attn_specs(da, dkv, nb)
    in_specs = [sp_["q"], sp_["kv_prev"], sp_["kv_cur"], sp_["kv_prev"], sp_["kv_cur"],
                sp_["tab_cur"], sp_["tab_cur"], sp_["tab_prev"], sp_["tab_prev"], pl.BlockSpec((1, nq), lambda n: (0, 0))]
    return _pcall(name, body, (nb,), in_specs, [q, k, k, v, v, cos, sin, cos, sin, sinks],
                  [sp_["q"], pl.BlockSpec((WINDOW, nq), lambda n: (n, 0))], [SDS((s, da), F32), SDS((s, nq), F32)],
                  semantics=("arbitrary",), plan=plan)


def _attn_bwd(name, q, k, v, cos, sin, sinks, out, lse, dout, plan=None):
    s, da = q.shape
    dkv = k.shape[1]
    nq, nb = da // HEAD_DIM, s // WINDOW
    scale = HEAD_DIM ** -0.5

    def body(q_ref, kp_ref, kc_ref, vp_ref, vc_ref, cc_ref, sc_ref, cp_ref, sp_ref, sink_ref, o_ref, lse_ref,
             do_ref, dq_ref, dk_ref, dv_ref, dsink_ref, dk_carry, dv_carry):
        n = pl.program_id(0)
        cp, sp = _lane_tile(cp_ref[...], dkv), _lane_tile(sp_ref[...], dkv)

        @pl.when(n == 0)
        def _():
            dk_carry[...] = jnp.zeros_like(dk_carry)
            dv_carry[...] = jnp.zeros_like(dv_carry)
            dsink_ref[...] = jnp.zeros_like(dsink_ref)

        @pl.when(n < nb)
        def _():
            cc, sc = cc_ref[...], sc_ref[...]
            qr = _rope(q_ref[...], _lane_tile(cc, da), _lane_tile(sc, da)).astype(BF16)
            kk = jnp.concatenate([_rope(kp_ref[...], cp, sp),
                                  _rope(kc_ref[...], _lane_tile(cc, dkv), _lane_tile(sc, dkv))], axis=0).astype(BF16)
            vv = jnp.concatenate([vp_ref[...], vc_ref[...]], axis=0).astype(BF16)
            valid = _band_mask(n)
            do_all, o_all = do_ref[...], o_ref[...]
            for hk in range(nq // Q_PER_KV):
                ks = slice(hk * HEAD_DIM, (hk + 1) * HEAD_DIM)
                q8, lse8 = _stack_heads(qr, hk), _stack_cols(lse_ref, hk)
                sco = lax.dot_general(q8, kk[:, ks], _DOT_DIMS["nt"], preferred_element_type=F32) * scale
                probs = jnp.where(valid, jnp.exp(sco - lse8), 0.0)
                do8 = _stack_heads(do_all, hk)
                delta = jnp.sum(do8 * _stack_heads(o_all, hk), axis=1, keepdims=True)
                do8 = do8.astype(BF16)
                dp = lax.dot_general(do8, vv[:, ks], _DOT_DIMS["nt"], preferred_element_type=F32)
                ds = (probs * (dp - delta) * scale).astype(BF16)
                dq8 = jnp.dot(ds, kk[:, ks], preferred_element_type=F32)
                dk_h = lax.dot_general(ds, q8, _DOT_DIMS["tn"], preferred_element_type=F32)
                dv_h = lax.dot_general(probs.astype(BF16), do8, _DOT_DIMS["tn"], preferred_element_type=F32)
                dsink8 = -jnp.exp(_stack_sinks(sink_ref, hk) - lse8) * delta
                for g in range(Q_PER_KV):
                    hq, rows = hk * Q_PER_KV + g, slice(g * WINDOW, (g + 1) * WINDOW)
                    dq_ref[:, hq * HEAD_DIM:(hq + 1) * HEAD_DIM] = dq8[rows]
                    dsink_ref[:, hq:hq + 1] += dsink8[rows].reshape(WINDOW // SUBLANES, SUBLANES, 1).sum(axis=0)
                dk_ref[:, ks] = dk_carry[:, ks] + dk_h[:WINDOW]
                dv_ref[:, ks] = dv_carry[:, ks] + dv_h[:WINDOW]
                dk_carry[:, ks] = dk_h[WINDOW:]
                dv_carry[:, ks] = dv_h[WINDOW:]
            dq_ref[...] = _rope_t(dq_ref[...], _lane_tile(cc, da), _lane_tile(sc, da))
            dk_ref[...] = _rope_t(dk_ref[...], cp, sp)

        @pl.when(n == nb)
        def _():
            dk_ref[...] = _rope_t(dk_carry[...], cp, sp)
            dv_ref[...] = dv_carry[...]

    sp_ = _attn_specs(da, dkv, nb)
    last_prev = lambda n: (jnp.maximum(n - 1, 0), 0)
    tab_prev = pl.BlockSpec((WINDOW, LANES), last_prev)
    kv_out = pl.BlockSpec((WINDOW, dkv), last_prev)
    lse_spec = pl.BlockSpec((WINDOW, nq), lambda n: (jnp.minimum(n, nb - 1), 0))
    in_specs = [sp_["q"], sp_["kv_prev"], sp_["kv_cur"], sp_["kv_prev"], sp_["kv_cur"],
                sp_["tab_cur"], sp_["tab_cur"], tab_prev, tab_prev,
                pl.BlockSpec((1, nq), lambda n: (0, 0)), sp_["q"], lse_spec, sp_["q"]]
    res = _pcall(name, body, (nb + 1,), in_specs, [q, k, k, v, v, cos, sin, cos, sin, sinks, out, lse, dout],
                 [sp_["q"], kv_out, kv_out, pl.BlockSpec((SUBLANES, nq), lambda n: (0, 0))],
                 [SDS((s, da), F32), SDS((s, dkv), F32), SDS((s, dkv), F32), SDS((SUBLANES, nq), F32)],
                 scratch=[pltpu.VMEM((WINDOW, dkv), F32), pltpu.VMEM((WINDOW, dkv), F32)],
                 semantics=("arbitrary",), plan=plan)
    (dq, dk, dv, dsink), rest = res if plan is not None else (res, None)
    return (dq, dk, dv, dsink.sum(0)), rest


def _ssm_operators(lam_re, lam_im, log_step, b_re, b_im, c_re, c_im, d_skip):
    g, p = lam_re.shape
    h = b_re.shape[-1]
    l = SSM_CHUNK
    step = jnp.exp(log_step)[:, None]
    mag = jnp.exp(lam_re * step)
    ar, ai = mag * jnp.cos(lam_im * step), mag * jnp.sin(lam_im * step)
    den = lam_re * lam_re + lam_im * lam_im
    cr = ((ar - 1.0) * lam_re + ai * lam_im) / den
    ci = (ai * lam_re - (ar - 1.0) * lam_im) / den
    bbr = cr[..., None] * b_re - ci[..., None] * b_im
    bbi = cr[..., None] * b_im + ci[..., None] * b_re
    pr, pi = [jnp.ones_like(ar)], [jnp.zeros_like(ar)]
    for _ in range(l):
        pr, pi = pr + [pr[-1] * ar - pi[-1] * ai], pi + [pr[-1] * ai + pi[-1] * ar]
    pwr, pwi = jnp.stack(pr, axis=1), jnp.stack(pi, axis=1)
    cpr = c_re[:, None] * pwr[:, :, None, :] - c_im[:, None] * pwi[:, :, None, :]
    cpi = c_re[:, None] * pwi[:, :, None, :] + c_im[:, None] * pwr[:, :, None, :]
    kern = (jnp.einsum("gtop,gpi->gtoi", cpr[:, :l], bbr, precision=lax.Precision.HIGHEST)
            - jnp.einsum("gtop,gpi->gtoi", cpi[:, :l], bbi, precision=lax.Precision.HIGHEST))
    kern = kern.at[:, 0].add(d_skip.reshape(g, h)[:, :, None] * jnp.eye(h, dtype=F32))
    tm = jnp.stack([jnp.pad(kern[:, :l - j], ((0, 0), (j, 0), (0, 0), (0, 0))) for j in range(l)], axis=1)
    tm = tm.transpose(0, 1, 4, 2, 3).reshape(g, l * h, l * h)
    rev_r, rev_i = pwr[:, l - 1::-1][:, :l], pwi[:, l - 1::-1][:, :l]
    er = rev_r[:, :, None, :] * bbr.transpose(0, 2, 1)[:, None] - rev_i[:, :, None, :] * bbi.transpose(0, 2, 1)[:, None]
    ei = rev_r[:, :, None, :] * bbi.transpose(0, 2, 1)[:, None] + rev_i[:, :, None, :] * bbr.transpose(0, 2, 1)[:, None]
    em = jnp.concatenate([er, ei], axis=-1).reshape(g, l * h, 2 * p)
    fr = cpr[:, 1:].transpose(0, 3, 1, 2).reshape(g, p, l * h)
    fi = -cpi[:, 1:].transpose(0, 3, 1, 2).reshape(g, p, l * h)
    fm = jnp.concatenate([fr, fi], axis=1)
    return tm, em, fm, pwr[:, l], pwi[:, l]


def _decay_lanes(alr, ali):
    return jnp.concatenate([alr, alr], axis=1), jnp.concatenate([-ali, ali], axis=1)


def _ssm_fwd(name, u, tm, em, fm, acat, bcat, plan=None):
    s, ds = u.shape
    g, lh, p2 = em.shape
    gb, h = SSM_GROUPS_PER_STEP, lh // SSM_CHUNK
    assert gb * h == LANES and g * h == ds and s % SSM_CHUNK == 0
    nc, half = s // SSM_CHUNK, p2 // 2

    def body(u_ref, tm_ref, em_ref, fm_ref, a_ref, b_ref, y_ref, xp_ref, uc_ref, yc_ref, st_ref):
        _to_chunks(u_ref, uc_ref, nc, h)
        for i in range(gb):
            st_ref[pl.ds(i, nc, stride=gb), :] = jnp.dot(uc_ref[i], em_ref[i], precision=SSM_PRECISION,
                                                         preferred_element_type=F32)
        av, bv = a_ref[...], b_ref[...]

        def step(c, carry):
            x, xs = carry
            rows = pl.ds(pl.multiple_of(c * gb, gb), gb)
            loc = st_ref[rows, :]
            st_ref[rows, :] = x
            return av * x + bv * xs + loc, av * xs - bv * x + pltpu.roll(loc, half, 1)
        zero = jnp.zeros((gb, p2), F32)
        lax.fori_loop(0, nc, step, (zero, zero), unroll=4)
        for i in range(gb):
            xp = st_ref[pl.ds(i, nc, stride=gb), :]
            xp_ref[i] = xp
            yc_ref[i] = (jnp.dot(uc_ref[i], tm_ref[i], precision=SSM_PRECISION, preferred_element_type=F32)
                         + jnp.dot(xp, fm_ref[i], precision=SSM_PRECISION, preferred_element_type=F32))
        _from_chunks(yc_ref, y_ref, nc, h)

    blk = lambda r, c: pl.BlockSpec((gb, r, c), lambda i: (i, 0, 0))
    vec = pl.BlockSpec((gb, p2), lambda i: (i, 0))
    col = pl.BlockSpec((s, LANES), lambda i: (0, i))
    return _pcall(name, body, (g // gb,), [col, blk(lh, lh), blk(lh, p2), blk(p2, lh), vec, vec],
                  [u, tm, em, fm, acat, bcat], [col, blk(nc, p2), blk(nc, lh)],
                  [SDS((s, ds), F32), SDS((g, nc, p2), F32), SDS((g, nc, lh), F32)],
                  scratch=[pltpu.VMEM((gb, nc, lh), F32), pltpu.VMEM((nc * gb, p2), F32)],
                  semantics=("parallel",), plan=plan)


def _ssm_bwd(name, u_chunks, dy, xprev, tm, em, fm, acat, bcat, plan=None):
    s, ds = dy.shape
    g, lh, p2 = em.shape
    gb, h = SSM_GROUPS_PER_STEP, lh // SSM_CHUNK
    nc, half = s // SSM_CHUNK, p2 // 2

    def body(uc_ref, dy_ref, xp_ref, tm_ref, em_ref, fm_ref, a_ref, b_ref,
             du_ref, dtm_ref, dem_ref, dfm_ref, r1_ref, r2_ref, dyc_ref, duc_ref, gs_ref, xs_ref):
        _to_chunks(dy_ref, dyc_ref, nc, h)
        for i in range(gb):
            gs_ref[pl.ds(i, nc, stride=gb), :] = lax.dot_general(
                dyc_ref[i], fm_ref[i], _DOT_DIMS["nt"], precision=SSM_PRECISION, preferred_element_type=F32)
            xs_ref[pl.ds(i, nc, stride=gb), :] = xp_ref[i]
        av, bv = a_ref[...], b_ref[...]

        def step(t, carry):
            grad, gsw, r1, r2 = carry
            c = nc - 1 - t
            rows = pl.ds(pl.multiple_of(c * gb, gb), gb)
            dxp, xp = gs_ref[rows, :], xs_ref[rows, :]
            gs_ref[rows, :] = grad
            r1 = r1 + grad * xp
            r2 = r2 + grad * pltpu.roll(xp, half, 1)
            return dxp + av * grad - bv * gsw, pltpu.roll(dxp, half, 1) + av * gsw + bv * grad, r1, r2
        zero = jnp.zeros((gb, p2), F32)
        _, _, r1, r2 = lax.fori_loop(0, nc, step, (zero, zero, zero, zero), unroll=4)
        r1_ref[...], r2_ref[...] = r1, r2
        for i in range(gb):
            dxl = gs_ref[pl.ds(i, nc, stride=gb), :]
            duc_ref[i] = (lax.dot_general(dyc_ref[i], tm_ref[i], _DOT_DIMS["nt"], precision=SSM_PRECISION,
                                          preferred_element_type=F32)
                          + lax.dot_general(dxl, em_ref[i], _DOT_DIMS["nt"], precision=SSM_PRECISION,
                                            preferred_element_type=F32))
            dtm_ref[i] = lax.dot_general(uc_ref[i], dyc_ref[i], _DOT_DIMS["tn"], precision=SSM_PRECISION,
                                         preferred_element_type=F32)
            dfm_ref[i] = lax.dot_general(xp_ref[i], dyc_ref[i], _DOT_DIMS["tn"], precision=SSM_PRECISION,
                                         preferred_element_type=F32)
            dem_ref[i] = lax.dot_general(uc_ref[i], dxl, _DOT_DIMS["tn"], precision=SSM_PRECISION,
                                         preferred_element_type=F32)
        _from_chunks(duc_ref, du_ref, nc, h)

    blk = lambda r, c: pl.BlockSpec((gb, r, c), lambda i: (i, 0, 0))
    vec = pl.BlockSpec((gb, p2), lambda i: (i, 0))
    col = pl.BlockSpec((s, LANES), lambda i: (0, i))
    chunked = pltpu.VMEM((gb, nc, lh), F32)
    res = _pcall(name, body, (g // gb,),
                 [blk(nc, lh), col, blk(nc, p2), blk(lh, lh), blk(lh, p2), blk(p2, lh), vec, vec],
                 [u_chunks, dy, xprev, tm, em, fm, acat, bcat],
                 [col, blk(lh, lh), blk(lh, p2), blk(p2, lh), vec, vec],
                 [SDS((s, ds), F32), SDS((g, lh, lh), F32), SDS((g, lh, p2), F32), SDS((g, p2, lh), F32),
                  SDS((g, p2), F32), SDS((g, p2), F32)],
                 scratch=[chunked, chunked, pltpu.VMEM((nc * gb, p2), F32), pltpu.VMEM((nc * gb, p2), F32)],
                 semantics=("parallel",), plan=plan)
    return res if plan is not None else (res, None)


def _to_chunks(src_ref, dst_ref, nc, h):
    per = LANES // h
    grp = lax.broadcasted_iota(jnp.int32, (nc, LANES), 1) // h
    for g in range(per):
        for part in range(SSM_CHUNK * h // LANES):
            acc = None
            for i in range(part * per, (part + 1) * per):
                piece = src_ref[pl.ds(i, nc, stride=SSM_CHUNK), :]
                lo = (i * h) % LANES
                if (lo - g * h) % LANES:
                    piece = pltpu.roll(piece, (lo - g * h) % LANES, 1)
                acc = piece if acc is None else jnp.where(grp == lo // h, piece, acc)
            dst_ref[g, :, part * LANES:(part + 1) * LANES] = acc


def _from_chunks(src_ref, dst_ref, nc, h):
    per = LANES // h
    grp = lax.broadcasted_iota(jnp.int32, (nc, LANES), 1) // h
    for i in range(SSM_CHUNK):
        part, lo = divmod(i * h, LANES)
        row = None
        for g in range(per):
            piece = src_ref[g, :, part * LANES:(part + 1) * LANES]
            if (g * h - lo) % LANES:
                piece = pltpu.roll(piece, (g * h - lo) % LANES, 1)
            row = piece if row is None else jnp.where(grp == g, piece, row)
        dst_ref[pl.ds(i, nc, stride=SSM_CHUNK), :] = row


_SMALL = ("b_ada", "norm1_g", "sinks", "ssm_lam_re", "ssm_lam_im", "ssm_log_step", "ssm_b_re", "ssm_b_im",
          "ssm_c_re", "ssm_c_im", "ssm_d", "b_glu", "attn_out_g", "ssm_out_g", "norm2_g", "final_g")
_WEIGHTS = ("w_ada", "b_ada", "norm1_g", "w_in", "sinks", "ssm_lam_re", "ssm_lam_im", "ssm_log_step", "ssm_b_re",
            "ssm_b_im", "ssm_c_re", "ssm_c_im", "ssm_d", "w_glu", "b_glu", "attn_out_g", "ssm_out_g", "w_out",
            "norm2_g", "w_ff1", "w_ff2", "final_g")
_PACK_ALIGN = 128 * LANES


def _pack(parts):
    flat = jnp.concatenate([p.reshape(-1).astype(F32) for p in parts])
    pad = (-flat.shape[0]) % _PACK_ALIGN
    return jnp.pad(flat, (0, pad)).reshape(-1, LANES)


def kernel(x, c, w_ada, b_ada, norm1_g, w_in, sinks, ssm_lam_re, ssm_lam_im, ssm_log_step, ssm_b_re, ssm_b_im, ssm_c_re, ssm_c_im, ssm_d, w_glu, b_glu, attn_out_g, ssm_out_g, w_out, norm2_g, w_ff1, w_ff2, final_g, loss_target, m_w_ada, m_b_ada, m_norm1_g, m_w_in, m_sinks, m_ssm_lam_re, m_ssm_lam_im, m_ssm_log_step, m_ssm_b_re, m_ssm_b_im, m_ssm_c_re, m_ssm_c_im, m_ssm_d, m_w_glu, m_b_glu, m_attn_out_g, m_ssm_out_g, m_w_out, m_norm2_g, m_w_ff1, m_w_ff2, m_final_g, v_w_ada, v_b_ada, v_norm1_g, v_w_in, v_sinks, v_ssm_lam_re, v_ssm_lam_im, v_ssm_log_step, v_ssm_b_re, v_ssm_b_im, v_ssm_c_re, v_ssm_c_im, v_ssm_d, v_w_glu, v_b_glu, v_attn_out_g, v_ssm_out_g, v_w_out, v_norm2_g, v_w_ff1, v_w_ff2, v_final_g):
    args = dict(locals())
    weights = {n: args[n] for n in _WEIGHTS}
    mom = {n: args["m_" + n] for n in _WEIGHTS}
    var = {n: args["v_" + n] for n in _WEIGHTS}
    me = 4 * lax.axis_index("x") + 2 * lax.axis_index("y") + lax.axis_index("c")

    _, s, d = x.shape
    xs, tgt = x[0], loss_target[0]
    d_ssm = ssm_d.shape[-1]
    d_attn = d - d_ssm
    nq = d_attn // HEAD_DIM
    d_kv = (nq // Q_PER_KV) * HEAD_DIM
    p_state = ssm_b_re.shape[2]

    c_all, g_in = _run_plan("gather_c_w_in", _Gather([c, w_in[0].T.astype(BF16)]))
    c_all = c_all.reshape(N_DEV, d)
    w_in_t = g_in.reshape(-1, d)

    n_loc = w_ada.shape[-1]
    b_loc = lax.dynamic_slice_in_dim(b_ada, me * n_loc, n_loc, axis=1)
    silu = lambda t: t * _sigmoid(t)
    mod_part = _matmul("ada_mod", c_all, w_ada[0], "nn", [F32], a_pro=silu, vecs=[b_loc], exact=True,
                       epilogue=lambda acc, e, v: (acc + v[0],), tn=512, tk=d)
    mod_all = _run_plan("gather_mod", _Gather([mod_part]))[0]
    mod = lax.dynamic_index_in_dim(mod_all, me, axis=1, keepdims=False).reshape(N_MOD, 1, d)
    shift1, scale1, gate1, shift2, scale2, gate2 = [mod[i] for i in range(N_MOD)]

    h1 = _norm_mod_fwd("norm1", xs, norm1_g, scale1, shift1)
    q = _matmul("proj_q", h1, w_in_t, "nt", [F32], b_rows=(0, d_attn))
    kv = _matmul("proj_kv", h1, w_in_t, "nt", [F32], b_rows=(d_attn, 2 * d_kv))
    u = _matmul("proj_u", h1, w_in_t, "nt", [F32], b_rows=(d_attn + 2 * d_kv, d_ssm))
    k, v = kv[:, :d_kv], kv[:, d_kv:]

    half = HEAD_DIM // 2
    inv_freq = ROPE_THETA ** (-jnp.arange(half, dtype=F32) / half)
    ang = jnp.arange(s, dtype=F32)[:, None] * inv_freq[None, :]
    cos_t, sin_t = jnp.tile(jnp.cos(ang), (1, 4)), jnp.tile(jnp.sin(ang), (1, 4))
    (attn, lse), (g_glu, g_out) = _attn_fwd("attn_fwd", q, k, v, cos_t, sin_t, sinks,
                                            plan=_Gather([w_glu[0].astype(BF16), w_out[0].astype(BF16)]))
    w_glu_f = g_glu.reshape(d_ssm, d_ssm)
    w_out_f = g_out.reshape(d, d)

    ssm_params = (ssm_lam_re[0], ssm_lam_im[0], ssm_log_step[0], ssm_b_re[0], ssm_b_im[0], ssm_c_re[0],
                  ssm_c_im[0], ssm_d[0])
    (tm_op, em_op, fm_op, alr, ali), ssm_vjp = jax.vjp(_ssm_operators, *ssm_params)
    acat, bcat = _decay_lanes(alr, ali)
    (y_ssm, x_prev, u_chunks), (g_ff1,) = _ssm_fwd("ssm_fwd", u, tm_op, em_op, fm_op, acat, bcat,
                                                   plan=_Gather([w_ff1[0].astype(BF16)]))
    yg = _gelu_fwd("gelu", y_ssm)
    ssm_out, z_glu = _matmul(
        "glu", yg, w_glu_f, "nn", [F32, F32], extras=[y_ssm], vecs=[b_glu],
        epilogue=lambda acc, e, v: (_gelu(e[0]) * _sigmoid(acc + v[0]), acc + v[0]))
    mixed = _group_norm_fwd("group_norm", attn, ssm_out, attn_out_g, ssm_out_g)
    x2, mo = _matmul("out_proj", mixed, w_out_f, "nn", [F32, BF16], extras=[xs], vecs=[gate1],
                     epilogue=lambda acc, e, v: (e[0] + v[0] * acc, acc))

    h2 = _norm_mod_fwd("norm2", x2, norm2_g, scale2, shift2)
    (a_ff, f_ff), (g_ff2,) = _matmul("ff1", h2, g_ff1, "nn", [BF16, BF16], b_blocked=True,
                                     epilogue=lambda acc, e, v: (acc, jnp.square(jnp.maximum(acc, 0.0))),
                                     plan=_Gather([w_ff2[0].astype(BF16)]))
    w_ff2_f = g_ff2.reshape(-1, d)
    x3, ff = _matmul("ff2", f_ff, w_ff2_f, "nn", [F32, BF16], extras=[x2], vecs=[gate2],
                     epilogue=lambda acc, e, v: (e[0] + v[0] * acc, acc))

    dx3, dff, loss_local, d_final_g, d_gate2 = _final_loss("final_loss", x3, tgt, final_g.reshape(1, d), ff, gate2)
    loss = lax.psum(loss_local, MESH_AXES)

    dw_ff2 = _matmul("ff2_dw", f_ff, dff, "tn", [BF16]).reshape(N_DEV, -1, d)
    da_ff, (p_ff2,) = _matmul("ff2_dx", dff, w_ff2_f, "nt", [BF16], extras=[a_ff],
                              epilogue=lambda acc, e, v: (acc * (2.0 * jnp.maximum(e[0].astype(F32), 0.0)),),
                              plan=_PairSwap([dw_ff2]))
    s_ff2 = _pair_add("pair_add_ff2", dw_ff2, p_ff2)
    dw_ff1, (r_ff2_a,) = _matmul("ff1_dw", h2, da_ff, "tn", [BF16], out_blocked=N_DEV,
                                 plan=_ChipExchange([s_ff2], (CHIP_X, CHIP_Y)))
    dh2, (r_ff2_b,) = _matmul("ff1_dx", da_ff, g_ff1, "nt", [F32], b_blocked=True,
                              plan=_ChipExchange([s_ff2], (CHIP_DIAGONAL,)))
    (dx2, dmo, d_scale2, d_shift2, d_norm2_g, d_gate1), (p_ff1,) = _norm_mod_bwd(
        "norm2_bwd", x2, dh2, dx3, norm2_g, scale2, gated=(mo, gate1), plan=_PairSwap([dw_ff1]))
    s_ff1 = _pair_add("pair_add_ff1", dw_ff1, p_ff1)

    dw_out = _matmul("out_dw", mixed, dmo, "tn", [BF16]).reshape(N_DEV, -1, d)
    dmixed, (r_out_a,) = _matmul("out_dx", dmo, w_out_f, "nt", [F32], plan=_Exchange([dw_out], (1, 4, 2)))
    (dattn, dssm_out, d_attn_g, d_ssm_g), (r_out_b,) = _group_norm_bwd(
        "group_norm_bwd", attn, ssm_out, dmixed, attn_out_g, ssm_out_g, plan=_Exchange([dw_out], (6,)))

    dz, dyg_direct, d_b_glu = _glu_bwd("glu_bwd", dssm_out, y_ssm, z_glu)
    dw_glu = _matmul("glu_dw", yg, dz, "tn", [BF16]).reshape(N_DEV, -1, d_ssm)
    dy_ssm, (r_glu,) = _matmul("glu_dx", dz, w_glu_f, "nt", [F32], extras=[dyg_direct, y_ssm],
                               epilogue=lambda acc, e, v: ((acc + e[0]) * _gelu_grad(e[1]),),
                               plan=_Exchange([dw_glu], RELATIONS_ALL))
    (du, d_tm, d_em, d_fm, r1, r2), (r_ff1_a,) = _ssm_bwd(
        "ssm_bwd", u_chunks, dy_ssm, x_prev, tm_op, em_op, fm_op, acat, bcat,
        plan=_ChipExchange([s_ff1], (CHIP_X, CHIP_Y)))
    d_alr = r1[:, :p_state] + r1[:, p_state:]
    d_ali = r2[:, p_state:] - r2[:, :p_state]
    d_ssm_params = ssm_vjp((d_tm, d_em, d_fm, d_alr, d_ali))

    (dq, dk, dv, d_sinks), (r_ff1_b,) = _attn_bwd("attn_bwd", q, k, v, cos_t, sin_t, sinks, attn, lse, dattn,
                                                  plan=_ChipExchange([s_ff1], (CHIP_DIAGONAL,)))
    dproj = jnp.concatenate([dq, dk, dv, du], axis=1).astype(BF16)
    dw_in_t, (r_out_c,) = _matmul("in_dw", dproj, h1, "tn", [BF16], plan=_Exchange([dw_out], (5, 3, 7)))
    dw_in_t = dw_in_t.reshape(N_DEV, -1, d)
    dh1, (r_in_a,) = _matmul("in_dx", dproj, w_in_t, "nn", [F32], plan=_Exchange([dw_in_t], RELATIONS_SAME_CORE))
    (grad_x, d_scale1, d_shift1, d_norm1_g), _ = _norm_mod_bwd("norm1_bwd", xs, dh1, dx2, norm1_g, scale1)

    d_mod = jnp.concatenate([d_shift1, d_scale1, d_gate1, d_shift2, d_scale2, d_gate2])
    small_g = dict(zip(("ssm_lam_re", "ssm_lam_im", "ssm_log_step", "ssm_b_re", "ssm_b_im", "ssm_c_re", "ssm_c_im",
                        "ssm_d"), d_ssm_params, strict=True))
    small_g.update(b_ada=d_mod, norm1_g=d_norm1_g, sinks=d_sinks, b_glu=d_b_glu, attn_out_g=d_attn_g,
                   ssm_out_g=d_ssm_g, norm2_g=d_norm2_g, final_g=d_final_g)
    small_parts, r_in_b = _run_plan("gather_small_grads", _Plans([_Gather([_pack([small_g[n] for n in _SMALL])]),
                                                                  _Exchange([dw_in_t], RELATIONS_OTHER_CORE)]))
    small = _adam_shard("adam_small", [small_parts], _pack([weights[n] for n in _SMALL]),
                        _pack([mom[n] for n in _SMALL]), _pack([var[n] for n in _SMALL]))
    out = {}
    off = 0
    for n in _SMALL:
        size = weights[n].size
        out[n] = [t.reshape(-1)[off:off + size].reshape(weights[n].shape) for t in small]
        off += size

    dmod_all = small_parts.reshape(N_DEV, -1)[:, :N_MOD * d]
    dmod_loc = lax.dynamic_slice_in_dim(dmod_all, me * n_loc, n_loc, axis=1)
    c_act_t = silu(c_all).T
    out["w_ada"] = [t[None] for t in _ada_update("adam_w_ada", c_act_t, dmod_loc, w_ada[0], m_w_ada[0], v_w_ada[0])]

    mine = lambda blocks: lax.dynamic_index_in_dim(blocks, me, axis=0, keepdims=False)
    in_parts = [mine(dw_in_t).T] + [r.transpose(0, 2, 1) for r in (r_in_a, r_in_b)]
    my_chip = 2 * lax.axis_index("x") + lax.axis_index("y")
    chip_sum = lambda sums: lax.dynamic_index_in_dim(sums, my_chip, axis=0, keepdims=False)
    received = dict(w_in=in_parts, w_glu=[mine(dw_glu), r_glu], w_out=[mine(dw_out), r_out_a, r_out_b, r_out_c],
                    w_ff1=[chip_sum(s_ff1), r_ff1_a, r_ff1_b], w_ff2=[chip_sum(s_ff2), r_ff2_a, r_ff2_b])
    for n, parts in received.items():
        out[n] = [t[None] for t in _adam_shard("adam_" + n, parts, weights[n][0], mom[n][0], var[n][0])]

    return (loss, grad_x[None], *[out[n][0] for n in _WEIGHTS], *[out[n][1] for n in _WEIGHTS],
            *[out[n][2] for n in _WEIGHTS], *[out[n][3] for n in _WEIGHTS])
```

```python
import math

import jax
import jax.numpy as jnp
from jax import lax
from jax.experimental import pallas as pl
from jax.experimental.pallas import tpu as pltpu

F32, BF16 = jnp.float32, jnp.bfloat16
SDS = jax.ShapeDtypeStruct
MESH_AXES = ("x", "y", "c")
N_DEV = 8
VMEM_LIMIT_BYTES = 56 * 1024 * 1024
MATMUL_VMEM_BUDGET = 44 * 1024 * 1024
SUBLANES, LANES = 8, 128

HEAD_DIM = 64
Q_PER_KV = 8
WINDOW = 128
ROPE_THETA = 10000.0
EPS = 1e-6
N_MOD = 6
SSM_CHUNK = 16
SSM_GROUPS_PER_STEP = 8

ADAM_LR, ADAM_B1, ADAM_B2, ADAM_EPS, ADAM_WD, ADAM_STEP = 0.001, 0.9, 0.999, 1e-08, 0.01, 10
HIGHEST = lax.Precision.HIGHEST
SSM_PRECISION = lax.Precision.HIGH

RELATIONS_ALL = (1, 4, 2, 6, 5, 3, 7)
RELATIONS_SAME_CORE = (1, 4, 2, 6)
RELATIONS_OTHER_CORE = (5, 3, 7)
PLAN_MIDDLE = 0.65


def _cparams(sem):
    return pltpu.CompilerParams(dimension_semantics=sem, vmem_limit_bytes=VMEM_LIMIT_BYTES)


def _block_index(p):
    return 4 * p[0] + 2 * p[1] + p[2]


def _me():
    return lax.axis_index("x"), lax.axis_index("y"), lax.axis_index("c")


class _Plan:
    def middle(self, ins, outs, send, recv, local):
        pass


class _Gather(_Plan):
    TO_SIBLING, TO_X, TO_Y, RELAY, PASS_X, PASS_Y, PASS_DIAGONAL = range(7)

    def __init__(self, arrs):
        self.ins = list(arrs)
        self.out_shapes = [SDS((N_DEV,) + a.shape, a.dtype) for a in arrs]
        self.n_rdma, self.n_local = 7 * len(arrs), len(arrs)
        self.rdma_base = self.local_base = 0

    def _copy(self, ins, outs, send, recv, a, k, block, to, from_input=False):
        dst = outs[a].at[_block_index(block)]
        sem = self.rdma_base + a * 7 + k
        return pltpu.make_async_remote_copy(
            src_ref=ins[a] if from_input else dst, dst_ref=dst, send_sem=send.at[sem], recv_sem=recv.at[sem],
            device_id=to, device_id_type=pl.DeviceIdType.MESH)

    @staticmethod
    def _places():
        x, y, c = _me()
        return (x, y, c), (x, y, 1 - c), (1 - x, y, c), (x, 1 - y, c), (1 - x, 1 - y, c)

    def _first(self, ins, outs, send, recv, a):
        me, sibling, x_nbr, y_nbr, _ = self._places()
        return [self._copy(ins, outs, send, recv, a, k, me, to, True)
                for k, to in ((self.TO_SIBLING, sibling), (self.TO_X, x_nbr), (self.TO_Y, y_nbr))]

    def _mine(self, ins, outs, local, a):
        return pltpu.make_async_copy(ins[a], outs[a].at[_block_index(_me())], local.at[self.local_base + a])

    def start(self, ins, outs, send, recv, local):
        for a in range(len(ins)):
            self._mine(ins, outs, local, a).start()
            for cp in self._first(ins, outs, send, recv, a):
                cp.start()

    def middle(self, ins, outs, send, recv, local):
        me, sibling, x_nbr, y_nbr, _ = self._places()
        core = me[2]
        for a in range(len(ins)):
            self._copy(ins, outs, send, recv, a, self.TO_X, x_nbr, me).wait_recv()
            self._copy(ins, outs, send, recv, a, self.TO_Y, y_nbr, me).wait_recv()

            @pl.when(core == 0)
            def _():
                self._copy(ins, outs, send, recv, a, self.RELAY, x_nbr, y_nbr).start()

            @pl.when(core == 1)
            def _():
                self._copy(ins, outs, send, recv, a, self.RELAY, y_nbr, x_nbr).start()

            self._copy(ins, outs, send, recv, a, self.PASS_X, x_nbr, sibling).start()
            self._copy(ins, outs, send, recv, a, self.PASS_Y, y_nbr, sibling).start()

    def finish(self, ins, outs, send, recv, local):
        me, sibling, x_nbr, y_nbr, diagonal = self._places()
        other = lambda p: (p[0], p[1], 1 - p[2])
        for a in range(len(ins)):
            self._copy(ins, outs, send, recv, a, self.RELAY, diagonal, me).wait_recv()
            self._copy(ins, outs, send, recv, a, self.PASS_DIAGONAL, diagonal, sibling).start()
        for a in range(len(ins)):
            self._copy(ins, outs, send, recv, a, self.TO_SIBLING, sibling, me).wait_recv()
            for k, src in ((self.PASS_X, x_nbr), (self.PASS_Y, y_nbr), (self.PASS_DIAGONAL, diagonal)):
                self._copy(ins, outs, send, recv, a, k, other(src), me).wait_recv()
                self._copy(ins, outs, send, recv, a, k, src, sibling).wait_send()
            for cp in self._first(ins, outs, send, recv, a):
                cp.wait_send()
            self._copy(ins, outs, send, recv, a, self.RELAY, me, me).wait_send()
            self._mine(ins, outs, local, a).wait()


class _Exchange(_Plan):
    def __init__(self, arrs, relations):
        self.ins, self.relations = list(arrs), tuple(relations)
        self.out_shapes = [SDS((len(relations),) + a.shape[1:], a.dtype) for a in arrs]
        self.n_rdma, self.n_local = len(relations) * len(arrs), 0
        self.rdma_base = self.local_base = 0

    def _copies(self, ins, outs, send, recv):
        x, y, c = _me()
        cps = []
        for a in range(len(ins)):
            for s, k in enumerate(self.relations):
                peer = ((1 - x) if (k & 4) else x, (1 - y) if (k & 2) else y, (1 - c) if (k & 1) else c)
                sem = self.rdma_base + a * len(self.relations) + s
                cps.append(pltpu.make_async_remote_copy(
                    src_ref=ins[a].at[_block_index(peer)], dst_ref=outs[a].at[s], send_sem=send.at[sem],
                    recv_sem=recv.at[sem], device_id=peer, device_id_type=pl.DeviceIdType.MESH))
        return cps

    def start(self, ins, outs, send, recv, local):
        for cp in self._copies(ins, outs, send, recv):
            cp.start()

    def finish(self, ins, outs, send, recv, local):
        for cp in self._copies(ins, outs, send, recv):
            cp.wait()


class _PairSwap(_Plan):
    def __init__(self, arrs):
        self.ins = list(arrs)
        self.out_shapes = [SDS((4,) + a.shape[1:], a.dtype) for a in arrs]
        self.n_rdma, self.n_local = 4 * len(arrs), 0
        self.rdma_base = self.local_base = 0

    def _copies(self, ins, outs, send, recv):
        x, y, c = _me()
        cps = []
        for a in range(len(ins)):
            for s in range(4):
                sem = self.rdma_base + a * 4 + s
                cps.append(pltpu.make_async_remote_copy(
                    src_ref=ins[a].at[2 * s + (1 - c)], dst_ref=outs[a].at[s], send_sem=send.at[sem],
                    recv_sem=recv.at[sem], device_id=(x, y, 1 - c), device_id_type=pl.DeviceIdType.MESH))
        return cps

    def start(self, ins, outs, send, recv, local):
        for cp in self._copies(ins, outs, send, recv):
            cp.start()

    def finish(self, ins, outs, send, recv, local):
        for cp in self._copies(ins, outs, send, recv):
            cp.wait()


CHIP_X, CHIP_Y, CHIP_DIAGONAL = (1, 0), (0, 1), (1, 1)


class _ChipExchange(_Plan):
    def __init__(self, arrs, hops):
        self.ins, self.hops = list(arrs), tuple(hops)
        self.out_shapes = [SDS((len(hops),) + a.shape[1:], a.dtype) for a in arrs]
        self.n_rdma, self.n_local = len(hops) * len(arrs), 0
        self.rdma_base = self.local_base = 0

    def _copies(self, ins, outs, send, recv):
        x, y, c = _me()
        cps = []
        for a in range(len(ins)):
            for s, (fx, fy) in enumerate(self.hops):
                px, py = (1 - x) if fx else x, (1 - y) if fy else y
                sem = self.rdma_base + a * len(self.hops) + s
                cps.append(pltpu.make_async_remote_copy(
                    src_ref=ins[a].at[2 * px + py], dst_ref=outs[a].at[s], send_sem=send.at[sem],
                    recv_sem=recv.at[sem], device_id=(px, py, c), device_id_type=pl.DeviceIdType.MESH))
        return cps

    def start(self, ins, outs, send, recv, local):
        for cp in self._copies(ins, outs, send, recv):
            cp.start()

    def finish(self, ins, outs, send, recv, local):
        for cp in self._copies(ins, outs, send, recv):
            cp.wait()


class _Plans:
    def __init__(self, plans):
        self.plans = list(plans)
        self.ins = [a for p in plans for a in p.ins]
        self.out_shapes = [s for p in plans for s in p.out_shapes]
        self.n_rdma = self.n_local = 0
        for p in plans:
            p.rdma_base, p.local_base = self.n_rdma, self.n_local
            self.n_rdma, self.n_local = self.n_rdma + p.n_rdma, self.n_local + p.n_local

    def _each(self, ins, outs):
        i = o = 0
        for p in self.plans:
            yield p, ins[i:i + len(p.ins)], outs[o:o + len(p.out_shapes)]
            i, o = i + len(p.ins), o + len(p.out_shapes)

    def start(self, ins, outs, send, recv, local):
        for p, pi, po in self._each(ins, outs):
            p.start(pi, po, send, recv, local)

    def middle(self, ins, outs, send, recv, local):
        for p, pi, po in self._each(ins, outs):
            p.middle(pi, po, send, recv, local)

    def finish(self, ins, outs, send, recv, local):
        for p, pi, po in self._each(ins, outs):
            p.finish(pi, po, send, recv, local)


def _plan_scratch(plan):
    return [pltpu.SemaphoreType.DMA((plan.n_rdma,)), pltpu.SemaphoreType.DMA((plan.n_rdma,)),
            pltpu.SemaphoreType.DMA((max(plan.n_local, 1),))]


def _run_plan(name, plan):
    n = len(plan.ins)

    def body(*refs):
        ins, outs, sems = refs[:n], refs[n:len(refs) - 3], refs[len(refs) - 3:]
        plan.start(ins, outs, *sems)
        plan.middle(ins, outs, *sems)
        plan.finish(ins, outs, *sems)

    any_spec = pl.BlockSpec(memory_space=pl.ANY)
    return pl.pallas_call(body, name=name, out_shape=list(plan.out_shapes), in_specs=[any_spec] * n,
                          out_specs=[any_spec] * len(plan.out_shapes), scratch_shapes=_plan_scratch(plan))(*plan.ins)


def _pcall(name, body, grid, in_specs, ins, out_specs, out_shape, scratch=(), semantics=None, plan=None):
    if plan is None:
        return pl.pallas_call(body, name=name, grid=grid, in_specs=list(in_specs), out_specs=list(out_specs),
                              out_shape=list(out_shape), scratch_shapes=list(scratch),
                              compiler_params=_cparams(semantics))(*ins)
    n_in, n_out, n_scr = len(ins), len(out_shape), len(scratch)
    p_in, p_out = len(plan.ins), len(plan.out_shapes)

    def with_plan(*refs):
        k_in, c_in = refs[:n_in], refs[n_in:n_in + p_in]
        refs = refs[n_in + p_in:]
        k_out, c_out = refs[:n_out], refs[n_out:n_out + p_out]
        refs = refs[n_out + p_out:]
        k_scr, sems = refs[:n_scr], refs[n_scr:]
        step = 0
        for d, g in enumerate(grid):
            step = step * g + pl.program_id(d)
        n_steps = math.prod(grid)

        @pl.when(step == 0)
        def _():
            plan.start(c_in, c_out, *sems)

        @pl.when(step == min(n_steps - 1, int(n_steps * PLAN_MIDDLE)))
        def _():
            plan.middle(c_in, c_out, *sems)

        body(*k_in, *k_out, *k_scr)

        @pl.when(step == n_steps - 1)
        def _():
            plan.finish(c_in, c_out, *sems)

    any_spec = pl.BlockSpec(memory_space=pl.ANY)
    res = pl.pallas_call(
        with_plan, name=name, grid=grid, in_specs=list(in_specs) + [any_spec] * p_in,
        out_specs=list(out_specs) + [any_spec] * p_out, out_shape=list(out_shape) + list(plan.out_shapes),
        scratch_shapes=list(scratch) + _plan_scratch(plan),
        compiler_params=_cparams(("arbitrary",) * len(grid)))(*ins, *plan.ins)
    return res[:n_out], res[n_out:]


def _rowwise(name, fn, rows, vecs, row_outs, acc_outs=(), tm=128, plan=None):
    t = rows[0].shape[0]
    tm = min(tm, t)
    assert t % tm == 0 and tm % SUBLANES == 0
    n_r, n_v, n_o = len(rows), len(vecs), len(row_outs)

    def body(*refs):
        r_in, v_in = refs[:n_r], refs[n_r:n_r + n_v]
        r_out, a_out = refs[n_r + n_v:n_r + n_v + n_o], refs[n_r + n_v + n_o:]
        outs, accs = fn([r[...] for r in r_in], [v[...] for v in v_in])
        for o_ref, o in zip(r_out, outs, strict=True):
            o_ref[...] = o.astype(o_ref.dtype)
        if a_out:
            @pl.when(pl.program_id(0) == 0)
            def _():
                for a_ref in a_out:
                    a_ref[...] = jnp.zeros_like(a_ref)
            for a_ref, a in zip(a_out, accs, strict=True):
                a_ref[...] += a.reshape(tm // SUBLANES, SUBLANES, a.shape[-1]).sum(axis=0)

    in_specs = [pl.BlockSpec((tm, r.shape[1]), lambda i: (i, 0)) for r in rows]
    in_specs += [pl.BlockSpec(v.shape, lambda i: (0, 0)) for v in vecs]
    out_specs = [pl.BlockSpec((tm, w), lambda i: (i, 0)) for w, _ in row_outs]
    out_specs += [pl.BlockSpec((SUBLANES, w), lambda i: (0, 0)) for w in acc_outs]
    out_shape = [SDS((t, w), dt) for w, dt in row_outs] + [SDS((SUBLANES, w), F32) for w in acc_outs]
    return _pcall(name, body, (t // tm,), in_specs, [*rows, *vecs], out_specs, out_shape, semantics=("arbitrary",),
                  plan=plan)


def _tile(n, want):
    if n <= want:
        return n
    for t in range(want // LANES * LANES, 0, -LANES):
        if n % t == 0:
            return t
    raise ValueError(f"no tile for {n}")


_DOT_DIMS = {"nn": (((1,), (0,)), ((), ())), "nt": (((1,), (1,)), ((), ())), "tn": (((0,), (0,)), ((), ()))}


def _matmul(name, a, b, mode, out_dtypes, epilogue=None, extras=(), vecs=(), a_pro=None,
            tm=1024, tn=512, tk=4096, exact=False, b_blocked=False, out_blocked=0, b_rows=None, plan=None):
    cs = b.shape[-1] if b_blocked else None
    b2 = (b.shape[1], b.shape[0] * b.shape[2]) if b_blocked else b.shape
    if mode == "tn":
        (k, m), (k2, n) = a.shape, b2
    elif mode == "nt":
        (m, k), (n, k2) = a.shape, b2
    else:
        (m, k), (k2, n) = a.shape, b2
    assert k == k2 and not (b_blocked and mode == "tn")
    row0 = 0
    if b_rows is not None:
        assert mode == "nt" and not b_blocked
        row0, n = b_rows
        tn = _tile(math.gcd(n, row0) if row0 else n, tn)
    tm, tn, tk = _tile(m, tm), _tile(n, tn), _tile(k, tk)
    if b_blocked and mode == "nn":
        tn = _tile(cs, tn)
    if b_blocked and mode == "nt":
        tk = _tile(cs, tk)
    if out_blocked:
        tn = _tile(n // out_blocked, tn)
    nk = k // tk

    def vmem_bytes(width):
        operands = 2 * (tm * tk * a.dtype.itemsize + tk * width * b.dtype.itemsize)
        tiles = 2 * tm * width * (sum(jnp.dtype(dt).itemsize for dt in out_dtypes) + sum(e.dtype.itemsize for e in extras))
        return operands + tiles + tm * width * 4 * (2 if nk > 1 else 1)

    extent = cs if (b_blocked and mode == "nn") else n // out_blocked if out_blocked else n
    if extent % (2 * tn) == 0 and row0 % (2 * tn) == 0 and vmem_bytes(2 * tn) <= MATMUL_VMEM_BUDGET:
        tn *= 2
    n_e, n_v, n_o = len(extras), len(vecs), len(out_dtypes)
    precision = HIGHEST if exact else None

    def body(*refs):
        a_ref, b_ref = refs[:2]
        e_refs, v_refs = refs[2:2 + n_e], refs[2 + n_e:2 + n_e + n_v]
        o_refs = refs[2 + n_e + n_v:2 + n_e + n_v + n_o]

        def product():
            av = a_ref[...]
            if a_pro is not None:
                av = a_pro(av)
            return lax.dot_general(av, b_ref[...], _DOT_DIMS[mode], precision=precision, preferred_element_type=F32)

        def finish(acc):
            res = (acc,) if epilogue is None else epilogue(acc, [e[...] for e in e_refs], [v[...] for v in v_refs])
            for o_ref, r in zip(o_refs, res, strict=True):
                o_ref[...] = r.astype(o_ref.dtype)

        if nk == 1:
            finish(product())
            return
        acc_ref = refs[-1]
        kk = pl.program_id(2)

        @pl.when(kk == 0)
        def _():
            acc_ref[...] = product()

        @pl.when(kk > 0)
        def _():
            acc_ref[...] += product()

        @pl.when(kk == nk - 1)
        def _():
            finish(acc_ref[...])

    if mode == "tn":
        a_spec = pl.BlockSpec((tk, tm), lambda i, j, kk: (kk, i))
    else:
        a_spec = pl.BlockSpec((tm, tk), lambda i, j, kk: (i, kk))
    if b_blocked and mode == "nn":
        per = cs // tn
        b_spec = pl.BlockSpec((None, tk, tn), lambda i, j, kk: (j // per, kk, j % per))
    elif b_blocked:
        per = cs // tk
        b_spec = pl.BlockSpec((None, tn, tk), lambda i, j, kk: (kk // per, j, kk % per))
    elif mode == "nt":
        assert row0 % tn == 0
        b_spec = pl.BlockSpec((tn, tk), lambda i, j, kk: (j + row0 // tn, kk))
    else:
        b_spec = pl.BlockSpec((tk, tn), lambda i, j, kk: (kk, j))
    tile = pl.BlockSpec((tm, tn), lambda i, j, kk: (i, j))
    if out_blocked:
        per_o = n // out_blocked // tn
        out_spec = pl.BlockSpec((None, tm, tn), lambda i, j, kk: (j // per_o, i, j % per_o))
        out_shape = [SDS((out_blocked, m, n // out_blocked), dt) for dt in out_dtypes]
    else:
        out_spec, out_shape = tile, [SDS((m, n), dt) for dt in out_dtypes]
    in_specs = [a_spec, b_spec] + [tile] * n_e + [pl.BlockSpec((1, tn), lambda i, j, kk: (0, j))] * n_v
    res = _pcall(name, body, (m // tm, n // tn, nk), in_specs, [a, b, *extras, *vecs], [out_spec] * n_o, out_shape,
                 scratch=[pltpu.VMEM((tm, tn), F32)] if nk > 1 else [],
                 semantics=("parallel", "parallel", "arbitrary"), plan=plan)
    if plan is None:
        return res[0] if n_o == 1 else res
    return (res[0][0] if n_o == 1 else res[0]), res[1]


def _rms_fwd(x):
    r = lax.rsqrt(jnp.mean(x * x, axis=-1, keepdims=True) + EPS)
    return x * r, r


def _rms_bwd(dxn, xn, r):
    return r * (dxn - xn * jnp.mean(dxn * xn, axis=-1, keepdims=True))


_INV_SQRT2 = 1.0 / math.sqrt(2.0)
_INV_SQRT2PI = 1.0 / math.sqrt(2.0 * math.pi)


def _gelu(y):
    return 0.5 * y * (1.0 + lax.erf(y * _INV_SQRT2))


def _gelu_grad(y):
    return 0.5 * (1.0 + lax.erf(y * _INV_SQRT2)) + y * (_INV_SQRT2PI * jnp.exp(-0.5 * y * y))


def _sigmoid(z):
    return 1.0 / (1.0 + jnp.exp(-z))


def _adam_math(w, g, m, v):
    m = ADAM_B1 * m + (1.0 - ADAM_B1) * g
    v = ADAM_B2 * v + (1.0 - ADAM_B2) * (g * g)
    m_hat = m / (1.0 - ADAM_B1 ** ADAM_STEP)
    v_hat = v / (1.0 - ADAM_B2 ** ADAM_STEP)
    delta = -ADAM_LR * (m_hat / (jnp.sqrt(v_hat) + ADAM_EPS) + ADAM_WD * w)
    return delta, m, v


def _norm_mod_fwd(name, x, g, scale, shift):
    def fn(rows, vecs):
        (xv,), (gv, sc, sh) = rows, vecs
        xn, _ = _rms_fwd(xv)
        return [(xn * gv) * (1.0 + sc) + sh], []
    return _rowwise(name, fn, [x], [g, scale, shift], [(x.shape[1], BF16)])[0]


def _norm_mod_bwd(name, x, dh, dres, g, scale, gated=None, plan=None):
    d = x.shape[1]

    def fn(rows, vecs):
        xv, dhv, drv = rows[:3]
        gv, sc = vecs[:2]
        xn, r = _rms_fwd(xv)
        t = xn * gv
        dt = dhv * (1.0 + sc)
        dx = drv + _rms_bwd(dt * gv, xn, r)
        if gated is None:
            return [dx], [dhv * t, dhv, dt * xn]
        return [dx, dx * vecs[2]], [dhv * t, dhv, dt * xn, dx * rows[3].astype(F32)]
    extra_rows, extra_vecs = ([gated[0]], [gated[1]]) if gated is not None else ([], [])
    res = _rowwise(name, fn, [x, dh, dres] + extra_rows, [g, scale] + extra_vecs,
                   [(d, F32)] + [(d, BF16)] * len(extra_rows), [d] * (3 + len(extra_rows)), plan=plan)
    outs, rest = res if plan is not None else (res, None)
    n_rows = 1 + len(extra_rows)
    return (*outs[:n_rows], *[a.sum(0) for a in outs[n_rows:]]), rest


def _final_loss(name, x, tgt, g, val, gate):
    d = x.shape[1]

    def fn(rows, vecs):
        (xv, tv, vv), (gv, gate_v) = rows, vecs
        xn, r = _rms_fwd(xv)
        e = xn * gv - tv
        dy = e * (1.0 / d)
        dx = _rms_bwd(dy * gv, xn, r)
        return [dx, dx * gate_v], [e * e, dy * xn, dx * vv.astype(F32)]
    dx, dval, sq, dg, dgate = _rowwise(name, fn, [x, tgt, val], [g, gate], [(d, F32), (d, BF16)], [d, d, d])
    return dx, dval, 0.5 * jnp.sum(sq) / d, dg.sum(0), dgate.sum(0)


def _group_norm_fwd(name, attn, ssm, g_a, g_s):
    def fn(rows, vecs):
        (av, sv), (ga, gs) = rows, vecs
        return [jnp.concatenate([_rms_fwd(av)[0] * ga, _rms_fwd(sv)[0] * gs], axis=1)], []
    return _rowwise(name, fn, [attn, ssm], [g_a, g_s], [(attn.shape[1] + ssm.shape[1], BF16)])[0]


def _group_norm_bwd(name, attn, ssm, dmixed, g_a, g_s, plan=None):
    da_w, ds_w = attn.shape[1], ssm.shape[1]

    def fn(rows, vecs):
        (av, sv, dm), (ga, gs) = rows, vecs
        an, ra = _rms_fwd(av)
        sn, rs = _rms_fwd(sv)
        dma, dms = dm[:, :da_w], dm[:, da_w:]
        return [_rms_bwd(dma * ga, an, ra), _rms_bwd(dms * gs, sn, rs)], [dma * an, dms * sn]
    res = _rowwise(name, fn, [attn, ssm, dmixed], [g_a, g_s], [(da_w, F32), (ds_w, F32)], [da_w, ds_w], plan=plan)
    (dattn, dssm, dga, dgs), rest = res if plan is not None else (res, None)
    return (dattn, dssm, dga.sum(0), dgs.sum(0)), rest


def _gelu_fwd(name, y):
    def fn(rows, vecs):
        return [_gelu(rows[0])], []
    return _rowwise(name, fn, [y], [], [(y.shape[1], BF16)])[0]


def _glu_bwd(name, dout, y, z):
    d = y.shape[1]

    def fn(rows, vecs):
        dov, yv, zv = rows
        sg = _sigmoid(zv)
        dz = dov * _gelu(yv) * sg * (1.0 - sg)
        return [dz, dov * sg], [dz]
    dz, dyg, db = _rowwise(name, fn, [dout, y, z], [], [(d, BF16), (d, F32)], [d])
    return dz, dyg, db.sum(0)


def _adam_shard(name, parts, w, m, v):
    r, c = w.shape
    n_parts = sum(1 if p.ndim == 2 else p.shape[0] for p in parts)
    row_bytes = 2 * c * (n_parts * parts[0].dtype.itemsize + 7 * 4)
    tr = min(128, r)
    while tr > SUBLANES and tr * row_bytes > VMEM_LIMIT_BYTES // 2:
        tr //= 2
    assert r % tr == 0
    n_p = len(parts)

    def body(*refs):
        p_refs, (w_ref, m_ref, v_ref, g_out, d_out, m_out, v_out) = refs[:n_p], refs[n_p:]
        g = None
        for p_ref in p_refs:
            terms = [p_ref[...]] if len(p_ref.shape) == 2 else [p_ref[j] for j in range(p_ref.shape[0])]
            for t in terms:
                g = t.astype(F32) if g is None else g + t.astype(F32)
        delta, m_new, v_new = _adam_math(w_ref[...], g, m_ref[...], v_ref[...])
        g_out[...], d_out[...], m_out[...], v_out[...] = g, delta, m_new, v_new

    tile = pl.BlockSpec((tr, c), lambda i: (i, 0))
    p_specs = [tile if p.ndim == 2 else pl.BlockSpec((p.shape[0], tr, c), lambda i: (0, i, 0)) for p in parts]
    return _pcall(name, body, (r // tr,), p_specs + [tile] * 3, [*parts, w, m, v], [tile] * 4, [SDS((r, c), F32)] * 4,
                  semantics=("parallel",))


def _pair_add(name, blocks, from_sibling):
    _, r, c = blocks.shape
    tr = min(256, r)
    assert r % tr == 0

    def body(b0_ref, b1_ref, s_ref, o_ref):
        mine = jnp.where(lax.axis_index("c") == 0, b0_ref[...].astype(F32), b1_ref[...].astype(F32))
        o_ref[...] = (mine + s_ref[...].astype(F32)).astype(o_ref.dtype)

    core_block = lambda k: pl.BlockSpec((None, tr, c), lambda s, i: (2 * s + k, i, 0))
    slot = pl.BlockSpec((None, tr, c), lambda s, i: (s, i, 0))
    return _pcall(name, body, (4, r // tr), [core_block(0), core_block(1), slot], [blocks, blocks, from_sibling],
                  [slot], [SDS((4, r, c), blocks.dtype)], semantics=("parallel", "parallel"))[0]


def _ada_update(name, c_act_t, dmod, w, m, v, tr=128, plan=None):
    r, c = w.shape
    tr = min(tr, r)
    assert r % tr == 0

    def body(c_ref, d_ref, w_ref, m_ref, v_ref, g_out, d_out, m_out, v_out):
        g = jnp.dot(c_ref[...], d_ref[...], precision=lax.Precision.HIGHEST, preferred_element_type=F32)
        delta, m_new, v_new = _adam_math(w_ref[...], g, m_ref[...], v_ref[...])
        g_out[...], d_out[...], m_out[...], v_out[...] = g, delta, m_new, v_new

    tile = pl.BlockSpec((tr, c), lambda i: (i, 0))
    in_specs = [pl.BlockSpec((tr, N_DEV), lambda i: (i, 0)), pl.BlockSpec((N_DEV, c), lambda i: (0, 0)), tile, tile, tile]
    return _pcall(name, body, (r // tr,), in_specs, [c_act_t, dmod, w, m, v], [tile] * 4, [SDS((r, c), F32)] * 4,
                  semantics=("parallel",), plan=plan)


def _rotate_half(x):
    w = x.shape[1]
    half = HEAD_DIM // 2
    lane = lax.broadcasted_iota(jnp.int32, x.shape, 1)
    return jnp.where((lane % HEAD_DIM) < half, -pltpu.roll(x, w - half, 1), pltpu.roll(x, half, 1))


def _lane_tile(tab, w):
    return tab[:, :w] if w <= LANES else jnp.tile(tab, (1, w // LANES))


def _rope(x, cos, sin):
    return x * cos + _rotate_half(x) * sin


def _rope_t(dy, cos, sin):
    return dy * cos - _rotate_half(dy) * sin


def _band_mask(n):
    shape = (Q_PER_KV * WINDOW, 2 * WINDOW)
    i = lax.broadcasted_iota(jnp.int32, shape, 0) & (WINDOW - 1)
    j = lax.broadcasted_iota(jnp.int32, shape, 1)
    return (j > i) & (j <= i + WINDOW) & ((n > 0) | (j >= WINDOW))


def _stack_heads(x, hk):
    first = hk * Q_PER_KV
    return jnp.concatenate([x[:, (first + g) * HEAD_DIM:(first + g + 1) * HEAD_DIM] for g in range(Q_PER_KV)], axis=0)


def _stack_cols(ref, hk):
    first = hk * Q_PER_KV
    return jnp.concatenate([ref[:, first + g:first + g + 1] for g in range(Q_PER_KV)], axis=0)


def _stack_sinks(sink_ref, hk):
    first = hk * Q_PER_KV
    return jnp.concatenate([jnp.broadcast_to(sink_ref[0:1, first + g:first + g + 1], (WINDOW, 1))
                            for g in range(Q_PER_KV)], axis=0)


def _attn_specs(da, dkv, nb):
    cur = lambda n: (jnp.minimum(n, nb - 1), 0)
    prev = lambda n: (jnp.maximum(jnp.minimum(n, nb - 1) - 1, 0), 0)
    return dict(
        q=pl.BlockSpec((WINDOW, da), cur), kv_cur=pl.BlockSpec((WINDOW, dkv), cur),
        kv_prev=pl.BlockSpec((WINDOW, dkv), prev), tab_cur=pl.BlockSpec((WINDOW, LANES), cur),
        tab_prev=pl.BlockSpec((WINDOW, LANES), prev))


def _attn_fwd(name, q, k, v, cos, sin, sinks, plan=None):
    s, da = q.shape
    dkv = k.shape[1]
    nq, nb = da // HEAD_DIM, s // WINDOW
    scale = HEAD_DIM ** -0.5

    def body(q_ref, kp_ref, kc_ref, vp_ref, vc_ref, cc_ref, sc_ref, cp_ref, sp_ref, sink_ref, o_ref, lse_ref):
        n = pl.program_id(0)
        cc, sc, cp, sp = cc_ref[...], sc_ref[...], cp_ref[...], sp_ref[...]
        qr = _rope(q_ref[...], _lane_tile(cc, da), _lane_tile(sc, da)).astype(BF16)
        kk = jnp.concatenate([_rope(kp_ref[...], _lane_tile(cp, dkv), _lane_tile(sp, dkv)),
                              _rope(kc_ref[...], _lane_tile(cc, dkv), _lane_tile(sc, dkv))], axis=0).astype(BF16)
        vv = jnp.concatenate([vp_ref[...], vc_ref[...]], axis=0).astype(BF16)
        valid = _band_mask(n)
        for hk in range(nq // Q_PER_KV):
            ks = slice(hk * HEAD_DIM, (hk + 1) * HEAD_DIM)
            sco = lax.dot_general(_stack_heads(qr, hk), kk[:, ks], _DOT_DIMS["nt"], preferred_element_type=F32) * scale
            sco = jnp.where(valid, sco, -1e30)
            sink = _stack_sinks(sink_ref, hk)
            mx = jnp.maximum(jnp.max(sco, axis=1, keepdims=True), sink)
            p = jnp.exp(sco - mx)
            den = jnp.sum(p, axis=1, keepdims=True) + jnp.exp(sink - mx)
            o8 = jnp.dot((p / den).astype(BF16), vv[:, ks], preferred_element_type=F32)
            lse8 = mx + jnp.log(den)
            for g in range(Q_PER_KV):
                hq, rows = hk * Q_PER_KV + g, slice(g * WINDOW, (g + 1) * WINDOW)
                o_ref[:, hq * HEAD_DIM:(hq + 1) * HEAD_DIM] = o8[rows]
                lse_ref[:, hq:hq + 1] = lse8[rows]

    sp_ = _attn_specs(da, dkv, nb)
    in_specs = [sp_["q"], sp_["kv_prev"], sp_["kv_cur"], sp_["kv_prev"], sp_["kv_cur"],
                sp_["tab_cur"], sp_["tab_cur"], sp_["tab_prev"], sp_["tab_prev"], pl.BlockSpec((1, nq), lambda n: (0, 0))]
    return _pcall(name, body, (nb,), in_specs, [q, k, k, v, v, cos, sin, cos, sin, sinks],
                  [sp_["q"], pl.BlockSpec((WINDOW, nq), lambda n: (n, 0))], [SDS((s, da), F32), SDS((s, nq), F32)],
                  semantics=("arbitrary",), plan=plan)


def _attn_bwd(name, q, k, v, cos, sin, sinks, out, lse, dout, plan=None):
    s, da = q.shape
    dkv = k.shape[1]
    nq, nb = da // HEAD_DIM, s // WINDOW
    scale = HEAD_DIM ** -0.5

    def body(q_ref, kp_ref, kc_ref, vp_ref, vc_ref, cc_ref, sc_ref, cp_ref, sp_ref, sink_ref, o_ref, lse_ref,
             do_ref, dq_ref, dk_ref, dv_ref, dsink_ref, dk_carry, dv_carry):
        n = pl.program_id(0)
        cp, sp = _lane_tile(cp_ref[...], dkv), _lane_tile(sp_ref[...], dkv)

        @pl.when(n == 0)
        def _():
            dk_carry[...] = jnp.zeros_like(dk_carry)
            dv_carry[...] = jnp.zeros_like(dv_carry)
            dsink_ref[...] = jnp.zeros_like(dsink_ref)

        @pl.when(n < nb)
        def _():
            cc, sc = cc_ref[...], sc_ref[...]
            qr = _rope(q_ref[...], _lane_tile(cc, da), _lane_tile(sc, da)).astype(BF16)
            kk = jnp.concatenate([_rope(kp_ref[...], cp, sp),
                                  _rope(kc_ref[...], _lane_tile(cc, dkv), _lane_tile(sc, dkv))], axis=0).astype(BF16)
            vv = jnp.concatenate([vp_ref[...], vc_ref[...]], axis=0).astype(BF16)
            valid = _band_mask(n)
            do_all, o_all = do_ref[...], o_ref[...]
            for hk in range(nq // Q_PER_KV):
                ks = slice(hk * HEAD_DIM, (hk + 1) * HEAD_DIM)
                q8, lse8 = _stack_heads(qr, hk), _stack_cols(lse_ref, hk)
                sco = lax.dot_general(q8, kk[:, ks], _DOT_DIMS["nt"], preferred_element_type=F32) * scale
                probs = jnp.where(valid, jnp.exp(sco - lse8), 0.0)
                do8 = _stack_heads(do_all, hk)
                delta = jnp.sum(do8 * _stack_heads(o_all, hk), axis=1, keepdims=True)
                do8 = do8.astype(BF16)
                dp = lax.dot_general(do8, vv[:, ks], _DOT_DIMS["nt"], preferred_element_type=F32)
                ds = (probs * (dp - delta) * scale).astype(BF16)
                dq8 = jnp.dot(ds, kk[:, ks], preferred_element_type=F32)
                dk_h = lax.dot_general(ds, q8, _DOT_DIMS["tn"], preferred_element_type=F32)
                dv_h = lax.dot_general(probs.astype(BF16), do8, _DOT_DIMS["tn"], preferred_element_type=F32)
                dsink8 = -jnp.exp(_stack_sinks(sink_ref, hk) - lse8) * delta
                for g in range(Q_PER_KV):
                    hq, rows = hk * Q_PER_KV + g, slice(g * WINDOW, (g + 1) * WINDOW)
                    dq_ref[:, hq * HEAD_DIM:(hq + 1) * HEAD_DIM] = dq8[rows]
                    dsink_ref[:, hq:hq + 1] += dsink8[rows].reshape(WINDOW // SUBLANES, SUBLANES, 1).sum(axis=0)
                dk_ref[:, ks] = dk_carry[:, ks] + dk_h[:WINDOW]
                dv_ref[:, ks] = dv_carry[:, ks] + dv_h[:WINDOW]
                dk_carry[:, ks] = dk_h[WINDOW:]
                dv_carry[:, ks] = dv_h[WINDOW:]
            dq_ref[...] = _rope_t(dq_ref[...], _lane_tile(cc, da), _lane_tile(sc, da))
            dk_ref[...] = _rope_t(dk_ref[...], cp, sp)

        @pl.when(n == nb)
        def _():
            dk_ref[...] = _rope_t(dk_carry[...], cp, sp)
            dv_ref[...] = dv_carry[...]

    sp_ = _attn_specs(da, dkv, nb)
    last_prev = lambda n: (jnp.maximum(n - 1, 0), 0)
    tab_prev = pl.BlockSpec((WINDOW, LANES), last_prev)
    kv_out = pl.BlockSpec((WINDOW, dkv), last_prev)
    lse_spec = pl.BlockSpec((WINDOW, nq), lambda n: (jnp.minimum(n, nb - 1), 0))
    in_specs = [sp_["q"], sp_["kv_prev"], sp_["kv_cur"], sp_["kv_prev"], sp_["kv_cur"],
                sp_["tab_cur"], sp_["tab_cur"], tab_prev, tab_prev,
                pl.BlockSpec((1, nq), lambda n: (0, 0)), sp_["q"], lse_spec, sp_["q"]]
    res = _pcall(name, body, (nb + 1,), in_specs, [q, k, k, v, v, cos, sin, cos, sin, sinks, out, lse, dout],
                 [sp_["q"], kv_out, kv_out, pl.BlockSpec((SUBLANES, nq), lambda n: (0, 0))],
                 [SDS((s, da), F32), SDS((s, dkv), F32), SDS((s, dkv), F32), SDS((SUBLANES, nq), F32)],
                 scratch=[pltpu.VMEM((WINDOW, dkv), F32), pltpu.VMEM((WINDOW, dkv), F32)],
                 semantics=("arbitrary",), plan=plan)
    (dq, dk, dv, dsink), rest = res if plan is not None else (res, None)
    return (dq, dk, dv, dsink.sum(0)), rest


def _ssm_operators(lam_re, lam_im, log_step, b_re, b_im, c_re, c_im, d_skip):
    g, p = lam_re.shape
    h = b_re.shape[-1]
    l = SSM_CHUNK
    step = jnp.exp(log_step)[:, None]
    mag = jnp.exp(lam_re * step)
    ar, ai = mag * jnp.cos(lam_im * step), mag * jnp.sin(lam_im * step)
    den = lam_re * lam_re + lam_im * lam_im
    cr = ((ar - 1.0) * lam_re + ai * lam_im) / den
    ci = (ai * lam_re - (ar - 1.0) * lam_im) / den
    bbr = cr[..., None] * b_re - ci[..., None] * b_im
    bbi = cr[..., None] * b_im + ci[..., None] * b_re
    pr, pi = [jnp.ones_like(ar)], [jnp.zeros_like(ar)]
    for _ in range(l):
        pr, pi = pr + [pr[-1] * ar - pi[-1] * ai], pi + [pr[-1] * ai + pi[-1] * ar]
    pwr, pwi = jnp.stack(pr, axis=1), jnp.stack(pi, axis=1)
    cpr = c_re[:, None] * pwr[:, :, None, :] - c_im[:, None] * pwi[:, :, None, :]
    cpi = c_re[:, None] * pwi[:, :, None, :] + c_im[:, None] * pwr[:, :, None, :]
    kern = (jnp.einsum("gtop,gpi->gtoi", cpr[:, :l], bbr, precision=lax.Precision.HIGHEST)
            - jnp.einsum("gtop,gpi->gtoi", cpi[:, :l], bbi, precision=lax.Precision.HIGHEST))
    kern = kern.at[:, 0].add(d_skip.reshape(g, h)[:, :, None] * jnp.eye(h, dtype=F32))
    tm = jnp.stack([jnp.pad(kern[:, :l - j], ((0, 0), (j, 0), (0, 0), (0, 0))) for j in range(l)], axis=1)
    tm = tm.transpose(0, 1, 4, 2, 3).reshape(g, l * h, l * h)
    rev_r, rev_i = pwr[:, l - 1::-1][:, :l], pwi[:, l - 1::-1][:, :l]
    er = rev_r[:, :, None, :] * bbr.transpose(0, 2, 1)[:, None] - rev_i[:, :, None, :] * bbi.transpose(0, 2, 1)[:, None]
    ei = rev_r[:, :, None, :] * bbi.transpose(0, 2, 1)[:, None] + rev_i[:, :, None, :] * bbr.transpose(0, 2, 1)[:, None]
    em = jnp.concatenate([er, ei], axis=-1).reshape(g, l * h, 2 * p)
    fr = cpr[:, 1:].transpose(0, 3, 1, 2).reshape(g, p, l * h)
    fi = -cpi[:, 1:].transpose(0, 3, 1, 2).reshape(g, p, l * h)
    fm = jnp.concatenate([fr, fi], axis=1)
    return tm, em, fm, pwr[:, l], pwi[:, l]


def _decay_lanes(alr, ali):
    return jnp.concatenate([alr, alr], axis=1), jnp.concatenate([-ali, ali], axis=1)


def _column_blocks(s, ds):
    wide = 2 if (ds // LANES) % 2 == 0 else 1
    return wide, pl.BlockSpec((s, wide * LANES), lambda i: (0, i // wide))


def _my_columns(wide, block_ref, stage_ref, store):
    part = pl.program_id(0) % wide
    for p in range(wide):
        @pl.when(part == p)
        def _():
            cols = slice(p * LANES, (p + 1) * LANES)
            if store:
                block_ref[:, cols] = stage_ref[...]
            else:
                stage_ref[...] = block_ref[:, cols]


def _ssm_fwd(name, u, tm, em, fm, acat, bcat, plan=None):
    s, ds = u.shape
    g, lh, p2 = em.shape
    gb, h = SSM_GROUPS_PER_STEP, lh // SSM_CHUNK
    assert gb * h == LANES and g * h == ds and s % SSM_CHUNK == 0
    nc, half = s // SSM_CHUNK, p2 // 2

    def body(u_ref, tm_ref, em_ref, fm_ref, a_ref, b_ref, y_ref, xp_ref, uc_ref, yc_ref, st_ref, col_ref):
        _my_columns(wide, u_ref, col_ref, store=False)
        _to_chunks(col_ref, uc_ref, nc, h)
        for i in range(gb):
            st_ref[pl.ds(i, nc, stride=gb), :] = jnp.dot(uc_ref[i], em_ref[i], precision=SSM_PRECISION,
                                                         preferred_element_type=F32)
        av, bv = a_ref[...], b_ref[...]

        def step(c, carry):
            x, xs = carry
            rows = pl.ds(pl.multiple_of(c * gb, gb), gb)
            loc = st_ref[rows, :]
            st_ref[rows, :] = x
            return av * x + bv * xs + loc, av * xs - bv * x + pltpu.roll(loc, half, 1)
        zero = jnp.zeros((gb, p2), F32)
        lax.fori_loop(0, nc, step, (zero, zero), unroll=4)
        for i in range(gb):
            xp = st_ref[pl.ds(i, nc, stride=gb), :]
            xp_ref[i] = xp
            yc_ref[i] = (jnp.dot(uc_ref[i], tm_ref[i], precision=SSM_PRECISION, preferred_element_type=F32)
                         + jnp.dot(xp, fm_ref[i], precision=SSM_PRECISION, preferred_element_type=F32))
        _from_chunks(yc_ref, col_ref, nc, h)
        _my_columns(wide, y_ref, col_ref, store=True)

    blk = lambda r, c: pl.BlockSpec((gb, r, c), lambda i: (i, 0, 0))
    vec = pl.BlockSpec((gb, p2), lambda i: (i, 0))
    wide, col = _column_blocks(s, ds)
    return _pcall(name, body, (g // gb,), [col, blk(lh, lh), blk(lh, p2), blk(p2, lh), vec, vec],
                  [u, tm, em, fm, acat, bcat], [col, blk(nc, p2), blk(nc, lh)],
                  [SDS((s, ds), F32), SDS((g, nc, p2), F32), SDS((g, nc, lh), F32)],
                  scratch=[pltpu.VMEM((gb, nc, lh), F32), pltpu.VMEM((nc * gb, p2), F32), pltpu.VMEM((s, LANES), F32)],
                  semantics=("arbitrary",), plan=plan)


def _ssm_bwd(name, u_chunks, dy, xprev, tm, em, fm, acat, bcat, plan=None):
    s, ds = dy.shape
    g, lh, p2 = em.shape
    gb, h = SSM_GROUPS_PER_STEP, lh // SSM_CHUNK
    nc, half = s // SSM_CHUNK, p2 // 2

    def body(uc_ref, dy_ref, xp_ref, tm_ref, em_ref, fm_ref, a_ref, b_ref,
             du_ref, dtm_ref, dem_ref, dfm_ref, r1_ref, r2_ref, dyc_ref, duc_ref, gs_ref, xs_ref, col_ref):
        _my_columns(wide, dy_ref, col_ref, store=False)
        _to_chunks(col_ref, dyc_ref, nc, h)
        for i in range(gb):
            gs_ref[pl.ds(i, nc, stride=gb), :] = lax.dot_general(
                dyc_ref[i], fm_ref[i], _DOT_DIMS["nt"], precision=SSM_PRECISION, preferred_element_type=F32)
            xs_ref[pl.ds(i, nc, stride=gb), :] = xp_ref[i]
        av, bv = a_ref[...], b_ref[...]

        def step(t, carry):
            grad, gsw, r1, r2 = carry
            c = nc - 1 - t
            rows = pl.ds(pl.multiple_of(c * gb, gb), gb)
            dxp, xp = gs_ref[rows, :], xs_ref[rows, :]
            gs_ref[rows, :] = grad
            r1 = r1 + grad * xp
            r2 = r2 + grad * pltpu.roll(xp, half, 1)
            return dxp + av * grad - bv * gsw, pltpu.roll(dxp, half, 1) + av * gsw + bv * grad, r1, r2
        zero = jnp.zeros((gb, p2), F32)
        _, _, r1, r2 = lax.fori_loop(0, nc, step, (zero, zero, zero, zero), unroll=4)
        r1_ref[...], r2_ref[...] = r1, r2
        for i in range(gb):
            dxl = gs_ref[pl.ds(i, nc, stride=gb), :]
            duc_ref[i] = (lax.dot_general(dyc_ref[i], tm_ref[i], _DOT_DIMS["nt"], precision=SSM_PRECISION,
                                          preferred_element_type=F32)
                          + lax.dot_general(dxl, em_ref[i], _DOT_DIMS["nt"], precision=SSM_PRECISION,
                                            preferred_element_type=F32))
            dtm_ref[i] = lax.dot_general(uc_ref[i], dyc_ref[i], _DOT_DIMS["tn"], precision=SSM_PRECISION,
                                         preferred_element_type=F32)
            dfm_ref[i] = lax.dot_general(xp_ref[i], dyc_ref[i], _DOT_DIMS["tn"], precision=SSM_PRECISION,
                                         preferred_element_type=F32)
            dem_ref[i] = lax.dot_general(uc_ref[i], dxl, _DOT_DIMS["tn"], precision=SSM_PRECISION,
                                         preferred_element_type=F32)
        _from_chunks(duc_ref, col_ref, nc, h)
        _my_columns(wide, du_ref, col_ref, store=True)

    blk = lambda r, c: pl.BlockSpec((gb, r, c), lambda i: (i, 0, 0))
    vec = pl.BlockSpec((gb, p2), lambda i: (i, 0))
    wide, col = _column_blocks(s, ds)
    chunked = pltpu.VMEM((gb, nc, lh), F32)
    res = _pcall(name, body, (g // gb,),
                 [blk(nc, lh), col, blk(nc, p2), blk(lh, lh), blk(lh, p2), blk(p2, lh), vec, vec],
                 [u_chunks, dy, xprev, tm, em, fm, acat, bcat],
                 [col, blk(lh, lh), blk(lh, p2), blk(p2, lh), vec, vec],
                 [SDS((s, ds), F32), SDS((g, lh, lh), F32), SDS((g, lh, p2), F32), SDS((g, p2, lh), F32),
                  SDS((g, p2), F32), SDS((g, p2), F32)],
                 scratch=[chunked, chunked, pltpu.VMEM((nc * gb, p2), F32), pltpu.VMEM((nc * gb, p2), F32),
                          pltpu.VMEM((s, LANES), F32)],
                 semantics=("arbitrary",), plan=plan)
    return res if plan is not None else (res, None)


def _to_chunks(src_ref, dst_ref, nc, h):
    per = LANES // h
    grp = lax.broadcasted_iota(jnp.int32, (nc, LANES), 1) // h
    for g in range(per):
        for part in range(SSM_CHUNK * h // LANES):
            acc = None
            for i in range(part * per, (part + 1) * per):
                piece = src_ref[pl.ds(i, nc, stride=SSM_CHUNK), :]
                lo = (i * h) % LANES
                if (lo - g * h) % LANES:
                    piece = pltpu.roll(piece, (lo - g * h) % LANES, 1)
                acc = piece if acc is None else jnp.where(grp == lo // h, piece, acc)
            dst_ref[g, :, part * LANES:(part + 1) * LANES] = acc


def _from_chunks(src_ref, dst_ref, nc, h):
    per = LANES // h
    grp = lax.broadcasted_iota(jnp.int32, (nc, LANES), 1) // h
    for i in range(SSM_CHUNK):
        part, lo = divmod(i * h, LANES)
        row = None
        for g in range(per):
            piece = src_ref[g, :, part * LANES:(part + 1) * LANES]
            if (g * h - lo) % LANES:
                piece = pltpu.roll(piece, (g * h - lo) % LANES, 1)
            row = piece if row is None else jnp.where(grp == g, piece, row)
        dst_ref[pl.ds(i, nc, stride=SSM_CHUNK), :] = row


_SMALL = ("b_ada", "norm1_g", "sinks", "ssm_lam_re", "ssm_lam_im", "ssm_log_step", "ssm_b_re", "ssm_b_im",
          "ssm_c_re", "ssm_c_im", "ssm_d", "b_glu", "attn_out_g", "ssm_out_g", "norm2_g", "final_g")
_WEIGHTS = ("w_ada", "b_ada", "norm1_g", "w_in", "sinks", "ssm_lam_re", "ssm_lam_im", "ssm_log_step", "ssm_b_re",
            "ssm_b_im", "ssm_c_re", "ssm_c_im", "ssm_d", "w_glu", "b_glu", "attn_out_g", "ssm_out_g", "w_out",
            "norm2_g", "w_ff1", "w_ff2", "final_g")
_PACK_ALIGN = 128 * LANES


def _pack(parts):
    flat = jnp.concatenate([p.reshape(-1).astype(F32) for p in parts])
    pad = (-flat.shape[0]) % _PACK_ALIGN
    return jnp.pad(flat, (0, pad)).reshape(-1, LANES)


def kernel(x, c, w_ada, b_ada, norm1_g, w_in, sinks, ssm_lam_re, ssm_lam_im, ssm_log_step, ssm_b_re, ssm_b_im, ssm_c_re, ssm_c_im, ssm_d, w_glu, b_glu, attn_out_g, ssm_out_g, w_out, norm2_g, w_ff1, w_ff2, final_g, loss_target, m_w_ada, m_b_ada, m_norm1_g, m_w_in, m_sinks, m_ssm_lam_re, m_ssm_lam_im, m_ssm_log_step, m_ssm_b_re, m_ssm_b_im, m_ssm_c_re, m_ssm_c_im, m_ssm_d, m_w_glu, m_b_glu, m_attn_out_g, m_ssm_out_g, m_w_out, m_norm2_g, m_w_ff1, m_w_ff2, m_final_g, v_w_ada, v_b_ada, v_norm1_g, v_w_in, v_sinks, v_ssm_lam_re, v_ssm_lam_im, v_ssm_log_step, v_ssm_b_re, v_ssm_b_im, v_ssm_c_re, v_ssm_c_im, v_ssm_d, v_w_glu, v_b_glu, v_attn_out_g, v_ssm_out_g, v_w_out, v_norm2_g, v_w_ff1, v_w_ff2, v_final_g):
    args = dict(locals())
    weights = {n: args[n] for n in _WEIGHTS}
    mom = {n: args["m_" + n] for n in _WEIGHTS}
    var = {n: args["v_" + n] for n in _WEIGHTS}
    me = 4 * lax.axis_index("x") + 2 * lax.axis_index("y") + lax.axis_index("c")

    _, s, d = x.shape
    xs, tgt = x[0], loss_target[0]
    d_ssm = ssm_d.shape[-1]
    d_attn = d - d_ssm
    nq = d_attn // HEAD_DIM
    d_kv = (nq // Q_PER_KV) * HEAD_DIM
    p_state = ssm_b_re.shape[2]

    c_all, g_in = _run_plan("gather_c_w_in", _Gather([c, w_in[0].T.astype(BF16)]))
    c_all = c_all.reshape(N_DEV, d)
    w_in_t = g_in.reshape(-1, d)

    n_loc = w_ada.shape[-1]
    b_loc = lax.dynamic_slice_in_dim(b_ada, me * n_loc, n_loc, axis=1)
    silu = lambda t: t * _sigmoid(t)
    mod_part = _matmul("ada_mod", c_all, w_ada[0], "nn", [F32], a_pro=silu, vecs=[b_loc], exact=True,
                       epilogue=lambda acc, e, v: (acc + v[0],), tn=512, tk=d)
    mod_all = _run_plan("gather_mod", _Gather([mod_part]))[0]
    mod = lax.dynamic_index_in_dim(mod_all, me, axis=1, keepdims=False).reshape(N_MOD, 1, d)
    shift1, scale1, gate1, shift2, scale2, gate2 = [mod[i] for i in range(N_MOD)]

    h1 = _norm_mod_fwd("norm1", xs, norm1_g, scale1, shift1)
    q = _matmul("proj_q", h1, w_in_t, "nt", [F32], b_rows=(0, d_attn))
    kv = _matmul("proj_kv", h1, w_in_t, "nt", [F32], b_rows=(d_attn, 2 * d_kv))
    u = _matmul("proj_u", h1, w_in_t, "nt", [F32], b_rows=(d_attn + 2 * d_kv, d_ssm))
    k, v = kv[:, :d_kv], kv[:, d_kv:]

    half = HEAD_DIM // 2
    inv_freq = ROPE_THETA ** (-jnp.arange(half, dtype=F32) / half)
    ang = jnp.arange(s, dtype=F32)[:, None] * inv_freq[None, :]
    cos_t, sin_t = jnp.tile(jnp.cos(ang), (1, 4)), jnp.tile(jnp.sin(ang), (1, 4))
    (attn, lse), (g_glu, g_out) = _attn_fwd("attn_fwd", q, k, v, cos_t, sin_t, sinks,
                                            plan=_Gather([w_glu[0].astype(BF16), w_out[0].astype(BF16)]))
    w_glu_f = g_glu.reshape(d_ssm, d_ssm)
    w_out_f = g_out.reshape(d, d)

    ssm_params = (ssm_lam_re[0], ssm_lam_im[0], ssm_log_step[0], ssm_b_re[0], ssm_b_im[0], ssm_c_re[0],
                  ssm_c_im[0], ssm_d[0])
    (tm_op, em_op, fm_op, alr, ali), ssm_vjp = jax.vjp(_ssm_operators, *ssm_params)
    acat, bcat = _decay_lanes(alr, ali)
    (y_ssm, x_prev, u_chunks), (g_ff1,) = _ssm_fwd("ssm_fwd", u, tm_op, em_op, fm_op, acat, bcat,
                                                   plan=_Gather([w_ff1[0].astype(BF16)]))
    yg = _gelu_fwd("gelu", y_ssm)
    ssm_out, z_glu = _matmul(
        "glu", yg, w_glu_f, "nn", [F32, F32], extras=[y_ssm], vecs=[b_glu],
        epilogue=lambda acc, e, v: (_gelu(e[0]) * _sigmoid(acc + v[0]), acc + v[0]))
    mixed = _group_norm_fwd("group_norm", attn, ssm_out, attn_out_g, ssm_out_g)
    x2, mo = _matmul("out_proj", mixed, w_out_f, "nn", [F32, BF16], extras=[xs], vecs=[gate1],
                     epilogue=lambda acc, e, v: (e[0] + v[0] * acc, acc))

    h2 = _norm_mod_fwd("norm2", x2, norm2_g, scale2, shift2)
    (a_ff, f_ff), (g_ff2,) = _matmul("ff1", h2, g_ff1, "nn", [BF16, BF16], b_blocked=True,
                                     epilogue=lambda acc, e, v: (acc, jnp.square(jnp.maximum(acc, 0.0))),
                                     plan=_Gather([w_ff2[0].astype(BF16)]))
    w_ff2_f = g_ff2.reshape(-1, d)
    x3, ff = _matmul("ff2", f_ff, w_ff2_f, "nn", [F32, BF16], extras=[x2], vecs=[gate2],
                     epilogue=lambda acc, e, v: (e[0] + v[0] * acc, acc))

    dx3, dff, loss_local, d_final_g, d_gate2 = _final_loss("final_loss", x3, tgt, final_g.reshape(1, d), ff, gate2)
    loss = lax.psum(loss_local, MESH_AXES)

    dw_ff2 = _matmul("ff2_dw", f_ff, dff, "tn", [BF16]).reshape(N_DEV, -1, d)
    da_ff, (p_ff2,) = _matmul("ff2_dx", dff, w_ff2_f, "nt", [BF16], extras=[a_ff],
                              epilogue=lambda acc, e, v: (acc * (2.0 * jnp.maximum(e[0].astype(F32), 0.0)),),
                              plan=_PairSwap([dw_ff2]))
    s_ff2 = _pair_add("pair_add_ff2", dw_ff2, p_ff2)
    dw_ff1, (r_ff2_a,) = _matmul("ff1_dw", h2, da_ff, "tn", [BF16], out_blocked=N_DEV,
                                 plan=_ChipExchange([s_ff2], (CHIP_X, CHIP_Y)))
    dh2, (r_ff2_b,) = _matmul("ff1_dx", da_ff, g_ff1, "nt", [F32], b_blocked=True,
                              plan=_ChipExchange([s_ff2], (CHIP_DIAGONAL,)))
    (dx2, dmo, d_scale2, d_shift2, d_norm2_g, d_gate1), (p_ff1,) = _norm_mod_bwd(
        "norm2_bwd", x2, dh2, dx3, norm2_g, scale2, gated=(mo, gate1), plan=_PairSwap([dw_ff1]))
    s_ff1 = _pair_add("pair_add_ff1", dw_ff1, p_ff1)

    dw_out = _matmul("out_dw", mixed, dmo, "tn", [BF16]).reshape(N_DEV, -1, d)
    dmixed, (r_out_a,) = _matmul("out_dx", dmo, w_out_f, "nt", [F32], plan=_Exchange([dw_out], (1, 4, 2)))
    (dattn, dssm_out, d_attn_g, d_ssm_g), (r_out_b,) = _group_norm_bwd(
        "group_norm_bwd", attn, ssm_out, dmixed, attn_out_g, ssm_out_g, plan=_Exchange([dw_out], (6,)))

    dz, dyg_direct, d_b_glu = _glu_bwd("glu_bwd", dssm_out, y_ssm, z_glu)
    dw_glu = _matmul("glu_dw", yg, dz, "tn", [BF16]).reshape(N_DEV, -1, d_ssm)
    dy_ssm, (r_glu,) = _matmul("glu_dx", dz, w_glu_f, "nt", [F32], extras=[dyg_direct, y_ssm],
                               epilogue=lambda acc, e, v: ((acc + e[0]) * _gelu_grad(e[1]),),
                               plan=_Exchange([dw_glu], RELATIONS_ALL))
    (du, d_tm, d_em, d_fm, r1, r2), (r_ff1_a,) = _ssm_bwd(
        "ssm_bwd", u_chunks, dy_ssm, x_prev, tm_op, em_op, fm_op, acat, bcat,
        plan=_ChipExchange([s_ff1], (CHIP_X, CHIP_Y)))
    d_alr = r1[:, :p_state] + r1[:, p_state:]
    d_ali = r2[:, p_state:] - r2[:, :p_state]
    d_ssm_params = ssm_vjp((d_tm, d_em, d_fm, d_alr, d_ali))

    (dq, dk, dv, d_sinks), (r_ff1_b,) = _attn_bwd("attn_bwd", q, k, v, cos_t, sin_t, sinks, attn, lse, dattn,
                                                  plan=_ChipExchange([s_ff1], (CHIP_DIAGONAL,)))
    dproj = jnp.concatenate([dq, dk, dv, du], axis=1).astype(BF16)
    dw_in_t, (r_out_c,) = _matmul("in_dw", dproj, h1, "tn", [BF16], plan=_Exchange([dw_out], (5, 3, 7)))
    dw_in_t = dw_in_t.reshape(N_DEV, -1, d)
    dh1, (r_in_a,) = _matmul("in_dx", dproj, w_in_t, "nn", [F32], plan=_Exchange([dw_in_t], RELATIONS_SAME_CORE))
    (grad_x, d_scale1, d_shift1, d_norm1_g), _ = _norm_mod_bwd("norm1_bwd", xs, dh1, dx2, norm1_g, scale1)

    d_mod = jnp.concatenate([d_shift1, d_scale1, d_gate1, d_shift2, d_scale2, d_gate2])
    small_g = dict(zip(("ssm_lam_re", "ssm_lam_im", "ssm_log_step", "ssm_b_re", "ssm_b_im", "ssm_c_re", "ssm_c_im",
                        "ssm_d"), d_ssm_params, strict=True))
    small_g.update(b_ada=d_mod, norm1_g=d_norm1_g, sinks=d_sinks, b_glu=d_b_glu, attn_out_g=d_attn_g,
                   ssm_out_g=d_ssm_g, norm2_g=d_norm2_g, final_g=d_final_g)
    small_parts, r_in_b = _run_plan("gather_small_grads", _Plans([_Gather([_pack([small_g[n] for n in _SMALL])]),
                                                                  _Exchange([dw_in_t], RELATIONS_OTHER_CORE)]))
    small = _adam_shard("adam_small", [small_parts], _pack([weights[n] for n in _SMALL]),
                        _pack([mom[n] for n in _SMALL]), _pack([var[n] for n in _SMALL]))
    out = {}
    off = 0
    for n in _SMALL:
        size = weights[n].size
        out[n] = [t.reshape(-1)[off:off + size].reshape(weights[n].shape) for t in small]
        off += size

    dmod_all = small_parts.reshape(N_DEV, -1)[:, :N_MOD * d]
    dmod_loc = lax.dynamic_slice_in_dim(dmod_all, me * n_loc, n_loc, axis=1)
    c_act_t = silu(c_all).T
    out["w_ada"] = [t[None] for t in _ada_update("adam_w_ada", c_act_t, dmod_loc, w_ada[0], m_w_ada[0], v_w_ada[0])]

    mine = lambda blocks: lax.dynamic_index_in_dim(blocks, me, axis=0, keepdims=False)
    in_parts = [mine(dw_in_t).T] + [r.transpose(0, 2, 1) for r in (r_in_a, r_in_b)]
    my_chip = 2 * lax.axis_index("x") + lax.axis_index("y")
    chip_sum = lambda sums: lax.dynamic_index_in_dim(sums, my_chip, axis=0, keepdims=False)
    received = dict(w_in=in_parts, w_glu=[mine(dw_glu), r_glu], w_out=[mine(dw_out), r_out_a, r_out_b, r_out_c],
                    w_ff1=[chip_sum(s_ff1), r_ff1_a, r_ff1_b], w_ff2=[chip_sum(s_ff2), r_ff2_a, r_ff2_b])
    for n, parts in received.items():
        out[n] = [t[None] for t in _adam_shard("adam_" + n, parts, weights[n][0], mom[n][0], var[n][0])]

    return (loss, grad_x[None], *[out[n][0] for n in _WEIGHTS], *[out[n][1] for n in _WEIGHTS],
            *[out[n][2] for n in _WEIGHTS], *[out[n][3] for n in _WEIGHTS])
```

```python
import math

import jax
import jax.numpy as jnp
from jax import lax
from jax.experimental import pallas as pl
from jax.experimental.pallas import tpu as pltpu

F32, BF16 = jnp.float32, jnp.bfloat16
SDS = jax.ShapeDtypeStruct
MESH_AXES = ("x", "y", "c")
N_DEV = 8
VMEM_LIMIT_BYTES = 56 * 1024 * 1024
MATMUL_VMEM_BUDGET = 44 * 1024 * 1024
SUBLANES, LANES = 8, 128

HEAD_DIM = 64
Q_PER_KV = 8
WINDOW = 128
ROPE_THETA = 10000.0
EPS = 1e-6
N_MOD = 6
SSM_CHUNK = 16
SSM_GROUPS_PER_STEP = 8

ADAM_LR, ADAM_B1, ADAM_B2, ADAM_EPS, ADAM_WD, ADAM_STEP = 0.001, 0.9, 0.999, 1e-08, 0.01, 10
HIGHEST = lax.Precision.HIGHEST
SSM_PRECISION = lax.Precision.HIGH

RELATIONS_ALL = (1, 4, 2, 6, 5, 3, 7)
RELATIONS_SAME_CORE = (1, 4, 2, 6)
RELATIONS_OTHER_CORE = (5, 3, 7)
PLAN_MIDDLE = 0.65


def _cparams(sem):
    return pltpu.CompilerParams(dimension_semantics=sem, vmem_limit_bytes=VMEM_LIMIT_BYTES)


def _block_index(p):
    return 4 * p[0] + 2 * p[1] + p[2]


def _me():
    return lax.axis_index("x"), lax.axis_index("y"), lax.axis_index("c")


class _Plan:
    def middle(self, ins, outs, send, recv, local):
        pass


class _Gather(_Plan):
    TO_SIBLING, TO_X, TO_Y, RELAY, PASS_X, PASS_Y, PASS_DIAGONAL = range(7)

    def __init__(self, arrs):
        self.ins = list(arrs)
        self.out_shapes = [SDS((N_DEV,) + a.shape, a.dtype) for a in arrs]
        self.n_rdma, self.n_local = 7 * len(arrs), len(arrs)
        self.rdma_base = self.local_base = 0

    def _copy(self, ins, outs, send, recv, a, k, block, to, from_input=False):
        dst = outs[a].at[_block_index(block)]
        sem = self.rdma_base + a * 7 + k
        return pltpu.make_async_remote_copy(
            src_ref=ins[a] if from_input else dst, dst_ref=dst, send_sem=send.at[sem], recv_sem=recv.at[sem],
            device_id=to, device_id_type=pl.DeviceIdType.MESH)

    @staticmethod
    def _places():
        x, y, c = _me()
        return (x, y, c), (x, y, 1 - c), (1 - x, y, c), (x, 1 - y, c), (1 - x, 1 - y, c)

    def _first(self, ins, outs, send, recv, a):
        me, sibling, x_nbr, y_nbr, _ = self._places()
        return [self._copy(ins, outs, send, recv, a, k, me, to, True)
                for k, to in ((self.TO_SIBLING, sibling), (self.TO_X, x_nbr), (self.TO_Y, y_nbr))]

    def _mine(self, ins, outs, local, a):
        return pltpu.make_async_copy(ins[a], outs[a].at[_block_index(_me())], local.at[self.local_base + a])

    def start(self, ins, outs, send, recv, local):
        for a in range(len(ins)):
            self._mine(ins, outs, local, a).start()
            for cp in self._first(ins, outs, send, recv, a):
                cp.start()

    def middle(self, ins, outs, send, recv, local):
        me, sibling, x_nbr, y_nbr, _ = self._places()
        core = me[2]
        for a in range(len(ins)):
            self._copy(ins, outs, send, recv, a, self.TO_X, x_nbr, me).wait_recv()
            self._copy(ins, outs, send, recv, a, self.TO_Y, y_nbr, me).wait_recv()

            @pl.when(core == 0)
            def _():
                self._copy(ins, outs, send, recv, a, self.RELAY, x_nbr, y_nbr).start()

            @pl.when(core == 1)
            def _():
                self._copy(ins, outs, send, recv, a, self.RELAY, y_nbr, x_nbr).start()

            self._copy(ins, outs, send, recv, a, self.PASS_X, x_nbr, sibling).start()
            self._copy(ins, outs, send, recv, a, self.PASS_Y, y_nbr, sibling).start()

    def finish(self, ins, outs, send, recv, local):
        me, sibling, x_nbr, y_nbr, diagonal = self._places()
        other = lambda p: (p[0], p[1], 1 - p[2])
        for a in range(len(ins)):
            self._copy(ins, outs, send, recv, a, self.RELAY, diagonal, me).wait_recv()
            self._copy(ins, outs, send, recv, a, self.PASS_DIAGONAL, diagonal, sibling).start()
        for a in range(len(ins)):
            self._copy(ins, outs, send, recv, a, self.TO_SIBLING, sibling, me).wait_recv()
            for k, src in ((self.PASS_X, x_nbr), (self.PASS_Y, y_nbr), (self.PASS_DIAGONAL, diagonal)):
                self._copy(ins, outs, send, recv, a, k, other(src), me).wait_recv()
                self._copy(ins, outs, send, recv, a, k, src, sibling).wait_send()
            for cp in self._first(ins, outs, send, recv, a):
                cp.wait_send()
            self._copy(ins, outs, send, recv, a, self.RELAY, me, me).wait_send()
            self._mine(ins, outs, local, a).wait()


class _Exchange(_Plan):
    def __init__(self, arrs, relations):
        self.ins, self.relations = list(arrs), tuple(relations)
        self.out_shapes = [SDS((len(relations),) + a.shape[1:], a.dtype) for a in arrs]
        self.n_rdma, self.n_local = len(relations) * len(arrs), 0
        self.rdma_base = self.local_base = 0

    def _copies(self, ins, outs, send, recv):
        x, y, c = _me()
        cps = []
        for a in range(len(ins)):
            for s, k in enumerate(self.relations):
                peer = ((1 - x) if (k & 4) else x, (1 - y) if (k & 2) else y, (1 - c) if (k & 1) else c)
                sem = self.rdma_base + a * len(self.relations) + s
                cps.append(pltpu.make_async_remote_copy(
                    src_ref=ins[a].at[_block_index(peer)], dst_ref=outs[a].at[s], send_sem=send.at[sem],
                    recv_sem=recv.at[sem], device_id=peer, device_id_type=pl.DeviceIdType.MESH))
        return cps

    def start(self, ins, outs, send, recv, local):
        for cp in self._copies(ins, outs, send, recv):
            cp.start()

    def finish(self, ins, outs, send, recv, local):
        for cp in self._copies(ins, outs, send, recv):
            cp.wait()


class _PairSwap(_Plan):
    def __init__(self, arrs):
        self.ins = list(arrs)
        self.out_shapes = [SDS((4,) + a.shape[1:], a.dtype) for a in arrs]
        self.n_rdma, self.n_local = 4 * len(arrs), 0
        self.rdma_base = self.local_base = 0

    def _copies(self, ins, outs, send, recv):
        x, y, c = _me()
        cps = []
        for a in range(len(ins)):
            for s in range(4):
                sem = self.rdma_base + a * 4 + s
                cps.append(pltpu.make_async_remote_copy(
                    src_ref=ins[a].at[2 * s + (1 - c)], dst_ref=outs[a].at[s], send_sem=send.at[sem],
                    recv_sem=recv.at[sem], device_id=(x, y, 1 - c), device_id_type=pl.DeviceIdType.MESH))
        return cps

    def start(self, ins, outs, send, recv, local):
        for cp in self._copies(ins, outs, send, recv):
            cp.start()

    def finish(self, ins, outs, send, recv, local):
        for cp in self._copies(ins, outs, send, recv):
            cp.wait()


CHIP_X, CHIP_Y, CHIP_DIAGONAL = (1, 0), (0, 1), (1, 1)


class _ChipExchange(_Plan):
    def __init__(self, arrs, hops):
        self.ins, self.hops = list(arrs), tuple(hops)
        self.out_shapes = [SDS((len(hops),) + a.shape[1:], a.dtype) for a in arrs]
        self.n_rdma, self.n_local = len(hops) * len(arrs), 0
        self.rdma_base = self.local_base = 0

    def _copies(self, ins, outs, send, recv):
        x, y, c = _me()
        cps = []
        for a in range(len(ins)):
            for s, (fx, fy) in enumerate(self.hops):
                px, py = (1 - x) if fx else x, (1 - y) if fy else y
                sem = self.rdma_base + a * len(self.hops) + s
                cps.append(pltpu.make_async_remote_copy(
                    src_ref=ins[a].at[2 * px + py], dst_ref=outs[a].at[s], send_sem=send.at[sem],
                    recv_sem=recv.at[sem], device_id=(px, py, c), device_id_type=pl.DeviceIdType.MESH))
        return cps

    def start(self, ins, outs, send, recv, local):
        for cp in self._copies(ins, outs, send, recv):
            cp.start()

    def finish(self, ins, outs, send, recv, local):
        for cp in self._copies(ins, outs, send, recv):
            cp.wait()


class _Plans:
    def __init__(self, plans):
        self.plans = list(plans)
        self.ins = [a for p in plans for a in p.ins]
        self.out_shapes = [s for p in plans for s in p.out_shapes]
        self.n_rdma = self.n_local = 0
        for p in plans:
            p.rdma_base, p.local_base = self.n_rdma, self.n_local
            self.n_rdma, self.n_local = self.n_rdma + p.n_rdma, self.n_local + p.n_local

    def _each(self, ins, outs):
        i = o = 0
        for p in self.plans:
            yield p, ins[i:i + len(p.ins)], outs[o:o + len(p.out_shapes)]
            i, o = i + len(p.ins), o + len(p.out_shapes)

    def start(self, ins, outs, send, recv, local):
        for p, pi, po in self._each(ins, outs):
            p.start(pi, po, send, recv, local)

    def middle(self, ins, outs, send, recv, local):
        for p, pi, po in self._each(ins, outs):
            p.middle(pi, po, send, recv, local)

    def finish(self, ins, outs, send, recv, local):
        for p, pi, po in self._each(ins, outs):
            p.finish(pi, po, send, recv, local)


def _plan_scratch(plan):
    return [pltpu.SemaphoreType.DMA((plan.n_rdma,)), pltpu.SemaphoreType.DMA((plan.n_rdma,)),
            pltpu.SemaphoreType.DMA((max(plan.n_local, 1),))]


def _run_plan(name, plan):
    n = len(plan.ins)

    def body(*refs):
        ins, outs, sems = refs[:n], refs[n:len(refs) - 3], refs[len(refs) - 3:]
        plan.start(ins, outs, *sems)
        plan.middle(ins, outs, *sems)
        plan.finish(ins, outs, *sems)

    any_spec = pl.BlockSpec(memory_space=pl.ANY)
    return pl.pallas_call(body, name=name, out_shape=list(plan.out_shapes), in_specs=[any_spec] * n,
                          out_specs=[any_spec] * len(plan.out_shapes), scratch_shapes=_plan_scratch(plan))(*plan.ins)


def _pcall(name, body, grid, in_specs, ins, out_specs, out_shape, scratch=(), semantics=None, plan=None):
    if plan is None:
        return pl.pallas_call(body, name=name, grid=grid, in_specs=list(in_specs), out_specs=list(out_specs),
                              out_shape=list(out_shape), scratch_shapes=list(scratch),
                              compiler_params=_cparams(semantics))(*ins)
    n_in, n_out, n_scr = len(ins), len(out_shape), len(scratch)
    p_in, p_out = len(plan.ins), len(plan.out_shapes)

    def with_plan(*refs):
        k_in, c_in = refs[:n_in], refs[n_in:n_in + p_in]
        refs = refs[n_in + p_in:]
        k_out, c_out = refs[:n_out], refs[n_out:n_out + p_out]
        refs = refs[n_out + p_out:]
        k_scr, sems = refs[:n_scr], refs[n_scr:]
        step = 0
        for d, g in enumerate(grid):
            step = step * g + pl.program_id(d)
        n_steps = math.prod(grid)

        @pl.when(step == 0)
        def _():
            plan.start(c_in, c_out, *sems)

        @pl.when(step == min(n_steps - 1, int(n_steps * PLAN_MIDDLE)))
        def _():
            plan.middle(c_in, c_out, *sems)

        body(*k_in, *k_out, *k_scr)

        @pl.when(step == n_steps - 1)
        def _():
            plan.finish(c_in, c_out, *sems)

    any_spec = pl.BlockSpec(memory_space=pl.ANY)
    res = pl.pallas_call(
        with_plan, name=name, grid=grid, in_specs=list(in_specs) + [any_spec] * p_in,
        out_specs=list(out_specs) + [any_spec] * p_out, out_shape=list(out_shape) + list(plan.out_shapes),
        scratch_shapes=list(scratch) + _plan_scratch(plan),
        compiler_params=_cparams(("arbitrary",) * len(grid)))(*ins, *plan.ins)
    return res[:n_out], res[n_out:]


def _rowwise(name, fn, rows, vecs, row_outs, acc_outs=(), tm=128, plan=None):
    t = rows[0].shape[0]
    tm = min(tm, t)
    assert t % tm == 0 and tm % SUBLANES == 0
    n_r, n_v, n_o = len(rows), len(vecs), len(row_outs)

    def body(*refs):
        r_in, v_in = refs[:n_r], refs[n_r:n_r + n_v]
        r_out, a_out = refs[n_r + n_v:n_r + n_v + n_o], refs[n_r + n_v + n_o:]
        outs, accs = fn([r[...] for r in r_in], [v[...] for v in v_in])
        for o_ref, o in zip(r_out, outs, strict=True):
            o_ref[...] = o.astype(o_ref.dtype)
        if a_out:
            @pl.when(pl.program_id(0) == 0)
            def _():
                for a_ref in a_out:
                    a_ref[...] = jnp.zeros_like(a_ref)
            for a_ref, a in zip(a_out, accs, strict=True):
                a_ref[...] += a.reshape(tm // SUBLANES, SUBLANES, a.shape[-1]).sum(axis=0)

    in_specs = [pl.BlockSpec((tm, r.shape[1]), lambda i: (i, 0)) for r in rows]
    in_specs += [pl.BlockSpec(v.shape, lambda i: (0, 0)) for v in vecs]
    out_specs = [pl.BlockSpec((tm, w), lambda i: (i, 0)) for w, _ in row_outs]
    out_specs += [pl.BlockSpec((SUBLANES, w), lambda i: (0, 0)) for w in acc_outs]
    out_shape = [SDS((t, w), dt) for w, dt in row_outs] + [SDS((SUBLANES, w), F32) for w in acc_outs]
    return _pcall(name, body, (t // tm,), in_specs, [*rows, *vecs], out_specs, out_shape, semantics=("arbitrary",),
                  plan=plan)


def _tile(n, want):
    if n <= want:
        return n
    for t in range(want // LANES * LANES, 0, -LANES):
        if n % t == 0:
            return t
    raise ValueError(f"no tile for {n}")


_DOT_DIMS = {"nn": (((1,), (0,)), ((), ())), "nt": (((1,), (1,)), ((), ())), "tn": (((0,), (0,)), ((), ()))}


def _matmul(name, a, b, mode, out_dtypes, epilogue=None, extras=(), vecs=(), a_pro=None,
            tm=1024, tn=512, tk=4096, exact=False, b_blocked=False, out_blocked=0, b_rows=None, plan=None):
    cs = b.shape[-1] if b_blocked else None
    b2 = (b.shape[1], b.shape[0] * b.shape[2]) if b_blocked else b.shape
    if mode == "tn":
        (k, m), (k2, n) = a.shape, b2
    elif mode == "nt":
        (m, k), (n, k2) = a.shape, b2
    else:
        (m, k), (k2, n) = a.shape, b2
    assert k == k2 and not (b_blocked and mode == "tn")
    row0 = 0
    if b_rows is not None:
        assert mode == "nt" and not b_blocked
        row0, n = b_rows
        tn = _tile(math.gcd(n, row0) if row0 else n, tn)
    tm, tn, tk = _tile(m, tm), _tile(n, tn), _tile(k, tk)
    if b_blocked and mode == "nn":
        tn = _tile(cs, tn)
    if b_blocked and mode == "nt":
        tk = _tile(cs, tk)
    if out_blocked:
        tn = _tile(n // out_blocked, tn)
    nk = k // tk

    def vmem_bytes(width):
        operands = 2 * (tm * tk * a.dtype.itemsize + tk * width * b.dtype.itemsize)
        tiles = 2 * tm * width * (sum(jnp.dtype(dt).itemsize for dt in out_dtypes) + sum(e.dtype.itemsize for e in extras))
        return operands + tiles + tm * width * 4 * (2 if nk > 1 else 1)

    extent = cs if (b_blocked and mode == "nn") else n // out_blocked if out_blocked else n
    if extent % (2 * tn) == 0 and row0 % (2 * tn) == 0 and vmem_bytes(2 * tn) <= MATMUL_VMEM_BUDGET:
        tn *= 2
    n_e, n_v, n_o = len(extras), len(vecs), len(out_dtypes)
    precision = HIGHEST if exact else None

    def body(*refs):
        a_ref, b_ref = refs[:2]
        e_refs, v_refs = refs[2:2 + n_e], refs[2 + n_e:2 + n_e + n_v]
        o_refs = refs[2 + n_e + n_v:2 + n_e + n_v + n_o]

        def product():
            av = a_ref[...]
            if a_pro is not None:
                av = a_pro(av)
            return lax.dot_general(av, b_ref[...], _DOT_DIMS[mode], precision=precision, preferred_element_type=F32)

        def finish(acc):
            res = (acc,) if epilogue is None else epilogue(acc, [e[...] for e in e_refs], [v[...] for v in v_refs])
            for o_ref, r in zip(o_refs, res, strict=True):
                o_ref[...] = r.astype(o_ref.dtype)

        if nk == 1:
            finish(product())
            return
        acc_ref = refs[-1]
        kk = pl.program_id(2)

        @pl.when(kk == 0)
        def _():
            acc_ref[...] = product()

        @pl.when(kk > 0)
        def _():
            acc_ref[...] += product()

        @pl.when(kk == nk - 1)
        def _():
            finish(acc_ref[...])

    if mode == "tn":
        a_spec = pl.BlockSpec((tk, tm), lambda i, j, kk: (kk, i))
    else:
        a_spec = pl.BlockSpec((tm, tk), lambda i, j, kk: (i, kk))
    if b_blocked and mode == "nn":
        per = cs // tn
        b_spec = pl.BlockSpec((None, tk, tn), lambda i, j, kk: (j // per, kk, j % per))
    elif b_blocked:
        per = cs // tk
        b_spec = pl.BlockSpec((None, tn, tk), lambda i, j, kk: (kk // per, j, kk % per))
    elif mode == "nt":
        assert row0 % tn == 0
        b_spec = pl.BlockSpec((tn, tk), lambda i, j, kk: (j + row0 // tn, kk))
    else:
        b_spec = pl.BlockSpec((tk, tn), lambda i, j, kk: (kk, j))
    tile = pl.BlockSpec((tm, tn), lambda i, j, kk: (i, j))
    if out_blocked:
        per_o = n // out_blocked // tn
        out_spec = pl.BlockSpec((None, tm, tn), lambda i, j, kk: (j // per_o, i, j % per_o))
        out_shape = [SDS((out_blocked, m, n // out_blocked), dt) for dt in out_dtypes]
    else:
        out_spec, out_shape = tile, [SDS((m, n), dt) for dt in out_dtypes]
    in_specs = [a_spec, b_spec] + [tile] * n_e + [pl.BlockSpec((1, tn), lambda i, j, kk: (0, j))] * n_v
    res = _pcall(name, body, (m // tm, n // tn, nk), in_specs, [a, b, *extras, *vecs], [out_spec] * n_o, out_shape,
                 scratch=[pltpu.VMEM((tm, tn), F32)] if nk > 1 else [],
                 semantics=("parallel", "parallel", "arbitrary"), plan=plan)
    if plan is None:
        return res[0] if n_o == 1 else res
    return (res[0][0] if n_o == 1 else res[0]), res[1]


def _rms_fwd(x):
    r = lax.rsqrt(jnp.mean(x * x, axis=-1, keepdims=True) + EPS)
    return x * r, r


def _rms_bwd(dxn, xn, r):
    return r * (dxn - xn * jnp.mean(dxn * xn, axis=-1, keepdims=True))


_INV_SQRT2 = 1.0 / math.sqrt(2.0)
_INV_SQRT2PI = 1.0 / math.sqrt(2.0 * math.pi)


def _gelu(y):
    return 0.5 * y * (1.0 + lax.erf(y * _INV_SQRT2))


def _gelu_grad(y):
    return 0.5 * (1.0 + lax.erf(y * _INV_SQRT2)) + y * (_INV_SQRT2PI * jnp.exp(-0.5 * y * y))


def _sigmoid(z):
    return 1.0 / (1.0 + jnp.exp(-z))


def _adam_math(w, g, m, v):
    m = ADAM_B1 * m + (1.0 - ADAM_B1) * g
    v = ADAM_B2 * v + (1.0 - ADAM_B2) * (g * g)
    m_hat = m / (1.0 - ADAM_B1 ** ADAM_STEP)
    v_hat = v / (1.0 - ADAM_B2 ** ADAM_STEP)
    delta = -ADAM_LR * (m_hat / (jnp.sqrt(v_hat) + ADAM_EPS) + ADAM_WD * w)
    return delta, m, v


def _norm_mod_fwd(name, x, g, scale, shift):
    def fn(rows, vecs):
        (xv,), (gv, sc, sh) = rows, vecs
        xn, _ = _rms_fwd(xv)
        return [(xn * gv) * (1.0 + sc) + sh], []
    return _rowwise(name, fn, [x], [g, scale, shift], [(x.shape[1], BF16)])[0]


def _norm_mod_bwd(name, x, dh, dres, g, scale, gated=None, plan=None):
    d = x.shape[1]

    def fn(rows, vecs):
        xv, dhv, drv = rows[:3]
        gv, sc = vecs[:2]
        xn, r = _rms_fwd(xv)
        t = xn * gv
        dt = dhv * (1.0 + sc)
        dx = drv + _rms_bwd(dt * gv, xn, r)
        if gated is None:
            return [dx], [dhv * t, dhv, dt * xn]
        return [dx, dx * vecs[2]], [dhv * t, dhv, dt * xn, dx * rows[3].astype(F32)]
    extra_rows, extra_vecs = ([gated[0]], [gated[1]]) if gated is not None else ([], [])
    res = _rowwise(name, fn, [x, dh, dres] + extra_rows, [g, scale] + extra_vecs,
                   [(d, F32)] + [(d, BF16)] * len(extra_rows), [d] * (3 + len(extra_rows)), plan=plan)
    outs, rest = res if plan is not None else (res, None)
    n_rows = 1 + len(extra_rows)
    return (*outs[:n_rows], *[a.sum(0) for a in outs[n_rows:]]), rest


def _final_loss(name, x, tgt, g, val, gate):
    d = x.shape[1]

    def fn(rows, vecs):
        (xv, tv, vv), (gv, gate_v) = rows, vecs
        xn, r = _rms_fwd(xv)
        e = xn * gv - tv
        dy = e * (1.0 / d)
        dx = _rms_bwd(dy * gv, xn, r)
        return [dx, dx * gate_v], [e * e, dy * xn, dx * vv.astype(F32)]
    dx, dval, sq, dg, dgate = _rowwise(name, fn, [x, tgt, val], [g, gate], [(d, F32), (d, BF16)], [d, d, d])
    return dx, dval, 0.5 * jnp.sum(sq) / d, dg.sum(0), dgate.sum(0)


def _group_norm_fwd(name, attn, ssm, g_a, g_s):
    def fn(rows, vecs):
        (av, sv), (ga, gs) = rows, vecs
        return [jnp.concatenate([_rms_fwd(av)[0] * ga, _rms_fwd(sv)[0] * gs], axis=1)], []
    return _rowwise(name, fn, [attn, ssm], [g_a, g_s], [(attn.shape[1] + ssm.shape[1], BF16)])[0]


def _group_norm_bwd(name, attn, ssm, dmixed, g_a, g_s, plan=None):
    da_w, ds_w = attn.shape[1], ssm.shape[1]

    def fn(rows, vecs):
        (av, sv, dm), (ga, gs) = rows, vecs
        an, ra = _rms_fwd(av)
        sn, rs = _rms_fwd(sv)
        dma, dms = dm[:, :da_w], dm[:, da_w:]
        return [_rms_bwd(dma * ga, an, ra), _rms_bwd(dms * gs, sn, rs)], [dma * an, dms * sn]
    res = _rowwise(name, fn, [attn, ssm, dmixed], [g_a, g_s], [(da_w, F32), (ds_w, F32)], [da_w, ds_w], plan=plan)
    (dattn, dssm, dga, dgs), rest = res if plan is not None else (res, None)
    return (dattn, dssm, dga.sum(0), dgs.sum(0)), rest


def _gelu_fwd(name, y):
    def fn(rows, vecs):
        return [_gelu(rows[0])], []
    return _rowwise(name, fn, [y], [], [(y.shape[1], BF16)])[0]


def _glu_bwd(name, dout, y, z):
    d = y.shape[1]

    def fn(rows, vecs):
        dov, yv, zv = rows
        sg = _sigmoid(zv)
        dz = dov * _gelu(yv) * sg * (1.0 - sg)
        return [dz, dov * sg], [dz]
    dz, dyg, db = _rowwise(name, fn, [dout, y, z], [], [(d, BF16), (d, F32)], [d])
    return dz, dyg, db.sum(0)


def _adam_shard(name, parts, w, m, v):
    r, c = w.shape
    n_parts = sum(1 if p.ndim == 2 else p.shape[0] for p in parts)
    row_bytes = 2 * c * (n_parts * parts[0].dtype.itemsize + 7 * 4)
    tr = min(128, r)
    while tr > SUBLANES and tr * row_bytes > VMEM_LIMIT_BYTES // 2:
        tr //= 2
    assert r % tr == 0
    n_p = len(parts)

    def body(*refs):
        p_refs, (w_ref, m_ref, v_ref, g_out, d_out, m_out, v_out) = refs[:n_p], refs[n_p:]
        g = None
        for p_ref in p_refs:
            terms = [p_ref[...]] if len(p_ref.shape) == 2 else [p_ref[j] for j in range(p_ref.shape[0])]
            for t in terms:
                g = t.astype(F32) if g is None else g + t.astype(F32)
        delta, m_new, v_new = _adam_math(w_ref[...], g, m_ref[...], v_ref[...])
        g_out[...], d_out[...], m_out[...], v_out[...] = g, delta, m_new, v_new

    tile = pl.BlockSpec((tr, c), lambda i: (i, 0))
    p_specs = [tile if p.ndim == 2 else pl.BlockSpec((p.shape[0], tr, c), lambda i: (0, i, 0)) for p in parts]
    return _pcall(name, body, (r // tr,), p_specs + [tile] * 3, [*parts, w, m, v], [tile] * 4, [SDS((r, c), F32)] * 4,
                  semantics=("parallel",))


def _pair_add(name, blocks, from_sibling):
    _, r, c = blocks.shape
    tr = min(256, r)
    assert r % tr == 0

    def body(b0_ref, b1_ref, s_ref, o_ref):
        mine = jnp.where(lax.axis_index("c") == 0, b0_ref[...].astype(F32), b1_ref[...].astype(F32))
        o_ref[...] = (mine + s_ref[...].astype(F32)).astype(o_ref.dtype)

    core_block = lambda k: pl.BlockSpec((None, tr, c), lambda s, i: (2 * s + k, i, 0))
    slot = pl.BlockSpec((None, tr, c), lambda s, i: (s, i, 0))
    return _pcall(name, body, (4, r // tr), [core_block(0), core_block(1), slot], [blocks, blocks, from_sibling],
                  [slot], [SDS((4, r, c), blocks.dtype)], semantics=("parallel", "parallel"))[0]


def _ada_update(name, c_act_t, dmod, w, m, v, tr=128, plan=None):
    r, c = w.shape
    tr = min(tr, r)
    assert r % tr == 0

    def body(c_ref, d_ref, w_ref, m_ref, v_ref, g_out, d_out, m_out, v_out):
        g = jnp.dot(c_ref[...], d_ref[...], precision=lax.Precision.HIGHEST, preferred_element_type=F32)
        delta, m_new, v_new = _adam_math(w_ref[...], g, m_ref[...], v_ref[...])
        g_out[...], d_out[...], m_out[...], v_out[...] = g, delta, m_new, v_new

    tile = pl.BlockSpec((tr, c), lambda i: (i, 0))
    in_specs = [pl.BlockSpec((tr, N_DEV), lambda i: (i, 0)), pl.BlockSpec((N_DEV, c), lambda i: (0, 0)), tile, tile, tile]
    return _pcall(name, body, (r // tr,), in_specs, [c_act_t, dmod, w, m, v], [tile] * 4, [SDS((r, c), F32)] * 4,
                  semantics=("parallel",), plan=plan)


def _rotate_half(x):
    w = x.shape[1]
    half = HEAD_DIM // 2
    lane = lax.broadcasted_iota(jnp.int32, x.shape, 1)
    return jnp.where((lane % HEAD_DIM) < half, -pltpu.roll(x, w - half, 1), pltpu.roll(x, half, 1))


def _lane_tile(tab, w):
    return tab[:, :w] if w <= LANES else jnp.tile(tab, (1, w // LANES))


def _rope(x, cos, sin):
    return x * cos + _rotate_half(x) * sin


def _rope_t(dy, cos, sin):
    return dy * cos - _rotate_half(dy) * sin


def _band_mask(n):
    shape = (Q_PER_KV * WINDOW, 2 * WINDOW)
    i = lax.broadcasted_iota(jnp.int32, shape, 0) & (WINDOW - 1)
    j = lax.broadcasted_iota(jnp.int32, shape, 1)
    return (j > i) & (j <= i + WINDOW) & ((n > 0) | (j >= WINDOW))


def _stack_heads(x, hk):
    first = hk * Q_PER_KV
    return jnp.concatenate([x[:, (first + g) * HEAD_DIM:(first + g + 1) * HEAD_DIM] for g in range(Q_PER_KV)], axis=0)


def _stack_cols(ref, hk):
    first = hk * Q_PER_KV
    return jnp.concatenate([ref[:, first + g:first + g + 1] for g in range(Q_PER_KV)], axis=0)


def _stack_sinks(sink_ref, hk):
    first = hk * Q_PER_KV
    return jnp.concatenate([jnp.broadcast_to(sink_ref[0:1, first + g:first + g + 1], (WINDOW, 1))
                            for g in range(Q_PER_KV)], axis=0)


def _attn_specs(da, dkv, nb):
    cur = lambda n: (jnp.minimum(n, nb - 1), 0)
    prev = lambda n: (jnp.maximum(jnp.minimum(n, nb - 1) - 1, 0), 0)
    return dict(
        q=pl.BlockSpec((WINDOW, da), cur), kv_cur=pl.BlockSpec((WINDOW, dkv), cur),
        kv_prev=pl.BlockSpec((WINDOW, dkv), prev), tab_cur=pl.BlockSpec((WINDOW, LANES), cur),
        tab_prev=pl.BlockSpec((WINDOW, LANES), prev))


def _attn_fwd(name, q, k, v, cos, sin, sinks, plan=None):
    s, da = q.shape
    dkv = k.shape[1]
    nq, nb = da // HEAD_DIM, s // WINDOW
    scale = HEAD_DIM ** -0.5

    def body(q_ref, kp_ref, kc_ref, vp_ref, vc_ref, cc_ref, sc_ref, cp_ref, sp_ref, sink_ref, o_ref, lse_ref):
        n = pl.program_id(0)
        cc, sc, cp, sp = cc_ref[...], sc_ref[...], cp_ref[...], sp_ref[...]
        qr = _rope(q_ref[...], _lane_tile(cc, da), _lane_tile(sc, da)).astype(BF16)
        kk = jnp.concatenate([_rope(kp_ref[...], _lane_tile(cp, dkv), _lane_tile(sp, dkv)),
                              _rope(kc_ref[...], _lane_tile(cc, dkv), _lane_tile(sc, dkv))], axis=0).astype(BF16)
        vv = jnp.concatenate([vp_ref[...], vc_ref[...]], axis=0).astype(BF16)
        valid = _band_mask(n)
        for hk in range(nq // Q_PER_KV):
            ks = slice(hk * HEAD_DIM, (hk + 1) * HEAD_DIM)
            sco = lax.dot_general(_stack_heads(qr, hk), kk[:, ks], _DOT_DIMS["nt"], preferred_element_type=F32) * scale
            sco = jnp.where(valid, sco, -1e30)
            sink = _stack_sinks(sink_ref, hk)
            mx = jnp.maximum(jnp.max(sco, axis=1, keepdims=True), sink)
            p = jnp.exp(sco - mx)
            den = jnp.sum(p, axis=1, keepdims=True) + jnp.exp(sink - mx)
            o8 = jnp.dot((p / den).astype(BF16), vv[:, ks], preferred_element_type=F32)
            lse8 = mx + jnp.log(den)
            for g in range(Q_PER_KV):
                hq, rows = hk * Q_PER_KV + g, slice(g * WINDOW, (g + 1) * WINDOW)
                o_ref[:, hq * HEAD_DIM:(hq + 1) * HEAD_DIM] = o8[rows]
                lse_ref[:, hq:hq + 1] = lse8[rows]

    sp_ = _attn_specs(da, dkv, nb)
    in_specs = [sp_["q"], sp_["kv_prev"], sp_["kv_cur"], sp_["kv_prev"], sp_["kv_cur"],
                sp_["tab_cur"], sp_["tab_cur"], sp_["tab_prev"], sp_["tab_prev"], pl.BlockSpec((1, nq), lambda n: (0, 0))]
    return _pcall(name, body, (nb,), in_specs, [q, k, k, v, v, cos, sin, cos, sin, sinks],
                  [sp_["q"], pl.BlockSpec((WINDOW, nq), lambda n: (n, 0))], [SDS((s, da), F32), SDS((s, nq), F32)],
                  semantics=("arbitrary",), plan=plan)


def _attn_bwd(name, q, k, v, cos, sin, sinks, out, lse, dout, plan=None):
    s, da = q.shape
    dkv = k.shape[1]
    nq, nb = da // HEAD_DIM, s // WINDOW
    scale = HEAD_DIM ** -0.5

    def body(q_ref, kp_ref, kc_ref, vp_ref, vc_ref, cc_ref, sc_ref, cp_ref, sp_ref, sink_ref, o_ref, lse_ref,
             do_ref, dq_ref, dk_ref, dv_ref, dsink_ref, dk_carry, dv_carry):
        n = pl.program_id(0)
        cp, sp = _lane_tile(cp_ref[...], dkv), _lane_tile(sp_ref[...], dkv)

        @pl.when(n == 0)
        def _():
            dk_carry[...] = jnp.zeros_like(dk_carry)
            dv_carry[...] = jnp.zeros_like(dv_carry)
            dsink_ref[...] = jnp.zeros_like(dsink_ref)

        @pl.when(n < nb)
        def _():
            cc, sc = cc_ref[...], sc_ref[...]
            qr = _rope(q_ref[...], _lane_tile(cc, da), _lane_tile(sc, da)).astype(BF16)
            kk = jnp.concatenate([_rope(kp_ref[...], cp, sp),
                                  _rope(kc_ref[...], _lane_tile(cc, dkv), _lane_tile(sc, dkv))], axis=0).astype(BF16)
            vv = jnp.concatenate([vp_ref[...], vc_ref[...]], axis=0).astype(BF16)
            valid = _band_mask(n)
            do_all, o_all = do_ref[...], o_ref[...]
            for hk in range(nq // Q_PER_KV):
                ks = slice(hk * HEAD_DIM, (hk + 1) * HEAD_DIM)
                q8, lse8 = _stack_heads(qr, hk), _stack_cols(lse_ref, hk)
                sco = lax.dot_general(q8, kk[:, ks], _DOT_DIMS["nt"], preferred_element_type=F32) * scale
                probs = jnp.where(valid, jnp.exp(sco - lse8), 0.0)
                do8 = _stack_heads(do_all, hk)
                delta = jnp.sum(do8 * _stack_heads(o_all, hk), axis=1, keepdims=True)
                do8 = do8.astype(BF16)
                dp = lax.dot_general(do8, vv[:, ks], _DOT_DIMS["nt"], preferred_element_type=F32)
                ds = (probs * (dp - delta) * scale).astype(BF16)
                dq8 = jnp.dot(ds, kk[:, ks], preferred_element_type=F32)
                dk_h = lax.dot_general(ds, q8, _DOT_DIMS["tn"], preferred_element_type=F32)
                dv_h = lax.dot_general(probs.astype(BF16), do8, _DOT_DIMS["tn"], preferred_element_type=F32)
                dsink8 = -jnp.exp(_stack_sinks(sink_ref, hk) - lse8) * delta
                for g in range(Q_PER_KV):
                    hq, rows = hk * Q_PER_KV + g, slice(g * WINDOW, (g + 1) * WINDOW)
                    dq_ref[:, hq * HEAD_DIM:(hq + 1) * HEAD_DIM] = dq8[rows]
                    dsink_ref[:, hq:hq + 1] += dsink8[rows].reshape(WINDOW // SUBLANES, SUBLANES, 1).sum(axis=0)
                dk_ref[:, ks] = dk_carry[:, ks] + dk_h[:WINDOW]
                dv_ref[:, ks] = dv_carry[:, ks] + dv_h[:WINDOW]
                dk_carry[:, ks] = dk_h[WINDOW:]
                dv_carry[:, ks] = dv_h[WINDOW:]
            dq_ref[...] = _rope_t(dq_ref[...], _lane_tile(cc, da), _lane_tile(sc, da))
            dk_ref[...] = _rope_t(dk_ref[...], cp, sp)

        @pl.when(n == nb)
        def _():
            dk_ref[...] = _rope_t(dk_carry[...], cp, sp)
            dv_ref[...] = dv_carry[...]

    sp_ = _attn_specs(da, dkv, nb)
    last_prev = lambda n: (jnp.maximum(n - 1, 0), 0)
    tab_prev = pl.BlockSpec((WINDOW, LANES), last_prev)
    kv_out = pl.BlockSpec((WINDOW, dkv), last_prev)
    lse_spec = pl.BlockSpec((WINDOW, nq), lambda n: (jnp.minimum(n, nb - 1), 0))
    in_specs = [sp_["q"], sp_["kv_prev"], sp_["kv_cur"], sp_["kv_prev"], sp_["kv_cur"],
                sp_["tab_cur"], sp_["tab_cur"], tab_prev, tab_prev,
                pl.BlockSpec((1, nq), lambda n: (0, 0)), sp_["q"], lse_spec, sp_["q"]]
    res = _pcall(name, body, (nb + 1,), in_specs, [q, k, k, v, v, cos, sin, cos, sin, sinks, out, lse, dout],
                 [sp_["q"], kv_out, kv_out, pl.BlockSpec((SUBLANES, nq), lambda n: (0, 0))],
                 [SDS((s, da), F32), SDS((s, dkv), F32), SDS((s, dkv), F32), SDS((SUBLANES, nq), F32)],
                 scratch=[pltpu.VMEM((WINDOW, dkv), F32), pltpu.VMEM((WINDOW, dkv), F32)],
                 semantics=("arbitrary",), plan=plan)
    (dq, dk, dv, dsink), rest = res if plan is not None else (res, None)
    return (dq, dk, dv, dsink.sum(0)), rest


def _ssm_operators(lam_re, lam_im, log_step, b_re, b_im, c_re, c_im, d_skip):
    g, p = lam_re.shape
    h = b_re.shape[-1]
    l = SSM_CHUNK
    step = jnp.exp(log_step)[:, None]
    mag = jnp.exp(lam_re * step)
    ar, ai = mag * jnp.cos(lam_im * step), mag * jnp.sin(lam_im * step)
    den = lam_re * lam_re + lam_im * lam_im
    cr = ((ar - 1.0) * lam_re + ai * lam_im) / den
    ci = (ai * lam_re - (ar - 1.0) * lam_im) / den
    bbr = cr[..., None] * b_re - ci[..., None] * b_im
    bbi = cr[..., None] * b_im + ci[..., None] * b_re
    pr, pi = [jnp.ones_like(ar)], [jnp.zeros_like(ar)]
    for _ in range(l):
        pr, pi = pr + [pr[-1] * ar - pi[-1] * ai], pi + [pr[-1] * ai + pi[-1] * ar]
    pwr, pwi = jnp.stack(pr, axis=1), jnp.stack(pi, axis=1)
    cpr = c_re[:, None] * pwr[:, :, None, :] - c_im[:, None] * pwi[:, :, None, :]
    cpi = c_re[:, None] * pwi[:, :, None, :] + c_im[:, None] * pwr[:, :, None, :]
    kern = (jnp.einsum("gtop,gpi->gtoi", cpr[:, :l], bbr, precision=lax.Precision.HIGHEST)
            - jnp.einsum("gtop,gpi->gtoi", cpi[:, :l], bbi, precision=lax.Precision.HIGHEST))
    kern = kern.at[:, 0].add(d_skip.reshape(g, h)[:, :, None] * jnp.eye(h, dtype=F32))
    tm = jnp.stack([jnp.pad(kern[:, :l - j], ((0, 0), (j, 0), (0, 0), (0, 0))) for j in range(l)], axis=1)
    tm = tm.transpose(0, 1, 4, 2, 3).reshape(g, l * h, l * h)
    rev_r, rev_i = pwr[:, l - 1::-1][:, :l], pwi[:, l - 1::-1][:, :l]
    er = rev_r[:, :, None, :] * bbr.transpose(0, 2, 1)[:, None] - rev_i[:, :, None, :] * bbi.transpose(0, 2, 1)[:, None]
    ei = rev_r[:, :, None, :] * bbi.transpose(0, 2, 1)[:, None] + rev_i[:, :, None, :] * bbr.transpose(0, 2, 1)[:, None]
    em = jnp.concatenate([er, ei], axis=-1).reshape(g, l * h, 2 * p)
    fr = cpr[:, 1:].transpose(0, 3, 1, 2).reshape(g, p, l * h)
    fi = -cpi[:, 1:].transpose(0, 3, 1, 2).reshape(g, p, l * h)
    fm = jnp.concatenate([fr, fi], axis=1)
    return tm, em, fm, pwr[:, l], pwi[:, l]


def _decay_lanes(alr, ali):
    return jnp.concatenate([alr, alr], axis=1), jnp.concatenate([-ali, ali], axis=1)


def _column_blocks(s, ds):
    wide = 2 if (ds // LANES) % 2 == 0 else 1
    return wide, pl.BlockSpec((s, wide * LANES), lambda i: (0, i // wide))


def _my_columns(wide, block_ref, stage_ref, store):
    part = pl.program_id(0) % wide
    for p in range(wide):
        @pl.when(part == p)
        def _():
            cols = slice(p * LANES, (p + 1) * LANES)
            if store:
                block_ref[:, cols] = stage_ref[...]
            else:
                stage_ref[...] = block_ref[:, cols]


def _ssm_fwd(name, u, tm, em, fm, acat, bcat, plan=None):
    s, ds = u.shape
    g, lh, p2 = em.shape
    gb, h = SSM_GROUPS_PER_STEP, lh // SSM_CHUNK
    assert gb * h == LANES and g * h == ds and s % SSM_CHUNK == 0
    nc, half = s // SSM_CHUNK, p2 // 2

    def body(u_ref, tm_ref, em_ref, fm_ref, a_ref, b_ref, y_ref, xp_ref, uc_ref, yc_ref, st_ref, col_ref):
        _my_columns(wide, u_ref, col_ref, store=False)
        _to_chunks(col_ref, uc_ref, nc, h)
        for i in range(gb):
            st_ref[pl.ds(i, nc, stride=gb), :] = jnp.dot(uc_ref[i], em_ref[i], precision=SSM_PRECISION,
                                                         preferred_element_type=F32)
        av, bv = a_ref[...], b_ref[...]

        def step(c, carry):
            x, xs = carry
            rows = pl.ds(pl.multiple_of(c * gb, gb), gb)
            loc = st_ref[rows, :]
            st_ref[rows, :] = x
            return av * x + bv * xs + loc, av * xs - bv * x + pltpu.roll(loc, half, 1)
        zero = jnp.zeros((gb, p2), F32)
        lax.fori_loop(0, nc, step, (zero, zero), unroll=4)
        for i in range(gb):
            xp = st_ref[pl.ds(i, nc, stride=gb), :]
            xp_ref[i] = xp
            yc_ref[i] = (jnp.dot(uc_ref[i], tm_ref[i], precision=SSM_PRECISION, preferred_element_type=F32)
                         + jnp.dot(xp, fm_ref[i], precision=SSM_PRECISION, preferred_element_type=F32))
        _from_chunks(yc_ref, col_ref, nc, h)
        _my_columns(wide, y_ref, col_ref, store=True)

    blk = lambda r, c: pl.BlockSpec((gb, r, c), lambda i: (i, 0, 0))
    vec = pl.BlockSpec((gb, p2), lambda i: (i, 0))
    wide, col = _column_blocks(s, ds)
    return _pcall(name, body, (g // gb,), [col, blk(lh, lh), blk(lh, p2), blk(p2, lh), vec, vec],
                  [u, tm, em, fm, acat, bcat], [col, blk(nc, p2), blk(nc, lh)],
                  [SDS((s, ds), F32), SDS((g, nc, p2), F32), SDS((g, nc, lh), F32)],
                  scratch=[pltpu.VMEM((gb, nc, lh), F32), pltpu.VMEM((nc * gb, p2), F32), pltpu.VMEM((s, LANES), F32)],
                  semantics=("arbitrary",), plan=plan)


def _ssm_bwd(name, u_chunks, dy, xprev, tm, em, fm, acat, bcat, plan=None):
    s, ds = dy.shape
    g, lh, p2 = em.shape
    gb, h = SSM_GROUPS_PER_STEP, lh // SSM_CHUNK
    nc, half = s // SSM_CHUNK, p2 // 2

    def body(uc_ref, dy_ref, xp_ref, tm_ref, em_ref, fm_ref, a_ref, b_ref,
             du_ref, dtm_ref, dem_ref, dfm_ref, r1_ref, r2_ref, dyc_ref, duc_ref, gs_ref, xs_ref, col_ref):
        _my_columns(wide, dy_ref, col_ref, store=False)
        _to_chunks(col_ref, dyc_ref, nc, h)
        for i in range(gb):
            gs_ref[pl.ds(i, nc, stride=gb), :] = lax.dot_general(
                dyc_ref[i], fm_ref[i], _DOT_DIMS["nt"], precision=SSM_PRECISION, preferred_element_type=F32)
            xs_ref[pl.ds(i, nc, stride=gb), :] = xp_ref[i]
        av, bv = a_ref[...], b_ref[...]

        def step(t, carry):
            grad, gsw, r1, r2 = carry
            c = nc - 1 - t
            rows = pl.ds(pl.multiple_of(c * gb, gb), gb)
            dxp, xp = gs_ref[rows, :], xs_ref[rows, :]
            gs_ref[rows, :] = grad
            r1 = r1 + grad * xp
            r2 = r2 + grad * pltpu.roll(xp, half, 1)
            return dxp + av * grad - bv * gsw, pltpu.roll(dxp, half, 1) + av * gsw + bv * grad, r1, r2
        zero = jnp.zeros((gb, p2), F32)
        _, _, r1, r2 = lax.fori_loop(0, nc, step, (zero, zero, zero, zero), unroll=4)
        r1_ref[...], r2_ref[...] = r1, r2
        for i in range(gb):
            dxl = gs_ref[pl.ds(i, nc, stride=gb), :]
            duc_ref[i] = (lax.dot_general(dyc_ref[i], tm_ref[i], _DOT_DIMS["nt"], precision=SSM_PRECISION,
                                          preferred_element_type=F32)
                          + lax.dot_general(dxl, em_ref[i], _DOT_DIMS["nt"], precision=SSM_PRECISION,
                                            preferred_element_type=F32))
            dtm_ref[i] = lax.dot_general(uc_ref[i], dyc_ref[i], _DOT_DIMS["tn"], precision=SSM_PRECISION,
                                         preferred_element_type=F32)
            dfm_ref[i] = lax.dot_general(xp_ref[i], dyc_ref[i], _DOT_DIMS["tn"], precision=SSM_PRECISION,
                                         preferred_element_type=F32)
            dem_ref[i] = lax.dot_general(uc_ref[i], dxl, _DOT_DIMS["tn"], precision=SSM_PRECISION,
                                         preferred_element_type=F32)
        _from_chunks(duc_ref, col_ref, nc, h)
        _my_columns(wide, du_ref, col_ref, store=True)

    blk = lambda r, c: pl.BlockSpec((gb, r, c), lambda i: (i, 0, 0))
    vec = pl.BlockSpec((gb, p2), lambda i: (i, 0))
    wide, col = _column_blocks(s, ds)
    chunked = pltpu.VMEM((gb, nc, lh), F32)
    res = _pcall(name, body, (g // gb,),
                 [blk(nc, lh), col, blk(nc, p2), blk(lh, lh), blk(lh, p2), blk(p2, lh), vec, vec],
                 [u_chunks, dy, xprev, tm, em, fm, acat, bcat],
                 [col, blk(lh, lh), blk(lh, p2), blk(p2, lh), vec, vec],
                 [SDS((s, ds), F32), SDS((g, lh, lh), F32), SDS((g, lh, p2), F32), SDS((g, p2, lh), F32),
                  SDS((g, p2), F32), SDS((g, p2), F32)],
                 scratch=[chunked, chunked, pltpu.VMEM((nc * gb, p2), F32), pltpu.VMEM((nc * gb, p2), F32),
                          pltpu.VMEM((s, LANES), F32)],
                 semantics=("arbitrary",), plan=plan)
    return res if plan is not None else (res, None)


def _to_chunks(src_ref, dst_ref, nc, h):
    per = LANES // h
    rows = min(nc, 64)
    grp = lax.broadcasted_iota(jnp.int32, (rows, LANES), 1) // h
    for r0 in range(0, nc, rows):
        for i in range(SSM_CHUNK):
            part, lo = divmod(i * h, LANES)
            step_rows = src_ref[pl.ds(r0 * SSM_CHUNK + i, rows, stride=SSM_CHUNK), :]
            for g in range(per):
                shift = (lo - g * h) % LANES
                piece = pltpu.roll(step_rows, shift, 1) if shift else step_rows
                out = dst_ref.at[g, r0:r0 + rows, part * LANES:(part + 1) * LANES]
                out[...] = piece if lo == 0 else jnp.where(grp == lo // h, piece, out[...])


def _from_chunks(src_ref, dst_ref, nc, h):
    per = LANES // h
    grp = lax.broadcasted_iota(jnp.int32, (nc, LANES), 1) // h
    for i in range(SSM_CHUNK):
        part, lo = divmod(i * h, LANES)
        row = None
        for g in range(per):
            piece = src_ref[g, :, part * LANES:(part + 1) * LANES]
            if (g * h - lo) % LANES:
                piece = pltpu.roll(piece, (g * h - lo) % LANES, 1)
            row = piece if row is None else jnp.where(grp == g, piece, row)
        dst_ref[pl.ds(i, nc, stride=SSM_CHUNK), :] = row


_SMALL = ("b_ada", "norm1_g", "sinks", "ssm_lam_re", "ssm_lam_im", "ssm_log_step", "ssm_b_re", "ssm_b_im",
          "ssm_c_re", "ssm_c_im", "ssm_d", "b_glu", "attn_out_g", "ssm_out_g", "norm2_g", "final_g")
_WEIGHTS = ("w_ada", "b_ada", "norm1_g", "w_in", "sinks", "ssm_lam_re", "ssm_lam_im", "ssm_log_step", "ssm_b_re",
            "ssm_b_im", "ssm_c_re", "ssm_c_im", "ssm_d", "w_glu", "b_glu", "attn_out_g", "ssm_out_g", "w_out",
            "norm2_g", "w_ff1", "w_ff2", "final_g")
_PACK_ALIGN = 128 * LANES


def _pack(parts):
    flat = jnp.concatenate([p.reshape(-1).astype(F32) for p in parts])
    pad = (-flat.shape[0]) % _PACK_ALIGN
    return jnp.pad(flat, (0, pad)).reshape(-1, LANES)


def kernel(x, c, w_ada, b_ada, norm1_g, w_in, sinks, ssm_lam_re, ssm_lam_im, ssm_log_step, ssm_b_re, ssm_b_im, ssm_c_re, ssm_c_im, ssm_d, w_glu, b_glu, attn_out_g, ssm_out_g, w_out, norm2_g, w_ff1, w_ff2, final_g, loss_target, m_w_ada, m_b_ada, m_norm1_g, m_w_in, m_sinks, m_ssm_lam_re, m_ssm_lam_im, m_ssm_log_step, m_ssm_b_re, m_ssm_b_im, m_ssm_c_re, m_ssm_c_im, m_ssm_d, m_w_glu, m_b_glu, m_attn_out_g, m_ssm_out_g, m_w_out, m_norm2_g, m_w_ff1, m_w_ff2, m_final_g, v_w_ada, v_b_ada, v_norm1_g, v_w_in, v_sinks, v_ssm_lam_re, v_ssm_lam_im, v_ssm_log_step, v_ssm_b_re, v_ssm_b_im, v_ssm_c_re, v_ssm_c_im, v_ssm_d, v_w_glu, v_b_glu, v_attn_out_g, v_ssm_out_g, v_w_out, v_norm2_g, v_w_ff1, v_w_ff2, v_final_g):
    args = dict(locals())
    weights = {n: args[n] for n in _WEIGHTS}
    mom = {n: args["m_" + n] for n in _WEIGHTS}
    var = {n: args["v_" + n] for n in _WEIGHTS}
    me = 4 * lax.axis_index("x") + 2 * lax.axis_index("y") + lax.axis_index("c")

    _, s, d = x.shape
    xs, tgt = x[0], loss_target[0]
    d_ssm = ssm_d.shape[-1]
    d_attn = d - d_ssm
    nq = d_attn // HEAD_DIM
    d_kv = (nq // Q_PER_KV) * HEAD_DIM
    p_state = ssm_b_re.shape[2]

    c_all, g_in = _run_plan("gather_c_w_in", _Gather([c, w_in[0].T.astype(BF16)]))
    c_all = c_all.reshape(N_DEV, d)
    w_in_t = g_in.reshape(-1, d)

    n_loc = w_ada.shape[-1]
    b_loc = lax.dynamic_slice_in_dim(b_ada, me * n_loc, n_loc, axis=1)
    silu = lambda t: t * _sigmoid(t)
    mod_part = _matmul("ada_mod", c_all, w_ada[0], "nn", [F32], a_pro=silu, vecs=[b_loc], exact=True,
                       epilogue=lambda acc, e, v: (acc + v[0],), tn=512, tk=d)
    mod_all = _run_plan("gather_mod", _Gather([mod_part]))[0]
    mod = lax.dynamic_index_in_dim(mod_all, me, axis=1, keepdims=False).reshape(N_MOD, 1, d)
    shift1, scale1, gate1, shift2, scale2, gate2 = [mod[i] for i in range(N_MOD)]

    h1 = _norm_mod_fwd("norm1", xs, norm1_g, scale1, shift1)
    q = _matmul("proj_q", h1, w_in_t, "nt", [F32], b_rows=(0, d_attn))
    kv = _matmul("proj_kv", h1, w_in_t, "nt", [F32], b_rows=(d_attn, 2 * d_kv))
    u = _matmul("proj_u", h1, w_in_t, "nt", [F32], b_rows=(d_attn + 2 * d_kv, d_ssm))
    k, v = kv[:, :d_kv], kv[:, d_kv:]

    half = HEAD_DIM // 2
    inv_freq = ROPE_THETA ** (-jnp.arange(half, dtype=F32) / half)
    ang = jnp.arange(s, dtype=F32)[:, None] * inv_freq[None, :]
    cos_t, sin_t = jnp.tile(jnp.cos(ang), (1, 4)), jnp.tile(jnp.sin(ang), (1, 4))
    (attn, lse), (g_glu, g_out) = _attn_fwd("attn_fwd", q, k, v, cos_t, sin_t, sinks,
                                            plan=_Gather([w_glu[0].astype(BF16), w_out[0].astype(BF16)]))
    w_glu_f = g_glu.reshape(d_ssm, d_ssm)
    w_out_f = g_out.reshape(d, d)

    ssm_params = (ssm_lam_re[0], ssm_lam_im[0], ssm_log_step[0], ssm_b_re[0], ssm_b_im[0], ssm_c_re[0],
                  ssm_c_im[0], ssm_d[0])
    (tm_op, em_op, fm_op, alr, ali), ssm_vjp = jax.vjp(_ssm_operators, *ssm_params)
    acat, bcat = _decay_lanes(alr, ali)
    (y_ssm, x_prev, u_chunks), (g_ff1,) = _ssm_fwd("ssm_fwd", u, tm_op, em_op, fm_op, acat, bcat,
                                                   plan=_Gather([w_ff1[0].astype(BF16)]))
    yg = _gelu_fwd("gelu", y_ssm)
    ssm_out, z_glu = _matmul(
        "glu", yg, w_glu_f, "nn", [F32, F32], extras=[y_ssm], vecs=[b_glu],
        epilogue=lambda acc, e, v: (_gelu(e[0]) * _sigmoid(acc + v[0]), acc + v[0]))
    mixed = _group_norm_fwd("group_norm", attn, ssm_out, attn_out_g, ssm_out_g)
    x2, mo = _matmul("out_proj", mixed, w_out_f, "nn", [F32, BF16], extras=[xs], vecs=[gate1],
                     epilogue=lambda acc, e, v: (e[0] + v[0] * acc, acc))

    h2 = _norm_mod_fwd("norm2", x2, norm2_g, scale2, shift2)
    (a_ff, f_ff), (g_ff2,) = _matmul("ff1", h2, g_ff1, "nn", [BF16, BF16], b_blocked=True,
                                     epilogue=lambda acc, e, v: (acc, jnp.square(jnp.maximum(acc, 0.0))),
                                     plan=_Gather([w_ff2[0].astype(BF16)]))
    w_ff2_f = g_ff2.reshape(-1, d)
    x3, ff = _matmul("ff2", f_ff, w_ff2_f, "nn", [F32, BF16], extras=[x2], vecs=[gate2],
                     epilogue=lambda acc, e, v: (e[0] + v[0] * acc, acc))

    dx3, dff, loss_local, d_final_g, d_gate2 = _final_loss("final_loss", x3, tgt, final_g.reshape(1, d), ff, gate2)
    loss = lax.psum(loss_local, MESH_AXES)

    dw_ff2 = _matmul("ff2_dw", f_ff, dff, "tn", [BF16]).reshape(N_DEV, -1, d)
    da_ff, (p_ff2,) = _matmul("ff2_dx", dff, w_ff2_f, "nt", [BF16], extras=[a_ff],
                              epilogue=lambda acc, e, v: (acc * (2.0 * jnp.maximum(e[0].astype(F32), 0.0)),),
                              plan=_PairSwap([dw_ff2]))
    s_ff2 = _pair_add("pair_add_ff2", dw_ff2, p_ff2)
    dw_ff1, (r_ff2_a,) = _matmul("ff1_dw", h2, da_ff, "tn", [BF16], out_blocked=N_DEV,
                                 plan=_ChipExchange([s_ff2], (CHIP_X, CHIP_Y)))
    dh2, (r_ff2_b,) = _matmul("ff1_dx", da_ff, g_ff1, "nt", [F32], b_blocked=True,
                              plan=_ChipExchange([s_ff2], (CHIP_DIAGONAL,)))
    (dx2, dmo, d_scale2, d_shift2, d_norm2_g, d_gate1), (p_ff1,) = _norm_mod_bwd(
        "norm2_bwd", x2, dh2, dx3, norm2_g, scale2, gated=(mo, gate1), plan=_PairSwap([dw_ff1]))
    s_ff1 = _pair_add("pair_add_ff1", dw_ff1, p_ff1)

    dw_out = _matmul("out_dw", mixed, dmo, "tn", [BF16]).reshape(N_DEV, -1, d)
    dmixed, (r_out_a,) = _matmul("out_dx", dmo, w_out_f, "nt", [F32], plan=_Exchange([dw_out], (1, 4, 2)))
    (dattn, dssm_out, d_attn_g, d_ssm_g), (r_out_b,) = _group_norm_bwd(
        "group_norm_bwd", attn, ssm_out, dmixed, attn_out_g, ssm_out_g, plan=_Exchange([dw_out], (6,)))

    dz, dyg_direct, d_b_glu = _glu_bwd("glu_bwd", dssm_out, y_ssm, z_glu)
    dw_glu = _matmul("glu_dw", yg, dz, "tn", [BF16]).reshape(N_DEV, -1, d_ssm)
    dy_ssm, (r_glu,) = _matmul("glu_dx", dz, w_glu_f, "nt", [F32], extras=[dyg_direct, y_ssm],
                               epilogue=lambda acc, e, v: ((acc + e[0]) * _gelu_grad(e[1]),),
                               plan=_Exchange([dw_glu], RELATIONS_ALL))
    (du, d_tm, d_em, d_fm, r1, r2), (r_ff1_a,) = _ssm_bwd(
        "ssm_bwd", u_chunks, dy_ssm, x_prev, tm_op, em_op, fm_op, acat, bcat,
        plan=_ChipExchange([s_ff1], (CHIP_X, CHIP_Y)))
    d_alr = r1[:, :p_state] + r1[:, p_state:]
    d_ali = r2[:, p_state:] - r2[:, :p_state]
    d_ssm_params = ssm_vjp((d_tm, d_em, d_fm, d_alr, d_ali))

    (dq, dk, dv, d_sinks), (r_ff1_b,) = _attn_bwd("attn_bwd", q, k, v, cos_t, sin_t, sinks, attn, lse, dattn,
                                                  plan=_ChipExchange([s_ff1], (CHIP_DIAGONAL,)))
    dproj = jnp.concatenate([dq, dk, dv, du], axis=1).astype(BF16)
    dw_in_t, (r_out_c,) = _matmul("in_dw", dproj, h1, "tn", [BF16], plan=_Exchange([dw_out], (5, 3, 7)))
    dw_in_t = dw_in_t.reshape(N_DEV, -1, d)
    dh1, (r_in_a,) = _matmul("in_dx", dproj, w_in_t, "nn", [F32], plan=_Exchange([dw_in_t], RELATIONS_SAME_CORE))
    (grad_x, d_scale1, d_shift1, d_norm1_g), _ = _norm_mod_bwd("norm1_bwd", xs, dh1, dx2, norm1_g, scale1)

    d_mod = jnp.concatenate([d_shift1, d_scale1, d_gate1, d_shift2, d_scale2, d_gate2])
    small_g = dict(zip(("ssm_lam_re", "ssm_lam_im", "ssm_log_step", "ssm_b_re", "ssm_b_im", "ssm_c_re", "ssm_c_im",
                        "ssm_d"), d_ssm_params, strict=True))
    small_g.update(b_ada=d_mod, norm1_g=d_norm1_g, sinks=d_sinks, b_glu=d_b_glu, attn_out_g=d_attn_g,
                   ssm_out_g=d_ssm_g, norm2_g=d_norm2_g, final_g=d_final_g)
    small_parts, r_in_b = _run_plan("gather_small_grads", _Plans([_Gather([_pack([small_g[n] for n in _SMALL])]),
                                                                  _Exchange([dw_in_t], RELATIONS_OTHER_CORE)]))
    small = _adam_shard("adam_small", [small_parts], _pack([weights[n] for n in _SMALL]),
                        _pack([mom[n] for n in _SMALL]), _pack([var[n] for n in _SMALL]))
    out = {}
    off = 0
    for n in _SMALL:
        size = weights[n].size
        out[n] = [t.reshape(-1)[off:off + size].reshape(weights[n].shape) for t in small]
        off += size

    dmod_all = small_parts.reshape(N_DEV, -1)[:, :N_MOD * d]
    dmod_loc = lax.dynamic_slice_in_dim(dmod_all, me * n_loc, n_loc, axis=1)
    c_act_t = silu(c_all).T
    out["w_ada"] = [t[None] for t in _ada_update("adam_w_ada", c_act_t, dmod_loc, w_ada[0], m_w_ada[0], v_w_ada[0])]

    mine = lambda blocks: lax.dynamic_index_in_dim(blocks, me, axis=0, keepdims=False)
    in_parts = [mine(dw_in_t).T] + [r.transpose(0, 2, 1) for r in (r_in_a, r_in_b)]
    my_chip = 2 * lax.axis_index("x") + lax.axis_index("y")
    chip_sum = lambda sums: lax.dynamic_index_in_dim(sums, my_chip, axis=0, keepdims=False)
    received = dict(w_in=in_parts, w_glu=[mine(dw_glu), r_glu], w_out=[mine(dw_out), r_out_a, r_out_b, r_out_c],
                    w_ff1=[chip_sum(s_ff1), r_ff1_a, r_ff1_b], w_ff2=[chip_sum(s_ff2), r_ff2_a, r_ff2_b])
    for n, parts in received.items():
        out[n] = [t[None] for t in _adam_shard("adam_" + n, parts, weights[n][0], mom[n][0], var[n][0])]

    return (loss, grad_x[None], *[out[n][0] for n in _WEIGHTS], *[out[n][1] for n in _WEIGHTS],
            *[out[n][2] for n in _WEIGHTS], *[out[n][3] for n in _WEIGHTS])
```

```python
import math

import jax
import jax.numpy as jnp
from jax import lax
from jax.experimental import pallas as pl
from jax.experimental.pallas import tpu as pltpu

F32, BF16 = jnp.float32, jnp.bfloat16
SDS = jax.ShapeDtypeStruct
MESH_AXES = ("x", "y", "c")
N_DEV = 8
VMEM_LIMIT_BYTES = 56 * 1024 * 1024
MATMUL_VMEM_BUDGET = 44 * 1024 * 1024
SUBLANES, LANES = 8, 128

HEAD_DIM = 64
Q_PER_KV = 8
WINDOW = 128
ROPE_THETA = 10000.0
EPS = 1e-6
N_MOD = 6
SSM_CHUNK = 16
SSM_GROUPS_PER_STEP = 8

ADAM_LR, ADAM_B1, ADAM_B2, ADAM_EPS, ADAM_WD, ADAM_STEP = 0.001, 0.9, 0.999, 1e-08, 0.01, 10
HIGHEST = lax.Precision.HIGHEST
SSM_PRECISION = lax.Precision.HIGH

RELATIONS_ALL = (1, 4, 2, 6, 5, 3, 7)
RELATIONS_SAME_CORE = (1, 4, 2, 6)
RELATIONS_OTHER_CORE = (5, 3, 7)
PLAN_MIDDLE = 0.65


def _cparams(sem):
    return pltpu.CompilerParams(dimension_semantics=sem, vmem_limit_bytes=VMEM_LIMIT_BYTES)


def _block_index(p):
    return 4 * p[0] + 2 * p[1] + p[2]


def _me():
    return lax.axis_index("x"), lax.axis_index("y"), lax.axis_index("c")


class _Plan:
    def middle(self, ins, outs, send, recv, local):
        pass


class _Gather(_Plan):
    TO_SIBLING, TO_X, TO_Y, RELAY, PASS_X, PASS_Y, PASS_DIAGONAL = range(7)

    def __init__(self, arrs):
        self.ins = list(arrs)
        self.out_shapes = [SDS((N_DEV,) + a.shape, a.dtype) for a in arrs]
        self.n_rdma, self.n_local = 7 * len(arrs), len(arrs)
        self.rdma_base = self.local_base = 0

    def _copy(self, ins, outs, send, recv, a, k, block, to, from_input=False):
        dst = outs[a].at[_block_index(block)]
        sem = self.rdma_base + a * 7 + k
        return pltpu.make_async_remote_copy(
            src_ref=ins[a] if from_input else dst, dst_ref=dst, send_sem=send.at[sem], recv_sem=recv.at[sem],
            device_id=to, device_id_type=pl.DeviceIdType.MESH)

    @staticmethod
    def _places():
        x, y, c = _me()
        return (x, y, c), (x, y, 1 - c), (1 - x, y, c), (x, 1 - y, c), (1 - x, 1 - y, c)

    def _first(self, ins, outs, send, recv, a):
        me, sibling, x_nbr, y_nbr, _ = self._places()
        return [self._copy(ins, outs, send, recv, a, k, me, to, True)
                for k, to in ((self.TO_SIBLING, sibling), (self.TO_X, x_nbr), (self.TO_Y, y_nbr))]

    def _mine(self, ins, outs, local, a):
        return pltpu.make_async_copy(ins[a], outs[a].at[_block_index(_me())], local.at[self.local_base + a])

    def start(self, ins, outs, send, recv, local):
        for a in range(len(ins)):
            self._mine(ins, outs, local, a).start()
            for cp in self._first(ins, outs, send, recv, a):
                cp.start()

    def middle(self, ins, outs, send, recv, local):
        me, sibling, x_nbr, y_nbr, _ = self._places()
        core = me[2]
        for a in range(len(ins)):
            self._copy(ins, outs, send, recv, a, self.TO_X, x_nbr, me).wait_recv()
            self._copy(ins, outs, send, recv, a, self.TO_Y, y_nbr, me).wait_recv()

            @pl.when(core == 0)
            def _():
                self._copy(ins, outs, send, recv, a, self.RELAY, x_nbr, y_nbr).start()

            @pl.when(core == 1)
            def _():
                self._copy(ins, outs, send, recv, a, self.RELAY, y_nbr, x_nbr).start()

            self._copy(ins, outs, send, recv, a, self.PASS_X, x_nbr, sibling).start()
            self._copy(ins, outs, send, recv, a, self.PASS_Y, y_nbr, sibling).start()

    def finish(self, ins, outs, send, recv, local):
        me, sibling, x_nbr, y_nbr, diagonal = self._places()
        other = lambda p: (p[0], p[1], 1 - p[2])
        for a in range(len(ins)):
            self._copy(ins, outs, send, recv, a, self.RELAY, diagonal, me).wait_recv()
            self._copy(ins, outs, send, recv, a, self.PASS_DIAGONAL, diagonal, sibling).start()
        for a in range(len(ins)):
            self._copy(ins, outs, send, recv, a, self.TO_SIBLING, sibling, me).wait_recv()
            for k, src in ((self.PASS_X, x_nbr), (self.PASS_Y, y_nbr), (self.PASS_DIAGONAL, diagonal)):
                self._copy(ins, outs, send, recv, a, k, other(src), me).wait_recv()
                self._copy(ins, outs, send, recv, a, k, src, sibling).wait_send()
            for cp in self._first(ins, outs, send, recv, a):
                cp.wait_send()
            self._copy(ins, outs, send, recv, a, self.RELAY, me, me).wait_send()
            self._mine(ins, outs, local, a).wait()


class _Exchange(_Plan):
    def __init__(self, arrs, relations):
        self.ins, self.relations = list(arrs), tuple(relations)
        self.out_shapes = [SDS((len(relations),) + a.shape[1:], a.dtype) for a in arrs]
        self.n_rdma, self.n_local = len(relations) * len(arrs), 0
        self.rdma_base = self.local_base = 0

    def _copies(self, ins, outs, send, recv):
        x, y, c = _me()
        cps = []
        for a in range(len(ins)):
            for s, k in enumerate(self.relations):
                peer = ((1 - x) if (k & 4) else x, (1 - y) if (k & 2) else y, (1 - c) if (k & 1) else c)
                sem = self.rdma_base + a * len(self.relations) + s
                cps.append(pltpu.make_async_remote_copy(
                    src_ref=ins[a].at[_block_index(peer)], dst_ref=outs[a].at[s], send_sem=send.at[sem],
                    recv_sem=recv.at[sem], device_id=peer, device_id_type=pl.DeviceIdType.MESH))
        return cps

    def start(self, ins, outs, send, recv, local):
        for cp in self._copies(ins, outs, send, recv):
            cp.start()

    def finish(self, ins, outs, send, recv, local):
        for cp in self._copies(ins, outs, send, recv):
            cp.wait()


class _PairSwap(_Plan):
    def __init__(self, arrs):
        self.ins = list(arrs)
        self.out_shapes = [SDS((4,) + a.shape[1:], a.dtype) for a in arrs]
        self.n_rdma, self.n_local = 4 * len(arrs), 0
        self.rdma_base = self.local_base = 0

    def _copies(self, ins, outs, send, recv):
        x, y, c = _me()
        cps = []
        for a in range(len(ins)):
            for s in range(4):
                sem = self.rdma_base + a * 4 + s
                cps.append(pltpu.make_async_remote_copy(
                    src_ref=ins[a].at[2 * s + (1 - c)], dst_ref=outs[a].at[s], send_sem=send.at[sem],
                    recv_sem=recv.at[sem], device_id=(x, y, 1 - c), device_id_type=pl.DeviceIdType.MESH))
        return cps

    def start(self, ins, outs, send, recv, local):
        for cp in self._copies(ins, outs, send, recv):
            cp.start()

    def finish(self, ins, outs, send, recv, local):
        for cp in self._copies(ins, outs, send, recv):
            cp.wait()


CHIP_X, CHIP_Y, CHIP_DIAGONAL = (1, 0), (0, 1), (1, 1)


class _ChipExchange(_Plan):
    def __init__(self, arrs, hops):
        self.ins, self.hops = list(arrs), tuple(hops)
        self.out_shapes = [SDS((len(hops),) + a.shape[1:], a.dtype) for a in arrs]
        self.n_rdma, self.n_local = len(hops) * len(arrs), 0
        self.rdma_base = self.local_base = 0

    def _copies(self, ins, outs, send, recv):
        x, y, c = _me()
        cps = []
        for a in range(len(ins)):
            for s, (fx, fy) in enumerate(self.hops):
                px, py = (1 - x) if fx else x, (1 - y) if fy else y
                sem = self.rdma_base + a * len(self.hops) + s
                cps.append(pltpu.make_async_remote_copy(
                    src_ref=ins[a].at[2 * px + py], dst_ref=outs[a].at[s], send_sem=send.at[sem],
                    recv_sem=recv.at[sem], device_id=(px, py, c), device_id_type=pl.DeviceIdType.MESH))
        return cps

    def start(self, ins, outs, send, recv, local):
        for cp in self._copies(ins, outs, send, recv):
            cp.start()

    def finish(self, ins, outs, send, recv, local):
        for cp in self._copies(ins, outs, send, recv):
            cp.wait()


class _Plans:
    def __init__(self, plans):
        self.plans = list(plans)
        self.ins = [a for p in plans for a in p.ins]
        self.out_shapes = [s for p in plans for s in p.out_shapes]
        self.n_rdma = self.n_local = 0
        for p in plans:
            p.rdma_base, p.local_base = self.n_rdma, self.n_local
            self.n_rdma, self.n_local = self.n_rdma + p.n_rdma, self.n_local + p.n_local

    def _each(self, ins, outs):
        i = o = 0
        for p in self.plans:
            yield p, ins[i:i + len(p.ins)], outs[o:o + len(p.out_shapes)]
            i, o = i + len(p.ins), o + len(p.out_shapes)

    def start(self, ins, outs, send, recv, local):
        for p, pi, po in self._each(ins, outs):
            p.start(pi, po, send, recv, local)

    def middle(self, ins, outs, send, recv, local):
        for p, pi, po in self._each(ins, outs):
            p.middle(pi, po, send, recv, local)

    def finish(self, ins, outs, send, recv, local):
        for p, pi, po in self._each(ins, outs):
            p.finish(pi, po, send, recv, local)


def _plan_scratch(plan):
    return [pltpu.SemaphoreType.DMA((plan.n_rdma,)), pltpu.SemaphoreType.DMA((plan.n_rdma,)),
            pltpu.SemaphoreType.DMA((max(plan.n_local, 1),))]


def _run_plan(name, plan):
    n = len(plan.ins)

    def body(*refs):
        ins, outs, sems = refs[:n], refs[n:len(refs) - 3], refs[len(refs) - 3:]
        plan.start(ins, outs, *sems)
        plan.middle(ins, outs, *sems)
        plan.finish(ins, outs, *sems)

    any_spec = pl.BlockSpec(memory_space=pl.ANY)
    return pl.pallas_call(body, name=name, out_shape=list(plan.out_shapes), in_specs=[any_spec] * n,
                          out_specs=[any_spec] * len(plan.out_shapes), scratch_shapes=_plan_scratch(plan))(*plan.ins)


def _pcall(name, body, grid, in_specs, ins, out_specs, out_shape, scratch=(), semantics=None, plan=None):
    if plan is None:
        return pl.pallas_call(body, name=name, grid=grid, in_specs=list(in_specs), out_specs=list(out_specs),
                              out_shape=list(out_shape), scratch_shapes=list(scratch),
                              compiler_params=_cparams(semantics))(*ins)
    n_in, n_out, n_scr = len(ins), len(out_shape), len(scratch)
    p_in, p_out = len(plan.ins), len(plan.out_shapes)

    def with_plan(*refs):
        k_in, c_in = refs[:n_in], refs[n_in:n_in + p_in]
        refs = refs[n_in + p_in:]
        k_out, c_out = refs[:n_out], refs[n_out:n_out + p_out]
        refs = refs[n_out + p_out:]
        k_scr, sems = refs[:n_scr], refs[n_scr:]
        step = 0
        for d, g in enumerate(grid):
            step = step * g + pl.program_id(d)
        n_steps = math.prod(grid)

        @pl.when(step == 0)
        def _():
            plan.start(c_in, c_out, *sems)

        @pl.when(step == min(n_steps - 1, int(n_steps * PLAN_MIDDLE)))
        def _():
            plan.middle(c_in, c_out, *sems)

        body(*k_in, *k_out, *k_scr)

        @pl.when(step == n_steps - 1)
        def _():
            plan.finish(c_in, c_out, *sems)

    any_spec = pl.BlockSpec(memory_space=pl.ANY)
    res = pl.pallas_call(
        with_plan, name=name, grid=grid, in_specs=list(in_specs) + [any_spec] * p_in,
        out_specs=list(out_specs) + [any_spec] * p_out, out_shape=list(out_shape) + list(plan.out_shapes),
        scratch_shapes=list(scratch) + _plan_scratch(plan),
        compiler_params=_cparams(("arbitrary",) * len(grid)))(*ins, *plan.ins)
    return res[:n_out], res[n_out:]


def _rowwise(name, fn, rows, vecs, row_outs, acc_outs=(), tm=128, plan=None):
    t = rows[0].shape[0]
    tm = min(tm, t)
    assert t % tm == 0 and tm % SUBLANES == 0
    n_r, n_v, n_o = len(rows), len(vecs), len(row_outs)

    def body(*refs):
        r_in, v_in = refs[:n_r], refs[n_r:n_r + n_v]
        r_out, a_out = refs[n_r + n_v:n_r + n_v + n_o], refs[n_r + n_v + n_o:]
        outs, accs = fn([r[...] for r in r_in], [v[...] for v in v_in])
        for o_ref, o in zip(r_out, outs, strict=True):
            o_ref[...] = o.astype(o_ref.dtype)
        if a_out:
            @pl.when(pl.program_id(0) == 0)
            def _():
                for a_ref in a_out:
                    a_ref[...] = jnp.zeros_like(a_ref)
            for a_ref, a in zip(a_out, accs, strict=True):
                a_ref[...] += a.reshape(tm // SUBLANES, SUBLANES, a.shape[-1]).sum(axis=0)

    in_specs = [pl.BlockSpec((tm, r.shape[1]), lambda i: (i, 0)) for r in rows]
    in_specs += [pl.BlockSpec(v.shape, lambda i: (0, 0)) for v in vecs]
    out_specs = [pl.BlockSpec((tm, w), lambda i: (i, 0)) for w, _ in row_outs]
    out_specs += [pl.BlockSpec((SUBLANES, w), lambda i: (0, 0)) for w in acc_outs]
    out_shape = [SDS((t, w), dt) for w, dt in row_outs] + [SDS((SUBLANES, w), F32) for w in acc_outs]
    return _pcall(name, body, (t // tm,), in_specs, [*rows, *vecs], out_specs, out_shape, semantics=("arbitrary",),
                  plan=plan)


def _tile(n, want):
    if n <= want:
        return n
    for t in range(want // LANES * LANES, 0, -LANES):
        if n % t == 0:
            return t
    raise ValueError(f"no tile for {n}")


_DOT_DIMS = {"nn": (((1,), (0,)), ((), ())), "nt": (((1,), (1,)), ((), ())), "tn": (((0,), (0,)), ((), ()))}


def _matmul(name, a, b, mode, out_dtypes, epilogue=None, extras=(), vecs=(), a_pro=None,
            tm=1024, tn=512, tk=4096, exact=False, b_blocked=False, out_blocked=0, b_rows=None, plan=None):
    cs = b.shape[-1] if b_blocked else None
    b2 = (b.shape[1], b.shape[0] * b.shape[2]) if b_blocked else b.shape
    if mode == "tn":
        (k, m), (k2, n) = a.shape, b2
    elif mode == "nt":
        (m, k), (n, k2) = a.shape, b2
    else:
        (m, k), (k2, n) = a.shape, b2
    assert k == k2 and not (b_blocked and mode == "tn")
    row0 = 0
    if b_rows is not None:
        assert mode == "nt" and not b_blocked
        row0, n = b_rows
        tn = _tile(math.gcd(n, row0) if row0 else n, tn)
    tm, tn, tk = _tile(m, tm), _tile(n, tn), _tile(k, tk)
    if b_blocked and mode == "nn":
        tn = _tile(cs, tn)
    if b_blocked and mode == "nt":
        tk = _tile(cs, tk)
    if out_blocked:
        tn = _tile(n // out_blocked, tn)
    nk = k // tk

    def vmem_bytes(width):
        operands = 2 * (tm * tk * a.dtype.itemsize + tk * width * b.dtype.itemsize)
        tiles = 2 * tm * width * (sum(jnp.dtype(dt).itemsize for dt in out_dtypes) + sum(e.dtype.itemsize for e in extras))
        return operands + tiles + tm * width * 4 * (2 if nk > 1 else 1)

    extent = cs if (b_blocked and mode == "nn") else n // out_blocked if out_blocked else n
    if extent % (2 * tn) == 0 and row0 % (2 * tn) == 0 and vmem_bytes(2 * tn) <= MATMUL_VMEM_BUDGET:
        tn *= 2
    n_e, n_v, n_o = len(extras), len(vecs), len(out_dtypes)
    precision = HIGHEST if exact else None

    def body(*refs):
        a_ref, b_ref = refs[:2]
        e_refs, v_refs = refs[2:2 + n_e], refs[2 + n_e:2 + n_e + n_v]
        o_refs = refs[2 + n_e + n_v:2 + n_e + n_v + n_o]

        def product():
            av = a_ref[...]
            if a_pro is not None:
                av = a_pro(av)
            return lax.dot_general(av, b_ref[...], _DOT_DIMS[mode], precision=precision, preferred_element_type=F32)

        def finish(acc):
            res = (acc,) if epilogue is None else epilogue(acc, [e[...] for e in e_refs], [v[...] for v in v_refs])
            for o_ref, r in zip(o_refs, res, strict=True):
                o_ref[...] = r.astype(o_ref.dtype)

        if nk == 1:
            finish(product())
            return
        acc_ref = refs[-1]
        kk = pl.program_id(2)

        @pl.when(kk == 0)
        def _():
            acc_ref[...] = product()

        @pl.when(kk > 0)
        def _():
            acc_ref[...] += product()

        @pl.when(kk == nk - 1)
        def _():
            finish(acc_ref[...])

    if mode == "tn":
        a_spec = pl.BlockSpec((tk, tm), lambda i, j, kk: (kk, i))
    else:
        a_spec = pl.BlockSpec((tm, tk), lambda i, j, kk: (i, kk))
    if b_blocked and mode == "nn":
        per = cs // tn
        b_spec = pl.BlockSpec((None, tk, tn), lambda i, j, kk: (j // per, kk, j % per))
    elif b_blocked:
        per = cs // tk
        b_spec = pl.BlockSpec((None, tn, tk), lambda i, j, kk: (kk // per, j, kk % per))
    elif mode == "nt":
        assert row0 % tn == 0
        b_spec = pl.BlockSpec((tn, tk), lambda i, j, kk: (j + row0 // tn, kk))
    else:
        b_spec = pl.BlockSpec((tk, tn), lambda i, j, kk: (kk, j))
    tile = pl.BlockSpec((tm, tn), lambda i, j, kk: (i, j))
    if out_blocked:
        per_o = n // out_blocked // tn
        out_spec = pl.BlockSpec((None, tm, tn), lambda i, j, kk: (j // per_o, i, j % per_o))
        out_shape = [SDS((out_blocked, m, n // out_blocked), dt) for dt in out_dtypes]
    else:
        out_spec, out_shape = tile, [SDS((m, n), dt) for dt in out_dtypes]
    in_specs = [a_spec, b_spec] + [tile] * n_e + [pl.BlockSpec((1, tn), lambda i, j, kk: (0, j))] * n_v
    res = _pcall(name, body, (m // tm, n // tn, nk), in_specs, [a, b, *extras, *vecs], [out_spec] * n_o, out_shape,
                 scratch=[pltpu.VMEM((tm, tn), F32)] if nk > 1 else [],
                 semantics=("parallel", "parallel", "arbitrary"), plan=plan)
    if plan is None:
        return res[0] if n_o == 1 else res
    return (res[0][0] if n_o == 1 else res[0]), res[1]


def _rms_fwd(x):
    r = lax.rsqrt(jnp.mean(x * x, axis=-1, keepdims=True) + EPS)
    return x * r, r


def _rms_bwd(dxn, xn, r):
    return r * (dxn - xn * jnp.mean(dxn * xn, axis=-1, keepdims=True))


_INV_SQRT2 = 1.0 / math.sqrt(2.0)
_INV_SQRT2PI = 1.0 / math.sqrt(2.0 * math.pi)


def _gelu(y):
    return 0.5 * y * (1.0 + lax.erf(y * _INV_SQRT2))


def _gelu_grad(y):
    return 0.5 * (1.0 + lax.erf(y * _INV_SQRT2)) + y * (_INV_SQRT2PI * jnp.exp(-0.5 * y * y))


def _sigmoid(z):
    return 1.0 / (1.0 + jnp.exp(-z))


def _adam_math(w, g, m, v):
    m = ADAM_B1 * m + (1.0 - ADAM_B1) * g
    v = ADAM_B2 * v + (1.0 - ADAM_B2) * (g * g)
    m_hat = m / (1.0 - ADAM_B1 ** ADAM_STEP)
    v_hat = v / (1.0 - ADAM_B2 ** ADAM_STEP)
    delta = -ADAM_LR * (m_hat / (jnp.sqrt(v_hat) + ADAM_EPS) + ADAM_WD * w)
    return delta, m, v


def _norm_mod_fwd(name, x, g, scale, shift):
    def fn(rows, vecs):
        (xv,), (gv, sc, sh) = rows, vecs
        xn, _ = _rms_fwd(xv)
        return [(xn * gv) * (1.0 + sc) + sh], []
    return _rowwise(name, fn, [x], [g, scale, shift], [(x.shape[1], BF16)])[0]


def _norm_mod_bwd(name, x, dh, dres, g, scale, gated=None, plan=None):
    d = x.shape[1]

    def fn(rows, vecs):
        xv, dhv, drv = rows[:3]
        gv, sc = vecs[:2]
        xn, r = _rms_fwd(xv)
        t = xn * gv
        dt = dhv * (1.0 + sc)
        dx = drv + _rms_bwd(dt * gv, xn, r)
        if gated is None:
            return [dx], [dhv * t, dhv, dt * xn]
        return [dx, dx * vecs[2]], [dhv * t, dhv, dt * xn, dx * rows[3].astype(F32)]
    extra_rows, extra_vecs = ([gated[0]], [gated[1]]) if gated is not None else ([], [])
    res = _rowwise(name, fn, [x, dh, dres] + extra_rows, [g, scale] + extra_vecs,
                   [(d, F32)] + [(d, BF16)] * len(extra_rows), [d] * (3 + len(extra_rows)), plan=plan)
    outs, rest = res if plan is not None else (res, None)
    n_rows = 1 + len(extra_rows)
    return (*outs[:n_rows], *[a.sum(0) for a in outs[n_rows:]]), rest


def _final_loss(name, x, tgt, g, val, gate):
    d = x.shape[1]

    def fn(rows, vecs):
        (xv, tv, vv), (gv, gate_v) = rows, vecs
        xn, r = _rms_fwd(xv)
        e = xn * gv - tv
        dy = e * (1.0 / d)
        dx = _rms_bwd(dy * gv, xn, r)
        return [dx, dx * gate_v], [e * e, dy * xn, dx * vv.astype(F32)]
    dx, dval, sq, dg, dgate = _rowwise(name, fn, [x, tgt, val], [g, gate], [(d, F32), (d, BF16)], [d, d, d])
    return dx, dval, 0.5 * jnp.sum(sq) / d, dg.sum(0), dgate.sum(0)


def _group_norm_fwd(name, attn, ssm, g_a, g_s):
    def fn(rows, vecs):
        (av, sv), (ga, gs) = rows, vecs
        return [jnp.concatenate([_rms_fwd(av)[0] * ga, _rms_fwd(sv)[0] * gs], axis=1)], []
    return _rowwise(name, fn, [attn, ssm], [g_a, g_s], [(attn.shape[1] + ssm.shape[1], BF16)])[0]


def _group_norm_bwd(name, attn, ssm, dmixed, g_a, g_s, plan=None):
    da_w, ds_w = attn.shape[1], ssm.shape[1]

    def fn(rows, vecs):
        (av, sv, dm), (ga, gs) = rows, vecs
        an, ra = _rms_fwd(av)
        sn, rs = _rms_fwd(sv)
        dma, dms = dm[:, :da_w], dm[:, da_w:]
        return [_rms_bwd(dma * ga, an, ra), _rms_bwd(dms * gs, sn, rs)], [dma * an, dms * sn]
    res = _rowwise(name, fn, [attn, ssm, dmixed], [g_a, g_s], [(da_w, F32), (ds_w, F32)], [da_w, ds_w], plan=plan)
    (dattn, dssm, dga, dgs), rest = res if plan is not None else (res, None)
    return (dattn, dssm, dga.sum(0), dgs.sum(0)), rest


def _gelu_fwd(name, y):
    def fn(rows, vecs):
        return [_gelu(rows[0])], []
    return _rowwise(name, fn, [y], [], [(y.shape[1], BF16)])[0]


def _glu_bwd(name, dout, y, z):
    d = y.shape[1]

    def fn(rows, vecs):
        dov, yv, zv = rows
        sg = _sigmoid(zv)
        dz = dov * _gelu(yv) * sg * (1.0 - sg)
        return [dz, dov * sg], [dz]
    dz, dyg, db = _rowwise(name, fn, [dout, y, z], [], [(d, BF16), (d, F32)], [d])
    return dz, dyg, db.sum(0)


def _adam_shard(name, parts, w, m, v):
    r, c = w.shape
    n_parts = sum(1 if p.ndim == 2 else p.shape[0] for p in parts)
    row_bytes = 2 * c * (n_parts * parts[0].dtype.itemsize + 7 * 4)
    tr = min(128, r)
    while tr > SUBLANES and tr * row_bytes > VMEM_LIMIT_BYTES // 2:
        tr //= 2
    assert r % tr == 0
    n_p = len(parts)

    def body(*refs):
        p_refs, (w_ref, m_ref, v_ref, g_out, d_out, m_out, v_out) = refs[:n_p], refs[n_p:]
        g = None
        for p_ref in p_refs:
            terms = [p_ref[...]] if len(p_ref.shape) == 2 else [p_ref[j] for j in range(p_ref.shape[0])]
            for t in terms:
                g = t.astype(F32) if g is None else g + t.astype(F32)
        delta, m_new, v_new = _adam_math(w_ref[...], g, m_ref[...], v_ref[...])
        g_out[...], d_out[...], m_out[...], v_out[...] = g, delta, m_new, v_new

    tile = pl.BlockSpec((tr, c), lambda i: (i, 0))
    p_specs = [tile if p.ndim == 2 else pl.BlockSpec((p.shape[0], tr, c), lambda i: (0, i, 0)) for p in parts]
    return _pcall(name, body, (r // tr,), p_specs + [tile] * 3, [*parts, w, m, v], [tile] * 4, [SDS((r, c), F32)] * 4,
                  semantics=("parallel",))


def _pair_add(name, blocks, from_sibling):
    _, r, c = blocks.shape
    tr = min(256, r)
    assert r % tr == 0

    def body(b0_ref, b1_ref, s_ref, o_ref):
        mine = jnp.where(lax.axis_index("c") == 0, b0_ref[...].astype(F32), b1_ref[...].astype(F32))
        o_ref[...] = (mine + s_ref[...].astype(F32)).astype(o_ref.dtype)

    core_block = lambda k: pl.BlockSpec((None, tr, c), lambda s, i: (2 * s + k, i, 0))
    slot = pl.BlockSpec((None, tr, c), lambda s, i: (s, i, 0))
    return _pcall(name, body, (4, r // tr), [core_block(0), core_block(1), slot], [blocks, blocks, from_sibling],
                  [slot], [SDS((4, r, c), blocks.dtype)], semantics=("parallel", "parallel"))[0]


def _ada_update(name, c_act_t, dmod, w, m, v, tr=128, plan=None):
    r, c = w.shape
    tr = min(tr, r)
    assert r % tr == 0

    def body(c_ref, d_ref, w_ref, m_ref, v_ref, g_out, d_out, m_out, v_out):
        g = jnp.dot(c_ref[...], d_ref[...], precision=lax.Precision.HIGHEST, preferred_element_type=F32)
        delta, m_new, v_new = _adam_math(w_ref[...], g, m_ref[...], v_ref[...])
        g_out[...], d_out[...], m_out[...], v_out[...] = g, delta, m_new, v_new

    tile = pl.BlockSpec((tr, c), lambda i: (i, 0))
    in_specs = [pl.BlockSpec((tr, N_DEV), lambda i: (i, 0)), pl.BlockSpec((N_DEV, c), lambda i: (0, 0)), tile, tile, tile]
    return _pcall(name, body, (r // tr,), in_specs, [c_act_t, dmod, w, m, v], [tile] * 4, [SDS((r, c), F32)] * 4,
                  semantics=("parallel",), plan=plan)


def _rotate_half(x):
    w = x.shape[1]
    half = HEAD_DIM // 2
    lane = lax.broadcasted_iota(jnp.int32, x.shape, 1)
    return jnp.where((lane % HEAD_DIM) < half, -pltpu.roll(x, w - half, 1), pltpu.roll(x, half, 1))


def _lane_tile(tab, w):
    return tab[:, :w] if w <= LANES else jnp.tile(tab, (1, w // LANES))


def _rope(x, cos, sin):
    return x * cos + _rotate_half(x) * sin


def _rope_t(dy, cos, sin):
    return dy * cos - _rotate_half(dy) * sin


def _band_mask(n):
    shape = (Q_PER_KV * WINDOW, 2 * WINDOW)
    i = lax.broadcasted_iota(jnp.int32, shape, 0) & (WINDOW - 1)
    j = lax.broadcasted_iota(jnp.int32, shape, 1)
    return (j > i) & (j <= i + WINDOW) & ((n > 0) | (j >= WINDOW))


def _stack_heads(x, hk):
    first = hk * Q_PER_KV
    return jnp.concatenate([x[:, (first + g) * HEAD_DIM:(first + g + 1) * HEAD_DIM] for g in range(Q_PER_KV)], axis=0)


def _stack_cols(ref, hk):
    first = hk * Q_PER_KV
    return jnp.concatenate([ref[:, first + g:first + g + 1] for g in range(Q_PER_KV)], axis=0)


def _stack_sinks(sink_ref, hk):
    first = hk * Q_PER_KV
    return jnp.concatenate([jnp.broadcast_to(sink_ref[0:1, first + g:first + g + 1], (WINDOW, 1))
                            for g in range(Q_PER_KV)], axis=0)


def _attn_specs(da, dkv, nb):
    cur = lambda n: (jnp.minimum(n, nb - 1), 0)
    prev = lambda n: (jnp.maximum(jnp.minimum(n, nb - 1) - 1, 0), 0)
    return dict(
        q=pl.BlockSpec((WINDOW, da), cur), kv_cur=pl.BlockSpec((WINDOW, dkv), cur),
        kv_prev=pl.BlockSpec((WINDOW, dkv), prev), tab_cur=pl.BlockSpec((WINDOW, LANES), cur),
        tab_prev=pl.BlockSpec((WINDOW, LANES), prev))


def _attn_fwd(name, q, k, v, cos, sin, sinks, plan=None):
    s, da = q.shape
    dkv = k.shape[1]
    nq, nb = da // HEAD_DIM, s // WINDOW
    scale = HEAD_DIM ** -0.5

    def body(q_ref, kp_ref, kc_ref, vp_ref, vc_ref, cc_ref, sc_ref, cp_ref, sp_ref, sink_ref, o_ref, lse_ref):
        n = pl.program_id(0)
        cc, sc, cp, sp = cc_ref[...], sc_ref[...], cp_ref[...], sp_ref[...]
        qr = _rope(q_ref[...], _lane_tile(cc, da), _lane_tile(sc, da)).astype(BF16)
        kk = jnp.concatenate([_rope(kp_ref[...], _lane_tile(cp, dkv), _lane_tile(sp, dkv)),
                              _rope(kc_ref[...], _lane_tile(cc, dkv), _lane_tile(sc, dkv))], axis=0).astype(BF16)
        vv = jnp.concatenate([vp_ref[...], vc_ref[...]], axis=0).astype(BF16)
        valid = _band_mask(n)
        for hk in range(nq // Q_PER_KV):
            ks = slice(hk * HEAD_DIM, (hk + 1) * HEAD_DIM)
            sco = lax.dot_general(_stack_heads(qr, hk), kk[:, ks], _DOT_DIMS["nt"], preferred_element_type=F32) * scale
            sco = jnp.where(valid, sco, -1e30)
            sink = _stack_sinks(sink_ref, hk)
            mx = jnp.maximum(jnp.max(sco, axis=1, keepdims=True), sink)
            p = jnp.exp(sco - mx)
            den = jnp.sum(p, axis=1, keepdims=True) + jnp.exp(sink - mx)
            o8 = jnp.dot((p / den).astype(BF16), vv[:, ks], preferred_element_type=F32)
            lse8 = mx + jnp.log(den)
            for g in range(Q_PER_KV):
                hq, rows = hk * Q_PER_KV + g, slice(g * WINDOW, (g + 1) * WINDOW)
                o_ref[:, hq * HEAD_DIM:(hq + 1) * HEAD_DIM] = o8[rows]
                lse_ref[:, hq:hq + 1] = lse8[rows]

    sp_ = _attn_specs(da, dkv, nb)
    in_specs = [sp_["q"], sp_["kv_prev"], sp_["kv_cur"], sp_["kv_prev"], sp_["kv_cur"],
                sp_["tab_cur"], sp_["tab_cur"], sp_["tab_prev"], sp_["tab_prev"], pl.BlockSpec((1, nq), lambda n: (0, 0))]
    return _pcall(name, body, (nb,), in_specs, [q, k, k, v, v, cos, sin, cos, sin, sinks],
                  [sp_["q"], pl.BlockSpec((WINDOW, nq), lambda n: (n, 0))], [SDS((s, da), F32), SDS((s, nq), F32)],
                  semantics=("arbitrary",), plan=plan)


def _attn_bwd(name, q, k, v, cos, sin, sinks, out, lse, dout, plan=None):
    s, da = q.shape
    dkv = k.shape[1]
    nq, nb = da // HEAD_DIM, s // WINDOW
    scale = HEAD_DIM ** -0.5

    def body(q_ref, kp_ref, kc_ref, vp_ref, vc_ref, cc_ref, sc_ref, cp_ref, sp_ref, sink_ref, o_ref, lse_ref,
             do_ref, dq_ref, dk_ref, dv_ref, dsink_ref, dk_carry, dv_carry):
        n = pl.program_id(0)
        cp, sp = _lane_tile(cp_ref[...], dkv), _lane_tile(sp_ref[...], dkv)

        @pl.when(n == 0)
        def _():
            dk_carry[...] = jnp.zeros_like(dk_carry)
            dv_carry[...] = jnp.zeros_like(dv_carry)
            dsink_ref[...] = jnp.zeros_like(dsink_ref)

        @pl.when(n < nb)
        def _():
            cc, sc = cc_ref[...], sc_ref[...]
            qr = _rope(q_ref[...], _lane_tile(cc, da), _lane_tile(sc, da)).astype(BF16)
            kk = jnp.concatenate([_rope(kp_ref[...], cp, sp),
                                  _rope(kc_ref[...], _lane_tile(cc, dkv), _lane_tile(sc, dkv))], axis=0).astype(BF16)
            vv = jnp.concatenate([vp_ref[...], vc_ref[...]], axis=0).astype(BF16)
            valid = _band_mask(n)
            do_all, o_all = do_ref[...], o_ref[...]
            for hk in range(nq // Q_PER_KV):
                ks = slice(hk * HEAD_DIM, (hk + 1) * HEAD_DIM)
                q8, lse8 = _stack_heads(qr, hk), _stack_cols(lse_ref, hk)
                sco = lax.dot_general(q8, kk[:, ks], _DOT_DIMS["nt"], preferred_element_type=F32) * scale
                probs = jnp.where(valid, jnp.exp(sco - lse8), 0.0)
                do8 = _stack_heads(do_all, hk)
                delta = jnp.sum(do8 * _stack_heads(o_all, hk), axis=1, keepdims=True)
                do8 = do8.astype(BF16)
                dp = lax.dot_general(do8, vv[:, ks], _DOT_DIMS["nt"], preferred_element_type=F32)
                ds = (probs * (dp - delta) * scale).astype(BF16)
                dq8 = jnp.dot(ds, kk[:, ks], preferred_element_type=F32)
                dk_h = lax.dot_general(ds, q8, _DOT_DIMS["tn"], preferred_element_type=F32)
                dv_h = lax.dot_general(probs.astype(BF16), do8, _DOT_DIMS["tn"], preferred_element_type=F32)
                dsink8 = -jnp.exp(_stack_sinks(sink_ref, hk) - lse8) * delta
                for g in range(Q_PER_KV):
                    hq, rows = hk * Q_PER_KV + g, slice(g * WINDOW, (g + 1) * WINDOW)
                    dq_ref[:, hq * HEAD_DIM:(hq + 1) * HEAD_DIM] = dq8[rows]
                    dsink_ref[:, hq:hq + 1] += dsink8[rows].reshape(WINDOW // SUBLANES, SUBLANES, 1).sum(axis=0)
                dk_ref[:, ks] = dk_carry[:, ks] + dk_h[:WINDOW]
                dv_ref[:, ks] = dv_carry[:, ks] + dv_h[:WINDOW]
                dk_carry[:, ks] = dk_h[WINDOW:]
                dv_carry[:, ks] = dv_h[WINDOW:]
            dq_ref[...] = _rope_t(dq_ref[...], _lane_tile(cc, da), _lane_tile(sc, da))
            dk_ref[...] = _rope_t(dk_ref[...], cp, sp)

        @pl.when(n == nb)
        def _():
            dk_ref[...] = _rope_t(dk_carry[...], cp, sp)
            dv_ref[...] = dv_carry[...]

    sp_ = _attn_specs(da, dkv, nb)
    last_prev = lambda n: (jnp.maximum(n - 1, 0), 0)
    tab_prev = pl.BlockSpec((WINDOW, LANES), last_prev)
    kv_out = pl.BlockSpec((WINDOW, dkv), last_prev)
    lse_spec = pl.BlockSpec((WINDOW, nq), lambda n: (jnp.minimum(n, nb - 1), 0))
    in_specs = [sp_["q"], sp_["kv_prev"], sp_["kv_cur"], sp_["kv_prev"], sp_["kv_cur"],
                sp_["tab_cur"], sp_["tab_cur"], tab_prev, tab_prev,
                pl.BlockSpec((1, nq), lambda n: (0, 0)), sp_["q"], lse_spec, sp_["q"]]
    res = _pcall(name, body, (nb + 1,), in_specs, [q, k, k, v, v, cos, sin, cos, sin, sinks, out, lse, dout],
                 [sp_["q"], kv_out, kv_out, pl.BlockSpec((SUBLANES, nq), lambda n: (0, 0))],
                 [SDS((s, da), F32), SDS((s, dkv), F32), SDS((s, dkv), F32), SDS((SUBLANES, nq), F32)],
                 scratch=[pltpu.VMEM((WINDOW, dkv), F32), pltpu.VMEM((WINDOW, dkv), F32)],
                 semantics=("arbitrary",), plan=plan)
    (dq, dk, dv, dsink), rest = res if plan is not None else (res, None)
    return (dq, dk, dv, dsink.sum(0)), rest


def _ssm_operators(lam_re, lam_im, log_step, b_re, b_im, c_re, c_im, d_skip):
    g, p = lam_re.shape
    h = b_re.shape[-1]
    l = SSM_CHUNK
    step = jnp.exp(log_step)[:, None]
    mag = jnp.exp(lam_re * step)
    ar, ai = mag * jnp.cos(lam_im * step), mag * jnp.sin(lam_im * step)
    den = lam_re * lam_re + lam_im * lam_im
    cr = ((ar - 1.0) * lam_re + ai * lam_im) / den
    ci = (ai * lam_re - (ar - 1.0) * lam_im) / den
    bbr = cr[..., None] * b_re - ci[..., None] * b_im
    bbi = cr[..., None] * b_im + ci[..., None] * b_re
    powers = jnp.arange(l + 1, dtype=F32)[None, :, None]
    pw_mag = jnp.exp((lam_re * step)[:, None, :] * powers)
    pw_ang = (lam_im * step)[:, None, :] * powers
    pwr, pwi = pw_mag * jnp.cos(pw_ang), pw_mag * jnp.sin(pw_ang)
    cpr = c_re[:, None] * pwr[:, :, None, :] - c_im[:, None] * pwi[:, :, None, :]
    cpi = c_re[:, None] * pwi[:, :, None, :] + c_im[:, None] * pwr[:, :, None, :]
    kern = (jnp.einsum("gtop,gpi->gtoi", cpr[:, :l], bbr, precision=lax.Precision.HIGHEST)
            - jnp.einsum("gtop,gpi->gtoi", cpi[:, :l], bbi, precision=lax.Precision.HIGHEST))
    kern = kern.at[:, 0].add(d_skip.reshape(g, h)[:, :, None] * jnp.eye(h, dtype=F32))
    lag = jnp.arange(l)
    place = (lag[None, None, :] - lag[None, :, None] == lag[:, None, None]).astype(F32)
    tm = jnp.einsum("gtoh,tji->gjhio", kern, place, precision=lax.Precision.HIGHEST).reshape(g, l * h, l * h)
    rev_r, rev_i = pwr[:, l - 1::-1][:, :l], pwi[:, l - 1::-1][:, :l]
    er = rev_r[:, :, None, :] * bbr.transpose(0, 2, 1)[:, None] - rev_i[:, :, None, :] * bbi.transpose(0, 2, 1)[:, None]
    ei = rev_r[:, :, None, :] * bbi.transpose(0, 2, 1)[:, None] + rev_i[:, :, None, :] * bbr.transpose(0, 2, 1)[:, None]
    em = jnp.concatenate([er, ei], axis=-1).reshape(g, l * h, 2 * p)
    fr = cpr[:, 1:].transpose(0, 3, 1, 2).reshape(g, p, l * h)
    fi = -cpi[:, 1:].transpose(0, 3, 1, 2).reshape(g, p, l * h)
    fm = jnp.concatenate([fr, fi], axis=1)
    return tm, em, fm, pwr[:, l], pwi[:, l]


def _decay_lanes(alr, ali):
    return jnp.concatenate([alr, alr], axis=1), jnp.concatenate([-ali, ali], axis=1)


def _column_blocks(s, ds):
    wide = 2 if (ds // LANES) % 2 == 0 else 1
    return wide, pl.BlockSpec((s, wide * LANES), lambda i: (0, i // wide))


def _my_columns(wide, block_ref, stage_ref, store):
    part = pl.program_id(0) % wide
    for p in range(wide):
        @pl.when(part == p)
        def _():
            cols = slice(p * LANES, (p + 1) * LANES)
            if store:
                block_ref[:, cols] = stage_ref[...]
            else:
                stage_ref[...] = block_ref[:, cols]


def _ssm_fwd(name, u, tm, em, fm, acat, bcat, plan=None):
    s, ds = u.shape
    g, lh, p2 = em.shape
    gb, h = SSM_GROUPS_PER_STEP, lh // SSM_CHUNK
    assert gb * h == LANES and g * h == ds and s % SSM_CHUNK == 0
    nc, half = s // SSM_CHUNK, p2 // 2

    def body(u_ref, tm_ref, em_ref, fm_ref, a_ref, b_ref, y_ref, xp_ref, uc_ref, yc_ref, st_ref, col_ref):
        _my_columns(wide, u_ref, col_ref, store=False)
        _to_chunks(col_ref, uc_ref, nc, h)
        for i in range(gb):
            st_ref[pl.ds(i, nc, stride=gb), :] = jnp.dot(uc_ref[i], em_ref[i], precision=SSM_PRECISION,
                                                         preferred_element_type=F32)
        av, bv = a_ref[...], b_ref[...]

        def step(c, carry):
            x, xs = carry
            rows = pl.ds(pl.multiple_of(c * gb, gb), gb)
            loc = st_ref[rows, :]
            st_ref[rows, :] = x
            return av * x + bv * xs + loc, av * xs - bv * x + pltpu.roll(loc, half, 1)
        zero = jnp.zeros((gb, p2), F32)
        lax.fori_loop(0, nc, step, (zero, zero), unroll=4)
        for i in range(gb):
            xp = st_ref[pl.ds(i, nc, stride=gb), :]
            xp_ref[i] = xp
            yc_ref[i] = (jnp.dot(uc_ref[i], tm_ref[i], precision=SSM_PRECISION, preferred_element_type=F32)
                         + jnp.dot(xp, fm_ref[i], precision=SSM_PRECISION, preferred_element_type=F32))
        _from_chunks(yc_ref, col_ref, nc, h)
        _my_columns(wide, y_ref, col_ref, store=True)

    blk = lambda r, c: pl.BlockSpec((gb, r, c), lambda i: (i, 0, 0))
    vec = pl.BlockSpec((gb, p2), lambda i: (i, 0))
    wide, col = _column_blocks(s, ds)
    return _pcall(name, body, (g // gb,), [col, blk(lh, lh), blk(lh, p2), blk(p2, lh), vec, vec],
                  [u, tm, em, fm, acat, bcat], [col, blk(nc, p2), blk(nc, lh)],
                  [SDS((s, ds), F32), SDS((g, nc, p2), F32), SDS((g, nc, lh), F32)],
                  scratch=[pltpu.VMEM((gb, nc, lh), F32), pltpu.VMEM((nc * gb, p2), F32), pltpu.VMEM((s, LANES), F32)],
                  semantics=("arbitrary",), plan=plan)


def _ssm_bwd(name, u_chunks, dy, xprev, tm, em, fm, acat, bcat, plan=None):
    s, ds = dy.shape
    g, lh, p2 = em.shape
    gb, h = SSM_GROUPS_PER_STEP, lh // SSM_CHUNK
    nc, half = s // SSM_CHUNK, p2 // 2

    def body(uc_ref, dy_ref, xp_ref, tm_ref, em_ref, fm_ref, a_ref, b_ref,
             du_ref, dtm_ref, dem_ref, dfm_ref, r1_ref, r2_ref, dyc_ref, duc_ref, gs_ref, xs_ref, col_ref):
        _my_columns(wide, dy_ref, col_ref, store=False)
        _to_chunks(col_ref, dyc_ref, nc, h)
        for i in range(gb):
            gs_ref[pl.ds(i, nc, stride=gb), :] = lax.dot_general(
                dyc_ref[i], fm_ref[i], _DOT_DIMS["nt"], precision=SSM_PRECISION, preferred_element_type=F32)
            xs_ref[pl.ds(i, nc, stride=gb), :] = xp_ref[i]
        av, bv = a_ref[...], b_ref[...]

        def step(t, carry):
            grad, gsw, r1, r2 = carry
            c = nc - 1 - t
            rows = pl.ds(pl.multiple_of(c * gb, gb), gb)
            dxp, xp = gs_ref[rows, :], xs_ref[rows, :]
            gs_ref[rows, :] = grad
            r1 = r1 + grad * xp
            r2 = r2 + grad * pltpu.roll(xp, half, 1)
            return dxp + av * grad - bv * gsw, pltpu.roll(dxp, half, 1) + av * gsw + bv * grad, r1, r2
        zero = jnp.zeros((gb, p2), F32)
        _, _, r1, r2 = lax.fori_loop(0, nc, step, (zero, zero, zero, zero), unroll=4)
        r1_ref[...], r2_ref[...] = r1, r2
        for i in range(gb):
            dxl = gs_ref[pl.ds(i, nc, stride=gb), :]
            duc_ref[i] = (lax.dot_general(dyc_ref[i], tm_ref[i], _DOT_DIMS["nt"], precision=SSM_PRECISION,
                                          preferred_element_type=F32)
                          + lax.dot_general(dxl, em_ref[i], _DOT_DIMS["nt"], precision=SSM_PRECISION,
                                            preferred_element_type=F32))
            dtm_ref[i] = lax.dot_general(uc_ref[i], dyc_ref[i], _DOT_DIMS["tn"], precision=SSM_PRECISION,
                                         preferred_element_type=F32)
            dfm_ref[i] = lax.dot_general(xp_ref[i], dyc_ref[i], _DOT_DIMS["tn"], precision=SSM_PRECISION,
                                         preferred_element_type=F32)
            dem_ref[i] = lax.dot_general(uc_ref[i], dxl, _DOT_DIMS["tn"], precision=SSM_PRECISION,
                                         preferred_element_type=F32)
        _from_chunks(duc_ref, col_ref, nc, h)
        _my_columns(wide, du_ref, col_ref, store=True)

    blk = lambda r, c: pl.BlockSpec((gb, r, c), lambda i: (i, 0, 0))
    vec = pl.BlockSpec((gb, p2), lambda i: (i, 0))
    wide, col = _column_blocks(s, ds)
    chunked = pltpu.VMEM((gb, nc, lh), F32)
    res = _pcall(name, body, (g // gb,),
                 [blk(nc, lh), col, blk(nc, p2), blk(lh, lh), blk(lh, p2), blk(p2, lh), vec, vec],
                 [u_chunks, dy, xprev, tm, em, fm, acat, bcat],
                 [col, blk(lh, lh), blk(lh, p2), blk(p2, lh), vec, vec],
                 [SDS((s, ds), F32), SDS((g, lh, lh), F32), SDS((g, lh, p2), F32), SDS((g, p2, lh), F32),
                  SDS((g, p2), F32), SDS((g, p2), F32)],
                 scratch=[chunked, chunked, pltpu.VMEM((nc * gb, p2), F32), pltpu.VMEM((nc * gb, p2), F32),
                          pltpu.VMEM((s, LANES), F32)],
                 semantics=("arbitrary",), plan=plan)
    return res if plan is not None else (res, None)


def _to_chunks(src_ref, dst_ref, nc, h):
    per = LANES // h
    rows = min(nc, 64)
    grp = lax.broadcasted_iota(jnp.int32, (rows, LANES), 1) // h
    for r0 in range(0, nc, rows):
        for i in range(SSM_CHUNK):
            part, lo = divmod(i * h, LANES)
            step_rows = src_ref[pl.ds(r0 * SSM_CHUNK + i, rows, stride=SSM_CHUNK), :]
            for g in range(per):
                shift = (lo - g * h) % LANES
                piece = pltpu.roll(step_rows, shift, 1) if shift else step_rows
                out = dst_ref.at[g, r0:r0 + rows, part * LANES:(part + 1) * LANES]
                out[...] = piece if lo == 0 else jnp.where(grp == lo // h, piece, out[...])


def _from_chunks(src_ref, dst_ref, nc, h):
    per = LANES // h
    grp = lax.broadcasted_iota(jnp.int32, (nc, LANES), 1) // h
    for i in range(SSM_CHUNK):
        part, lo = divmod(i * h, LANES)
        row = None
        for g in range(per):
            piece = src_ref[g, :, part * LANES:(part + 1) * LANES]
            if (g * h - lo) % LANES:
                piece = pltpu.roll(piece, (g * h - lo) % LANES, 1)
            row = piece if row is None else jnp.where(grp == g, piece, row)
        dst_ref[pl.ds(i, nc, stride=SSM_CHUNK), :] = row


_SMALL = ("b_ada", "norm1_g", "sinks", "ssm_lam_re", "ssm_lam_im", "ssm_log_step", "ssm_b_re", "ssm_b_im",
          "ssm_c_re", "ssm_c_im", "ssm_d", "b_glu", "attn_out_g", "ssm_out_g", "norm2_g", "final_g")
_WEIGHTS = ("w_ada", "b_ada", "norm1_g", "w_in", "sinks", "ssm_lam_re", "ssm_lam_im", "ssm_log_step", "ssm_b_re",
            "ssm_b_im", "ssm_c_re", "ssm_c_im", "ssm_d", "w_glu", "b_glu", "attn_out_g", "ssm_out_g", "w_out",
            "norm2_g", "w_ff1", "w_ff2", "final_g")
_PACK_ALIGN = 128 * LANES


def _pack(parts):
    flat = jnp.concatenate([p.reshape(-1).astype(F32) for p in parts])
    pad = (-flat.shape[0]) % _PACK_ALIGN
    return jnp.pad(flat, (0, pad)).reshape(-1, LANES)


def kernel(x, c, w_ada, b_ada, norm1_g, w_in, sinks, ssm_lam_re, ssm_lam_im, ssm_log_step, ssm_b_re, ssm_b_im, ssm_c_re, ssm_c_im, ssm_d, w_glu, b_glu, attn_out_g, ssm_out_g, w_out, norm2_g, w_ff1, w_ff2, final_g, loss_target, m_w_ada, m_b_ada, m_norm1_g, m_w_in, m_sinks, m_ssm_lam_re, m_ssm_lam_im, m_ssm_log_step, m_ssm_b_re, m_ssm_b_im, m_ssm_c_re, m_ssm_c_im, m_ssm_d, m_w_glu, m_b_glu, m_attn_out_g, m_ssm_out_g, m_w_out, m_norm2_g, m_w_ff1, m_w_ff2, m_final_g, v_w_ada, v_b_ada, v_norm1_g, v_w_in, v_sinks, v_ssm_lam_re, v_ssm_lam_im, v_ssm_log_step, v_ssm_b_re, v_ssm_b_im, v_ssm_c_re, v_ssm_c_im, v_ssm_d, v_w_glu, v_b_glu, v_attn_out_g, v_ssm_out_g, v_w_out, v_norm2_g, v_w_ff1, v_w_ff2, v_final_g):
    args = dict(locals())
    weights = {n: args[n] for n in _WEIGHTS}
    mom = {n: args["m_" + n] for n in _WEIGHTS}
    var = {n: args["v_" + n] for n in _WEIGHTS}
    me = 4 * lax.axis_index("x") + 2 * lax.axis_index("y") + lax.axis_index("c")

    _, s, d = x.shape
    xs, tgt = x[0], loss_target[0]
    d_ssm = ssm_d.shape[-1]
    d_attn = d - d_ssm
    nq = d_attn // HEAD_DIM
    d_kv = (nq // Q_PER_KV) * HEAD_DIM
    p_state = ssm_b_re.shape[2]

    c_all, g_in = _run_plan("gather_c_w_in", _Gather([c, w_in[0].T.astype(BF16)]))
    c_all = c_all.reshape(N_DEV, d)
    w_in_t = g_in.reshape(-1, d)

    n_loc = w_ada.shape[-1]
    b_loc = lax.dynamic_slice_in_dim(b_ada, me * n_loc, n_loc, axis=1)
    silu = lambda t: t * _sigmoid(t)
    mod_part = _matmul("ada_mod", c_all, w_ada[0], "nn", [F32], a_pro=silu, vecs=[b_loc], exact=True,
                       epilogue=lambda acc, e, v: (acc + v[0],), tn=512, tk=d)
    mod_all = _run_plan("gather_mod", _Gather([mod_part]))[0]
    mod = lax.dynamic_index_in_dim(mod_all, me, axis=1, keepdims=False).reshape(N_MOD, 1, d)
    shift1, scale1, gate1, shift2, scale2, gate2 = [mod[i] for i in range(N_MOD)]

    h1 = _norm_mod_fwd("norm1", xs, norm1_g, scale1, shift1)
    q = _matmul("proj_q", h1, w_in_t, "nt", [F32], b_rows=(0, d_attn))
    kv = _matmul("proj_kv", h1, w_in_t, "nt", [F32], b_rows=(d_attn, 2 * d_kv))
    u = _matmul("proj_u", h1, w_in_t, "nt", [F32], b_rows=(d_attn + 2 * d_kv, d_ssm))
    k, v = kv[:, :d_kv], kv[:, d_kv:]

    half = HEAD_DIM // 2
    inv_freq = ROPE_THETA ** (-jnp.arange(half, dtype=F32) / half)
    ang = jnp.arange(s, dtype=F32)[:, None] * inv_freq[None, :]
    cos_t, sin_t = jnp.tile(jnp.cos(ang), (1, 4)), jnp.tile(jnp.sin(ang), (1, 4))
    (attn, lse), (g_glu, g_out) = _attn_fwd("attn_fwd", q, k, v, cos_t, sin_t, sinks,
                                            plan=_Gather([w_glu[0].astype(BF16), w_out[0].astype(BF16)]))
    w_glu_f = g_glu.reshape(d_ssm, d_ssm)
    w_out_f = g_out.reshape(d, d)

    ssm_params = (ssm_lam_re[0], ssm_lam_im[0], ssm_log_step[0], ssm_b_re[0], ssm_b_im[0], ssm_c_re[0],
                  ssm_c_im[0], ssm_d[0])
    (tm_op, em_op, fm_op, alr, ali), ssm_vjp = jax.vjp(_ssm_operators, *ssm_params)
    acat, bcat = _decay_lanes(alr, ali)
    (y_ssm, x_prev, u_chunks), (g_ff1,) = _ssm_fwd("ssm_fwd", u, tm_op, em_op, fm_op, acat, bcat,
                                                   plan=_Gather([w_ff1[0].astype(BF16)]))
    yg = _gelu_fwd("gelu", y_ssm)
    ssm_out, z_glu = _matmul(
        "glu", yg, w_glu_f, "nn", [F32, F32], extras=[y_ssm], vecs=[b_glu],
        epilogue=lambda acc, e, v: (_gelu(e[0]) * _sigmoid(acc + v[0]), acc + v[0]))
    mixed = _group_norm_fwd("group_norm", attn, ssm_out, attn_out_g, ssm_out_g)
    x2, mo = _matmul("out_proj", mixed, w_out_f, "nn", [F32, BF16], extras=[xs], vecs=[gate1],
                     epilogue=lambda acc, e, v: (e[0] + v[0] * acc, acc))

    h2 = _norm_mod_fwd("norm2", x2, norm2_g, scale2, shift2)
    (a_ff, f_ff), (g_ff2,) = _matmul("ff1", h2, g_ff1, "nn", [BF16, BF16], b_blocked=True,
                                     epilogue=lambda acc, e, v: (acc, jnp.square(jnp.maximum(acc, 0.0))),
                                     plan=_Gather([w_ff2[0].astype(BF16)]))
    w_ff2_f = g_ff2.reshape(-1, d)
    x3, ff = _matmul("ff2", f_ff, w_ff2_f, "nn", [F32, BF16], extras=[x2], vecs=[gate2],
                     epilogue=lambda acc, e, v: (e[0] + v[0] * acc, acc))

    dx3, dff, loss_local, d_final_g, d_gate2 = _final_loss("final_loss", x3, tgt, final_g.reshape(1, d), ff, gate2)
    loss = lax.psum(loss_local, MESH_AXES)

    dw_ff2 = _matmul("ff2_dw", f_ff, dff, "tn", [BF16]).reshape(N_DEV, -1, d)
    da_ff, (p_ff2,) = _matmul("ff2_dx", dff, w_ff2_f, "nt", [BF16], extras=[a_ff],
                              epilogue=lambda acc, e, v: (acc * (2.0 * jnp.maximum(e[0].astype(F32), 0.0)),),
                              plan=_PairSwap([dw_ff2]))
    s_ff2 = _pair_add("pair_add_ff2", dw_ff2, p_ff2)
    dw_ff1, (r_ff2_a,) = _matmul("ff1_dw", h2, da_ff, "tn", [BF16], out_blocked=N_DEV,
                                 plan=_ChipExchange([s_ff2], (CHIP_X, CHIP_Y)))
    dh2, (r_ff2_b,) = _matmul("ff1_dx", da_ff, g_ff1, "nt", [F32], b_blocked=True,
                              plan=_ChipExchange([s_ff2], (CHIP_DIAGONAL,)))
    (dx2, dmo, d_scale2, d_shift2, d_norm2_g, d_gate1), (p_ff1,) = _norm_mod_bwd(
        "norm2_bwd", x2, dh2, dx3, norm2_g, scale2, gated=(mo, gate1), plan=_PairSwap([dw_ff1]))
    s_ff1 = _pair_add("pair_add_ff1", dw_ff1, p_ff1)

    dw_out = _matmul("out_dw", mixed, dmo, "tn", [BF16]).reshape(N_DEV, -1, d)
    dmixed, (r_out_a,) = _matmul("out_dx", dmo, w_out_f, "nt", [F32], plan=_Exchange([dw_out], (1, 4, 2)))
    (dattn, dssm_out, d_attn_g, d_ssm_g), (r_out_b,) = _group_norm_bwd(
        "group_norm_bwd", attn, ssm_out, dmixed, attn_out_g, ssm_out_g, plan=_Exchange([dw_out], (6,)))

    dz, dyg_direct, d_b_glu = _glu_bwd("glu_bwd", dssm_out, y_ssm, z_glu)
    dw_glu = _matmul("glu_dw", yg, dz, "tn", [BF16]).reshape(N_DEV, -1, d_ssm)
    dy_ssm = _matmul("glu_dx", dz, w_glu_f, "nt", [F32], extras=[dyg_direct, y_ssm],
                     epilogue=lambda acc, e, v: ((acc + e[0]) * _gelu_grad(e[1]),))
    (du, d_tm, d_em, d_fm, r1, r2), (r_ff1_a, r_glu) = _ssm_bwd(
        "ssm_bwd", u_chunks, dy_ssm, x_prev, tm_op, em_op, fm_op, acat, bcat,
        plan=_Plans([_ChipExchange([s_ff1], (CHIP_X, CHIP_Y)), _Exchange([dw_glu], RELATIONS_ALL)]))
    d_alr = r1[:, :p_state] + r1[:, p_state:]
    d_ali = r2[:, p_state:] - r2[:, :p_state]
    d_ssm_params = ssm_vjp((d_tm, d_em, d_fm, d_alr, d_ali))

    (dq, dk, dv, d_sinks), (r_ff1_b,) = _attn_bwd("attn_bwd", q, k, v, cos_t, sin_t, sinks, attn, lse, dattn,
                                                  plan=_ChipExchange([s_ff1], (CHIP_DIAGONAL,)))
    dproj = jnp.concatenate([dq, dk, dv, du], axis=1).astype(BF16)
    dw_in_t, (r_out_c,) = _matmul("in_dw", dproj, h1, "tn", [BF16], plan=_Exchange([dw_out], (5, 3, 7)))
    dw_in_t = dw_in_t.reshape(N_DEV, -1, d)
    dh1, (r_in_a,) = _matmul("in_dx", dproj, w_in_t, "nn", [F32], plan=_Exchange([dw_in_t], RELATIONS_SAME_CORE))
    (grad_x, d_scale1, d_shift1, d_norm1_g), _ = _norm_mod_bwd("norm1_bwd", xs, dh1, dx2, norm1_g, scale1)

    d_mod = jnp.concatenate([d_shift1, d_scale1, d_gate1, d_shift2, d_scale2, d_gate2])
    small_g = dict(zip(("ssm_lam_re", "ssm_lam_im", "ssm_log_step", "ssm_b_re", "ssm_b_im", "ssm_c_re", "ssm_c_im",
                        "ssm_d"), d_ssm_params, strict=True))
    small_g.update(b_ada=d_mod, norm1_g=d_norm1_g, sinks=d_sinks, b_glu=d_b_glu, attn_out_g=d_attn_g,
                   ssm_out_g=d_ssm_g, norm2_g=d_norm2_g, final_g=d_final_g)
    small_parts, r_in_b = _run_plan("gather_small_grads", _Plans([_Gather([_pack([small_g[n] for n in _SMALL])]),
                                                                  _Exchange([dw_in_t], RELATIONS_OTHER_CORE)]))
    small = _adam_shard("adam_small", [small_parts], _pack([weights[n] for n in _SMALL]),
                        _pack([mom[n] for n in _SMALL]), _pack([var[n] for n in _SMALL]))
    out = {}
    off = 0
    for n in _SMALL:
        size = weights[n].size
        out[n] = [t.reshape(-1)[off:off + size].reshape(weights[n].shape) for t in small]
        off += size

    dmod_all = small_parts.reshape(N_DEV, -1)[:, :N_MOD * d]
    dmod_loc = lax.dynamic_slice_in_dim(dmod_all, me * n_loc, n_loc, axis=1)
    c_act_t = silu(c_all).T
    out["w_ada"] = [t[None] for t in _ada_update("adam_w_ada", c_act_t, dmod_loc, w_ada[0], m_w_ada[0], v_w_ada[0])]

    mine = lambda blocks: lax.dynamic_index_in_dim(blocks, me, axis=0, keepdims=False)
    in_parts = [mine(dw_in_t).T] + [r.transpose(0, 2, 1) for r in (r_in_a, r_in_b)]
    my_chip = 2 * lax.axis_index("x") + lax.axis_index("y")
    chip_sum = lambda sums: lax.dynamic_index_in_dim(sums, my_chip, axis=0, keepdims=False)
    received = dict(w_in=in_parts, w_glu=[mine(dw_glu), r_glu], w_out=[mine(dw_out), r_out_a, r_out_b, r_out_c],
                    w_ff1=[chip_sum(s_ff1), r_ff1_a, r_ff1_b], w_ff2=[chip_sum(s_ff2), r_ff2_a, r_ff2_b])
    for n, parts in received.items():
        out[n] = [t[None] for t in _adam_shard("adam_" + n, parts, weights[n][0], mom[n][0], var[n][0])]

    return (loss, grad_x[None], *[out[n][0] for n in _WEIGHTS], *[out[n][1] for n in _WEIGHTS],
            *[out[n][2] for n in _WEIGHTS], *[out[n][3] for n in _WEIGHTS])
```

```python
import math

import jax
import jax.numpy as jnp
from jax import lax
from jax.experimental import pallas as pl
from jax.experimental.pallas import tpu as pltpu

F32, BF16 = jnp.float32, jnp.bfloat16
SDS = jax.ShapeDtypeStruct
MESH_AXES = ("x", "y", "c")
N_DEV = 8
VMEM_LIMIT_BYTES = 56 * 1024 * 1024
MATMUL_VMEM_BUDGET = 44 * 1024 * 1024
SUBLANES, LANES = 8, 128

HEAD_DIM = 64
Q_PER_KV = 8
WINDOW = 128
ROPE_THETA = 10000.0
EPS = 1e-6
N_MOD = 6
SSM_CHUNK = 16
SSM_GROUPS_PER_STEP = 8

ADAM_LR, ADAM_B1, ADAM_B2, ADAM_EPS, ADAM_WD, ADAM_STEP = 0.001, 0.9, 0.999, 1e-08, 0.01, 10
HIGHEST = lax.Precision.HIGHEST
SSM_PRECISION = lax.Precision.HIGH

RELATIONS_ALL = (1, 4, 2, 6, 5, 3, 7)
RELATIONS_SAME_CORE = (1, 4, 2, 6)
RELATIONS_OTHER_CORE = (5, 3, 7)
PLAN_MIDDLE = 0.65


def _cparams(sem):
    return pltpu.CompilerParams(dimension_semantics=sem, vmem_limit_bytes=VMEM_LIMIT_BYTES)


def _block_index(p):
    return 4 * p[0] + 2 * p[1] + p[2]


def _me():
    return lax.axis_index("x"), lax.axis_index("y"), lax.axis_index("c")


class _Plan:
    def middle(self, ins, outs, send, recv, local):
        pass


class _Gather(_Plan):
    TO_SIBLING, TO_X, TO_Y, RELAY, PASS_X, PASS_Y, PASS_DIAGONAL = range(7)

    def __init__(self, arrs):
        self.ins = list(arrs)
        self.out_shapes = [SDS((N_DEV,) + a.shape, a.dtype) for a in arrs]
        self.n_rdma, self.n_local = 7 * len(arrs), len(arrs)
        self.rdma_base = self.local_base = 0

    def _copy(self, ins, outs, send, recv, a, k, block, to, from_input=False):
        dst = outs[a].at[_block_index(block)]
        sem = self.rdma_base + a * 7 + k
        return pltpu.make_async_remote_copy(
            src_ref=ins[a] if from_input else dst, dst_ref=dst, send_sem=send.at[sem], recv_sem=recv.at[sem],
            device_id=to, device_id_type=pl.DeviceIdType.MESH)

    @staticmethod
    def _places():
        x, y, c = _me()
        return (x, y, c), (x, y, 1 - c), (1 - x, y, c), (x, 1 - y, c), (1 - x, 1 - y, c)

    def _first(self, ins, outs, send, recv, a):
        me, sibling, x_nbr, y_nbr, _ = self._places()
        return [self._copy(ins, outs, send, recv, a, k, me, to, True)
                for k, to in ((self.TO_SIBLING, sibling), (self.TO_X, x_nbr), (self.TO_Y, y_nbr))]

    def _mine(self, ins, outs, local, a):
        return pltpu.make_async_copy(ins[a], outs[a].at[_block_index(_me())], local.at[self.local_base + a])

    def start(self, ins, outs, send, recv, local):
        for a in range(len(ins)):
            self._mine(ins, outs, local, a).start()
            for cp in self._first(ins, outs, send, recv, a):
                cp.start()

    def middle(self, ins, outs, send, recv, local):
        me, sibling, x_nbr, y_nbr, _ = self._places()
        core = me[2]
        for a in range(len(ins)):
            self._copy(ins, outs, send, recv, a, self.TO_X, x_nbr, me).wait_recv()
            self._copy(ins, outs, send, recv, a, self.TO_Y, y_nbr, me).wait_recv()

            @pl.when(core == 0)
            def _():
                self._copy(ins, outs, send, recv, a, self.RELAY, x_nbr, y_nbr).start()

            @pl.when(core == 1)
            def _():
                self._copy(ins, outs, send, recv, a, self.RELAY, y_nbr, x_nbr).start()

            self._copy(ins, outs, send, recv, a, self.PASS_X, x_nbr, sibling).start()
            self._copy(ins, outs, send, recv, a, self.PASS_Y, y_nbr, sibling).start()

    def finish(self, ins, outs, send, recv, local):
        me, sibling, x_nbr, y_nbr, diagonal = self._places()
        other = lambda p: (p[0], p[1], 1 - p[2])
        for a in range(len(ins)):
            self._copy(ins, outs, send, recv, a, self.RELAY, diagonal, me).wait_recv()
            self._copy(ins, outs, send, recv, a, self.PASS_DIAGONAL, diagonal, sibling).start()
        for a in range(len(ins)):
            self._copy(ins, outs, send, recv, a, self.TO_SIBLING, sibling, me).wait_recv()
            for k, src in ((self.PASS_X, x_nbr), (self.PASS_Y, y_nbr), (self.PASS_DIAGONAL, diagonal)):
                self._copy(ins, outs, send, recv, a, k, other(src), me).wait_recv()
                self._copy(ins, outs, send, recv, a, k, src, sibling).wait_send()
            for cp in self._first(ins, outs, send, recv, a):
                cp.wait_send()
            self._copy(ins, outs, send, recv, a, self.RELAY, me, me).wait_send()
            self._mine(ins, outs, local, a).wait()


class _Exchange(_Plan):
    def __init__(self, arrs, relations):
        self.ins, self.relations = list(arrs), tuple(relations)
        self.out_shapes = [SDS((len(relations),) + a.shape[1:], a.dtype) for a in arrs]
        self.n_rdma, self.n_local = len(relations) * len(arrs), 0
        self.rdma_base = self.local_base = 0

    def _copies(self, ins, outs, send, recv):
        x, y, c = _me()
        cps = []
        for a in range(len(ins)):
            for s, k in enumerate(self.relations):
                peer = ((1 - x) if (k & 4) else x, (1 - y) if (k & 2) else y, (1 - c) if (k & 1) else c)
                sem = self.rdma_base + a * len(self.relations) + s
                cps.append(pltpu.make_async_remote_copy(
                    src_ref=ins[a].at[_block_index(peer)], dst_ref=outs[a].at[s], send_sem=send.at[sem],
                    recv_sem=recv.at[sem], device_id=peer, device_id_type=pl.DeviceIdType.MESH))
        return cps

    def start(self, ins, outs, send, recv, local):
        for cp in self._copies(ins, outs, send, recv):
            cp.start()

    def finish(self, ins, outs, send, recv, local):
        for cp in self._copies(ins, outs, send, recv):
            cp.wait()


class _PairSwap(_Plan):
    def __init__(self, arrs):
        self.ins = list(arrs)
        self.out_shapes = [SDS((4,) + a.shape[1:], a.dtype) for a in arrs]
        self.n_rdma, self.n_local = 4 * len(arrs), 0
        self.rdma_base = self.local_base = 0

    def _copies(self, ins, outs, send, recv):
        x, y, c = _me()
        cps = []
        for a in range(len(ins)):
            for s in range(4):
                sem = self.rdma_base + a * 4 + s
                cps.append(pltpu.make_async_remote_copy(
                    src_ref=ins[a].at[2 * s + (1 - c)], dst_ref=outs[a].at[s], send_sem=send.at[sem],
                    recv_sem=recv.at[sem], device_id=(x, y, 1 - c), device_id_type=pl.DeviceIdType.MESH))
        return cps

    def start(self, ins, outs, send, recv, local):
        for cp in self._copies(ins, outs, send, recv):
            cp.start()

    def finish(self, ins, outs, send, recv, local):
        for cp in self._copies(ins, outs, send, recv):
            cp.wait()


CHIP_X, CHIP_Y, CHIP_DIAGONAL = (1, 0), (0, 1), (1, 1)


class _ChipExchange(_Plan):
    def __init__(self, arrs, hops):
        self.ins, self.hops = list(arrs), tuple(hops)
        self.out_shapes = [SDS((len(hops),) + a.shape[1:], a.dtype) for a in arrs]
        self.n_rdma, self.n_local = len(hops) * len(arrs), 0
        self.rdma_base = self.local_base = 0

    def _copies(self, ins, outs, send, recv):
        x, y, c = _me()
        cps = []
        for a in range(len(ins)):
            for s, (fx, fy) in enumerate(self.hops):
                px, py = (1 - x) if fx else x, (1 - y) if fy else y
                sem = self.rdma_base + a * len(self.hops) + s
                cps.append(pltpu.make_async_remote_copy(
                    src_ref=ins[a].at[2 * px + py], dst_ref=outs[a].at[s], send_sem=send.at[sem],
                    recv_sem=recv.at[sem], device_id=(px, py, c), device_id_type=pl.DeviceIdType.MESH))
        return cps

    def start(self, ins, outs, send, recv, local):
        for cp in self._copies(ins, outs, send, recv):
            cp.start()

    def finish(self, ins, outs, send, recv, local):
        for cp in self._copies(ins, outs, send, recv):
            cp.wait()


class _Plans:
    def __init__(self, plans):
        self.plans = list(plans)
        self.ins = [a for p in plans for a in p.ins]
        self.out_shapes = [s for p in plans for s in p.out_shapes]
        self.n_rdma = self.n_local = 0
        for p in plans:
            p.rdma_base, p.local_base = self.n_rdma, self.n_local
            self.n_rdma, self.n_local = self.n_rdma + p.n_rdma, self.n_local + p.n_local

    def _each(self, ins, outs):
        i = o = 0
        for p in self.plans:
            yield p, ins[i:i + len(p.ins)], outs[o:o + len(p.out_shapes)]
            i, o = i + len(p.ins), o + len(p.out_shapes)

    def start(self, ins, outs, send, recv, local):
        for p, pi, po in self._each(ins, outs):
            p.start(pi, po, send, recv, local)

    def middle(self, ins, outs, send, recv, local):
        for p, pi, po in self._each(ins, outs):
            p.middle(pi, po, send, recv, local)

    def finish(self, ins, outs, send, recv, local):
        for p, pi, po in self._each(ins, outs):
            p.finish(pi, po, send, recv, local)


def _plan_scratch(plan):
    return [pltpu.SemaphoreType.DMA((plan.n_rdma,)), pltpu.SemaphoreType.DMA((plan.n_rdma,)),
            pltpu.SemaphoreType.DMA((max(plan.n_local, 1),))]


def _run_plan(name, plan):
    n = len(plan.ins)

    def body(*refs):
        ins, outs, sems = refs[:n], refs[n:len(refs) - 3], refs[len(refs) - 3:]
        plan.start(ins, outs, *sems)
        plan.middle(ins, outs, *sems)
        plan.finish(ins, outs, *sems)

    any_spec = pl.BlockSpec(memory_space=pl.ANY)
    return pl.pallas_call(body, name=name, out_shape=list(plan.out_shapes), in_specs=[any_spec] * n,
                          out_specs=[any_spec] * len(plan.out_shapes), scratch_shapes=_plan_scratch(plan))(*plan.ins)


def _pcall(name, body, grid, in_specs, ins, out_specs, out_shape, scratch=(), semantics=None, plan=None):
    if plan is None:
        return pl.pallas_call(body, name=name, grid=grid, in_specs=list(in_specs), out_specs=list(out_specs),
                              out_shape=list(out_shape), scratch_shapes=list(scratch),
                              compiler_params=_cparams(semantics))(*ins)
    n_in, n_out, n_scr = len(ins), len(out_shape), len(scratch)
    p_in, p_out = len(plan.ins), len(plan.out_shapes)

    def with_plan(*refs):
        k_in, c_in = refs[:n_in], refs[n_in:n_in + p_in]
        refs = refs[n_in + p_in:]
        k_out, c_out = refs[:n_out], refs[n_out:n_out + p_out]
        refs = refs[n_out + p_out:]
        k_scr, sems = refs[:n_scr], refs[n_scr:]
        step = 0
        for d, g in enumerate(grid):
            step = step * g + pl.program_id(d)
        n_steps = math.prod(grid)

        @pl.when(step == 0)
        def _():
            plan.start(c_in, c_out, *sems)

        @pl.when(step == min(n_steps - 1, int(n_steps * PLAN_MIDDLE)))
        def _():
            plan.middle(c_in, c_out, *sems)

        body(*k_in, *k_out, *k_scr)

        @pl.when(step == n_steps - 1)
        def _():
            plan.finish(c_in, c_out, *sems)

    any_spec = pl.BlockSpec(memory_space=pl.ANY)
    res = pl.pallas_call(
        with_plan, name=name, grid=grid, in_specs=list(in_specs) + [any_spec] * p_in,
        out_specs=list(out_specs) + [any_spec] * p_out, out_shape=list(out_shape) + list(plan.out_shapes),
        scratch_shapes=list(scratch) + _plan_scratch(plan),
        compiler_params=_cparams(("arbitrary",) * len(grid)))(*ins, *plan.ins)
    return res[:n_out], res[n_out:]


def _rowwise(name, fn, rows, vecs, row_outs, acc_outs=(), tm=128, plan=None):
    t = rows[0].shape[0]
    tm = min(tm, t)
    assert t % tm == 0 and tm % SUBLANES == 0
    n_r, n_v, n_o = len(rows), len(vecs), len(row_outs)

    def body(*refs):
        r_in, v_in = refs[:n_r], refs[n_r:n_r + n_v]
        r_out, a_out = refs[n_r + n_v:n_r + n_v + n_o], refs[n_r + n_v + n_o:]
        outs, accs = fn([r[...] for r in r_in], [v[...] for v in v_in])
        for o_ref, o in zip(r_out, outs, strict=True):
            o_ref[...] = o.astype(o_ref.dtype)
        if a_out:
            @pl.when(pl.program_id(0) == 0)
            def _():
                for a_ref in a_out:
                    a_ref[...] = jnp.zeros_like(a_ref)
            for a_ref, a in zip(a_out, accs, strict=True):
                a_ref[...] += a.reshape(tm // SUBLANES, SUBLANES, a.shape[-1]).sum(axis=0)

    in_specs = [pl.BlockSpec((tm, r.shape[1]), lambda i: (i, 0)) for r in rows]
    in_specs += [pl.BlockSpec(v.shape, lambda i: (0, 0)) for v in vecs]
    out_specs = [pl.BlockSpec((tm, w), lambda i: (i, 0)) for w, _ in row_outs]
    out_specs += [pl.BlockSpec((SUBLANES, w), lambda i: (0, 0)) for w in acc_outs]
    out_shape = [SDS((t, w), dt) for w, dt in row_outs] + [SDS((SUBLANES, w), F32) for w in acc_outs]
    return _pcall(name, body, (t // tm,), in_specs, [*rows, *vecs], out_specs, out_shape, semantics=("arbitrary",),
                  plan=plan)


def _tile(n, want):
    if n <= want:
        return n
    for t in range(want // LANES * LANES, 0, -LANES):
        if n % t == 0:
            return t
    raise ValueError(f"no tile for {n}")


_DOT_DIMS = {"nn": (((1,), (0,)), ((), ())), "nt": (((1,), (1,)), ((), ())), "tn": (((0,), (0,)), ((), ()))}


def _matmul(name, a, b, mode, out_dtypes, epilogue=None, extras=(), vecs=(), a_pro=None,
            tm=1024, tn=512, tk=4096, exact=False, b_blocked=False, out_blocked=0, b_rows=None, plan=None):
    cs = b.shape[-1] if b_blocked else None
    b2 = (b.shape[1], b.shape[0] * b.shape[2]) if b_blocked else b.shape
    if mode == "tn":
        (k, m), (k2, n) = a.shape, b2
    elif mode == "nt":
        (m, k), (n, k2) = a.shape, b2
    else:
        (m, k), (k2, n) = a.shape, b2
    assert k == k2 and not (b_blocked and mode == "tn")
    row0 = 0
    if b_rows is not None:
        assert mode == "nt" and not b_blocked
        row0, n = b_rows
        tn = _tile(math.gcd(n, row0) if row0 else n, tn)
    tm, tn, tk = _tile(m, tm), _tile(n, tn), _tile(k, tk)
    if b_blocked and mode == "nn":
        tn = _tile(cs, tn)
    if b_blocked and mode == "nt":
        tk = _tile(cs, tk)
    if out_blocked:
        tn = _tile(n // out_blocked, tn)
    nk = k // tk

    def vmem_bytes(width):
        operands = 2 * (tm * tk * a.dtype.itemsize + tk * width * b.dtype.itemsize)
        tiles = 2 * tm * width * (sum(jnp.dtype(dt).itemsize for dt in out_dtypes) + sum(e.dtype.itemsize for e in extras))
        return operands + tiles + tm * width * 4 * (2 if nk > 1 else 1)

    extent = cs if (b_blocked and mode == "nn") else n // out_blocked if out_blocked else n
    if extent % (2 * tn) == 0 and row0 % (2 * tn) == 0 and vmem_bytes(2 * tn) <= MATMUL_VMEM_BUDGET:
        tn *= 2
    n_e, n_v, n_o = len(extras), len(vecs), len(out_dtypes)
    precision = HIGHEST if exact else None

    def body(*refs):
        a_ref, b_ref = refs[:2]
        e_refs, v_refs = refs[2:2 + n_e], refs[2 + n_e:2 + n_e + n_v]
        o_refs = refs[2 + n_e + n_v:2 + n_e + n_v + n_o]

        def product():
            av = a_ref[...]
            if a_pro is not None:
                av = a_pro(av)
            return lax.dot_general(av, b_ref[...], _DOT_DIMS[mode], precision=precision, preferred_element_type=F32)

        def finish(acc):
            res = (acc,) if epilogue is None else epilogue(acc, [e[...] for e in e_refs], [v[...] for v in v_refs])
            for o_ref, r in zip(o_refs, res, strict=True):
                o_ref[...] = r.astype(o_ref.dtype)

        if nk == 1:
            finish(product())
            return
        acc_ref = refs[-1]
        kk = pl.program_id(2)

        @pl.when(kk == 0)
        def _():
            acc_ref[...] = product()

        @pl.when(kk > 0)
        def _():
            acc_ref[...] += product()

        @pl.when(kk == nk - 1)
        def _():
            finish(acc_ref[...])

    if mode == "tn":
        a_spec = pl.BlockSpec((tk, tm), lambda i, j, kk: (kk, i))
    else:
        a_spec = pl.BlockSpec((tm, tk), lambda i, j, kk: (i, kk))
    if b_blocked and mode == "nn":
        per = cs // tn
        b_spec = pl.BlockSpec((None, tk, tn), lambda i, j, kk: (j // per, kk, j % per))
    elif b_blocked:
        per = cs // tk
        b_spec = pl.BlockSpec((None, tn, tk), lambda i, j, kk: (kk // per, j, kk % per))
    elif mode == "nt":
        assert row0 % tn == 0
        b_spec = pl.BlockSpec((tn, tk), lambda i, j, kk: (j + row0 // tn, kk))
    else:
        b_spec = pl.BlockSpec((tk, tn), lambda i, j, kk: (kk, j))
    tile = pl.BlockSpec((tm, tn), lambda i, j, kk: (i, j))
    if out_blocked:
        per_o = n // out_blocked // tn
        out_spec = pl.BlockSpec((None, tm, tn), lambda i, j, kk: (j // per_o, i, j % per_o))
        out_shape = [SDS((out_blocked, m, n // out_blocked), dt) for dt in out_dtypes]
    else:
        out_spec, out_shape = tile, [SDS((m, n), dt) for dt in out_dtypes]
    in_specs = [a_spec, b_spec] + [tile] * n_e + [pl.BlockSpec((1, tn), lambda i, j, kk: (0, j))] * n_v
    res = _pcall(name, body, (m // tm, n // tn, nk), in_specs, [a, b, *extras, *vecs], [out_spec] * n_o, out_shape,
                 scratch=[pltpu.VMEM((tm, tn), F32)] if nk > 1 else [],
                 semantics=("parallel", "parallel", "arbitrary"), plan=plan)
    if plan is None:
        return res[0] if n_o == 1 else res
    return (res[0][0] if n_o == 1 else res[0]), res[1]


def _rms_fwd(x):
    r = lax.rsqrt(jnp.mean(x * x, axis=-1, keepdims=True) + EPS)
    return x * r, r


def _rms_bwd(dxn, xn, r):
    return r * (dxn - xn * jnp.mean(dxn * xn, axis=-1, keepdims=True))


_INV_SQRT2 = 1.0 / math.sqrt(2.0)
_INV_SQRT2PI = 1.0 / math.sqrt(2.0 * math.pi)


def _gelu(y):
    return 0.5 * y * (1.0 + lax.erf(y * _INV_SQRT2))


def _gelu_grad(y):
    return 0.5 * (1.0 + lax.erf(y * _INV_SQRT2)) + y * (_INV_SQRT2PI * jnp.exp(-0.5 * y * y))


def _sigmoid(z):
    return 1.0 / (1.0 + jnp.exp(-z))


def _adam_math(w, g, m, v):
    m = ADAM_B1 * m + (1.0 - ADAM_B1) * g
    v = ADAM_B2 * v + (1.0 - ADAM_B2) * (g * g)
    m_hat = m / (1.0 - ADAM_B1 ** ADAM_STEP)
    v_hat = v / (1.0 - ADAM_B2 ** ADAM_STEP)
    delta = -ADAM_LR * (m_hat / (jnp.sqrt(v_hat) + ADAM_EPS) + ADAM_WD * w)
    return delta, m, v


def _norm_mod_fwd(name, x, g, scale, shift):
    def fn(rows, vecs):
        (xv,), (gv, sc, sh) = rows, vecs
        xn, _ = _rms_fwd(xv)
        return [(xn * gv) * (1.0 + sc) + sh], []
    return _rowwise(name, fn, [x], [g, scale, shift], [(x.shape[1], BF16)])[0]


def _norm_mod_bwd(name, x, dh, dres, g, scale, gated=None, plan=None):
    d = x.shape[1]

    def fn(rows, vecs):
        xv, dhv, drv = rows[:3]
        gv, sc = vecs[:2]
        xn, r = _rms_fwd(xv)
        t = xn * gv
        dt = dhv * (1.0 + sc)
        dx = drv + _rms_bwd(dt * gv, xn, r)
        if gated is None:
            return [dx], [dhv * t, dhv, dt * xn]
        return [dx, dx * vecs[2]], [dhv * t, dhv, dt * xn, dx * rows[3].astype(F32)]
    extra_rows, extra_vecs = ([gated[0]], [gated[1]]) if gated is not None else ([], [])
    res = _rowwise(name, fn, [x, dh, dres] + extra_rows, [g, scale] + extra_vecs,
                   [(d, F32)] + [(d, BF16)] * len(extra_rows), [d] * (3 + len(extra_rows)), plan=plan)
    outs, rest = res if plan is not None else (res, None)
    n_rows = 1 + len(extra_rows)
    return (*outs[:n_rows], *[a.sum(0) for a in outs[n_rows:]]), rest


def _final_loss(name, x, tgt, g, val, gate):
    d = x.shape[1]

    def fn(rows, vecs):
        (xv, tv, vv), (gv, gate_v) = rows, vecs
        xn, r = _rms_fwd(xv)
        e = xn * gv - tv
        dy = e * (1.0 / d)
        dx = _rms_bwd(dy * gv, xn, r)
        return [dx, dx * gate_v], [e * e, dy * xn, dx * vv.astype(F32)]
    dx, dval, sq, dg, dgate = _rowwise(name, fn, [x, tgt, val], [g, gate], [(d, F32), (d, BF16)], [d, d, d])
    return dx, dval, 0.5 * jnp.sum(sq) / d, dg.sum(0), dgate.sum(0)


def _group_norm_fwd(name, attn, ssm, g_a, g_s):
    def fn(rows, vecs):
        (av, sv), (ga, gs) = rows, vecs
        return [jnp.concatenate([_rms_fwd(av)[0] * ga, _rms_fwd(sv)[0] * gs], axis=1)], []
    return _rowwise(name, fn, [attn, ssm], [g_a, g_s], [(attn.shape[1] + ssm.shape[1], BF16)])[0]


def _group_norm_bwd(name, attn, ssm, dmixed, g_a, g_s, plan=None):
    da_w, ds_w = attn.shape[1], ssm.shape[1]

    def fn(rows, vecs):
        (av, sv, dm), (ga, gs) = rows, vecs
        an, ra = _rms_fwd(av)
        sn, rs = _rms_fwd(sv)
        dma, dms = dm[:, :da_w], dm[:, da_w:]
        return [_rms_bwd(dma * ga, an, ra), _rms_bwd(dms * gs, sn, rs)], [dma * an, dms * sn]
    res = _rowwise(name, fn, [attn, ssm, dmixed], [g_a, g_s], [(da_w, F32), (ds_w, F32)], [da_w, ds_w], plan=plan)
    (dattn, dssm, dga, dgs), rest = res if plan is not None else (res, None)
    return (dattn, dssm, dga.sum(0), dgs.sum(0)), rest


def _gelu_fwd(name, y):
    def fn(rows, vecs):
        return [_gelu(rows[0])], []
    return _rowwise(name, fn, [y], [], [(y.shape[1], BF16)])[0]


def _glu_bwd(name, dout, y, z):
    d = y.shape[1]

    def fn(rows, vecs):
        dov, yv, zv = rows
        sg = _sigmoid(zv)
        dz = dov * _gelu(yv) * sg * (1.0 - sg)
        return [dz, dov * sg], [dz]
    dz, dyg, db = _rowwise(name, fn, [dout, y, z], [], [(d, BF16), (d, F32)], [d])
    return dz, dyg, db.sum(0)


def _adam_shard(name, parts, w, m, v):
    r, c = w.shape
    n_parts = sum(1 if p.ndim == 2 else p.shape[0] for p in parts)
    row_bytes = 2 * c * (n_parts * parts[0].dtype.itemsize + 7 * 4)
    tr = min(128, r)
    while tr > SUBLANES and tr * row_bytes > VMEM_LIMIT_BYTES // 2:
        tr //= 2
    assert r % tr == 0
    n_p = len(parts)

    def body(*refs):
        p_refs, (w_ref, m_ref, v_ref, g_out, d_out, m_out, v_out) = refs[:n_p], refs[n_p:]
        g = None
        for p_ref in p_refs:
            terms = [p_ref[...]] if len(p_ref.shape) == 2 else [p_ref[j] for j in range(p_ref.shape[0])]
            for t in terms:
                g = t.astype(F32) if g is None else g + t.astype(F32)
        delta, m_new, v_new = _adam_math(w_ref[...], g, m_ref[...], v_ref[...])
        g_out[...], d_out[...], m_out[...], v_out[...] = g, delta, m_new, v_new

    tile = pl.BlockSpec((tr, c), lambda i: (i, 0))
    p_specs = [tile if p.ndim == 2 else pl.BlockSpec((p.shape[0], tr, c), lambda i: (0, i, 0)) for p in parts]
    return _pcall(name, body, (r // tr,), p_specs + [tile] * 3, [*parts, w, m, v], [tile] * 4, [SDS((r, c), F32)] * 4,
                  semantics=("parallel",))


def _pair_add(name, blocks, from_sibling):
    _, r, c = blocks.shape
    tr = min(256, r)
    assert r % tr == 0

    def body(core_ref, b_ref, s_ref, o_ref):
        o_ref[...] = (b_ref[...].astype(F32) + s_ref[...].astype(F32)).astype(o_ref.dtype)

    core = lax.axis_index("c").astype(jnp.int32).reshape(1)
    mine = pl.BlockSpec((None, tr, c), lambda s, i, core_ref: (2 * s + core_ref[0], i, 0))
    slot = pl.BlockSpec((None, tr, c), lambda s, i, core_ref: (s, i, 0))
    grid_spec = pltpu.PrefetchScalarGridSpec(num_scalar_prefetch=1, grid=(4, r // tr), in_specs=[mine, slot],
                                             out_specs=slot)
    return pl.pallas_call(body, name=name, grid_spec=grid_spec, out_shape=SDS((4, r, c), blocks.dtype),
                          compiler_params=_cparams(("parallel", "parallel")))(core, blocks, from_sibling)


def _ada_update(name, c_act_t, dmod, w, m, v, tr=128, plan=None):
    r, c = w.shape
    tr = min(tr, r)
    assert r % tr == 0

    def body(c_ref, d_ref, w_ref, m_ref, v_ref, g_out, d_out, m_out, v_out):
        g = jnp.dot(c_ref[...], d_ref[...], precision=lax.Precision.HIGHEST, preferred_element_type=F32)
        delta, m_new, v_new = _adam_math(w_ref[...], g, m_ref[...], v_ref[...])
        g_out[...], d_out[...], m_out[...], v_out[...] = g, delta, m_new, v_new

    tile = pl.BlockSpec((tr, c), lambda i: (i, 0))
    in_specs = [pl.BlockSpec((tr, N_DEV), lambda i: (i, 0)), pl.BlockSpec((N_DEV, c), lambda i: (0, 0)), tile, tile, tile]
    return _pcall(name, body, (r // tr,), in_specs, [c_act_t, dmod, w, m, v], [tile] * 4, [SDS((r, c), F32)] * 4,
                  semantics=("parallel",), plan=plan)


def _rotate_half(x):
    w = x.shape[1]
    half = HEAD_DIM // 2
    lane = lax.broadcasted_iota(jnp.int32, x.shape, 1)
    return jnp.where((lane % HEAD_DIM) < half, -pltpu.roll(x, w - half, 1), pltpu.roll(x, half, 1))


def _lane_tile(tab, w):
    return tab[:, :w] if w <= LANES else jnp.tile(tab, (1, w // LANES))


def _rope(x, cos, sin):
    return x * cos + _rotate_half(x) * sin


def _rope_t(dy, cos, sin):
    return dy * cos - _rotate_half(dy) * sin


def _band_mask(n):
    shape = (Q_PER_KV * WINDOW, 2 * WINDOW)
    i = lax.broadcasted_iota(jnp.int32, shape, 0) & (WINDOW - 1)
    j = lax.broadcasted_iota(jnp.int32, shape, 1)
    return (j > i) & (j <= i + WINDOW) & ((n > 0) | (j >= WINDOW))


def _stack_heads(x, hk):
    first = hk * Q_PER_KV
    return jnp.concatenate([x[:, (first + g) * HEAD_DIM:(first + g + 1) * HEAD_DIM] for g in range(Q_PER_KV)], axis=0)


def _stack_cols(ref, hk):
    first = hk * Q_PER_KV
    return jnp.concatenate([ref[:, first + g:first + g + 1] for g in range(Q_PER_KV)], axis=0)


def _stack_sinks(sink_ref, hk):
    first = hk * Q_PER_KV
    return jnp.concatenate([jnp.broadcast_to(sink_ref[0:1, first + g:first + g + 1], (WINDOW, 1))
                            for g in range(Q_PER_KV)], axis=0)


def _attn_specs(da, dkv, nb):
    cur = lambda n: (jnp.minimum(n, nb - 1), 0)
    prev = lambda n: (jnp.maximum(jnp.minimum(n, nb - 1) - 1, 0), 0)
    return dict(
        q=pl.BlockSpec((WINDOW, da), cur), kv_cur=pl.BlockSpec((WINDOW, dkv), cur),
        kv_prev=pl.BlockSpec((WINDOW, dkv), prev), tab_cur=pl.BlockSpec((WINDOW, LANES), cur),
        tab_prev=pl.BlockSpec((WINDOW, LANES), prev))


def _attn_fwd(name, q, k, v, cos, sin, sinks, plan=None):
    s, da = q.shape
    dkv = k.shape[1]
    nq, nb = da // HEAD_DIM, s // WINDOW
    scale = HEAD_DIM ** -0.5

    def body(q_ref, kp_ref, kc_ref, vp_ref, vc_ref, cc_ref, sc_ref, cp_ref, sp_ref, sink_ref, o_ref, lse_ref):
        n = pl.program_id(0)
        cc, sc, cp, sp = cc_ref[...], sc_ref[...], cp_ref[...], sp_ref[...]
        qr = _rope(q_ref[...], _lane_tile(cc, da), _lane_tile(sc, da)).astype(BF16)
        kk = jnp.concatenate([_rope(kp_ref[...], _lane_tile(cp, dkv), _lane_tile(sp, dkv)),
                              _rope(kc_ref[...], _lane_tile(cc, dkv), _lane_tile(sc, dkv))], axis=0).astype(BF16)
        vv = jnp.concatenate([vp_ref[...], vc_ref[...]], axis=0).astype(BF16)
        valid = _band_mask(n)
        for hk in range(nq // Q_PER_KV):
            ks = slice(hk * HEAD_DIM, (hk + 1) * HEAD_DIM)
            sco = lax.dot_general(_stack_heads(qr, hk), kk[:, ks], _DOT_DIMS["nt"], preferred_element_type=F32) * scale
            sco = jnp.where(valid, sco, -1e30)
            sink = _stack_sinks(sink_ref, hk)
            mx = jnp.maximum(jnp.max(sco, axis=1, keepdims=True), sink)
            p = jnp.exp(sco - mx)
            den = jnp.sum(p, axis=1, keepdims=True) + jnp.exp(sink - mx)
            o8 = jnp.dot((p / den).astype(BF16), vv[:, ks], preferred_element_type=F32)
            lse8 = mx + jnp.log(den)
            for g in range(Q_PER_KV):
                hq, rows = hk * Q_PER_KV + g, slice(g * WINDOW, (g + 1) * WINDOW)
                o_ref[:, hq * HEAD_DIM:(hq + 1) * HEAD_DIM] = o8[rows]
                lse_ref[:, hq:hq + 1] = lse8[rows]

    sp_ = _attn_specs(da, dkv, nb)
    in_specs = [sp_["q"], sp_["kv_prev"], sp_["kv_cur"], sp_["kv_prev"], sp_["kv_cur"],
                sp_["tab_cur"], sp_["tab_cur"], sp_["tab_prev"], sp_["tab_prev"], pl.BlockSpec((1, nq), lambda n: (0, 0))]
    return _pcall(name, body, (nb,), in_specs, [q, k, k, v, v, cos, sin, cos, sin, sinks],
                  [sp_["q"], pl.BlockSpec((WINDOW, nq), lambda n: (n, 0))], [SDS((s, da), F32), SDS((s, nq), F32)],
                  semantics=("arbitrary",), plan=plan)


def _attn_bwd(name, q, k, v, cos, sin, sinks, out, lse, dout, plan=None):
    s, da = q.shape
    dkv = k.shape[1]
    nq, nb = da // HEAD_DIM, s // WINDOW
    scale = HEAD_DIM ** -0.5

    def body(q_ref, kp_ref, kc_ref, vp_ref, vc_ref, cc_ref, sc_ref, cp_ref, sp_ref, sink_ref, o_ref, lse_ref,
             do_ref, dq_ref, dk_ref, dv_ref, dsink_ref, dk_carry, dv_carry):
        n = pl.program_id(0)
        cp, sp = _lane_tile(cp_ref[...], dkv), _lane_tile(sp_ref[...], dkv)

        @pl.when(n == 0)
        def _():
            dk_carry[...] = jnp.zeros_like(dk_carry)
            dv_carry[...] = jnp.zeros_like(dv_carry)
            dsink_ref[...] = jnp.zeros_like(dsink_ref)

        @pl.when(n < nb)
        def _():
            cc, sc = cc_ref[...], sc_ref[...]
            qr = _rope(q_ref[...], _lane_tile(cc, da), _lane_tile(sc, da)).astype(BF16)
            kk = jnp.concatenate([_rope(kp_ref[...], cp, sp),
                                  _rope(kc_ref[...], _lane_tile(cc, dkv), _lane_tile(sc, dkv))], axis=0).astype(BF16)
            vv = jnp.concatenate([vp_ref[...], vc_ref[...]], axis=0).astype(BF16)
            valid = _band_mask(n)
            do_all, o_all = do_ref[...], o_ref[...]
            for hk in range(nq // Q_PER_KV):
                ks = slice(hk * HEAD_DIM, (hk + 1) * HEAD_DIM)
                q8, lse8 = _stack_heads(qr, hk), _stack_cols(lse_ref, hk)
                sco = lax.dot_general(q8, kk[:, ks], _DOT_DIMS["nt"], preferred_element_type=F32) * scale
                probs = jnp.where(valid, jnp.exp(sco - lse8), 0.0)
                do8 = _stack_heads(do_all, hk)
                delta = jnp.sum(do8 * _stack_heads(o_all, hk), axis=1, keepdims=True)
                do8 = do8.astype(BF16)
                dp = lax.dot_general(do8, vv[:, ks], _DOT_DIMS["nt"], preferred_element_type=F32)
                ds = (probs * (dp - delta) * scale).astype(BF16)
                dq8 = jnp.dot(ds, kk[:, ks], preferred_element_type=F32)
                dk_h = lax.dot_general(ds, q8, _DOT_DIMS["tn"], preferred_element_type=F32)
                dv_h = lax.dot_general(probs.astype(BF16), do8, _DOT_DIMS["tn"], preferred_element_type=F32)
                dsink8 = -jnp.exp(_stack_sinks(sink_ref, hk) - lse8) * delta
                for g in range(Q_PER_KV):
                    hq, rows = hk * Q_PER_KV + g, slice(g * WINDOW, (g + 1) * WINDOW)
                    dq_ref[:, hq * HEAD_DIM:(hq + 1) * HEAD_DIM] = dq8[rows]
                    dsink_ref[:, hq:hq + 1] += dsink8[rows].reshape(WINDOW // SUBLANES, SUBLANES, 1).sum(axis=0)
                dk_ref[:, ks] = dk_carry[:, ks] + dk_h[:WINDOW]
                dv_ref[:, ks] = dv_carry[:, ks] + dv_h[:WINDOW]
                dk_carry[:, ks] = dk_h[WINDOW:]
                dv_carry[:, ks] = dv_h[WINDOW:]
            dq_ref[...] = _rope_t(dq_ref[...], _lane_tile(cc, da), _lane_tile(sc, da))
            dk_ref[...] = _rope_t(dk_ref[...], cp, sp)

        @pl.when(n == nb)
        def _():
            dk_ref[...] = _rope_t(dk_carry[...], cp, sp)
            dv_ref[...] = dv_carry[...]

    sp_ = _attn_specs(da, dkv, nb)
    last_prev = lambda n: (jnp.maximum(n - 1, 0), 0)
    tab_prev = pl.BlockSpec((WINDOW, LANES), last_prev)
    kv_out = pl.BlockSpec((WINDOW, dkv), last_prev)
    lse_spec = pl.BlockSpec((WINDOW, nq), lambda n: (jnp.minimum(n, nb - 1), 0))
    in_specs = [sp_["q"], sp_["kv_prev"], sp_["kv_cur"], sp_["kv_prev"], sp_["kv_cur"],
                sp_["tab_cur"], sp_["tab_cur"], tab_prev, tab_prev,
                pl.BlockSpec((1, nq), lambda n: (0, 0)), sp_["q"], lse_spec, sp_["q"]]
    res = _pcall(name, body, (nb + 1,), in_specs, [q, k, k, v, v, cos, sin, cos, sin, sinks, out, lse, dout],
                 [sp_["q"], kv_out, kv_out, pl.BlockSpec((SUBLANES, nq), lambda n: (0, 0))],
                 [SDS((s, da), F32), SDS((s, dkv), F32), SDS((s, dkv), F32), SDS((SUBLANES, nq), F32)],
                 scratch=[pltpu.VMEM((WINDOW, dkv), F32), pltpu.VMEM((WINDOW, dkv), F32)],
                 semantics=("arbitrary",), plan=plan)
    (dq, dk, dv, dsink), rest = res if plan is not None else (res, None)
    return (dq, dk, dv, dsink.sum(0)), rest


def _ssm_operators(lam_re, lam_im, log_step, b_re, b_im, c_re, c_im, d_skip):
    g, p = lam_re.shape
    h = b_re.shape[-1]
    l = SSM_CHUNK
    step = jnp.exp(log_step)[:, None]
    mag = jnp.exp(lam_re * step)
    ar, ai = mag * jnp.cos(lam_im * step), mag * jnp.sin(lam_im * step)
    den = lam_re * lam_re + lam_im * lam_im
    cr = ((ar - 1.0) * lam_re + ai * lam_im) / den
    ci = (ai * lam_re - (ar - 1.0) * lam_im) / den
    bbr = cr[..., None] * b_re - ci[..., None] * b_im
    bbi = cr[..., None] * b_im + ci[..., None] * b_re
    powers = jnp.arange(l + 1, dtype=F32)[None, :, None]
    pw_mag = jnp.exp((lam_re * step)[:, None, :] * powers)
    pw_ang = (lam_im * step)[:, None, :] * powers
    pwr, pwi = pw_mag * jnp.cos(pw_ang), pw_mag * jnp.sin(pw_ang)
    cpr = c_re[:, None] * pwr[:, :, None, :] - c_im[:, None] * pwi[:, :, None, :]
    cpi = c_re[:, None] * pwi[:, :, None, :] + c_im[:, None] * pwr[:, :, None, :]
    kern = (jnp.einsum("gtop,gpi->gtoi", cpr[:, :l], bbr, precision=lax.Precision.HIGHEST)
            - jnp.einsum("gtop,gpi->gtoi", cpi[:, :l], bbi, precision=lax.Precision.HIGHEST))
    kern = kern.at[:, 0].add(d_skip.reshape(g, h)[:, :, None] * jnp.eye(h, dtype=F32))
    lag = jnp.arange(l)
    place = (lag[None, None, :] - lag[None, :, None] == lag[:, None, None]).astype(F32)
    tm = jnp.einsum("gtoh,tji->gjhio", kern, place, precision=lax.Precision.HIGHEST).reshape(g, l * h, l * h)
    rev_r, rev_i = pwr[:, l - 1::-1][:, :l], pwi[:, l - 1::-1][:, :l]
    er = rev_r[:, :, None, :] * bbr.transpose(0, 2, 1)[:, None] - rev_i[:, :, None, :] * bbi.transpose(0, 2, 1)[:, None]
    ei = rev_r[:, :, None, :] * bbi.transpose(0, 2, 1)[:, None] + rev_i[:, :, None, :] * bbr.transpose(0, 2, 1)[:, None]
    em = jnp.concatenate([er, ei], axis=-1).reshape(g, l * h, 2 * p)
    fr = cpr[:, 1:].transpose(0, 3, 1, 2).reshape(g, p, l * h)
    fi = -cpi[:, 1:].transpose(0, 3, 1, 2).reshape(g, p, l * h)
    fm = jnp.concatenate([fr, fi], axis=1)
    return tm, em, fm, pwr[:, l], pwi[:, l]


def _decay_lanes(alr, ali):
    return jnp.concatenate([alr, alr], axis=1), jnp.concatenate([-ali, ali], axis=1)


def _column_blocks(s, ds):
    wide = 2 if (ds // LANES) % 2 == 0 else 1
    return wide, pl.BlockSpec((s, wide * LANES), lambda i: (0, i // wide))


def _my_columns(wide, block_ref, stage_ref, store):
    part = pl.program_id(0) % wide
    for p in range(wide):
        @pl.when(part == p)
        def _():
            cols = slice(p * LANES, (p + 1) * LANES)
            if store:
                block_ref[:, cols] = stage_ref[...]
            else:
                stage_ref[...] = block_ref[:, cols]


def _ssm_fwd(name, u, tm, em, fm, acat, bcat, plan=None):
    s, ds = u.shape
    g, lh, p2 = em.shape
    gb, h = SSM_GROUPS_PER_STEP, lh // SSM_CHUNK
    assert gb * h == LANES and g * h == ds and s % SSM_CHUNK == 0
    nc, half = s // SSM_CHUNK, p2 // 2

    def body(u_ref, tm_ref, em_ref, fm_ref, a_ref, b_ref, y_ref, xp_ref, uc_ref, yc_ref, st_ref, col_ref):
        _my_columns(wide, u_ref, col_ref, store=False)
        _to_chunks(col_ref, uc_ref, nc, h)
        for i in range(gb):
            st_ref[pl.ds(i, nc, stride=gb), :] = jnp.dot(uc_ref[i], em_ref[i], precision=SSM_PRECISION,
                                                         preferred_element_type=F32)
        av, bv = a_ref[...], b_ref[...]

        def step(c, carry):
            x, xs = carry
            rows = pl.ds(pl.multiple_of(c * gb, gb), gb)
            loc = st_ref[rows, :]
            st_ref[rows, :] = x
            return av * x + bv * xs + loc, av * xs - bv * x + pltpu.roll(loc, half, 1)
        zero = jnp.zeros((gb, p2), F32)
        lax.fori_loop(0, nc, step, (zero, zero), unroll=4)
        for i in range(gb):
            xp = st_ref[pl.ds(i, nc, stride=gb), :]
            xp_ref[i] = xp
            yc_ref[i] = (jnp.dot(uc_ref[i], tm_ref[i], precision=SSM_PRECISION, preferred_element_type=F32)
                         + jnp.dot(xp, fm_ref[i], precision=SSM_PRECISION, preferred_element_type=F32))
        _from_chunks(yc_ref, col_ref, nc, h)
        _my_columns(wide, y_ref, col_ref, store=True)

    blk = lambda r, c: pl.BlockSpec((gb, r, c), lambda i: (i, 0, 0))
    vec = pl.BlockSpec((gb, p2), lambda i: (i, 0))
    wide, col = _column_blocks(s, ds)
    return _pcall(name, body, (g // gb,), [col, blk(lh, lh), blk(lh, p2), blk(p2, lh), vec, vec],
                  [u, tm, em, fm, acat, bcat], [col, blk(nc, p2), blk(nc, lh)],
                  [SDS((s, ds), F32), SDS((g, nc, p2), F32), SDS((g, nc, lh), F32)],
                  scratch=[pltpu.VMEM((gb, nc, lh), F32), pltpu.VMEM((nc * gb, p2), F32), pltpu.VMEM((s, LANES), F32)],
                  semantics=("arbitrary",), plan=plan)


def _ssm_bwd(name, u_chunks, dy, xprev, tm, em, fm, acat, bcat, plan=None):
    s, ds = dy.shape
    g, lh, p2 = em.shape
    gb, h = SSM_GROUPS_PER_STEP, lh // SSM_CHUNK
    nc, half = s // SSM_CHUNK, p2 // 2

    def body(uc_ref, dy_ref, xp_ref, tm_ref, em_ref, fm_ref, a_ref, b_ref,
             du_ref, dtm_ref, dem_ref, dfm_ref, r1_ref, r2_ref, dyc_ref, duc_ref, gs_ref, xs_ref, col_ref):
        _my_columns(wide, dy_ref, col_ref, store=False)
        _to_chunks(col_ref, dyc_ref, nc, h)
        for i in range(gb):
            gs_ref[pl.ds(i, nc, stride=gb), :] = lax.dot_general(
                dyc_ref[i], fm_ref[i], _DOT_DIMS["nt"], precision=SSM_PRECISION, preferred_element_type=F32)
            xs_ref[pl.ds(i, nc, stride=gb), :] = xp_ref[i]
        av, bv = a_ref[...], b_ref[...]

        def step(t, carry):
            grad, gsw, r1, r2 = carry
            c = nc - 1 - t
            rows = pl.ds(pl.multiple_of(c * gb, gb), gb)
            dxp, xp = gs_ref[rows, :], xs_ref[rows, :]
            gs_ref[rows, :] = grad
            r1 = r1 + grad * xp
            r2 = r2 + grad * pltpu.roll(xp, half, 1)
            return dxp + av * grad - bv * gsw, pltpu.roll(dxp, half, 1) + av * gsw + bv * grad, r1, r2
        zero = jnp.zeros((gb, p2), F32)
        _, _, r1, r2 = lax.fori_loop(0, nc, step, (zero, zero, zero, zero), unroll=4)
        r1_ref[...], r2_ref[...] = r1, r2
        for i in range(gb):
            dxl = gs_ref[pl.ds(i, nc, stride=gb), :]
            duc_ref[i] = (lax.dot_general(dyc_ref[i], tm_ref[i], _DOT_DIMS["nt"], precision=SSM_PRECISION,
                                          preferred_element_type=F32)
                          + lax.dot_general(dxl, em_ref[i], _DOT_DIMS["nt"], precision=SSM_PRECISION,
                                            preferred_element_type=F32))
            dtm_ref[i] = lax.dot_general(uc_ref[i], dyc_ref[i], _DOT_DIMS["tn"], precision=SSM_PRECISION,
                                         preferred_element_type=F32)
            dfm_ref[i] = lax.dot_general(xp_ref[i], dyc_ref[i], _DOT_DIMS["tn"], precision=SSM_PRECISION,
                                         preferred_element_type=F32)
            dem_ref[i] = lax.dot_general(uc_ref[i], dxl, _DOT_DIMS["tn"], precision=SSM_PRECISION,
                                         preferred_element_type=F32)
        _from_chunks(duc_ref, col_ref, nc, h)
        _my_columns(wide, du_ref, col_ref, store=True)

    blk = lambda r, c: pl.BlockSpec((gb, r, c), lambda i: (i, 0, 0))
    vec = pl.BlockSpec((gb, p2), lambda i: (i, 0))
    wide, col = _column_blocks(s, ds)
    chunked = pltpu.VMEM((gb, nc, lh), F32)
    res = _pcall(name, body, (g // gb,),
                 [blk(nc, lh), col, blk(nc, p2), blk(lh, lh), blk(lh, p2), blk(p2, lh), vec, vec],
                 [u_chunks, dy, xprev, tm, em, fm, acat, bcat],
                 [col, blk(lh, lh), blk(lh, p2), blk(p2, lh), vec, vec],
                 [SDS((s, ds), F32), SDS((g, lh, lh), F32), SDS((g, lh, p2), F32), SDS((g, p2, lh), F32),
                  SDS((g, p2), F32), SDS((g, p2), F32)],
                 scratch=[chunked, chunked, pltpu.VMEM((nc * gb, p2), F32), pltpu.VMEM((nc * gb, p2), F32),
                          pltpu.VMEM((s, LANES), F32)],
                 semantics=("arbitrary",), plan=plan)
    return res if plan is not None else (res, None)


def _to_chunks(src_ref, dst_ref, nc, h):
    per = LANES // h
    rows = min(nc, 64)
    grp = lax.broadcasted_iota(jnp.int32, (rows, LANES), 1) // h
    for r0 in range(0, nc, rows):
        for i in range(SSM_CHUNK):
            part, lo = divmod(i * h, LANES)
            step_rows = src_ref[pl.ds(r0 * SSM_CHUNK + i, rows, stride=SSM_CHUNK), :]
            for g in range(per):
                shift = (lo - g * h) % LANES
                piece = pltpu.roll(step_rows, shift, 1) if shift else step_rows
                out = dst_ref.at[g, r0:r0 + rows, part * LANES:(part + 1) * LANES]
                out[...] = piece if lo == 0 else jnp.where(grp == lo // h, piece, out[...])


def _from_chunks(src_ref, dst_ref, nc, h):
    per = LANES // h
    grp = lax.broadcasted_iota(jnp.int32, (nc, LANES), 1) // h
    for i in range(SSM_CHUNK):
        part, lo = divmod(i * h, LANES)
        row = None
        for g in range(per):
            piece = src_ref[g, :, part * LANES:(part + 1) * LANES]
            if (g * h - lo) % LANES:
                piece = pltpu.roll(piece, (g * h - lo) % LANES, 1)
            row = piece if row is None else jnp.where(grp == g, piece, row)
        dst_ref[pl.ds(i, nc, stride=SSM_CHUNK), :] = row


_SMALL = ("b_ada", "norm1_g", "sinks", "ssm_lam_re", "ssm_lam_im", "ssm_log_step", "ssm_b_re", "ssm_b_im",
          "ssm_c_re", "ssm_c_im", "ssm_d", "b_glu", "attn_out_g", "ssm_out_g", "norm2_g", "final_g")
_WEIGHTS = ("w_ada", "b_ada", "norm1_g", "w_in", "sinks", "ssm_lam_re", "ssm_lam_im", "ssm_log_step", "ssm_b_re",
            "ssm_b_im", "ssm_c_re", "ssm_c_im", "ssm_d", "w_glu", "b_glu", "attn_out_g", "ssm_out_g", "w_out",
            "norm2_g", "w_ff1", "w_ff2", "final_g")
_PACK_ALIGN = 128 * LANES


def _pack(parts):
    flat = jnp.concatenate([p.reshape(-1).astype(F32) for p in parts])
    pad = (-flat.shape[0]) % _PACK_ALIGN
    return jnp.pad(flat, (0, pad)).reshape(-1, LANES)


def kernel(x, c, w_ada, b_ada, norm1_g, w_in, sinks, ssm_lam_re, ssm_lam_im, ssm_log_step, ssm_b_re, ssm_b_im, ssm_c_re, ssm_c_im, ssm_d, w_glu, b_glu, attn_out_g, ssm_out_g, w_out, norm2_g, w_ff1, w_ff2, final_g, loss_target, m_w_ada, m_b_ada, m_norm1_g, m_w_in, m_sinks, m_ssm_lam_re, m_ssm_lam_im, m_ssm_log_step, m_ssm_b_re, m_ssm_b_im, m_ssm_c_re, m_ssm_c_im, m_ssm_d, m_w_glu, m_b_glu, m_attn_out_g, m_ssm_out_g, m_w_out, m_norm2_g, m_w_ff1, m_w_ff2, m_final_g, v_w_ada, v_b_ada, v_norm1_g, v_w_in, v_sinks, v_ssm_lam_re, v_ssm_lam_im, v_ssm_log_step, v_ssm_b_re, v_ssm_b_im, v_ssm_c_re, v_ssm_c_im, v_ssm_d, v_w_glu, v_b_glu, v_attn_out_g, v_ssm_out_g, v_w_out, v_norm2_g, v_w_ff1, v_w_ff2, v_final_g):
    args = dict(locals())
    weights = {n: args[n] for n in _WEIGHTS}
    mom = {n: args["m_" + n] for n in _WEIGHTS}
    var = {n: args["v_" + n] for n in _WEIGHTS}
    me = 4 * lax.axis_index("x") + 2 * lax.axis_index("y") + lax.axis_index("c")

    _, s, d = x.shape
    xs, tgt = x[0], loss_target[0]
    d_ssm = ssm_d.shape[-1]
    d_attn = d - d_ssm
    nq = d_attn // HEAD_DIM
    d_kv = (nq // Q_PER_KV) * HEAD_DIM
    p_state = ssm_b_re.shape[2]

    c_all, g_in = _run_plan("gather_c_w_in", _Gather([c, w_in[0].T.astype(BF16)]))
    c_all = c_all.reshape(N_DEV, d)
    w_in_t = g_in.reshape(-1, d)

    n_loc = w_ada.shape[-1]
    b_loc = lax.dynamic_slice_in_dim(b_ada, me * n_loc, n_loc, axis=1)
    silu = lambda t: t * _sigmoid(t)
    mod_part = _matmul("ada_mod", c_all, w_ada[0], "nn", [F32], a_pro=silu, vecs=[b_loc], exact=True,
                       epilogue=lambda acc, e, v: (acc + v[0],), tn=512, tk=d)
    mod_all = _run_plan("gather_mod", _Gather([mod_part]))[0]
    mod = lax.dynamic_index_in_dim(mod_all, me, axis=1, keepdims=False).reshape(N_MOD, 1, d)
    shift1, scale1, gate1, shift2, scale2, gate2 = [mod[i] for i in range(N_MOD)]

    h1 = _norm_mod_fwd("norm1", xs, norm1_g, scale1, shift1)
    q = _matmul("proj_q", h1, w_in_t, "nt", [F32], b_rows=(0, d_attn))
    kv = _matmul("proj_kv", h1, w_in_t, "nt", [F32], b_rows=(d_attn, 2 * d_kv))
    u = _matmul("proj_u", h1, w_in_t, "nt", [F32], b_rows=(d_attn + 2 * d_kv, d_ssm))
    k, v = kv[:, :d_kv], kv[:, d_kv:]

    half = HEAD_DIM // 2
    inv_freq = ROPE_THETA ** (-jnp.arange(half, dtype=F32) / half)
    ang = jnp.arange(s, dtype=F32)[:, None] * inv_freq[None, :]
    cos_t, sin_t = jnp.tile(jnp.cos(ang), (1, 4)), jnp.tile(jnp.sin(ang), (1, 4))
    (attn, lse), (g_glu, g_out) = _attn_fwd("attn_fwd", q, k, v, cos_t, sin_t, sinks,
                                            plan=_Gather([w_glu[0].astype(BF16), w_out[0].astype(BF16)]))
    w_glu_f = g_glu.reshape(d_ssm, d_ssm)
    w_out_f = g_out.reshape(d, d)

    ssm_params = (ssm_lam_re[0], ssm_lam_im[0], ssm_log_step[0], ssm_b_re[0], ssm_b_im[0], ssm_c_re[0],
                  ssm_c_im[0], ssm_d[0])
    (tm_op, em_op, fm_op, alr, ali), ssm_vjp = jax.vjp(_ssm_operators, *ssm_params)
    acat, bcat = _decay_lanes(alr, ali)
    (y_ssm, x_prev, u_chunks), (g_ff1,) = _ssm_fwd("ssm_fwd", u, tm_op, em_op, fm_op, acat, bcat,
                                                   plan=_Gather([w_ff1[0].astype(BF16)]))
    yg = _gelu_fwd("gelu", y_ssm)
    ssm_out, z_glu = _matmul(
        "glu", yg, w_glu_f, "nn", [F32, F32], extras=[y_ssm], vecs=[b_glu],
        epilogue=lambda acc, e, v: (_gelu(e[0]) * _sigmoid(acc + v[0]), acc + v[0]))
    mixed = _group_norm_fwd("group_norm", attn, ssm_out, attn_out_g, ssm_out_g)
    x2, mo = _matmul("out_proj", mixed, w_out_f, "nn", [F32, BF16], extras=[xs], vecs=[gate1],
                     epilogue=lambda acc, e, v: (e[0] + v[0] * acc, acc))

    h2 = _norm_mod_fwd("norm2", x2, norm2_g, scale2, shift2)
    (a_ff, f_ff), (g_ff2,) = _matmul("ff1", h2, g_ff1, "nn", [BF16, BF16], b_blocked=True,
                                     epilogue=lambda acc, e, v: (acc, jnp.square(jnp.maximum(acc, 0.0))),
                                     plan=_Gather([w_ff2[0].astype(BF16)]))
    w_ff2_f = g_ff2.reshape(-1, d)
    x3, ff = _matmul("ff2", f_ff, w_ff2_f, "nn", [F32, BF16], extras=[x2], vecs=[gate2],
                     epilogue=lambda acc, e, v: (e[0] + v[0] * acc, acc))

    dx3, dff, loss_local, d_final_g, d_gate2 = _final_loss("final_loss", x3, tgt, final_g.reshape(1, d), ff, gate2)
    loss = lax.psum(loss_local, MESH_AXES)

    dw_ff2 = _matmul("ff2_dw", f_ff, dff, "tn", [BF16]).reshape(N_DEV, -1, d)
    da_ff, (p_ff2,) = _matmul("ff2_dx", dff, w_ff2_f, "nt", [BF16], extras=[a_ff],
                              epilogue=lambda acc, e, v: (acc * (2.0 * jnp.maximum(e[0].astype(F32), 0.0)),),
                              plan=_PairSwap([dw_ff2]))
    s_ff2 = _pair_add("pair_add_ff2", dw_ff2, p_ff2)
    dw_ff1, (r_ff2_a,) = _matmul("ff1_dw", h2, da_ff, "tn", [BF16], out_blocked=N_DEV,
                                 plan=_ChipExchange([s_ff2], (CHIP_X, CHIP_Y)))
    dh2, (r_ff2_b,) = _matmul("ff1_dx", da_ff, g_ff1, "nt", [F32], b_blocked=True,
                              plan=_ChipExchange([s_ff2], (CHIP_DIAGONAL,)))
    (dx2, dmo, d_scale2, d_shift2, d_norm2_g, d_gate1), (p_ff1,) = _norm_mod_bwd(
        "norm2_bwd", x2, dh2, dx3, norm2_g, scale2, gated=(mo, gate1), plan=_PairSwap([dw_ff1]))
    s_ff1 = _pair_add("pair_add_ff1", dw_ff1, p_ff1)

    dw_out = _matmul("out_dw", mixed, dmo, "tn", [BF16]).reshape(N_DEV, -1, d)
    dmixed, (r_out_a,) = _matmul("out_dx", dmo, w_out_f, "nt", [F32], plan=_Exchange([dw_out], (1, 4, 2)))
    (dattn, dssm_out, d_attn_g, d_ssm_g), (r_out_b,) = _group_norm_bwd(
        "group_norm_bwd", attn, ssm_out, dmixed, attn_out_g, ssm_out_g, plan=_Exchange([dw_out], (6,)))

    dz, dyg_direct, d_b_glu = _glu_bwd("glu_bwd", dssm_out, y_ssm, z_glu)
    dw_glu = _matmul("glu_dw", yg, dz, "tn", [BF16]).reshape(N_DEV, -1, d_ssm)
    dy_ssm = _matmul("glu_dx", dz, w_glu_f, "nt", [F32], extras=[dyg_direct, y_ssm],
                     epilogue=lambda acc, e, v: ((acc + e[0]) * _gelu_grad(e[1]),))
    (du, d_tm, d_em, d_fm, r1, r2), (r_ff1_a, r_glu) = _ssm_bwd(
        "ssm_bwd", u_chunks, dy_ssm, x_prev, tm_op, em_op, fm_op, acat, bcat,
        plan=_Plans([_ChipExchange([s_ff1], (CHIP_X, CHIP_Y)), _Exchange([dw_glu], RELATIONS_ALL)]))
    d_alr = r1[:, :p_state] + r1[:, p_state:]
    d_ali = r2[:, p_state:] - r2[:, :p_state]
    d_ssm_params = ssm_vjp((d_tm, d_em, d_fm, d_alr, d_ali))

    (dq, dk, dv, d_sinks), (r_ff1_b,) = _attn_bwd("attn_bwd", q, k, v, cos_t, sin_t, sinks, attn, lse, dattn,
                                                  plan=_ChipExchange([s_ff1], (CHIP_DIAGONAL,)))
    dproj = jnp.concatenate([dq, dk, dv, du], axis=1).astype(BF16)
    dw_in_t, (r_out_c,) = _matmul("in_dw", dproj, h1, "tn", [BF16], plan=_Exchange([dw_out], (5, 3, 7)))
    dw_in_t = dw_in_t.reshape(N_DEV, -1, d)
    dh1, (r_in_a,) = _matmul("in_dx", dproj, w_in_t, "nn", [F32], plan=_Exchange([dw_in_t], RELATIONS_SAME_CORE))
    (grad_x, d_scale1, d_shift1, d_norm1_g), _ = _norm_mod_bwd("norm1_bwd", xs, dh1, dx2, norm1_g, scale1)

    d_mod = jnp.concatenate([d_shift1, d_scale1, d_gate1, d_shift2, d_scale2, d_gate2])
    small_g = dict(zip(("ssm_lam_re", "ssm_lam_im", "ssm_log_step", "ssm_b_re", "ssm_b_im", "ssm_c_re", "ssm_c_im",
                        "ssm_d"), d_ssm_params, strict=True))
    small_g.update(b_ada=d_mod, norm1_g=d_norm1_g, sinks=d_sinks, b_glu=d_b_glu, attn_out_g=d_attn_g,
                   ssm_out_g=d_ssm_g, norm2_g=d_norm2_g, final_g=d_final_g)
    small_parts, r_in_b = _run_plan("gather_small_grads", _Plans([_Gather([_pack([small_g[n] for n in _SMALL])]),
                                                                  _Exchange([dw_in_t], RELATIONS_OTHER_CORE)]))
    small = _adam_shard("adam_small", [small_parts], _pack([weights[n] for n in _SMALL]),
                        _pack([mom[n] for n in _SMALL]), _pack([var[n] for n in _SMALL]))
    out = {}
    off = 0
    for n in _SMALL:
        size = weights[n].size
        out[n] = [t.reshape(-1)[off:off + size].reshape(weights[n].shape) for t in small]
        off += size

    dmod_all = small_parts.reshape(N_DEV, -1)[:, :N_MOD * d]
    dmod_loc = lax.dynamic_slice_in_dim(dmod_all, me * n_loc, n_loc, axis=1)
    c_act_t = silu(c_all).T
    out["w_ada"] = [t[None] for t in _ada_update("adam_w_ada", c_act_t, dmod_loc, w_ada[0], m_w_ada[0], v_w_ada[0])]

    mine = lambda blocks: lax.dynamic_index_in_dim(blocks, me, axis=0, keepdims=False)
    in_parts = [mine(dw_in_t).T] + [r.transpose(0, 2, 1) for r in (r_in_a, r_in_b)]
    my_chip = 2 * lax.axis_index("x") + lax.axis_index("y")
    chip_sum = lambda sums: lax.dynamic_index_in_dim(sums, my_chip, axis=0, keepdims=False)
    received = dict(w_in=in_parts, w_glu=[mine(dw_glu), r_glu], w_out=[mine(dw_out), r_out_a, r_out_b, r_out_c],
                    w_ff1=[chip_sum(s_ff1), r_ff1_a, r_ff1_b], w_ff2=[chip_sum(s_ff2), r_ff2_a, r_ff2_b])
    for n, parts in received.items():
        out[n] = [t[None] for t in _adam_shard("adam_" + n, parts, weights[n][0], mom[n][0], var[n][0])]

    return (loss, grad_x[None], *[out[n][0] for n in _WEIGHTS], *[out[n][1] for n in _WEIGHTS],
            *[out[n][2] for n in _WEIGHTS], *[out[n][3] for n in _WEIGHTS])
```

```python
import math

import jax
import jax.numpy as jnp
from jax import lax
from jax.experimental import pallas as pl
from jax.experimental.pallas import tpu as pltpu

F32, BF16 = jnp.float32, jnp.bfloat16
SDS = jax.ShapeDtypeStruct
MESH_AXES = ("x", "y", "c")
N_DEV = 8
VMEM_LIMIT_BYTES = 56 * 1024 * 1024
MATMUL_VMEM_BUDGET = 44 * 1024 * 1024
SUBLANES, LANES = 8, 128

HEAD_DIM = 64
Q_PER_KV = 8
WINDOW = 128
ROPE_THETA = 10000.0
EPS = 1e-6
N_MOD = 6
SSM_CHUNK = 16
SSM_GROUPS_PER_STEP = 8

ADAM_LR, ADAM_B1, ADAM_B2, ADAM_EPS, ADAM_WD, ADAM_STEP = 0.001, 0.9, 0.999, 1e-08, 0.01, 10
HIGHEST = lax.Precision.HIGHEST
SSM_PRECISION = lax.Precision.HIGH

RELATIONS_ALL = (1, 4, 2, 6, 5, 3, 7)
RELATIONS_SAME_CORE = (1, 4, 2, 6)
RELATIONS_OTHER_CORE = (5, 3, 7)
PLAN_MIDDLE = 0.6


def _cparams(sem):
    return pltpu.CompilerParams(dimension_semantics=sem, vmem_limit_bytes=VMEM_LIMIT_BYTES)


def _block_index(p):
    return 4 * p[0] + 2 * p[1] + p[2]


def _me():
    return lax.axis_index("x"), lax.axis_index("y"), lax.axis_index("c")


class _Plan:
    def middle(self, ins, outs, send, recv, local):
        pass


class _Gather(_Plan):
    TO_SIBLING, TO_X, TO_Y, RELAY, PASS_X, PASS_Y, PASS_DIAGONAL = range(7)

    def __init__(self, arrs):
        self.ins = list(arrs)
        self.out_shapes = [SDS((N_DEV,) + a.shape, a.dtype) for a in arrs]
        self.n_rdma, self.n_local = 7 * len(arrs), len(arrs)
        self.rdma_base = self.local_base = 0

    def _copy(self, ins, outs, send, recv, a, k, block, to, from_input=False):
        dst = outs[a].at[_block_index(block)]
        sem = self.rdma_base + a * 7 + k
        return pltpu.make_async_remote_copy(
            src_ref=ins[a] if from_input else dst, dst_ref=dst, send_sem=send.at[sem], recv_sem=recv.at[sem],
            device_id=to, device_id_type=pl.DeviceIdType.MESH)

    @staticmethod
    def _places():
        x, y, c = _me()
        return (x, y, c), (x, y, 1 - c), (1 - x, y, c), (x, 1 - y, c), (1 - x, 1 - y, c)

    def _first(self, ins, outs, send, recv, a):
        me, sibling, x_nbr, y_nbr, _ = self._places()
        return [self._copy(ins, outs, send, recv, a, k, me, to, True)
                for k, to in ((self.TO_SIBLING, sibling), (self.TO_X, x_nbr), (self.TO_Y, y_nbr))]

    def _mine(self, ins, outs, local, a):
        return pltpu.make_async_copy(ins[a], outs[a].at[_block_index(_me())], local.at[self.local_base + a])

    def start(self, ins, outs, send, recv, local):
        for a in range(len(ins)):
            self._mine(ins, outs, local, a).start()
            for cp in self._first(ins, outs, send, recv, a):
                cp.start()

    def middle(self, ins, outs, send, recv, local):
        me, sibling, x_nbr, y_nbr, _ = self._places()
        core = me[2]
        for a in range(len(ins)):
            self._copy(ins, outs, send, recv, a, self.TO_X, x_nbr, me).wait_recv()
            self._copy(ins, outs, send, recv, a, self.TO_Y, y_nbr, me).wait_recv()

            @pl.when(core == 0)
            def _():
                self._copy(ins, outs, send, recv, a, self.RELAY, x_nbr, y_nbr).start()

            @pl.when(core == 1)
            def _():
                self._copy(ins, outs, send, recv, a, self.RELAY, y_nbr, x_nbr).start()

            self._copy(ins, outs, send, recv, a, self.PASS_X, x_nbr, sibling).start()
            self._copy(ins, outs, send, recv, a, self.PASS_Y, y_nbr, sibling).start()

    def finish(self, ins, outs, send, recv, local):
        me, sibling, x_nbr, y_nbr, diagonal = self._places()
        other = lambda p: (p[0], p[1], 1 - p[2])
        for a in range(len(ins)):
            self._copy(ins, outs, send, recv, a, self.RELAY, diagonal, me).wait_recv()
            self._copy(ins, outs, send, recv, a, self.PASS_DIAGONAL, diagonal, sibling).start()
        for a in range(len(ins)):
            self._copy(ins, outs, send, recv, a, self.TO_SIBLING, sibling, me).wait_recv()
            for k, src in ((self.PASS_X, x_nbr), (self.PASS_Y, y_nbr), (self.PASS_DIAGONAL, diagonal)):
                self._copy(ins, outs, send, recv, a, k, other(src), me).wait_recv()
                self._copy(ins, outs, send, recv, a, k, src, sibling).wait_send()
            for cp in self._first(ins, outs, send, recv, a):
                cp.wait_send()
            self._copy(ins, outs, send, recv, a, self.RELAY, me, me).wait_send()
            self._mine(ins, outs, local, a).wait()


class _Exchange(_Plan):
    def __init__(self, arrs, relations):
        self.ins, self.relations = list(arrs), tuple(relations)
        self.out_shapes = [SDS((len(relations),) + a.shape[1:], a.dtype) for a in arrs]
        self.n_rdma, self.n_local = len(relations) * len(arrs), 0
        self.rdma_base = self.local_base = 0

    def _copies(self, ins, outs, send, recv):
        x, y, c = _me()
        cps = []
        for a in range(len(ins)):
            for s, k in enumerate(self.relations):
                peer = ((1 - x) if (k & 4) else x, (1 - y) if (k & 2) else y, (1 - c) if (k & 1) else c)
                sem = self.rdma_base + a * len(self.relations) + s
                cps.append(pltpu.make_async_remote_copy(
                    src_ref=ins[a].at[_block_index(peer)], dst_ref=outs[a].at[s], send_sem=send.at[sem],
                    recv_sem=recv.at[sem], device_id=peer, device_id_type=pl.DeviceIdType.MESH))
        return cps

    def start(self, ins, outs, send, recv, local):
        for cp in self._copies(ins, outs, send, recv):
            cp.start()

    def finish(self, ins, outs, send, recv, local):
        for cp in self._copies(ins, outs, send, recv):
            cp.wait()


class _PairSwap(_Plan):
    def __init__(self, arrs):
        self.ins = list(arrs)
        self.out_shapes = [SDS((4,) + a.shape[1:], a.dtype) for a in arrs]
        self.n_rdma, self.n_local = 4 * len(arrs), 0
        self.rdma_base = self.local_base = 0

    def _copies(self, ins, outs, send, recv):
        x, y, c = _me()
        cps = []
        for a in range(len(ins)):
            for s in range(4):
                sem = self.rdma_base + a * 4 + s
                cps.append(pltpu.make_async_remote_copy(
                    src_ref=ins[a].at[2 * s + (1 - c)], dst_ref=outs[a].at[s], send_sem=send.at[sem],
                    recv_sem=recv.at[sem], device_id=(x, y, 1 - c), device_id_type=pl.DeviceIdType.MESH))
        return cps

    def start(self, ins, outs, send, recv, local):
        for cp in self._copies(ins, outs, send, recv):
            cp.start()

    def finish(self, ins, outs, send, recv, local):
        for cp in self._copies(ins, outs, send, recv):
            cp.wait()


CHIP_X, CHIP_Y, CHIP_DIAGONAL = (1, 0), (0, 1), (1, 1)


class _ChipExchange(_Plan):
    def __init__(self, arrs, hops):
        self.ins, self.hops = list(arrs), tuple(hops)
        self.out_shapes = [SDS((len(hops),) + a.shape[1:], a.dtype) for a in arrs]
        self.n_rdma, self.n_local = len(hops) * len(arrs), 0
        self.rdma_base = self.local_base = 0

    def _copies(self, ins, outs, send, recv):
        x, y, c = _me()
        cps = []
        for a in range(len(ins)):
            for s, (fx, fy) in enumerate(self.hops):
                px, py = (1 - x) if fx else x, (1 - y) if fy else y
                sem = self.rdma_base + a * len(self.hops) + s
                cps.append(pltpu.make_async_remote_copy(
                    src_ref=ins[a].at[2 * px + py], dst_ref=outs[a].at[s], send_sem=send.at[sem],
                    recv_sem=recv.at[sem], device_id=(px, py, c), device_id_type=pl.DeviceIdType.MESH))
        return cps

    def start(self, ins, outs, send, recv, local):
        for cp in self._copies(ins, outs, send, recv):
            cp.start()

    def finish(self, ins, outs, send, recv, local):
        for cp in self._copies(ins, outs, send, recv):
            cp.wait()


class _Plans:
    def __init__(self, plans):
        self.plans = list(plans)
        self.ins = [a for p in plans for a in p.ins]
        self.out_shapes = [s for p in plans for s in p.out_shapes]
        self.n_rdma = self.n_local = 0
        for p in plans:
            p.rdma_base, p.local_base = self.n_rdma, self.n_local
            self.n_rdma, self.n_local = self.n_rdma + p.n_rdma, self.n_local + p.n_local

    def _each(self, ins, outs):
        i = o = 0
        for p in self.plans:
            yield p, ins[i:i + len(p.ins)], outs[o:o + len(p.out_shapes)]
            i, o = i + len(p.ins), o + len(p.out_shapes)

    def start(self, ins, outs, send, recv, local):
        for p, pi, po in self._each(ins, outs):
            p.start(pi, po, send, recv, local)

    def middle(self, ins, outs, send, recv, local):
        for p, pi, po in self._each(ins, outs):
            p.middle(pi, po, send, recv, local)

    def finish(self, ins, outs, send, recv, local):
        for p, pi, po in self._each(ins, outs):
            p.finish(pi, po, send, recv, local)


def _plan_scratch(plan):
    return [pltpu.SemaphoreType.DMA((plan.n_rdma,)), pltpu.SemaphoreType.DMA((plan.n_rdma,)),
            pltpu.SemaphoreType.DMA((max(plan.n_local, 1),))]


def _run_plan(name, plan):
    n = len(plan.ins)

    def body(*refs):
        ins, outs, sems = refs[:n], refs[n:len(refs) - 3], refs[len(refs) - 3:]
        plan.start(ins, outs, *sems)
        plan.middle(ins, outs, *sems)
        plan.finish(ins, outs, *sems)

    any_spec = pl.BlockSpec(memory_space=pl.ANY)
    return pl.pallas_call(body, name=name, out_shape=list(plan.out_shapes), in_specs=[any_spec] * n,
                          out_specs=[any_spec] * len(plan.out_shapes), scratch_shapes=_plan_scratch(plan))(*plan.ins)


def _pcall(name, body, grid, in_specs, ins, out_specs, out_shape, scratch=(), semantics=None, plan=None):
    if plan is None:
        return pl.pallas_call(body, name=name, grid=grid, in_specs=list(in_specs), out_specs=list(out_specs),
                              out_shape=list(out_shape), scratch_shapes=list(scratch),
                              compiler_params=_cparams(semantics))(*ins)
    n_in, n_out, n_scr = len(ins), len(out_shape), len(scratch)
    p_in, p_out = len(plan.ins), len(plan.out_shapes)

    def with_plan(*refs):
        k_in, c_in = refs[:n_in], refs[n_in:n_in + p_in]
        refs = refs[n_in + p_in:]
        k_out, c_out = refs[:n_out], refs[n_out:n_out + p_out]
        refs = refs[n_out + p_out:]
        k_scr, sems = refs[:n_scr], refs[n_scr:]
        step = 0
        for d, g in enumerate(grid):
            step = step * g + pl.program_id(d)
        n_steps = math.prod(grid)

        @pl.when(step == 0)
        def _():
            plan.start(c_in, c_out, *sems)

        @pl.when(step == min(n_steps - 1, int(n_steps * PLAN_MIDDLE)))
        def _():
            plan.middle(c_in, c_out, *sems)

        body(*k_in, *k_out, *k_scr)

        @pl.when(step == n_steps - 1)
        def _():
            plan.finish(c_in, c_out, *sems)

    any_spec = pl.BlockSpec(memory_space=pl.ANY)
    res = pl.pallas_call(
        with_plan, name=name, grid=grid, in_specs=list(in_specs) + [any_spec] * p_in,
        out_specs=list(out_specs) + [any_spec] * p_out, out_shape=list(out_shape) + list(plan.out_shapes),
        scratch_shapes=list(scratch) + _plan_scratch(plan),
        compiler_params=_cparams(("arbitrary",) * len(grid)))(*ins, *plan.ins)
    return res[:n_out], res[n_out:]


def _rowwise(name, fn, rows, vecs, row_outs, acc_outs=(), tm=128, plan=None):
    t = rows[0].shape[0]
    tm = min(tm, t)
    assert t % tm == 0 and tm % SUBLANES == 0
    n_r, n_v, n_o = len(rows), len(vecs), len(row_outs)

    def body(*refs):
        r_in, v_in = refs[:n_r], refs[n_r:n_r + n_v]
        r_out, a_out = refs[n_r + n_v:n_r + n_v + n_o], refs[n_r + n_v + n_o:]
        outs, accs = fn([r[...] for r in r_in], [v[...] for v in v_in])
        for o_ref, o in zip(r_out, outs, strict=True):
            o_ref[...] = o.astype(o_ref.dtype)
        if a_out:
            @pl.when(pl.program_id(0) == 0)
            def _():
                for a_ref in a_out:
                    a_ref[...] = jnp.zeros_like(a_ref)
            for a_ref, a in zip(a_out, accs, strict=True):
                a_ref[...] += a.reshape(tm // SUBLANES, SUBLANES, a.shape[-1]).sum(axis=0)

    in_specs = [pl.BlockSpec((tm, r.shape[1]), lambda i: (i, 0)) for r in rows]
    in_specs += [pl.BlockSpec(v.shape, lambda i: (0, 0)) for v in vecs]
    out_specs = [pl.BlockSpec((tm, w), lambda i: (i, 0)) for w, _ in row_outs]
    out_specs += [pl.BlockSpec((SUBLANES, w), lambda i: (0, 0)) for w in acc_outs]
    out_shape = [SDS((t, w), dt) for w, dt in row_outs] + [SDS((SUBLANES, w), F32) for w in acc_outs]
    return _pcall(name, body, (t // tm,), in_specs, [*rows, *vecs], out_specs, out_shape, semantics=("arbitrary",),
                  plan=plan)


def _tile(n, want):
    if n <= want:
        return n
    for t in range(want // LANES * LANES, 0, -LANES):
        if n % t == 0:
            return t
    raise ValueError(f"no tile for {n}")


_DOT_DIMS = {"nn": (((1,), (0,)), ((), ())), "nt": (((1,), (1,)), ((), ())), "tn": (((0,), (0,)), ((), ()))}


def _matmul(name, a, b, mode, out_dtypes, epilogue=None, extras=(), vecs=(), a_pro=None,
            tm=1024, tn=512, tk=4096, exact=False, b_blocked=False, out_blocked=0, b_rows=None, plan=None):
    cs = b.shape[-1] if b_blocked else None
    b2 = (b.shape[1], b.shape[0] * b.shape[2]) if b_blocked else b.shape
    if mode == "tn":
        (k, m), (k2, n) = a.shape, b2
    elif mode == "nt":
        (m, k), (n, k2) = a.shape, b2
    else:
        (m, k), (k2, n) = a.shape, b2
    assert k == k2 and not (b_blocked and mode == "tn")
    row0 = 0
    if b_rows is not None:
        assert mode == "nt" and not b_blocked
        row0, n = b_rows
        tn = _tile(math.gcd(n, row0) if row0 else n, tn)
    tm, tn, tk = _tile(m, tm), _tile(n, tn), _tile(k, tk)
    if b_blocked and mode == "nn":
        tn = _tile(cs, tn)
    if b_blocked and mode == "nt":
        tk = _tile(cs, tk)
    if out_blocked:
        tn = _tile(n // out_blocked, tn)
    nk = k // tk

    def vmem_bytes(width):
        operands = 2 * (tm * tk * a.dtype.itemsize + tk * width * b.dtype.itemsize)
        tiles = 2 * tm * width * (sum(jnp.dtype(dt).itemsize for dt in out_dtypes) + sum(e.dtype.itemsize for e in extras))
        return operands + tiles + tm * width * 4 * (2 if nk > 1 else 1)

    extent = cs if (b_blocked and mode == "nn") else n // out_blocked if out_blocked else n
    if extent % (2 * tn) == 0 and row0 % (2 * tn) == 0 and vmem_bytes(2 * tn) <= MATMUL_VMEM_BUDGET:
        tn *= 2
    n_e, n_v, n_o = len(extras), len(vecs), len(out_dtypes)
    precision = HIGHEST if exact else None

    def body(*refs):
        a_ref, b_ref = refs[:2]
        e_refs, v_refs = refs[2:2 + n_e], refs[2 + n_e:2 + n_e + n_v]
        o_refs = refs[2 + n_e + n_v:2 + n_e + n_v + n_o]

        def product():
            av = a_ref[...]
            if a_pro is not None:
                av = a_pro(av)
            return lax.dot_general(av, b_ref[...], _DOT_DIMS[mode], precision=precision, preferred_element_type=F32)

        def finish(acc):
            res = (acc,) if epilogue is None else epilogue(acc, [e[...] for e in e_refs], [v[...] for v in v_refs])
            for o_ref, r in zip(o_refs, res, strict=True):
                o_ref[...] = r.astype(o_ref.dtype)

        if nk == 1:
            finish(product())
            return
        acc_ref = refs[-1]
        kk = pl.program_id(2)

        @pl.when(kk == 0)
        def _():
            acc_ref[...] = product()

        @pl.when(kk > 0)
        def _():
            acc_ref[...] += product()

        @pl.when(kk == nk - 1)
        def _():
            finish(acc_ref[...])

    if mode == "tn":
        a_spec = pl.BlockSpec((tk, tm), lambda i, j, kk: (kk, i))
    else:
        a_spec = pl.BlockSpec((tm, tk), lambda i, j, kk: (i, kk))
    if b_blocked and mode == "nn":
        per = cs // tn
        b_spec = pl.BlockSpec((None, tk, tn), lambda i, j, kk: (j // per, kk, j % per))
    elif b_blocked:
        per = cs // tk
        b_spec = pl.BlockSpec((None, tn, tk), lambda i, j, kk: (kk // per, j, kk % per))
    elif mode == "nt":
        assert row0 % tn == 0
        b_spec = pl.BlockSpec((tn, tk), lambda i, j, kk: (j + row0 // tn, kk))
    else:
        b_spec = pl.BlockSpec((tk, tn), lambda i, j, kk: (kk, j))
    tile = pl.BlockSpec((tm, tn), lambda i, j, kk: (i, j))
    if out_blocked:
        per_o = n // out_blocked // tn
        out_spec = pl.BlockSpec((None, tm, tn), lambda i, j, kk: (j // per_o, i, j % per_o))
        out_shape = [SDS((out_blocked, m, n // out_blocked), dt) for dt in out_dtypes]
    else:
        out_spec, out_shape = tile, [SDS((m, n), dt) for dt in out_dtypes]
    in_specs = [a_spec, b_spec] + [tile] * n_e + [pl.BlockSpec((1, tn), lambda i, j, kk: (0, j))] * n_v
    res = _pcall(name, body, (m // tm, n // tn, nk), in_specs, [a, b, *extras, *vecs], [out_spec] * n_o, out_shape,
                 scratch=[pltpu.VMEM((tm, tn), F32)] if nk > 1 else [],
                 semantics=("parallel", "parallel", "arbitrary"), plan=plan)
    if plan is None:
        return res[0] if n_o == 1 else res
    return (res[0][0] if n_o == 1 else res[0]), res[1]


def _rms_fwd(x):
    r = lax.rsqrt(jnp.mean(x * x, axis=-1, keepdims=True) + EPS)
    return x * r, r


def _rms_bwd(dxn, xn, r):
    return r * (dxn - xn * jnp.mean(dxn * xn, axis=-1, keepdims=True))


_INV_SQRT2 = 1.0 / math.sqrt(2.0)
_INV_SQRT2PI = 1.0 / math.sqrt(2.0 * math.pi)


def _gelu(y):
    return 0.5 * y * (1.0 + lax.erf(y * _INV_SQRT2))


def _gelu_grad(y):
    return 0.5 * (1.0 + lax.erf(y * _INV_SQRT2)) + y * (_INV_SQRT2PI * jnp.exp(-0.5 * y * y))


def _sigmoid(z):
    return 1.0 / (1.0 + jnp.exp(-z))


def _adam_math(w, g, m, v):
    m = ADAM_B1 * m + (1.0 - ADAM_B1) * g
    v = ADAM_B2 * v + (1.0 - ADAM_B2) * (g * g)
    m_hat = m / (1.0 - ADAM_B1 ** ADAM_STEP)
    v_hat = v / (1.0 - ADAM_B2 ** ADAM_STEP)
    delta = -ADAM_LR * (m_hat / (jnp.sqrt(v_hat) + ADAM_EPS) + ADAM_WD * w)
    return delta, m, v


def _norm_mod_fwd(name, x, g, scale, shift):
    def fn(rows, vecs):
        (xv,), (gv, sc, sh) = rows, vecs
        xn, _ = _rms_fwd(xv)
        return [(xn * gv) * (1.0 + sc) + sh], []
    return _rowwise(name, fn, [x], [g, scale, shift], [(x.shape[1], BF16)])[0]


def _norm_mod_bwd(name, x, dh, dres, g, scale, gated=None, plan=None):
    d = x.shape[1]

    def fn(rows, vecs):
        xv, dhv, drv = rows[:3]
        gv, sc = vecs[:2]
        xn, r = _rms_fwd(xv)
        t = xn * gv
        dt = dhv * (1.0 + sc)
        dx = drv + _rms_bwd(dt * gv, xn, r)
        if gated is None:
            return [dx], [dhv * t, dhv, dt * xn]
        return [dx, dx * vecs[2]], [dhv * t, dhv, dt * xn, dx * rows[3].astype(F32)]
    extra_rows, extra_vecs = ([gated[0]], [gated[1]]) if gated is not None else ([], [])
    res = _rowwise(name, fn, [x, dh, dres] + extra_rows, [g, scale] + extra_vecs,
                   [(d, F32)] + [(d, BF16)] * len(extra_rows), [d] * (3 + len(extra_rows)), plan=plan)
    outs, rest = res if plan is not None else (res, None)
    n_rows = 1 + len(extra_rows)
    return (*outs[:n_rows], *[a.sum(0) for a in outs[n_rows:]]), rest


def _final_loss(name, x, tgt, g, val, gate):
    d = x.shape[1]

    def fn(rows, vecs):
        (xv, tv, vv), (gv, gate_v) = rows, vecs
        xn, r = _rms_fwd(xv)
        e = xn * gv - tv
        dy = e * (1.0 / d)
        dx = _rms_bwd(dy * gv, xn, r)
        return [dx, dx * gate_v], [e * e, dy * xn, dx * vv.astype(F32)]
    dx, dval, sq, dg, dgate = _rowwise(name, fn, [x, tgt, val], [g, gate], [(d, F32), (d, BF16)], [d, d, d])
    return dx, dval, 0.5 * jnp.sum(sq) / d, dg.sum(0), dgate.sum(0)


def _group_norm_fwd(name, attn, ssm, g_a, g_s):
    def fn(rows, vecs):
        (av, sv), (ga, gs) = rows, vecs
        return [jnp.concatenate([_rms_fwd(av)[0] * ga, _rms_fwd(sv)[0] * gs], axis=1)], []
    return _rowwise(name, fn, [attn, ssm], [g_a, g_s], [(attn.shape[1] + ssm.shape[1], BF16)])[0]


def _group_norm_bwd(name, attn, ssm, dmixed, g_a, g_s, plan=None):
    da_w, ds_w = attn.shape[1], ssm.shape[1]

    def fn(rows, vecs):
        (av, sv, dm), (ga, gs) = rows, vecs
        an, ra = _rms_fwd(av)
        sn, rs = _rms_fwd(sv)
        dma, dms = dm[:, :da_w], dm[:, da_w:]
        return [_rms_bwd(dma * ga, an, ra), _rms_bwd(dms * gs, sn, rs)], [dma * an, dms * sn]
    res = _rowwise(name, fn, [attn, ssm, dmixed], [g_a, g_s], [(da_w, F32), (ds_w, F32)], [da_w, ds_w], plan=plan)
    (dattn, dssm, dga, dgs), rest = res if plan is not None else (res, None)
    return (dattn, dssm, dga.sum(0), dgs.sum(0)), rest


def _gelu_fwd(name, y):
    def fn(rows, vecs):
        return [_gelu(rows[0])], []
    return _rowwise(name, fn, [y], [], [(y.shape[1], BF16)])[0]


def _glu_bwd(name, dout, y, z):
    d = y.shape[1]

    def fn(rows, vecs):
        dov, yv, zv = rows
        sg = _sigmoid(zv)
        dz = dov * _gelu(yv) * sg * (1.0 - sg)
        return [dz, dov * sg], [dz]
    dz, dyg, db = _rowwise(name, fn, [dout, y, z], [], [(d, BF16), (d, F32)], [d])
    return dz, dyg, db.sum(0)


def _adam_shard(name, parts, w, m, v):
    r, c = w.shape
    n_parts = sum(1 if p.ndim == 2 else p.shape[0] for p in parts)
    row_bytes = 2 * c * (n_parts * parts[0].dtype.itemsize + 7 * 4)
    tr = min(128, r)
    while tr > SUBLANES and tr * row_bytes > VMEM_LIMIT_BYTES // 2:
        tr //= 2
    assert r % tr == 0
    n_p = len(parts)

    def body(*refs):
        p_refs, (w_ref, m_ref, v_ref, g_out, d_out, m_out, v_out) = refs[:n_p], refs[n_p:]
        g = None
        for p_ref in p_refs:
            terms = [p_ref[...]] if len(p_ref.shape) == 2 else [p_ref[j] for j in range(p_ref.shape[0])]
            for t in terms:
                g = t.astype(F32) if g is None else g + t.astype(F32)
        delta, m_new, v_new = _adam_math(w_ref[...], g, m_ref[...], v_ref[...])
        g_out[...], d_out[...], m_out[...], v_out[...] = g, delta, m_new, v_new

    tile = pl.BlockSpec((tr, c), lambda i: (i, 0))
    p_specs = [tile if p.ndim == 2 else pl.BlockSpec((p.shape[0], tr, c), lambda i: (0, i, 0)) for p in parts]
    return _pcall(name, body, (r // tr,), p_specs + [tile] * 3, [*parts, w, m, v], [tile] * 4, [SDS((r, c), F32)] * 4,
                  semantics=("parallel",))


def _pair_add(name, blocks, from_sibling):
    _, r, c = blocks.shape
    tr = min(256, r)
    assert r % tr == 0

    def body(core_ref, b_ref, s_ref, o_ref):
        o_ref[...] = (b_ref[...].astype(F32) + s_ref[...].astype(F32)).astype(o_ref.dtype)

    core = lax.axis_index("c").astype(jnp.int32).reshape(1)
    mine = pl.BlockSpec((None, tr, c), lambda s, i, core_ref: (2 * s + core_ref[0], i, 0))
    slot = pl.BlockSpec((None, tr, c), lambda s, i, core_ref: (s, i, 0))
    grid_spec = pltpu.PrefetchScalarGridSpec(num_scalar_prefetch=1, grid=(4, r // tr), in_specs=[mine, slot],
                                             out_specs=slot)
    return pl.pallas_call(body, name=name, grid_spec=grid_spec, out_shape=SDS((4, r, c), blocks.dtype),
                          compiler_params=_cparams(("parallel", "parallel")))(core, blocks, from_sibling)


def _ada_update(name, c_act_t, dmod, w, m, v, tr=128, plan=None):
    r, c = w.shape
    tr = min(tr, r)
    assert r % tr == 0

    def body(c_ref, d_ref, w_ref, m_ref, v_ref, g_out, d_out, m_out, v_out):
        g = jnp.dot(c_ref[...], d_ref[...], precision=lax.Precision.HIGHEST, preferred_element_type=F32)
        delta, m_new, v_new = _adam_math(w_ref[...], g, m_ref[...], v_ref[...])
        g_out[...], d_out[...], m_out[...], v_out[...] = g, delta, m_new, v_new

    tile = pl.BlockSpec((tr, c), lambda i: (i, 0))
    in_specs = [pl.BlockSpec((tr, N_DEV), lambda i: (i, 0)), pl.BlockSpec((N_DEV, c), lambda i: (0, 0)), tile, tile, tile]
    return _pcall(name, body, (r // tr,), in_specs, [c_act_t, dmod, w, m, v], [tile] * 4, [SDS((r, c), F32)] * 4,
                  semantics=("parallel",), plan=plan)


def _rotate_half(x):
    w = x.shape[1]
    half = HEAD_DIM // 2
    lane = lax.broadcasted_iota(jnp.int32, x.shape, 1)
    return jnp.where((lane % HEAD_DIM) < half, -pltpu.roll(x, w - half, 1), pltpu.roll(x, half, 1))


def _lane_tile(tab, w):
    return tab[:, :w] if w <= LANES else jnp.tile(tab, (1, w // LANES))


def _rope(x, cos, sin):
    return x * cos + _rotate_half(x) * sin


def _rope_t(dy, cos, sin):
    return dy * cos - _rotate_half(dy) * sin


def _band_mask(n):
    shape = (Q_PER_KV * WINDOW, 2 * WINDOW)
    i = lax.broadcasted_iota(jnp.int32, shape, 0) & (WINDOW - 1)
    j = lax.broadcasted_iota(jnp.int32, shape, 1)
    return (j > i) & (j <= i + WINDOW) & ((n > 0) | (j >= WINDOW))


def _stack_heads(x, hk):
    first = hk * Q_PER_KV
    return jnp.concatenate([x[:, (first + g) * HEAD_DIM:(first + g + 1) * HEAD_DIM] for g in range(Q_PER_KV)], axis=0)


def _stack_cols(ref, hk):
    first = hk * Q_PER_KV
    return jnp.concatenate([ref[:, first + g:first + g + 1] for g in range(Q_PER_KV)], axis=0)


def _stack_sinks(sink_ref, hk):
    first = hk * Q_PER_KV
    return jnp.concatenate([jnp.broadcast_to(sink_ref[0:1, first + g:first + g + 1], (WINDOW, 1))
                            for g in range(Q_PER_KV)], axis=0)


def _attn_specs(da, dkv, nb):
    cur = lambda n: (jnp.minimum(n, nb - 1), 0)
    prev = lambda n: (jnp.maximum(jnp.minimum(n, nb - 1) - 1, 0), 0)
    return dict(
        q=pl.BlockSpec((WINDOW, da), cur), kv_cur=pl.BlockSpec((WINDOW, dkv), cur),
        kv_prev=pl.BlockSpec((WINDOW, dkv), prev), tab_cur=pl.BlockSpec((WINDOW, LANES), cur),
        tab_prev=pl.BlockSpec((WINDOW, LANES), prev))


def _attn_fwd(name, q, k, v, cos, sin, sinks, plan=None):
    s, da = q.shape
    dkv = k.shape[1]
    nq, nb = da // HEAD_DIM, s // WINDOW
    scale = HEAD_DIM ** -0.5

    def body(q_ref, kp_ref, kc_ref, vp_ref, vc_ref, cc_ref, sc_ref, cp_ref, sp_ref, sink_ref, o_ref, lse_ref):
        n = pl.program_id(0)
        cc, sc, cp, sp = cc_ref[...], sc_ref[...], cp_ref[...], sp_ref[...]
        qr = _rope(q_ref[...], _lane_tile(cc, da), _lane_tile(sc, da)).astype(BF16)
        kk = jnp.concatenate([_rope(kp_ref[...], _lane_tile(cp, dkv), _lane_tile(sp, dkv)),
                              _rope(kc_ref[...], _lane_tile(cc, dkv), _lane_tile(sc, dkv))], axis=0).astype(BF16)
        vv = jnp.concatenate([vp_ref[...], vc_ref[...]], axis=0).astype(BF16)
        valid = _band_mask(n)
        for hk in range(nq // Q_PER_KV):
            ks = slice(hk * HEAD_DIM, (hk + 1) * HEAD_DIM)
            sco = lax.dot_general(_stack_heads(qr, hk), kk[:, ks], _DOT_DIMS["nt"], preferred_element_type=F32) * scale
            sco = jnp.where(valid, sco, -1e30)
            sink = _stack_sinks(sink_ref, hk)
            mx = jnp.maximum(jnp.max(sco, axis=1, keepdims=True), sink)
            p = jnp.exp(sco - mx)
            den = jnp.sum(p, axis=1, keepdims=True) + jnp.exp(sink - mx)
            o8 = jnp.dot((p / den).astype(BF16), vv[:, ks], preferred_element_type=F32)
            lse8 = mx + jnp.log(den)
            for g in range(Q_PER_KV):
                hq, rows = hk * Q_PER_KV + g, slice(g * WINDOW, (g + 1) * WINDOW)
                o_ref[:, hq * HEAD_DIM:(hq + 1) * HEAD_DIM] = o8[rows]
                lse_ref[:, hq:hq + 1] = lse8[rows]

    sp_ = _attn_specs(da, dkv, nb)
    in_specs = [sp_["q"], sp_["kv_prev"], sp_["kv_cur"], sp_["kv_prev"], sp_["kv_cur"],
                sp_["tab_cur"], sp_["tab_cur"], sp_["tab_prev"], sp_["tab_prev"], pl.BlockSpec((1, nq), lambda n: (0, 0))]
    return _pcall(name, body, (nb,), in_specs, [q, k, k, v, v, cos, sin, cos, sin, sinks],
                  [sp_["q"], pl.BlockSpec((WINDOW, nq), lambda n: (n, 0))], [SDS((s, da), F32), SDS((s, nq), F32)],
                  semantics=("arbitrary",), plan=plan)


def _attn_bwd(name, q, k, v, cos, sin, sinks, out, lse, dout, plan=None):
    s, da = q.shape
    dkv = k.shape[1]
    nq, nb = da // HEAD_DIM, s // WINDOW
    scale = HEAD_DIM ** -0.5

    def body(q_ref, kp_ref, kc_ref, vp_ref, vc_ref, cc_ref, sc_ref, cp_ref, sp_ref, sink_ref, o_ref, lse_ref,
             do_ref, dq_ref, dk_ref, dv_ref, dsink_ref, dk_carry, dv_carry):
        n = pl.program_id(0)
        cp, sp = _lane_tile(cp_ref[...], dkv), _lane_tile(sp_ref[...], dkv)

        @pl.when(n == 0)
        def _():
            dk_carry[...] = jnp.zeros_like(dk_carry)
            dv_carry[...] = jnp.zeros_like(dv_carry)
            dsink_ref[...] = jnp.zeros_like(dsink_ref)

        @pl.when(n < nb)
        def _():
            cc, sc = cc_ref[...], sc_ref[...]
            qr = _rope(q_ref[...], _lane_tile(cc, da), _lane_tile(sc, da)).astype(BF16)
            kk = jnp.concatenate([_rope(kp_ref[...], cp, sp),
                                  _rope(kc_ref[...], _lane_tile(cc, dkv), _lane_tile(sc, dkv))], axis=0).astype(BF16)
            vv = jnp.concatenate([vp_ref[...], vc_ref[...]], axis=0).astype(BF16)
            valid = _band_mask(n)
            do_all, o_all = do_ref[...], o_ref[...]
            for hk in range(nq // Q_PER_KV):
                ks = slice(hk * HEAD_DIM, (hk + 1) * HEAD_DIM)
                q8, lse8 = _stack_heads(qr, hk), _stack_cols(lse_ref, hk)
                sco = lax.dot_general(q8, kk[:, ks], _DOT_DIMS["nt"], preferred_element_type=F32) * scale
                probs = jnp.where(valid, jnp.exp(sco - lse8), 0.0)
                do8 = _stack_heads(do_all, hk)
                delta = jnp.sum(do8 * _stack_heads(o_all, hk), axis=1, keepdims=True)
                do8 = do8.astype(BF16)
                dp = lax.dot_general(do8, vv[:, ks], _DOT_DIMS["nt"], preferred_element_type=F32)
                ds = (probs * (dp - delta) * scale).astype(BF16)
                dq8 = jnp.dot(ds, kk[:, ks], preferred_element_type=F32)
                dk_h = lax.dot_general(ds, q8, _DOT_DIMS["tn"], preferred_element_type=F32)
                dv_h = lax.dot_general(probs.astype(BF16), do8, _DOT_DIMS["tn"], preferred_element_type=F32)
                dsink8 = -jnp.exp(_stack_sinks(sink_ref, hk) - lse8) * delta
                for g in range(Q_PER_KV):
                    hq, rows = hk * Q_PER_KV + g, slice(g * WINDOW, (g + 1) * WINDOW)
                    dq_ref[:, hq * HEAD_DIM:(hq + 1) * HEAD_DIM] = dq8[rows]
                    dsink_ref[:, hq:hq + 1] += dsink8[rows].reshape(WINDOW // SUBLANES, SUBLANES, 1).sum(axis=0)
                dk_ref[:, ks] = dk_carry[:, ks] + dk_h[:WINDOW]
                dv_ref[:, ks] = dv_carry[:, ks] + dv_h[:WINDOW]
                dk_carry[:, ks] = dk_h[WINDOW:]
                dv_carry[:, ks] = dv_h[WINDOW:]
            dq_ref[...] = _rope_t(dq_ref[...], _lane_tile(cc, da), _lane_tile(sc, da))
            dk_ref[...] = _rope_t(dk_ref[...], cp, sp)

        @pl.when(n == nb)
        def _():
            dk_ref[...] = _rope_t(dk_carry[...], cp, sp)
            dv_ref[...] = dv_carry[...]

    sp_ = _attn_specs(da, dkv, nb)
    last_prev = lambda n: (jnp.maximum(n - 1, 0), 0)
    tab_prev = pl.BlockSpec((WINDOW, LANES), last_prev)
    kv_out = pl.BlockSpec((WINDOW, dkv), last_prev)
    lse_spec = pl.BlockSpec((WINDOW, nq), lambda n: (jnp.minimum(n, nb - 1), 0))
    in_specs = [sp_["q"], sp_["kv_prev"], sp_["kv_cur"], sp_["kv_prev"], sp_["kv_cur"],
                sp_["tab_cur"], sp_["tab_cur"], tab_prev, tab_prev,
                pl.BlockSpec((1, nq), lambda n: (0, 0)), sp_["q"], lse_spec, sp_["q"]]
    res = _pcall(name, body, (nb + 1,), in_specs, [q, k, k, v, v, cos, sin, cos, sin, sinks, out, lse, dout],
                 [sp_["q"], kv_out, kv_out, pl.BlockSpec((SUBLANES, nq), lambda n: (0, 0))],
                 [SDS((s, da), F32), SDS((s, dkv), F32), SDS((s, dkv), F32), SDS((SUBLANES, nq), F32)],
                 scratch=[pltpu.VMEM((WINDOW, dkv), F32), pltpu.VMEM((WINDOW, dkv), F32)],
                 semantics=("arbitrary",), plan=plan)
    (dq, dk, dv, dsink), rest = res if plan is not None else (res, None)
    return (dq, dk, dv, dsink.sum(0)), rest


def _ssm_operators(lam_re, lam_im, log_step, b_re, b_im, c_re, c_im, d_skip):
    g, p = lam_re.shape
    h = b_re.shape[-1]
    l = SSM_CHUNK
    step = jnp.exp(log_step)[:, None]
    mag = jnp.exp(lam_re * step)
    ar, ai = mag * jnp.cos(lam_im * step), mag * jnp.sin(lam_im * step)
    den = lam_re * lam_re + lam_im * lam_im
    cr = ((ar - 1.0) * lam_re + ai * lam_im) / den
    ci = (ai * lam_re - (ar - 1.0) * lam_im) / den
    bbr = cr[..., None] * b_re - ci[..., None] * b_im
    bbi = cr[..., None] * b_im + ci[..., None] * b_re
    powers = jnp.arange(l + 1, dtype=F32)[None, :, None]
    pw_mag = jnp.exp((lam_re * step)[:, None, :] * powers)
    pw_ang = (lam_im * step)[:, None, :] * powers
    pwr, pwi = pw_mag * jnp.cos(pw_ang), pw_mag * jnp.sin(pw_ang)
    cpr = c_re[:, None] * pwr[:, :, None, :] - c_im[:, None] * pwi[:, :, None, :]
    cpi = c_re[:, None] * pwi[:, :, None, :] + c_im[:, None] * pwr[:, :, None, :]
    kern = (jnp.einsum("gtop,gpi->gtoi", cpr[:, :l], bbr, precision=lax.Precision.HIGHEST)
            - jnp.einsum("gtop,gpi->gtoi", cpi[:, :l], bbi, precision=lax.Precision.HIGHEST))
    kern = kern.at[:, 0].add(d_skip.reshape(g, h)[:, :, None] * jnp.eye(h, dtype=F32))
    lag = jnp.arange(l)
    place = (lag[None, None, :] - lag[None, :, None] == lag[:, None, None]).astype(F32)
    tm = jnp.einsum("gtoh,tji->gjhio", kern, place, precision=lax.Precision.HIGHEST).reshape(g, l * h, l * h)
    rev_r, rev_i = pwr[:, l - 1::-1][:, :l], pwi[:, l - 1::-1][:, :l]
    er = rev_r[:, :, None, :] * bbr.transpose(0, 2, 1)[:, None] - rev_i[:, :, None, :] * bbi.transpose(0, 2, 1)[:, None]
    ei = rev_r[:, :, None, :] * bbi.transpose(0, 2, 1)[:, None] + rev_i[:, :, None, :] * bbr.transpose(0, 2, 1)[:, None]
    em = jnp.concatenate([er, ei], axis=-1).reshape(g, l * h, 2 * p)
    fr = cpr[:, 1:].transpose(0, 3, 1, 2).reshape(g, p, l * h)
    fi = -cpi[:, 1:].transpose(0, 3, 1, 2).reshape(g, p, l * h)
    fm = jnp.concatenate([fr, fi], axis=1)
    return tm, em, fm, pwr[:, l], pwi[:, l]


def _decay_lanes(alr, ali):
    return jnp.concatenate([alr, alr], axis=1), jnp.concatenate([-ali, ali], axis=1)


def _column_blocks(s, ds):
    wide = 2 if (ds // LANES) % 2 == 0 else 1
    return wide, pl.BlockSpec((s, wide * LANES), lambda i: (0, i // wide))


def _my_columns(wide, block_ref, stage_ref, store):
    part = pl.program_id(0) % wide
    for p in range(wide):
        @pl.when(part == p)
        def _():
            cols = slice(p * LANES, (p + 1) * LANES)
            if store:
                block_ref[:, cols] = stage_ref[...]
            else:
                stage_ref[...] = block_ref[:, cols]


def _ssm_fwd(name, u, tm, em, fm, acat, bcat, plan=None):
    s, ds = u.shape
    g, lh, p2 = em.shape
    gb, h = SSM_GROUPS_PER_STEP, lh // SSM_CHUNK
    assert gb * h == LANES and g * h == ds and s % SSM_CHUNK == 0
    nc, half = s // SSM_CHUNK, p2 // 2

    def body(u_ref, tm_ref, em_ref, fm_ref, a_ref, b_ref, y_ref, xp_ref, uc_ref, yc_ref, st_ref, col_ref):
        _my_columns(wide, u_ref, col_ref, store=False)
        _to_chunks(col_ref, uc_ref, nc, h)
        for i in range(gb):
            st_ref[pl.ds(i, nc, stride=gb), :] = jnp.dot(uc_ref[i], em_ref[i], precision=SSM_PRECISION,
                                                         preferred_element_type=F32)
        av, bv = a_ref[...], b_ref[...]

        def step(c, carry):
            x, xs = carry
            rows = pl.ds(pl.multiple_of(c * gb, gb), gb)
            loc = st_ref[rows, :]
            st_ref[rows, :] = x
            return av * x + bv * xs + loc, av * xs - bv * x + pltpu.roll(loc, half, 1)
        zero = jnp.zeros((gb, p2), F32)
        lax.fori_loop(0, nc, step, (zero, zero), unroll=4)
        for i in range(gb):
            xp = st_ref[pl.ds(i, nc, stride=gb), :]
            xp_ref[i] = xp
            yc_ref[i] = (jnp.dot(uc_ref[i], tm_ref[i], precision=SSM_PRECISION, preferred_element_type=F32)
                         + jnp.dot(xp, fm_ref[i], precision=SSM_PRECISION, preferred_element_type=F32))
        _from_chunks(yc_ref, col_ref, nc, h)
        _my_columns(wide, y_ref, col_ref, store=True)

    blk = lambda r, c: pl.BlockSpec((gb, r, c), lambda i: (i, 0, 0))
    vec = pl.BlockSpec((gb, p2), lambda i: (i, 0))
    wide, col = _column_blocks(s, ds)
    return _pcall(name, body, (g // gb,), [col, blk(lh, lh), blk(lh, p2), blk(p2, lh), vec, vec],
                  [u, tm, em, fm, acat, bcat], [col, blk(nc, p2), blk(nc, lh)],
                  [SDS((s, ds), F32), SDS((g, nc, p2), F32), SDS((g, nc, lh), F32)],
                  scratch=[pltpu.VMEM((gb, nc, lh), F32), pltpu.VMEM((nc * gb, p2), F32), pltpu.VMEM((s, LANES), F32)],
                  semantics=("arbitrary",), plan=plan)


def _ssm_bwd(name, u_chunks, dy, xprev, tm, em, fm, acat, bcat, plan=None):
    s, ds = dy.shape
    g, lh, p2 = em.shape
    gb, h = SSM_GROUPS_PER_STEP, lh // SSM_CHUNK
    nc, half = s // SSM_CHUNK, p2 // 2

    def body(uc_ref, dy_ref, xp_ref, tm_ref, em_ref, fm_ref, a_ref, b_ref,
             du_ref, dtm_ref, dem_ref, dfm_ref, r1_ref, r2_ref, dyc_ref, duc_ref, gs_ref, xs_ref, col_ref):
        _my_columns(wide, dy_ref, col_ref, store=False)
        _to_chunks(col_ref, dyc_ref, nc, h)
        for i in range(gb):
            gs_ref[pl.ds(i, nc, stride=gb), :] = lax.dot_general(
                dyc_ref[i], fm_ref[i], _DOT_DIMS["nt"], precision=SSM_PRECISION, preferred_element_type=F32)
            xs_ref[pl.ds(i, nc, stride=gb), :] = xp_ref[i]
        av, bv = a_ref[...], b_ref[...]

        def step(t, carry):
            grad, gsw, r1, r2 = carry
            c = nc - 1 - t
            rows = pl.ds(pl.multiple_of(c * gb, gb), gb)
            dxp, xp = gs_ref[rows, :], xs_ref[rows, :]
            gs_ref[rows, :] = grad
            r1 = r1 + grad * xp
            r2 = r2 + grad * pltpu.roll(xp, half, 1)
            return dxp + av * grad - bv * gsw, pltpu.roll(dxp, half, 1) + av * gsw + bv * grad, r1, r2
        zero = jnp.zeros((gb, p2), F32)
        _, _, r1, r2 = lax.fori_loop(0, nc, step, (zero, zero, zero, zero), unroll=4)
        r1_ref[...], r2_ref[...] = r1, r2
        for i in range(gb):
            dxl = gs_ref[pl.ds(i, nc, stride=gb), :]
            duc_ref[i] = (lax.dot_general(dyc_ref[i], tm_ref[i], _DOT_DIMS["nt"], precision=SSM_PRECISION,
                                          preferred_element_type=F32)
                          + lax.dot_general(dxl, em_ref[i], _DOT_DIMS["nt"], precision=SSM_PRECISION,
                                            preferred_element_type=F32))
            dtm_ref[i] = lax.dot_general(uc_ref[i], dyc_ref[i], _DOT_DIMS["tn"], precision=SSM_PRECISION,
                                         preferred_element_type=F32)
            dfm_ref[i] = lax.dot_general(xp_ref[i], dyc_ref[i], _DOT_DIMS["tn"], precision=SSM_PRECISION,
                                         preferred_element_type=F32)
            dem_ref[i] = lax.dot_general(uc_ref[i], dxl, _DOT_DIMS["tn"], precision=SSM_PRECISION,
                                         preferred_element_type=F32)
        _from_chunks(duc_ref, col_ref, nc, h)
        _my_columns(wide, du_ref, col_ref, store=True)

    blk = lambda r, c: pl.BlockSpec((gb, r, c), lambda i: (i, 0, 0))
    vec = pl.BlockSpec((gb, p2), lambda i: (i, 0))
    wide, col = _column_blocks(s, ds)
    chunked = pltpu.VMEM((gb, nc, lh), F32)
    res = _pcall(name, body, (g // gb,),
                 [blk(nc, lh), col, blk(nc, p2), blk(lh, lh), blk(lh, p2), blk(p2, lh), vec, vec],
                 [u_chunks, dy, xprev, tm, em, fm, acat, bcat],
                 [col, blk(lh, lh), blk(lh, p2), blk(p2, lh), vec, vec],
                 [SDS((s, ds), F32), SDS((g, lh, lh), F32), SDS((g, lh, p2), F32), SDS((g, p2, lh), F32),
                  SDS((g, p2), F32), SDS((g, p2), F32)],
                 scratch=[chunked, chunked, pltpu.VMEM((nc * gb, p2), F32), pltpu.VMEM((nc * gb, p2), F32),
                          pltpu.VMEM((s, LANES), F32)],
                 semantics=("arbitrary",), plan=plan)
    return res if plan is not None else (res, None)


def _to_chunks(src_ref, dst_ref, nc, h):
    per = LANES // h
    rows = min(nc, 64)
    grp = lax.broadcasted_iota(jnp.int32, (rows, LANES), 1) // h
    for r0 in range(0, nc, rows):
        for i in range(SSM_CHUNK):
            part, lo = divmod(i * h, LANES)
            step_rows = src_ref[pl.ds(r0 * SSM_CHUNK + i, rows, stride=SSM_CHUNK), :]
            for g in range(per):
                shift = (lo - g * h) % LANES
                piece = pltpu.roll(step_rows, shift, 1) if shift else step_rows
                out = dst_ref.at[g, r0:r0 + rows, part * LANES:(part + 1) * LANES]
                out[...] = piece if lo == 0 else jnp.where(grp == lo // h, piece, out[...])


def _from_chunks(src_ref, dst_ref, nc, h):
    per = LANES // h
    grp = lax.broadcasted_iota(jnp.int32, (nc, LANES), 1) // h
    for i in range(SSM_CHUNK):
        part, lo = divmod(i * h, LANES)
        row = None
        for g in range(per):
            piece = src_ref[g, :, part * LANES:(part + 1) * LANES]
            if (g * h - lo) % LANES:
                piece = pltpu.roll(piece, (g * h - lo) % LANES, 1)
            row = piece if row is None else jnp.where(grp == g, piece, row)
        dst_ref[pl.ds(i, nc, stride=SSM_CHUNK), :] = row


_SMALL = ("b_ada", "norm1_g", "sinks", "ssm_lam_re", "ssm_lam_im", "ssm_log_step", "ssm_b_re", "ssm_b_im",
          "ssm_c_re", "ssm_c_im", "ssm_d", "b_glu", "attn_out_g", "ssm_out_g", "norm2_g", "final_g")
_WEIGHTS = ("w_ada", "b_ada", "norm1_g", "w_in", "sinks", "ssm_lam_re", "ssm_lam_im", "ssm_log_step", "ssm_b_re",
            "ssm_b_im", "ssm_c_re", "ssm_c_im", "ssm_d", "w_glu", "b_glu", "attn_out_g", "ssm_out_g", "w_out",
            "norm2_g", "w_ff1", "w_ff2", "final_g")
_PACK_ALIGN = 128 * LANES


def _pack(parts):
    flat = jnp.concatenate([p.reshape(-1).astype(F32) for p in parts])
    pad = (-flat.shape[0]) % _PACK_ALIGN
    return jnp.pad(flat, (0, pad)).reshape(-1, LANES)


def kernel(x, c, w_ada, b_ada, norm1_g, w_in, sinks, ssm_lam_re, ssm_lam_im, ssm_log_step, ssm_b_re, ssm_b_im, ssm_c_re, ssm_c_im, ssm_d, w_glu, b_glu, attn_out_g, ssm_out_g, w_out, norm2_g, w_ff1, w_ff2, final_g, loss_target, m_w_ada, m_b_ada, m_norm1_g, m_w_in, m_sinks, m_ssm_lam_re, m_ssm_lam_im, m_ssm_log_step, m_ssm_b_re, m_ssm_b_im, m_ssm_c_re, m_ssm_c_im, m_ssm_d, m_w_glu, m_b_glu, m_attn_out_g, m_ssm_out_g, m_w_out, m_norm2_g, m_w_ff1, m_w_ff2, m_final_g, v_w_ada, v_b_ada, v_norm1_g, v_w_in, v_sinks, v_ssm_lam_re, v_ssm_lam_im, v_ssm_log_step, v_ssm_b_re, v_ssm_b_im, v_ssm_c_re, v_ssm_c_im, v_ssm_d, v_w_glu, v_b_glu, v_attn_out_g, v_ssm_out_g, v_w_out, v_norm2_g, v_w_ff1, v_w_ff2, v_final_g):
    args = dict(locals())
    weights = {n: args[n] for n in _WEIGHTS}
    mom = {n: args["m_" + n] for n in _WEIGHTS}
    var = {n: args["v_" + n] for n in _WEIGHTS}
    me = 4 * lax.axis_index("x") + 2 * lax.axis_index("y") + lax.axis_index("c")

    _, s, d = x.shape
    xs, tgt = x[0], loss_target[0]
    d_ssm = ssm_d.shape[-1]
    d_attn = d - d_ssm
    nq = d_attn // HEAD_DIM
    d_kv = (nq // Q_PER_KV) * HEAD_DIM
    p_state = ssm_b_re.shape[2]

    c_all, g_in = _run_plan("gather_c_w_in", _Gather([c, w_in[0].T.astype(BF16)]))
    c_all = c_all.reshape(N_DEV, d)
    w_in_t = g_in.reshape(-1, d)

    n_loc = w_ada.shape[-1]
    b_loc = lax.dynamic_slice_in_dim(b_ada, me * n_loc, n_loc, axis=1)
    silu = lambda t: t * _sigmoid(t)
    mod_part = _matmul("ada_mod", c_all, w_ada[0], "nn", [F32], a_pro=silu, vecs=[b_loc], exact=True,
                       epilogue=lambda acc, e, v: (acc + v[0],), tn=512, tk=d)
    mod_all = _run_plan("gather_mod", _Gather([mod_part]))[0]
    mod = lax.dynamic_index_in_dim(mod_all, me, axis=1, keepdims=False).reshape(N_MOD, 1, d)
    shift1, scale1, gate1, shift2, scale2, gate2 = [mod[i] for i in range(N_MOD)]

    h1 = _norm_mod_fwd("norm1", xs, norm1_g, scale1, shift1)
    q = _matmul("proj_q", h1, w_in_t, "nt", [F32], b_rows=(0, d_attn))
    kv = _matmul("proj_kv", h1, w_in_t, "nt", [F32], b_rows=(d_attn, 2 * d_kv))
    u = _matmul("proj_u", h1, w_in_t, "nt", [F32], b_rows=(d_attn + 2 * d_kv, d_ssm))
    k, v = kv[:, :d_kv], kv[:, d_kv:]

    half = HEAD_DIM // 2
    inv_freq = ROPE_THETA ** (-jnp.arange(half, dtype=F32) / half)
    ang = jnp.arange(s, dtype=F32)[:, None] * inv_freq[None, :]
    cos_t, sin_t = jnp.tile(jnp.cos(ang), (1, 4)), jnp.tile(jnp.sin(ang), (1, 4))
    (attn, lse), (g_glu, g_out) = _attn_fwd("attn_fwd", q, k, v, cos_t, sin_t, sinks,
                                            plan=_Gather([w_glu[0].astype(BF16), w_out[0].astype(BF16)]))
    w_glu_f = g_glu.reshape(d_ssm, d_ssm)
    w_out_f = g_out.reshape(d, d)

    ssm_params = (ssm_lam_re[0], ssm_lam_im[0], ssm_log_step[0], ssm_b_re[0], ssm_b_im[0], ssm_c_re[0],
                  ssm_c_im[0], ssm_d[0])
    (tm_op, em_op, fm_op, alr, ali), ssm_vjp = jax.vjp(_ssm_operators, *ssm_params)
    acat, bcat = _decay_lanes(alr, ali)
    (y_ssm, x_prev, u_chunks), (g_ff1,) = _ssm_fwd("ssm_fwd", u, tm_op, em_op, fm_op, acat, bcat,
                                                   plan=_Gather([w_ff1[0].astype(BF16)]))
    yg = _gelu_fwd("gelu", y_ssm)
    ssm_out, z_glu = _matmul(
        "glu", yg, w_glu_f, "nn", [F32, F32], extras=[y_ssm], vecs=[b_glu],
        epilogue=lambda acc, e, v: (_gelu(e[0]) * _sigmoid(acc + v[0]), acc + v[0]))
    mixed = _group_norm_fwd("group_norm", attn, ssm_out, attn_out_g, ssm_out_g)
    x2, mo = _matmul("out_proj", mixed, w_out_f, "nn", [F32, BF16], extras=[xs], vecs=[gate1],
                     epilogue=lambda acc, e, v: (e[0] + v[0] * acc, acc))

    h2 = _norm_mod_fwd("norm2", x2, norm2_g, scale2, shift2)
    (a_ff, f_ff), (g_ff2,) = _matmul("ff1", h2, g_ff1, "nn", [BF16, BF16], b_blocked=True,
                                     epilogue=lambda acc, e, v: (acc, jnp.square(jnp.maximum(acc, 0.0))),
                                     plan=_Gather([w_ff2[0].astype(BF16)]))
    w_ff2_f = g_ff2.reshape(-1, d)
    x3, ff = _matmul("ff2", f_ff, w_ff2_f, "nn", [F32, BF16], extras=[x2], vecs=[gate2],
                     epilogue=lambda acc, e, v: (e[0] + v[0] * acc, acc))

    dx3, dff, loss_local, d_final_g, d_gate2 = _final_loss("final_loss", x3, tgt, final_g.reshape(1, d), ff, gate2)
    loss = lax.psum(loss_local, MESH_AXES)

    dw_ff2 = _matmul("ff2_dw", f_ff, dff, "tn", [BF16]).reshape(N_DEV, -1, d)
    da_ff, (p_ff2,) = _matmul("ff2_dx", dff, w_ff2_f, "nt", [BF16], extras=[a_ff],
                              epilogue=lambda acc, e, v: (acc * (2.0 * jnp.maximum(e[0].astype(F32), 0.0)),),
                              plan=_PairSwap([dw_ff2]))
    s_ff2 = _pair_add("pair_add_ff2", dw_ff2, p_ff2)
    dw_ff1, (r_ff2_a,) = _matmul("ff1_dw", h2, da_ff, "tn", [BF16], out_blocked=N_DEV,
                                 plan=_ChipExchange([s_ff2], (CHIP_X, CHIP_Y)))
    dh2, (r_ff2_b,) = _matmul("ff1_dx", da_ff, g_ff1, "nt", [F32], b_blocked=True,
                              plan=_ChipExchange([s_ff2], (CHIP_DIAGONAL,)))
    (dx2, dmo, d_scale2, d_shift2, d_norm2_g, d_gate1), (p_ff1,) = _norm_mod_bwd(
        "norm2_bwd", x2, dh2, dx3, norm2_g, scale2, gated=(mo, gate1), plan=_PairSwap([dw_ff1]))
    s_ff1 = _pair_add("pair_add_ff1", dw_ff1, p_ff1)

    dw_out = _matmul("out_dw", mixed, dmo, "tn", [BF16]).reshape(N_DEV, -1, d)
    dmixed, (r_out_a,) = _matmul("out_dx", dmo, w_out_f, "nt", [F32], plan=_Exchange([dw_out], (1, 4, 2)))
    (dattn, dssm_out, d_attn_g, d_ssm_g), (r_out_b,) = _group_norm_bwd(
        "group_norm_bwd", attn, ssm_out, dmixed, attn_out_g, ssm_out_g, plan=_Exchange([dw_out], (6,)))

    dz, dyg_direct, d_b_glu = _glu_bwd("glu_bwd", dssm_out, y_ssm, z_glu)
    dw_glu = _matmul("glu_dw", yg, dz, "tn", [BF16]).reshape(N_DEV, -1, d_ssm)
    dy_ssm = _matmul("glu_dx", dz, w_glu_f, "nt", [F32], extras=[dyg_direct, y_ssm],
                     epilogue=lambda acc, e, v: ((acc + e[0]) * _gelu_grad(e[1]),))
    (du, d_tm, d_em, d_fm, r1, r2), (r_ff1_a, r_glu) = _ssm_bwd(
        "ssm_bwd", u_chunks, dy_ssm, x_prev, tm_op, em_op, fm_op, acat, bcat,
        plan=_Plans([_ChipExchange([s_ff1], (CHIP_X, CHIP_Y)), _Exchange([dw_glu], RELATIONS_ALL)]))
    d_alr = r1[:, :p_state] + r1[:, p_state:]
    d_ali = r2[:, p_state:] - r2[:, :p_state]
    d_ssm_params = ssm_vjp((d_tm, d_em, d_fm, d_alr, d_ali))

    (dq, dk, dv, d_sinks), (r_ff1_b,) = _attn_bwd("attn_bwd", q, k, v, cos_t, sin_t, sinks, attn, lse, dattn,
                                                  plan=_ChipExchange([s_ff1], (CHIP_DIAGONAL,)))
    dproj = jnp.concatenate([dq, dk, dv, du], axis=1).astype(BF16)
    dw_in_t, (r_out_c,) = _matmul("in_dw", dproj, h1, "tn", [BF16], plan=_Exchange([dw_out], (5, 3, 7)))
    dw_in_t = dw_in_t.reshape(N_DEV, -1, d)
    dh1, (r_in_a,) = _matmul("in_dx", dproj, w_in_t, "nn", [F32], plan=_Exchange([dw_in_t], RELATIONS_SAME_CORE))
    (grad_x, d_scale1, d_shift1, d_norm1_g), _ = _norm_mod_bwd("norm1_bwd", xs, dh1, dx2, norm1_g, scale1)

    d_mod = jnp.concatenate([d_shift1, d_scale1, d_gate1, d_shift2, d_scale2, d_gate2])
    small_g = dict(zip(("ssm_lam_re", "ssm_lam_im", "ssm_log_step", "ssm_b_re", "ssm_b_im", "ssm_c_re", "ssm_c_im",
                        "ssm_d"), d_ssm_params, strict=True))
    small_g.update(b_ada=d_mod, norm1_g=d_norm1_g, sinks=d_sinks, b_glu=d_b_glu, attn_out_g=d_attn_g,
                   ssm_out_g=d_ssm_g, norm2_g=d_norm2_g, final_g=d_final_g)
    small_parts, r_in_b = _run_plan("gather_small_grads", _Plans([_Gather([_pack([small_g[n] for n in _SMALL])]),
                                                                  _Exchange([dw_in_t], RELATIONS_OTHER_CORE)]))
    small = _adam_shard("adam_small", [small_parts], _pack([weights[n] for n in _SMALL]),
                        _pack([mom[n] for n in _SMALL]), _pack([var[n] for n in _SMALL]))
    out = {}
    off = 0
    for n in _SMALL:
        size = weights[n].size
        out[n] = [t.reshape(-1)[off:off + size].reshape(weights[n].shape) for t in small]
        off += size

    dmod_all = small_parts.reshape(N_DEV, -1)[:, :N_MOD * d]
    dmod_loc = lax.dynamic_slice_in_dim(dmod_all, me * n_loc, n_loc, axis=1)
    c_act_t = silu(c_all).T
    out["w_ada"] = [t[None] for t in _ada_update("adam_w_ada", c_act_t, dmod_loc, w_ada[0], m_w_ada[0], v_w_ada[0])]

    mine = lambda blocks: lax.dynamic_index_in_dim(blocks, me, axis=0, keepdims=False)
    in_parts = [mine(dw_in_t).T] + [r.transpose(0, 2, 1) for r in (r_in_a, r_in_b)]
    my_chip = 2 * lax.axis_index("x") + lax.axis_index("y")
    chip_sum = lambda sums: lax.dynamic_index_in_dim(sums, my_chip, axis=0, keepdims=False)
    received = dict(w_in=in_parts, w_glu=[mine(dw_glu), r_glu], w_out=[mine(dw_out), r_out_a, r_out_b, r_out_c],
                    w_ff1=[chip_sum(s_ff1), r_ff1_a, r_ff1_b], w_ff2=[chip_sum(s_ff2), r_ff2_a, r_ff2_b])
    for n, parts in received.items():
        out[n] = [t[None] for t in _adam_shard("adam_" + n, parts, weights[n][0], mom[n][0], var[n][0])]

    return (loss, grad_x[None], *[out[n][0] for n in _WEIGHTS], *[out[n][1] for n in _WEIGHTS],
            *[out[n][2] for n in _WEIGHTS], *[out[n][3] for n in _WEIGHTS])
```

```python
import math

import jax
import jax.numpy as jnp
from jax import lax
from jax.experimental import pallas as pl
from jax.experimental.pallas import tpu as pltpu

F32, BF16 = jnp.float32, jnp.bfloat16
SDS = jax.ShapeDtypeStruct
MESH_AXES = ("x", "y", "c")
N_DEV = 8
VMEM_LIMIT_BYTES = 56 * 1024 * 1024
MATMUL_VMEM_BUDGET = 44 * 1024 * 1024
SUBLANES, LANES = 8, 128

HEAD_DIM = 64
Q_PER_KV = 8
WINDOW = 128
ROPE_THETA = 10000.0
EPS = 1e-6
N_MOD = 6
SSM_CHUNK = 16
SSM_GROUPS_PER_STEP = 8

ADAM_LR, ADAM_B1, ADAM_B2, ADAM_EPS, ADAM_WD, ADAM_STEP = 0.001, 0.9, 0.999, 1e-08, 0.01, 10
HIGHEST = lax.Precision.HIGHEST
SSM_PRECISION = lax.Precision.HIGH

RELATIONS_ALL = (1, 4, 2, 6, 5, 3, 7)
RELATIONS_SAME_CORE = (1, 4, 2, 6)
RELATIONS_OTHER_CORE = (5, 3, 7)
PLAN_MIDDLE = 0.65


def _cparams(sem):
    return pltpu.CompilerParams(dimension_semantics=sem, vmem_limit_bytes=VMEM_LIMIT_BYTES)


def _block_index(p):
    return 4 * p[0] + 2 * p[1] + p[2]


def _me():
    return lax.axis_index("x"), lax.axis_index("y"), lax.axis_index("c")


class _Plan:
    def middle(self, ins, outs, send, recv, local):
        pass


class _Gather(_Plan):
    TO_SIBLING, TO_X, TO_Y, RELAY, PASS_X, PASS_Y, PASS_DIAGONAL = range(7)

    def __init__(self, arrs):
        self.ins = list(arrs)
        self.out_shapes = [SDS((N_DEV,) + a.shape, a.dtype) for a in arrs]
        self.n_rdma, self.n_local = 7 * len(arrs), len(arrs)
        self.rdma_base = self.local_base = 0

    def _copy(self, ins, outs, send, recv, a, k, block, to, from_input=False):
        dst = outs[a].at[_block_index(block)]
        sem = self.rdma_base + a * 7 + k
        return pltpu.make_async_remote_copy(
            src_ref=ins[a] if from_input else dst, dst_ref=dst, send_sem=send.at[sem], recv_sem=recv.at[sem],
            device_id=to, device_id_type=pl.DeviceIdType.MESH)

    @staticmethod
    def _places():
        x, y, c = _me()
        return (x, y, c), (x, y, 1 - c), (1 - x, y, c), (x, 1 - y, c), (1 - x, 1 - y, c)

    def _first(self, ins, outs, send, recv, a):
        me, sibling, x_nbr, y_nbr, _ = self._places()
        return [self._copy(ins, outs, send, recv, a, k, me, to, True)
                for k, to in ((self.TO_SIBLING, sibling), (self.TO_X, x_nbr), (self.TO_Y, y_nbr))]

    def _mine(self, ins, outs, local, a):
        return pltpu.make_async_copy(ins[a], outs[a].at[_block_index(_me())], local.at[self.local_base + a])

    def start(self, ins, outs, send, recv, local):
        for a in range(len(ins)):
            self._mine(ins, outs, local, a).start()
            for cp in self._first(ins, outs, send, recv, a):
                cp.start()

    def middle(self, ins, outs, send, recv, local):
        me, sibling, x_nbr, y_nbr, _ = self._places()
        core = me[2]
        for a in range(len(ins)):
            self._copy(ins, outs, send, recv, a, self.TO_X, x_nbr, me).wait_recv()
            self._copy(ins, outs, send, recv, a, self.TO_Y, y_nbr, me).wait_recv()

            @pl.when(core == 0)
            def _():
                self._copy(ins, outs, send, recv, a, self.RELAY, x_nbr, y_nbr).start()

            @pl.when(core == 1)
            def _():
                self._copy(ins, outs, send, recv, a, self.RELAY, y_nbr, x_nbr).start()

            self._copy(ins, outs, send, recv, a, self.PASS_X, x_nbr, sibling).start()
            self._copy(ins, outs, send, recv, a, self.PASS_Y, y_nbr, sibling).start()

    def finish(self, ins, outs, send, recv, local):
        me, sibling, x_nbr, y_nbr, diagonal = self._places()
        other = lambda p: (p[0], p[1], 1 - p[2])
        for a in range(len(ins)):
            self._copy(ins, outs, send, recv, a, self.RELAY, diagonal, me).wait_recv()
            self._copy(ins, outs, send, recv, a, self.PASS_DIAGONAL, diagonal, sibling).start()
        for a in range(len(ins)):
            self._copy(ins, outs, send, recv, a, self.TO_SIBLING, sibling, me).wait_recv()
            for k, src in ((self.PASS_X, x_nbr), (self.PASS_Y, y_nbr), (self.PASS_DIAGONAL, diagonal)):
                self._copy(ins, outs, send, recv, a, k, other(src), me).wait_recv()
                self._copy(ins, outs, send, recv, a, k, src, sibling).wait_send()
            for cp in self._first(ins, outs, send, recv, a):
                cp.wait_send()
            self._copy(ins, outs, send, recv, a, self.RELAY, me, me).wait_send()
            self._mine(ins, outs, local, a).wait()


class _Exchange(_Plan):
    def __init__(self, arrs, relations):
        self.ins, self.relations = list(arrs), tuple(relations)
        self.out_shapes = [SDS((len(relations),) + a.shape[1:], a.dtype) for a in arrs]
        self.n_rdma, self.n_local = len(relations) * len(arrs), 0
        self.rdma_base = self.local_base = 0

    def _copies(self, ins, outs, send, recv):
        x, y, c = _me()
        cps = []
        for a in range(len(ins)):
            for s, k in enumerate(self.relations):
                peer = ((1 - x) if (k & 4) else x, (1 - y) if (k & 2) else y, (1 - c) if (k & 1) else c)
                sem = self.rdma_base + a * len(self.relations) + s
                cps.append(pltpu.make_async_remote_copy(
                    src_ref=ins[a].at[_block_index(peer)], dst_ref=outs[a].at[s], send_sem=send.at[sem],
                    recv_sem=recv.at[sem], device_id=peer, device_id_type=pl.DeviceIdType.MESH))
        return cps

    def start(self, ins, outs, send, recv, local):
        for cp in self._copies(ins, outs, send, recv):
            cp.start()

    def finish(self, ins, outs, send, recv, local):
        for cp in self._copies(ins, outs, send, recv):
            cp.wait()


class _PairSwap(_Plan):
    def __init__(self, arrs):
        self.ins = list(arrs)
        self.out_shapes = [SDS((4,) + a.shape[1:], a.dtype) for a in arrs]
        self.n_rdma, self.n_local = 4 * len(arrs), 0
        self.rdma_base = self.local_base = 0

    def _copies(self, ins, outs, send, recv):
        x, y, c = _me()
        cps = []
        for a in range(len(ins)):
            for s in range(4):
                sem = self.rdma_base + a * 4 + s
                cps.append(pltpu.make_async_remote_copy(
                    src_ref=ins[a].at[2 * s + (1 - c)], dst_ref=outs[a].at[s], send_sem=send.at[sem],
                    recv_sem=recv.at[sem], device_id=(x, y, 1 - c), device_id_type=pl.DeviceIdType.MESH))
        return cps

    def start(self, ins, outs, send, recv, local):
        for cp in self._copies(ins, outs, send, recv):
            cp.start()

    def finish(self, ins, outs, send, recv, local):
        for cp in self._copies(ins, outs, send, recv):
            cp.wait()


CHIP_X, CHIP_Y, CHIP_DIAGONAL = (1, 0), (0, 1), (1, 1)


class _ChipExchange(_Plan):
    def __init__(self, arrs, hops):
        self.ins, self.hops = list(arrs), tuple(hops)
        self.out_shapes = [SDS((len(hops),) + a.shape[1:], a.dtype) for a in arrs]
        self.n_rdma, self.n_local = len(hops) * len(arrs), 0
        self.rdma_base = self.local_base = 0

    def _copies(self, ins, outs, send, recv):
        x, y, c = _me()
        cps = []
        for a in range(len(ins)):
            for s, (fx, fy) in enumerate(self.hops):
                px, py = (1 - x) if fx else x, (1 - y) if fy else y
                sem = self.rdma_base + a * len(self.hops) + s
                cps.append(pltpu.make_async_remote_copy(
                    src_ref=ins[a].at[2 * px + py], dst_ref=outs[a].at[s], send_sem=send.at[sem],
                    recv_sem=recv.at[sem], device_id=(px, py, c), device_id_type=pl.DeviceIdType.MESH))
        return cps

    def start(self, ins, outs, send, recv, local):
        for cp in self._copies(ins, outs, send, recv):
            cp.start()

    def finish(self, ins, outs, send, recv, local):
        for cp in self._copies(ins, outs, send, recv):
            cp.wait()


class _Plans:
    def __init__(self, plans):
        self.plans = list(plans)
        self.ins = [a for p in plans for a in p.ins]
        self.out_shapes = [s for p in plans for s in p.out_shapes]
        self.n_rdma = self.n_local = 0
        for p in plans:
            p.rdma_base, p.local_base = self.n_rdma, self.n_local
            self.n_rdma, self.n_local = self.n_rdma + p.n_rdma, self.n_local + p.n_local

    def _each(self, ins, outs):
        i = o = 0
        for p in self.plans:
            yield p, ins[i:i + len(p.ins)], outs[o:o + len(p.out_shapes)]
            i, o = i + len(p.ins), o + len(p.out_shapes)

    def start(self, ins, outs, send, recv, local):
        for p, pi, po in self._each(ins, outs):
            p.start(pi, po, send, recv, local)

    def middle(self, ins, outs, send, recv, local):
        for p, pi, po in self._each(ins, outs):
            p.middle(pi, po, send, recv, local)

    def finish(self, ins, outs, send, recv, local):
        for p, pi, po in self._each(ins, outs):
            p.finish(pi, po, send, recv, local)


def _plan_scratch(plan):
    return [pltpu.SemaphoreType.DMA((plan.n_rdma,)), pltpu.SemaphoreType.DMA((plan.n_rdma,)),
            pltpu.SemaphoreType.DMA((max(plan.n_local, 1),))]


def _run_plan(name, plan):
    n = len(plan.ins)

    def body(*refs):
        ins, outs, sems = refs[:n], refs[n:len(refs) - 3], refs[len(refs) - 3:]
        plan.start(ins, outs, *sems)
        plan.middle(ins, outs, *sems)
        plan.finish(ins, outs, *sems)

    any_spec = pl.BlockSpec(memory_space=pl.ANY)
    return pl.pallas_call(body, name=name, out_shape=list(plan.out_shapes), in_specs=[any_spec] * n,
                          out_specs=[any_spec] * len(plan.out_shapes), scratch_shapes=_plan_scratch(plan))(*plan.ins)


def _pcall(name, body, grid, in_specs, ins, out_specs, out_shape, scratch=(), semantics=None, plan=None):
    if plan is None:
        return pl.pallas_call(body, name=name, grid=grid, in_specs=list(in_specs), out_specs=list(out_specs),
                              out_shape=list(out_shape), scratch_shapes=list(scratch),
                              compiler_params=_cparams(semantics))(*ins)
    n_in, n_out, n_scr = len(ins), len(out_shape), len(scratch)
    p_in, p_out = len(plan.ins), len(plan.out_shapes)

    def with_plan(*refs):
        k_in, c_in = refs[:n_in], refs[n_in:n_in + p_in]
        refs = refs[n_in + p_in:]
        k_out, c_out = refs[:n_out], refs[n_out:n_out + p_out]
        refs = refs[n_out + p_out:]
        k_scr, sems = refs[:n_scr], refs[n_scr:]
        step = 0
        for d, g in enumerate(grid):
            step = step * g + pl.program_id(d)
        n_steps = math.prod(grid)

        @pl.when(step == 0)
        def _():
            plan.start(c_in, c_out, *sems)

        @pl.when(step == min(n_steps - 1, int(n_steps * PLAN_MIDDLE)))
        def _():
            plan.middle(c_in, c_out, *sems)

        body(*k_in, *k_out, *k_scr)

        @pl.when(step == n_steps - 1)
        def _():
            plan.finish(c_in, c_out, *sems)

    any_spec = pl.BlockSpec(memory_space=pl.ANY)
    res = pl.pallas_call(
        with_plan, name=name, grid=grid, in_specs=list(in_specs) + [any_spec] * p_in,
        out_specs=list(out_specs) + [any_spec] * p_out, out_shape=list(out_shape) + list(plan.out_shapes),
        scratch_shapes=list(scratch) + _plan_scratch(plan),
        compiler_params=_cparams(("arbitrary",) * len(grid)))(*ins, *plan.ins)
    return res[:n_out], res[n_out:]


def _rowwise(name, fn, rows, vecs, row_outs, acc_outs=(), tm=128, plan=None):
    t = rows[0].shape[0]
    tm = min(tm, t)
    assert t % tm == 0 and tm % SUBLANES == 0
    n_r, n_v, n_o = len(rows), len(vecs), len(row_outs)

    def body(*refs):
        r_in, v_in = refs[:n_r], refs[n_r:n_r + n_v]
        r_out, a_out = refs[n_r + n_v:n_r + n_v + n_o], refs[n_r + n_v + n_o:]
        outs, accs = fn([r[...] for r in r_in], [v[...] for v in v_in])
        for o_ref, o in zip(r_out, outs, strict=True):
            o_ref[...] = o.astype(o_ref.dtype)
        if a_out:
            @pl.when(pl.program_id(0) == 0)
            def _():
                for a_ref in a_out:
                    a_ref[...] = jnp.zeros_like(a_ref)
            for a_ref, a in zip(a_out, accs, strict=True):
                a_ref[...] += a.reshape(tm // SUBLANES, SUBLANES, a.shape[-1]).sum(axis=0)

    in_specs = [pl.BlockSpec((tm, r.shape[1]), lambda i: (i, 0)) for r in rows]
    in_specs += [pl.BlockSpec(v.shape, lambda i: (0, 0)) for v in vecs]
    out_specs = [pl.BlockSpec((tm, w), lambda i: (i, 0)) for w, _ in row_outs]
    out_specs += [pl.BlockSpec((SUBLANES, w), lambda i: (0, 0)) for w in acc_outs]
    out_shape = [SDS((t, w), dt) for w, dt in row_outs] + [SDS((SUBLANES, w), F32) for w in acc_outs]
    return _pcall(name, body, (t // tm,), in_specs, [*rows, *vecs], out_specs, out_shape, semantics=("arbitrary",),
                  plan=plan)


def _tile(n, want):
    if n <= want:
        return n
    for t in range(want // LANES * LANES, 0, -LANES):
        if n % t == 0:
            return t
    raise ValueError(f"no tile for {n}")


_DOT_DIMS = {"nn": (((1,), (0,)), ((), ())), "nt": (((1,), (1,)), ((), ())), "tn": (((0,), (0,)), ((), ()))}


def _matmul(name, a, b, mode, out_dtypes, epilogue=None, extras=(), vecs=(), a_pro=None,
            tm=1024, tn=512, tk=4096, exact=False, b_blocked=False, out_blocked=0, b_rows=None, plan=None):
    cs = b.shape[-1] if b_blocked else None
    b2 = (b.shape[1], b.shape[0] * b.shape[2]) if b_blocked else b.shape
    if mode == "tn":
        (k, m), (k2, n) = a.shape, b2
    elif mode == "nt":
        (m, k), (n, k2) = a.shape, b2
    else:
        (m, k), (k2, n) = a.shape, b2
    assert k == k2 and not (b_blocked and mode == "tn")
    row0 = 0
    if b_rows is not None:
        assert mode == "nt" and not b_blocked
        row0, n = b_rows
        tn = _tile(math.gcd(n, row0) if row0 else n, tn)
    tm, tn, tk = _tile(m, tm), _tile(n, tn), _tile(k, tk)
    if b_blocked and mode == "nn":
        tn = _tile(cs, tn)
    if b_blocked and mode == "nt":
        tk = _tile(cs, tk)
    if out_blocked:
        tn = _tile(n // out_blocked, tn)
    nk = k // tk

    def vmem_bytes(width):
        operands = 2 * (tm * tk * a.dtype.itemsize + tk * width * b.dtype.itemsize)
        tiles = 2 * tm * width * (sum(jnp.dtype(dt).itemsize for dt in out_dtypes) + sum(e.dtype.itemsize for e in extras))
        return operands + tiles + tm * width * 4 * (2 if nk > 1 else 1)

    extent = cs if (b_blocked and mode == "nn") else n // out_blocked if out_blocked else n
    if extent % (2 * tn) == 0 and row0 % (2 * tn) == 0 and vmem_bytes(2 * tn) <= MATMUL_VMEM_BUDGET:
        tn *= 2
    n_e, n_v, n_o = len(extras), len(vecs), len(out_dtypes)
    precision = HIGHEST if exact else None

    def body(*refs):
        a_ref, b_ref = refs[:2]
        e_refs, v_refs = refs[2:2 + n_e], refs[2 + n_e:2 + n_e + n_v]
        o_refs = refs[2 + n_e + n_v:2 + n_e + n_v + n_o]

        def product():
            av = a_ref[...]
            if a_pro is not None:
                av = a_pro(av)
            return lax.dot_general(av, b_ref[...], _DOT_DIMS[mode], precision=precision, preferred_element_type=F32)

        def finish(acc):
            res = (acc,) if epilogue is None else epilogue(acc, [e[...] for e in e_refs], [v[...] for v in v_refs])
            for o_ref, r in zip(o_refs, res, strict=True):
                o_ref[...] = r.astype(o_ref.dtype)

        if nk == 1:
            finish(product())
            return
        acc_ref = refs[-1]
        kk = pl.program_id(2)

        @pl.when(kk == 0)
        def _():
            acc_ref[...] = product()

        @pl.when(kk > 0)
        def _():
            acc_ref[...] += product()

        @pl.when(kk == nk - 1)
        def _():
            finish(acc_ref[...])

    if mode == "tn":
        a_spec = pl.BlockSpec((tk, tm), lambda i, j, kk: (kk, i))
    else:
        a_spec = pl.BlockSpec((tm, tk), lambda i, j, kk: (i, kk))
    if b_blocked and mode == "nn":
        per = cs // tn
        b_spec = pl.BlockSpec((None, tk, tn), lambda i, j, kk: (j // per, kk, j % per))
    elif b_blocked:
        per = cs // tk
        b_spec = pl.BlockSpec((None, tn, tk), lambda i, j, kk: (kk // per, j, kk % per))
    elif mode == "nt":
        assert row0 % tn == 0
        b_spec = pl.BlockSpec((tn, tk), lambda i, j, kk: (j + row0 // tn, kk))
    else:
        b_spec = pl.BlockSpec((tk, tn), lambda i, j, kk: (kk, j))
    tile = pl.BlockSpec((tm, tn), lambda i, j, kk: (i, j))
    if out_blocked:
        per_o = n // out_blocked // tn
        out_spec = pl.BlockSpec((None, tm, tn), lambda i, j, kk: (j // per_o, i, j % per_o))
        out_shape = [SDS((out_blocked, m, n // out_blocked), dt) for dt in out_dtypes]
    else:
        out_spec, out_shape = tile, [SDS((m, n), dt) for dt in out_dtypes]
    in_specs = [a_spec, b_spec] + [tile] * n_e + [pl.BlockSpec((1, tn), lambda i, j, kk: (0, j))] * n_v
    res = _pcall(name, body, (m // tm, n // tn, nk), in_specs, [a, b, *extras, *vecs], [out_spec] * n_o, out_shape,
                 scratch=[pltpu.VMEM((tm, tn), F32)] if nk > 1 else [],
                 semantics=("parallel", "parallel", "arbitrary"), plan=plan)
    if plan is None:
        return res[0] if n_o == 1 else res
    return (res[0][0] if n_o == 1 else res[0]), res[1]


def _rms_fwd(x):
    r = lax.rsqrt(jnp.mean(x * x, axis=-1, keepdims=True) + EPS)
    return x * r, r


def _rms_bwd(dxn, xn, r):
    return r * (dxn - xn * jnp.mean(dxn * xn, axis=-1, keepdims=True))


_INV_SQRT2 = 1.0 / math.sqrt(2.0)
_INV_SQRT2PI = 1.0 / math.sqrt(2.0 * math.pi)


def _gelu(y):
    return 0.5 * y * (1.0 + lax.erf(y * _INV_SQRT2))


def _gelu_grad(y):
    return 0.5 * (1.0 + lax.erf(y * _INV_SQRT2)) + y * (_INV_SQRT2PI * jnp.exp(-0.5 * y * y))


def _sigmoid(z):
    return 1.0 / (1.0 + jnp.exp(-z))


def _adam_math(w, g, m, v):
    m = ADAM_B1 * m + (1.0 - ADAM_B1) * g
    v = ADAM_B2 * v + (1.0 - ADAM_B2) * (g * g)
    m_hat = m / (1.0 - ADAM_B1 ** ADAM_STEP)
    v_hat = v / (1.0 - ADAM_B2 ** ADAM_STEP)
    delta = -ADAM_LR * (m_hat / (jnp.sqrt(v_hat) + ADAM_EPS) + ADAM_WD * w)
    return delta, m, v


def _norm_mod_fwd(name, x, g, scale, shift):
    def fn(rows, vecs):
        (xv,), (gv, sc, sh) = rows, vecs
        xn, _ = _rms_fwd(xv)
        return [(xn * gv) * (1.0 + sc) + sh], []
    return _rowwise(name, fn, [x], [g, scale, shift], [(x.shape[1], BF16)])[0]


def _norm_mod_bwd(name, x, dh, dres, g, scale, gated=None, plan=None):
    d = x.shape[1]

    def fn(rows, vecs):
        xv, dhv, drv = rows[:3]
        gv, sc = vecs[:2]
        xn, r = _rms_fwd(xv)
        t = xn * gv
        dt = dhv * (1.0 + sc)
        dx = drv + _rms_bwd(dt * gv, xn, r)
        if gated is None:
            return [dx], [dhv * t, dhv, dt * xn]
        return [dx, dx * vecs[2]], [dhv * t, dhv, dt * xn, dx * rows[3].astype(F32)]
    extra_rows, extra_vecs = ([gated[0]], [gated[1]]) if gated is not None else ([], [])
    res = _rowwise(name, fn, [x, dh, dres] + extra_rows, [g, scale] + extra_vecs,
                   [(d, F32)] + [(d, BF16)] * len(extra_rows), [d] * (3 + len(extra_rows)), plan=plan)
    outs, rest = res if plan is not None else (res, None)
    n_rows = 1 + len(extra_rows)
    return (*outs[:n_rows], *[a.sum(0) for a in outs[n_rows:]]), rest


def _final_loss(name, x, tgt, g, val, gate):
    d = x.shape[1]

    def fn(rows, vecs):
        (xv, tv, vv), (gv, gate_v) = rows, vecs
        xn, r = _rms_fwd(xv)
        e = xn * gv - tv
        dy = e * (1.0 / d)
        dx = _rms_bwd(dy * gv, xn, r)
        return [dx, dx * gate_v], [e * e, dy * xn, dx * vv.astype(F32)]
    dx, dval, sq, dg, dgate = _rowwise(name, fn, [x, tgt, val], [g, gate], [(d, F32), (d, BF16)], [d, d, d])
    return dx, dval, 0.5 * jnp.sum(sq) / d, dg.sum(0), dgate.sum(0)


def _group_norm_fwd(name, attn, ssm, g_a, g_s):
    def fn(rows, vecs):
        (av, sv), (ga, gs) = rows, vecs
        return [jnp.concatenate([_rms_fwd(av)[0] * ga, _rms_fwd(sv)[0] * gs], axis=1)], []
    return _rowwise(name, fn, [attn, ssm], [g_a, g_s], [(attn.shape[1] + ssm.shape[1], BF16)])[0]


def _group_norm_bwd(name, attn, ssm, dmixed, g_a, g_s, plan=None):
    da_w, ds_w = attn.shape[1], ssm.shape[1]

    def fn(rows, vecs):
        (av, sv, dm), (ga, gs) = rows, vecs
        an, ra = _rms_fwd(av)
        sn, rs = _rms_fwd(sv)
        dma, dms = dm[:, :da_w], dm[:, da_w:]
        return [_rms_bwd(dma * ga, an, ra), _rms_bwd(dms * gs, sn, rs)], [dma * an, dms * sn]
    res = _rowwise(name, fn, [attn, ssm, dmixed], [g_a, g_s], [(da_w, F32), (ds_w, F32)], [da_w, ds_w], plan=plan)
    (dattn, dssm, dga, dgs), rest = res if plan is not None else (res, None)
    return (dattn, dssm, dga.sum(0), dgs.sum(0)), rest


def _gelu_fwd(name, y):
    def fn(rows, vecs):
        return [_gelu(rows[0])], []
    return _rowwise(name, fn, [y], [], [(y.shape[1], BF16)])[0]


def _glu_bwd(name, dout, y, z):
    d = y.shape[1]

    def fn(rows, vecs):
        dov, yv, zv = rows
        sg = _sigmoid(zv)
        dz = dov * _gelu(yv) * sg * (1.0 - sg)
        return [dz, dov * sg], [dz]
    dz, dyg, db = _rowwise(name, fn, [dout, y, z], [], [(d, BF16), (d, F32)], [d])
    return dz, dyg, db.sum(0)


def _adam_shard(name, parts, w, m, v):
    r, c = w.shape
    n_parts = sum(1 if p.ndim == 2 else p.shape[0] for p in parts)
    row_bytes = 2 * c * (n_parts * parts[0].dtype.itemsize + 7 * 4)
    tr = min(128, r)
    while tr > SUBLANES and tr * row_bytes > VMEM_LIMIT_BYTES // 2:
        tr //= 2
    assert r % tr == 0
    n_p = len(parts)

    def body(*refs):
        p_refs, (w_ref, m_ref, v_ref, g_out, d_out, m_out, v_out) = refs[:n_p], refs[n_p:]
        g = None
        for p_ref in p_refs:
            terms = [p_ref[...]] if len(p_ref.shape) == 2 else [p_ref[j] for j in range(p_ref.shape[0])]
            for t in terms:
                g = t.astype(F32) if g is None else g + t.astype(F32)
        delta, m_new, v_new = _adam_math(w_ref[...], g, m_ref[...], v_ref[...])
        g_out[...], d_out[...], m_out[...], v_out[...] = g, delta, m_new, v_new

    tile = pl.BlockSpec((tr, c), lambda i: (i, 0))
    p_specs = [tile if p.ndim == 2 else pl.BlockSpec((p.shape[0], tr, c), lambda i: (0, i, 0)) for p in parts]
    return _pcall(name, body, (r // tr,), p_specs + [tile] * 3, [*parts, w, m, v], [tile] * 4, [SDS((r, c), F32)] * 4,
                  semantics=("parallel",))


def _pair_add(name, blocks, from_sibling):
    _, r, c = blocks.shape
    tr = min(256, r)
    assert r % tr == 0

    def body(core_ref, b_ref, s_ref, o_ref):
        o_ref[...] = (b_ref[...].astype(F32) + s_ref[...].astype(F32)).astype(o_ref.dtype)

    core = lax.axis_index("c").astype(jnp.int32).reshape(1)
    mine = pl.BlockSpec((None, tr, c), lambda s, i, core_ref: (2 * s + core_ref[0], i, 0))
    slot = pl.BlockSpec((None, tr, c), lambda s, i, core_ref: (s, i, 0))
    grid_spec = pltpu.PrefetchScalarGridSpec(num_scalar_prefetch=1, grid=(4, r // tr), in_specs=[mine, slot],
                                             out_specs=slot)
    return pl.pallas_call(body, name=name, grid_spec=grid_spec, out_shape=SDS((4, r, c), blocks.dtype),
                          compiler_params=_cparams(("parallel", "parallel")))(core, blocks, from_sibling)


def _ada_update(name, c_act_t, dmod, w, m, v, tr=128, plan=None):
    r, c = w.shape
    tr = min(tr, r)
    assert r % tr == 0

    def body(c_ref, d_ref, w_ref, m_ref, v_ref, g_out, d_out, m_out, v_out):
        g = jnp.dot(c_ref[...], d_ref[...], precision=lax.Precision.HIGHEST, preferred_element_type=F32)
        delta, m_new, v_new = _adam_math(w_ref[...], g, m_ref[...], v_ref[...])
        g_out[...], d_out[...], m_out[...], v_out[...] = g, delta, m_new, v_new

    tile = pl.BlockSpec((tr, c), lambda i: (i, 0))
    in_specs = [pl.BlockSpec((tr, N_DEV), lambda i: (i, 0)), pl.BlockSpec((N_DEV, c), lambda i: (0, 0)), tile, tile, tile]
    return _pcall(name, body, (r // tr,), in_specs, [c_act_t, dmod, w, m, v], [tile] * 4, [SDS((r, c), F32)] * 4,
                  semantics=("parallel",), plan=plan)


def _rotate_half(x):
    w = x.shape[1]
    half = HEAD_DIM // 2
    lane = lax.broadcasted_iota(jnp.int32, x.shape, 1)
    return jnp.where((lane % HEAD_DIM) < half, -pltpu.roll(x, w - half, 1), pltpu.roll(x, half, 1))


def _lane_tile(tab, w):
    return tab[:, :w] if w <= LANES else jnp.tile(tab, (1, w // LANES))


def _rope(x, cos, sin):
    return x * cos + _rotate_half(x) * sin


def _rope_t(dy, cos, sin):
    return dy * cos - _rotate_half(dy) * sin


def _band_mask(n):
    shape = (Q_PER_KV * WINDOW, 2 * WINDOW)
    i = lax.broadcasted_iota(jnp.int32, shape, 0) & (WINDOW - 1)
    j = lax.broadcasted_iota(jnp.int32, shape, 1)
    return (j > i) & (j <= i + WINDOW) & ((n > 0) | (j >= WINDOW))


def _stack_heads(x, hk):
    first = hk * Q_PER_KV
    return jnp.concatenate([x[:, (first + g) * HEAD_DIM:(first + g + 1) * HEAD_DIM] for g in range(Q_PER_KV)], axis=0)


def _stack_cols(ref, hk):
    first = hk * Q_PER_KV
    return jnp.concatenate([ref[:, first + g:first + g + 1] for g in range(Q_PER_KV)], axis=0)


def _stack_sinks(sink_ref, hk):
    first = hk * Q_PER_KV
    return jnp.concatenate([jnp.broadcast_to(sink_ref[0:1, first + g:first + g + 1], (WINDOW, 1))
                            for g in range(Q_PER_KV)], axis=0)


def _attn_specs(da, dkv, nb):
    cur = lambda n: (jnp.minimum(n, nb - 1), 0)
    prev = lambda n: (jnp.maximum(jnp.minimum(n, nb - 1) - 1, 0), 0)
    return dict(
        q=pl.BlockSpec((WINDOW, da), cur), kv_cur=pl.BlockSpec((WINDOW, dkv), cur),
        kv_prev=pl.BlockSpec((WINDOW, dkv), prev), tab_cur=pl.BlockSpec((WINDOW, LANES), cur),
        tab_prev=pl.BlockSpec((WINDOW, LANES), prev))


def _attn_fwd(name, q, k, v, cos, sin, sinks, plan=None):
    s, da = q.shape
    dkv = k.shape[1]
    nq, nb = da // HEAD_DIM, s // WINDOW
    scale = HEAD_DIM ** -0.5

    def body(q_ref, kp_ref, kc_ref, vp_ref, vc_ref, cc_ref, sc_ref, cp_ref, sp_ref, sink_ref, o_ref, lse_ref):
        n = pl.program_id(0)
        cc, sc, cp, sp = cc_ref[...], sc_ref[...], cp_ref[...], sp_ref[...]
        qr = _rope(q_ref[...], _lane_tile(cc, da), _lane_tile(sc, da)).astype(BF16)
        kk = jnp.concatenate([_rope(kp_ref[...], _lane_tile(cp, dkv), _lane_tile(sp, dkv)),
                              _rope(kc_ref[...], _lane_tile(cc, dkv), _lane_tile(sc, dkv))], axis=0).astype(BF16)
        vv = jnp.concatenate([vp_ref[...], vc_ref[...]], axis=0).astype(BF16)
        valid = _band_mask(n)
        for hk in range(nq // Q_PER_KV):
            ks = slice(hk * HEAD_DIM, (hk + 1) * HEAD_DIM)
            sco = lax.dot_general(_stack_heads(qr, hk), kk[:, ks], _DOT_DIMS["nt"], preferred_element_type=F32) * scale
            sco = jnp.where(valid, sco, -1e30)
            sink = _stack_sinks(sink_ref, hk)
            mx = jnp.maximum(jnp.max(sco, axis=1, keepdims=True), sink)
            p = jnp.exp(sco - mx)
            den = jnp.sum(p, axis=1, keepdims=True) + jnp.exp(sink - mx)
            o8 = jnp.dot((p / den).astype(BF16), vv[:, ks], preferred_element_type=F32)
            lse8 = mx + jnp.log(den)
            for g in range(Q_PER_KV):
                hq, rows = hk * Q_PER_KV + g, slice(g * WINDOW, (g + 1) * WINDOW)
                o_ref[:, hq * HEAD_DIM:(hq + 1) * HEAD_DIM] = o8[rows]
                lse_ref[:, hq:hq + 1] = lse8[rows]

    sp_ = _attn_specs(da, dkv, nb)
    in_specs = [sp_["q"], sp_["kv_prev"], sp_["kv_cur"], sp_["kv_prev"], sp_["kv_cur"],
                sp_["tab_cur"], sp_["tab_cur"], sp_["tab_prev"], sp_["tab_prev"], pl.BlockSpec((1, nq), lambda n: (0, 0))]
    return _pcall(name, body, (nb,), in_specs, [q, k, k, v, v, cos, sin, cos, sin, sinks],
                  [sp_["q"], pl.BlockSpec((WINDOW, nq), lambda n: (n, 0))], [SDS((s, da), F32), SDS((s, nq), F32)],
                  semantics=("arbitrary",), plan=plan)


def _attn_bwd(name, q, k, v, cos, sin, sinks, out, lse, dout, plan=None):
    s, da = q.shape
    dkv = k.shape[1]
    nq, nb = da // HEAD_DIM, s // WINDOW
    scale = HEAD_DIM ** -0.5

    def body(q_ref, kp_ref, kc_ref, vp_ref, vc_ref, cc_ref, sc_ref, cp_ref, sp_ref, sink_ref, o_ref, lse_ref,
             do_ref, dq_ref, dk_ref, dv_ref, dsink_ref, dk_carry, dv_carry):
        n = pl.program_id(0)
        cp, sp = _lane_tile(cp_ref[...], dkv), _lane_tile(sp_ref[...], dkv)

        @pl.when(n == 0)
        def _():
            dk_carry[...] = jnp.zeros_like(dk_carry)
            dv_carry[...] = jnp.zeros_like(dv_carry)
            dsink_ref[...] = jnp.zeros_like(dsink_ref)

        @pl.when(n < nb)
        def _():
            cc, sc = cc_ref[...], sc_ref[...]
            qr = _rope(q_ref[...], _lane_tile(cc, da), _lane_tile(sc, da)).astype(BF16)
            kk = jnp.concatenate([_rope(kp_ref[...], cp, sp),
                                  _rope(kc_ref[...], _lane_tile(cc, dkv), _lane_tile(sc, dkv))], axis=0).astype(BF16)
            vv = jnp.concatenate([vp_ref[...], vc_ref[...]], axis=0).astype(BF16)
            valid = _band_mask(n)
            do_all, o_all = do_ref[...], o_ref[...]
            for hk in range(nq // Q_PER_KV):
                ks = slice(hk * HEAD_DIM, (hk + 1) * HEAD_DIM)
                q8, lse8 = _stack_heads(qr, hk), _stack_cols(lse_ref, hk)
                sco = lax.dot_general(q8, kk[:, ks], _DOT_DIMS["nt"], preferred_element_type=F32) * scale
                probs = jnp.where(valid, jnp.exp(sco - lse8), 0.0)
                do8 = _stack_heads(do_all, hk)
                delta = jnp.sum(do8 * _stack_heads(o_all, hk), axis=1, keepdims=True)
                do8 = do8.astype(BF16)
                dp = lax.dot_general(do8, vv[:, ks], _DOT_DIMS["nt"], preferred_element_type=F32)
                ds = (probs * (dp - delta) * scale).astype(BF16)
                dq8 = jnp.dot(ds, kk[:, ks], preferred_element_type=F32)
                dk_h = lax.dot_general(ds, q8, _DOT_DIMS["tn"], preferred_element_type=F32)
                dv_h = lax.dot_general(probs.astype(BF16), do8, _DOT_DIMS["tn"], preferred_element_type=F32)
                dsink8 = -jnp.exp(_stack_sinks(sink_ref, hk) - lse8) * delta
                for g in range(Q_PER_KV):
                    hq, rows = hk * Q_PER_KV + g, slice(g * WINDOW, (g + 1) * WINDOW)
                    dq_ref[:, hq * HEAD_DIM:(hq + 1) * HEAD_DIM] = dq8[rows]
                    dsink_ref[:, hq:hq + 1] += dsink8[rows].reshape(WINDOW // SUBLANES, SUBLANES, 1).sum(axis=0)
                dk_ref[:, ks] = dk_carry[:, ks] + dk_h[:WINDOW]
                dv_ref[:, ks] = dv_carry[:, ks] + dv_h[:WINDOW]
                dk_carry[:, ks] = dk_h[WINDOW:]
                dv_carry[:, ks] = dv_h[WINDOW:]
            dq_ref[...] = _rope_t(dq_ref[...], _lane_tile(cc, da), _lane_tile(sc, da))
            dk_ref[...] = _rope_t(dk_ref[...], cp, sp)

        @pl.when(n == nb)
        def _():
            dk_ref[...] = _rope_t(dk_carry[...], cp, sp)
            dv_ref[...] = dv_carry[...]

    sp_ = _attn_specs(da, dkv, nb)
    last_prev = lambda n: (jnp.maximum(n - 1, 0), 0)
    tab_prev = pl.BlockSpec((WINDOW, LANES), last_prev)
    kv_out = pl.BlockSpec((WINDOW, dkv), last_prev)
    lse_spec = pl.BlockSpec((WINDOW, nq), lambda n: (jnp.minimum(n, nb - 1), 0))
    in_specs = [sp_["q"], sp_["kv_prev"], sp_["kv_cur"], sp_["kv_prev"], sp_["kv_cur"],
                sp_["tab_cur"], sp_["tab_cur"], tab_prev, tab_prev,
                pl.BlockSpec((1, nq), lambda n: (0, 0)), sp_["q"], lse_spec, sp_["q"]]
    res = _pcall(name, body, (nb + 1,), in_specs, [q, k, k, v, v, cos, sin, cos, sin, sinks, out, lse, dout],
                 [sp_["q"], kv_out, kv_out, pl.BlockSpec((SUBLANES, nq), lambda n: (0, 0))],
                 [SDS((s, da), F32), SDS((s, dkv), F32), SDS((s, dkv), F32), SDS((SUBLANES, nq), F32)],
                 scratch=[pltpu.VMEM((WINDOW, dkv), F32), pltpu.VMEM((WINDOW, dkv), F32)],
                 semantics=("arbitrary",), plan=plan)
    (dq, dk, dv, dsink), rest = res if plan is not None else (res, None)
    return (dq, dk, dv, dsink.sum(0)), rest


def _ssm_operators(lam_re, lam_im, log_step, b_re, b_im, c_re, c_im, d_skip):
    g, p = lam_re.shape
    h = b_re.shape[-1]
    l = SSM_CHUNK
    step = jnp.exp(log_step)[:, None]
    mag = jnp.exp(lam_re * step)
    ar, ai = mag * jnp.cos(lam_im * step), mag * jnp.sin(lam_im * step)
    den = lam_re * lam_re + lam_im * lam_im
    cr = ((ar - 1.0) * lam_re + ai * lam_im) / den
    ci = (ai * lam_re - (ar - 1.0) * lam_im) / den
    bbr = cr[..., None] * b_re - ci[..., None] * b_im
    bbi = cr[..., None] * b_im + ci[..., None] * b_re
    powers = jnp.arange(l + 1, dtype=F32)[None, :, None]
    pw_mag = jnp.exp((lam_re * step)[:, None, :] * powers)
    pw_ang = (lam_im * step)[:, None, :] * powers
    pwr, pwi = pw_mag * jnp.cos(pw_ang), pw_mag * jnp.sin(pw_ang)
    cpr = c_re[:, None] * pwr[:, :, None, :] - c_im[:, None] * pwi[:, :, None, :]
    cpi = c_re[:, None] * pwi[:, :, None, :] + c_im[:, None] * pwr[:, :, None, :]
    kern = (jnp.einsum("gtop,gpi->gtoi", cpr[:, :l], bbr, precision=lax.Precision.HIGHEST)
            - jnp.einsum("gtop,gpi->gtoi", cpi[:, :l], bbi, precision=lax.Precision.HIGHEST))
    kern = kern.at[:, 0].add(d_skip.reshape(g, h)[:, :, None] * jnp.eye(h, dtype=F32))
    lag = jnp.arange(l)
    place = (lag[None, None, :] - lag[None, :, None] == lag[:, None, None]).astype(F32)
    tm = jnp.einsum("gtoh,tji->gjhio", kern, place, precision=lax.Precision.HIGHEST).reshape(g, l * h, l * h)
    rev_r, rev_i = pwr[:, l - 1::-1][:, :l], pwi[:, l - 1::-1][:, :l]
    er = rev_r[:, :, None, :] * bbr.transpose(0, 2, 1)[:, None] - rev_i[:, :, None, :] * bbi.transpose(0, 2, 1)[:, None]
    ei = rev_r[:, :, None, :] * bbi.transpose(0, 2, 1)[:, None] + rev_i[:, :, None, :] * bbr.transpose(0, 2, 1)[:, None]
    em = jnp.concatenate([er, ei], axis=-1).reshape(g, l * h, 2 * p)
    fr = cpr[:, 1:].transpose(0, 3, 1, 2).reshape(g, p, l * h)
    fi = -cpi[:, 1:].transpose(0, 3, 1, 2).reshape(g, p, l * h)
    fm = jnp.concatenate([fr, fi], axis=1)
    return tm, em, fm, pwr[:, l], pwi[:, l]


def _decay_lanes(alr, ali):
    return jnp.concatenate([alr, alr], axis=1), jnp.concatenate([-ali, ali], axis=1)


def _column_blocks(s, ds):
    wide = 2 if (ds // LANES) % 2 == 0 else 1
    return wide, pl.BlockSpec((s, wide * LANES), lambda i: (0, i // wide))


def _my_columns(wide, block_ref, stage_ref, store):
    part = pl.program_id(0) % wide
    for p in range(wide):
        @pl.when(part == p)
        def _():
            cols = slice(p * LANES, (p + 1) * LANES)
            if store:
                block_ref[:, cols] = stage_ref[...]
            else:
                stage_ref[...] = block_ref[:, cols]


def _ssm_fwd(name, u, tm, em, fm, acat, bcat, plan=None):
    s, ds = u.shape
    g, lh, p2 = em.shape
    gb, h = SSM_GROUPS_PER_STEP, lh // SSM_CHUNK
    assert gb * h == LANES and g * h == ds and s % SSM_CHUNK == 0
    nc, half = s // SSM_CHUNK, p2 // 2

    def body(u_ref, tm_ref, em_ref, fm_ref, a_ref, b_ref, y_ref, xp_ref, uc_ref, yc_ref, st_ref, col_ref):
        _my_columns(wide, u_ref, col_ref, store=False)
        _to_chunks(col_ref, uc_ref, nc, h)
        for i in range(gb):
            st_ref[pl.ds(i, nc, stride=gb), :] = jnp.dot(uc_ref[i], em_ref[i], precision=SSM_PRECISION,
                                                         preferred_element_type=F32)
        av, bv = a_ref[...], b_ref[...]

        def step(c, carry):
            x, xs = carry
            rows = pl.ds(pl.multiple_of(c * gb, gb), gb)
            loc = st_ref[rows, :]
            st_ref[rows, :] = x
            return av * x + bv * xs + loc, av * xs - bv * x + pltpu.roll(loc, half, 1)
        zero = jnp.zeros((gb, p2), F32)
        lax.fori_loop(0, nc, step, (zero, zero), unroll=4)
        for i in range(gb):
            xp = st_ref[pl.ds(i, nc, stride=gb), :]
            xp_ref[i] = xp
            yc_ref[i] = (jnp.dot(uc_ref[i], tm_ref[i], precision=SSM_PRECISION, preferred_element_type=F32)
                         + jnp.dot(xp, fm_ref[i], precision=SSM_PRECISION, preferred_element_type=F32))
        _from_chunks(yc_ref, col_ref, nc, h)
        _my_columns(wide, y_ref, col_ref, store=True)

    blk = lambda r, c: pl.BlockSpec((gb, r, c), lambda i: (i, 0, 0))
    vec = pl.BlockSpec((gb, p2), lambda i: (i, 0))
    wide, col = _column_blocks(s, ds)
    return _pcall(name, body, (g // gb,), [col, blk(lh, lh), blk(lh, p2), blk(p2, lh), vec, vec],
                  [u, tm, em, fm, acat, bcat], [col, blk(nc, p2), blk(nc, lh)],
                  [SDS((s, ds), F32), SDS((g, nc, p2), F32), SDS((g, nc, lh), F32)],
                  scratch=[pltpu.VMEM((gb, nc, lh), F32), pltpu.VMEM((nc * gb, p2), F32), pltpu.VMEM((s, LANES), F32)],
                  semantics=("arbitrary",), plan=plan)


def _ssm_bwd(name, u_chunks, dy, xprev, tm, em, fm, acat, bcat, plan=None):
    s, ds = dy.shape
    g, lh, p2 = em.shape
    gb, h = SSM_GROUPS_PER_STEP, lh // SSM_CHUNK
    nc, half = s // SSM_CHUNK, p2 // 2

    def body(uc_ref, dy_ref, xp_ref, tm_ref, em_ref, fm_ref, a_ref, b_ref,
             du_ref, dtm_ref, dem_ref, dfm_ref, r1_ref, r2_ref, dyc_ref, duc_ref, gs_ref, xs_ref, col_ref):
        _my_columns(wide, dy_ref, col_ref, store=False)
        _to_chunks(col_ref, dyc_ref, nc, h)
        for i in range(gb):
            gs_ref[pl.ds(i, nc, stride=gb), :] = lax.dot_general(
                dyc_ref[i], fm_ref[i], _DOT_DIMS["nt"], precision=SSM_PRECISION, preferred_element_type=F32)
            xs_ref[pl.ds(i, nc, stride=gb), :] = xp_ref[i]
        av, bv = a_ref[...], b_ref[...]

        def step(t, carry):
            grad, gsw, r1, r2 = carry
            c = nc - 1 - t
            rows = pl.ds(pl.multiple_of(c * gb, gb), gb)
            dxp, xp = gs_ref[rows, :], xs_ref[rows, :]
            gs_ref[rows, :] = grad
            r1 = r1 + grad * xp
            r2 = r2 + grad * pltpu.roll(xp, half, 1)
            return dxp + av * grad - bv * gsw, pltpu.roll(dxp, half, 1) + av * gsw + bv * grad, r1, r2
        zero = jnp.zeros((gb, p2), F32)
        _, _, r1, r2 = lax.fori_loop(0, nc, step, (zero, zero, zero, zero), unroll=4)
        r1_ref[...], r2_ref[...] = r1, r2
        for i in range(gb):
            dxl = gs_ref[pl.ds(i, nc, stride=gb), :]
            duc_ref[i] = (lax.dot_general(dyc_ref[i], tm_ref[i], _DOT_DIMS["nt"], precision=SSM_PRECISION,
                                          preferred_element_type=F32)
                          + lax.dot_general(dxl, em_ref[i], _DOT_DIMS["nt"], precision=SSM_PRECISION,
                                            preferred_element_type=F32))
            dtm_ref[i] = lax.dot_general(uc_ref[i], dyc_ref[i], _DOT_DIMS["tn"], precision=SSM_PRECISION,
                                         preferred_element_type=F32)
            dfm_ref[i] = lax.dot_general(xp_ref[i], dyc_ref[i], _DOT_DIMS["tn"], precision=SSM_PRECISION,
                                         preferred_element_type=F32)
            dem_ref[i] = lax.dot_general(uc_ref[i], dxl, _DOT_DIMS["tn"], precision=SSM_PRECISION,
                                         preferred_element_type=F32)
        _from_chunks(duc_ref, col_ref, nc, h)
        _my_columns(wide, du_ref, col_ref, store=True)

    blk = lambda r, c: pl.BlockSpec((gb, r, c), lambda i: (i, 0, 0))
    vec = pl.BlockSpec((gb, p2), lambda i: (i, 0))
    wide, col = _column_blocks(s, ds)
    chunked = pltpu.VMEM((gb, nc, lh), F32)
    res = _pcall(name, body, (g // gb,),
                 [blk(nc, lh), col, blk(nc, p2), blk(lh, lh), blk(lh, p2), blk(p2, lh), vec, vec],
                 [u_chunks, dy, xprev, tm, em, fm, acat, bcat],
                 [col, blk(lh, lh), blk(lh, p2), blk(p2, lh), vec, vec],
                 [SDS((s, ds), F32), SDS((g, lh, lh), F32), SDS((g, lh, p2), F32), SDS((g, p2, lh), F32),
                  SDS((g, p2), F32), SDS((g, p2), F32)],
                 scratch=[chunked, chunked, pltpu.VMEM((nc * gb, p2), F32), pltpu.VMEM((nc * gb, p2), F32),
                          pltpu.VMEM((s, LANES), F32)],
                 semantics=("arbitrary",), plan=plan)
    return res if plan is not None else (res, None)


def _to_chunks(src_ref, dst_ref, nc, h):
    per = LANES // h
    rows = min(nc, 64)
    grp = lax.broadcasted_iota(jnp.int32, (rows, LANES), 1) // h
    for r0 in range(0, nc, rows):
        for i in range(SSM_CHUNK):
            part, lo = divmod(i * h, LANES)
            step_rows = src_ref[pl.ds(r0 * SSM_CHUNK + i, rows, stride=SSM_CHUNK), :]
            for g in range(per):
                shift = (lo - g * h) % LANES
                piece = pltpu.roll(step_rows, shift, 1) if shift else step_rows
                out = dst_ref.at[g, r0:r0 + rows, part * LANES:(part + 1) * LANES]
                out[...] = piece if lo == 0 else jnp.where(grp == lo // h, piece, out[...])


def _from_chunks(src_ref, dst_ref, nc, h):
    per = LANES // h
    grp = lax.broadcasted_iota(jnp.int32, (nc, LANES), 1) // h
    for i in range(SSM_CHUNK):
        part, lo = divmod(i * h, LANES)
        row = None
        for g in range(per):
            piece = src_ref[g, :, part * LANES:(part + 1) * LANES]
            if (g * h - lo) % LANES:
                piece = pltpu.roll(piece, (g * h - lo) % LANES, 1)
            row = piece if row is None else jnp.where(grp == g, piece, row)
        dst_ref[pl.ds(i, nc, stride=SSM_CHUNK), :] = row


_SMALL = ("b_ada", "norm1_g", "sinks", "ssm_lam_re", "ssm_lam_im", "ssm_log_step", "ssm_b_re", "ssm_b_im",
          "ssm_c_re", "ssm_c_im", "ssm_d", "b_glu", "attn_out_g", "ssm_out_g", "norm2_g", "final_g")
_WEIGHTS = ("w_ada", "b_ada", "norm1_g", "w_in", "sinks", "ssm_lam_re", "ssm_lam_im", "ssm_log_step", "ssm_b_re",
            "ssm_b_im", "ssm_c_re", "ssm_c_im", "ssm_d", "w_glu", "b_glu", "attn_out_g", "ssm_out_g", "w_out",
            "norm2_g", "w_ff1", "w_ff2", "final_g")
_PACK_ALIGN = 128 * LANES


def _pack(parts):
    flat = jnp.concatenate([p.reshape(-1).astype(F32) for p in parts])
    pad = (-flat.shape[0]) % _PACK_ALIGN
    return jnp.pad(flat, (0, pad)).reshape(-1, LANES)


def kernel(x, c, w_ada, b_ada, norm1_g, w_in, sinks, ssm_lam_re, ssm_lam_im, ssm_log_step, ssm_b_re, ssm_b_im, ssm_c_re, ssm_c_im, ssm_d, w_glu, b_glu, attn_out_g, ssm_out_g, w_out, norm2_g, w_ff1, w_ff2, final_g, loss_target, m_w_ada, m_b_ada, m_norm1_g, m_w_in, m_sinks, m_ssm_lam_re, m_ssm_lam_im, m_ssm_log_step, m_ssm_b_re, m_ssm_b_im, m_ssm_c_re, m_ssm_c_im, m_ssm_d, m_w_glu, m_b_glu, m_attn_out_g, m_ssm_out_g, m_w_out, m_norm2_g, m_w_ff1, m_w_ff2, m_final_g, v_w_ada, v_b_ada, v_norm1_g, v_w_in, v_sinks, v_ssm_lam_re, v_ssm_lam_im, v_ssm_log_step, v_ssm_b_re, v_ssm_b_im, v_ssm_c_re, v_ssm_c_im, v_ssm_d, v_w_glu, v_b_glu, v_attn_out_g, v_ssm_out_g, v_w_out, v_norm2_g, v_w_ff1, v_w_ff2, v_final_g):
    args = dict(locals())
    weights = {n: args[n] for n in _WEIGHTS}
    mom = {n: args["m_" + n] for n in _WEIGHTS}
    var = {n: args["v_" + n] for n in _WEIGHTS}
    me = 4 * lax.axis_index("x") + 2 * lax.axis_index("y") + lax.axis_index("c")

    _, s, d = x.shape
    xs, tgt = x[0], loss_target[0]
    d_ssm = ssm_d.shape[-1]
    d_attn = d - d_ssm
    nq = d_attn // HEAD_DIM
    d_kv = (nq // Q_PER_KV) * HEAD_DIM
    p_state = ssm_b_re.shape[2]

    c_all, g_in = _run_plan("gather_c_w_in", _Gather([c, w_in[0].T.astype(BF16)]))
    c_all = c_all.reshape(N_DEV, d)
    w_in_t = g_in.reshape(-1, d)

    n_loc = w_ada.shape[-1]
    b_loc = lax.dynamic_slice_in_dim(b_ada, me * n_loc, n_loc, axis=1)
    silu = lambda t: t * _sigmoid(t)
    mod_part = _matmul("ada_mod", c_all, w_ada[0], "nn", [F32], a_pro=silu, vecs=[b_loc], exact=True,
                       epilogue=lambda acc, e, v: (acc + v[0],), tn=512, tk=d)
    mod_all = _run_plan("gather_mod", _Gather([mod_part]))[0]
    mod = lax.dynamic_index_in_dim(mod_all, me, axis=1, keepdims=False).reshape(N_MOD, 1, d)
    shift1, scale1, gate1, shift2, scale2, gate2 = [mod[i] for i in range(N_MOD)]

    h1 = _norm_mod_fwd("norm1", xs, norm1_g, scale1, shift1)
    q = _matmul("proj_q", h1, w_in_t, "nt", [F32], b_rows=(0, d_attn))
    kv = _matmul("proj_kv", h1, w_in_t, "nt", [F32], b_rows=(d_attn, 2 * d_kv))
    u = _matmul("proj_u", h1, w_in_t, "nt", [F32], b_rows=(d_attn + 2 * d_kv, d_ssm))
    k, v = kv[:, :d_kv], kv[:, d_kv:]

    half = HEAD_DIM // 2
    inv_freq = ROPE_THETA ** (-jnp.arange(half, dtype=F32) / half)
    ang = jnp.arange(s, dtype=F32)[:, None] * inv_freq[None, :]
    cos_t, sin_t = jnp.tile(jnp.cos(ang), (1, 4)), jnp.tile(jnp.sin(ang), (1, 4))
    (attn, lse), (g_glu, g_out) = _attn_fwd("attn_fwd", q, k, v, cos_t, sin_t, sinks,
                                            plan=_Gather([w_glu[0].astype(BF16), w_out[0].astype(BF16)]))
    w_glu_f = g_glu.reshape(d_ssm, d_ssm)
    w_out_f = g_out.reshape(d, d)

    ssm_params = (ssm_lam_re[0], ssm_lam_im[0], ssm_log_step[0], ssm_b_re[0], ssm_b_im[0], ssm_c_re[0],
                  ssm_c_im[0], ssm_d[0])
    (tm_op, em_op, fm_op, alr, ali), ssm_vjp = jax.vjp(_ssm_operators, *ssm_params)
    acat, bcat = _decay_lanes(alr, ali)
    (y_ssm, x_prev, u_chunks), (g_ff1,) = _ssm_fwd("ssm_fwd", u, tm_op, em_op, fm_op, acat, bcat,
                                                   plan=_Gather([w_ff1[0].astype(BF16)]))
    gelu_bf16 = lambda t: _gelu(t).astype(BF16)
    ssm_out, z_glu = _matmul(
        "glu", y_ssm, w_glu_f, "nn", [F32, F32], extras=[y_ssm], vecs=[b_glu], a_pro=gelu_bf16, tm=512,
        epilogue=lambda acc, e, v: (_gelu(e[0]) * _sigmoid(acc + v[0]), acc + v[0]))
    mixed = _group_norm_fwd("group_norm", attn, ssm_out, attn_out_g, ssm_out_g)
    x2, mo = _matmul("out_proj", mixed, w_out_f, "nn", [F32, BF16], extras=[xs], vecs=[gate1],
                     epilogue=lambda acc, e, v: (e[0] + v[0] * acc, acc))

    h2 = _norm_mod_fwd("norm2", x2, norm2_g, scale2, shift2)
    (a_ff, f_ff), (g_ff2,) = _matmul("ff1", h2, g_ff1, "nn", [BF16, BF16], b_blocked=True,
                                     epilogue=lambda acc, e, v: (acc, jnp.square(jnp.maximum(acc, 0.0))),
                                     plan=_Gather([w_ff2[0].astype(BF16)]))
    w_ff2_f = g_ff2.reshape(-1, d)
    x3, ff = _matmul("ff2", f_ff, w_ff2_f, "nn", [F32, BF16], extras=[x2], vecs=[gate2],
                     epilogue=lambda acc, e, v: (e[0] + v[0] * acc, acc))

    dx3, dff, loss_local, d_final_g, d_gate2 = _final_loss("final_loss", x3, tgt, final_g.reshape(1, d), ff, gate2)
    loss = lax.psum(loss_local, MESH_AXES)

    dw_ff2 = _matmul("ff2_dw", f_ff, dff, "tn", [BF16]).reshape(N_DEV, -1, d)
    da_ff, (p_ff2,) = _matmul("ff2_dx", dff, w_ff2_f, "nt", [BF16], extras=[a_ff],
                              epilogue=lambda acc, e, v: (acc * (2.0 * jnp.maximum(e[0].astype(F32), 0.0)),),
                              plan=_PairSwap([dw_ff2]))
    s_ff2 = _pair_add("pair_add_ff2", dw_ff2, p_ff2)
    dw_ff1, (r_ff2_a,) = _matmul("ff1_dw", h2, da_ff, "tn", [BF16], out_blocked=N_DEV,
                                 plan=_ChipExchange([s_ff2], (CHIP_X, CHIP_Y)))
    dh2, (r_ff2_b,) = _matmul("ff1_dx", da_ff, g_ff1, "nt", [F32], b_blocked=True,
                              plan=_ChipExchange([s_ff2], (CHIP_DIAGONAL,)))
    (dx2, dmo, d_scale2, d_shift2, d_norm2_g, d_gate1), (p_ff1,) = _norm_mod_bwd(
        "norm2_bwd", x2, dh2, dx3, norm2_g, scale2, gated=(mo, gate1), plan=_PairSwap([dw_ff1]))
    s_ff1 = _pair_add("pair_add_ff1", dw_ff1, p_ff1)

    dw_out = _matmul("out_dw", mixed, dmo, "tn", [BF16]).reshape(N_DEV, -1, d)
    dmixed, (r_out_a,) = _matmul("out_dx", dmo, w_out_f, "nt", [F32], plan=_Exchange([dw_out], (1, 4, 2)))
    (dattn, dssm_out, d_attn_g, d_ssm_g), (r_out_b,) = _group_norm_bwd(
        "group_norm_bwd", attn, ssm_out, dmixed, attn_out_g, ssm_out_g, plan=_Exchange([dw_out], (6,)))

    dz, dyg_direct, d_b_glu = _glu_bwd("glu_bwd", dssm_out, y_ssm, z_glu)
    dw_glu = _matmul("glu_dw", y_ssm, dz, "tn", [BF16], a_pro=gelu_bf16, tm=512).reshape(N_DEV, -1, d_ssm)
    dy_ssm = _matmul("glu_dx", dz, w_glu_f, "nt", [F32], extras=[dyg_direct, y_ssm],
                     epilogue=lambda acc, e, v: ((acc + e[0]) * _gelu_grad(e[1]),))
    (du, d_tm, d_em, d_fm, r1, r2), (r_ff1_a, r_glu) = _ssm_bwd(
        "ssm_bwd", u_chunks, dy_ssm, x_prev, tm_op, em_op, fm_op, acat, bcat,
        plan=_Plans([_ChipExchange([s_ff1], (CHIP_X, CHIP_Y)), _Exchange([dw_glu], RELATIONS_ALL)]))
    d_alr = r1[:, :p_state] + r1[:, p_state:]
    d_ali = r2[:, p_state:] - r2[:, :p_state]
    d_ssm_params = ssm_vjp((d_tm, d_em, d_fm, d_alr, d_ali))

    (dq, dk, dv, d_sinks), (r_ff1_b,) = _attn_bwd("attn_bwd", q, k, v, cos_t, sin_t, sinks, attn, lse, dattn,
                                                  plan=_ChipExchange([s_ff1], (CHIP_DIAGONAL,)))
    dproj = jnp.concatenate([dq, dk, dv, du], axis=1).astype(BF16)
    dw_in_t, (r_out_c,) = _matmul("in_dw", dproj, h1, "tn", [BF16], plan=_Exchange([dw_out], (5, 3, 7)))
    dw_in_t = dw_in_t.reshape(N_DEV, -1, d)
    dh1, (r_in_a,) = _matmul("in_dx", dproj, w_in_t, "nn", [F32], plan=_Exchange([dw_in_t], RELATIONS_SAME_CORE))
    (grad_x, d_scale1, d_shift1, d_norm1_g), _ = _norm_mod_bwd("norm1_bwd", xs, dh1, dx2, norm1_g, scale1)

    d_mod = jnp.concatenate([d_shift1, d_scale1, d_gate1, d_shift2, d_scale2, d_gate2])
    small_g = dict(zip(("ssm_lam_re", "ssm_lam_im", "ssm_log_step", "ssm_b_re", "ssm_b_im", "ssm_c_re", "ssm_c_im",
                        "ssm_d"), d_ssm_params, strict=True))
    small_g.update(b_ada=d_mod, norm1_g=d_norm1_g, sinks=d_sinks, b_glu=d_b_glu, attn_out_g=d_attn_g,
                   ssm_out_g=d_ssm_g, norm2_g=d_norm2_g, final_g=d_final_g)
    small_parts, r_in_b = _run_plan("gather_small_grads", _Plans([_Gather([_pack([small_g[n] for n in _SMALL])]),
                                                                  _Exchange([dw_in_t], RELATIONS_OTHER_CORE)]))
    small = _adam_shard("adam_small", [small_parts], _pack([weights[n] for n in _SMALL]),
                        _pack([mom[n] for n in _SMALL]), _pack([var[n] for n in _SMALL]))
    out = {}
    off = 0
    for n in _SMALL:
        size = weights[n].size
        out[n] = [t.reshape(-1)[off:off + size].reshape(weights[n].shape) for t in small]
        off += size

    dmod_all = small_parts.reshape(N_DEV, -1)[:, :N_MOD * d]
    dmod_loc = lax.dynamic_slice_in_dim(dmod_all, me * n_loc, n_loc, axis=1)
    c_act_t = silu(c_all).T
    out["w_ada"] = [t[None] for t in _ada_update("adam_w_ada", c_act_t, dmod_loc, w_ada[0], m_w_ada[0], v_w_ada[0])]

    mine = lambda blocks: lax.dynamic_index_in_dim(blocks, me, axis=0, keepdims=False)
    in_parts = [mine(dw_in_t).T] + [r.transpose(0, 2, 1) for r in (r_in_a, r_in_b)]
    my_chip = 2 * lax.axis_index("x") + lax.axis_index("y")
    chip_sum = lambda sums: lax.dynamic_index_in_dim(sums, my_chip, axis=0, keepdims=False)
    received = dict(w_in=in_parts, w_glu=[mine(dw_glu), r_glu], w_out=[mine(dw_out), r_out_a, r_out_b, r_out_c],
                    w_ff1=[chip_sum(s_ff1), r_ff1_a, r_ff1_b], w_ff2=[chip_sum(s_ff2), r_ff2_a, r_ff2_b])
    for n, parts in received.items():
        out[n] = [t[None] for t in _adam_shard("adam_" + n, parts, weights[n][0], mom[n][0], var[n][0])]

    return (loss, grad_x[None], *[out[n][0] for n in _WEIGHTS], *[out[n][1] for n in _WEIGHTS],
            *[out[n][2] for n in _WEIGHTS], *[out[n][3] for n in _WEIGHTS])
```
